```python
import jax, jax.numpy as jnp
from jax import lax
import numpy as np

D_MODEL = 2048
BATCH = 8
SEQ = 4096
DEPTH = 1

MEM_TOKENS = 256
HEAD_DIM = 64
N_Q_HEADS = 16
N_KV_HEADS = 4
Q_PER_KV = N_Q_HEADS // N_KV_HEADS
ATTN_WIDTH = N_Q_HEADS * HEAD_DIM
KV_WIDTH = N_KV_HEADS * HEAD_DIM
WINDOW = 128
BLOCK = 128
ROPE_THETA = 10000.0
CONV_WIDTH = 1024
CONV_K = 3
X_HEADS = 4
X_HEAD_DIM = 128
X_WIDTH = X_HEADS * X_HEAD_DIM
FFN_HIDDEN = -(-(8 * D_MODEL) // (3 * 256)) * 256
EPS = 1e-6
IN_SIZES = (ATTN_WIDTH, KV_WIDTH, KV_WIDTH, CONV_WIDTH, CONV_WIDTH, CONV_WIDTH, D_MODEL, D_MODEL)
IN_WIDTH = ATTN_WIDTH + 2 * KV_WIDTH + 3 * CONV_WIDTH + 2 * D_MODEL

kernel_name = "hybrid_gated_swa_shortconv_xattn_block"


def rms_norm(x, g):
    xf = x.astype(jnp.float32)
    y = xf * lax.rsqrt(jnp.mean(xf * xf, axis=-1, keepdims=True) + EPS)
    return (y * g.astype(jnp.float32)).astype(x.dtype)


def rope(x, positions):
    half = HEAD_DIM // 2
    inv_freq = ROPE_THETA ** (-jnp.arange(half, dtype=jnp.float32) / half)
    ang = positions.astype(jnp.float32)[:, None] * inv_freq[None, :]
    cos = jnp.cos(ang)[None, :, None, :]
    sin = jnp.sin(ang)[None, :, None, :]
    xf = x.astype(jnp.float32)
    x1, x2 = xf[..., :half], xf[..., half:]
    out = jnp.concatenate([x1 * cos - x2 * sin, x2 * cos + x1 * sin], axis=-1)
    return out.astype(x.dtype)


def _with_prev_block(t):
    pad = [(0, 0)] * t.ndim
    pad[1] = (1, 0)
    prev = jnp.pad(t, pad)[:, :-1]
    return jnp.concatenate([prev, t], axis=2)


def sliding_window_attention(q, k, v, sinks):
    b, t = q.shape[0], q.shape[1]
    nb = t // BLOCK
    qb = q.reshape(b, nb, BLOCK, N_KV_HEADS, Q_PER_KV, HEAD_DIM)
    kband = _with_prev_block(k.reshape(b, nb, BLOCK, N_KV_HEADS, HEAD_DIM))
    vband = _with_prev_block(v.reshape(b, nb, BLOCK, N_KV_HEADS, HEAD_DIM))
    scale = HEAD_DIM ** -0.5
    s = jnp.einsum('bnqhgd,bnkhd->bnhgqk', qb, kband).astype(jnp.float32) * scale
    blk = jnp.arange(nb)[:, None]
    q_pos = blk * BLOCK + jnp.arange(BLOCK)[None, :]
    k_pos = (blk - 1) * BLOCK + jnp.arange(2 * BLOCK)[None, :]
    diff = q_pos[:, :, None] - k_pos[:, None, :]
    valid = (diff >= 0) & (diff < WINDOW) & (k_pos[:, None, :] >= 0)
    s = jnp.where(valid[None, :, None, None, :, :], s, -jnp.inf)
    sink = sinks.astype(jnp.float32).reshape(N_KV_HEADS, Q_PER_KV)[None, None, :, :, None, None]
    m = jnp.maximum(jnp.max(s, axis=-1, keepdims=True), sink)
    p = jnp.exp(s - m)
    p = p / (jnp.sum(p, axis=-1, keepdims=True) + jnp.exp(sink - m))
    o = jnp.einsum('bnhgqk,bnkhd->bnqhgd', p.astype(v.dtype), vband)
    return o.reshape(b, t, ATTN_WIDTH)


def short_gated_conv(z, gate_b, gate_c, conv_w):
    t = z.shape[1]
    cz = gate_c * z
    zp = jnp.pad(cz, ((0, 0), (CONV_K - 1, 0), (0, 0)))
    y = conv_w[0] * zp[:, 0:t]
    for j in range(1, CONV_K):
        y = y + conv_w[j] * zp[:, j:j + t]
    return gate_b * y


def cross_attention(u, mem_n, w_xq, w_xkv, w_xo):
    b, t = u.shape[0], u.shape[1]
    q = (u @ w_xq).reshape(b, t, X_HEADS, X_HEAD_DIM)
    kv = mem_n @ w_xkv
    k = kv[..., :X_WIDTH].reshape(b, -1, X_HEADS, X_HEAD_DIM)
    v = kv[..., X_WIDTH:].reshape(b, -1, X_HEADS, X_HEAD_DIM)
    s = jnp.einsum('bthd,bmhd->bhtm', q, k).astype(jnp.float32) * (X_HEAD_DIM ** -0.5)
    p = jax.nn.softmax(s, axis=-1)
    o = jnp.einsum('bhtm,bmhd->bthd', p.astype(v.dtype), v).reshape(b, t, X_WIDTH)
    return o @ w_xo


def _fwd_setup_inputs(seed: int = 0) -> dict:
    key = jax.random.key(seed)
    ks = jax.random.split(key, 20)
    f32 = jnp.float32

    def w(k, shape, fan_in):
        return jax.random.normal(k, shape, f32) * (fan_in ** -0.5)

    def gain(k, shape):
        return 1.0 + 0.01 * jax.random.normal(k, shape, f32)

    return {
        "x": jax.random.normal(ks[0], (BATCH, SEQ, D_MODEL), f32),
        "mem": jax.random.normal(ks[1], (BATCH, MEM_TOKENS, D_MODEL), f32),
        "g_mix": gain(ks[2], (DEPTH, D_MODEL)),
        "w_in": w(ks[3], (DEPTH, D_MODEL, IN_WIDTH), D_MODEL),
        "conv_w": w(ks[4], (DEPTH, CONV_K, CONV_WIDTH), CONV_K),
        "attn_sinks": 0.5 * jax.random.normal(ks[5], (DEPTH, N_Q_HEADS), f32),
        "w_attn_proj": w(ks[6], (DEPTH, ATTN_WIDTH, D_MODEL), ATTN_WIDTH),
        "w_conv_proj": w(ks[7], (DEPTH, CONV_WIDTH, D_MODEL), CONV_WIDTH),
        "w_mix_out": w(ks[8], (DEPTH, D_MODEL, D_MODEL), D_MODEL),
        "g_xattn": gain(ks[9], (DEPTH, D_MODEL)),
        "g_mem": gain(ks[10], (DEPTH, D_MODEL)),
        "w_xq": w(ks[11], (DEPTH, D_MODEL, X_WIDTH), D_MODEL),
        "w_xkv": w(ks[12], (DEPTH, D_MODEL, 2 * X_WIDTH), D_MODEL),
        "w_xo": w(ks[13], (DEPTH, X_WIDTH, D_MODEL), X_WIDTH),
        "g_ffn": gain(ks[14], (DEPTH, D_MODEL)),
        "w_ffn_in": w(ks[15], (DEPTH, D_MODEL, 2 * FFN_HIDDEN), D_MODEL),
        "w_ffn_out": w(ks[16], (DEPTH, FFN_HIDDEN, D_MODEL), FFN_HIDDEN),
        "g_final": gain(ks[17], (D_MODEL,)),
    }


def _fwd_reference(x, mem, g_mix, w_in, conv_w, attn_sinks, w_attn_proj, w_conv_proj, w_mix_out,
              g_xattn, g_mem, w_xq, w_xkv, w_xo, g_ffn, w_ffn_in, w_ffn_out, g_final):
    b, t = x.shape[0], x.shape[1]
    positions = jnp.arange(t, dtype=jnp.int32)
    split_points = np.cumsum(IN_SIZES)[:-1].tolist()
    h = x
    for l in range(DEPTH):
        u = rms_norm(h, g_mix[l])
        proj = u @ w_in[l]
        q, k, v, z, gb, gc, gate_a, gate_c = jnp.split(proj, split_points, axis=-1)
        q = rope(q.reshape(b, t, N_Q_HEADS, HEAD_DIM), positions)
        k = rope(k.reshape(b, t, N_KV_HEADS, HEAD_DIM), positions)
        v = v.reshape(b, t, N_KV_HEADS, HEAD_DIM)
        y_attn = sliding_window_attention(q, k, v, attn_sinks[l]) @ w_attn_proj[l]
        y_conv = short_gated_conv(z, gb, gc, conv_w[l]) @ w_conv_proj[l]
        merged = jax.nn.sigmoid(gate_a) * y_attn + jax.nn.sigmoid(gate_c) * y_conv
        h = h + merged @ w_mix_out[l]
        u = rms_norm(h, g_xattn[l])
        mem_n = rms_norm(mem, g_mem[l])
        h = h + cross_attention(u, mem_n, w_xq[l], w_xkv[l], w_xo[l])
        u = rms_norm(h, g_ffn[l])
        hid = u @ w_ffn_in[l]
        h = h + (jax.nn.silu(hid[..., :FFN_HIDDEN]) * hid[..., FFN_HIDDEN:]) @ w_ffn_out[l]
    return rms_norm(h, g_final)


import jax as _jax
import jax.numpy as _jnp

TWIN_FORMAT = 'train_step'
FWD_PARAMS = ['x', 'mem', 'g_mix', 'w_in', 'conv_w', 'attn_sinks', 'w_attn_proj', 'w_conv_proj', 'w_mix_out', 'g_xattn', 'g_mem', 'w_xq', 'w_xkv', 'w_xo', 'g_ffn', 'w_ffn_in', 'w_ffn_out', 'g_final']
TWIN_WEIGHTS = ['g_mix', 'w_in', 'conv_w', 'attn_sinks', 'w_attn_proj', 'w_conv_proj', 'w_mix_out', 'g_xattn', 'g_mem', 'w_xq', 'w_xkv', 'w_xo', 'g_ffn', 'w_ffn_in', 'w_ffn_out', 'g_final']
TWIN_DIFF_INPUT = 'x'
TWIN_INPUTS = ['x', 'mem', 'g_mix', 'w_in', 'conv_w', 'attn_sinks', 'w_attn_proj', 'w_conv_proj', 'w_mix_out', 'g_xattn', 'g_mem', 'w_xq', 'w_xkv', 'w_xo', 'g_ffn', 'w_ffn_in', 'w_ffn_out', 'g_final', 'loss_target', 'm_g_mix', 'm_w_in', 'm_conv_w', 'm_attn_sinks', 'm_w_attn_proj', 'm_w_conv_proj', 'm_w_mix_out', 'm_g_xattn', 'm_g_mem', 'm_w_xq', 'm_w_xkv', 'm_w_xo', 'm_g_ffn', 'm_w_ffn_in', 'm_w_ffn_out', 'm_g_final', 'v_g_mix', 'v_w_in', 'v_conv_w', 'v_attn_sinks', 'v_w_attn_proj', 'v_w_conv_proj', 'v_w_mix_out', 'v_g_xattn', 'v_g_mem', 'v_w_xq', 'v_w_xkv', 'v_w_xo', 'v_g_ffn', 'v_w_ffn_in', 'v_w_ffn_out', 'v_g_final']
TWIN_OUTPUTS = ['loss', 'grad_x', 'grad_g_mix', 'grad_w_in', 'grad_conv_w', 'grad_attn_sinks', 'grad_w_attn_proj', 'grad_w_conv_proj', 'grad_w_mix_out', 'grad_g_xattn', 'grad_g_mem', 'grad_w_xq', 'grad_w_xkv', 'grad_w_xo', 'grad_g_ffn', 'grad_w_ffn_in', 'grad_w_ffn_out', 'grad_g_final', 'delta_g_mix', 'delta_w_in', 'delta_conv_w', 'delta_attn_sinks', 'delta_w_attn_proj', 'delta_w_conv_proj', 'delta_w_mix_out', 'delta_g_xattn', 'delta_g_mem', 'delta_w_xq', 'delta_w_xkv', 'delta_w_xo', 'delta_g_ffn', 'delta_w_ffn_in', 'delta_w_ffn_out', 'delta_g_final', 'new_m_g_mix', 'new_m_w_in', 'new_m_conv_w', 'new_m_attn_sinks', 'new_m_w_attn_proj', 'new_m_w_conv_proj', 'new_m_w_mix_out', 'new_m_g_xattn', 'new_m_g_mem', 'new_m_w_xq', 'new_m_w_xkv', 'new_m_w_xo', 'new_m_g_ffn', 'new_m_w_ffn_in', 'new_m_w_ffn_out', 'new_m_g_final', 'new_v_g_mix', 'new_v_w_in', 'new_v_conv_w', 'new_v_attn_sinks', 'new_v_w_attn_proj', 'new_v_w_conv_proj', 'new_v_w_mix_out', 'new_v_g_xattn', 'new_v_g_mem', 'new_v_w_xq', 'new_v_w_xkv', 'new_v_w_xo', 'new_v_g_ffn', 'new_v_w_ffn_in', 'new_v_w_ffn_out', 'new_v_g_final']
TWIN_LEAF_KINDS = {'loss': 'loss', 'grad_x': 'grad_x', 'grad_g_mix': 'grad_w', 'grad_w_in': 'grad_w', 'grad_conv_w': 'grad_w', 'grad_attn_sinks': 'grad_w', 'grad_w_attn_proj': 'grad_w', 'grad_w_conv_proj': 'grad_w', 'grad_w_mix_out': 'grad_w', 'grad_g_xattn': 'grad_w', 'grad_g_mem': 'grad_w', 'grad_w_xq': 'grad_w', 'grad_w_xkv': 'grad_w', 'grad_w_xo': 'grad_w', 'grad_g_ffn': 'grad_w', 'grad_w_ffn_in': 'grad_w', 'grad_w_ffn_out': 'grad_w', 'grad_g_final': 'grad_w', 'delta_g_mix': 'delta_w', 'delta_w_in': 'delta_w', 'delta_conv_w': 'delta_w', 'delta_attn_sinks': 'delta_w', 'delta_w_attn_proj': 'delta_w', 'delta_w_conv_proj': 'delta_w', 'delta_w_mix_out': 'delta_w', 'delta_g_xattn': 'delta_w', 'delta_g_mem': 'delta_w', 'delta_w_xq': 'delta_w', 'delta_w_xkv': 'delta_w', 'delta_w_xo': 'delta_w', 'delta_g_ffn': 'delta_w', 'delta_w_ffn_in': 'delta_w', 'delta_w_ffn_out': 'delta_w', 'delta_g_final': 'delta_w', 'new_m_g_mix': 'new_m', 'new_m_w_in': 'new_m', 'new_m_conv_w': 'new_m', 'new_m_attn_sinks': 'new_m', 'new_m_w_attn_proj': 'new_m', 'new_m_w_conv_proj': 'new_m', 'new_m_w_mix_out': 'new_m', 'new_m_g_xattn': 'new_m', 'new_m_g_mem': 'new_m', 'new_m_w_xq': 'new_m', 'new_m_w_xkv': 'new_m', 'new_m_w_xo': 'new_m', 'new_m_g_ffn': 'new_m', 'new_m_w_ffn_in': 'new_m', 'new_m_w_ffn_out': 'new_m', 'new_m_g_final': 'new_m', 'new_v_g_mix': 'new_v', 'new_v_w_in': 'new_v', 'new_v_conv_w': 'new_v', 'new_v_attn_sinks': 'new_v', 'new_v_w_attn_proj': 'new_v', 'new_v_w_conv_proj': 'new_v', 'new_v_w_mix_out': 'new_v', 'new_v_g_xattn': 'new_v', 'new_v_g_mem': 'new_v', 'new_v_w_xq': 'new_v', 'new_v_w_xkv': 'new_v', 'new_v_w_xo': 'new_v', 'new_v_g_ffn': 'new_v', 'new_v_w_ffn_in': 'new_v', 'new_v_w_ffn_out': 'new_v', 'new_v_g_final': 'new_v'}


def _forward(args):
    return _fwd_reference(*[args[k] for k in FWD_PARAMS])


def _output_shape():
    def fwd():
        inp = _fwd_setup_inputs(0)
        return _fwd_reference(*[inp[k] for k in FWD_PARAMS])
    out = _jax.eval_shape(fwd)
    return out.shape, out.dtype

N_MICROBATCH = 1
ADAM_LR = 0.001
ADAM_B1 = 0.9
ADAM_B2 = 0.999
ADAM_EPS = 1e-08
ADAM_WD = 0.01
ADAM_STEP = 10
PER_EXAMPLE_BATCH_AXIS = {'x': 0, 'mem': 0, 'loss_target': 0}
SHARED_INPUTS = []
_WEIGHT_DTYPES = {'g_mix': _jnp.float32, 'w_in': _jnp.float32, 'conv_w': _jnp.float32, 'attn_sinks': _jnp.float32, 'w_attn_proj': _jnp.float32, 'w_conv_proj': _jnp.float32, 'w_mix_out': _jnp.float32, 'g_xattn': _jnp.float32, 'g_mem': _jnp.float32, 'w_xq': _jnp.float32, 'w_xkv': _jnp.float32, 'w_xo': _jnp.float32, 'g_ffn': _jnp.float32, 'w_ffn_in': _jnp.float32, 'w_ffn_out': _jnp.float32, 'g_final': _jnp.float32}
MOMENT_SCALE = {'g_mix': 8.556976e-02, 'w_in': 4.078530e-02, 'conv_w': 6.664035e-02, 'attn_sinks': 1.384512e-02, 'w_attn_proj': 9.343166e-03, 'w_conv_proj': 4.658457e-02, 'w_mix_out': 4.732878e-02, 'g_xattn': 9.006894e-03, 'g_mem': 1.287129e-02, 'w_xq': 1.756741e-02, 'w_xkv': 1.781453e-02, 'w_xo': 8.888891e-03, 'g_ffn': 5.914176e-02, 'w_ffn_in': 2.507628e-02, 'w_ffn_out': 4.093456e-02, 'g_final': 1.599235e+01}


def _to_microbatches(a, axis):
    t = _jnp.moveaxis(a, axis, 0)
    t = t.reshape((N_MICROBATCH, t.shape[0] // N_MICROBATCH) + t.shape[1:])
    return _jnp.moveaxis(t, 1, axis + 1)


def setup_inputs(seed: int = 0) -> dict:
    inp = _fwd_setup_inputs(seed)
    key = _jax.random.fold_in(_jax.random.key(seed), 7919)
    shape, _ = _output_shape()
    out = dict(inp)
    out["loss_target"] = _jax.random.normal(_jax.random.fold_in(key, 0), shape, _jnp.float32)
    for i, name in enumerate(TWIN_WEIGHTS):
        w = inp[name].astype(_jnp.float32)
        if MOMENT_SCALE is None:
            s = _jnp.sqrt(_jnp.mean(_jnp.square(w)) + 1e-30)
        else:
            s = MOMENT_SCALE[name]
        km, kv = _jax.random.split(_jax.random.fold_in(key, i + 1))
        out[name] = w
        out["m_" + name] = s * _jax.random.normal(km, w.shape, _jnp.float32)
        out["v_" + name] = (s * s) * _jax.random.uniform(kv, w.shape, _jnp.float32, 0.5, 1.5)
    if N_MICROBATCH > 1:
        for name, axis in PER_EXAMPLE_BATCH_AXIS.items():
            out[name] = _to_microbatches(out[name], axis)
    return {'x': out['x'], 'mem': out['mem'], 'g_mix': out['g_mix'], 'w_in': out['w_in'], 'conv_w': out['conv_w'], 'attn_sinks': out['attn_sinks'], 'w_attn_proj': out['w_attn_proj'], 'w_conv_proj': out['w_conv_proj'], 'w_mix_out': out['w_mix_out'], 'g_xattn': out['g_xattn'], 'g_mem': out['g_mem'], 'w_xq': out['w_xq'], 'w_xkv': out['w_xkv'], 'w_xo': out['w_xo'], 'g_ffn': out['g_ffn'], 'w_ffn_in': out['w_ffn_in'], 'w_ffn_out': out['w_ffn_out'], 'g_final': out['g_final'], 'loss_target': out['loss_target'], 'm_g_mix': out['m_g_mix'], 'm_w_in': out['m_w_in'], 'm_conv_w': out['m_conv_w'], 'm_attn_sinks': out['m_attn_sinks'], 'm_w_attn_proj': out['m_w_attn_proj'], 'm_w_conv_proj': out['m_w_conv_proj'], 'm_w_mix_out': out['m_w_mix_out'], 'm_g_xattn': out['m_g_xattn'], 'm_g_mem': out['m_g_mem'], 'm_w_xq': out['m_w_xq'], 'm_w_xkv': out['m_w_xkv'], 'm_w_xo': out['m_w_xo'], 'm_g_ffn': out['m_g_ffn'], 'm_w_ffn_in': out['m_w_ffn_in'], 'm_w_ffn_out': out['m_w_ffn_out'], 'm_g_final': out['m_g_final'], 'v_g_mix': out['v_g_mix'], 'v_w_in': out['v_w_in'], 'v_conv_w': out['v_conv_w'], 'v_attn_sinks': out['v_attn_sinks'], 'v_w_attn_proj': out['v_w_attn_proj'], 'v_w_conv_proj': out['v_w_conv_proj'], 'v_w_mix_out': out['v_w_mix_out'], 'v_g_xattn': out['v_g_xattn'], 'v_g_mem': out['v_g_mem'], 'v_w_xq': out['v_w_xq'], 'v_w_xkv': out['v_w_xkv'], 'v_w_xo': out['v_w_xo'], 'v_g_ffn': out['v_g_ffn'], 'v_w_ffn_in': out['v_w_ffn_in'], 'v_w_ffn_out': out['v_w_ffn_out'], 'v_g_final': out['v_g_final']}


def _loss(weights, diff, rest, loss_target):
    with _jax.named_scope("forward"):
        args = {**rest, TWIN_DIFF_INPUT: diff, **{k: w.astype(_WEIGHT_DTYPES[k]) for k, w in weights.items()}}
        y = _forward(args)
    with _jax.named_scope("loss_head"):
        err = _jnp.square(y.astype(_jnp.float32) - loss_target)
        return 0.5 * _jnp.sum(_jnp.mean(err, axis=-1)) if err.ndim else 0.5 * err


def _adamw(w, g, m, v):
    m = ADAM_B1 * m + (1.0 - ADAM_B1) * g
    v = ADAM_B2 * v + (1.0 - ADAM_B2) * _jnp.square(g)
    m_hat = m / (1.0 - ADAM_B1 ** ADAM_STEP)
    v_hat = v / (1.0 - ADAM_B2 ** ADAM_STEP)
    delta = -ADAM_LR * (m_hat / (_jnp.sqrt(v_hat) + ADAM_EPS) + ADAM_WD * w)
    return delta, m, v


def reference(x, mem, g_mix, w_in, conv_w, attn_sinks, w_attn_proj, w_conv_proj, w_mix_out, g_xattn, g_mem, w_xq, w_xkv, w_xo, g_ffn, w_ffn_in, w_ffn_out, g_final, loss_target, m_g_mix, m_w_in, m_conv_w, m_attn_sinks, m_w_attn_proj, m_w_conv_proj, m_w_mix_out, m_g_xattn, m_g_mem, m_w_xq, m_w_xkv, m_w_xo, m_g_ffn, m_w_ffn_in, m_w_ffn_out, m_g_final, v_g_mix, v_w_in, v_conv_w, v_attn_sinks, v_w_attn_proj, v_w_conv_proj, v_w_mix_out, v_g_xattn, v_g_mem, v_w_xq, v_w_xkv, v_w_xo, v_g_ffn, v_w_ffn_in, v_w_ffn_out, v_g_final):
    given = dict(x=x, mem=mem, g_mix=g_mix, w_in=w_in, conv_w=conv_w, attn_sinks=attn_sinks, w_attn_proj=w_attn_proj, w_conv_proj=w_conv_proj, w_mix_out=w_mix_out, g_xattn=g_xattn, g_mem=g_mem, w_xq=w_xq, w_xkv=w_xkv, w_xo=w_xo, g_ffn=g_ffn, w_ffn_in=w_ffn_in, w_ffn_out=w_ffn_out, g_final=g_final, loss_target=loss_target, m_g_mix=m_g_mix, m_w_in=m_w_in, m_conv_w=m_conv_w, m_attn_sinks=m_attn_sinks, m_w_attn_proj=m_w_attn_proj, m_w_conv_proj=m_w_conv_proj, m_w_mix_out=m_w_mix_out, m_g_xattn=m_g_xattn, m_g_mem=m_g_mem, m_w_xq=m_w_xq, m_w_xkv=m_w_xkv, m_w_xo=m_w_xo, m_g_ffn=m_g_ffn, m_w_ffn_in=m_w_ffn_in, m_w_ffn_out=m_w_ffn_out, m_g_final=m_g_final, v_g_mix=v_g_mix, v_w_in=v_w_in, v_conv_w=v_conv_w, v_attn_sinks=v_attn_sinks, v_w_attn_proj=v_w_attn_proj, v_w_conv_proj=v_w_conv_proj, v_w_mix_out=v_w_mix_out, v_g_xattn=v_g_xattn, v_g_mem=v_g_mem, v_w_xq=v_w_xq, v_w_xkv=v_w_xkv, v_w_xo=v_w_xo, v_g_ffn=v_g_ffn, v_w_ffn_in=v_w_ffn_in, v_w_ffn_out=v_w_ffn_out, v_g_final=v_g_final)
    weights = {n: given[n] for n in TWIN_WEIGHTS}
    shared = {n: given[n] for n in SHARED_INPUTS}
    per_example = {n: given[n] for n in ['x', 'mem']}
    grad_fn = _jax.value_and_grad(_loss, argnums=(0, 1))

    def one_microbatch(ex, loss_target):
        ex = dict(ex)
        diff = ex.pop(TWIN_DIFF_INPUT)
        return grad_fn(weights, diff, {**shared, **ex}, loss_target)

    if N_MICROBATCH == 1:
        loss, (grad_w, grad_x) = one_microbatch(per_example, given["loss_target"])
    else:
        def body(carry, xs):
            loss_sum, grad_sum = carry
            l_k, (gw_k, gx_k) = one_microbatch(xs[0], xs[1])
            with _jax.named_scope("update"):
                return (loss_sum + l_k, _jax.tree.map(_jnp.add, grad_sum, gw_k)), gx_k

        init = (_jnp.zeros((), _jnp.float32), _jax.tree.map(_jnp.zeros_like, weights))
        (loss, grad_w), grad_x = _jax.lax.scan(body, init, (per_example, given["loss_target"]))
    with _jax.named_scope("update"):
        delta_w, new_m, new_v = {}, {}, {}
        for n in TWIN_WEIGHTS:
            delta_w[n], new_m[n], new_v[n] = _adamw(weights[n], grad_w[n], given["m_" + n], given["v_" + n])
    return (loss, grad_x, *[grad_w[n] for n in TWIN_WEIGHTS], *[delta_w[n] for n in TWIN_WEIGHTS],
            *[new_m[n] for n in TWIN_WEIGHTS], *[new_v[n] for n in TWIN_WEIGHTS])
```

```python
import functools

import jax
import jax.numpy as jnp
from jax import lax
from jax.experimental import pallas as pl
from jax.experimental.pallas import tpu as pltpu

F32 = jnp.float32
BF16 = jnp.bfloat16

VMEM_LIMIT_BYTES = 56 * 1024 * 1024
LANES = 128
HEAD_DIM = 64
BLOCK = 128
X_HEAD_DIM = 128
ROPE_THETA = 10000.0
EPS = 1e-6
NEG = -1e30
ADAM_LR, ADAM_B1, ADAM_B2, ADAM_EPS, ADAM_WD, ADAM_STEP = 0.001, 0.9, 0.999, 1e-08, 0.01, 10
N_CHIPS = 4
MESH = pl.DeviceIdType.MESH
ANY = pl.BlockSpec(memory_space=pl.ANY)


def _pick(dim, prefs):
    for p in prefs:
        if dim % p == 0:
            return p
    return dim


def _cparams(*sem):
    return pltpu.CompilerParams(dimension_semantics=sem, vmem_limit_bytes=VMEM_LIMIT_BYTES)


def _sds(shape, dtype):
    return jax.ShapeDtypeStruct(shape, dtype)


def _sigmoid(v):
    return 1.0 / (1.0 + jnp.exp(-v))


def _matmul(a, b, *, mode, out_dtype, name, res=None, tm=None, tn=None, tk=None):
    if mode == "nn":
        (m, k), (k2, n) = a.shape, b.shape
    elif mode == "nt":
        (m, k), (n, k2) = a.shape, b.shape
    else:
        (k, m), (k2, n) = a.shape, b.shape
    assert k == k2, (a.shape, b.shape, mode)
    tm = tm or _pick(m, (1024, 512, 256, 128))
    tn = tn or _pick(n, (512, 256, 128))
    tk = tk or (k if k <= 2048 else _pick(k, (2048, 1408, 1024, 512, 256, 128)))
    nk = k // tk
    dims = {"nn": (((1,), (0,)), ((), ())), "nt": (((1,), (1,)), ((), ())), "tn": (((0,), (0,)), ((), ()))}[mode]
    has_res = res is not None

    def body(*refs):
        a_ref, b_ref = refs[0], refs[1]
        r_ref = refs[2] if has_res else None
        o_ref = refs[2 + has_res]
        part = lax.dot_general(a_ref[...], b_ref[...], dims, preferred_element_type=F32)

        def finish(val):
            if has_res:
                val = val + r_ref[...]
            o_ref[...] = val.astype(o_ref.dtype)

        if nk == 1:
            finish(part)
        else:
            acc = refs[3 + has_res]
            kk = pl.program_id(2)

            @pl.when(kk == 0)
            def _():
                acc[...] = part

            @pl.when(kk > 0)
            def _():
                acc[...] += part

            @pl.when(kk == nk - 1)
            def _():
                finish(acc[...])

    if mode == "tn":
        a_spec = pl.BlockSpec((tk, tm), lambda i, j, kk: (kk, i))
    else:
        a_spec = pl.BlockSpec((tm, tk), lambda i, j, kk: (i, kk))
    if mode == "nt":
        b_spec = pl.BlockSpec((tn, tk), lambda i, j, kk: (j, kk))
    else:
        b_spec = pl.BlockSpec((tk, tn), lambda i, j, kk: (kk, j))
    o_spec = pl.BlockSpec((tm, tn), lambda i, j, kk: (i, j))
    return pl.pallas_call(
        body,
        name=name,
        grid=(m // tm, n // tn, nk),
        in_specs=[a_spec, b_spec] + ([o_spec] if has_res else []),
        out_specs=o_spec,
        out_shape=_sds((m, n), out_dtype),
        scratch_shapes=[pltpu.VMEM((tm, tn), F32)] if nk > 1 else [],
        compiler_params=_cparams("parallel", "parallel", "arbitrary"),
    )(*([a, b] + ([res] if has_res else [])))


def _rms_fwd(x, g, name):
    t, d = x.shape
    tm = _pick(t, (512, 256, 128))

    def body(x_ref, g_ref, o_ref):
        xf = x_ref[...]
        r = lax.rsqrt(jnp.mean(xf * xf, axis=-1, keepdims=True) + EPS)
        o_ref[...] = (xf * r * g_ref[...]).astype(o_ref.dtype)

    row = pl.BlockSpec((tm, d), lambda i: (i, 0))
    return pl.pallas_call(
        body, name=name, grid=(t // tm,),
        in_specs=[row, pl.BlockSpec((1, d), lambda i: (0, 0))],
        out_specs=row, out_shape=_sds((t, d), BF16),
        compiler_params=_cparams("parallel"),
    )(x, g)


def _rms_bwd_math(xf, g, du):
    r = lax.rsqrt(jnp.mean(xf * xf, axis=-1, keepdims=True) + EPS)
    xh = xf * r
    gdy = g * du
    dx = r * (gdy - xh * jnp.mean(gdy * xh, axis=-1, keepdims=True))
    dg = jnp.sum(du * xh, axis=0, keepdims=True)
    return dx, dg


def _rms_bwd(x, g, du, dh, name):
    t, d = x.shape
    tm = _pick(t, (256, 128))
    has_dh = dh is not None

    def body(*refs):
        x_ref, g_ref, du_ref = refs[0], refs[1], refs[2]
        o_ref, ob_ref, dg_ref = refs[3 + has_dh:]
        dx, dg = _rms_bwd_math(x_ref[...], g_ref[...], du_ref[...].astype(F32))
        if has_dh:
            dx = dx + refs[3][...]
        o_ref[...] = dx
        ob_ref[...] = dx.astype(BF16)

        @pl.when(pl.program_id(0) == 0)
        def _():
            dg_ref[...] = dg

        @pl.when(pl.program_id(0) > 0)
        def _():
            dg_ref[...] += dg

    row = pl.BlockSpec((tm, d), lambda i: (i, 0))
    vec = pl.BlockSpec((1, d), lambda i: (0, 0))
    return pl.pallas_call(
        body, name=name, grid=(t // tm,),
        in_specs=[row, vec, row] + ([row] if has_dh else []),
        out_specs=[row, row, vec],
        out_shape=[_sds((t, d), F32), _sds((t, d), BF16), _sds((1, d), F32)],
        compiler_params=_cparams("arbitrary"),
    )(*([x, g, du] + ([dh] if has_dh else [])))


def _loss_head(h, g, tgt):
    t, d = h.shape
    tm = _pick(t, (256, 128))

    def body(h_ref, g_ref, t_ref, o_ref, ob_ref, dg_ref, l_ref):
        xf = h_ref[...]
        gv = g_ref[...]
        r = lax.rsqrt(jnp.mean(xf * xf, axis=-1, keepdims=True) + EPS)
        err = xf * r * gv - t_ref[...]
        part = 0.5 * jnp.sum(jnp.mean(err * err, axis=-1, keepdims=True), axis=0, keepdims=True)
        dx, dg = _rms_bwd_math(xf, gv, err * (1.0 / d))
        o_ref[...] = dx
        ob_ref[...] = dx.astype(BF16)
        lrow = jnp.broadcast_to(part, (1, LANES))

        @pl.when(pl.program_id(0) == 0)
        def _():
            dg_ref[...] = dg
            l_ref[...] = lrow

        @pl.when(pl.program_id(0) > 0)
        def _():
            dg_ref[...] += dg
            l_ref[...] += lrow

    row = pl.BlockSpec((tm, d), lambda i: (i, 0))
    vec = pl.BlockSpec((1, d), lambda i: (0, 0))
    return pl.pallas_call(
        body, name="loss_head", grid=(t // tm,),
        in_specs=[row, vec, row],
        out_specs=[row, row, vec, pl.BlockSpec((1, LANES), lambda i: (0, 0))],
        out_shape=[_sds((t, d), F32), _sds((t, d), BF16), _sds((1, d), F32), _sds((1, LANES), F32)],
        compiler_params=_cparams("arbitrary"),
    )(h, g, tgt)


def _rope_tables(t):
    half = HEAD_DIM // 2
    inv_freq = ROPE_THETA ** (-jnp.arange(half, dtype=F32) / half)
    ang = jnp.arange(t, dtype=F32)[:, None] * inv_freq[None, :]
    cos = jnp.cos(ang)
    sin = jnp.sin(ang)
    reps = LANES // HEAD_DIM
    cos_t = jnp.tile(jnp.concatenate([cos, cos], axis=1), (1, reps))
    sin_t = jnp.tile(jnp.concatenate([-sin, sin], axis=1), (1, reps))
    return cos_t, sin_t


def _rope(v, cos, sin):
    w = v.shape[1]
    c = jnp.tile(cos, (1, w // LANES))
    s = jnp.tile(sin, (1, w // LANES))
    lane = lax.broadcasted_iota(jnp.int32, v.shape, 1)
    first = (lane % HEAD_DIM) < (HEAD_DIM // 2)
    partner = jnp.where(first, pltpu.roll(v, w - HEAD_DIM // 2, 1), pltpu.roll(v, HEAD_DIM // 2, 1))
    return v * c + partner * s


def _heads(v, first, count):
    return jnp.concatenate([v[:, (first + i) * HEAD_DIM:(first + i + 1) * HEAD_DIM] for i in range(count)], axis=0)


def _attn_probs(qs, kb, n, h, qpk, sinks_ref):
    s = lax.dot_general(qs, kb, (((1,), (1,)), ((), ())), preferred_element_type=F32) * (HEAD_DIM ** -0.5)
    qi = lax.broadcasted_iota(jnp.int32, (BLOCK, 2 * BLOCK), 0)
    kc = lax.broadcasted_iota(jnp.int32, (BLOCK, 2 * BLOCK), 1)
    valid = (kc > qi) & (kc <= qi + BLOCK) & ((kc >= BLOCK) | (n > 0))
    bias = jnp.tile(jnp.where(valid, 0.0, NEG).astype(F32), (qpk, 1))
    s = s + bias
    rowg = lax.broadcasted_iota(jnp.int32, (qpk * BLOCK, 1), 0) // BLOCK
    sink = jnp.zeros((qpk * BLOCK, 1), F32)
    for g in range(qpk):
        sink = jnp.where(rowg == g, sinks_ref[0, h * qpk + g], sink)
    m = jnp.maximum(jnp.max(s, axis=-1, keepdims=True), sink)
    e = jnp.exp(s - m)
    es = jnp.exp(sink - m)
    inv = 1.0 / (jnp.sum(e, axis=-1, keepdims=True) + es)
    return e * inv, es * inv, rowg


def _attn_specs(aw, kvw):
    koff = aw // kvw
    prev = lambda n: jnp.maximum(n - 1, 0)
    return [
        pl.BlockSpec((BLOCK, aw), lambda n: (n, 0)),
        pl.BlockSpec((BLOCK, kvw), lambda n: (n, koff)),
        pl.BlockSpec((BLOCK, kvw), lambda n: (prev(n), koff)),
        pl.BlockSpec((BLOCK, kvw), lambda n: (n, koff + 1)),
        pl.BlockSpec((BLOCK, kvw), lambda n: (prev(n), koff + 1)),
        pl.BlockSpec((BLOCK, LANES), lambda n: (n, 0)),
        pl.BlockSpec((BLOCK, LANES), lambda n: (n, 0)),
        pl.BlockSpec((BLOCK, LANES), lambda n: (prev(n), 0)),
        pl.BlockSpec((BLOCK, LANES), lambda n: (prev(n), 0)),
        pl.BlockSpec(memory_space=pltpu.SMEM),
    ]


def _attn_fwd(proj, cos, sin, sinks, aw, kvw):
    t = proj.shape[0]
    nkv = kvw // HEAD_DIM
    qpk = aw // kvw

    def body(q_ref, kc_ref, kp_ref, vc_ref, vp_ref, cc_ref, sc_ref, cp_ref, sp_ref, sinks_ref, o_ref):
        n = pl.program_id(0)
        q = _rope(q_ref[...], cc_ref[...], sc_ref[...]).astype(BF16)
        kc = _rope(kc_ref[...], cc_ref[...], sc_ref[...]).astype(BF16)
        kp = _rope(kp_ref[...], cp_ref[...], sp_ref[...]).astype(BF16)
        vc = vc_ref[...].astype(BF16)
        vp = vp_ref[...].astype(BF16)
        outs = []
        for h in range(nkv):
            hs = slice(h * HEAD_DIM, (h + 1) * HEAD_DIM)
            kb = jnp.concatenate([kp[:, hs], kc[:, hs]], axis=0)
            vb = jnp.concatenate([vp[:, hs], vc[:, hs]], axis=0)
            p, _, _ = _attn_probs(_heads(q, h * qpk, qpk), kb, n, h, qpk, sinks_ref)
            o = jnp.dot(p.astype(BF16), vb, preferred_element_type=F32)
            outs += [o[g * BLOCK:(g + 1) * BLOCK] for g in range(qpk)]
        o_ref[...] = jnp.concatenate(outs, axis=1).astype(o_ref.dtype)

    return pl.pallas_call(
        body, name="attn_fwd", grid=(t // BLOCK,),
        in_specs=_attn_specs(aw, kvw),
        out_specs=pl.BlockSpec((BLOCK, aw), lambda n: (n, 0)),
        out_shape=_sds((t, aw), BF16),
        compiler_params=_cparams("parallel"),
    )(proj, proj, proj, proj, proj, cos, sin, cos, sin, sinks)


def _attn_bwd(proj, cos, sin, sinks, do, aw, kvw):
    t = proj.shape[0]
    nkv = kvw // HEAD_DIM
    qpk = aw // kvw
    scale = HEAD_DIM ** -0.5

    def body(q_ref, kc_ref, kp_ref, vc_ref, vp_ref, cc_ref, sc_ref, cp_ref, sp_ref, sinks_ref, do_ref,
             dq_ref, dkc_ref, dkp_ref, dvc_ref, dvp_ref, ds_ref):
        n = pl.program_id(0)
        cc, sc, cp, sp = cc_ref[...], sc_ref[...], cp_ref[...], sp_ref[...]
        q = _rope(q_ref[...], cc, sc).astype(BF16)
        kc = _rope(kc_ref[...], cc, sc).astype(BF16)
        kp = _rope(kp_ref[...], cp, sp).astype(BF16)
        vc = vc_ref[...].astype(BF16)
        vp = vp_ref[...].astype(BF16)
        dout = do_ref[...]
        dqs, dks, dvs = [], [], []
        lane = lax.broadcasted_iota(jnp.int32, (8, LANES), 1)
        row0 = lax.broadcasted_iota(jnp.int32, (8, LANES), 0) == 0
        dsink = jnp.zeros((8, LANES), F32)
        for h in range(nkv):
            hs = slice(h * HEAD_DIM, (h + 1) * HEAD_DIM)
            kb = jnp.concatenate([kp[:, hs], kc[:, hs]], axis=0)
            vb = jnp.concatenate([vp[:, hs], vc[:, hs]], axis=0)
            qs = _heads(q, h * qpk, qpk)
            dos = _heads(dout, h * qpk, qpk)
            p, psink, rowg = _attn_probs(qs, kb, n, h, qpk, sinks_ref)
            pb = p.astype(BF16)
            o = jnp.dot(pb, vb, preferred_element_type=F32)
            delta = jnp.sum(dos.astype(F32) * o, axis=-1, keepdims=True)
            dvs.append(lax.dot_general(pb, dos, (((0,), (0,)), ((), ())), preferred_element_type=F32))
            dp = lax.dot_general(dos, vb, (((1,), (1,)), ((), ())), preferred_element_type=F32)
            dsc = (p * (dp - delta)).astype(BF16)
            dq_h = jnp.dot(dsc, kb, preferred_element_type=F32) * scale
            dqs += [dq_h[g * BLOCK:(g + 1) * BLOCK] for g in range(qpk)]
            dks.append(lax.dot_general(dsc, qs, (((0,), (0,)), ((), ())), preferred_element_type=F32) * scale)
            sink_term = psink * delta
            for g in range(qpk):
                val = -jnp.sum(jnp.where(rowg == g, sink_term, 0.0))
                dsink = jnp.where(row0 & (lane == h * qpk + g), val, dsink)
        dq = jnp.concatenate(dqs, axis=1)
        dq_ref[...] = _rope(dq, cc, -sc).astype(dq_ref.dtype)
        dk = jnp.concatenate(dks, axis=1)
        dv = jnp.concatenate(dvs, axis=1)
        dkp_ref[...] = _rope(dk[:BLOCK], cp, -sp)
        dkc_ref[...] = _rope(dk[BLOCK:], cc, -sc)
        dvp_ref[...] = dv[:BLOCK]
        dvc_ref[...] = dv[BLOCK:]

        @pl.when(n == 0)
        def _():
            ds_ref[...] = dsink

        @pl.when(n > 0)
        def _():
            ds_ref[...] += dsink

    kv_spec = pl.BlockSpec((BLOCK, kvw), lambda n: (n, 0))
    return pl.pallas_call(
        body, name="attn_bwd", grid=(t // BLOCK,),
        in_specs=_attn_specs(aw, kvw) + [pl.BlockSpec((BLOCK, aw), lambda n: (n, 0))],
        out_specs=[pl.BlockSpec((BLOCK, aw), lambda n: (n, 0)), kv_spec, kv_spec, kv_spec, kv_spec,
                   pl.BlockSpec((8, LANES), lambda n: (0, 0))],
        out_shape=[_sds((t, aw), BF16)] + [_sds((t, kvw), F32)] * 4 + [_sds((8, LANES), F32)],
        compiler_params=_cparams("arbitrary"),
    )(proj, proj, proj, proj, proj, cos, sin, cos, sin, sinks, do)


def _shift_down(v, k, halo):
    rows = lax.broadcasted_iota(jnp.int32, v.shape, 0)
    out = pltpu.roll(v, k, 0)
    for r in range(k):
        out = jnp.where(rows == r, halo[8 - k + r:8 - k + r + 1, :], out)
    return out


def _shift_up(v, k, halo):
    tm = v.shape[0]
    rows = lax.broadcasted_iota(jnp.int32, v.shape, 0)
    out = pltpu.roll(v, tm - k, 0)
    for r in range(k):
        out = jnp.where(rows == tm - k + r, halo[r:r + 1, :], out)
    return out


def _conv_fwd(proj, conv_w, zoff, cw, cb):
    t = proj.shape[0]
    tm = _pick(t, (512, 256, 128))
    zb, nb = zoff // cb, cw // cb
    h8 = tm // 8

    def body(z_ref, gb_ref, gc_ref, zp_ref, gcp_ref, w_ref, o_ref):
        i = pl.program_id(0)
        cz = gc_ref[...] * z_ref[...]
        czp = gcp_ref[...] * zp_ref[...] * (i > 0).astype(F32)
        w = w_ref[...]
        y = w[0:1] * _shift_down(cz, 2, czp) + w[1:2] * _shift_down(cz, 1, czp) + w[2:3] * cz
        o_ref[...] = (gb_ref[...] * y).astype(o_ref.dtype)

    def col(k):
        return pl.BlockSpec((tm, cb), lambda i, j: (i, zb + k * nb + j))

    def halo(k):
        return pl.BlockSpec((8, cb), lambda i, j: (jnp.maximum(i * h8 - 1, 0), zb + k * nb + j))

    return pl.pallas_call(
        body, name="conv_fwd", grid=(t // tm, nb),
        in_specs=[col(0), col(1), col(2), halo(0), halo(2), pl.BlockSpec((3, cb), lambda i, j: (0, j))],
        out_specs=pl.BlockSpec((tm, cb), lambda i, j: (i, j)),
        out_shape=_sds((t, cw), BF16),
        compiler_params=_cparams("parallel", "parallel"),
    )(proj, proj, proj, proj, proj, conv_w)


def _conv_bwd(proj, conv_w, dco, zoff, cw, cb):
    t = proj.shape[0]
    tm = _pick(t, (512, 256, 128))
    zb, nb = zoff // cb, cw // cb
    h8 = tm // 8
    nt = t // tm

    def body(z_ref, gb_ref, gc_ref, zp_ref, gcp_ref, gbn_ref, w_ref, d_ref, dn_ref, dz_ref, dgb_ref, dgc_ref, dw_ref):
        i = pl.program_id(1)
        z, gb, gc = z_ref[...], gb_ref[...], gc_ref[...]
        d = d_ref[...].astype(F32)
        cz = gc * z
        czp = gcp_ref[...] * zp_ref[...] * (i > 0).astype(F32)
        w = w_ref[...]
        cz1 = _shift_down(cz, 1, czp)
        cz2 = _shift_down(cz, 2, czp)
        y = w[0:1] * cz2 + w[1:2] * cz1 + w[2:3] * cz
        dgb_ref[...] = (d * y).astype(dgb_ref.dtype)
        dy = d * gb
        dyn = dn_ref[...].astype(F32) * gbn_ref[...] * (i < nt - 1).astype(F32)
        dcz = w[2:3] * dy + w[1:2] * _shift_up(dy, 1, dyn) + w[0:1] * _shift_up(dy, 2, dyn)
        dgc_ref[...] = (dcz * z).astype(dgc_ref.dtype)
        dz_ref[...] = (dcz * gc).astype(dz_ref.dtype)
        rows = lax.broadcasted_iota(jnp.int32, (8, cb), 0)
        dw = jnp.zeros((8, cb), F32)
        for r, tap in enumerate((cz2, cz1, cz)):
            dw = jnp.where(rows == r, jnp.sum(dy * tap, axis=0, keepdims=True), dw)

        @pl.when(i == 0)
        def _():
            dw_ref[...] = dw

        @pl.when(i > 0)
        def _():
            dw_ref[...] += dw

    def col(k):
        return pl.BlockSpec((tm, cb), lambda j, i: (i, zb + k * nb + j))

    def halo_prev(k):
        return pl.BlockSpec((8, cb), lambda j, i: (jnp.maximum(i * h8 - 1, 0), zb + k * nb + j))

    own = pl.BlockSpec((tm, cb), lambda j, i: (i, j))
    nxt = lambda i: jnp.minimum((i + 1) * h8, t // 8 - 1)
    return pl.pallas_call(
        body, name="conv_bwd", grid=(nb, nt),
        in_specs=[col(0), col(1), col(2), halo_prev(0), halo_prev(2),
                  pl.BlockSpec((8, cb), lambda j, i: (nxt(i), zb + nb + j)),
                  pl.BlockSpec((3, cb), lambda j, i: (0, j)), own,
                  pl.BlockSpec((8, cb), lambda j, i: (nxt(i), j))],
        out_specs=[own, own, own, pl.BlockSpec((8, cb), lambda j, i: (0, j))],
        out_shape=[_sds((t, cw), BF16)] * 3 + [_sds((8, cw), F32)],
        compiler_params=_cparams("parallel", "arbitrary"),
    )(proj, proj, proj, proj, proj, proj, conv_w, dco, dco)


def _merge_fwd(proj, ya, yc, goff, cb):
    t, d = ya.shape
    tm = _pick(t, (512, 256, 128))
    gb_, nb = goff // cb, d // cb

    def body(ga_ref, gc_ref, ya_ref, yc_ref, o_ref):
        o_ref[...] = (_sigmoid(ga_ref[...]) * ya_ref[...] + _sigmoid(gc_ref[...]) * yc_ref[...]).astype(o_ref.dtype)

    own = pl.BlockSpec((tm, cb), lambda i, j: (i, j))
    return pl.pallas_call(
        body, name="merge_fwd", grid=(t // tm, nb),
        in_specs=[pl.BlockSpec((tm, cb), lambda i, j: (i, gb_ + j)),
                  pl.BlockSpec((tm, cb), lambda i, j: (i, gb_ + nb + j)), own, own],
        out_specs=own, out_shape=_sds((t, d), BF16),
        compiler_params=_cparams("parallel", "parallel"),
    )(proj, proj, ya, yc)


def _merge_bwd(proj, ya, yc, dm, goff, cb):
    t, d = ya.shape
    tm = _pick(t, (512, 256, 128))
    gb_, nb = goff // cb, d // cb

    def body(ga_ref, gc_ref, ya_ref, yc_ref, dm_ref, dya_ref, dyc_ref, dga_ref, dgc_ref):
        dmv = dm_ref[...].astype(F32)
        sa = _sigmoid(ga_ref[...])
        sc = _sigmoid(gc_ref[...])
        dya_ref[...] = (dmv * sa).astype(BF16)
        dyc_ref[...] = (dmv * sc).astype(BF16)
        dga_ref[...] = (dmv * ya_ref[...] * sa * (1.0 - sa)).astype(BF16)
        dgc_ref[...] = (dmv * yc_ref[...] * sc * (1.0 - sc)).astype(BF16)

    own = pl.BlockSpec((tm, cb), lambda i, j: (i, j))
    return pl.pallas_call(
        body, name="merge_bwd", grid=(t // tm, nb),
        in_specs=[pl.BlockSpec((tm, cb), lambda i, j: (i, gb_ + j)),
                  pl.BlockSpec((tm, cb), lambda i, j: (i, gb_ + nb + j)), own, own, own],
        out_specs=[own] * 4, out_shape=[_sds((t, d), BF16)] * 4,
        compiler_params=_cparams("parallel", "parallel"),
    )(proj, proj, ya, yc, dm)


def _assemble_dproj(dq, dkc, dkp, dvc, dvp, dz, dgb, dgc, dga, dgg):
    t, aw = dq.shape
    kvw, cw, d = dkc.shape[1], dz.shape[1], dga.shape[1]
    nblk = t // BLOCK
    width = aw + 2 * kvw + 3 * cw + 2 * d

    def body(dq_ref, dkc_ref, dkp_ref, dvc_ref, dvp_ref, dz_ref, dgb_ref, dgc_ref, dga_ref, dgg_ref, o_ref):
        keep = (pl.program_id(0) < nblk - 1).astype(F32)
        dk = dkc_ref[...] + dkp_ref[...] * keep
        dv = dvc_ref[...] + dvp_ref[...] * keep
        o_ref[...] = jnp.concatenate(
            [dq_ref[...], dk.astype(BF16), dv.astype(BF16), dz_ref[...], dgb_ref[...], dgc_ref[...],
             dga_ref[...], dgg_ref[...]], axis=1)

    def cur(w):
        return pl.BlockSpec((BLOCK, w), lambda n: (n, 0))

    def nxt(w):
        return pl.BlockSpec((BLOCK, w), lambda n: (jnp.minimum(n + 1, nblk - 1), 0))

    return pl.pallas_call(
        body, name="assemble_dproj", grid=(nblk,),
        in_specs=[cur(aw), cur(kvw), nxt(kvw), cur(kvw), nxt(kvw), cur(cw), cur(cw), cur(cw), cur(d), cur(d)],
        out_specs=cur(width), out_shape=_sds((t, width), BF16),
        compiler_params=_cparams("parallel"),
    )(dq, dkc, dkp, dvc, dvp, dz, dgb, dgc, dga, dgg)


def _xattn_probs(qh, kh):
    s = lax.dot_general(qh, kh, (((1,), (1,)), ((), ())), preferred_element_type=F32) * (X_HEAD_DIM ** -0.5)
    e = jnp.exp(s - jnp.max(s, axis=-1, keepdims=True))
    return e * (1.0 / jnp.sum(e, axis=-1, keepdims=True))


def _xattn_fwd(xq, kv):
    t, xw = xq.shape
    mt = kv.shape[0]
    tm = _pick(t, (512, 256, 128))

    def body(q_ref, kv_ref, o_ref):
        outs = []
        for hd in range(xw // X_HEAD_DIM):
            hs = slice(hd * X_HEAD_DIM, (hd + 1) * X_HEAD_DIM)
            vs = slice(xw + hd * X_HEAD_DIM, xw + (hd + 1) * X_HEAD_DIM)
            p = _xattn_probs(q_ref[:, hs], kv_ref[:, hs])
            outs.append(jnp.dot(p.astype(BF16), kv_ref[:, vs], preferred_element_type=F32))
        o_ref[...] = jnp.concatenate(outs, axis=1).astype(o_ref.dtype)

    return pl.pallas_call(
        body, name="xattn_fwd", grid=(t // tm,),
        in_specs=[pl.BlockSpec((tm, xw), lambda i: (i, 0)), pl.BlockSpec((mt, 2 * xw), lambda i: (0, 0))],
        out_specs=pl.BlockSpec((tm, xw), lambda i: (i, 0)), out_shape=_sds((t, xw), BF16),
        compiler_params=_cparams("parallel"),
    )(xq, kv)


def _xattn_bwd(xq, kv, do):
    t, xw = xq.shape
    mt = kv.shape[0]
    tm = _pick(t, (512, 256, 128))
    scale = X_HEAD_DIM ** -0.5

    def body(q_ref, kv_ref, do_ref, dq_ref, dkv_ref):
        dqs, dks, dvs = [], [], []
        for hd in range(xw // X_HEAD_DIM):
            hs = slice(hd * X_HEAD_DIM, (hd + 1) * X_HEAD_DIM)
            vs = slice(xw + hd * X_HEAD_DIM, xw + (hd + 1) * X_HEAD_DIM)
            qh, kh, vh, doh = q_ref[:, hs], kv_ref[:, hs], kv_ref[:, vs], do_ref[:, hs]
            p = _xattn_probs(qh, kh)
            pb = p.astype(BF16)
            o = jnp.dot(pb, vh, preferred_element_type=F32)
            delta = jnp.sum(doh.astype(F32) * o, axis=-1, keepdims=True)
            dvs.append(lax.dot_general(pb, doh, (((0,), (0,)), ((), ())), preferred_element_type=F32))
            dp = lax.dot_general(doh, vh, (((1,), (1,)), ((), ())), preferred_element_type=F32)
            dsc = (p * (dp - delta)).astype(BF16)
            dqs.append(jnp.dot(dsc, kh, preferred_element_type=F32) * scale)
            dks.append(lax.dot_general(dsc, qh, (((0,), (0,)), ((), ())), preferred_element_type=F32) * scale)
        dq_ref[...] = jnp.concatenate(dqs, axis=1).astype(dq_ref.dtype)
        dkv = jnp.concatenate(dks + dvs, axis=1)

        @pl.when(pl.program_id(0) == 0)
        def _():
            dkv_ref[...] = dkv

        @pl.when(pl.program_id(0) > 0)
        def _():
            dkv_ref[...] += dkv

    row = pl.BlockSpec((tm, xw), lambda i: (i, 0))
    whole = pl.BlockSpec((mt, 2 * xw), lambda i: (0, 0))
    return pl.pallas_call(
        body, name="xattn_bwd", grid=(t // tm,),
        in_specs=[row, whole, row], out_specs=[row, whole],
        out_shape=[_sds((t, xw), BF16), _sds((mt, 2 * xw), F32)],
        compiler_params=_cparams("arbitrary"),
    )(xq, kv, do)


def _swiglu_fwd(hid):
    t, f2 = hid.shape
    f = f2 // 2
    tm = _pick(t, (256, 128))

    def body(h_ref, o_ref):
        a = h_ref[:, :f].astype(F32)
        b = h_ref[:, f:].astype(F32)
        o_ref[...] = (a * _sigmoid(a) * b).astype(o_ref.dtype)

    return pl.pallas_call(
        body, name="swiglu_fwd", grid=(t // tm,),
        in_specs=[pl.BlockSpec((tm, f2), lambda i: (i, 0))],
        out_specs=pl.BlockSpec((tm, f), lambda i: (i, 0)), out_shape=_sds((t, f), BF16),
        compiler_params=_cparams("parallel"),
    )(hid)


def _swiglu_bwd(hid, dact):
    t, f2 = hid.shape
    f = f2 // 2
    tm = _pick(t, (128,))

    def body(h_ref, d_ref, o_ref):
        a = h_ref[:, :f].astype(F32)
        b = h_ref[:, f:].astype(F32)
        d = d_ref[...].astype(F32)
        sg = _sigmoid(a)
        o_ref[:, :f] = (d * b * sg * (1.0 + a * (1.0 - sg))).astype(o_ref.dtype)
        o_ref[:, f:] = (d * a * sg).astype(o_ref.dtype)

    return pl.pallas_call(
        body, name="swiglu_bwd", grid=(t // tm,),
        in_specs=[pl.BlockSpec((tm, f2), lambda i: (i, 0)), pl.BlockSpec((tm, f), lambda i: (i, 0))],
        out_specs=pl.BlockSpec((tm, f2), lambda i: (i, 0)), out_shape=_sds((t, f2), BF16),
        compiler_params=_cparams("parallel"),
    )(hid, dact)


def _adamw(w, g, m, v, name):
    r, c = w.shape
    tr = _pick(r, (256, 128, 64, 32, 16, 8)) if r * c > 65536 else r

    def body(w_ref, g_ref, m_ref, v_ref, d_ref, nm_ref, nv_ref):
        gv = g_ref[...]
        m2 = ADAM_B1 * m_ref[...] + (1.0 - ADAM_B1) * gv
        v2 = ADAM_B2 * v_ref[...] + (1.0 - ADAM_B2) * (gv * gv)
        m_hat = m2 / (1.0 - ADAM_B1 ** ADAM_STEP)
        v_hat = v2 / (1.0 - ADAM_B2 ** ADAM_STEP)
        d_ref[...] = -ADAM_LR * (m_hat / (jnp.sqrt(v_hat) + ADAM_EPS) + ADAM_WD * w_ref[...])
        nm_ref[...] = m2
        nv_ref[...] = v2

    blk = pl.BlockSpec((tr, c), lambda i: (i, 0))
    return pl.pallas_call(
        body, name=name, grid=(r // tr,),
        in_specs=[blk] * 4, out_specs=[blk] * 3, out_shape=[_sds((r, c), F32)] * 3,
        compiler_params=_cparams("parallel"),
    )(w, g, m, v)


def _local_step(x, mem, tgt, g_mix, conv_w, sinks, g_xattn, g_mem, g_ffn, g_final,
                w_in, w_ap, w_cp, w_mo, w_xq, w_xkv, w_xo, w_fi, w_fo):
    t, d = x.shape
    aw, cw = w_ap.shape[0], w_cp.shape[0]
    kvw = (w_in.shape[1] - aw - 3 * cw - 2 * d) // 2
    cb = 2 * kvw
    zoff = aw + 2 * kvw
    goff = zoff + 3 * cw
    cos, sin = _rope_tables(t)
    mm = _matmul

    u1 = _rms_fwd(x, g_mix, "rms_mix")
    proj = mm(u1, w_in, mode="nn", out_dtype=F32, name="mm_proj")
    attn_o = _attn_fwd(proj, cos, sin, sinks, aw, kvw)
    conv_o = _conv_fwd(proj, conv_w, zoff, cw, cb)
    ya = mm(attn_o, w_ap, mode="nn", out_dtype=F32, name="mm_yattn")
    yc = mm(conv_o, w_cp, mode="nn", out_dtype=F32, name="mm_yconv")
    merged = _merge_fwd(proj, ya, yc, goff, cb)
    h1 = mm(merged, w_mo, mode="nn", out_dtype=F32, name="mm_mix", res=x)
    u2 = _rms_fwd(h1, g_xattn, "rms_xattn")
    mem_n = _rms_fwd(mem, g_mem, "rms_mem")
    xq = mm(u2, w_xq, mode="nn", out_dtype=BF16, name="mm_xq")
    kv = mm(mem_n, w_xkv, mode="nn", out_dtype=BF16, name="mm_xkv")
    xo = _xattn_fwd(xq, kv)
    h2 = mm(xo, w_xo, mode="nn", out_dtype=F32, name="mm_xo", res=h1)
    u3 = _rms_fwd(h2, g_ffn, "rms_ffn")
    hid = mm(u3, w_fi, mode="nn", out_dtype=BF16, name="mm_ffn_in")
    act = _swiglu_fwd(hid)
    h3 = mm(act, w_fo, mode="nn", out_dtype=F32, name="mm_ffn_out", res=h2)
    dh3, dh3b, dg_final, loss = _loss_head(h3, g_final, tgt)

    gw_fo = mm(act, dh3b, mode="tn", out_dtype=BF16, name="mm_dw_ffn_out")
    dact = mm(dh3b, w_fo, mode="nt", out_dtype=BF16, name="mm_dact")
    dhid = _swiglu_bwd(hid, dact)
    gw_fi = mm(u3, dhid, mode="tn", out_dtype=BF16, name="mm_dw_ffn_in")
    du3 = mm(dhid, w_fi, mode="nt", out_dtype=F32, name="mm_du3")
    dh2, dh2b, dg_ffn = _rms_bwd(h2, g_ffn, du3, dh3, "rms_bwd_ffn")
    gw_xo = mm(xo, dh2b, mode="tn", out_dtype=BF16, name="mm_dw_xo")
    dxo = mm(dh2b, w_xo, mode="nt", out_dtype=BF16, name="mm_dxo")
    dxq, dkv = _xattn_bwd(xq, kv, dxo)
    dkvb = dkv.astype(BF16)
    gw_xq = mm(u2, dxq, mode="tn", out_dtype=BF16, name="mm_dw_xq")
    du2 = mm(dxq, w_xq, mode="nt", out_dtype=F32, name="mm_du2")
    gw_xkv = mm(mem_n, dkvb, mode="tn", out_dtype=BF16, name="mm_dw_xkv")
    dmem_n = mm(dkvb, w_xkv, mode="nt", out_dtype=F32, name="mm_dmem")
    _, _, dg_mem = _rms_bwd(mem, g_mem, dmem_n, None, "rms_bwd_mem")
    dh1, dh1b, dg_xattn = _rms_bwd(h1, g_xattn, du2, dh2, "rms_bwd_xattn")
    gw_mo = mm(merged, dh1b, mode="tn", out_dtype=BF16, name="mm_dw_mix")
    dmerged = mm(dh1b, w_mo, mode="nt", out_dtype=BF16, name="mm_dmerged")
    dya, dyc, dga, dgg = _merge_bwd(proj, ya, yc, dmerged, goff, cb)
    gw_ap = mm(attn_o, dya, mode="tn", out_dtype=BF16, name="mm_dw_attn_proj")
    dattn_o = mm(dya, w_ap, mode="nt", out_dtype=BF16, name="mm_dattn")
    gw_cp = mm(conv_o, dyc, mode="tn", out_dtype=BF16, name="mm_dw_conv_proj")
    dconv_o = mm(dyc, w_cp, mode="nt", out_dtype=BF16, name="mm_dconv")
    dz, dgb, dgc, dconv_w = _conv_bwd(proj, conv_w, dconv_o, zoff, cw, cb)
    dq, dkc, dkp, dvc, dvp, dsinks = _attn_bwd(proj, cos, sin, sinks, dattn_o, aw, kvw)
    dproj = _assemble_dproj(dq, dkc, dkp, dvc, dvp, dz, dgb, dgc, dga, dgg)
    gw_in = mm(u1, dproj, mode="tn", out_dtype=BF16, name="mm_dw_in")
    du1 = mm(dproj, w_in, mode="nt", out_dtype=F32, name="mm_du1")
    grad_x, _, dg_mix = _rms_bwd(x, g_mix, du1, dh1, "rms_bwd_mix")

    big = dict(w_in=gw_in, w_attn_proj=gw_ap, w_conv_proj=gw_cp, w_mix_out=gw_mo, w_xq=gw_xq, w_xkv=gw_xkv,
               w_xo=gw_xo, w_ffn_in=gw_fi, w_ffn_out=gw_fo)
    small = dict(g_mix=dg_mix, g_xattn=dg_xattn, g_mem=dg_mem, g_ffn=dg_ffn, g_final=dg_final,
                 conv_w=dconv_w[:3], attn_sinks=dsinks[0:1, :sinks.shape[1]], loss=loss[0:1, 0:1])
    return grad_x, big, small


BIG = (("w_in", 1), ("w_attn_proj", 1), ("w_conv_proj", 1), ("w_mix_out", 0), ("w_xq", 0), ("w_xkv", 0),
       ("w_xo", 1), ("w_ffn_in", 1), ("w_ffn_out", 0))


def _place():
    x, y, c = lax.axis_index("x"), lax.axis_index("y"), lax.axis_index("c")
    chips = [(1 - x, y), (x, 1 - y), (1 - x, 1 - y)]
    return x, y, c, chips


def _window(ref, ax, shard_shape, s, h):
    sr, sc = shard_shape
    hr = sr // 2
    if ax == 1:
        return ref.at[pl.ds(pl.multiple_of(h * hr, 16), hr), pl.ds(pl.multiple_of(s * sc, LANES), sc)]
    return ref.at[pl.ds(pl.multiple_of(s * sr + h * hr, 16), hr), :]


def _half(ref, h):
    hr = ref.shape[0] // 2
    return ref.at[pl.ds(pl.multiple_of(h * hr, 16), hr), :]


def _remote(src, dst, send_sem, recv_sem, dev):
    return pltpu.make_async_remote_copy(src_ref=src, dst_ref=dst, send_sem=send_sem, recv_sem=recv_sem,
                                        device_id=dev, device_id_type=MESH)


def _all_gather(shards, axes, whole):
    n = len(shards)
    fulls = [_sds((s.shape[0] * (N_CHIPS if ax == 0 else 1), s.shape[1] * (N_CHIPS if ax == 1 else 1)), s.dtype)
             for s, ax in zip(shards, axes)]

    def body(*refs):
        sh, out = refs[:n], refs[n:2 * n]
        send_sems, recv_sems, local_sems = refs[2 * n:]
        x, y, c, chips = _place()
        me = 2 * x + y
        sib = (x, y, 1 - c)

        def block(i, s):
            sr, sc = sh[i].shape
            if axes[i] == 1:
                return out[i].at[:, pl.ds(pl.multiple_of(s * sc, LANES), sc)]
            return out[i].at[pl.ds(pl.multiple_of(s * sr, 8), sr), :]

        def win(i, s, h):
            return _window(out[i], axes[i], sh[i].shape, s, h)

        local = [pltpu.make_async_copy(sh[i], block(i, me), local_sems.at[i]) for i in range(n)]
        for cp in local:
            cp.start()
        sends = []
        for i in range(n):
            for j, chip in enumerate(chips):
                if i in whole:
                    cp = _remote(sh[i], block(i, me), send_sems.at[i, j], recv_sems.at[i, j], (*chip, c))
                else:
                    cp = _remote(_half(sh[i], c), win(i, me, c), send_sems.at[i, j], recv_sems.at[i, j], (*chip, c))
                cp.start()
                sends.append(cp)
        for i in range(n):
            for j, chip in enumerate(chips):
                cj = 2 * chip[0] + chip[1]
                if i in whole:
                    _remote(block(i, cj), block(i, cj), send_sems.at[i, j], recv_sems.at[i, j], (*chip, c)).wait_recv()
                    continue
                _remote(win(i, cj, c), win(i, cj, c), send_sems.at[i, j], recv_sems.at[i, j], (*chip, c)).wait_recv()
                fwd = _remote(win(i, cj, c), win(i, cj, c), send_sems.at[i, 3 + j], recv_sems.at[i, 3 + j], sib)
                fwd.start()
                sends.append(fwd)
        for i in range(n):
            if i in whole:
                continue
            for j, chip in enumerate(chips):
                cj = 2 * chip[0] + chip[1]
                _remote(win(i, cj, 1 - c), win(i, cj, 1 - c), send_sems.at[i, 3 + j], recv_sems.at[i, 3 + j],
                        sib).wait_recv()
        for cp in sends:
            cp.wait_send()
        for cp in local:
            cp.wait()

    return pl.pallas_call(
        body, name="all_gather_weights",
        in_specs=[ANY] * n, out_specs=[ANY] * n, out_shape=fulls,
        scratch_shapes=[pltpu.SemaphoreType.DMA((n, 6)), pltpu.SemaphoreType.DMA((n, 6)),
                        pltpu.SemaphoreType.DMA((n,))],
    )(*shards)


def _pair_exchange(grads, axes, shard_shapes):
    n = len(grads)

    def body(*refs):
        g, ra = refs[:n], refs[n:2 * n]
        send_sems, recv_sems = refs[2 * n:]
        x, y, c, _ = _place()
        sib = (x, y, 1 - c)
        sends = []

        def pieces(ref, i, h):
            if axes[i] == 1:
                return [_half(ref, h)]
            return [_window(ref, 0, shard_shapes[i], s, h) for s in range(N_CHIPS)]

        for i in range(n):
            for k, (src, dst) in enumerate(zip(pieces(g[i], i, 1 - c), pieces(ra[i], i, 1 - c))):
                cp = _remote(src, dst, send_sems.at[i, k], recv_sems.at[i, k], sib)
                cp.start()
                sends.append(cp)
        for i in range(n):
            for k, dst in enumerate(pieces(ra[i], i, c)):
                _remote(dst, dst, send_sems.at[i, k], recv_sems.at[i, k], sib).wait_recv()
        for cp in sends:
            cp.wait_send()

    return pl.pallas_call(
        body, name="grad_pair_exchange",
        in_specs=[ANY] * n, out_specs=[ANY] * n, out_shape=[_sds(g.shape, g.dtype) for g in grads],
        scratch_shapes=[pltpu.SemaphoreType.DMA((n, N_CHIPS)), pltpu.SemaphoreType.DMA((n, N_CHIPS))],
    )(*grads)


def _pair_add(g, ra, ax, shard_shape, c, name):
    sr, sc = shard_shape
    hr = sr // 2
    wc = sc
    tr = _pick(hr, (256, 352, 128, 64, 32, 16))
    nr = hr // tr

    def body(c_ref, a_ref, b_ref, o_ref):
        o_ref[...] = (a_ref[...].astype(F32) + b_ref[...].astype(F32)).astype(o_ref.dtype)

    if ax == 1:
        src = pl.BlockSpec((tr, wc), lambda s, r, c_ref: (c_ref[0] * nr + r, s))
    else:
        src = pl.BlockSpec((tr, wc), lambda s, r, c_ref: (s * 2 * nr + c_ref[0] * nr + r, 0))
    return pl.pallas_call(
        body, name=name,
        grid_spec=pltpu.PrefetchScalarGridSpec(
            num_scalar_prefetch=1, grid=(N_CHIPS, nr), in_specs=[src, src],
            out_specs=pl.BlockSpec((None, tr, wc), lambda s, r, c_ref: (s, r, 0))),
        out_shape=_sds((N_CHIPS, hr, wc), BF16),
        compiler_params=_cparams("parallel", "parallel"),
    )(c, g, ra)


def _chip_exchange(parts):
    n = len(parts)

    def body(*refs):
        p, rc = refs[:n], refs[n:2 * n]
        send_sems, recv_sems, local_sems = refs[2 * n:]
        x, y, c, chips = _place()
        me = 2 * x + y
        local = [pltpu.make_async_copy(p[i].at[me], rc[i].at[me], local_sems.at[i]) for i in range(n)]
        for cp in local:
            cp.start()
        sends = []
        for i in range(n):
            for j, chip in enumerate(chips):
                cj = 2 * chip[0] + chip[1]
                cp = _remote(p[i].at[cj], rc[i].at[me], send_sems.at[i, j], recv_sems.at[i, j], (*chip, c))
                cp.start()
                sends.append(cp)
        for i in range(n):
            for j, chip in enumerate(chips):
                cj = 2 * chip[0] + chip[1]
                _remote(rc[i].at[cj], rc[i].at[cj], send_sems.at[i, j], recv_sems.at[i, j], (*chip, c)).wait_recv()
        for cp in sends:
            cp.wait_send()
        for cp in local:
            cp.wait()

    return pl.pallas_call(
        body, name="grad_chip_exchange",
        in_specs=[ANY] * n, out_specs=[ANY] * n, out_shape=[_sds(p.shape, p.dtype) for p in parts],
        scratch_shapes=[pltpu.SemaphoreType.DMA((n, 3)), pltpu.SemaphoreType.DMA((n, 3)),
                        pltpu.SemaphoreType.DMA((n,))],
    )(*parts)


def _chip_add(rc, name):
    _, hr, wc = rc.shape
    tr = _pick(hr, (256, 352, 128, 64, 32, 16))

    def body(r_ref, o_ref):
        acc = r_ref[0].astype(F32)
        for s in range(1, N_CHIPS):
            acc = acc + r_ref[s].astype(F32)
        o_ref[...] = acc

    return pl.pallas_call(
        body, name=name, grid=(hr // tr,),
        in_specs=[pl.BlockSpec((N_CHIPS, tr, wc), lambda r: (0, r, 0))],
        out_specs=pl.BlockSpec((tr, wc), lambda r: (r, 0)), out_shape=_sds((hr, wc), F32),
        compiler_params=_cparams("parallel"),
    )(rc)


def _pair_gather(halves):
    n = len(halves)

    def body(*refs):
        hf, out = refs[:n], refs[n:2 * n]
        send_sems, recv_sems, local_sems = refs[2 * n:]
        x, y, c, _ = _place()
        sib = (x, y, 1 - c)
        local = [pltpu.make_async_copy(hf[i], _half(out[i], c), local_sems.at[i]) for i in range(n)]
        sends = [_remote(hf[i], _half(out[i], c), send_sems.at[i], recv_sems.at[i], sib) for i in range(n)]
        for cp in local + sends:
            cp.start()
        for i in range(n):
            _remote(_half(out[i], 1 - c), _half(out[i], 1 - c), send_sems.at[i], recv_sems.at[i], sib).wait_recv()
        for cp in sends:
            cp.wait_send()
        for cp in local:
            cp.wait()

    return pl.pallas_call(
        body, name="grad_pair_gather",
        in_specs=[ANY] * n, out_specs=[ANY] * n,
        out_shape=[_sds((2 * h.shape[0], h.shape[1]), h.dtype) for h in halves],
        scratch_shapes=[pltpu.SemaphoreType.DMA((n,)), pltpu.SemaphoreType.DMA((n,)), pltpu.SemaphoreType.DMA((n,))],
    )(*halves)


N_DEV = 8


def _all_reduce_small(buf):
    r, cdim = buf.shape

    def body(x_ref, o_ref, land, send_sems, recv_sems):
        x, y, c, _ = _place()
        me = 4 * x + 2 * y + c
        land[me] = x_ref[...]
        sends = []
        for k in range(1, N_DEV):
            kx, ky, kc = (k >> 2) & 1, (k >> 1) & 1, k & 1
            peer = (1 - x if kx else x, 1 - y if ky else y, 1 - c if kc else c)
            cp = _remote(x_ref, land.at[me], send_sems.at[k - 1], recv_sems.at[k - 1], peer)
            cp.start()
            sends.append(cp)
        for k in range(1, N_DEV):
            kx, ky, kc = (k >> 2) & 1, (k >> 1) & 1, k & 1
            peer = (1 - x if kx else x, 1 - y if ky else y, 1 - c if kc else c)
            pidx = 4 * peer[0] + 2 * peer[1] + peer[2]
            _remote(land.at[pidx], land.at[pidx], send_sems.at[k - 1], recv_sems.at[k - 1], peer).wait_recv()
        for cp in sends:
            cp.wait_send()
        acc = land[0]
        for dev in range(1, N_DEV):
            acc = acc + land[dev]
        o_ref[...] = acc

    vm = pl.BlockSpec(memory_space=pltpu.VMEM)
    return pl.pallas_call(
        body, name="all_reduce_small", in_specs=[vm], out_specs=vm, out_shape=_sds((r, cdim), F32),
        scratch_shapes=[pltpu.VMEM((N_DEV, r, cdim), F32), pltpu.SemaphoreType.DMA((N_DEV - 1,)),
                        pltpu.SemaphoreType.DMA((N_DEV - 1,))],
    )(buf)


SMALL_ROWS = 16


def kernel(x, mem, g_mix, w_in, conv_w, attn_sinks, w_attn_proj, w_conv_proj, w_mix_out, g_xattn, g_mem, w_xq, w_xkv, w_xo, g_ffn, w_ffn_in, w_ffn_out, g_final, loss_target, m_g_mix, m_w_in, m_conv_w, m_attn_sinks, m_w_attn_proj, m_w_conv_proj, m_w_mix_out, m_g_xattn, m_g_mem, m_w_xq, m_w_xkv, m_w_xo, m_g_ffn, m_w_ffn_in, m_w_ffn_out, m_g_final, v_g_mix, v_w_in, v_conv_w, v_attn_sinks, v_w_attn_proj, v_w_conv_proj, v_w_mix_out, v_g_xattn, v_g_mem, v_w_xq, v_w_xkv, v_w_xo, v_g_ffn, v_w_ffn_in, v_w_ffn_out, v_g_final):
    w = dict(g_mix=g_mix, w_in=w_in[0], conv_w=conv_w[0], attn_sinks=attn_sinks, w_attn_proj=w_attn_proj[0],
             w_conv_proj=w_conv_proj[0], w_mix_out=w_mix_out[0], g_xattn=g_xattn, g_mem=g_mem, w_xq=w_xq[0],
             w_xkv=w_xkv[0], w_xo=w_xo[0], g_ffn=g_ffn, w_ffn_in=w_ffn_in[0], w_ffn_out=w_ffn_out[0],
             g_final=g_final[None])
    m = dict(g_mix=m_g_mix, w_in=m_w_in[0], conv_w=m_conv_w[0], attn_sinks=m_attn_sinks,
             w_attn_proj=m_w_attn_proj[0], w_conv_proj=m_w_conv_proj[0], w_mix_out=m_w_mix_out[0],
             g_xattn=m_g_xattn, g_mem=m_g_mem, w_xq=m_w_xq[0], w_xkv=m_w_xkv[0], w_xo=m_w_xo[0], g_ffn=m_g_ffn,
             w_ffn_in=m_w_ffn_in[0], w_ffn_out=m_w_ffn_out[0], g_final=m_g_final[None])
    v = dict(g_mix=v_g_mix, w_in=v_w_in[0], conv_w=v_conv_w[0], attn_sinks=v_attn_sinks,
             w_attn_proj=v_w_attn_proj[0], w_conv_proj=v_w_conv_proj[0], w_mix_out=v_w_mix_out[0],
             g_xattn=v_g_xattn, g_mem=v_g_mem, w_xq=v_w_xq[0], w_xkv=v_w_xkv[0], w_xo=v_w_xo[0], g_ffn=v_g_ffn,
             w_ffn_in=v_w_ffn_in[0], w_ffn_out=v_w_ffn_out[0], g_final=v_g_final[None])
    names = [nm for nm, _ in BIG]
    axes = [ax for _, ax in BIG]
    d = x.shape[2]
    cw = w["conv_w"].shape[1] * N_CHIPS
    c_idx = lax.axis_index("c").astype(jnp.int32).reshape(1)
    chip = 2 * lax.axis_index("x") + lax.axis_index("y")

    shards = [w[nm].astype(BF16) for nm in names] + [w["conv_w"]]
    gathered = _all_gather(shards, axes + [1], whole=(len(names),))
    full = dict(zip(names, gathered[:-1]))
    conv_full = gathered[-1]

    grad_x, big, small = _local_step(
        x[0], mem[0], loss_target[0], w["g_mix"], conv_full, w["attn_sinks"], w["g_xattn"], w["g_mem"], w["g_ffn"],
        w["g_final"], *[full[nm] for nm in names])

    shard_shapes = [w[nm].shape for nm in names]
    glist = [big[nm] for nm in names]
    received = _pair_exchange(glist, axes, shard_shapes)
    parts = [_pair_add(g, ra, ax, ss, c_idx, "pair_add_" + nm)
             for g, ra, ax, ss, nm in zip(glist, received, axes, shard_shapes, names)]
    landed = _chip_exchange(parts)
    halves = [_chip_add(rc, "chip_add_" + nm) for rc, nm in zip(landed, names)]
    grads = dict(zip(names, _pair_gather(halves)))

    pw = max(d, cw)

    def row(a):
        return jnp.pad(a, ((0, 0), (0, pw - a.shape[1])))

    gains = ("g_mix", "g_xattn", "g_mem", "g_ffn", "g_final")
    packed = jnp.concatenate(
        [row(small[nm]) for nm in gains] + [row(small["conv_w"]),
         row(jnp.concatenate([small["attn_sinks"], small["loss"]], axis=1)),
         jnp.zeros((SMALL_ROWS - 9, pw), F32)], axis=0)
    total = _all_reduce_small(packed)
    nsink = attn_sinks.shape[1]
    grads.update({nm: total[k:k + 1, :d] for k, nm in enumerate(gains)})
    grads.update(conv_w=lax.dynamic_slice(total, (5, chip * (cw // N_CHIPS)), (3, cw // N_CHIPS)),
                 attn_sinks=total[8:9, :nsink])
    loss = total[8, nsink]

    order = ["g_mix", "w_in", "conv_w", "attn_sinks", "w_attn_proj", "w_conv_proj", "w_mix_out", "g_xattn", "g_mem",
             "w_xq", "w_xkv", "w_xo", "g_ffn", "w_ffn_in", "w_ffn_out", "g_final"]
    upd = {nm: _adamw(w[nm], grads[nm], m[nm], v[nm], "adamw_" + nm) for nm in order}

    stacked = set(names) | {"conv_w"}

    def shaped(nm, a):
        if nm == "g_final":
            return a[0]
        return a[None] if nm in stacked else a

    outs = [loss, grad_x[None]]
    outs += [shaped(nm, grads[nm]) for nm in order]
    for k in range(3):
        outs += [shaped(nm, upd[nm][k]) for nm in order]
    return tuple(outs)
```

```python
import functools

import jax
import jax.numpy as jnp
from jax import lax
from jax.experimental import pallas as pl
from jax.experimental.pallas import tpu as pltpu

F32 = jnp.float32
BF16 = jnp.bfloat16

VMEM_LIMIT_BYTES = 56 * 1024 * 1024
LANES = 128
HEAD_DIM = 64
BLOCK = 128
X_HEAD_DIM = 128
ROPE_THETA = 10000.0
EPS = 1e-6
NEG = -1e30
ADAM_LR, ADAM_B1, ADAM_B2, ADAM_EPS, ADAM_WD, ADAM_STEP = 0.001, 0.9, 0.999, 1e-08, 0.01, 10
N_CHIPS = 4
MESH = pl.DeviceIdType.MESH
ANY = pl.BlockSpec(memory_space=pl.ANY)


def _pick(dim, prefs):
    for p in prefs:
        if dim % p == 0:
            return p
    return dim


def _cparams(*sem):
    return pltpu.CompilerParams(dimension_semantics=sem, vmem_limit_bytes=VMEM_LIMIT_BYTES)


def _sds(shape, dtype):
    return jax.ShapeDtypeStruct(shape, dtype)


def _sigmoid(v):
    return 1.0 / (1.0 + jnp.exp(-v))


def _matmul(a, b, *, mode, out_dtype, name, res=None, tm=None, tn=None, tk=None):
    if mode == "nn":
        (m, k), (k2, n) = a.shape, b.shape
    elif mode == "nt":
        (m, k), (n, k2) = a.shape, b.shape
    else:
        (k, m), (k2, n) = a.shape, b.shape
    assert k == k2, (a.shape, b.shape, mode)
    tm = tm or _pick(m, (1024, 512, 256, 128))
    tn = tn or _pick(n, (512, 256, 128))
    tk = tk or (k if k <= 2048 else _pick(k, (2048, 2176, 2816, 1408, 1024, 512, 256, 128)))
    nk = k // tk
    dims = {"nn": (((1,), (0,)), ((), ())), "nt": (((1,), (1,)), ((), ())), "tn": (((0,), (0,)), ((), ()))}[mode]
    has_res = res is not None

    def body(*refs):
        a_ref, b_ref = refs[0], refs[1]
        r_ref = refs[2] if has_res else None
        o_ref = refs[2 + has_res]
        part = lax.dot_general(a_ref[...], b_ref[...], dims, preferred_element_type=F32)

        def finish(val):
            if has_res:
                val = val + r_ref[...]
            o_ref[...] = val.astype(o_ref.dtype)

        if nk == 1:
            finish(part)
        else:
            acc = refs[3 + has_res]
            kk = pl.program_id(2)

            @pl.when(kk == 0)
            def _():
                acc[...] = part

            @pl.when(kk > 0)
            def _():
                acc[...] += part

            @pl.when(kk == nk - 1)
            def _():
                finish(acc[...])

    if mode == "tn":
        a_spec = pl.BlockSpec((tk, tm), lambda i, j, kk: (kk, i))
    else:
        a_spec = pl.BlockSpec((tm, tk), lambda i, j, kk: (i, kk))
    if mode == "nt":
        b_spec = pl.BlockSpec((tn, tk), lambda i, j, kk: (j, kk))
    else:
        b_spec = pl.BlockSpec((tk, tn), lambda i, j, kk: (kk, j))
    o_spec = pl.BlockSpec((tm, tn), lambda i, j, kk: (i, j))
    return pl.pallas_call(
        body,
        name=name,
        grid=(m // tm, n // tn, nk),
        in_specs=[a_spec, b_spec] + ([o_spec] if has_res else []),
        out_specs=o_spec,
        out_shape=_sds((m, n), out_dtype),
        scratch_shapes=[pltpu.VMEM((tm, tn), F32)] if nk > 1 else [],
        compiler_params=_cparams("parallel", "parallel", "arbitrary"),
    )(*([a, b] + ([res] if has_res else [])))


def _rms_fwd(x, g, name):
    t, d = x.shape
    tm = _pick(t, (512, 256, 128))

    def body(x_ref, g_ref, o_ref):
        xf = x_ref[...]
        r = lax.rsqrt(jnp.mean(xf * xf, axis=-1, keepdims=True) + EPS)
        o_ref[...] = (xf * r * g_ref[...]).astype(o_ref.dtype)

    row = pl.BlockSpec((tm, d), lambda i: (i, 0))
    return pl.pallas_call(
        body, name=name, grid=(t // tm,),
        in_specs=[row, pl.BlockSpec((1, d), lambda i: (0, 0))],
        out_specs=row, out_shape=_sds((t, d), BF16),
        compiler_params=_cparams("parallel"),
    )(x, g)


def _rms_bwd_math(xf, g, du):
    r = lax.rsqrt(jnp.mean(xf * xf, axis=-1, keepdims=True) + EPS)
    xh = xf * r
    gdy = g * du
    dx = r * (gdy - xh * jnp.mean(gdy * xh, axis=-1, keepdims=True))
    dg = jnp.sum(du * xh, axis=0, keepdims=True)
    return dx, dg


def _rms_bwd(x, g, du, dh, name):
    t, d = x.shape
    tm = _pick(t, (256, 128))
    has_dh = dh is not None

    def body(*refs):
        x_ref, g_ref, du_ref = refs[0], refs[1], refs[2]
        o_ref, ob_ref, dg_ref = refs[3 + has_dh:]
        dx, dg = _rms_bwd_math(x_ref[...], g_ref[...], du_ref[...].astype(F32))
        if has_dh:
            dx = dx + refs[3][...]
        o_ref[...] = dx
        ob_ref[...] = dx.astype(BF16)

        @pl.when(pl.program_id(0) == 0)
        def _():
            dg_ref[...] = dg

        @pl.when(pl.program_id(0) > 0)
        def _():
            dg_ref[...] += dg

    row = pl.BlockSpec((tm, d), lambda i: (i, 0))
    vec = pl.BlockSpec((1, d), lambda i: (0, 0))
    return pl.pallas_call(
        body, name=name, grid=(t // tm,),
        in_specs=[row, vec, row] + ([row] if has_dh else []),
        out_specs=[row, row, vec],
        out_shape=[_sds((t, d), F32), _sds((t, d), BF16), _sds((1, d), F32)],
        compiler_params=_cparams("arbitrary"),
    )(*([x, g, du] + ([dh] if has_dh else [])))


def _loss_head(h, g, tgt):
    t, d = h.shape
    tm = _pick(t, (256, 128))

    def body(h_ref, g_ref, t_ref, o_ref, ob_ref, dg_ref, l_ref):
        xf = h_ref[...]
        gv = g_ref[...]
        r = lax.rsqrt(jnp.mean(xf * xf, axis=-1, keepdims=True) + EPS)
        err = xf * r * gv - t_ref[...]
        part = 0.5 * jnp.sum(jnp.mean(err * err, axis=-1, keepdims=True), axis=0, keepdims=True)
        dx, dg = _rms_bwd_math(xf, gv, err * (1.0 / d))
        o_ref[...] = dx
        ob_ref[...] = dx.astype(BF16)
        lrow = jnp.broadcast_to(part, (1, LANES))

        @pl.when(pl.program_id(0) == 0)
        def _():
            dg_ref[...] = dg
            l_ref[...] = lrow

        @pl.when(pl.program_id(0) > 0)
        def _():
            dg_ref[...] += dg
            l_ref[...] += lrow

    row = pl.BlockSpec((tm, d), lambda i: (i, 0))
    vec = pl.BlockSpec((1, d), lambda i: (0, 0))
    return pl.pallas_call(
        body, name="loss_head", grid=(t // tm,),
        in_specs=[row, vec, row],
        out_specs=[row, row, vec, pl.BlockSpec((1, LANES), lambda i: (0, 0))],
        out_shape=[_sds((t, d), F32), _sds((t, d), BF16), _sds((1, d), F32), _sds((1, LANES), F32)],
        compiler_params=_cparams("arbitrary"),
    )(h, g, tgt)


def _rope_tables(t):
    half = HEAD_DIM // 2
    inv_freq = ROPE_THETA ** (-jnp.arange(half, dtype=F32) / half)
    ang = jnp.arange(t, dtype=F32)[:, None] * inv_freq[None, :]
    cos = jnp.cos(ang)
    sin = jnp.sin(ang)
    reps = LANES // HEAD_DIM
    cos_t = jnp.tile(jnp.concatenate([cos, cos], axis=1), (1, reps))
    sin_t = jnp.tile(jnp.concatenate([-sin, sin], axis=1), (1, reps))
    return cos_t, sin_t


def _rope(v, cos, sin):
    w = v.shape[1]
    c = jnp.tile(cos, (1, w // LANES))
    s = jnp.tile(sin, (1, w // LANES))
    lane = lax.broadcasted_iota(jnp.int32, v.shape, 1)
    first = (lane % HEAD_DIM) < (HEAD_DIM // 2)
    partner = jnp.where(first, pltpu.roll(v, w - HEAD_DIM // 2, 1), pltpu.roll(v, HEAD_DIM // 2, 1))
    return v * c + partner * s


def _heads(v, first, count):
    return jnp.concatenate([v[:, (first + i) * HEAD_DIM:(first + i + 1) * HEAD_DIM] for i in range(count)], axis=0)


def _attn_probs(qs, kb, n, h, qpk, sinks_ref):
    s = lax.dot_general(qs, kb, (((1,), (1,)), ((), ())), preferred_element_type=F32) * (HEAD_DIM ** -0.5)
    qi = lax.broadcasted_iota(jnp.int32, (BLOCK, 2 * BLOCK), 0)
    kc = lax.broadcasted_iota(jnp.int32, (BLOCK, 2 * BLOCK), 1)
    valid = (kc > qi) & (kc <= qi + BLOCK) & ((kc >= BLOCK) | (n > 0))
    bias = jnp.tile(jnp.where(valid, 0.0, NEG).astype(F32), (qpk, 1))
    s = s + bias
    rowg = lax.broadcasted_iota(jnp.int32, (qpk * BLOCK, 1), 0) // BLOCK
    sink = jnp.zeros((qpk * BLOCK, 1), F32)
    for g in range(qpk):
        sink = jnp.where(rowg == g, sinks_ref[0, h * qpk + g], sink)
    m = jnp.maximum(jnp.max(s, axis=-1, keepdims=True), sink)
    e = jnp.exp(s - m)
    es = jnp.exp(sink - m)
    inv = 1.0 / (jnp.sum(e, axis=-1, keepdims=True) + es)
    return e * inv, es * inv, rowg


def _attn_specs(aw, kvw):
    koff = aw // kvw
    prev = lambda n: jnp.maximum(n - 1, 0)
    return [
        pl.BlockSpec((BLOCK, aw), lambda n: (n, 0)),
        pl.BlockSpec((BLOCK, kvw), lambda n: (n, koff)),
        pl.BlockSpec((BLOCK, kvw), lambda n: (prev(n), koff)),
        pl.BlockSpec((BLOCK, kvw), lambda n: (n, koff + 1)),
        pl.BlockSpec((BLOCK, kvw), lambda n: (prev(n), koff + 1)),
        pl.BlockSpec((BLOCK, LANES), lambda n: (n, 0)),
        pl.BlockSpec((BLOCK, LANES), lambda n: (n, 0)),
        pl.BlockSpec((BLOCK, LANES), lambda n: (prev(n), 0)),
        pl.BlockSpec((BLOCK, LANES), lambda n: (prev(n), 0)),
        pl.BlockSpec(memory_space=pltpu.SMEM),
    ]


def _attn_fwd(proj, cos, sin, sinks, aw, kvw):
    t = proj.shape[0]
    nkv = kvw // HEAD_DIM
    qpk = aw // kvw

    def body(q_ref, kc_ref, kp_ref, vc_ref, vp_ref, cc_ref, sc_ref, cp_ref, sp_ref, sinks_ref, o_ref):
        n = pl.program_id(0)
        q = _rope(q_ref[...], cc_ref[...], sc_ref[...]).astype(BF16)
        kc = _rope(kc_ref[...], cc_ref[...], sc_ref[...]).astype(BF16)
        kp = _rope(kp_ref[...], cp_ref[...], sp_ref[...]).astype(BF16)
        vc = vc_ref[...].astype(BF16)
        vp = vp_ref[...].astype(BF16)
        outs = []
        for h in range(nkv):
            hs = slice(h * HEAD_DIM, (h + 1) * HEAD_DIM)
            kb = jnp.concatenate([kp[:, hs], kc[:, hs]], axis=0)
            vb = jnp.concatenate([vp[:, hs], vc[:, hs]], axis=0)
            p, _, _ = _attn_probs(_heads(q, h * qpk, qpk), kb, n, h, qpk, sinks_ref)
            o = jnp.dot(p.astype(BF16), vb, preferred_element_type=F32)
            outs += [o[g * BLOCK:(g + 1) * BLOCK] for g in range(qpk)]
        o_ref[...] = jnp.concatenate(outs, axis=1).astype(o_ref.dtype)

    return pl.pallas_call(
        body, name="attn_fwd", grid=(t // BLOCK,),
        in_specs=_attn_specs(aw, kvw),
        out_specs=pl.BlockSpec((BLOCK, aw), lambda n: (n, 0)),
        out_shape=_sds((t, aw), BF16),
        compiler_params=_cparams("parallel"),
    )(proj, proj, proj, proj, proj, cos, sin, cos, sin, sinks)


def _attn_bwd(proj, cos, sin, sinks, do, aw, kvw):
    t = proj.shape[0]
    nkv = kvw // HEAD_DIM
    qpk = aw // kvw
    scale = HEAD_DIM ** -0.5

    def body(q_ref, kc_ref, kp_ref, vc_ref, vp_ref, cc_ref, sc_ref, cp_ref, sp_ref, sinks_ref, do_ref,
             dq_ref, dkc_ref, dkp_ref, dvc_ref, dvp_ref, ds_ref):
        n = pl.program_id(0)
        cc, sc, cp, sp = cc_ref[...], sc_ref[...], cp_ref[...], sp_ref[...]
        q = _rope(q_ref[...], cc, sc).astype(BF16)
        kc = _rope(kc_ref[...], cc, sc).astype(BF16)
        kp = _rope(kp_ref[...], cp, sp).astype(BF16)
        vc = vc_ref[...].astype(BF16)
        vp = vp_ref[...].astype(BF16)
        dout = do_ref[...]
        dqs, dks, dvs = [], [], []
        lane = lax.broadcasted_iota(jnp.int32, (8, LANES), 1)
        row0 = lax.broadcasted_iota(jnp.int32, (8, LANES), 0) == 0
        dsink = jnp.zeros((8, LANES), F32)
        for h in range(nkv):
            hs = slice(h * HEAD_DIM, (h + 1) * HEAD_DIM)
            kb = jnp.concatenate([kp[:, hs], kc[:, hs]], axis=0)
            vb = jnp.concatenate([vp[:, hs], vc[:, hs]], axis=0)
            qs = _heads(q, h * qpk, qpk)
            dos = _heads(dout, h * qpk, qpk)
            p, psink, rowg = _attn_probs(qs, kb, n, h, qpk, sinks_ref)
            pb = p.astype(BF16)
            o = jnp.dot(pb, vb, preferred_element_type=F32)
            delta = jnp.sum(dos.astype(F32) * o, axis=-1, keepdims=True)
            dvs.append(lax.dot_general(pb, dos, (((0,), (0,)), ((), ())), preferred_element_type=F32))
            dp = lax.dot_general(dos, vb, (((1,), (1,)), ((), ())), preferred_element_type=F32)
            dsc = (p * (dp - delta)).astype(BF16)
            dq_h = jnp.dot(dsc, kb, preferred_element_type=F32) * scale
            dqs += [dq_h[g * BLOCK:(g + 1) * BLOCK] for g in range(qpk)]
            dks.append(lax.dot_general(dsc, qs, (((0,), (0,)), ((), ())), preferred_element_type=F32) * scale)
            sink_term = psink * delta
            for g in range(qpk):
                val = -jnp.sum(jnp.where(rowg == g, sink_term, 0.0))
                dsink = jnp.where(row0 & (lane == h * qpk + g), val, dsink)
        dq = jnp.concatenate(dqs, axis=1)
        dq_ref[...] = _rope(dq, cc, -sc).astype(dq_ref.dtype)
        dk = jnp.concatenate(dks, axis=1)
        dv = jnp.concatenate(dvs, axis=1)
        dkp_ref[...] = _rope(dk[:BLOCK], cp, -sp)
        dkc_ref[...] = _rope(dk[BLOCK:], cc, -sc)
        dvp_ref[...] = dv[:BLOCK]
        dvc_ref[...] = dv[BLOCK:]

        @pl.when(n == 0)
        def _():
            ds_ref[...] = dsink

        @pl.when(n > 0)
        def _():
            ds_ref[...] += dsink

    kv_spec = pl.BlockSpec((BLOCK, kvw), lambda n: (n, 0))
    return pl.pallas_call(
        body, name="attn_bwd", grid=(t // BLOCK,),
        in_specs=_attn_specs(aw, kvw) + [pl.BlockSpec((BLOCK, aw), lambda n: (n, 0))],
        out_specs=[pl.BlockSpec((BLOCK, aw), lambda n: (n, 0)), kv_spec, kv_spec, kv_spec, kv_spec,
                   pl.BlockSpec((8, LANES), lambda n: (0, 0))],
        out_shape=[_sds((t, aw), BF16)] + [_sds((t, kvw), F32)] * 4 + [_sds((8, LANES), F32)],
        compiler_params=_cparams("arbitrary"),
    )(proj, proj, proj, proj, proj, cos, sin, cos, sin, sinks, do)


def _shift_down(v, k, halo):
    rows = lax.broadcasted_iota(jnp.int32, v.shape, 0)
    out = pltpu.roll(v, k, 0)
    for r in range(k):
        out = jnp.where(rows == r, halo[8 - k + r:8 - k + r + 1, :], out)
    return out


def _shift_up(v, k, halo):
    tm = v.shape[0]
    rows = lax.broadcasted_iota(jnp.int32, v.shape, 0)
    out = pltpu.roll(v, tm - k, 0)
    for r in range(k):
        out = jnp.where(rows == tm - k + r, halo[r:r + 1, :], out)
    return out


def _conv_fwd(proj, conv_w, zoff, cw, cb):
    t = proj.shape[0]
    tm = _pick(t, (512, 256, 128))
    zb, nb = zoff // cb, cw // cb
    h8 = tm // 8

    def body(z_ref, gb_ref, gc_ref, zp_ref, gcp_ref, w_ref, o_ref):
        i = pl.program_id(0)
        cz = gc_ref[...] * z_ref[...]
        czp = gcp_ref[...] * zp_ref[...] * (i > 0).astype(F32)
        w = w_ref[...]
        y = w[0:1] * _shift_down(cz, 2, czp) + w[1:2] * _shift_down(cz, 1, czp) + w[2:3] * cz
        o_ref[...] = (gb_ref[...] * y).astype(o_ref.dtype)

    def col(k):
        return pl.BlockSpec((tm, cb), lambda i, j: (i, zb + k * nb + j))

    def halo(k):
        return pl.BlockSpec((8, cb), lambda i, j: (jnp.maximum(i * h8 - 1, 0), zb + k * nb + j))

    return pl.pallas_call(
        body, name="conv_fwd", grid=(t // tm, nb),
        in_specs=[col(0), col(1), col(2), halo(0), halo(2), pl.BlockSpec((3, cb), lambda i, j: (0, j))],
        out_specs=pl.BlockSpec((tm, cb), lambda i, j: (i, j)),
        out_shape=_sds((t, cw), BF16),
        compiler_params=_cparams("parallel", "parallel"),
    )(proj, proj, proj, proj, proj, conv_w)


def _conv_bwd(proj, conv_w, dco, zoff, cw, cb):
    t = proj.shape[0]
    tm = _pick(t, (512, 256, 128))
    zb, nb = zoff // cb, cw // cb
    h8 = tm // 8
    nt = t // tm

    def body(z_ref, gb_ref, gc_ref, zp_ref, gcp_ref, gbn_ref, w_ref, d_ref, dn_ref, dz_ref, dgb_ref, dgc_ref, dw_ref):
        i = pl.program_id(1)
        z, gb, gc = z_ref[...], gb_ref[...], gc_ref[...]
        d = d_ref[...].astype(F32)
        cz = gc * z
        czp = gcp_ref[...] * zp_ref[...] * (i > 0).astype(F32)
        w = w_ref[...]
        cz1 = _shift_down(cz, 1, czp)
        cz2 = _shift_down(cz, 2, czp)
        y = w[0:1] * cz2 + w[1:2] * cz1 + w[2:3] * cz
        dgb_ref[...] = (d * y).astype(dgb_ref.dtype)
        dy = d * gb
        dyn = dn_ref[...].astype(F32) * gbn_ref[...] * (i < nt - 1).astype(F32)
        dcz = w[2:3] * dy + w[1:2] * _shift_up(dy, 1, dyn) + w[0:1] * _shift_up(dy, 2, dyn)
        dgc_ref[...] = (dcz * z).astype(dgc_ref.dtype)
        dz_ref[...] = (dcz * gc).astype(dz_ref.dtype)
        rows = lax.broadcasted_iota(jnp.int32, (8, cb), 0)
        dw = jnp.zeros((8, cb), F32)
        for r, tap in enumerate((cz2, cz1, cz)):
            dw = jnp.where(rows == r, jnp.sum(dy * tap, axis=0, keepdims=True), dw)

        @pl.when(i == 0)
        def _():
            dw_ref[...] = dw

        @pl.when(i > 0)
        def _():
            dw_ref[...] += dw

    def col(k):
        return pl.BlockSpec((tm, cb), lambda j, i: (i, zb + k * nb + j))

    def halo_prev(k):
        return pl.BlockSpec((8, cb), lambda j, i: (jnp.maximum(i * h8 - 1, 0), zb + k * nb + j))

    own = pl.BlockSpec((tm, cb), lambda j, i: (i, j))
    nxt = lambda i: jnp.minimum((i + 1) * h8, t // 8 - 1)
    return pl.pallas_call(
        body, name="conv_bwd", grid=(nb, nt),
        in_specs=[col(0), col(1), col(2), halo_prev(0), halo_prev(2),
                  pl.BlockSpec((8, cb), lambda j, i: (nxt(i), zb + nb + j)),
                  pl.BlockSpec((3, cb), lambda j, i: (0, j)), own,
                  pl.BlockSpec((8, cb), lambda j, i: (nxt(i), j))],
        out_specs=[own, own, own, pl.BlockSpec((8, cb), lambda j, i: (0, j))],
        out_shape=[_sds((t, cw), BF16)] * 3 + [_sds((8, cw), F32)],
        compiler_params=_cparams("parallel", "arbitrary"),
    )(proj, proj, proj, proj, proj, proj, conv_w, dco, dco)


def _merge_fwd(proj, ya, yc, goff, cb):
    t, d = ya.shape
    tm = _pick(t, (512, 256, 128))
    gb_, nb = goff // cb, d // cb

    def body(ga_ref, gc_ref, ya_ref, yc_ref, o_ref):
        o_ref[...] = (_sigmoid(ga_ref[...]) * ya_ref[...] + _sigmoid(gc_ref[...]) * yc_ref[...]).astype(o_ref.dtype)

    own = pl.BlockSpec((tm, cb), lambda i, j: (i, j))
    return pl.pallas_call(
        body, name="merge_fwd", grid=(t // tm, nb),
        in_specs=[pl.BlockSpec((tm, cb), lambda i, j: (i, gb_ + j)),
                  pl.BlockSpec((tm, cb), lambda i, j: (i, gb_ + nb + j)), own, own],
        out_specs=own, out_shape=_sds((t, d), BF16),
        compiler_params=_cparams("parallel", "parallel"),
    )(proj, proj, ya, yc)


def _merge_bwd(proj, ya, yc, dm, goff, cb):
    t, d = ya.shape
    tm = _pick(t, (512, 256, 128))
    gb_, nb = goff // cb, d // cb

    def body(ga_ref, gc_ref, ya_ref, yc_ref, dm_ref, dya_ref, dyc_ref, dga_ref, dgc_ref):
        dmv = dm_ref[...].astype(F32)
        sa = _sigmoid(ga_ref[...])
        sc = _sigmoid(gc_ref[...])
        dya_ref[...] = (dmv * sa).astype(BF16)
        dyc_ref[...] = (dmv * sc).astype(BF16)
        dga_ref[...] = (dmv * ya_ref[...] * sa * (1.0 - sa)).astype(BF16)
        dgc_ref[...] = (dmv * yc_ref[...] * sc * (1.0 - sc)).astype(BF16)

    own = pl.BlockSpec((tm, cb), lambda i, j: (i, j))
    return pl.pallas_call(
        body, name="merge_bwd", grid=(t // tm, nb),
        in_specs=[pl.BlockSpec((tm, cb), lambda i, j: (i, gb_ + j)),
                  pl.BlockSpec((tm, cb), lambda i, j: (i, gb_ + nb + j)), own, own, own],
        out_specs=[own] * 4, out_shape=[_sds((t, d), BF16)] * 4,
        compiler_params=_cparams("parallel", "parallel"),
    )(proj, proj, ya, yc, dm)


def _assemble_dproj(dq, dkc, dkp, dvc, dvp, dz, dgb, dgc, dga, dgg):
    t, aw = dq.shape
    kvw, cw, d = dkc.shape[1], dz.shape[1], dga.shape[1]
    nblk = t // BLOCK
    width = aw + 2 * kvw + 3 * cw + 2 * d

    def body(dq_ref, dkc_ref, dkp_ref, dvc_ref, dvp_ref, dz_ref, dgb_ref, dgc_ref, dga_ref, dgg_ref, o_ref):
        keep = (pl.program_id(0) < nblk - 1).astype(F32)
        dk = dkc_ref[...] + dkp_ref[...] * keep
        dv = dvc_ref[...] + dvp_ref[...] * keep
        o_ref[...] = jnp.concatenate(
            [dq_ref[...], dk.astype(BF16), dv.astype(BF16), dz_ref[...], dgb_ref[...], dgc_ref[...],
             dga_ref[...], dgg_ref[...]], axis=1)

    def cur(w):
        return pl.BlockSpec((BLOCK, w), lambda n: (n, 0))

    def nxt(w):
        return pl.BlockSpec((BLOCK, w), lambda n: (jnp.minimum(n + 1, nblk - 1), 0))

    return pl.pallas_call(
        body, name="assemble_dproj", grid=(nblk,),
        in_specs=[cur(aw), cur(kvw), nxt(kvw), cur(kvw), nxt(kvw), cur(cw), cur(cw), cur(cw), cur(d), cur(d)],
        out_specs=cur(width), out_shape=_sds((t, width), BF16),
        compiler_params=_cparams("parallel"),
    )(dq, dkc, dkp, dvc, dvp, dz, dgb, dgc, dga, dgg)


def _xattn_probs(qh, kh):
    s = lax.dot_general(qh, kh, (((1,), (1,)), ((), ())), preferred_element_type=F32) * (X_HEAD_DIM ** -0.5)
    e = jnp.exp(s - jnp.max(s, axis=-1, keepdims=True))
    return e * (1.0 / jnp.sum(e, axis=-1, keepdims=True))


def _xattn_fwd(xq, kv):
    t, xw = xq.shape
    mt = kv.shape[0]
    tm = _pick(t, (512, 256, 128))

    def body(q_ref, kv_ref, o_ref):
        outs = []
        for hd in range(xw // X_HEAD_DIM):
            hs = slice(hd * X_HEAD_DIM, (hd + 1) * X_HEAD_DIM)
            vs = slice(xw + hd * X_HEAD_DIM, xw + (hd + 1) * X_HEAD_DIM)
            p = _xattn_probs(q_ref[:, hs], kv_ref[:, hs])
            outs.append(jnp.dot(p.astype(BF16), kv_ref[:, vs], preferred_element_type=F32))
        o_ref[...] = jnp.concatenate(outs, axis=1).astype(o_ref.dtype)

    return pl.pallas_call(
        body, name="xattn_fwd", grid=(t // tm,),
        in_specs=[pl.BlockSpec((tm, xw), lambda i: (i, 0)), pl.BlockSpec((mt, 2 * xw), lambda i: (0, 0))],
        out_specs=pl.BlockSpec((tm, xw), lambda i: (i, 0)), out_shape=_sds((t, xw), BF16),
        compiler_params=_cparams("parallel"),
    )(xq, kv)


def _xattn_bwd(xq, kv, do):
    t, xw = xq.shape
    mt = kv.shape[0]
    tm = _pick(t, (512, 256, 128))
    scale = X_HEAD_DIM ** -0.5

    def body(q_ref, kv_ref, do_ref, dq_ref, dkv_ref):
        dqs, dks, dvs = [], [], []
        for hd in range(xw // X_HEAD_DIM):
            hs = slice(hd * X_HEAD_DIM, (hd + 1) * X_HEAD_DIM)
            vs = slice(xw + hd * X_HEAD_DIM, xw + (hd + 1) * X_HEAD_DIM)
            qh, kh, vh, doh = q_ref[:, hs], kv_ref[:, hs], kv_ref[:, vs], do_ref[:, hs]
            p = _xattn_probs(qh, kh)
            pb = p.astype(BF16)
            o = jnp.dot(pb, vh, preferred_element_type=F32)
            delta = jnp.sum(doh.astype(F32) * o, axis=-1, keepdims=True)
            dvs.append(lax.dot_general(pb, doh, (((0,), (0,)), ((), ())), preferred_element_type=F32))
            dp = lax.dot_general(doh, vh, (((1,), (1,)), ((), ())), preferred_element_type=F32)
            dsc = (p * (dp - delta)).astype(BF16)
            dqs.append(jnp.dot(dsc, kh, preferred_element_type=F32) * scale)
            dks.append(lax.dot_general(dsc, qh, (((0,), (0,)), ((), ())), preferred_element_type=F32) * scale)
        dq_ref[...] = jnp.concatenate(dqs, axis=1).astype(dq_ref.dtype)
        dkv = jnp.concatenate(dks + dvs, axis=1)

        @pl.when(pl.program_id(0) == 0)
        def _():
            dkv_ref[...] = dkv

        @pl.when(pl.program_id(0) > 0)
        def _():
            dkv_ref[...] += dkv

    row = pl.BlockSpec((tm, xw), lambda i: (i, 0))
    whole = pl.BlockSpec((mt, 2 * xw), lambda i: (0, 0))
    return pl.pallas_call(
        body, name="xattn_bwd", grid=(t // tm,),
        in_specs=[row, whole, row], out_specs=[row, whole],
        out_shape=[_sds((t, xw), BF16), _sds((mt, 2 * xw), F32)],
        compiler_params=_cparams("arbitrary"),
    )(xq, kv, do)


def _swiglu_fwd(hid):
    t, f2 = hid.shape
    f = f2 // 2
    tm = _pick(t, (256, 128))

    def body(h_ref, o_ref):
        a = h_ref[:, :f].astype(F32)
        b = h_ref[:, f:].astype(F32)
        o_ref[...] = (a * _sigmoid(a) * b).astype(o_ref.dtype)

    return pl.pallas_call(
        body, name="swiglu_fwd", grid=(t // tm,),
        in_specs=[pl.BlockSpec((tm, f2), lambda i: (i, 0))],
        out_specs=pl.BlockSpec((tm, f), lambda i: (i, 0)), out_shape=_sds((t, f), BF16),
        compiler_params=_cparams("parallel"),
    )(hid)


def _swiglu_bwd(hid, dact):
    t, f2 = hid.shape
    f = f2 // 2
    tm = _pick(t, (128,))

    def body(h_ref, d_ref, o_ref):
        a = h_ref[:, :f].astype(F32)
        b = h_ref[:, f:].astype(F32)
        d = d_ref[...].astype(F32)
        sg = _sigmoid(a)
        o_ref[:, :f] = (d * b * sg * (1.0 + a * (1.0 - sg))).astype(o_ref.dtype)
        o_ref[:, f:] = (d * a * sg).astype(o_ref.dtype)

    return pl.pallas_call(
        body, name="swiglu_bwd", grid=(t // tm,),
        in_specs=[pl.BlockSpec((tm, f2), lambda i: (i, 0)), pl.BlockSpec((tm, f), lambda i: (i, 0))],
        out_specs=pl.BlockSpec((tm, f2), lambda i: (i, 0)), out_shape=_sds((t, f2), BF16),
        compiler_params=_cparams("parallel"),
    )(hid, dact)


def _adamw(w, g, m, v, name):
    r, c = w.shape
    tr = _pick(r, (256, 128, 64, 32, 16, 8)) if r * c > 65536 else r

    def body(w_ref, g_ref, m_ref, v_ref, d_ref, nm_ref, nv_ref):
        gv = g_ref[...]
        m2 = ADAM_B1 * m_ref[...] + (1.0 - ADAM_B1) * gv
        v2 = ADAM_B2 * v_ref[...] + (1.0 - ADAM_B2) * (gv * gv)
        m_hat = m2 / (1.0 - ADAM_B1 ** ADAM_STEP)
        v_hat = v2 / (1.0 - ADAM_B2 ** ADAM_STEP)
        d_ref[...] = -ADAM_LR * (m_hat / (jnp.sqrt(v_hat) + ADAM_EPS) + ADAM_WD * w_ref[...])
        nm_ref[...] = m2
        nv_ref[...] = v2

    blk = pl.BlockSpec((tr, c), lambda i: (i, 0))
    return pl.pallas_call(
        body, name=name, grid=(r // tr,),
        in_specs=[blk] * 4, out_specs=[blk] * 3, out_shape=[_sds((r, c), F32)] * 3,
        compiler_params=_cparams("parallel"),
    )(w, g, m, v)


def _local_step(x, mem, tgt, g_mix, conv_w, sinks, g_xattn, g_mem, g_ffn, g_final,
                w_in, w_ap, w_cp, w_mo, w_xq, w_xkv, w_xo, w_fi, w_fo):
    t, d = x.shape
    aw, cw = w_ap.shape[0], w_cp.shape[0]
    kvw = (w_in.shape[1] - aw - 3 * cw - 2 * d) // 2
    cb = 2 * kvw
    zoff = aw + 2 * kvw
    goff = zoff + 3 * cw
    cos, sin = _rope_tables(t)
    mm = _matmul

    u1 = _rms_fwd(x, g_mix, "rms_mix")
    proj = mm(u1, w_in, mode="nn", out_dtype=F32, name="mm_proj")
    attn_o = _attn_fwd(proj, cos, sin, sinks, aw, kvw)
    conv_o = _conv_fwd(proj, conv_w, zoff, cw, cb)
    ya = mm(attn_o, w_ap, mode="nn", out_dtype=F32, name="mm_yattn")
    yc = mm(conv_o, w_cp, mode="nn", out_dtype=F32, name="mm_yconv")
    merged = _merge_fwd(proj, ya, yc, goff, cb)
    h1 = mm(merged, w_mo, mode="nn", out_dtype=F32, name="mm_mix", res=x)
    u2 = _rms_fwd(h1, g_xattn, "rms_xattn")
    mem_n = _rms_fwd(mem, g_mem, "rms_mem")
    xq = mm(u2, w_xq, mode="nn", out_dtype=BF16, name="mm_xq")
    kv = mm(mem_n, w_xkv, mode="nn", out_dtype=BF16, name="mm_xkv")
    xo = _xattn_fwd(xq, kv)
    h2 = mm(xo, w_xo, mode="nn", out_dtype=F32, name="mm_xo", res=h1)
    u3 = _rms_fwd(h2, g_ffn, "rms_ffn")
    hid = mm(u3, w_fi, mode="nn", out_dtype=BF16, name="mm_ffn_in")
    act = _swiglu_fwd(hid)
    h3 = mm(act, w_fo, mode="nn", out_dtype=F32, name="mm_ffn_out", res=h2)
    dh3, dh3b, dg_final, loss = _loss_head(h3, g_final, tgt)

    gw_fo = mm(act, dh3b, mode="tn", out_dtype=BF16, name="mm_dw_ffn_out")
    dact = mm(dh3b, w_fo, mode="nt", out_dtype=BF16, name="mm_dact")
    dhid = _swiglu_bwd(hid, dact)
    gw_fi = mm(u3, dhid, mode="tn", out_dtype=BF16, name="mm_dw_ffn_in")
    du3 = mm(dhid, w_fi, mode="nt", out_dtype=F32, name="mm_du3")
    dh2, dh2b, dg_ffn = _rms_bwd(h2, g_ffn, du3, dh3, "rms_bwd_ffn")
    gw_xo = mm(xo, dh2b, mode="tn", out_dtype=BF16, name="mm_dw_xo")
    dxo = mm(dh2b, w_xo, mode="nt", out_dtype=BF16, name="mm_dxo")
    dxq, dkv = _xattn_bwd(xq, kv, dxo)
    dkvb = dkv.astype(BF16)
    gw_xq = mm(u2, dxq, mode="tn", out_dtype=BF16, name="mm_dw_xq")
    du2 = mm(dxq, w_xq, mode="nt", out_dtype=F32, name="mm_du2")
    gw_xkv = mm(mem_n, dkvb, mode="tn", out_dtype=BF16, name="mm_dw_xkv")
    dmem_n = mm(dkvb, w_xkv, mode="nt", out_dtype=F32, name="mm_dmem")
    _, _, dg_mem = _rms_bwd(mem, g_mem, dmem_n, None, "rms_bwd_mem")
    dh1, dh1b, dg_xattn = _rms_bwd(h1, g_xattn, du2, dh2, "rms_bwd_xattn")
    gw_mo = mm(merged, dh1b, mode="tn", out_dtype=BF16, name="mm_dw_mix")
    dmerged = mm(dh1b, w_mo, mode="nt", out_dtype=BF16, name="mm_dmerged")
    dya, dyc, dga, dgg = _merge_bwd(proj, ya, yc, dmerged, goff, cb)
    gw_ap = mm(attn_o, dya, mode="tn", out_dtype=BF16, name="mm_dw_attn_proj")
    dattn_o = mm(dya, w_ap, mode="nt", out_dtype=BF16, name="mm_dattn")
    gw_cp = mm(conv_o, dyc, mode="tn", out_dtype=BF16, name="mm_dw_conv_proj")
    dconv_o = mm(dyc, w_cp, mode="nt", out_dtype=BF16, name="mm_dconv")
    dz, dgb, dgc, dconv_w = _conv_bwd(proj, conv_w, dconv_o, zoff, cw, cb)
    dq, dkc, dkp, dvc, dvp, dsinks = _attn_bwd(proj, cos, sin, sinks, dattn_o, aw, kvw)
    dproj = _assemble_dproj(dq, dkc, dkp, dvc, dvp, dz, dgb, dgc, dga, dgg)
    gw_in = mm(u1, dproj, mode="tn", out_dtype=BF16, name="mm_dw_in")
    du1 = mm(dproj, w_in, mode="nt", out_dtype=F32, name="mm_du1")
    grad_x, _, dg_mix = _rms_bwd(x, g_mix, du1, dh1, "rms_bwd_mix")

    big = dict(w_in=gw_in, w_attn_proj=gw_ap, w_conv_proj=gw_cp, w_mix_out=gw_mo, w_xq=gw_xq, w_xkv=gw_xkv,
               w_xo=gw_xo, w_ffn_in=gw_fi, w_ffn_out=gw_fo)
    small = dict(g_mix=dg_mix, g_xattn=dg_xattn, g_mem=dg_mem, g_ffn=dg_ffn, g_final=dg_final,
                 conv_w=dconv_w[:3], attn_sinks=dsinks[0:1, :sinks.shape[1]], loss=loss[0:1, 0:1])
    return grad_x, big, small


BIG = (("w_in", 1), ("w_attn_proj", 1), ("w_conv_proj", 1), ("w_mix_out", 0), ("w_xq", 0), ("w_xkv", 0),
       ("w_xo", 1), ("w_ffn_in", 1), ("w_ffn_out", 0))


def _place():
    x, y, c = lax.axis_index("x"), lax.axis_index("y"), lax.axis_index("c")
    chips = [(1 - x, y), (x, 1 - y), (1 - x, 1 - y)]
    return x, y, c, chips


def _window(ref, ax, shard_shape, s, h):
    sr, sc = shard_shape
    hr = sr // 2
    if ax == 1:
        return ref.at[pl.ds(pl.multiple_of(h * hr, 16), hr), pl.ds(pl.multiple_of(s * sc, LANES), sc)]
    return ref.at[pl.ds(pl.multiple_of(s * sr + h * hr, 16), hr), :]


def _half(ref, h):
    hr = ref.shape[0] // 2
    return ref.at[pl.ds(pl.multiple_of(h * hr, 16), hr), :]


def _remote(src, dst, send_sem, recv_sem, dev):
    return pltpu.make_async_remote_copy(src_ref=src, dst_ref=dst, send_sem=send_sem, recv_sem=recv_sem,
                                        device_id=dev, device_id_type=MESH)


def _cast_to_full(shard, ax, me, name):
    sr, sc = shard.shape
    tr = _pick(sr, (256, 352, 128, 64, 32, 16))
    nr = sr // tr
    full = (sr * N_CHIPS, sc) if ax == 0 else (sr, sc * N_CHIPS)

    def body(me_ref, s_ref, o_ref):
        o_ref[...] = s_ref[...].astype(o_ref.dtype)

    if ax == 1:
        out_spec = pl.BlockSpec((tr, sc), lambda r, me_ref: (r, me_ref[0]))
    else:
        out_spec = pl.BlockSpec((tr, sc), lambda r, me_ref: (me_ref[0] * nr + r, 0))
    return pl.pallas_call(
        body, name=name,
        grid_spec=pltpu.PrefetchScalarGridSpec(
            num_scalar_prefetch=1, grid=(nr,), in_specs=[pl.BlockSpec((tr, sc), lambda r, me_ref: (r, 0))],
            out_specs=out_spec),
        out_shape=_sds(full, BF16),
        compiler_params=_cparams("parallel"),
    )(me, shard)


def _all_gather(fulls, axes, shard_shapes, small):
    n = len(fulls)
    sm_full = _sds((small.shape[0], small.shape[1] * N_CHIPS), small.dtype)

    def body(*refs):
        sm_ref = refs[n]
        out, sm_out = refs[n + 1:2 * n + 1], refs[2 * n + 1]
        send_sems, recv_sems, sm_send, sm_recv, local_sem = refs[2 * n + 2:]
        x, y, c, chips = _place()
        me = 2 * x + y
        sib = (x, y, 1 - c)
        smc = small.shape[1]

        def sm_block(s):
            return sm_out.at[:, pl.ds(pl.multiple_of(s * smc, LANES), smc)]

        def win(i, s, h):
            return _window(out[i], axes[i], shard_shapes[i], s, h)

        local = pltpu.make_async_copy(sm_ref, sm_block(me), local_sem)
        local.start()
        sends = []
        for j, chip in enumerate(chips):
            cp = _remote(sm_ref, sm_block(me), sm_send.at[j], sm_recv.at[j], (*chip, c))
            cp.start()
            sends.append(cp)
        for i in range(n):
            for j, chip in enumerate(chips):
                cp = _remote(win(i, me, c), win(i, me, c), send_sems.at[i, j], recv_sems.at[i, j], (*chip, c))
                cp.start()
                sends.append(cp)
        for i in range(n):
            for j, chip in enumerate(chips):
                cj = 2 * chip[0] + chip[1]
                _remote(win(i, cj, c), win(i, cj, c), send_sems.at[i, j], recv_sems.at[i, j], (*chip, c)).wait_recv()
                fwd = _remote(win(i, cj, c), win(i, cj, c), send_sems.at[i, 3 + j], recv_sems.at[i, 3 + j], sib)
                fwd.start()
                sends.append(fwd)
        for i in range(n):
            for j, chip in enumerate(chips):
                cj = 2 * chip[0] + chip[1]
                _remote(win(i, cj, 1 - c), win(i, cj, 1 - c), send_sems.at[i, 3 + j], recv_sems.at[i, 3 + j],
                        sib).wait_recv()
        for j, chip in enumerate(chips):
            cj = 2 * chip[0] + chip[1]
            _remote(sm_block(cj), sm_block(cj), sm_send.at[j], sm_recv.at[j], (*chip, c)).wait_recv()
        for cp in sends:
            cp.wait_send()
        local.wait()

    res = pl.pallas_call(
        body, name="all_gather_weights",
        in_specs=[ANY] * (n + 1), out_specs=[ANY] * (n + 1),
        out_shape=[_sds(f.shape, f.dtype) for f in fulls] + [sm_full],
        input_output_aliases={i: i for i in range(n)},
        scratch_shapes=[pltpu.SemaphoreType.DMA((n, 6)), pltpu.SemaphoreType.DMA((n, 6)),
                        pltpu.SemaphoreType.DMA((3,)), pltpu.SemaphoreType.DMA((3,)), pltpu.SemaphoreType.DMA],
    )(*fulls, small)
    return res[:n], res[n]


def _pair_exchange(grads, axes, shard_shapes):
    n = len(grads)

    def body(*refs):
        g, ra = refs[:n], refs[n:2 * n]
        send_sems, recv_sems = refs[2 * n:]
        x, y, c, _ = _place()
        sib = (x, y, 1 - c)
        sends = []

        def pieces(ref, i, h):
            if axes[i] == 1:
                return [_half(ref, h)]
            return [_window(ref, 0, shard_shapes[i], s, h) for s in range(N_CHIPS)]

        for i in range(n):
            for k, (src, dst) in enumerate(zip(pieces(g[i], i, 1 - c), pieces(ra[i], i, 1 - c))):
                cp = _remote(src, dst, send_sems.at[i, k], recv_sems.at[i, k], sib)
                cp.start()
                sends.append(cp)
        for i in range(n):
            for k, dst in enumerate(pieces(ra[i], i, c)):
                _remote(dst, dst, send_sems.at[i, k], recv_sems.at[i, k], sib).wait_recv()
        for cp in sends:
            cp.wait_send()

    return pl.pallas_call(
        body, name="grad_pair_exchange",
        in_specs=[ANY] * n, out_specs=[ANY] * n, out_shape=[_sds(g.shape, g.dtype) for g in grads],
        scratch_shapes=[pltpu.SemaphoreType.DMA((n, N_CHIPS)), pltpu.SemaphoreType.DMA((n, N_CHIPS))],
    )(*grads)


def _pair_add(g, ra, ax, shard_shape, place, name):
    sr, sc = shard_shape
    hr = sr // 2
    wc = sc
    tr = _pick(hr, (256, 352, 128, 64, 32, 16))
    nr = hr // tr

    def body(p_ref, a_ref, b_ref, o_ref):
        o_ref[...] = (a_ref[...].astype(F32) + b_ref[...].astype(F32)).astype(o_ref.dtype)

    if ax == 1:
        src = pl.BlockSpec((tr, wc), lambda s, r, p_ref: (p_ref[0] * nr + r, s))
    else:
        src = pl.BlockSpec((tr, wc), lambda s, r, p_ref: (s * 2 * nr + p_ref[0] * nr + r, 0))
    return pl.pallas_call(
        body, name=name,
        grid_spec=pltpu.PrefetchScalarGridSpec(
            num_scalar_prefetch=1, grid=(N_CHIPS, nr), in_specs=[src, src],
            out_specs=pl.BlockSpec((None, tr, wc), lambda s, r, p_ref: (s, r, 0))),
        out_shape=_sds((N_CHIPS, hr, wc), BF16),
        compiler_params=_cparams("parallel", "parallel"),
    )(place, g, ra)


def _chip_exchange(parts):
    n = len(parts)

    def body(*refs):
        p, rc = refs[:n], refs[n:2 * n]
        send_sems, recv_sems = refs[2 * n:]
        x, y, c, chips = _place()
        me = 2 * x + y
        sends = []
        for i in range(n):
            for j, chip in enumerate(chips):
                cj = 2 * chip[0] + chip[1]
                cp = _remote(p[i].at[cj], rc[i].at[me], send_sems.at[i, j], recv_sems.at[i, j], (*chip, c))
                cp.start()
                sends.append(cp)
        for i in range(n):
            for j, chip in enumerate(chips):
                cj = 2 * chip[0] + chip[1]
                _remote(rc[i].at[cj], rc[i].at[cj], send_sems.at[i, j], recv_sems.at[i, j], (*chip, c)).wait_recv()
        for cp in sends:
            cp.wait_send()

    return pl.pallas_call(
        body, name="grad_chip_exchange",
        in_specs=[ANY] * n, out_specs=[ANY] * n, out_shape=[_sds(p.shape, p.dtype) for p in parts],
        scratch_shapes=[pltpu.SemaphoreType.DMA((n, 3)), pltpu.SemaphoreType.DMA((n, 3))],
    )(*parts)


def _chip_add(part, rc, place, name):
    _, hr, wc = rc.shape
    tr = _pick(hr, (256, 352, 128, 64, 32, 16))
    nr = hr // tr

    def body(p_ref, own_ref, r1_ref, r2_ref, r3_ref, o_ref):
        acc = own_ref[...].astype(F32)
        for r_ref in (r1_ref, r2_ref, r3_ref):
            acc = acc + r_ref[...].astype(F32)
        o_ref[...] = acc

    def slot(k):
        return pl.BlockSpec((None, tr, wc), lambda r, p_ref: ((p_ref[1] + k) % N_CHIPS, r, 0))

    return pl.pallas_call(
        body, name=name,
        grid_spec=pltpu.PrefetchScalarGridSpec(
            num_scalar_prefetch=1, grid=(nr,), in_specs=[slot(0), slot(1), slot(2), slot(3)],
            out_specs=pl.BlockSpec((tr, wc), lambda r, p_ref: (p_ref[0] * nr + r, 0))),
        out_shape=_sds((2 * hr, wc), F32),
        compiler_params=_cparams("parallel"),
    )(place, part, rc, rc, rc)


def _pair_gather(shards):
    n = len(shards)

    def body(*refs):
        out = refs[n:2 * n]
        send_sems, recv_sems = refs[2 * n:]
        x, y, c, _ = _place()
        sib = (x, y, 1 - c)
        sends = [_remote(_half(out[i], c), _half(out[i], c), send_sems.at[i], recv_sems.at[i], sib) for i in range(n)]
        for cp in sends:
            cp.start()
        for i in range(n):
            _remote(_half(out[i], 1 - c), _half(out[i], 1 - c), send_sems.at[i], recv_sems.at[i], sib).wait_recv()
        for cp in sends:
            cp.wait_send()

    return pl.pallas_call(
        body, name="grad_pair_gather",
        in_specs=[ANY] * n, out_specs=[ANY] * n, out_shape=[_sds(s.shape, s.dtype) for s in shards],
        input_output_aliases={i: i for i in range(n)},
        scratch_shapes=[pltpu.SemaphoreType.DMA((n,)), pltpu.SemaphoreType.DMA((n,))],
    )(*shards)


N_DEV = 8


def _all_reduce_small(buf):
    r, cdim = buf.shape

    def body(x_ref, o_ref, land, send_sems, recv_sems):
        x, y, c, _ = _place()
        me = 4 * x + 2 * y + c
        land[me] = x_ref[...]
        sends = []
        for k in range(1, N_DEV):
            kx, ky, kc = (k >> 2) & 1, (k >> 1) & 1, k & 1
            peer = (1 - x if kx else x, 1 - y if ky else y, 1 - c if kc else c)
            cp = _remote(x_ref, land.at[me], send_sems.at[k - 1], recv_sems.at[k - 1], peer)
            cp.start()
            sends.append(cp)
        for k in range(1, N_DEV):
            kx, ky, kc = (k >> 2) & 1, (k >> 1) & 1, k & 1
            peer = (1 - x if kx else x, 1 - y if ky else y, 1 - c if kc else c)
            pidx = 4 * peer[0] + 2 * peer[1] + peer[2]
            _remote(land.at[pidx], land.at[pidx], send_sems.at[k - 1], recv_sems.at[k - 1], peer).wait_recv()
        for cp in sends:
            cp.wait_send()
        acc = land[0]
        for dev in range(1, N_DEV):
            acc = acc + land[dev]
        o_ref[...] = acc

    vm = pl.BlockSpec(memory_space=pltpu.VMEM)
    return pl.pallas_call(
        body, name="all_reduce_small", in_specs=[vm], out_specs=vm, out_shape=_sds((r, cdim), F32),
        scratch_shapes=[pltpu.VMEM((N_DEV, r, cdim), F32), pltpu.SemaphoreType.DMA((N_DEV - 1,)),
                        pltpu.SemaphoreType.DMA((N_DEV - 1,))],
    )(buf)


SMALL_ROWS = 16


def kernel(x, mem, g_mix, w_in, conv_w, attn_sinks, w_attn_proj, w_conv_proj, w_mix_out, g_xattn, g_mem, w_xq, w_xkv, w_xo, g_ffn, w_ffn_in, w_ffn_out, g_final, loss_target, m_g_mix, m_w_in, m_conv_w, m_attn_sinks, m_w_attn_proj, m_w_conv_proj, m_w_mix_out, m_g_xattn, m_g_mem, m_w_xq, m_w_xkv, m_w_xo, m_g_ffn, m_w_ffn_in, m_w_ffn_out, m_g_final, v_g_mix, v_w_in, v_conv_w, v_attn_sinks, v_w_attn_proj, v_w_conv_proj, v_w_mix_out, v_g_xattn, v_g_mem, v_w_xq, v_w_xkv, v_w_xo, v_g_ffn, v_w_ffn_in, v_w_ffn_out, v_g_final):
    w = dict(g_mix=g_mix, w_in=w_in[0], conv_w=conv_w[0], attn_sinks=attn_sinks, w_attn_proj=w_attn_proj[0],
             w_conv_proj=w_conv_proj[0], w_mix_out=w_mix_out[0], g_xattn=g_xattn, g_mem=g_mem, w_xq=w_xq[0],
             w_xkv=w_xkv[0], w_xo=w_xo[0], g_ffn=g_ffn, w_ffn_in=w_ffn_in[0], w_ffn_out=w_ffn_out[0],
             g_final=g_final[None])
    m = dict(g_mix=m_g_mix, w_in=m_w_in[0], conv_w=m_conv_w[0], attn_sinks=m_attn_sinks,
             w_attn_proj=m_w_attn_proj[0], w_conv_proj=m_w_conv_proj[0], w_mix_out=m_w_mix_out[0],
             g_xattn=m_g_xattn, g_mem=m_g_mem, w_xq=m_w_xq[0], w_xkv=m_w_xkv[0], w_xo=m_w_xo[0], g_ffn=m_g_ffn,
             w_ffn_in=m_w_ffn_in[0], w_ffn_out=m_w_ffn_out[0], g_final=m_g_final[None])
    v = dict(g_mix=v_g_mix, w_in=v_w_in[0], conv_w=v_conv_w[0], attn_sinks=v_attn_sinks,
             w_attn_proj=v_w_attn_proj[0], w_conv_proj=v_w_conv_proj[0], w_mix_out=v_w_mix_out[0],
             g_xattn=v_g_xattn, g_mem=v_g_mem, w_xq=v_w_xq[0], w_xkv=v_w_xkv[0], w_xo=v_w_xo[0], g_ffn=v_g_ffn,
             w_ffn_in=v_w_ffn_in[0], w_ffn_out=v_w_ffn_out[0], g_final=v_g_final[None])
    names = [nm for nm, _ in BIG]
    axes = [ax for _, ax in BIG]
    d = x.shape[2]
    cw = w["conv_w"].shape[1] * N_CHIPS
    chip = (2 * lax.axis_index("x") + lax.axis_index("y")).astype(jnp.int32)
    place = jnp.stack([lax.axis_index("c").astype(jnp.int32), chip])
    shard_shapes = [w[nm].shape for nm in names]

    seeded = [_cast_to_full(w[nm], ax, chip.reshape(1), "cast_" + nm) for nm, ax in zip(names, axes)]
    gathered, conv_full = _all_gather(seeded, axes, shard_shapes, w["conv_w"])
    full = dict(zip(names, gathered))

    grad_x, big, small = _local_step(
        x[0], mem[0], loss_target[0], w["g_mix"], conv_full, w["attn_sinks"], w["g_xattn"], w["g_mem"], w["g_ffn"],
        w["g_final"], *[full[nm] for nm in names])

    glist = [big[nm] for nm in names]
    received = _pair_exchange(glist, axes, shard_shapes)
    parts = [_pair_add(g, ra, ax, ss, place, "pair_add_" + nm)
             for g, ra, ax, ss, nm in zip(glist, received, axes, shard_shapes, names)]
    landed = _chip_exchange(parts)
    halves = [_chip_add(p, rc, place, "chip_add_" + nm) for p, rc, nm in zip(parts, landed, names)]
    grads = dict(zip(names, _pair_gather(halves)))

    pw = max(d, cw)

    def row(a):
        return jnp.pad(a, ((0, 0), (0, pw - a.shape[1])))

    gains = ("g_mix", "g_xattn", "g_mem", "g_ffn", "g_final")
    packed = jnp.concatenate(
        [row(small[nm]) for nm in gains] + [row(small["conv_w"]),
         row(jnp.concatenate([small["attn_sinks"], small["loss"]], axis=1)),
         jnp.zeros((SMALL_ROWS - 9, pw), F32)], axis=0)
    total = _all_reduce_small(packed)
    nsink = attn_sinks.shape[1]
    grads.update({nm: total[k:k + 1, :d] for k, nm in enumerate(gains)})
    grads.update(conv_w=lax.dynamic_slice(total, (5, chip * (cw // N_CHIPS)), (3, cw // N_CHIPS)),
                 attn_sinks=total[8:9, :nsink])
    loss = total[8, nsink]

    order = ["g_mix", "w_in", "conv_w", "attn_sinks", "w_attn_proj", "w_conv_proj", "w_mix_out", "g_xattn", "g_mem",
             "w_xq", "w_xkv", "w_xo", "g_ffn", "w_ffn_in", "w_ffn_out", "g_final"]
    upd = {nm: _adamw(w[nm], grads[nm], m[nm], v[nm], "adamw_" + nm) for nm in order}

    stacked = set(names) | {"conv_w"}

    def shaped(nm, a):
        if nm == "g_final":
            return a[0]
        return a[None] if nm in stacked else a

    outs = [loss, grad_x[None]]
    outs += [shaped(nm, grads[nm]) for nm in order]
    for k in range(3):
        outs += [shaped(nm, upd[nm][k]) for nm in order]
    return tuple(outs)
```

```python
import functools

import jax
import jax.numpy as jnp
from jax import lax
from jax.experimental import pallas as pl
from jax.experimental.pallas import tpu as pltpu

F32 = jnp.float32
BF16 = jnp.bfloat16

VMEM_LIMIT_BYTES = 56 * 1024 * 1024
LANES = 128
HEAD_DIM = 64
BLOCK = 128
X_HEAD_DIM = 128
ROPE_THETA = 10000.0
EPS = 1e-6
NEG = -1e30
ADAM_LR, ADAM_B1, ADAM_B2, ADAM_EPS, ADAM_WD, ADAM_STEP = 0.001, 0.9, 0.999, 1e-08, 0.01, 10
N_CHIPS = 4
MESH = pl.DeviceIdType.MESH
ANY = pl.BlockSpec(memory_space=pl.ANY)


def _pick(dim, prefs):
    for p in prefs:
        if dim % p == 0:
            return p
    return dim


def _cparams(*sem):
    return pltpu.CompilerParams(dimension_semantics=sem, vmem_limit_bytes=VMEM_LIMIT_BYTES)


def _sds(shape, dtype):
    return jax.ShapeDtypeStruct(shape, dtype)


def _sigmoid(v):
    return 1.0 / (1.0 + jnp.exp(-v))


def _matmul(a, b, *, mode, out_dtype, name, res=None, tm=None, tn=None, tk=None):
    if mode == "nn":
        (m, k), (k2, n) = a.shape, b.shape
    elif mode == "nt":
        (m, k), (n, k2) = a.shape, b.shape
    else:
        (k, m), (k2, n) = a.shape, b.shape
    assert k == k2, (a.shape, b.shape, mode)
    tm = tm or _pick(m, (1024, 512, 256, 128))
    tn = tn or _pick(n, (512, 256, 128))
    tk = tk or (k if k <= 2048 else _pick(k, (2048, 2176, 2816, 1408, 1024, 512, 256, 128)))
    nk = k // tk
    dims = {"nn": (((1,), (0,)), ((), ())), "nt": (((1,), (1,)), ((), ())), "tn": (((0,), (0,)), ((), ()))}[mode]
    has_res = res is not None

    def body(*refs):
        a_ref, b_ref = refs[0], refs[1]
        r_ref = refs[2] if has_res else None
        o_ref = refs[2 + has_res]
        part = lax.dot_general(a_ref[...], b_ref[...], dims, preferred_element_type=F32)

        def finish(val):
            if has_res:
                val = val + r_ref[...]
            o_ref[...] = val.astype(o_ref.dtype)

        if nk == 1:
            finish(part)
        else:
            acc = refs[3 + has_res]
            kk = pl.program_id(2)

            @pl.when(kk == 0)
            def _():
                acc[...] = part

            @pl.when(kk > 0)
            def _():
                acc[...] += part

            @pl.when(kk == nk - 1)
            def _():
                finish(acc[...])

    if mode == "tn":
        a_spec = pl.BlockSpec((tk, tm), lambda i, j, kk: (kk, i))
    else:
        a_spec = pl.BlockSpec((tm, tk), lambda i, j, kk: (i, kk))
    if mode == "nt":
        b_spec = pl.BlockSpec((tn, tk), lambda i, j, kk: (j, kk))
    else:
        b_spec = pl.BlockSpec((tk, tn), lambda i, j, kk: (kk, j))
    o_spec = pl.BlockSpec((tm, tn), lambda i, j, kk: (i, j))
    return pl.pallas_call(
        body,
        name=name,
        grid=(m // tm, n // tn, nk),
        in_specs=[a_spec, b_spec] + ([o_spec] if has_res else []),
        out_specs=o_spec,
        out_shape=_sds((m, n), out_dtype),
        scratch_shapes=[pltpu.VMEM((tm, tn), F32)] if nk > 1 else [],
        compiler_params=_cparams("parallel", "parallel", "arbitrary"),
    )(*([a, b] + ([res] if has_res else [])))


def _rms_fwd(x, g, name):
    t, d = x.shape
    tm = _pick(t, (512, 256, 128))

    def body(x_ref, g_ref, o_ref):
        xf = x_ref[...]
        r = lax.rsqrt(jnp.mean(xf * xf, axis=-1, keepdims=True) + EPS)
        o_ref[...] = (xf * r * g_ref[...]).astype(o_ref.dtype)

    row = pl.BlockSpec((tm, d), lambda i: (i, 0))
    return pl.pallas_call(
        body, name=name, grid=(t // tm,),
        in_specs=[row, pl.BlockSpec((1, d), lambda i: (0, 0))],
        out_specs=row, out_shape=_sds((t, d), BF16),
        compiler_params=_cparams("parallel"),
    )(x, g)


def _rms_bwd_math(xf, g, du):
    r = lax.rsqrt(jnp.mean(xf * xf, axis=-1, keepdims=True) + EPS)
    xh = xf * r
    gdy = g * du
    dx = r * (gdy - xh * jnp.mean(gdy * xh, axis=-1, keepdims=True))
    dg = jnp.sum(du * xh, axis=0, keepdims=True)
    return dx, dg


def _rms_bwd(x, g, du, dh, name):
    t, d = x.shape
    tm = _pick(t, (256, 128))
    has_dh = dh is not None

    def body(*refs):
        x_ref, g_ref, du_ref = refs[0], refs[1], refs[2]
        o_ref, ob_ref, dg_ref = refs[3 + has_dh:]
        dx, dg = _rms_bwd_math(x_ref[...], g_ref[...], du_ref[...].astype(F32))
        if has_dh:
            dx = dx + refs[3][...]
        o_ref[...] = dx
        ob_ref[...] = dx.astype(BF16)

        @pl.when(pl.program_id(0) == 0)
        def _():
            dg_ref[...] = dg

        @pl.when(pl.program_id(0) > 0)
        def _():
            dg_ref[...] += dg

    row = pl.BlockSpec((tm, d), lambda i: (i, 0))
    vec = pl.BlockSpec((1, d), lambda i: (0, 0))
    return pl.pallas_call(
        body, name=name, grid=(t // tm,),
        in_specs=[row, vec, row] + ([row] if has_dh else []),
        out_specs=[row, row, vec],
        out_shape=[_sds((t, d), F32), _sds((t, d), BF16), _sds((1, d), F32)],
        compiler_params=_cparams("arbitrary"),
    )(*([x, g, du] + ([dh] if has_dh else [])))


def _loss_head(h, g, tgt):
    t, d = h.shape
    tm = _pick(t, (256, 128))

    def body(h_ref, g_ref, t_ref, o_ref, ob_ref, dg_ref, l_ref):
        xf = h_ref[...]
        gv = g_ref[...]
        r = lax.rsqrt(jnp.mean(xf * xf, axis=-1, keepdims=True) + EPS)
        err = xf * r * gv - t_ref[...]
        part = 0.5 * jnp.sum(jnp.mean(err * err, axis=-1, keepdims=True), axis=0, keepdims=True)
        dx, dg = _rms_bwd_math(xf, gv, err * (1.0 / d))
        o_ref[...] = dx
        ob_ref[...] = dx.astype(BF16)
        lrow = jnp.broadcast_to(part, (1, LANES))

        @pl.when(pl.program_id(0) == 0)
        def _():
            dg_ref[...] = dg
            l_ref[...] = lrow

        @pl.when(pl.program_id(0) > 0)
        def _():
            dg_ref[...] += dg
            l_ref[...] += lrow

    row = pl.BlockSpec((tm, d), lambda i: (i, 0))
    vec = pl.BlockSpec((1, d), lambda i: (0, 0))
    return pl.pallas_call(
        body, name="loss_head", grid=(t // tm,),
        in_specs=[row, vec, row],
        out_specs=[row, row, vec, pl.BlockSpec((1, LANES), lambda i: (0, 0))],
        out_shape=[_sds((t, d), F32), _sds((t, d), BF16), _sds((1, d), F32), _sds((1, LANES), F32)],
        compiler_params=_cparams("arbitrary"),
    )(h, g, tgt)


def _rope_tables(t):
    half = HEAD_DIM // 2
    inv_freq = ROPE_THETA ** (-jnp.arange(half, dtype=F32) / half)
    ang = jnp.arange(t, dtype=F32)[:, None] * inv_freq[None, :]
    cos = jnp.cos(ang)
    sin = jnp.sin(ang)
    reps = LANES // HEAD_DIM
    cos_t = jnp.tile(jnp.concatenate([cos, cos], axis=1), (1, reps))
    sin_t = jnp.tile(jnp.concatenate([-sin, sin], axis=1), (1, reps))
    return cos_t, sin_t


def _rope(v, cos, sin):
    w = v.shape[1]
    c = jnp.tile(cos, (1, w // LANES))
    s = jnp.tile(sin, (1, w // LANES))
    lane = lax.broadcasted_iota(jnp.int32, v.shape, 1)
    first = (lane % HEAD_DIM) < (HEAD_DIM // 2)
    partner = jnp.where(first, pltpu.roll(v, w - HEAD_DIM // 2, 1), pltpu.roll(v, HEAD_DIM // 2, 1))
    return v * c + partner * s


def _heads(v, first, count):
    return jnp.concatenate([v[:, (first + i) * HEAD_DIM:(first + i + 1) * HEAD_DIM] for i in range(count)], axis=0)


def _attn_probs(qs, kb, n, h, qpk, sinks_ref):
    s = lax.dot_general(qs, kb, (((1,), (1,)), ((), ())), preferred_element_type=F32) * (HEAD_DIM ** -0.5)
    qi = lax.broadcasted_iota(jnp.int32, (BLOCK, 2 * BLOCK), 0)
    kc = lax.broadcasted_iota(jnp.int32, (BLOCK, 2 * BLOCK), 1)
    valid = (kc > qi) & (kc <= qi + BLOCK) & ((kc >= BLOCK) | (n > 0))
    bias = jnp.tile(jnp.where(valid, 0.0, NEG).astype(F32), (qpk, 1))
    s = s + bias
    rowg = lax.broadcasted_iota(jnp.int32, (qpk * BLOCK, 1), 0) // BLOCK
    sink = jnp.zeros((qpk * BLOCK, 1), F32)
    for g in range(qpk):
        sink = jnp.where(rowg == g, sinks_ref[0, h * qpk + g], sink)
    m = jnp.maximum(jnp.max(s, axis=-1, keepdims=True), sink)
    e = jnp.exp(s - m)
    es = jnp.exp(sink - m)
    inv = 1.0 / (jnp.sum(e, axis=-1, keepdims=True) + es)
    return e * inv, es * inv, rowg


def _attn_specs(aw, kvw):
    koff = aw // kvw
    prev = lambda n: jnp.maximum(n - 1, 0)
    return [
        pl.BlockSpec((BLOCK, aw), lambda n: (n, 0)),
        pl.BlockSpec((BLOCK, kvw), lambda n: (n, koff)),
        pl.BlockSpec((BLOCK, kvw), lambda n: (prev(n), koff)),
        pl.BlockSpec((BLOCK, kvw), lambda n: (n, koff + 1)),
        pl.BlockSpec((BLOCK, kvw), lambda n: (prev(n), koff + 1)),
        pl.BlockSpec((BLOCK, LANES), lambda n: (n, 0)),
        pl.BlockSpec((BLOCK, LANES), lambda n: (n, 0)),
        pl.BlockSpec((BLOCK, LANES), lambda n: (prev(n), 0)),
        pl.BlockSpec((BLOCK, LANES), lambda n: (prev(n), 0)),
        pl.BlockSpec(memory_space=pltpu.SMEM),
    ]


def _attn_fwd(proj, cos, sin, sinks, aw, kvw):
    t = proj.shape[0]
    nkv = kvw // HEAD_DIM
    qpk = aw // kvw

    def body(q_ref, kc_ref, kp_ref, vc_ref, vp_ref, cc_ref, sc_ref, cp_ref, sp_ref, sinks_ref, o_ref):
        n = pl.program_id(0)
        q = _rope(q_ref[...], cc_ref[...], sc_ref[...]).astype(BF16)
        kc = _rope(kc_ref[...], cc_ref[...], sc_ref[...]).astype(BF16)
        kp = _rope(kp_ref[...], cp_ref[...], sp_ref[...]).astype(BF16)
        vc = vc_ref[...].astype(BF16)
        vp = vp_ref[...].astype(BF16)
        outs = []
        for h in range(nkv):
            hs = slice(h * HEAD_DIM, (h + 1) * HEAD_DIM)
            kb = jnp.concatenate([kp[:, hs], kc[:, hs]], axis=0)
            vb = jnp.concatenate([vp[:, hs], vc[:, hs]], axis=0)
            p, _, _ = _attn_probs(_heads(q, h * qpk, qpk), kb, n, h, qpk, sinks_ref)
            o = jnp.dot(p.astype(BF16), vb, preferred_element_type=F32)
            outs += [o[g * BLOCK:(g + 1) * BLOCK] for g in range(qpk)]
        o_ref[...] = jnp.concatenate(outs, axis=1).astype(o_ref.dtype)

    return pl.pallas_call(
        body, name="attn_fwd", grid=(t // BLOCK,),
        in_specs=_attn_specs(aw, kvw),
        out_specs=pl.BlockSpec((BLOCK, aw), lambda n: (n, 0)),
        out_shape=_sds((t, aw), BF16),
        compiler_params=_cparams("parallel"),
    )(proj, proj, proj, proj, proj, cos, sin, cos, sin, sinks)


def _attn_bwd(proj, cos, sin, sinks, do, aw, kvw):
    t = proj.shape[0]
    nkv = kvw // HEAD_DIM
    qpk = aw // kvw
    scale = HEAD_DIM ** -0.5

    def body(q_ref, kc_ref, kp_ref, vc_ref, vp_ref, cc_ref, sc_ref, cp_ref, sp_ref, sinks_ref, do_ref,
             dq_ref, dkc_ref, dkp_ref, dvc_ref, dvp_ref, ds_ref):
        n = pl.program_id(0)
        cc, sc, cp, sp = cc_ref[...], sc_ref[...], cp_ref[...], sp_ref[...]
        q = _rope(q_ref[...], cc, sc).astype(BF16)
        kc = _rope(kc_ref[...], cc, sc).astype(BF16)
        kp = _rope(kp_ref[...], cp, sp).astype(BF16)
        vc = vc_ref[...].astype(BF16)
        vp = vp_ref[...].astype(BF16)
        dout = do_ref[...]
        dqs, dks, dvs = [], [], []
        lane = lax.broadcasted_iota(jnp.int32, (8, LANES), 1)
        row0 = lax.broadcasted_iota(jnp.int32, (8, LANES), 0) == 0
        dsink = jnp.zeros((8, LANES), F32)
        for h in range(nkv):
            hs = slice(h * HEAD_DIM, (h + 1) * HEAD_DIM)
            kb = jnp.concatenate([kp[:, hs], kc[:, hs]], axis=0)
            vb = jnp.concatenate([vp[:, hs], vc[:, hs]], axis=0)
            qs = _heads(q, h * qpk, qpk)
            dos = _heads(dout, h * qpk, qpk)
            p, psink, rowg = _attn_probs(qs, kb, n, h, qpk, sinks_ref)
            pb = p.astype(BF16)
            o = jnp.dot(pb, vb, preferred_element_type=F32)
            delta = jnp.sum(dos.astype(F32) * o, axis=-1, keepdims=True)
            dvs.append(lax.dot_general(pb, dos, (((0,), (0,)), ((), ())), preferred_element_type=F32))
            dp = lax.dot_general(dos, vb, (((1,), (1,)), ((), ())), preferred_element_type=F32)
            dsc = (p * (dp - delta)).astype(BF16)
            dq_h = jnp.dot(dsc, kb, preferred_element_type=F32) * scale
            dqs += [dq_h[g * BLOCK:(g + 1) * BLOCK] for g in range(qpk)]
            dks.append(lax.dot_general(dsc, qs, (((0,), (0,)), ((), ())), preferred_element_type=F32) * scale)
            sink_term = psink * delta
            for g in range(qpk):
                val = -jnp.sum(jnp.where(rowg == g, sink_term, 0.0))
                dsink = jnp.where(row0 & (lane == h * qpk + g), val, dsink)
        dq = jnp.concatenate(dqs, axis=1)
        dq_ref[...] = _rope(dq, cc, -sc).astype(dq_ref.dtype)
        dk = jnp.concatenate(dks, axis=1)
        dv = jnp.concatenate(dvs, axis=1)
        dkp_ref[...] = _rope(dk[:BLOCK], cp, -sp)
        dkc_ref[...] = _rope(dk[BLOCK:], cc, -sc)
        dvp_ref[...] = dv[:BLOCK]
        dvc_ref[...] = dv[BLOCK:]

        @pl.when(n == 0)
        def _():
            ds_ref[...] = dsink

        @pl.when(n > 0)
        def _():
            ds_ref[...] += dsink

    kv_spec = pl.BlockSpec((BLOCK, kvw), lambda n: (n, 0))
    return pl.pallas_call(
        body, name="attn_bwd", grid=(t // BLOCK,),
        in_specs=_attn_specs(aw, kvw) + [pl.BlockSpec((BLOCK, aw), lambda n: (n, 0))],
        out_specs=[pl.BlockSpec((BLOCK, aw), lambda n: (n, 0)), kv_spec, kv_spec, kv_spec, kv_spec,
                   pl.BlockSpec((8, LANES), lambda n: (0, 0))],
        out_shape=[_sds((t, aw), BF16)] + [_sds((t, kvw), F32)] * 4 + [_sds((8, LANES), F32)],
        compiler_params=_cparams("arbitrary"),
    )(proj, proj, proj, proj, proj, cos, sin, cos, sin, sinks, do)


def _shift_down(v, k, halo):
    rows = lax.broadcasted_iota(jnp.int32, v.shape, 0)
    out = pltpu.roll(v, k, 0)
    for r in range(k):
        out = jnp.where(rows == r, halo[8 - k + r:8 - k + r + 1, :], out)
    return out


def _shift_up(v, k, halo):
    tm = v.shape[0]
    rows = lax.broadcasted_iota(jnp.int32, v.shape, 0)
    out = pltpu.roll(v, tm - k, 0)
    for r in range(k):
        out = jnp.where(rows == tm - k + r, halo[r:r + 1, :], out)
    return out


def _conv_fwd(proj, conv_w, zoff, cw, cb):
    t = proj.shape[0]
    tm = _pick(t, (512, 256, 128))
    zb, nb = zoff // cb, cw // cb
    h8 = tm // 8

    def body(z_ref, gb_ref, gc_ref, zp_ref, gcp_ref, w_ref, o_ref):
        i = pl.program_id(0)
        cz = gc_ref[...] * z_ref[...]
        czp = gcp_ref[...] * zp_ref[...] * (i > 0).astype(F32)
        w = w_ref[...]
        y = w[0:1] * _shift_down(cz, 2, czp) + w[1:2] * _shift_down(cz, 1, czp) + w[2:3] * cz
        o_ref[...] = (gb_ref[...] * y).astype(o_ref.dtype)

    def col(k):
        return pl.BlockSpec((tm, cb), lambda i, j: (i, zb + k * nb + j))

    def halo(k):
        return pl.BlockSpec((8, cb), lambda i, j: (jnp.maximum(i * h8 - 1, 0), zb + k * nb + j))

    return pl.pallas_call(
        body, name="conv_fwd", grid=(t // tm, nb),
        in_specs=[col(0), col(1), col(2), halo(0), halo(2), pl.BlockSpec((3, cb), lambda i, j: (0, j))],
        out_specs=pl.BlockSpec((tm, cb), lambda i, j: (i, j)),
        out_shape=_sds((t, cw), BF16),
        compiler_params=_cparams("parallel", "parallel"),
    )(proj, proj, proj, proj, proj, conv_w)


def _conv_bwd(proj, conv_w, dco, zoff, cw, cb):
    t = proj.shape[0]
    tm = _pick(t, (512, 256, 128))
    zb, nb = zoff // cb, cw // cb
    h8 = tm // 8
    nt = t // tm

    def body(z_ref, gb_ref, gc_ref, zp_ref, gcp_ref, gbn_ref, w_ref, d_ref, dn_ref, dz_ref, dgb_ref, dgc_ref, dw_ref):
        i = pl.program_id(1)
        z, gb, gc = z_ref[...], gb_ref[...], gc_ref[...]
        d = d_ref[...].astype(F32)
        cz = gc * z
        czp = gcp_ref[...] * zp_ref[...] * (i > 0).astype(F32)
        w = w_ref[...]
        cz1 = _shift_down(cz, 1, czp)
        cz2 = _shift_down(cz, 2, czp)
        y = w[0:1] * cz2 + w[1:2] * cz1 + w[2:3] * cz
        dgb_ref[...] = (d * y).astype(dgb_ref.dtype)
        dy = d * gb
        dyn = dn_ref[...].astype(F32) * gbn_ref[...] * (i < nt - 1).astype(F32)
        dcz = w[2:3] * dy + w[1:2] * _shift_up(dy, 1, dyn) + w[0:1] * _shift_up(dy, 2, dyn)
        dgc_ref[...] = (dcz * z).astype(dgc_ref.dtype)
        dz_ref[...] = (dcz * gc).astype(dz_ref.dtype)
        rows = lax.broadcasted_iota(jnp.int32, (8, cb), 0)
        dw = jnp.zeros((8, cb), F32)
        for r, tap in enumerate((cz2, cz1, cz)):
            dw = jnp.where(rows == r, jnp.sum(dy * tap, axis=0, keepdims=True), dw)

        @pl.when(i == 0)
        def _():
            dw_ref[...] = dw

        @pl.when(i > 0)
        def _():
            dw_ref[...] += dw

    def col(k):
        return pl.BlockSpec((tm, cb), lambda j, i: (i, zb + k * nb + j))

    def halo_prev(k):
        return pl.BlockSpec((8, cb), lambda j, i: (jnp.maximum(i * h8 - 1, 0), zb + k * nb + j))

    own = pl.BlockSpec((tm, cb), lambda j, i: (i, j))
    nxt = lambda i: jnp.minimum((i + 1) * h8, t // 8 - 1)
    return pl.pallas_call(
        body, name="conv_bwd", grid=(nb, nt),
        in_specs=[col(0), col(1), col(2), halo_prev(0), halo_prev(2),
                  pl.BlockSpec((8, cb), lambda j, i: (nxt(i), zb + nb + j)),
                  pl.BlockSpec((3, cb), lambda j, i: (0, j)), own,
                  pl.BlockSpec((8, cb), lambda j, i: (nxt(i), j))],
        out_specs=[own, own, own, pl.BlockSpec((8, cb), lambda j, i: (0, j))],
        out_shape=[_sds((t, cw), BF16)] * 3 + [_sds((8, cw), F32)],
        compiler_params=_cparams("parallel", "arbitrary"),
    )(proj, proj, proj, proj, proj, proj, conv_w, dco, dco)


def _merge_fwd(proj, ya, yc, goff, cb):
    t, d = ya.shape
    tm = _pick(t, (512, 256, 128))
    gb_, nb = goff // cb, d // cb

    def body(ga_ref, gc_ref, ya_ref, yc_ref, o_ref):
        o_ref[...] = (_sigmoid(ga_ref[...]) * ya_ref[...] + _sigmoid(gc_ref[...]) * yc_ref[...]).astype(o_ref.dtype)

    own = pl.BlockSpec((tm, cb), lambda i, j: (i, j))
    return pl.pallas_call(
        body, name="merge_fwd", grid=(t // tm, nb),
        in_specs=[pl.BlockSpec((tm, cb), lambda i, j: (i, gb_ + j)),
                  pl.BlockSpec((tm, cb), lambda i, j: (i, gb_ + nb + j)), own, own],
        out_specs=own, out_shape=_sds((t, d), BF16),
        compiler_params=_cparams("parallel", "parallel"),
    )(proj, proj, ya, yc)


def _merge_bwd(proj, ya, yc, dm, goff, cb):
    t, d = ya.shape
    tm = _pick(t, (512, 256, 128))
    gb_, nb = goff // cb, d // cb

    def body(ga_ref, gc_ref, ya_ref, yc_ref, dm_ref, dya_ref, dyc_ref, dga_ref, dgc_ref):
        dmv = dm_ref[...].astype(F32)
        sa = _sigmoid(ga_ref[...])
        sc = _sigmoid(gc_ref[...])
        dya_ref[...] = (dmv * sa).astype(BF16)
        dyc_ref[...] = (dmv * sc).astype(BF16)
        dga_ref[...] = (dmv * ya_ref[...] * sa * (1.0 - sa)).astype(BF16)
        dgc_ref[...] = (dmv * yc_ref[...] * sc * (1.0 - sc)).astype(BF16)

    own = pl.BlockSpec((tm, cb), lambda i, j: (i, j))
    return pl.pallas_call(
        body, name="merge_bwd", grid=(t // tm, nb),
        in_specs=[pl.BlockSpec((tm, cb), lambda i, j: (i, gb_ + j)),
                  pl.BlockSpec((tm, cb), lambda i, j: (i, gb_ + nb + j)), own, own, own],
        out_specs=[own] * 4, out_shape=[_sds((t, d), BF16)] * 4,
        compiler_params=_cparams("parallel", "parallel"),
    )(proj, proj, ya, yc, dm)


def _assemble_dproj(dq, dkc, dkp, dvc, dvp, dz, dgb, dgc, dga, dgg):
    t, aw = dq.shape
    kvw, cw, d = dkc.shape[1], dz.shape[1], dga.shape[1]
    nblk = t // BLOCK
    width = aw + 2 * kvw + 3 * cw + 2 * d

    def body(dq_ref, dkc_ref, dkp_ref, dvc_ref, dvp_ref, dz_ref, dgb_ref, dgc_ref, dga_ref, dgg_ref, o_ref):
        keep = (pl.program_id(0) < nblk - 1).astype(F32)
        dk = dkc_ref[...] + dkp_ref[...] * keep
        dv = dvc_ref[...] + dvp_ref[...] * keep
        o_ref[...] = jnp.concatenate(
            [dq_ref[...], dk.astype(BF16), dv.astype(BF16), dz_ref[...], dgb_ref[...], dgc_ref[...],
             dga_ref[...], dgg_ref[...]], axis=1)

    def cur(w):
        return pl.BlockSpec((BLOCK, w), lambda n: (n, 0))

    def nxt(w):
        return pl.BlockSpec((BLOCK, w), lambda n: (jnp.minimum(n + 1, nblk - 1), 0))

    return pl.pallas_call(
        body, name="assemble_dproj", grid=(nblk,),
        in_specs=[cur(aw), cur(kvw), nxt(kvw), cur(kvw), nxt(kvw), cur(cw), cur(cw), cur(cw), cur(d), cur(d)],
        out_specs=cur(width), out_shape=_sds((t, width), BF16),
        compiler_params=_cparams("parallel"),
    )(dq, dkc, dkp, dvc, dvp, dz, dgb, dgc, dga, dgg)


def _xattn_probs(qh, kh):
    s = lax.dot_general(qh, kh, (((1,), (1,)), ((), ())), preferred_element_type=F32) * (X_HEAD_DIM ** -0.5)
    e = jnp.exp(s - jnp.max(s, axis=-1, keepdims=True))
    return e * (1.0 / jnp.sum(e, axis=-1, keepdims=True))


def _xattn_fwd(xq, kv):
    t, xw = xq.shape
    mt = kv.shape[0]
    tm = _pick(t, (512, 256, 128))

    def body(q_ref, kv_ref, o_ref):
        outs = []
        for hd in range(xw // X_HEAD_DIM):
            hs = slice(hd * X_HEAD_DIM, (hd + 1) * X_HEAD_DIM)
            vs = slice(xw + hd * X_HEAD_DIM, xw + (hd + 1) * X_HEAD_DIM)
            p = _xattn_probs(q_ref[:, hs], kv_ref[:, hs])
            outs.append(jnp.dot(p.astype(BF16), kv_ref[:, vs], preferred_element_type=F32))
        o_ref[...] = jnp.concatenate(outs, axis=1).astype(o_ref.dtype)

    return pl.pallas_call(
        body, name="xattn_fwd", grid=(t // tm,),
        in_specs=[pl.BlockSpec((tm, xw), lambda i: (i, 0)), pl.BlockSpec((mt, 2 * xw), lambda i: (0, 0))],
        out_specs=pl.BlockSpec((tm, xw), lambda i: (i, 0)), out_shape=_sds((t, xw), BF16),
        compiler_params=_cparams("parallel"),
    )(xq, kv)


def _xattn_bwd(xq, kv, do):
    t, xw = xq.shape
    mt = kv.shape[0]
    tm = _pick(t, (512, 256, 128))
    scale = X_HEAD_DIM ** -0.5

    def body(q_ref, kv_ref, do_ref, dq_ref, dkv_ref):
        dqs, dks, dvs = [], [], []
        for hd in range(xw // X_HEAD_DIM):
            hs = slice(hd * X_HEAD_DIM, (hd + 1) * X_HEAD_DIM)
            vs = slice(xw + hd * X_HEAD_DIM, xw + (hd + 1) * X_HEAD_DIM)
            qh, kh, vh, doh = q_ref[:, hs], kv_ref[:, hs], kv_ref[:, vs], do_ref[:, hs]
            p = _xattn_probs(qh, kh)
            pb = p.astype(BF16)
            o = jnp.dot(pb, vh, preferred_element_type=F32)
            delta = jnp.sum(doh.astype(F32) * o, axis=-1, keepdims=True)
            dvs.append(lax.dot_general(pb, doh, (((0,), (0,)), ((), ())), preferred_element_type=F32))
            dp = lax.dot_general(doh, vh, (((1,), (1,)), ((), ())), preferred_element_type=F32)
            dsc = (p * (dp - delta)).astype(BF16)
            dqs.append(jnp.dot(dsc, kh, preferred_element_type=F32) * scale)
            dks.append(lax.dot_general(dsc, qh, (((0,), (0,)), ((), ())), preferred_element_type=F32) * scale)
        dq_ref[...] = jnp.concatenate(dqs, axis=1).astype(dq_ref.dtype)
        dkv = jnp.concatenate(dks + dvs, axis=1)

        @pl.when(pl.program_id(0) == 0)
        def _():
            dkv_ref[...] = dkv

        @pl.when(pl.program_id(0) > 0)
        def _():
            dkv_ref[...] += dkv

    row = pl.BlockSpec((tm, xw), lambda i: (i, 0))
    whole = pl.BlockSpec((mt, 2 * xw), lambda i: (0, 0))
    return pl.pallas_call(
        body, name="xattn_bwd", grid=(t // tm,),
        in_specs=[row, whole, row], out_specs=[row, whole],
        out_shape=[_sds((t, xw), BF16), _sds((mt, 2 * xw), F32)],
        compiler_params=_cparams("arbitrary"),
    )(xq, kv, do)


def _swiglu_fwd(hid):
    t, f2 = hid.shape
    f = f2 // 2
    tm = _pick(t, (256, 128))

    def body(h_ref, o_ref):
        a = h_ref[:, :f].astype(F32)
        b = h_ref[:, f:].astype(F32)
        o_ref[...] = (a * _sigmoid(a) * b).astype(o_ref.dtype)

    return pl.pallas_call(
        body, name="swiglu_fwd", grid=(t // tm,),
        in_specs=[pl.BlockSpec((tm, f2), lambda i: (i, 0))],
        out_specs=pl.BlockSpec((tm, f), lambda i: (i, 0)), out_shape=_sds((t, f), BF16),
        compiler_params=_cparams("parallel"),
    )(hid)


def _swiglu_bwd(hid, dact):
    t, f2 = hid.shape
    f = f2 // 2
    tm = _pick(t, (128,))

    def body(h_ref, d_ref, o_ref):
        a = h_ref[:, :f].astype(F32)
        b = h_ref[:, f:].astype(F32)
        d = d_ref[...].astype(F32)
        sg = _sigmoid(a)
        o_ref[:, :f] = (d * b * sg * (1.0 + a * (1.0 - sg))).astype(o_ref.dtype)
        o_ref[:, f:] = (d * a * sg).astype(o_ref.dtype)

    return pl.pallas_call(
        body, name="swiglu_bwd", grid=(t // tm,),
        in_specs=[pl.BlockSpec((tm, f2), lambda i: (i, 0)), pl.BlockSpec((tm, f), lambda i: (i, 0))],
        out_specs=pl.BlockSpec((tm, f2), lambda i: (i, 0)), out_shape=_sds((t, f2), BF16),
        compiler_params=_cparams("parallel"),
    )(hid, dact)


def _adamw(w, g, m, v, name):
    r, c = w.shape
    tr = _pick(r, (256, 128, 64, 32, 16, 8)) if r * c > 65536 else r

    def body(w_ref, g_ref, m_ref, v_ref, d_ref, nm_ref, nv_ref):
        gv = g_ref[...]
        m2 = ADAM_B1 * m_ref[...] + (1.0 - ADAM_B1) * gv
        v2 = ADAM_B2 * v_ref[...] + (1.0 - ADAM_B2) * (gv * gv)
        m_hat = m2 / (1.0 - ADAM_B1 ** ADAM_STEP)
        v_hat = v2 / (1.0 - ADAM_B2 ** ADAM_STEP)
        d_ref[...] = -ADAM_LR * (m_hat / (jnp.sqrt(v_hat) + ADAM_EPS) + ADAM_WD * w_ref[...])
        nm_ref[...] = m2
        nv_ref[...] = v2

    blk = pl.BlockSpec((tr, c), lambda i: (i, 0))
    return pl.pallas_call(
        body, name=name, grid=(r // tr,),
        in_specs=[blk] * 4, out_specs=[blk] * 3, out_shape=[_sds((r, c), F32)] * 3,
        compiler_params=_cparams("parallel"),
    )(w, g, m, v)


class _Weights:
    def __init__(self, full):
        self.full = full
        self.grads = {}

    def get(self, name):
        return self.full[name]

    def mark(self, tag, value):
        return value

    def grad(self, name, g):
        self.grads[name] = g


def _local_step(x, mem, tgt, g_mix, sinks, g_xattn, g_mem, g_ffn, g_final, dims, wts):
    t, d = x.shape
    aw, cw, kvw = dims
    cb = 2 * kvw
    zoff = aw + 2 * kvw
    goff = zoff + 3 * cw
    cos, sin = _rope_tables(t)
    mm = _matmul
    mark, get = wts.mark, wts.get

    u1 = mark("u1", _rms_fwd(x, g_mix, "rms_mix"))
    proj = mark("proj", mm(u1, get("w_in"), mode="nn", out_dtype=F32, name="mm_proj"))
    attn_o = mark("attn_o", _attn_fwd(proj, cos, sin, sinks, aw, kvw))
    conv_o = mark("conv_o", _conv_fwd(proj, get("conv_w"), zoff, cw, cb))
    ya = mark("ya", mm(attn_o, get("w_attn_proj"), mode="nn", out_dtype=F32, name="mm_yattn"))
    yc = mark("yc", mm(conv_o, get("w_conv_proj"), mode="nn", out_dtype=F32, name="mm_yconv"))
    merged = mark("merged", _merge_fwd(proj, ya, yc, goff, cb))
    h1 = mark("h1", mm(merged, get("w_mix_out"), mode="nn", out_dtype=F32, name="mm_mix", res=x))
    u2 = mark("u2", _rms_fwd(h1, g_xattn, "rms_xattn"))
    mem_n = _rms_fwd(mem, g_mem, "rms_mem")
    xq = mark("xq", mm(u2, get("w_xq"), mode="nn", out_dtype=BF16, name="mm_xq"))
    kv = mm(mem_n, get("w_xkv"), mode="nn", out_dtype=BF16, name="mm_xkv")
    xo = mark("xo", _xattn_fwd(xq, kv))
    h2 = mark("h2", mm(xo, get("w_xo"), mode="nn", out_dtype=F32, name="mm_xo", res=h1))
    u3 = mark("u3", _rms_fwd(h2, g_ffn, "rms_ffn"))
    hid = mark("hid", mm(u3, get("w_ffn_in"), mode="nn", out_dtype=BF16, name="mm_ffn_in"))
    act = mark("act", _swiglu_fwd(hid))
    h3 = mm(act, get("w_ffn_out"), mode="nn", out_dtype=F32, name="mm_ffn_out", res=h2)
    dh3, dh3b, dg_final, loss = _loss_head(h3, g_final, tgt)
    mark("dh3", dh3b)

    wts.grad("w_ffn_out", mm(act, dh3b, mode="tn", out_dtype=BF16, name="mm_dw_ffn_out"))
    dact = mark("dact", mm(dh3b, get("w_ffn_out"), mode="nt", out_dtype=BF16, name="mm_dact"))
    dhid = mark("dhid", _swiglu_bwd(hid, dact))
    wts.grad("w_ffn_in", mm(u3, dhid, mode="tn", out_dtype=BF16, name="mm_dw_ffn_in"))
    du3 = mark("du3", mm(dhid, get("w_ffn_in"), mode="nt", out_dtype=F32, name="mm_du3"))
    dh2, dh2b, dg_ffn = _rms_bwd(h2, g_ffn, du3, dh3, "rms_bwd_ffn")
    mark("dh2", dh2b)
    wts.grad("w_xo", mm(xo, dh2b, mode="tn", out_dtype=BF16, name="mm_dw_xo"))
    dxo = mm(dh2b, get("w_xo"), mode="nt", out_dtype=BF16, name="mm_dxo")
    dxq, dkv = _xattn_bwd(xq, kv, dxo)
    dkvb = dkv.astype(BF16)
    wts.grad("w_xq", mm(u2, dxq, mode="tn", out_dtype=BF16, name="mm_dw_xq"))
    du2 = mm(dxq, get("w_xq"), mode="nt", out_dtype=F32, name="mm_du2")
    wts.grad("w_xkv", mm(mem_n, dkvb, mode="tn", out_dtype=BF16, name="mm_dw_xkv"))
    dmem_n = mm(dkvb, get("w_xkv"), mode="nt", out_dtype=F32, name="mm_dmem")
    _, _, dg_mem = _rms_bwd(mem, g_mem, dmem_n, None, "rms_bwd_mem")
    dh1, dh1b, dg_xattn = _rms_bwd(h1, g_xattn, du2, dh2, "rms_bwd_xattn")
    mark("dh1", dh1b)
    wts.grad("w_mix_out", mm(merged, dh1b, mode="tn", out_dtype=BF16, name="mm_dw_mix"))
    dmerged = mm(dh1b, get("w_mix_out"), mode="nt", out_dtype=BF16, name="mm_dmerged")
    dya, dyc, dga, dgg = _merge_bwd(proj, ya, yc, dmerged, goff, cb)
    mark("dya", dya)
    wts.grad("w_attn_proj", mm(attn_o, dya, mode="tn", out_dtype=BF16, name="mm_dw_attn_proj"))
    dattn_o = mm(dya, get("w_attn_proj"), mode="nt", out_dtype=BF16, name="mm_dattn")
    wts.grad("w_conv_proj", mm(conv_o, dyc, mode="tn", out_dtype=BF16, name="mm_dw_conv_proj"))
    dconv_o = mark("dconv_o", mm(dyc, get("w_conv_proj"), mode="nt", out_dtype=BF16, name="mm_dconv"))
    dz, dgb, dgc, dconv_w = _conv_bwd(proj, get("conv_w"), dconv_o, zoff, cw, cb)
    mark("dz", dz)
    dq, dkc, dkp, dvc, dvp, dsinks = _attn_bwd(proj, cos, sin, sinks, dattn_o, aw, kvw)
    mark("dq", dq)
    dproj = mark("dproj", _assemble_dproj(dq, dkc, dkp, dvc, dvp, dz, dgb, dgc, dga, dgg))
    wts.grad("w_in", mm(u1, dproj, mode="tn", out_dtype=BF16, name="mm_dw_in"))
    du1 = mark("du1", mm(dproj, get("w_in"), mode="nt", out_dtype=F32, name="mm_du1"))
    grad_x, _, dg_mix = _rms_bwd(x, g_mix, du1, dh1, "rms_bwd_mix")
    mark("grad_x", grad_x)

    small = dict(g_mix=dg_mix, g_xattn=dg_xattn, g_mem=dg_mem, g_ffn=dg_ffn, g_final=dg_final,
                 conv_w=dconv_w[:3], attn_sinks=dsinks[0:1, :sinks.shape[1]], loss=loss[0:1, 0:1])
    return grad_x, small


BIG = (("w_in", 1), ("w_attn_proj", 1), ("w_conv_proj", 1), ("w_mix_out", 0), ("w_xq", 0), ("w_xkv", 0),
       ("w_xo", 1), ("w_ffn_in", 1), ("w_ffn_out", 0))


def _place():
    x, y, c = lax.axis_index("x"), lax.axis_index("y"), lax.axis_index("c")
    chips = [(1 - x, y), (x, 1 - y), (1 - x, 1 - y)]
    return x, y, c, chips


def _window(ref, ax, shard_shape, s, h):
    sr, sc = shard_shape
    hr = sr // 2
    if ax == 1:
        return ref.at[pl.ds(pl.multiple_of(h * hr, 16), hr), pl.ds(pl.multiple_of(s * sc, LANES), sc)]
    return ref.at[pl.ds(pl.multiple_of(s * sr + h * hr, 16), hr), :]


def _half(ref, h):
    hr = ref.shape[0] // 2
    return ref.at[pl.ds(pl.multiple_of(h * hr, 16), hr), :]


def _remote(src, dst, send_sem, recv_sem, dev):
    return pltpu.make_async_remote_copy(src_ref=src, dst_ref=dst, send_sem=send_sem, recv_sem=recv_sem,
                                        device_id=dev, device_id_type=MESH)


def _cast_to_full(shard, ax, me, name, dtype=BF16):
    sr, sc = shard.shape
    tr = _pick(sr, (256, 352, 128, 64, 32, 16))
    nr = sr // tr
    full = (sr * N_CHIPS, sc) if ax == 0 else (sr, sc * N_CHIPS)

    def body(me_ref, s_ref, o_ref):
        o_ref[...] = s_ref[...].astype(o_ref.dtype)

    if ax == 1:
        out_spec = pl.BlockSpec((tr, sc), lambda r, me_ref: (r, me_ref[0]))
    else:
        out_spec = pl.BlockSpec((tr, sc), lambda r, me_ref: (me_ref[0] * nr + r, 0))
    return pl.pallas_call(
        body, name=name,
        grid_spec=pltpu.PrefetchScalarGridSpec(
            num_scalar_prefetch=1, grid=(nr,), in_specs=[pl.BlockSpec((tr, sc), lambda r, me_ref: (r, 0))],
            out_specs=out_spec),
        out_shape=_sds(full, dtype),
        compiler_params=_cparams("parallel"),
    )(me, shard)


HBM = pl.BlockSpec(memory_space=pltpu.HBM)
SEM = pl.BlockSpec(memory_space=pltpu.SEMAPHORE)
EFFECT = pltpu.SideEffectType.DATAFLOW_SIDE_EFFECTING


def _in_hbm(a):
    return pltpu.with_memory_space_constraint(a, pltpu.HBM)


def _gather_window(ref, ax, shard_shape, s, h):
    if h is not None:
        return _window(ref, ax, shard_shape, s, h)
    sr, sc = shard_shape
    if ax == 1:
        return ref.at[:, pl.ds(pl.multiple_of(s * sc, LANES), sc)]
    return ref.at[pl.ds(pl.multiple_of(s * sr, 8), sr), :]


def _ag_start(fulls, axes, shard_shapes, whole):
    n = len(fulls)

    def body(*refs):
        src = refs[:n]
        send_sems, recv_sems = refs[n], refs[n + 1]
        token = refs[2 * n + 2]
        x, y, c, chips = _place()
        me = 2 * x + y
        for i in range(n):
            h = None if i in whole else c
            for j, chip in enumerate(chips):
                blk = _gather_window(src[i], axes[i], shard_shapes[i], me, h)
                _remote(blk, blk, send_sems.at[3 * i + j], recv_sems.at[3 * i + j], (*chip, c)).start()
        token[...] = jnp.zeros_like(token)

    res = pl.pallas_call(
        body, name="ag_start",
        out_shape=(pltpu.SemaphoreType.DMA((3 * n,)), pltpu.SemaphoreType.DMA((3 * n,)),
                   *[pltpu.HBM(f.shape, f.dtype) for f in fulls], _sds((8, LANES), F32)),
        in_specs=[HBM] * n, out_specs=(SEM, SEM, *[HBM] * n, pl.BlockSpec(memory_space=pltpu.VMEM)),
        input_output_aliases={i: 2 + i for i in range(n)},
        compiler_params=pltpu.CompilerParams(has_side_effects=EFFECT),
    )(*[_in_hbm(f) for f in fulls])
    return res[0], res[1], list(res[2:2 + n]), res[2 + n]


def _ag_mid(bufs, idxs, axes, shard_shapes, whole, send_sems, recv_sems, after, name):
    ng = len(bufs)

    def body(*refs):
        src = refs[:ng]
        s_in, r_in = refs[ng], refs[ng + 1]
        fsend, frecv = refs[ng + 3], refs[ng + 4]
        x, y, c, chips = _place()
        me = 2 * x + y
        sib = (x, y, 1 - c)
        for k, i in enumerate(idxs):
            h = None if i in whole else c
            for j, chip in enumerate(chips):
                cj = 2 * chip[0] + chip[1]
                mine = _gather_window(src[k], axes[i], shard_shapes[i], me, h)
                theirs = _gather_window(src[k], axes[i], shard_shapes[i], cj, h)
                _remote(theirs, theirs, s_in.at[3 * i + j], r_in.at[3 * i + j], (*chip, c)).wait_recv()
                _remote(mine, mine, s_in.at[3 * i + j], r_in.at[3 * i + j], (*chip, c)).wait_send()
                if i not in whole:
                    _remote(theirs, theirs, fsend.at[3 * k + j], frecv.at[3 * k + j], sib).start()

    res = pl.pallas_call(
        body, name=name,
        out_shape=(pltpu.SemaphoreType.DMA((3 * ng,)), pltpu.SemaphoreType.DMA((3 * ng,)),
                   *[pltpu.HBM(b.shape, b.dtype) for b in bufs]),
        in_specs=[HBM] * ng + [SEM, SEM, ANY], out_specs=(SEM, SEM, *[HBM] * ng),
        input_output_aliases={k: 2 + k for k in range(ng)},
        compiler_params=pltpu.CompilerParams(has_side_effects=EFFECT),
    )(*bufs, send_sems, recv_sems, after)
    return res[0], res[1], list(res[2:])


def _ag_wait(bufs, idxs, axes, shard_shapes, whole, fsend, frecv, after, name):
    ng = len(bufs)

    def body(*refs):
        src = refs[:ng]
        s_in, r_in = refs[ng], refs[ng + 1]
        x, y, c, chips = _place()
        sib = (x, y, 1 - c)
        for k, i in enumerate(idxs):
            if i in whole:
                continue
            for j, chip in enumerate(chips):
                cj = 2 * chip[0] + chip[1]
                sent = _gather_window(src[k], axes[i], shard_shapes[i], cj, c)
                landed = _gather_window(src[k], axes[i], shard_shapes[i], cj, 1 - c)
                _remote(landed, landed, s_in.at[3 * k + j], r_in.at[3 * k + j], sib).wait_recv()
                _remote(sent, sent, s_in.at[3 * k + j], r_in.at[3 * k + j], sib).wait_send()

    res = pl.pallas_call(
        body, name=name,
        out_shape=tuple(pltpu.HBM(b.shape, b.dtype) for b in bufs),
        in_specs=[HBM] * ng + [SEM, SEM, ANY], out_specs=tuple([HBM] * ng),
        input_output_aliases={k: k for k in range(ng)},
        compiler_params=pltpu.CompilerParams(has_side_effects=EFFECT),
    )(*bufs, fsend, frecv, after)
    return list(res)


class _Gather:
    GROUPS = ((("w_in", "conv_w"), "start", "u1"),
              (("w_attn_proj", "w_conv_proj", "w_mix_out"), "proj", "conv_o"),
              (("w_xq", "w_xkv", "w_xo"), "ya", "h1"),
              (("w_ffn_in",), "u2", "h2"),
              (("w_ffn_out",), "u3", "hid"))

    def __init__(self, seeded, axes, shard_shapes):
        self.order = [nm for names, _, _ in self.GROUPS for nm in names]
        self.axes = [axes[nm] for nm in self.order]
        self.shapes = [shard_shapes[nm] for nm in self.order]
        self.whole = (self.order.index("conv_w"),)
        self.send, self.recv, bufs, self.token = _ag_start(
            [seeded[nm] for nm in self.order], self.axes, self.shapes, self.whole)
        self.buf = dict(zip(self.order, bufs))
        self.passes = {}
        self.ready = set()
        self.grads = {}
        self.mark("start", self.token)

    def mark(self, tag, value):
        for g, (names, mid, wait) in enumerate(self.GROUPS):
            idxs = [self.order.index(nm) for nm in names]
            if tag == mid:
                fs, fr, bufs = _ag_mid([self.buf[nm] for nm in names], idxs, self.axes, self.shapes, self.whole,
                                       self.send, self.recv, value, "ag_mid_%d" % g)
                self.buf.update(zip(names, bufs))
                self.passes[g] = (fs, fr)
            if tag == wait:
                fs, fr = self.passes[g]
                bufs = _ag_wait([self.buf[nm] for nm in names], idxs, self.axes, self.shapes, self.whole, fs, fr,
                                value, "ag_wait_%d" % g)
                self.buf.update(zip(names, bufs))
                self.ready.update(names)
        return value

    def get(self, name):
        assert name in self.ready, name
        return self.buf[name]

    def grad(self, name, g):
        self.grads[name] = g


def _pair_exchange(grads, axes, shard_shapes):
    n = len(grads)

    def body(*refs):
        g, ra = refs[:n], refs[n:2 * n]
        send_sems, recv_sems = refs[2 * n:]
        x, y, c, _ = _place()
        sib = (x, y, 1 - c)
        sends = []

        def pieces(ref, i, h):
            if axes[i] == 1:
                return [_half(ref, h)]
            return [_window(ref, 0, shard_shapes[i], s, h) for s in range(N_CHIPS)]

        for i in range(n):
            for k, (src, dst) in enumerate(zip(pieces(g[i], i, 1 - c), pieces(ra[i], i, 1 - c))):
                cp = _remote(src, dst, send_sems.at[i, k], recv_sems.at[i, k], sib)
                cp.start()
                sends.append(cp)
        for i in range(n):
            for k, dst in enumerate(pieces(ra[i], i, c)):
                _remote(dst, dst, send_sems.at[i, k], recv_sems.at[i, k], sib).wait_recv()
        for cp in sends:
            cp.wait_send()

    return pl.pallas_call(
        body, name="grad_pair_exchange",
        in_specs=[ANY] * n, out_specs=[ANY] * n, out_shape=[_sds(g.shape, g.dtype) for g in grads],
        scratch_shapes=[pltpu.SemaphoreType.DMA((n, N_CHIPS)), pltpu.SemaphoreType.DMA((n, N_CHIPS))],
    )(*grads)


def _pair_add(g, ra, ax, shard_shape, place, name):
    sr, sc = shard_shape
    hr = sr // 2
    wc = sc
    tr = _pick(hr, (256, 352, 128, 64, 32, 16))
    nr = hr // tr

    def body(p_ref, a_ref, b_ref, o_ref):
        o_ref[...] = (a_ref[...].astype(F32) + b_ref[...].astype(F32)).astype(o_ref.dtype)

    if ax == 1:
        src = pl.BlockSpec((tr, wc), lambda s, r, p_ref: (p_ref[0] * nr + r, s))
    else:
        src = pl.BlockSpec((tr, wc), lambda s, r, p_ref: (s * 2 * nr + p_ref[0] * nr + r, 0))
    return pl.pallas_call(
        body, name=name,
        grid_spec=pltpu.PrefetchScalarGridSpec(
            num_scalar_prefetch=1, grid=(N_CHIPS, nr), in_specs=[src, src],
            out_specs=pl.BlockSpec((None, tr, wc), lambda s, r, p_ref: (s, r, 0))),
        out_shape=_sds((N_CHIPS, hr, wc), BF16),
        compiler_params=_cparams("parallel", "parallel"),
    )(place, g, ra)


def _chip_exchange(parts):
    n = len(parts)

    def body(*refs):
        p, rc = refs[:n], refs[n:2 * n]
        send_sems, recv_sems = refs[2 * n:]
        x, y, c, chips = _place()
        me = 2 * x + y
        sends = []
        for i in range(n):
            for j, chip in enumerate(chips):
                cj = 2 * chip[0] + chip[1]
                cp = _remote(p[i].at[cj], rc[i].at[me], send_sems.at[i, j], recv_sems.at[i, j], (*chip, c))
                cp.start()
                sends.append(cp)
        for i in range(n):
            for j, chip in enumerate(chips):
                cj = 2 * chip[0] + chip[1]
                _remote(rc[i].at[cj], rc[i].at[cj], send_sems.at[i, j], recv_sems.at[i, j], (*chip, c)).wait_recv()
        for cp in sends:
            cp.wait_send()

    return pl.pallas_call(
        body, name="grad_chip_exchange",
        in_specs=[ANY] * n, out_specs=[ANY] * n, out_shape=[_sds(p.shape, p.dtype) for p in parts],
        scratch_shapes=[pltpu.SemaphoreType.DMA((n, 3)), pltpu.SemaphoreType.DMA((n, 3))],
    )(*parts)


def _chip_add(part, rc, place, name):
    _, hr, wc = rc.shape
    tr = _pick(hr, (256, 352, 128, 64, 32, 16))
    nr = hr // tr

    def body(p_ref, own_ref, r1_ref, r2_ref, r3_ref, o_ref):
        acc = own_ref[...].astype(F32)
        for r_ref in (r1_ref, r2_ref, r3_ref):
            acc = acc + r_ref[...].astype(F32)
        o_ref[...] = acc

    def slot(k):
        return pl.BlockSpec((None, tr, wc), lambda r, p_ref: ((p_ref[1] + k) % N_CHIPS, r, 0))

    return pl.pallas_call(
        body, name=name,
        grid_spec=pltpu.PrefetchScalarGridSpec(
            num_scalar_prefetch=1, grid=(nr,), in_specs=[slot(0), slot(1), slot(2), slot(3)],
            out_specs=pl.BlockSpec((tr, wc), lambda r, p_ref: (p_ref[0] * nr + r, 0))),
        out_shape=_sds((2 * hr, wc), F32),
        compiler_params=_cparams("parallel"),
    )(place, part, rc, rc, rc)


def _pair_gather(shards):
    n = len(shards)

    def body(*refs):
        out = refs[n:2 * n]
        send_sems, recv_sems = refs[2 * n:]
        x, y, c, _ = _place()
        sib = (x, y, 1 - c)
        sends = [_remote(_half(out[i], c), _half(out[i], c), send_sems.at[i], recv_sems.at[i], sib) for i in range(n)]
        for cp in sends:
            cp.start()
        for i in range(n):
            _remote(_half(out[i], 1 - c), _half(out[i], 1 - c), send_sems.at[i], recv_sems.at[i], sib).wait_recv()
        for cp in sends:
            cp.wait_send()

    return pl.pallas_call(
        body, name="grad_pair_gather",
        in_specs=[ANY] * n, out_specs=[ANY] * n, out_shape=[_sds(s.shape, s.dtype) for s in shards],
        input_output_aliases={i: i for i in range(n)},
        scratch_shapes=[pltpu.SemaphoreType.DMA((n,)), pltpu.SemaphoreType.DMA((n,))],
    )(*shards)


N_DEV = 8


def _all_reduce_small(buf):
    r, cdim = buf.shape

    def body(x_ref, o_ref, land, send_sems, recv_sems):
        x, y, c, _ = _place()
        me = 4 * x + 2 * y + c
        land[me] = x_ref[...]
        sends = []
        for k in range(1, N_DEV):
            kx, ky, kc = (k >> 2) & 1, (k >> 1) & 1, k & 1
            peer = (1 - x if kx else x, 1 - y if ky else y, 1 - c if kc else c)
            cp = _remote(x_ref, land.at[me], send_sems.at[k - 1], recv_sems.at[k - 1], peer)
            cp.start()
            sends.append(cp)
        for k in range(1, N_DEV):
            kx, ky, kc = (k >> 2) & 1, (k >> 1) & 1, k & 1
            peer = (1 - x if kx else x, 1 - y if ky else y, 1 - c if kc else c)
            pidx = 4 * peer[0] + 2 * peer[1] + peer[2]
            _remote(land.at[pidx], land.at[pidx], send_sems.at[k - 1], recv_sems.at[k - 1], peer).wait_recv()
        for cp in sends:
            cp.wait_send()
        acc = land[0]
        for dev in range(1, N_DEV):
            acc = acc + land[dev]
        o_ref[...] = acc

    vm = pl.BlockSpec(memory_space=pltpu.VMEM)
    return pl.pallas_call(
        body, name="all_reduce_small", in_specs=[vm], out_specs=vm, out_shape=_sds((r, cdim), F32),
        scratch_shapes=[pltpu.VMEM((N_DEV, r, cdim), F32), pltpu.SemaphoreType.DMA((N_DEV - 1,)),
                        pltpu.SemaphoreType.DMA((N_DEV - 1,))],
    )(buf)


SMALL_ROWS = 16


def kernel(x, mem, g_mix, w_in, conv_w, attn_sinks, w_attn_proj, w_conv_proj, w_mix_out, g_xattn, g_mem, w_xq, w_xkv, w_xo, g_ffn, w_ffn_in, w_ffn_out, g_final, loss_target, m_g_mix, m_w_in, m_conv_w, m_attn_sinks, m_w_attn_proj, m_w_conv_proj, m_w_mix_out, m_g_xattn, m_g_mem, m_w_xq, m_w_xkv, m_w_xo, m_g_ffn, m_w_ffn_in, m_w_ffn_out, m_g_final, v_g_mix, v_w_in, v_conv_w, v_attn_sinks, v_w_attn_proj, v_w_conv_proj, v_w_mix_out, v_g_xattn, v_g_mem, v_w_xq, v_w_xkv, v_w_xo, v_g_ffn, v_w_ffn_in, v_w_ffn_out, v_g_final):
    w = dict(g_mix=g_mix, w_in=w_in[0], conv_w=conv_w[0], attn_sinks=attn_sinks, w_attn_proj=w_attn_proj[0],
             w_conv_proj=w_conv_proj[0], w_mix_out=w_mix_out[0], g_xattn=g_xattn, g_mem=g_mem, w_xq=w_xq[0],
             w_xkv=w_xkv[0], w_xo=w_xo[0], g_ffn=g_ffn, w_ffn_in=w_ffn_in[0], w_ffn_out=w_ffn_out[0],
             g_final=g_final[None])
    m = dict(g_mix=m_g_mix, w_in=m_w_in[0], conv_w=m_conv_w[0], attn_sinks=m_attn_sinks,
             w_attn_proj=m_w_attn_proj[0], w_conv_proj=m_w_conv_proj[0], w_mix_out=m_w_mix_out[0],
             g_xattn=m_g_xattn, g_mem=m_g_mem, w_xq=m_w_xq[0], w_xkv=m_w_xkv[0], w_xo=m_w_xo[0], g_ffn=m_g_ffn,
             w_ffn_in=m_w_ffn_in[0], w_ffn_out=m_w_ffn_out[0], g_final=m_g_final[None])
    v = dict(g_mix=v_g_mix, w_in=v_w_in[0], conv_w=v_conv_w[0], attn_sinks=v_attn_sinks,
             w_attn_proj=v_w_attn_proj[0], w_conv_proj=v_w_conv_proj[0], w_mix_out=v_w_mix_out[0],
             g_xattn=v_g_xattn, g_mem=v_g_mem, w_xq=v_w_xq[0], w_xkv=v_w_xkv[0], w_xo=v_w_xo[0], g_ffn=v_g_ffn,
             w_ffn_in=v_w_ffn_in[0], w_ffn_out=v_w_ffn_out[0], g_final=v_g_final[None])
    names = [nm for nm, _ in BIG]
    axes = [ax for _, ax in BIG]
    d = x.shape[2]
    cw = w["conv_w"].shape[1] * N_CHIPS
    chip = (2 * lax.axis_index("x") + lax.axis_index("y")).astype(jnp.int32)
    place = jnp.stack([lax.axis_index("c").astype(jnp.int32), chip])
    shard_shapes = [w[nm].shape for nm in names]

    me1 = chip.reshape(1)
    seeded = {nm: _cast_to_full(w[nm], ax, me1, "cast_" + nm) for nm, ax in zip(names, axes)}
    seeded["conv_w"] = _cast_to_full(w["conv_w"], 1, me1, "place_conv_w", F32)
    wts = _Gather(seeded, dict(zip(names + ["conv_w"], axes + [1])),
                  dict(zip(names + ["conv_w"], shard_shapes + [w["conv_w"].shape])))
    aw, cw = w["w_attn_proj"].shape[0], w["w_conv_proj"].shape[0]
    kvw = (w["w_in"].shape[1] * N_CHIPS - aw - 3 * cw - 2 * d) // 2
    grad_x, small = _local_step(
        x[0], mem[0], loss_target[0], w["g_mix"] + wts.token[0:1, 0:1], w["attn_sinks"], w["g_xattn"], w["g_mem"],
        w["g_ffn"], w["g_final"], (aw, cw, kvw), wts)
    big = wts.grads

    glist = [big[nm] for nm in names]
    received = _pair_exchange(glist, axes, shard_shapes)
    parts = [_pair_add(g, ra, ax, ss, place, "pair_add_" + nm)
             for g, ra, ax, ss, nm in zip(glist, received, axes, shard_shapes, names)]
    landed = _chip_exchange(parts)
    halves = [_chip_add(p, rc, place, "chip_add_" + nm) for p, rc, nm in zip(parts, landed, names)]
    grads = dict(zip(names, _pair_gather(halves)))

    pw = max(d, cw)

    def row(a):
        return jnp.pad(a, ((0, 0), (0, pw - a.shape[1])))

    gains = ("g_mix", "g_xattn", "g_mem", "g_ffn", "g_final")
    packed = jnp.concatenate(
        [row(small[nm]) for nm in gains] + [row(small["conv_w"]),
         row(jnp.concatenate([small["attn_sinks"], small["loss"]], axis=1)),
         jnp.zeros((SMALL_ROWS - 9, pw), F32)], axis=0)
    total = _all_reduce_small(packed)
    nsink = attn_sinks.shape[1]
    grads.update({nm: total[k:k + 1, :d] for k, nm in enumerate(gains)})
    grads.update(conv_w=lax.dynamic_slice(total, (5, chip * (cw // N_CHIPS)), (3, cw // N_CHIPS)),
                 attn_sinks=total[8:9, :nsink])
    loss = total[8, nsink]

    order = ["g_mix", "w_in", "conv_w", "attn_sinks", "w_attn_proj", "w_conv_proj", "w_mix_out", "g_xattn", "g_mem",
             "w_xq", "w_xkv", "w_xo", "g_ffn", "w_ffn_in", "w_ffn_out", "g_final"]
    upd = {nm: _adamw(w[nm], grads[nm], m[nm], v[nm], "adamw_" + nm) for nm in order}

    stacked = set(names) | {"conv_w"}

    def shaped(nm, a):
        if nm == "g_final":
            return a[0]
        return a[None] if nm in stacked else a

    outs = [loss, grad_x[None]]
    outs += [shaped(nm, grads[nm]) for nm in order]
    for k in range(3):
        outs += [shaped(nm, upd[nm][k]) for nm in order]
    return tuple(outs)
```

```python
import functools

import jax
import jax.numpy as jnp
from jax import lax
from jax.experimental import pallas as pl
from jax.experimental.pallas import tpu as pltpu

F32 = jnp.float32
BF16 = jnp.bfloat16

VMEM_LIMIT_BYTES = 56 * 1024 * 1024
LANES = 128
HEAD_DIM = 64
BLOCK = 128
X_HEAD_DIM = 128
ROPE_THETA = 10000.0
EPS = 1e-6
NEG = -1e30
ADAM_LR, ADAM_B1, ADAM_B2, ADAM_EPS, ADAM_WD, ADAM_STEP = 0.001, 0.9, 0.999, 1e-08, 0.01, 10
N_CHIPS = 4
MESH = pl.DeviceIdType.MESH
ANY = pl.BlockSpec(memory_space=pl.ANY)


def _pick(dim, prefs):
    for p in prefs:
        if dim % p == 0:
            return p
    return dim


def _cparams(*sem):
    return pltpu.CompilerParams(dimension_semantics=sem, vmem_limit_bytes=VMEM_LIMIT_BYTES)


def _sds(shape, dtype):
    return jax.ShapeDtypeStruct(shape, dtype)


def _sigmoid(v):
    return 1.0 / (1.0 + jnp.exp(-v))


def _matmul(a, b, *, mode, out_dtype, name, res=None, tm=None, tn=None, tk=None):
    if mode == "nn":
        (m, k), (k2, n) = a.shape, b.shape
    elif mode == "nt":
        (m, k), (n, k2) = a.shape, b.shape
    else:
        (k, m), (k2, n) = a.shape, b.shape
    assert k == k2, (a.shape, b.shape, mode)
    tm = tm or _pick(m, (1024, 512, 256, 128))
    tn = tn or _pick(n, (512, 256, 128))
    tk = tk or (k if k <= 2048 else _pick(k, (2048, 2176, 2816, 1408, 1024, 512, 256, 128)))
    nk = k // tk
    dims = {"nn": (((1,), (0,)), ((), ())), "nt": (((1,), (1,)), ((), ())), "tn": (((0,), (0,)), ((), ()))}[mode]
    has_res = res is not None

    def body(*refs):
        a_ref, b_ref = refs[0], refs[1]
        r_ref = refs[2] if has_res else None
        o_ref = refs[2 + has_res]
        part = lax.dot_general(a_ref[...], b_ref[...], dims, preferred_element_type=F32)

        def finish(val):
            if has_res:
                val = val + r_ref[...]
            o_ref[...] = val.astype(o_ref.dtype)

        if nk == 1:
            finish(part)
        else:
            acc = refs[3 + has_res]
            kk = pl.program_id(2)

            @pl.when(kk == 0)
            def _():
                acc[...] = part

            @pl.when(kk > 0)
            def _():
                acc[...] += part

            @pl.when(kk == nk - 1)
            def _():
                finish(acc[...])

    if mode == "tn":
        a_spec = pl.BlockSpec((tk, tm), lambda i, j, kk: (kk, i))
    else:
        a_spec = pl.BlockSpec((tm, tk), lambda i, j, kk: (i, kk))
    if mode == "nt":
        b_spec = pl.BlockSpec((tn, tk), lambda i, j, kk: (j, kk))
    else:
        b_spec = pl.BlockSpec((tk, tn), lambda i, j, kk: (kk, j))
    o_spec = pl.BlockSpec((tm, tn), lambda i, j, kk: (i, j))
    return pl.pallas_call(
        body,
        name=name,
        grid=(m // tm, n // tn, nk),
        in_specs=[a_spec, b_spec] + ([o_spec] if has_res else []),
        out_specs=o_spec,
        out_shape=_sds((m, n), out_dtype),
        scratch_shapes=[pltpu.VMEM((tm, tn), F32)] if nk > 1 else [],
        compiler_params=_cparams("parallel", "parallel", "arbitrary"),
    )(*([a, b] + ([res] if has_res else [])))


def _rms_fwd(x, g, name):
    t, d = x.shape
    tm = _pick(t, (512, 256, 128))

    def body(x_ref, g_ref, o_ref):
        xf = x_ref[...]
        r = lax.rsqrt(jnp.mean(xf * xf, axis=-1, keepdims=True) + EPS)
        o_ref[...] = (xf * r * g_ref[...]).astype(o_ref.dtype)

    row = pl.BlockSpec((tm, d), lambda i: (i, 0))
    return pl.pallas_call(
        body, name=name, grid=(t // tm,),
        in_specs=[row, pl.BlockSpec((1, d), lambda i: (0, 0))],
        out_specs=row, out_shape=_sds((t, d), BF16),
        compiler_params=_cparams("parallel"),
    )(x, g)


def _rms_bwd_math(xf, g, du):
    r = lax.rsqrt(jnp.mean(xf * xf, axis=-1, keepdims=True) + EPS)
    xh = xf * r
    gdy = g * du
    dx = r * (gdy - xh * jnp.mean(gdy * xh, axis=-1, keepdims=True))
    dg = jnp.sum(du * xh, axis=0, keepdims=True)
    return dx, dg


def _rms_bwd(x, g, du, dh, name):
    t, d = x.shape
    tm = _pick(t, (256, 128))
    has_dh = dh is not None

    def body(*refs):
        x_ref, g_ref, du_ref = refs[0], refs[1], refs[2]
        o_ref, ob_ref, dg_ref = refs[3 + has_dh:]
        dx, dg = _rms_bwd_math(x_ref[...], g_ref[...], du_ref[...].astype(F32))
        if has_dh:
            dx = dx + refs[3][...]
        o_ref[...] = dx
        ob_ref[...] = dx.astype(BF16)

        @pl.when(pl.program_id(0) == 0)
        def _():
            dg_ref[...] = dg

        @pl.when(pl.program_id(0) > 0)
        def _():
            dg_ref[...] += dg

    row = pl.BlockSpec((tm, d), lambda i: (i, 0))
    vec = pl.BlockSpec((1, d), lambda i: (0, 0))
    return pl.pallas_call(
        body, name=name, grid=(t // tm,),
        in_specs=[row, vec, row] + ([row] if has_dh else []),
        out_specs=[row, row, vec],
        out_shape=[_sds((t, d), F32), _sds((t, d), BF16), _sds((1, d), F32)],
        compiler_params=_cparams("arbitrary"),
    )(*([x, g, du] + ([dh] if has_dh else [])))


def _loss_head(h, g, tgt):
    t, d = h.shape
    tm = _pick(t, (256, 128))

    def body(h_ref, g_ref, t_ref, o_ref, ob_ref, dg_ref, l_ref):
        xf = h_ref[...]
        gv = g_ref[...]
        r = lax.rsqrt(jnp.mean(xf * xf, axis=-1, keepdims=True) + EPS)
        err = xf * r * gv - t_ref[...]
        part = 0.5 * jnp.sum(jnp.mean(err * err, axis=-1, keepdims=True), axis=0, keepdims=True)
        dx, dg = _rms_bwd_math(xf, gv, err * (1.0 / d))
        o_ref[...] = dx
        ob_ref[...] = dx.astype(BF16)
        lrow = jnp.broadcast_to(part, (1, LANES))

        @pl.when(pl.program_id(0) == 0)
        def _():
            dg_ref[...] = dg
            l_ref[...] = lrow

        @pl.when(pl.program_id(0) > 0)
        def _():
            dg_ref[...] += dg
            l_ref[...] += lrow

    row = pl.BlockSpec((tm, d), lambda i: (i, 0))
    vec = pl.BlockSpec((1, d), lambda i: (0, 0))
    return pl.pallas_call(
        body, name="loss_head", grid=(t // tm,),
        in_specs=[row, vec, row],
        out_specs=[row, row, vec, pl.BlockSpec((1, LANES), lambda i: (0, 0))],
        out_shape=[_sds((t, d), F32), _sds((t, d), BF16), _sds((1, d), F32), _sds((1, LANES), F32)],
        compiler_params=_cparams("arbitrary"),
    )(h, g, tgt)


def _rope_tables(t):
    half = HEAD_DIM // 2
    inv_freq = ROPE_THETA ** (-jnp.arange(half, dtype=F32) / half)
    ang = jnp.arange(t, dtype=F32)[:, None] * inv_freq[None, :]
    cos = jnp.cos(ang)
    sin = jnp.sin(ang)
    reps = LANES // HEAD_DIM
    cos_t = jnp.tile(jnp.concatenate([cos, cos], axis=1), (1, reps))
    sin_t = jnp.tile(jnp.concatenate([-sin, sin], axis=1), (1, reps))
    return cos_t, sin_t


def _rope(v, cos, sin):
    w = v.shape[1]
    c = jnp.tile(cos, (1, w // LANES))
    s = jnp.tile(sin, (1, w // LANES))
    lane = lax.broadcasted_iota(jnp.int32, v.shape, 1)
    first = (lane % HEAD_DIM) < (HEAD_DIM // 2)
    partner = jnp.where(first, pltpu.roll(v, w - HEAD_DIM // 2, 1), pltpu.roll(v, HEAD_DIM // 2, 1))
    return v * c + partner * s


def _heads(v, first, count):
    return jnp.concatenate([v[:, (first + i) * HEAD_DIM:(first + i + 1) * HEAD_DIM] for i in range(count)], axis=0)


def _attn_probs(qs, kb, n, h, qpk, sinks_ref):
    s = lax.dot_general(qs, kb, (((1,), (1,)), ((), ())), preferred_element_type=F32) * (HEAD_DIM ** -0.5)
    qi = lax.broadcasted_iota(jnp.int32, (BLOCK, 2 * BLOCK), 0)
    kc = lax.broadcasted_iota(jnp.int32, (BLOCK, 2 * BLOCK), 1)
    valid = (kc > qi) & (kc <= qi + BLOCK) & ((kc >= BLOCK) | (n > 0))
    bias = jnp.tile(jnp.where(valid, 0.0, NEG).astype(F32), (qpk, 1))
    s = s + bias
    rowg = lax.broadcasted_iota(jnp.int32, (qpk * BLOCK, 1), 0) // BLOCK
    sink = jnp.zeros((qpk * BLOCK, 1), F32)
    for g in range(qpk):
        sink = jnp.where(rowg == g, sinks_ref[0, h * qpk + g], sink)
    m = jnp.maximum(jnp.max(s, axis=-1, keepdims=True), sink)
    e = jnp.exp(s - m)
    es = jnp.exp(sink - m)
    inv = 1.0 / (jnp.sum(e, axis=-1, keepdims=True) + es)
    return e * inv, es * inv, rowg


def _attn_specs(aw, kvw):
    koff = aw // kvw
    prev = lambda n: jnp.maximum(n - 1, 0)
    return [
        pl.BlockSpec((BLOCK, aw), lambda n: (n, 0)),
        pl.BlockSpec((BLOCK, kvw), lambda n: (n, koff)),
        pl.BlockSpec((BLOCK, kvw), lambda n: (prev(n), koff)),
        pl.BlockSpec((BLOCK, kvw), lambda n: (n, koff + 1)),
        pl.BlockSpec((BLOCK, kvw), lambda n: (prev(n), koff + 1)),
        pl.BlockSpec((BLOCK, LANES), lambda n: (n, 0)),
        pl.BlockSpec((BLOCK, LANES), lambda n: (n, 0)),
        pl.BlockSpec((BLOCK, LANES), lambda n: (prev(n), 0)),
        pl.BlockSpec((BLOCK, LANES), lambda n: (prev(n), 0)),
        pl.BlockSpec(memory_space=pltpu.SMEM),
    ]


def _attn_fwd(proj, cos, sin, sinks, aw, kvw):
    t = proj.shape[0]
    nkv = kvw // HEAD_DIM
    qpk = aw // kvw

    def body(q_ref, kc_ref, kp_ref, vc_ref, vp_ref, cc_ref, sc_ref, cp_ref, sp_ref, sinks_ref, o_ref):
        n = pl.program_id(0)
        q = _rope(q_ref[...], cc_ref[...], sc_ref[...]).astype(BF16)
        kc = _rope(kc_ref[...], cc_ref[...], sc_ref[...]).astype(BF16)
        kp = _rope(kp_ref[...], cp_ref[...], sp_ref[...]).astype(BF16)
        vc = vc_ref[...].astype(BF16)
        vp = vp_ref[...].astype(BF16)
        outs = []
        for h in range(nkv):
            hs = slice(h * HEAD_DIM, (h + 1) * HEAD_DIM)
            kb = jnp.concatenate([kp[:, hs], kc[:, hs]], axis=0)
            vb = jnp.concatenate([vp[:, hs], vc[:, hs]], axis=0)
            p, _, _ = _attn_probs(_heads(q, h * qpk, qpk), kb, n, h, qpk, sinks_ref)
            o = jnp.dot(p.astype(BF16), vb, preferred_element_type=F32)
            outs += [o[g * BLOCK:(g + 1) * BLOCK] for g in range(qpk)]
        o_ref[...] = jnp.concatenate(outs, axis=1).astype(o_ref.dtype)

    return pl.pallas_call(
        body, name="attn_fwd", grid=(t // BLOCK,),
        in_specs=_attn_specs(aw, kvw),
        out_specs=pl.BlockSpec((BLOCK, aw), lambda n: (n, 0)),
        out_shape=_sds((t, aw), BF16),
        compiler_params=_cparams("parallel"),
    )(proj, proj, proj, proj, proj, cos, sin, cos, sin, sinks)


def _attn_bwd(proj, cos, sin, sinks, do, aw, kvw):
    t = proj.shape[0]
    nkv = kvw // HEAD_DIM
    qpk = aw // kvw
    scale = HEAD_DIM ** -0.5

    def body(q_ref, kc_ref, kp_ref, vc_ref, vp_ref, cc_ref, sc_ref, cp_ref, sp_ref, sinks_ref, do_ref,
             dq_ref, dkc_ref, dkp_ref, dvc_ref, dvp_ref, ds_ref):
        n = pl.program_id(0)
        cc, sc, cp, sp = cc_ref[...], sc_ref[...], cp_ref[...], sp_ref[...]
        q = _rope(q_ref[...], cc, sc).astype(BF16)
        kc = _rope(kc_ref[...], cc, sc).astype(BF16)
        kp = _rope(kp_ref[...], cp, sp).astype(BF16)
        vc = vc_ref[...].astype(BF16)
        vp = vp_ref[...].astype(BF16)
        dout = do_ref[...]
        dqs, dks, dvs = [], [], []
        lane = lax.broadcasted_iota(jnp.int32, (8, LANES), 1)
        row0 = lax.broadcasted_iota(jnp.int32, (8, LANES), 0) == 0
        dsink = jnp.zeros((8, LANES), F32)
        for h in range(nkv):
            hs = slice(h * HEAD_DIM, (h + 1) * HEAD_DIM)
            kb = jnp.concatenate([kp[:, hs], kc[:, hs]], axis=0)
            vb = jnp.concatenate([vp[:, hs], vc[:, hs]], axis=0)
            qs = _heads(q, h * qpk, qpk)
            dos = _heads(dout, h * qpk, qpk)
            p, psink, rowg = _attn_probs(qs, kb, n, h, qpk, sinks_ref)
            pb = p.astype(BF16)
            o = jnp.dot(pb, vb, preferred_element_type=F32)
            delta = jnp.sum(dos.astype(F32) * o, axis=-1, keepdims=True)
            dvs.append(lax.dot_general(pb, dos, (((0,), (0,)), ((), ())), preferred_element_type=F32))
            dp = lax.dot_general(dos, vb, (((1,), (1,)), ((), ())), preferred_element_type=F32)
            dsc = (p * (dp - delta)).astype(BF16)
            dq_h = jnp.dot(dsc, kb, preferred_element_type=F32) * scale
            dqs += [dq_h[g * BLOCK:(g + 1) * BLOCK] for g in range(qpk)]
            dks.append(lax.dot_general(dsc, qs, (((0,), (0,)), ((), ())), preferred_element_type=F32) * scale)
            sink_term = psink * delta
            for g in range(qpk):
                val = -jnp.sum(jnp.where(rowg == g, sink_term, 0.0))
                dsink = jnp.where(row0 & (lane == h * qpk + g), val, dsink)
        dq = jnp.concatenate(dqs, axis=1)
        dq_ref[...] = _rope(dq, cc, -sc).astype(dq_ref.dtype)
        dk = jnp.concatenate(dks, axis=1)
        dv = jnp.concatenate(dvs, axis=1)
        dkp_ref[...] = _rope(dk[:BLOCK], cp, -sp)
        dkc_ref[...] = _rope(dk[BLOCK:], cc, -sc)
        dvp_ref[...] = dv[:BLOCK]
        dvc_ref[...] = dv[BLOCK:]

        @pl.when(n == 0)
        def _():
            ds_ref[...] = dsink

        @pl.when(n > 0)
        def _():
            ds_ref[...] += dsink

    kv_spec = pl.BlockSpec((BLOCK, kvw), lambda n: (n, 0))
    return pl.pallas_call(
        body, name="attn_bwd", grid=(t // BLOCK,),
        in_specs=_attn_specs(aw, kvw) + [pl.BlockSpec((BLOCK, aw), lambda n: (n, 0))],
        out_specs=[pl.BlockSpec((BLOCK, aw), lambda n: (n, 0)), kv_spec, kv_spec, kv_spec, kv_spec,
                   pl.BlockSpec((8, LANES), lambda n: (0, 0))],
        out_shape=[_sds((t, aw), BF16)] + [_sds((t, kvw), F32)] * 4 + [_sds((8, LANES), F32)],
        compiler_params=_cparams("arbitrary"),
    )(proj, proj, proj, proj, proj, cos, sin, cos, sin, sinks, do)


def _shift_down(v, k, halo):
    rows = lax.broadcasted_iota(jnp.int32, v.shape, 0)
    out = pltpu.roll(v, k, 0)
    for r in range(k):
        out = jnp.where(rows == r, halo[8 - k + r:8 - k + r + 1, :], out)
    return out


def _shift_up(v, k, halo):
    tm = v.shape[0]
    rows = lax.broadcasted_iota(jnp.int32, v.shape, 0)
    out = pltpu.roll(v, tm - k, 0)
    for r in range(k):
        out = jnp.where(rows == tm - k + r, halo[r:r + 1, :], out)
    return out


def _conv_fwd(proj, conv_w, zoff, cw, cb):
    t = proj.shape[0]
    tm = _pick(t, (512, 256, 128))
    zb, nb = zoff // cb, cw // cb
    h8 = tm // 8

    def body(z_ref, gb_ref, gc_ref, zp_ref, gcp_ref, w_ref, o_ref):
        i = pl.program_id(0)
        cz = gc_ref[...] * z_ref[...]
        czp = gcp_ref[...] * zp_ref[...] * (i > 0).astype(F32)
        w = w_ref[...]
        y = w[0:1] * _shift_down(cz, 2, czp) + w[1:2] * _shift_down(cz, 1, czp) + w[2:3] * cz
        o_ref[...] = (gb_ref[...] * y).astype(o_ref.dtype)

    def col(k):
        return pl.BlockSpec((tm, cb), lambda i, j: (i, zb + k * nb + j))

    def halo(k):
        return pl.BlockSpec((8, cb), lambda i, j: (jnp.maximum(i * h8 - 1, 0), zb + k * nb + j))

    return pl.pallas_call(
        body, name="conv_fwd", grid=(t // tm, nb),
        in_specs=[col(0), col(1), col(2), halo(0), halo(2), pl.BlockSpec((3, cb), lambda i, j: (0, j))],
        out_specs=pl.BlockSpec((tm, cb), lambda i, j: (i, j)),
        out_shape=_sds((t, cw), BF16),
        compiler_params=_cparams("parallel", "parallel"),
    )(proj, proj, proj, proj, proj, conv_w)


def _conv_bwd(proj, conv_w, dco, zoff, cw, cb):
    t = proj.shape[0]
    tm = _pick(t, (512, 256, 128))
    zb, nb = zoff // cb, cw // cb
    h8 = tm // 8
    nt = t // tm

    def body(z_ref, gb_ref, gc_ref, zp_ref, gcp_ref, gbn_ref, w_ref, d_ref, dn_ref, dz_ref, dgb_ref, dgc_ref, dw_ref):
        i = pl.program_id(1)
        z, gb, gc = z_ref[...], gb_ref[...], gc_ref[...]
        d = d_ref[...].astype(F32)
        cz = gc * z
        czp = gcp_ref[...] * zp_ref[...] * (i > 0).astype(F32)
        w = w_ref[...]
        cz1 = _shift_down(cz, 1, czp)
        cz2 = _shift_down(cz, 2, czp)
        y = w[0:1] * cz2 + w[1:2] * cz1 + w[2:3] * cz
        dgb_ref[...] = (d * y).astype(dgb_ref.dtype)
        dy = d * gb
        dyn = dn_ref[...].astype(F32) * gbn_ref[...] * (i < nt - 1).astype(F32)
        dcz = w[2:3] * dy + w[1:2] * _shift_up(dy, 1, dyn) + w[0:1] * _shift_up(dy, 2, dyn)
        dgc_ref[...] = (dcz * z).astype(dgc_ref.dtype)
        dz_ref[...] = (dcz * gc).astype(dz_ref.dtype)
        rows = lax.broadcasted_iota(jnp.int32, (8, cb), 0)
        dw = jnp.zeros((8, cb), F32)
        for r, tap in enumerate((cz2, cz1, cz)):
            dw = jnp.where(rows == r, jnp.sum(dy * tap, axis=0, keepdims=True), dw)

        @pl.when(i == 0)
        def _():
            dw_ref[...] = dw

        @pl.when(i > 0)
        def _():
            dw_ref[...] += dw

    def col(k):
        return pl.BlockSpec((tm, cb), lambda j, i: (i, zb + k * nb + j))

    def halo_prev(k):
        return pl.BlockSpec((8, cb), lambda j, i: (jnp.maximum(i * h8 - 1, 0), zb + k * nb + j))

    own = pl.BlockSpec((tm, cb), lambda j, i: (i, j))
    nxt = lambda i: jnp.minimum((i + 1) * h8, t // 8 - 1)
    return pl.pallas_call(
        body, name="conv_bwd", grid=(nb, nt),
        in_specs=[col(0), col(1), col(2), halo_prev(0), halo_prev(2),
                  pl.BlockSpec((8, cb), lambda j, i: (nxt(i), zb + nb + j)),
                  pl.BlockSpec((3, cb), lambda j, i: (0, j)), own,
                  pl.BlockSpec((8, cb), lambda j, i: (nxt(i), j))],
        out_specs=[own, own, own, pl.BlockSpec((8, cb), lambda j, i: (0, j))],
        out_shape=[_sds((t, cw), BF16)] * 3 + [_sds((8, cw), F32)],
        compiler_params=_cparams("parallel", "arbitrary"),
    )(proj, proj, proj, proj, proj, proj, conv_w, dco, dco)


def _merge_fwd(proj, ya, yc, goff, cb):
    t, d = ya.shape
    tm = _pick(t, (512, 256, 128))
    gb_, nb = goff // cb, d // cb

    def body(ga_ref, gc_ref, ya_ref, yc_ref, o_ref):
        o_ref[...] = (_sigmoid(ga_ref[...]) * ya_ref[...] + _sigmoid(gc_ref[...]) * yc_ref[...]).astype(o_ref.dtype)

    own = pl.BlockSpec((tm, cb), lambda i, j: (i, j))
    return pl.pallas_call(
        body, name="merge_fwd", grid=(t // tm, nb),
        in_specs=[pl.BlockSpec((tm, cb), lambda i, j: (i, gb_ + j)),
                  pl.BlockSpec((tm, cb), lambda i, j: (i, gb_ + nb + j)), own, own],
        out_specs=own, out_shape=_sds((t, d), BF16),
        compiler_params=_cparams("parallel", "parallel"),
    )(proj, proj, ya, yc)


def _merge_bwd(proj, ya, yc, dm, goff, cb):
    t, d = ya.shape
    tm = _pick(t, (512, 256, 128))
    gb_, nb = goff // cb, d // cb

    def body(ga_ref, gc_ref, ya_ref, yc_ref, dm_ref, dya_ref, dyc_ref, dga_ref, dgc_ref):
        dmv = dm_ref[...].astype(F32)
        sa = _sigmoid(ga_ref[...])
        sc = _sigmoid(gc_ref[...])
        dya_ref[...] = (dmv * sa).astype(BF16)
        dyc_ref[...] = (dmv * sc).astype(BF16)
        dga_ref[...] = (dmv * ya_ref[...] * sa * (1.0 - sa)).astype(BF16)
        dgc_ref[...] = (dmv * yc_ref[...] * sc * (1.0 - sc)).astype(BF16)

    own = pl.BlockSpec((tm, cb), lambda i, j: (i, j))
    return pl.pallas_call(
        body, name="merge_bwd", grid=(t // tm, nb),
        in_specs=[pl.BlockSpec((tm, cb), lambda i, j: (i, gb_ + j)),
                  pl.BlockSpec((tm, cb), lambda i, j: (i, gb_ + nb + j)), own, own, own],
        out_specs=[own] * 4, out_shape=[_sds((t, d), BF16)] * 4,
        compiler_params=_cparams("parallel", "parallel"),
    )(proj, proj, ya, yc, dm)


def _assemble_dproj(dq, dkc, dkp, dvc, dvp, dz, dgb, dgc, dga, dgg):
    t, aw = dq.shape
    kvw, cw, d = dkc.shape[1], dz.shape[1], dga.shape[1]
    nblk = t // BLOCK
    width = aw + 2 * kvw + 3 * cw + 2 * d

    def body(dq_ref, dkc_ref, dkp_ref, dvc_ref, dvp_ref, dz_ref, dgb_ref, dgc_ref, dga_ref, dgg_ref, o_ref):
        keep = (pl.program_id(0) < nblk - 1).astype(F32)
        dk = dkc_ref[...] + dkp_ref[...] * keep
        dv = dvc_ref[...] + dvp_ref[...] * keep
        o_ref[...] = jnp.concatenate(
            [dq_ref[...], dk.astype(BF16), dv.astype(BF16), dz_ref[...], dgb_ref[...], dgc_ref[...],
             dga_ref[...], dgg_ref[...]], axis=1)

    def cur(w):
        return pl.BlockSpec((BLOCK, w), lambda n: (n, 0))

    def nxt(w):
        return pl.BlockSpec((BLOCK, w), lambda n: (jnp.minimum(n + 1, nblk - 1), 0))

    return pl.pallas_call(
        body, name="assemble_dproj", grid=(nblk,),
        in_specs=[cur(aw), cur(kvw), nxt(kvw), cur(kvw), nxt(kvw), cur(cw), cur(cw), cur(cw), cur(d), cur(d)],
        out_specs=cur(width), out_shape=_sds((t, width), BF16),
        compiler_params=_cparams("parallel"),
    )(dq, dkc, dkp, dvc, dvp, dz, dgb, dgc, dga, dgg)


def _xattn_probs(qh, kh):
    s = lax.dot_general(qh, kh, (((1,), (1,)), ((), ())), preferred_element_type=F32) * (X_HEAD_DIM ** -0.5)
    e = jnp.exp(s - jnp.max(s, axis=-1, keepdims=True))
    return e * (1.0 / jnp.sum(e, axis=-1, keepdims=True))


def _xattn_fwd(xq, kv):
    t, xw = xq.shape
    mt = kv.shape[0]
    tm = _pick(t, (512, 256, 128))

    def body(q_ref, kv_ref, o_ref):
        outs = []
        for hd in range(xw // X_HEAD_DIM):
            hs = slice(hd * X_HEAD_DIM, (hd + 1) * X_HEAD_DIM)
            vs = slice(xw + hd * X_HEAD_DIM, xw + (hd + 1) * X_HEAD_DIM)
            p = _xattn_probs(q_ref[:, hs], kv_ref[:, hs])
            outs.append(jnp.dot(p.astype(BF16), kv_ref[:, vs], preferred_element_type=F32))
        o_ref[...] = jnp.concatenate(outs, axis=1).astype(o_ref.dtype)

    return pl.pallas_call(
        body, name="xattn_fwd", grid=(t // tm,),
        in_specs=[pl.BlockSpec((tm, xw), lambda i: (i, 0)), pl.BlockSpec((mt, 2 * xw), lambda i: (0, 0))],
        out_specs=pl.BlockSpec((tm, xw), lambda i: (i, 0)), out_shape=_sds((t, xw), BF16),
        compiler_params=_cparams("parallel"),
    )(xq, kv)


def _xattn_bwd(xq, kv, do):
    t, xw = xq.shape
    mt = kv.shape[0]
    tm = _pick(t, (512, 256, 128))
    scale = X_HEAD_DIM ** -0.5

    def body(q_ref, kv_ref, do_ref, dq_ref, dkv_ref):
        dqs, dks, dvs = [], [], []
        for hd in range(xw // X_HEAD_DIM):
            hs = slice(hd * X_HEAD_DIM, (hd + 1) * X_HEAD_DIM)
            vs = slice(xw + hd * X_HEAD_DIM, xw + (hd + 1) * X_HEAD_DIM)
            qh, kh, vh, doh = q_ref[:, hs], kv_ref[:, hs], kv_ref[:, vs], do_ref[:, hs]
            p = _xattn_probs(qh, kh)
            pb = p.astype(BF16)
            o = jnp.dot(pb, vh, preferred_element_type=F32)
            delta = jnp.sum(doh.astype(F32) * o, axis=-1, keepdims=True)
            dvs.append(lax.dot_general(pb, doh, (((0,), (0,)), ((), ())), preferred_element_type=F32))
            dp = lax.dot_general(doh, vh, (((1,), (1,)), ((), ())), preferred_element_type=F32)
            dsc = (p * (dp - delta)).astype(BF16)
            dqs.append(jnp.dot(dsc, kh, preferred_element_type=F32) * scale)
            dks.append(lax.dot_general(dsc, qh, (((0,), (0,)), ((), ())), preferred_element_type=F32) * scale)
        dq_ref[...] = jnp.concatenate(dqs, axis=1).astype(dq_ref.dtype)
        dkv = jnp.concatenate(dks + dvs, axis=1)

        @pl.when(pl.program_id(0) == 0)
        def _():
            dkv_ref[...] = dkv

        @pl.when(pl.program_id(0) > 0)
        def _():
            dkv_ref[...] += dkv

    row = pl.BlockSpec((tm, xw), lambda i: (i, 0))
    whole = pl.BlockSpec((mt, 2 * xw), lambda i: (0, 0))
    return pl.pallas_call(
        body, name="xattn_bwd", grid=(t // tm,),
        in_specs=[row, whole, row], out_specs=[row, whole],
        out_shape=[_sds((t, xw), BF16), _sds((mt, 2 * xw), F32)],
        compiler_params=_cparams("arbitrary"),
    )(xq, kv, do)


def _swiglu_fwd(hid):
    t, f2 = hid.shape
    f = f2 // 2
    tm = _pick(t, (256, 128))

    def body(h_ref, o_ref):
        a = h_ref[:, :f].astype(F32)
        b = h_ref[:, f:].astype(F32)
        o_ref[...] = (a * _sigmoid(a) * b).astype(o_ref.dtype)

    return pl.pallas_call(
        body, name="swiglu_fwd", grid=(t // tm,),
        in_specs=[pl.BlockSpec((tm, f2), lambda i: (i, 0))],
        out_specs=pl.BlockSpec((tm, f), lambda i: (i, 0)), out_shape=_sds((t, f), BF16),
        compiler_params=_cparams("parallel"),
    )(hid)


def _swiglu_bwd(hid, dact):
    t, f2 = hid.shape
    f = f2 // 2
    tm = _pick(t, (128,))

    def body(h_ref, d_ref, o_ref):
        a = h_ref[:, :f].astype(F32)
        b = h_ref[:, f:].astype(F32)
        d = d_ref[...].astype(F32)
        sg = _sigmoid(a)
        o_ref[:, :f] = (d * b * sg * (1.0 + a * (1.0 - sg))).astype(o_ref.dtype)
        o_ref[:, f:] = (d * a * sg).astype(o_ref.dtype)

    return pl.pallas_call(
        body, name="swiglu_bwd", grid=(t // tm,),
        in_specs=[pl.BlockSpec((tm, f2), lambda i: (i, 0)), pl.BlockSpec((tm, f), lambda i: (i, 0))],
        out_specs=pl.BlockSpec((tm, f2), lambda i: (i, 0)), out_shape=_sds((t, f2), BF16),
        compiler_params=_cparams("parallel"),
    )(hid, dact)


def _adamw(w, g, m, v, name):
    r, c = w.shape
    tr = _pick(r, (256, 128, 64, 32, 16, 8)) if r * c > 65536 else r

    def body(w_ref, g_ref, m_ref, v_ref, d_ref, nm_ref, nv_ref):
        gv = g_ref[...]
        m2 = ADAM_B1 * m_ref[...] + (1.0 - ADAM_B1) * gv
        v2 = ADAM_B2 * v_ref[...] + (1.0 - ADAM_B2) * (gv * gv)
        m_hat = m2 / (1.0 - ADAM_B1 ** ADAM_STEP)
        v_hat = v2 / (1.0 - ADAM_B2 ** ADAM_STEP)
        d_ref[...] = -ADAM_LR * (m_hat / (jnp.sqrt(v_hat) + ADAM_EPS) + ADAM_WD * w_ref[...])
        nm_ref[...] = m2
        nv_ref[...] = v2

    blk = pl.BlockSpec((tr, c), lambda i: (i, 0))
    return pl.pallas_call(
        body, name=name, grid=(r // tr,),
        in_specs=[blk] * 4, out_specs=[blk] * 3, out_shape=[_sds((r, c), F32)] * 3,
        compiler_params=_cparams("parallel"),
    )(w, g, m, v)


class _Weights:
    def __init__(self, full):
        self.full = full
        self.grads = {}

    def get(self, name):
        return self.full[name]

    def mark(self, tag, value):
        return value

    def grad(self, name, g):
        self.grads[name] = g


def _local_step(x, mem, tgt, g_mix, sinks, g_xattn, g_mem, g_ffn, g_final, dims, wts):
    t, d = x.shape
    aw, cw, kvw = dims
    cb = 2 * kvw
    zoff = aw + 2 * kvw
    goff = zoff + 3 * cw
    cos, sin = _rope_tables(t)
    mm = _matmul
    mark, get = wts.mark, wts.get

    u1 = mark("u1", _rms_fwd(x, g_mix, "rms_mix"))
    proj = mark("proj", mm(u1, get("w_in"), mode="nn", out_dtype=F32, name="mm_proj"))
    attn_o = mark("attn_o", _attn_fwd(proj, cos, sin, sinks, aw, kvw))
    conv_o = mark("conv_o", _conv_fwd(proj, get("conv_w"), zoff, cw, cb))
    ya = mark("ya", mm(attn_o, get("w_attn_proj"), mode="nn", out_dtype=F32, name="mm_yattn"))
    yc = mark("yc", mm(conv_o, get("w_conv_proj"), mode="nn", out_dtype=F32, name="mm_yconv"))
    merged = mark("merged", _merge_fwd(proj, ya, yc, goff, cb))
    h1 = mark("h1", mm(merged, get("w_mix_out"), mode="nn", out_dtype=F32, name="mm_mix", res=x))
    u2 = mark("u2", _rms_fwd(h1, g_xattn, "rms_xattn"))
    mem_n = _rms_fwd(mem, g_mem, "rms_mem")
    xq = mark("xq", mm(u2, get("w_xq"), mode="nn", out_dtype=BF16, name="mm_xq"))
    kv = mm(mem_n, get("w_xkv"), mode="nn", out_dtype=BF16, name="mm_xkv")
    xo = mark("xo", _xattn_fwd(xq, kv))
    h2 = mark("h2", mm(xo, get("w_xo"), mode="nn", out_dtype=F32, name="mm_xo", res=h1))
    u3 = mark("u3", _rms_fwd(h2, g_ffn, "rms_ffn"))
    hid = mark("hid", mm(u3, get("w_ffn_in"), mode="nn", out_dtype=BF16, name="mm_ffn_in"))
    act = mark("act", _swiglu_fwd(hid))
    h3 = mm(act, get("w_ffn_out"), mode="nn", out_dtype=F32, name="mm_ffn_out", res=h2)
    dh3, dh3b, dg_final, loss = _loss_head(h3, g_final, tgt)
    mark("dh3", dh3b)

    wts.grad("w_ffn_out", mm(act, dh3b, mode="tn", out_dtype=BF16, name="mm_dw_ffn_out"))
    dact = mark("dact", mm(dh3b, get("w_ffn_out"), mode="nt", out_dtype=BF16, name="mm_dact"))
    dhid = mark("dhid", _swiglu_bwd(hid, dact))
    wts.grad("w_ffn_in", mm(u3, dhid, mode="tn", out_dtype=BF16, name="mm_dw_ffn_in"))
    du3 = mark("du3", mm(dhid, get("w_ffn_in"), mode="nt", out_dtype=F32, name="mm_du3"))
    dh2, dh2b, dg_ffn = _rms_bwd(h2, g_ffn, du3, dh3, "rms_bwd_ffn")
    mark("dh2", dh2b)
    wts.grad("w_xo", mm(xo, dh2b, mode="tn", out_dtype=BF16, name="mm_dw_xo"))
    dxo = mm(dh2b, get("w_xo"), mode="nt", out_dtype=BF16, name="mm_dxo")
    dxq, dkv = _xattn_bwd(xq, kv, dxo)
    dkvb = dkv.astype(BF16)
    wts.grad("w_xq", mm(u2, dxq, mode="tn", out_dtype=BF16, name="mm_dw_xq"))
    du2 = mm(dxq, get("w_xq"), mode="nt", out_dtype=F32, name="mm_du2")
    wts.grad("w_xkv", mm(mem_n, dkvb, mode="tn", out_dtype=BF16, name="mm_dw_xkv"))
    dmem_n = mm(dkvb, get("w_xkv"), mode="nt", out_dtype=F32, name="mm_dmem")
    _, _, dg_mem = _rms_bwd(mem, g_mem, dmem_n, None, "rms_bwd_mem")
    dh1, dh1b, dg_xattn = _rms_bwd(h1, g_xattn, du2, dh2, "rms_bwd_xattn")
    mark("dh1", dh1b)
    wts.grad("w_mix_out", mm(merged, dh1b, mode="tn", out_dtype=BF16, name="mm_dw_mix"))
    dmerged = mm(dh1b, get("w_mix_out"), mode="nt", out_dtype=BF16, name="mm_dmerged")
    dya, dyc, dga, dgg = _merge_bwd(proj, ya, yc, dmerged, goff, cb)
    mark("dya", dya)
    wts.grad("w_attn_proj", mm(attn_o, dya, mode="tn", out_dtype=BF16, name="mm_dw_attn_proj"))
    dattn_o = mm(dya, get("w_attn_proj"), mode="nt", out_dtype=BF16, name="mm_dattn")
    wts.grad("w_conv_proj", mm(conv_o, dyc, mode="tn", out_dtype=BF16, name="mm_dw_conv_proj"))
    dconv_o = mark("dconv_o", mm(dyc, get("w_conv_proj"), mode="nt", out_dtype=BF16, name="mm_dconv"))
    dz, dgb, dgc, dconv_w = _conv_bwd(proj, get("conv_w"), dconv_o, zoff, cw, cb)
    mark("dz", dz)
    dq, dkc, dkp, dvc, dvp, dsinks = _attn_bwd(proj, cos, sin, sinks, dattn_o, aw, kvw)
    mark("dq", dq)
    dproj = mark("dproj", _assemble_dproj(dq, dkc, dkp, dvc, dvp, dz, dgb, dgc, dga, dgg))
    wts.grad("w_in", mm(u1, dproj, mode="tn", out_dtype=BF16, name="mm_dw_in"))
    du1 = mark("du1", mm(dproj, get("w_in"), mode="nt", out_dtype=F32, name="mm_du1"))
    grad_x, _, dg_mix = _rms_bwd(x, g_mix, du1, dh1, "rms_bwd_mix")
    mark("grad_x", grad_x)

    small = dict(g_mix=dg_mix, g_xattn=dg_xattn, g_mem=dg_mem, g_ffn=dg_ffn, g_final=dg_final,
                 conv_w=dconv_w[:3], attn_sinks=dsinks[0:1, :sinks.shape[1]], loss=loss[0:1, 0:1])
    return grad_x, small


BIG = (("w_in", 1), ("w_attn_proj", 1), ("w_conv_proj", 1), ("w_mix_out", 0), ("w_xq", 0), ("w_xkv", 0),
       ("w_xo", 1), ("w_ffn_in", 1), ("w_ffn_out", 0))


def _place():
    x, y, c = lax.axis_index("x"), lax.axis_index("y"), lax.axis_index("c")
    chips = [(1 - x, y), (x, 1 - y), (1 - x, 1 - y)]
    return x, y, c, chips


def _window(ref, ax, shard_shape, s, h):
    sr, sc = shard_shape
    hr = sr // 2
    if ax == 1:
        return ref.at[pl.ds(pl.multiple_of(h * hr, 16), hr), pl.ds(pl.multiple_of(s * sc, LANES), sc)]
    return ref.at[pl.ds(pl.multiple_of(s * sr + h * hr, 16), hr), :]


def _half(ref, h):
    hr = ref.shape[0] // 2
    return ref.at[pl.ds(pl.multiple_of(h * hr, 16), hr), :]


def _remote(src, dst, send_sem, recv_sem, dev):
    return pltpu.make_async_remote_copy(src_ref=src, dst_ref=dst, send_sem=send_sem, recv_sem=recv_sem,
                                        device_id=dev, device_id_type=MESH)


def _cast_to_full(shard, ax, me, name, dtype=BF16):
    sr, sc = shard.shape
    tr = _pick(sr, (256, 352, 128, 64, 32, 16))
    nr = sr // tr
    full = (sr * N_CHIPS, sc) if ax == 0 else (sr, sc * N_CHIPS)

    def body(me_ref, s_ref, o_ref):
        o_ref[...] = s_ref[...].astype(o_ref.dtype)

    if ax == 1:
        out_spec = pl.BlockSpec((tr, sc), lambda r, me_ref: (r, me_ref[0]))
    else:
        out_spec = pl.BlockSpec((tr, sc), lambda r, me_ref: (me_ref[0] * nr + r, 0))
    return pl.pallas_call(
        body, name=name,
        grid_spec=pltpu.PrefetchScalarGridSpec(
            num_scalar_prefetch=1, grid=(nr,), in_specs=[pl.BlockSpec((tr, sc), lambda r, me_ref: (r, 0))],
            out_specs=out_spec),
        out_shape=_sds(full, dtype),
        compiler_params=_cparams("parallel"),
    )(me, shard)


HBM = pl.BlockSpec(memory_space=pltpu.HBM)
SEM = pl.BlockSpec(memory_space=pltpu.SEMAPHORE)
EFFECT = pltpu.SideEffectType.DATAFLOW_SIDE_EFFECTING


def _in_hbm(a):
    return pltpu.with_memory_space_constraint(a, pltpu.HBM)


def _gather_window(ref, ax, shard_shape, s, h):
    if h is not None:
        return _window(ref, ax, shard_shape, s, h)
    sr, sc = shard_shape
    if ax == 1:
        return ref.at[:, pl.ds(pl.multiple_of(s * sc, LANES), sc)]
    return ref.at[pl.ds(pl.multiple_of(s * sr, 8), sr), :]


def _ag_start(fulls, axes, shard_shapes, whole):
    n = len(fulls)

    def body(*refs):
        src = refs[:n]
        send_sems, recv_sems = refs[n], refs[n + 1]
        token = refs[2 * n + 2]
        x, y, c, chips = _place()
        me = 2 * x + y
        for i in range(n):
            h = None if i in whole else c
            for j, chip in enumerate(chips):
                blk = _gather_window(src[i], axes[i], shard_shapes[i], me, h)
                _remote(blk, blk, send_sems.at[3 * i + j], recv_sems.at[3 * i + j], (*chip, c)).start()
        token[...] = jnp.zeros_like(token)

    res = pl.pallas_call(
        body, name="ag_start",
        out_shape=(pltpu.SemaphoreType.DMA((3 * n,)), pltpu.SemaphoreType.DMA((3 * n,)),
                   *[pltpu.HBM(f.shape, f.dtype) for f in fulls], _sds((8, LANES), F32)),
        in_specs=[HBM] * n, out_specs=(SEM, SEM, *[HBM] * n, pl.BlockSpec(memory_space=pltpu.VMEM)),
        input_output_aliases={i: 2 + i for i in range(n)},
        compiler_params=pltpu.CompilerParams(has_side_effects=EFFECT),
    )(*[_in_hbm(f) for f in fulls])
    return res[0], res[1], list(res[2:2 + n]), res[2 + n]


def _ag_mid(bufs, idxs, axes, shard_shapes, whole, send_sems, recv_sems, after, name):
    ng = len(bufs)

    def body(*refs):
        src = refs[:ng]
        s_in, r_in = refs[ng], refs[ng + 1]
        fsend, frecv = refs[ng + 3], refs[ng + 4]
        x, y, c, chips = _place()
        me = 2 * x + y
        sib = (x, y, 1 - c)
        for k, i in enumerate(idxs):
            h = None if i in whole else c
            for j, chip in enumerate(chips):
                cj = 2 * chip[0] + chip[1]
                mine = _gather_window(src[k], axes[i], shard_shapes[i], me, h)
                theirs = _gather_window(src[k], axes[i], shard_shapes[i], cj, h)
                _remote(theirs, theirs, s_in.at[3 * i + j], r_in.at[3 * i + j], (*chip, c)).wait_recv()
                _remote(mine, mine, s_in.at[3 * i + j], r_in.at[3 * i + j], (*chip, c)).wait_send()
                if i not in whole:
                    _remote(theirs, theirs, fsend.at[3 * k + j], frecv.at[3 * k + j], sib).start()

    res = pl.pallas_call(
        body, name=name,
        out_shape=(pltpu.SemaphoreType.DMA((3 * ng,)), pltpu.SemaphoreType.DMA((3 * ng,)),
                   *[pltpu.HBM(b.shape, b.dtype) for b in bufs]),
        in_specs=[HBM] * ng + [SEM, SEM, ANY], out_specs=(SEM, SEM, *[HBM] * ng),
        input_output_aliases={k: 2 + k for k in range(ng)},
        compiler_params=pltpu.CompilerParams(has_side_effects=EFFECT),
    )(*bufs, send_sems, recv_sems, after)
    return res[0], res[1], list(res[2:])


def _ag_wait(bufs, idxs, axes, shard_shapes, whole, fsend, frecv, after, name):
    ng = len(bufs)

    def body(*refs):
        src = refs[:ng]
        s_in, r_in = refs[ng], refs[ng + 1]
        x, y, c, chips = _place()
        sib = (x, y, 1 - c)
        for k, i in enumerate(idxs):
            if i in whole:
                continue
            for j, chip in enumerate(chips):
                cj = 2 * chip[0] + chip[1]
                sent = _gather_window(src[k], axes[i], shard_shapes[i], cj, c)
                landed = _gather_window(src[k], axes[i], shard_shapes[i], cj, 1 - c)
                _remote(landed, landed, s_in.at[3 * k + j], r_in.at[3 * k + j], sib).wait_recv()
                _remote(sent, sent, s_in.at[3 * k + j], r_in.at[3 * k + j], sib).wait_send()

    res = pl.pallas_call(
        body, name=name,
        out_shape=tuple(pltpu.HBM(b.shape, b.dtype) for b in bufs),
        in_specs=[HBM] * ng + [SEM, SEM, ANY], out_specs=tuple([HBM] * ng),
        input_output_aliases={k: k for k in range(ng)},
        compiler_params=pltpu.CompilerParams(has_side_effects=EFFECT),
    )(*bufs, fsend, frecv, after)
    return list(res)


class _Schedule:
    GROUPS = ((("w_in", "conv_w"), "start", "u1"),
              (("w_attn_proj", "w_conv_proj", "w_mix_out"), "proj", "conv_o"),
              (("w_xq", "w_xkv", "w_xo"), "ya", "h1"),
              (("w_ffn_in",), "u2", "h2"),
              (("w_ffn_out",), "u3", "hid"))
    REDUCE = ((("w_ffn_out",), "dact", "grad:w_ffn_in", "du3"),
              (("w_ffn_in",), "du3", "grad:w_mix_out", "grad_x"),
              (("w_xo", "w_xq", "w_xkv"), "dh1", "dconv_o", "grad_x"),
              (("w_mix_out", "w_attn_proj", "w_conv_proj"), "dconv_o", "dq", "grad_x"),
              (("w_in",), "du1", "end", "end2"))

    def __init__(self, seeded, axes, shard_shapes, place, on_ready):
        self.ax, self.shape, self.place, self.on_ready = axes, shard_shapes, place, on_ready
        self.stage = {}
        self.order = [nm for names, _, _ in self.GROUPS for nm in names]
        self.axes = [axes[nm] for nm in self.order]
        self.shapes = [shard_shapes[nm] for nm in self.order]
        self.whole = (self.order.index("conv_w"),)
        self.send, self.recv, bufs, self.token = _ag_start(
            [seeded[nm] for nm in self.order], self.axes, self.shapes, self.whole)
        self.buf = dict(zip(self.order, bufs))
        self.passes = {}
        self.ready = set()
        self.grads = {}
        self.mark("start", self.token)

    def mark(self, tag, value):
        for g, (names, mid, wait) in enumerate(self.GROUPS):
            idxs = [self.order.index(nm) for nm in names]
            if tag == mid:
                fs, fr, bufs = _ag_mid([self.buf[nm] for nm in names], idxs, self.axes, self.shapes, self.whole,
                                       self.send, self.recv, value, "ag_mid_%d" % g)
                self.buf.update(zip(names, bufs))
                self.passes[g] = (fs, fr)
            if tag == wait:
                fs, fr = self.passes[g]
                bufs = _ag_wait([self.buf[nm] for nm in names], idxs, self.axes, self.shapes, self.whole, fs, fr,
                                value, "ag_wait_%d" % g)
                self.buf.update(zip(names, bufs))
                self.ready.update(names)
        for g, (names, send, total, finish) in enumerate(self.REDUCE):
            st = self.stage.get(g)
            if st is None:
                continue
            ng = len(names)
            if tag == send and st["at"] == "pair":
                arrs = _exchange_wait("rs_pair_wait_%d" % g, st["arrs"], *st["sems"], st["plan"], value)
                parts = [_pair_add(arrs[k], arrs[ng + k], self.ax[nm], self.shape[nm], self.place, "pair_add_" + nm)
                         for k, nm in enumerate(names)]
                plan, nsem = _plan_chip(ng)
                ss, rs, arrs = _exchange_start("rs_chip_start_%d" % g, parts + [lax.empty(p.shape, p.dtype) for p in parts],
                                               nsem, plan)
                self.stage[g] = dict(at="chip", arrs=arrs, sems=(ss, rs), plan=plan)
            elif tag == total and st["at"] == "chip":
                arrs = _exchange_wait("rs_chip_wait_%d" % g, st["arrs"], *st["sems"], st["plan"], value)
                halves = [_chip_add(arrs[k], arrs[ng + k], self.place, "chip_add_" + nm) for k, nm in enumerate(names)]
                plan, nsem = _plan_gather(ng)
                ss, rs, arrs = _exchange_start("rs_gather_start_%d" % g, halves, nsem, plan)
                self.stage[g] = dict(at="gather", arrs=arrs, sems=(ss, rs), plan=plan)
            elif tag == finish and st["at"] == "gather":
                arrs = _exchange_wait("rs_gather_wait_%d" % g, st["arrs"], *st["sems"], st["plan"], value)
                self.stage[g] = dict(at="done")
                for nm, shard in zip(names, arrs):
                    self.on_ready(nm, shard)
        return value

    def get(self, name):
        assert name in self.ready, name
        return self.buf[name]

    def grad(self, name, g):
        self.grads[name] = g
        for gi, (names, _, _, _) in enumerate(self.REDUCE):
            if name == names[-1]:
                gs = [self.grads[nm] for nm in names]
                plan, nsem = _plan_pair(len(names), [self.ax[nm] for nm in names], [self.shape[nm] for nm in names])
                ss, rs, arrs = _exchange_start("rs_pair_start_%d" % gi, gs + [lax.empty(a.shape, a.dtype) for a in gs],
                                               nsem, plan)
                self.stage[gi] = dict(at="pair", arrs=arrs, sems=(ss, rs), plan=plan)
        self.mark("grad:" + name, g)


def _exchange_start(name, arrays, nsem, plan):
    n = len(arrays)

    def body(*refs):
        send_sems, recv_sems, token = refs[n], refs[n + 1], refs[2 * n + 2]
        sends, _ = plan(refs[:n])
        for k, (src, dst, dev) in enumerate(sends):
            _remote(src, dst, send_sems.at[k], recv_sems.at[k], dev).start()
        token[...] = jnp.zeros_like(token)

    res = pl.pallas_call(
        body, name=name,
        out_shape=(pltpu.SemaphoreType.DMA((nsem,)), pltpu.SemaphoreType.DMA((nsem,)),
                   *[pltpu.HBM(a.shape, a.dtype) for a in arrays], _sds((8, LANES), F32)),
        in_specs=[HBM] * n, out_specs=(SEM, SEM, *[HBM] * n, pl.BlockSpec(memory_space=pltpu.VMEM)),
        input_output_aliases={i: 2 + i for i in range(n)},
        compiler_params=pltpu.CompilerParams(has_side_effects=EFFECT),
    )(*[_in_hbm(a) for a in arrays])
    return res[0], res[1], list(res[2:2 + n])


def _exchange_wait(name, arrays, send_sems, recv_sems, plan, after):
    n = len(arrays)

    def body(*refs):
        s_in, r_in = refs[n], refs[n + 1]
        sends, recvs = plan(refs[:n])
        for k, land in enumerate(recvs):
            _remote(land, land, s_in.at[k], r_in.at[k], sends[k][2]).wait_recv()
        for k, (src, _, dev) in enumerate(sends):
            _remote(src, src, s_in.at[k], r_in.at[k], dev).wait_send()

    res = pl.pallas_call(
        body, name=name,
        out_shape=tuple(pltpu.HBM(a.shape, a.dtype) for a in arrays),
        in_specs=[HBM] * n + [SEM, SEM, ANY], out_specs=tuple([HBM] * n),
        input_output_aliases={i: i for i in range(n)},
        compiler_params=pltpu.CompilerParams(has_side_effects=EFFECT),
    )(*arrays, send_sems, recv_sems, after)
    return list(res)


def _plan_pair(n, axes, shard_shapes):
    def plan(refs):
        g, ra = refs[:n], refs[n:]
        x, y, c, _ = _place()
        sib = (x, y, 1 - c)

        def pieces(ref, i, h):
            if axes[i] == 1:
                return [_half(ref, h)]
            return [_window(ref, 0, shard_shapes[i], s, h) for s in range(N_CHIPS)]

        sends, recvs = [], []
        for i in range(n):
            sends += [(src, dst, sib) for src, dst in zip(pieces(g[i], i, 1 - c), pieces(ra[i], i, 1 - c))]
            recvs += pieces(ra[i], i, c)
        return sends, recvs

    return plan, sum(1 if ax == 1 else N_CHIPS for ax in axes)


def _plan_chip(n):
    def plan(refs):
        p, rc = refs[:n], refs[n:]
        x, y, c, chips = _place()
        me = 2 * x + y
        sends, recvs = [], []
        for i in range(n):
            for chip in chips:
                cj = 2 * chip[0] + chip[1]
                sends.append((p[i].at[cj], rc[i].at[me], (*chip, c)))
                recvs.append(rc[i].at[cj])
        return sends, recvs

    return plan, 3 * n


def _plan_gather(n):
    def plan(refs):
        x, y, c, _ = _place()
        sib = (x, y, 1 - c)
        return ([(_half(r, c), _half(r, c), sib) for r in refs], [_half(r, 1 - c) for r in refs])

    return plan, n


def _pair_add(g, ra, ax, shard_shape, place, name):
    sr, sc = shard_shape
    hr = sr // 2
    wc = sc
    tr = _pick(hr, (256, 352, 128, 64, 32, 16))
    nr = hr // tr

    def body(p_ref, a_ref, b_ref, o_ref):
        o_ref[...] = (a_ref[...].astype(F32) + b_ref[...].astype(F32)).astype(o_ref.dtype)

    if ax == 1:
        src = pl.BlockSpec((tr, wc), lambda s, r, p_ref: (p_ref[0] * nr + r, s))
    else:
        src = pl.BlockSpec((tr, wc), lambda s, r, p_ref: (s * 2 * nr + p_ref[0] * nr + r, 0))
    return pl.pallas_call(
        body, name=name,
        grid_spec=pltpu.PrefetchScalarGridSpec(
            num_scalar_prefetch=1, grid=(N_CHIPS, nr), in_specs=[src, src],
            out_specs=pl.BlockSpec((None, tr, wc), lambda s, r, p_ref: (s, r, 0))),
        out_shape=_sds((N_CHIPS, hr, wc), BF16),
        compiler_params=_cparams("parallel", "parallel"),
    )(place, g, ra)


def _chip_add(part, rc, place, name):
    _, hr, wc = rc.shape
    tr = _pick(hr, (256, 352, 128, 64, 32, 16))
    nr = hr // tr

    def body(p_ref, own_ref, r1_ref, r2_ref, r3_ref, o_ref):
        acc = own_ref[...].astype(F32)
        for r_ref in (r1_ref, r2_ref, r3_ref):
            acc = acc + r_ref[...].astype(F32)
        o_ref[...] = acc

    def slot(k):
        return pl.BlockSpec((None, tr, wc), lambda r, p_ref: ((p_ref[1] + k) % N_CHIPS, r, 0))

    return pl.pallas_call(
        body, name=name,
        grid_spec=pltpu.PrefetchScalarGridSpec(
            num_scalar_prefetch=1, grid=(nr,), in_specs=[slot(0), slot(1), slot(2), slot(3)],
            out_specs=pl.BlockSpec((tr, wc), lambda r, p_ref: (p_ref[0] * nr + r, 0))),
        out_shape=_sds((2 * hr, wc), F32),
        compiler_params=_cparams("parallel"),
    )(place, part, rc, rc, rc)


N_DEV = 8


def _all_reduce_small(buf):
    r, cdim = buf.shape

    def body(x_ref, o_ref, land, send_sems, recv_sems):
        x, y, c, _ = _place()
        me = 4 * x + 2 * y + c
        land[me] = x_ref[...]
        sends = []
        for k in range(1, N_DEV):
            kx, ky, kc = (k >> 2) & 1, (k >> 1) & 1, k & 1
            peer = (1 - x if kx else x, 1 - y if ky else y, 1 - c if kc else c)
            cp = _remote(x_ref, land.at[me], send_sems.at[k - 1], recv_sems.at[k - 1], peer)
            cp.start()
            sends.append(cp)
        for k in range(1, N_DEV):
            kx, ky, kc = (k >> 2) & 1, (k >> 1) & 1, k & 1
            peer = (1 - x if kx else x, 1 - y if ky else y, 1 - c if kc else c)
            pidx = 4 * peer[0] + 2 * peer[1] + peer[2]
            _remote(land.at[pidx], land.at[pidx], send_sems.at[k - 1], recv_sems.at[k - 1], peer).wait_recv()
        for cp in sends:
            cp.wait_send()
        acc = land[0]
        for dev in range(1, N_DEV):
            acc = acc + land[dev]
        o_ref[...] = acc

    vm = pl.BlockSpec(memory_space=pltpu.VMEM)
    return pl.pallas_call(
        body, name="all_reduce_small", in_specs=[vm], out_specs=vm, out_shape=_sds((r, cdim), F32),
        scratch_shapes=[pltpu.VMEM((N_DEV, r, cdim), F32), pltpu.SemaphoreType.DMA((N_DEV - 1,)),
                        pltpu.SemaphoreType.DMA((N_DEV - 1,))],
    )(buf)


SMALL_ROWS = 16


def kernel(x, mem, g_mix, w_in, conv_w, attn_sinks, w_attn_proj, w_conv_proj, w_mix_out, g_xattn, g_mem, w_xq, w_xkv, w_xo, g_ffn, w_ffn_in, w_ffn_out, g_final, loss_target, m_g_mix, m_w_in, m_conv_w, m_attn_sinks, m_w_attn_proj, m_w_conv_proj, m_w_mix_out, m_g_xattn, m_g_mem, m_w_xq, m_w_xkv, m_w_xo, m_g_ffn, m_w_ffn_in, m_w_ffn_out, m_g_final, v_g_mix, v_w_in, v_conv_w, v_attn_sinks, v_w_attn_proj, v_w_conv_proj, v_w_mix_out, v_g_xattn, v_g_mem, v_w_xq, v_w_xkv, v_w_xo, v_g_ffn, v_w_ffn_in, v_w_ffn_out, v_g_final):
    w = dict(g_mix=g_mix, w_in=w_in[0], conv_w=conv_w[0], attn_sinks=attn_sinks, w_attn_proj=w_attn_proj[0],
             w_conv_proj=w_conv_proj[0], w_mix_out=w_mix_out[0], g_xattn=g_xattn, g_mem=g_mem, w_xq=w_xq[0],
             w_xkv=w_xkv[0], w_xo=w_xo[0], g_ffn=g_ffn, w_ffn_in=w_ffn_in[0], w_ffn_out=w_ffn_out[0],
             g_final=g_final[None])
    m = dict(g_mix=m_g_mix, w_in=m_w_in[0], conv_w=m_conv_w[0], attn_sinks=m_attn_sinks,
             w_attn_proj=m_w_attn_proj[0], w_conv_proj=m_w_conv_proj[0], w_mix_out=m_w_mix_out[0],
             g_xattn=m_g_xattn, g_mem=m_g_mem, w_xq=m_w_xq[0], w_xkv=m_w_xkv[0], w_xo=m_w_xo[0], g_ffn=m_g_ffn,
             w_ffn_in=m_w_ffn_in[0], w_ffn_out=m_w_ffn_out[0], g_final=m_g_final[None])
    v = dict(g_mix=v_g_mix, w_in=v_w_in[0], conv_w=v_conv_w[0], attn_sinks=v_attn_sinks,
             w_attn_proj=v_w_attn_proj[0], w_conv_proj=v_w_conv_proj[0], w_mix_out=v_w_mix_out[0],
             g_xattn=v_g_xattn, g_mem=v_g_mem, w_xq=v_w_xq[0], w_xkv=v_w_xkv[0], w_xo=v_w_xo[0], g_ffn=v_g_ffn,
             w_ffn_in=v_w_ffn_in[0], w_ffn_out=v_w_ffn_out[0], g_final=v_g_final[None])
    names = [nm for nm, _ in BIG]
    axes = [ax for _, ax in BIG]
    d = x.shape[2]
    cw = w["conv_w"].shape[1] * N_CHIPS
    chip = (2 * lax.axis_index("x") + lax.axis_index("y")).astype(jnp.int32)
    place = jnp.stack([lax.axis_index("c").astype(jnp.int32), chip])
    shard_shapes = [w[nm].shape for nm in names]

    me1 = chip.reshape(1)
    seeded = {nm: _cast_to_full(w[nm], ax, me1, "cast_" + nm) for nm, ax in zip(names, axes)}
    seeded["conv_w"] = _cast_to_full(w["conv_w"], 1, me1, "place_conv_w", F32)
    upd = {}

    def on_ready(nm, shard):
        grads[nm] = shard
        upd[nm] = _adamw(w[nm], shard, m[nm], v[nm], "adamw_" + nm)

    grads = {}
    wts = _Schedule(seeded, dict(zip(names + ["conv_w"], axes + [1])),
                    dict(zip(names + ["conv_w"], shard_shapes + [w["conv_w"].shape])), place, on_ready)
    aw, cw = w["w_attn_proj"].shape[0], w["w_conv_proj"].shape[0]
    kvw = (w["w_in"].shape[1] * N_CHIPS - aw - 3 * cw - 2 * d) // 2
    grad_x, small = _local_step(
        x[0], mem[0], loss_target[0], w["g_mix"] + wts.token[0:1, 0:1], w["attn_sinks"], w["g_xattn"], w["g_mem"],
        w["g_ffn"], w["g_final"], (aw, cw, kvw), wts)

    pw = max(d, cw)

    def row(a):
        return jnp.pad(a, ((0, 0), (0, pw - a.shape[1])))

    gains = ("g_mix", "g_xattn", "g_mem", "g_ffn", "g_final")
    packed = jnp.concatenate(
        [row(small[nm]) for nm in gains] + [row(small["conv_w"]),
         row(jnp.concatenate([small["attn_sinks"], small["loss"]], axis=1)),
         jnp.zeros((SMALL_ROWS - 9, pw), F32)], axis=0)
    total = _all_reduce_small(packed)
    wts.mark("end", total)
    nsink = attn_sinks.shape[1]
    grads.update({nm: total[k:k + 1, :d] for k, nm in enumerate(gains)})
    grads.update(conv_w=lax.dynamic_slice(total, (5, chip * (cw // N_CHIPS)), (3, cw // N_CHIPS)),
                 attn_sinks=total[8:9, :nsink])
    loss = total[8, nsink]
    for nm in gains + ("conv_w", "attn_sinks"):
        upd[nm] = _adamw(w[nm], grads[nm], m[nm], v[nm], "adamw_" + nm)
    wts.mark("end2", upd["g_final"][0])

    order = ["g_mix", "w_in", "conv_w", "attn_sinks", "w_attn_proj", "w_conv_proj", "w_mix_out", "g_xattn", "g_mem",
             "w_xq", "w_xkv", "w_xo", "g_ffn", "w_ffn_in", "w_ffn_out", "g_final"]

    stacked = set(names) | {"conv_w"}

    def shaped(nm, a):
        if nm == "g_final":
            return a[0]
        return a[None] if nm in stacked else a

    outs = [loss, grad_x[None]]
    outs += [shaped(nm, grads[nm]) for nm in order]
    for k in range(3):
        outs += [shaped(nm, upd[nm][k]) for nm in order]
    return tuple(outs)
```

```python
import functools

import jax
import jax.numpy as jnp
from jax import lax
from jax.experimental import pallas as pl
from jax.experimental.pallas import tpu as pltpu

F32 = jnp.float32
BF16 = jnp.bfloat16

VMEM_LIMIT_BYTES = 56 * 1024 * 1024
LANES = 128
HEAD_DIM = 64
BLOCK = 128
X_HEAD_DIM = 128
ROPE_THETA = 10000.0
EPS = 1e-6
NEG = -1e30
ADAM_LR, ADAM_B1, ADAM_B2, ADAM_EPS, ADAM_WD, ADAM_STEP = 0.001, 0.9, 0.999, 1e-08, 0.01, 10
N_CHIPS = 4
MESH = pl.DeviceIdType.MESH
ANY = pl.BlockSpec(memory_space=pl.ANY)


def _pick(dim, prefs):
    for p in prefs:
        if dim % p == 0:
            return p
    return dim


def _cparams(*sem):
    return pltpu.CompilerParams(dimension_semantics=sem, vmem_limit_bytes=VMEM_LIMIT_BYTES)


def _sds(shape, dtype):
    return jax.ShapeDtypeStruct(shape, dtype)


def _sigmoid(v):
    return 1.0 / (1.0 + jnp.exp(-v))


MATMUL_VMEM_BUDGET = 40 * 1024 * 1024


def _tiles(mode, m, n, k, out_bytes, has_res):
    if mode == "tn":
        if m % 1024 == 0:
            return 1024, _pick(n, (512, 256, 128)), k, False
        return _pick(m, (512, 256, 128)), _pick(n, (1024, 512, 256, 128)), k, True
    tm = _pick(m, (1024, 512, 256, 128))
    if k <= 2048:
        return tm, _pick(n, (512, 256, 128)), k, False
    tn = n if (n <= 2048 and not has_res) else _pick(n, (1024, 512, 256, 128))
    fixed = tm * tn * (4 + 2 * out_bytes + (8 if has_res else 0))
    tk = LANES
    for cand in range(LANES, k + 1, LANES):
        if k % cand == 0 and fixed + (tm + tn) * 4 * cand <= MATMUL_VMEM_BUDGET:
            tk = cand
    return tm, tn, tk, False


def _matmul(a, b, *, mode, out_dtype, name, res=None):
    if mode == "nn":
        (m, k), (k2, n) = a.shape, b.shape
    elif mode == "nt":
        (m, k), (n, k2) = a.shape, b.shape
    else:
        (k, m), (k2, n) = a.shape, b.shape
    assert k == k2, (a.shape, b.shape, mode)
    tm, tn, tk, swap = _tiles(mode, m, n, k, jnp.dtype(out_dtype).itemsize, res is not None)
    nk = k // tk
    dims = {"nn": (((1,), (0,)), ((), ())), "nt": (((1,), (1,)), ((), ())), "tn": (((0,), (0,)), ((), ()))}[mode]
    has_res = res is not None

    def body(*refs):
        a_ref, b_ref = refs[0], refs[1]
        r_ref = refs[2] if has_res else None
        o_ref = refs[2 + has_res]
        part = lax.dot_general(a_ref[...], b_ref[...], dims, preferred_element_type=F32)

        def finish(val):
            if has_res:
                val = val + r_ref[...]
            o_ref[...] = val.astype(o_ref.dtype)

        if nk == 1:
            finish(part)
        else:
            acc = refs[3 + has_res]
            kk = pl.program_id(2)

            @pl.when(kk == 0)
            def _():
                acc[...] = part

            @pl.when(kk > 0)
            def _():
                acc[...] += part

            @pl.when(kk == nk - 1)
            def _():
                finish(acc[...])

    def spec(shape, f):
        if swap:
            return pl.BlockSpec(shape, lambda j, i, kk: f(i, j, kk))
        return pl.BlockSpec(shape, f)

    if mode == "tn":
        a_spec = spec((tk, tm), lambda i, j, kk: (kk, i))
    else:
        a_spec = spec((tm, tk), lambda i, j, kk: (i, kk))
    if mode == "nt":
        b_spec = spec((tn, tk), lambda i, j, kk: (j, kk))
    else:
        b_spec = spec((tk, tn), lambda i, j, kk: (kk, j))
    o_spec = spec((tm, tn), lambda i, j, kk: (i, j))
    return pl.pallas_call(
        body,
        name=name,
        grid=(n // tn, m // tm, nk) if swap else (m // tm, n // tn, nk),
        in_specs=[a_spec, b_spec] + ([o_spec] if has_res else []),
        out_specs=o_spec,
        out_shape=_sds((m, n), out_dtype),
        scratch_shapes=[pltpu.VMEM((tm, tn), F32)] if nk > 1 else [],
        compiler_params=_cparams("parallel", "parallel", "arbitrary"),
    )(*([a, b] + ([res] if has_res else [])))


def _rms_fwd(x, g, name):
    t, d = x.shape
    tm = _pick(t, (512, 256, 128))

    def body(x_ref, g_ref, o_ref):
        xf = x_ref[...]
        r = lax.rsqrt(jnp.mean(xf * xf, axis=-1, keepdims=True) + EPS)
        o_ref[...] = (xf * r * g_ref[...]).astype(o_ref.dtype)

    row = pl.BlockSpec((tm, d), lambda i: (i, 0))
    return pl.pallas_call(
        body, name=name, grid=(t // tm,),
        in_specs=[row, pl.BlockSpec((1, d), lambda i: (0, 0))],
        out_specs=row, out_shape=_sds((t, d), BF16),
        compiler_params=_cparams("parallel"),
    )(x, g)


def _rms_bwd_math(xf, g, du):
    r = lax.rsqrt(jnp.mean(xf * xf, axis=-1, keepdims=True) + EPS)
    xh = xf * r
    gdy = g * du
    dx = r * (gdy - xh * jnp.mean(gdy * xh, axis=-1, keepdims=True))
    dg = jnp.sum(du * xh, axis=0, keepdims=True)
    return dx, dg


def _rms_bwd(x, g, du, dh, name):
    t, d = x.shape
    tm = _pick(t, (256, 128))
    has_dh = dh is not None

    def body(*refs):
        x_ref, g_ref, du_ref = refs[0], refs[1], refs[2]
        o_ref, ob_ref, dg_ref = refs[3 + has_dh:]
        dx, dg = _rms_bwd_math(x_ref[...], g_ref[...], du_ref[...].astype(F32))
        if has_dh:
            dx = dx + refs[3][...]
        o_ref[...] = dx
        ob_ref[...] = dx.astype(BF16)

        @pl.when(pl.program_id(0) == 0)
        def _():
            dg_ref[...] = dg

        @pl.when(pl.program_id(0) > 0)
        def _():
            dg_ref[...] += dg

    row = pl.BlockSpec((tm, d), lambda i: (i, 0))
    vec = pl.BlockSpec((1, d), lambda i: (0, 0))
    return pl.pallas_call(
        body, name=name, grid=(t // tm,),
        in_specs=[row, vec, row] + ([row] if has_dh else []),
        out_specs=[row, row, vec],
        out_shape=[_sds((t, d), F32), _sds((t, d), BF16), _sds((1, d), F32)],
        compiler_params=_cparams("arbitrary"),
    )(*([x, g, du] + ([dh] if has_dh else [])))


def _loss_head(h, g, tgt):
    t, d = h.shape
    tm = _pick(t, (256, 128))

    def body(h_ref, g_ref, t_ref, o_ref, ob_ref, dg_ref, l_ref):
        xf = h_ref[...]
        gv = g_ref[...]
        r = lax.rsqrt(jnp.mean(xf * xf, axis=-1, keepdims=True) + EPS)
        err = xf * r * gv - t_ref[...]
        part = 0.5 * jnp.sum(jnp.mean(err * err, axis=-1, keepdims=True), axis=0, keepdims=True)
        dx, dg = _rms_bwd_math(xf, gv, err * (1.0 / d))
        o_ref[...] = dx
        ob_ref[...] = dx.astype(BF16)
        lrow = jnp.broadcast_to(part, (1, LANES))

        @pl.when(pl.program_id(0) == 0)
        def _():
            dg_ref[...] = dg
            l_ref[...] = lrow

        @pl.when(pl.program_id(0) > 0)
        def _():
            dg_ref[...] += dg
            l_ref[...] += lrow

    row = pl.BlockSpec((tm, d), lambda i: (i, 0))
    vec = pl.BlockSpec((1, d), lambda i: (0, 0))
    return pl.pallas_call(
        body, name="loss_head", grid=(t // tm,),
        in_specs=[row, vec, row],
        out_specs=[row, row, vec, pl.BlockSpec((1, LANES), lambda i: (0, 0))],
        out_shape=[_sds((t, d), F32), _sds((t, d), BF16), _sds((1, d), F32), _sds((1, LANES), F32)],
        compiler_params=_cparams("arbitrary"),
    )(h, g, tgt)


def _rope_tables(t):
    half = HEAD_DIM // 2
    inv_freq = ROPE_THETA ** (-jnp.arange(half, dtype=F32) / half)
    ang = jnp.arange(t, dtype=F32)[:, None] * inv_freq[None, :]
    cos = jnp.cos(ang)
    sin = jnp.sin(ang)
    reps = LANES // HEAD_DIM
    cos_t = jnp.tile(jnp.concatenate([cos, cos], axis=1), (1, reps))
    sin_t = jnp.tile(jnp.concatenate([-sin, sin], axis=1), (1, reps))
    return cos_t, sin_t


def _rope(v, cos, sin):
    w = v.shape[1]
    c = jnp.tile(cos, (1, w // LANES))
    s = jnp.tile(sin, (1, w // LANES))
    lane = lax.broadcasted_iota(jnp.int32, v.shape, 1)
    first = (lane % HEAD_DIM) < (HEAD_DIM // 2)
    partner = jnp.where(first, pltpu.roll(v, w - HEAD_DIM // 2, 1), pltpu.roll(v, HEAD_DIM // 2, 1))
    return v * c + partner * s


def _heads(v, first, count):
    return jnp.concatenate([v[:, (first + i) * HEAD_DIM:(first + i + 1) * HEAD_DIM] for i in range(count)], axis=0)


def _attn_probs(qs, kb, n, h, qpk, sinks_ref):
    s = lax.dot_general(qs, kb, (((1,), (1,)), ((), ())), preferred_element_type=F32) * (HEAD_DIM ** -0.5)
    qi = lax.broadcasted_iota(jnp.int32, (BLOCK, 2 * BLOCK), 0)
    kc = lax.broadcasted_iota(jnp.int32, (BLOCK, 2 * BLOCK), 1)
    valid = (kc > qi) & (kc <= qi + BLOCK) & ((kc >= BLOCK) | (n > 0))
    bias = jnp.tile(jnp.where(valid, 0.0, NEG).astype(F32), (qpk, 1))
    s = s + bias
    rowg = lax.broadcasted_iota(jnp.int32, (qpk * BLOCK, 1), 0) // BLOCK
    sink = jnp.zeros((qpk * BLOCK, 1), F32)
    for g in range(qpk):
        sink = jnp.where(rowg == g, sinks_ref[0, h * qpk + g], sink)
    m = jnp.maximum(jnp.max(s, axis=-1, keepdims=True), sink)
    e = jnp.exp(s - m)
    es = jnp.exp(sink - m)
    inv = 1.0 / (jnp.sum(e, axis=-1, keepdims=True) + es)
    return e * inv, es * inv, rowg


def _attn_specs(aw, kvw):
    koff = aw // kvw
    prev = lambda n: jnp.maximum(n - 1, 0)
    return [
        pl.BlockSpec((BLOCK, aw), lambda n: (n, 0)),
        pl.BlockSpec((BLOCK, kvw), lambda n: (n, koff)),
        pl.BlockSpec((BLOCK, kvw), lambda n: (prev(n), koff)),
        pl.BlockSpec((BLOCK, kvw), lambda n: (n, koff + 1)),
        pl.BlockSpec((BLOCK, kvw), lambda n: (prev(n), koff + 1)),
        pl.BlockSpec((BLOCK, LANES), lambda n: (n, 0)),
        pl.BlockSpec((BLOCK, LANES), lambda n: (n, 0)),
        pl.BlockSpec((BLOCK, LANES), lambda n: (prev(n), 0)),
        pl.BlockSpec((BLOCK, LANES), lambda n: (prev(n), 0)),
        pl.BlockSpec(memory_space=pltpu.SMEM),
    ]


def _attn_fwd(proj, cos, sin, sinks, aw, kvw):
    t = proj.shape[0]
    nkv = kvw // HEAD_DIM
    qpk = aw // kvw

    def body(q_ref, kc_ref, kp_ref, vc_ref, vp_ref, cc_ref, sc_ref, cp_ref, sp_ref, sinks_ref, o_ref):
        n = pl.program_id(0)
        q = _rope(q_ref[...].astype(F32), cc_ref[...], sc_ref[...]).astype(BF16)
        kc = _rope(kc_ref[...].astype(F32), cc_ref[...], sc_ref[...]).astype(BF16)
        kp = _rope(kp_ref[...].astype(F32), cp_ref[...], sp_ref[...]).astype(BF16)
        vc = vc_ref[...].astype(BF16)
        vp = vp_ref[...].astype(BF16)
        outs = []
        for h in range(nkv):
            hs = slice(h * HEAD_DIM, (h + 1) * HEAD_DIM)
            kb = jnp.concatenate([kp[:, hs], kc[:, hs]], axis=0)
            vb = jnp.concatenate([vp[:, hs], vc[:, hs]], axis=0)
            p, _, _ = _attn_probs(_heads(q, h * qpk, qpk), kb, n, h, qpk, sinks_ref)
            o = jnp.dot(p.astype(BF16), vb, preferred_element_type=F32)
            outs += [o[g * BLOCK:(g + 1) * BLOCK] for g in range(qpk)]
        o_ref[...] = jnp.concatenate(outs, axis=1).astype(o_ref.dtype)

    return pl.pallas_call(
        body, name="attn_fwd", grid=(t // BLOCK,),
        in_specs=_attn_specs(aw, kvw),
        out_specs=pl.BlockSpec((BLOCK, aw), lambda n: (n, 0)),
        out_shape=_sds((t, aw), BF16),
        compiler_params=_cparams("parallel"),
    )(proj, proj, proj, proj, proj, cos, sin, cos, sin, sinks)


def _attn_bwd(proj, cos, sin, sinks, do, aw, kvw):
    t = proj.shape[0]
    nkv = kvw // HEAD_DIM
    qpk = aw // kvw
    scale = HEAD_DIM ** -0.5

    def body(q_ref, kc_ref, kp_ref, vc_ref, vp_ref, cc_ref, sc_ref, cp_ref, sp_ref, sinks_ref, do_ref,
             dq_ref, dkc_ref, dkp_ref, dvc_ref, dvp_ref, ds_ref):
        n = pl.program_id(0)
        cc, sc, cp, sp = cc_ref[...], sc_ref[...], cp_ref[...], sp_ref[...]
        q = _rope(q_ref[...].astype(F32), cc, sc).astype(BF16)
        kc = _rope(kc_ref[...].astype(F32), cc, sc).astype(BF16)
        kp = _rope(kp_ref[...].astype(F32), cp, sp).astype(BF16)
        vc = vc_ref[...].astype(BF16)
        vp = vp_ref[...].astype(BF16)
        dout = do_ref[...]
        dqs, dks, dvs = [], [], []
        lane = lax.broadcasted_iota(jnp.int32, (8, LANES), 1)
        row0 = lax.broadcasted_iota(jnp.int32, (8, LANES), 0) == 0
        dsink = jnp.zeros((8, LANES), F32)
        for h in range(nkv):
            hs = slice(h * HEAD_DIM, (h + 1) * HEAD_DIM)
            kb = jnp.concatenate([kp[:, hs], kc[:, hs]], axis=0)
            vb = jnp.concatenate([vp[:, hs], vc[:, hs]], axis=0)
            qs = _heads(q, h * qpk, qpk)
            dos = _heads(dout, h * qpk, qpk)
            p, psink, rowg = _attn_probs(qs, kb, n, h, qpk, sinks_ref)
            pb = p.astype(BF16)
            o = jnp.dot(pb, vb, preferred_element_type=F32)
            delta = jnp.sum(dos.astype(F32) * o, axis=-1, keepdims=True)
            dvs.append(lax.dot_general(pb, dos, (((0,), (0,)), ((), ())), preferred_element_type=F32))
            dp = lax.dot_general(dos, vb, (((1,), (1,)), ((), ())), preferred_element_type=F32)
            dsc = (p * (dp - delta)).astype(BF16)
            dq_h = jnp.dot(dsc, kb, preferred_element_type=F32) * scale
            dqs += [dq_h[g * BLOCK:(g + 1) * BLOCK] for g in range(qpk)]
            dks.append(lax.dot_general(dsc, qs, (((0,), (0,)), ((), ())), preferred_element_type=F32) * scale)
            sink_term = psink * delta
            for g in range(qpk):
                val = -jnp.sum(jnp.where(rowg == g, sink_term, 0.0))
                dsink = jnp.where(row0 & (lane == h * qpk + g), val, dsink)
        dq = jnp.concatenate(dqs, axis=1)
        dq_ref[...] = _rope(dq, cc, -sc).astype(dq_ref.dtype)
        dk = jnp.concatenate(dks, axis=1)
        dv = jnp.concatenate(dvs, axis=1)
        dkp_ref[...] = _rope(dk[:BLOCK], cp, -sp)
        dkc_ref[...] = _rope(dk[BLOCK:], cc, -sc)
        dvp_ref[...] = dv[:BLOCK]
        dvc_ref[...] = dv[BLOCK:]

        @pl.when(n == 0)
        def _():
            ds_ref[...] = dsink

        @pl.when(n > 0)
        def _():
            ds_ref[...] += dsink

    kv_spec = pl.BlockSpec((BLOCK, kvw), lambda n: (n, 0))
    return pl.pallas_call(
        body, name="attn_bwd", grid=(t // BLOCK,),
        in_specs=_attn_specs(aw, kvw) + [pl.BlockSpec((BLOCK, aw), lambda n: (n, 0))],
        out_specs=[pl.BlockSpec((BLOCK, aw), lambda n: (n, 0)), kv_spec, kv_spec, kv_spec, kv_spec,
                   pl.BlockSpec((8, LANES), lambda n: (0, 0))],
        out_shape=[_sds((t, aw), BF16)] + [_sds((t, kvw), F32)] * 4 + [_sds((8, LANES), F32)],
        compiler_params=_cparams("arbitrary"),
    )(proj, proj, proj, proj, proj, cos, sin, cos, sin, sinks, do)


HALO = 16


def _shift_down(v, k, halo):
    rows = lax.broadcasted_iota(jnp.int32, v.shape, 0)
    out = pltpu.roll(v, k, 0)
    for r in range(k):
        out = jnp.where(rows == r, halo[HALO - k + r:HALO - k + r + 1, :], out)
    return out


def _shift_up(v, k, halo):
    tm = v.shape[0]
    rows = lax.broadcasted_iota(jnp.int32, v.shape, 0)
    out = pltpu.roll(v, tm - k, 0)
    for r in range(k):
        out = jnp.where(rows == tm - k + r, halo[r:r + 1, :], out)
    return out


def _conv_fwd(proj, conv_w, zoff, cw, cb):
    t = proj.shape[0]
    tm = _pick(t, (512, 256, 128))
    zb, nb = zoff // cb, cw // cb
    hb = tm // HALO

    def body(z_ref, gb_ref, gc_ref, zp_ref, gcp_ref, w_ref, o_ref):
        i = pl.program_id(0)
        cz = gc_ref[...].astype(F32) * z_ref[...].astype(F32)
        czp = gcp_ref[...].astype(F32) * zp_ref[...].astype(F32) * (i > 0).astype(F32)
        w = w_ref[...]
        y = w[0:1] * _shift_down(cz, 2, czp) + w[1:2] * _shift_down(cz, 1, czp) + w[2:3] * cz
        o_ref[...] = (gb_ref[...].astype(F32) * y).astype(o_ref.dtype)

    def col(k):
        return pl.BlockSpec((tm, cb), lambda i, j: (i, zb + k * nb + j))

    def halo(k):
        return pl.BlockSpec((HALO, cb), lambda i, j: (jnp.maximum(i * hb - 1, 0), zb + k * nb + j))

    return pl.pallas_call(
        body, name="conv_fwd", grid=(t // tm, nb),
        in_specs=[col(0), col(1), col(2), halo(0), halo(2), pl.BlockSpec((3, cb), lambda i, j: (0, j))],
        out_specs=pl.BlockSpec((tm, cb), lambda i, j: (i, j)),
        out_shape=_sds((t, cw), BF16),
        compiler_params=_cparams("parallel", "parallel"),
    )(proj, proj, proj, proj, proj, conv_w)


def _conv_bwd(proj, conv_w, dco, zoff, cw, cb):
    t = proj.shape[0]
    tm = _pick(t, (512, 256, 128))
    zb, nb = zoff // cb, cw // cb
    hb = tm // HALO
    nt = t // tm

    def body(z_ref, gb_ref, gc_ref, zp_ref, gcp_ref, gbn_ref, w_ref, d_ref, dn_ref, dz_ref, dgb_ref, dgc_ref, dw_ref):
        i = pl.program_id(1)
        z, gb, gc = z_ref[...].astype(F32), gb_ref[...].astype(F32), gc_ref[...].astype(F32)
        d = d_ref[...].astype(F32)
        cz = gc * z
        czp = gcp_ref[...].astype(F32) * zp_ref[...].astype(F32) * (i > 0).astype(F32)
        w = w_ref[...]
        cz1 = _shift_down(cz, 1, czp)
        cz2 = _shift_down(cz, 2, czp)
        y = w[0:1] * cz2 + w[1:2] * cz1 + w[2:3] * cz
        dgb_ref[...] = (d * y).astype(dgb_ref.dtype)
        dy = d * gb
        dyn = dn_ref[...].astype(F32) * gbn_ref[...].astype(F32) * (i < nt - 1).astype(F32)
        dcz = w[2:3] * dy + w[1:2] * _shift_up(dy, 1, dyn) + w[0:1] * _shift_up(dy, 2, dyn)
        dgc_ref[...] = (dcz * z).astype(dgc_ref.dtype)
        dz_ref[...] = (dcz * gc).astype(dz_ref.dtype)
        rows = lax.broadcasted_iota(jnp.int32, (8, cb), 0)
        dw = jnp.zeros((8, cb), F32)
        for r, tap in enumerate((cz2, cz1, cz)):
            dw = jnp.where(rows == r, jnp.sum(dy * tap, axis=0, keepdims=True), dw)

        @pl.when(i == 0)
        def _():
            dw_ref[...] = dw

        @pl.when(i > 0)
        def _():
            dw_ref[...] += dw

    def col(k):
        return pl.BlockSpec((tm, cb), lambda j, i: (i, zb + k * nb + j))

    def halo_prev(k):
        return pl.BlockSpec((HALO, cb), lambda j, i: (jnp.maximum(i * hb - 1, 0), zb + k * nb + j))

    own = pl.BlockSpec((tm, cb), lambda j, i: (i, j))
    nxt = lambda i: jnp.minimum((i + 1) * hb, t // HALO - 1)
    return pl.pallas_call(
        body, name="conv_bwd", grid=(nb, nt),
        in_specs=[col(0), col(1), col(2), halo_prev(0), halo_prev(2),
                  pl.BlockSpec((HALO, cb), lambda j, i: (nxt(i), zb + nb + j)),
                  pl.BlockSpec((3, cb), lambda j, i: (0, j)), own,
                  pl.BlockSpec((HALO, cb), lambda j, i: (nxt(i), j))],
        out_specs=[own, own, own, pl.BlockSpec((8, cb), lambda j, i: (0, j))],
        out_shape=[_sds((t, cw), BF16)] * 3 + [_sds((8, cw), F32)],
        compiler_params=_cparams("parallel", "arbitrary"),
    )(proj, proj, proj, proj, proj, proj, conv_w, dco, dco)


def _merge_fwd(proj, ya, yc, goff, cb):
    t, d = ya.shape
    tm = _pick(t, (512, 256, 128))
    gb_, nb = goff // cb, d // cb

    def body(ga_ref, gc_ref, ya_ref, yc_ref, o_ref):
        f = lambda r: r[...].astype(F32)
        o_ref[...] = (_sigmoid(f(ga_ref)) * f(ya_ref) + _sigmoid(f(gc_ref)) * f(yc_ref)).astype(o_ref.dtype)

    own = pl.BlockSpec((tm, cb), lambda i, j: (i, j))
    return pl.pallas_call(
        body, name="merge_fwd", grid=(t // tm, nb),
        in_specs=[pl.BlockSpec((tm, cb), lambda i, j: (i, gb_ + j)),
                  pl.BlockSpec((tm, cb), lambda i, j: (i, gb_ + nb + j)), own, own],
        out_specs=own, out_shape=_sds((t, d), BF16),
        compiler_params=_cparams("parallel", "parallel"),
    )(proj, proj, ya, yc)


def _merge_bwd(proj, ya, yc, dm, goff, cb):
    t, d = ya.shape
    tm = _pick(t, (512, 256, 128))
    gb_, nb = goff // cb, d // cb

    def body(ga_ref, gc_ref, ya_ref, yc_ref, dm_ref, dya_ref, dyc_ref, dga_ref, dgc_ref):
        dmv = dm_ref[...].astype(F32)
        sa = _sigmoid(ga_ref[...].astype(F32))
        sc = _sigmoid(gc_ref[...].astype(F32))
        dya_ref[...] = (dmv * sa).astype(BF16)
        dyc_ref[...] = (dmv * sc).astype(BF16)
        dga_ref[...] = (dmv * ya_ref[...].astype(F32) * sa * (1.0 - sa)).astype(BF16)
        dgc_ref[...] = (dmv * yc_ref[...].astype(F32) * sc * (1.0 - sc)).astype(BF16)

    own = pl.BlockSpec((tm, cb), lambda i, j: (i, j))
    return pl.pallas_call(
        body, name="merge_bwd", grid=(t // tm, nb),
        in_specs=[pl.BlockSpec((tm, cb), lambda i, j: (i, gb_ + j)),
                  pl.BlockSpec((tm, cb), lambda i, j: (i, gb_ + nb + j)), own, own, own],
        out_specs=[own] * 4, out_shape=[_sds((t, d), BF16)] * 4,
        compiler_params=_cparams("parallel", "parallel"),
    )(proj, proj, ya, yc, dm)


def _assemble_dproj(dq, dkc, dkp, dvc, dvp, dz, dgb, dgc, dga, dgg):
    t, aw = dq.shape
    kvw, cw, d = dkc.shape[1], dz.shape[1], dga.shape[1]
    nblk = t // BLOCK
    width = aw + 2 * kvw + 3 * cw + 2 * d

    def body(dq_ref, dkc_ref, dkp_ref, dvc_ref, dvp_ref, dz_ref, dgb_ref, dgc_ref, dga_ref, dgg_ref, o_ref):
        keep = (pl.program_id(0) < nblk - 1).astype(F32)
        dk = dkc_ref[...] + dkp_ref[...] * keep
        dv = dvc_ref[...] + dvp_ref[...] * keep
        o_ref[...] = jnp.concatenate(
            [dq_ref[...], dk.astype(BF16), dv.astype(BF16), dz_ref[...], dgb_ref[...], dgc_ref[...],
             dga_ref[...], dgg_ref[...]], axis=1)

    def cur(w):
        return pl.BlockSpec((BLOCK, w), lambda n: (n, 0))

    def nxt(w):
        return pl.BlockSpec((BLOCK, w), lambda n: (jnp.minimum(n + 1, nblk - 1), 0))

    return pl.pallas_call(
        body, name="assemble_dproj", grid=(nblk,),
        in_specs=[cur(aw), cur(kvw), nxt(kvw), cur(kvw), nxt(kvw), cur(cw), cur(cw), cur(cw), cur(d), cur(d)],
        out_specs=cur(width), out_shape=_sds((t, width), BF16),
        compiler_params=_cparams("parallel"),
    )(dq, dkc, dkp, dvc, dvp, dz, dgb, dgc, dga, dgg)


def _xattn_probs(qh, kh):
    s = lax.dot_general(qh, kh, (((1,), (1,)), ((), ())), preferred_element_type=F32) * (X_HEAD_DIM ** -0.5)
    e = jnp.exp(s - jnp.max(s, axis=-1, keepdims=True))
    return e * (1.0 / jnp.sum(e, axis=-1, keepdims=True))


def _xattn_fwd(xq, kv):
    t, xw = xq.shape
    mt = kv.shape[0]
    tm = _pick(t, (512, 256, 128))

    def body(q_ref, kv_ref, o_ref):
        outs = []
        for hd in range(xw // X_HEAD_DIM):
            hs = slice(hd * X_HEAD_DIM, (hd + 1) * X_HEAD_DIM)
            vs = slice(xw + hd * X_HEAD_DIM, xw + (hd + 1) * X_HEAD_DIM)
            p = _xattn_probs(q_ref[:, hs], kv_ref[:, hs])
            outs.append(jnp.dot(p.astype(BF16), kv_ref[:, vs], preferred_element_type=F32))
        o_ref[...] = jnp.concatenate(outs, axis=1).astype(o_ref.dtype)

    return pl.pallas_call(
        body, name="xattn_fwd", grid=(t // tm,),
        in_specs=[pl.BlockSpec((tm, xw), lambda i: (i, 0)), pl.BlockSpec((mt, 2 * xw), lambda i: (0, 0))],
        out_specs=pl.BlockSpec((tm, xw), lambda i: (i, 0)), out_shape=_sds((t, xw), BF16),
        compiler_params=_cparams("parallel"),
    )(xq, kv)


def _xattn_bwd(xq, kv, do):
    t, xw = xq.shape
    mt = kv.shape[0]
    tm = _pick(t, (512, 256, 128))
    scale = X_HEAD_DIM ** -0.5

    def body(q_ref, kv_ref, do_ref, dq_ref, dkv_ref):
        dqs, dks, dvs = [], [], []
        for hd in range(xw // X_HEAD_DIM):
            hs = slice(hd * X_HEAD_DIM, (hd + 1) * X_HEAD_DIM)
            vs = slice(xw + hd * X_HEAD_DIM, xw + (hd + 1) * X_HEAD_DIM)
            qh, kh, vh, doh = q_ref[:, hs], kv_ref[:, hs], kv_ref[:, vs], do_ref[:, hs]
            p = _xattn_probs(qh, kh)
            pb = p.astype(BF16)
            o = jnp.dot(pb, vh, preferred_element_type=F32)
            delta = jnp.sum(doh.astype(F32) * o, axis=-1, keepdims=True)
            dvs.append(lax.dot_general(pb, doh, (((0,), (0,)), ((), ())), preferred_element_type=F32))
            dp = lax.dot_general(doh, vh, (((1,), (1,)), ((), ())), preferred_element_type=F32)
            dsc = (p * (dp - delta)).astype(BF16)
            dqs.append(jnp.dot(dsc, kh, preferred_element_type=F32) * scale)
            dks.append(lax.dot_general(dsc, qh, (((0,), (0,)), ((), ())), preferred_element_type=F32) * scale)
        dq_ref[...] = jnp.concatenate(dqs, axis=1).astype(dq_ref.dtype)
        dkv = jnp.concatenate(dks + dvs, axis=1)

        @pl.when(pl.program_id(0) == 0)
        def _():
            dkv_ref[...] = dkv

        @pl.when(pl.program_id(0) > 0)
        def _():
            dkv_ref[...] += dkv

    row = pl.BlockSpec((tm, xw), lambda i: (i, 0))
    whole = pl.BlockSpec((mt, 2 * xw), lambda i: (0, 0))
    return pl.pallas_call(
        body, name="xattn_bwd", grid=(t // tm,),
        in_specs=[row, whole, row], out_specs=[row, whole],
        out_shape=[_sds((t, xw), BF16), _sds((mt, 2 * xw), F32)],
        compiler_params=_cparams("arbitrary"),
    )(xq, kv, do)


def _swiglu_fwd(hid):
    t, f2 = hid.shape
    f = f2 // 2
    tm = _pick(t, (256, 128))

    def body(h_ref, o_ref):
        a = h_ref[:, :f].astype(F32)
        b = h_ref[:, f:].astype(F32)
        o_ref[...] = (a * _sigmoid(a) * b).astype(o_ref.dtype)

    return pl.pallas_call(
        body, name="swiglu_fwd", grid=(t // tm,),
        in_specs=[pl.BlockSpec((tm, f2), lambda i: (i, 0))],
        out_specs=pl.BlockSpec((tm, f), lambda i: (i, 0)), out_shape=_sds((t, f), BF16),
        compiler_params=_cparams("parallel"),
    )(hid)


def _swiglu_bwd(hid, dact):
    t, f2 = hid.shape
    f = f2 // 2
    tm = _pick(t, (128,))

    def body(h_ref, d_ref, o_ref):
        a = h_ref[:, :f].astype(F32)
        b = h_ref[:, f:].astype(F32)
        d = d_ref[...].astype(F32)
        sg = _sigmoid(a)
        o_ref[:, :f] = (d * b * sg * (1.0 + a * (1.0 - sg))).astype(o_ref.dtype)
        o_ref[:, f:] = (d * a * sg).astype(o_ref.dtype)

    return pl.pallas_call(
        body, name="swiglu_bwd", grid=(t // tm,),
        in_specs=[pl.BlockSpec((tm, f2), lambda i: (i, 0)), pl.BlockSpec((tm, f), lambda i: (i, 0))],
        out_specs=pl.BlockSpec((tm, f2), lambda i: (i, 0)), out_shape=_sds((t, f2), BF16),
        compiler_params=_cparams("parallel"),
    )(hid, dact)


def _adamw(w, g, m, v, name):
    r, c = w.shape
    tr = _pick(r, (256, 128, 64, 32, 16, 8)) if r * c > 65536 else r

    def body(w_ref, g_ref, m_ref, v_ref, d_ref, nm_ref, nv_ref):
        gv = g_ref[...]
        m2 = ADAM_B1 * m_ref[...] + (1.0 - ADAM_B1) * gv
        v2 = ADAM_B2 * v_ref[...] + (1.0 - ADAM_B2) * (gv * gv)
        m_hat = m2 / (1.0 - ADAM_B1 ** ADAM_STEP)
        v_hat = v2 / (1.0 - ADAM_B2 ** ADAM_STEP)
        d_ref[...] = -ADAM_LR * (m_hat / (jnp.sqrt(v_hat) + ADAM_EPS) + ADAM_WD * w_ref[...])
        nm_ref[...] = m2
        nv_ref[...] = v2

    blk = pl.BlockSpec((tr, c), lambda i: (i, 0))
    return pl.pallas_call(
        body, name=name, grid=(r // tr,),
        in_specs=[blk] * 4, out_specs=[blk] * 3, out_shape=[_sds((r, c), F32)] * 3,
        compiler_params=_cparams("parallel"),
    )(w, g, m, v)


class _Weights:
    def __init__(self, full):
        self.full = full
        self.grads = {}

    def get(self, name):
        return self.full[name]

    def mark(self, tag, value):
        return value

    def grad(self, name, g):
        self.grads[name] = g


def _local_step(x, mem, tgt, g_mix, sinks, g_xattn, g_mem, g_ffn, g_final, dims, wts):
    t, d = x.shape
    aw, cw, kvw = dims
    cb = 2 * kvw
    zoff = aw + 2 * kvw
    goff = zoff + 3 * cw
    cos, sin = _rope_tables(t)
    mm = _matmul
    mark, get = wts.mark, wts.get

    u1 = mark("u1", _rms_fwd(x, g_mix, "rms_mix"))
    proj = mark("proj", mm(u1, get("w_in"), mode="nn", out_dtype=F32, name="mm_proj"))
    attn_o = mark("attn_o", _attn_fwd(proj, cos, sin, sinks, aw, kvw))
    conv_o = mark("conv_o", _conv_fwd(proj, get("conv_w"), zoff, cw, cb))
    ya = mark("ya", mm(attn_o, get("w_attn_proj"), mode="nn", out_dtype=BF16, name="mm_yattn"))
    yc = mark("yc", mm(conv_o, get("w_conv_proj"), mode="nn", out_dtype=BF16, name="mm_yconv"))
    merged = mark("merged", _merge_fwd(proj, ya, yc, goff, cb))
    h1 = mark("h1", mm(merged, get("w_mix_out"), mode="nn", out_dtype=F32, name="mm_mix", res=x))
    u2 = mark("u2", _rms_fwd(h1, g_xattn, "rms_xattn"))
    mem_n = _rms_fwd(mem, g_mem, "rms_mem")
    xq = mark("xq", mm(u2, get("w_xq"), mode="nn", out_dtype=BF16, name="mm_xq"))
    kv = mm(mem_n, get("w_xkv"), mode="nn", out_dtype=BF16, name="mm_xkv")
    xo = mark("xo", _xattn_fwd(xq, kv))
    h2 = mark("h2", mm(xo, get("w_xo"), mode="nn", out_dtype=F32, name="mm_xo", res=h1))
    u3 = mark("u3", _rms_fwd(h2, g_ffn, "rms_ffn"))
    hid = mark("hid", mm(u3, get("w_ffn_in"), mode="nn", out_dtype=BF16, name="mm_ffn_in"))
    act = mark("act", _swiglu_fwd(hid))
    h3 = mm(act, get("w_ffn_out"), mode="nn", out_dtype=F32, name="mm_ffn_out", res=h2)
    dh3, dh3b, dg_final, loss = _loss_head(h3, g_final, tgt)
    mark("dh3", dh3b)

    wts.grad("w_ffn_out", mm(act, dh3b, mode="tn", out_dtype=BF16, name="mm_dw_ffn_out"))
    dact = mark("dact", mm(dh3b, get("w_ffn_out"), mode="nt", out_dtype=BF16, name="mm_dact"))
    dhid = mark("dhid", _swiglu_bwd(hid, dact))
    wts.grad("w_ffn_in", mm(u3, dhid, mode="tn", out_dtype=BF16, name="mm_dw_ffn_in"))
    du3 = mark("du3", mm(dhid, get("w_ffn_in"), mode="nt", out_dtype=F32, name="mm_du3"))
    dh2, dh2b, dg_ffn = _rms_bwd(h2, g_ffn, du3, dh3, "rms_bwd_ffn")
    mark("dh2", dh2b)
    wts.grad("w_xo", mm(xo, dh2b, mode="tn", out_dtype=BF16, name="mm_dw_xo"))
    dxo = mm(dh2b, get("w_xo"), mode="nt", out_dtype=BF16, name="mm_dxo")
    dxq, dkv = _xattn_bwd(xq, kv, dxo)
    dkvb = dkv.astype(BF16)
    wts.grad("w_xq", mm(u2, dxq, mode="tn", out_dtype=BF16, name="mm_dw_xq"))
    du2 = mm(dxq, get("w_xq"), mode="nt", out_dtype=F32, name="mm_du2")
    wts.grad("w_xkv", mm(mem_n, dkvb, mode="tn", out_dtype=BF16, name="mm_dw_xkv"))
    dmem_n = mm(dkvb, get("w_xkv"), mode="nt", out_dtype=F32, name="mm_dmem")
    _, _, dg_mem = _rms_bwd(mem, g_mem, dmem_n, None, "rms_bwd_mem")
    dh1, dh1b, dg_xattn = _rms_bwd(h1, g_xattn, du2, dh2, "rms_bwd_xattn")
    mark("dh1", dh1b)
    wts.grad("w_mix_out", mm(merged, dh1b, mode="tn", out_dtype=BF16, name="mm_dw_mix"))
    dmerged = mm(dh1b, get("w_mix_out"), mode="nt", out_dtype=BF16, name="mm_dmerged")
    dya, dyc, dga, dgg = _merge_bwd(proj, ya, yc, dmerged, goff, cb)
    mark("dya", dya)
    wts.grad("w_attn_proj", mm(attn_o, dya, mode="tn", out_dtype=BF16, name="mm_dw_attn_proj"))
    dattn_o = mm(dya, get("w_attn_proj"), mode="nt", out_dtype=BF16, name="mm_dattn")
    wts.grad("w_conv_proj", mm(conv_o, dyc, mode="tn", out_dtype=BF16, name="mm_dw_conv_proj"))
    dconv_o = mark("dconv_o", mm(dyc, get("w_conv_proj"), mode="nt", out_dtype=BF16, name="mm_dconv"))
    dz, dgb, dgc, dconv_w = _conv_bwd(proj, get("conv_w"), dconv_o, zoff, cw, cb)
    mark("dz", dz)
    dq, dkc, dkp, dvc, dvp, dsinks = _attn_bwd(proj, cos, sin, sinks, dattn_o, aw, kvw)
    mark("dq", dq)
    dproj = mark("dproj", _assemble_dproj(dq, dkc, dkp, dvc, dvp, dz, dgb, dgc, dga, dgg))
    wts.grad("w_in", mm(u1, dproj, mode="tn", out_dtype=BF16, name="mm_dw_in"))
    du1 = mark("du1", mm(dproj, get("w_in"), mode="nt", out_dtype=F32, name="mm_du1"))
    grad_x, _, dg_mix = _rms_bwd(x, g_mix, du1, dh1, "rms_bwd_mix")
    mark("grad_x", grad_x)

    small = dict(g_mix=dg_mix, g_xattn=dg_xattn, g_mem=dg_mem, g_ffn=dg_ffn, g_final=dg_final,
                 conv_w=dconv_w[:3], attn_sinks=dsinks[0:1, :sinks.shape[1]], loss=loss[0:1, 0:1])
    return grad_x, small


BIG = (("w_in", 1), ("w_attn_proj", 1), ("w_conv_proj", 1), ("w_mix_out", 0), ("w_xq", 0), ("w_xkv", 0),
       ("w_xo", 1), ("w_ffn_in", 1), ("w_ffn_out", 0))


def _place():
    x, y, c = lax.axis_index("x"), lax.axis_index("y"), lax.axis_index("c")
    chips = [(1 - x, y), (x, 1 - y), (1 - x, 1 - y)]
    return x, y, c, chips


def _window(ref, ax, shard_shape, s, h):
    sr, sc = shard_shape
    hr = sr // 2
    if ax == 1:
        return ref.at[pl.ds(pl.multiple_of(h * hr, 16), hr), pl.ds(pl.multiple_of(s * sc, LANES), sc)]
    return ref.at[pl.ds(pl.multiple_of(s * sr + h * hr, 16), hr), :]


def _half(ref, h):
    hr = ref.shape[0] // 2
    return ref.at[pl.ds(pl.multiple_of(h * hr, 16), hr), :]


def _remote(src, dst, send_sem, recv_sem, dev):
    return pltpu.make_async_remote_copy(src_ref=src, dst_ref=dst, send_sem=send_sem, recv_sem=recv_sem,
                                        device_id=dev, device_id_type=MESH)


def _cast_to_full(shard, ax, me, name, dtype=BF16):
    sr, sc = shard.shape
    tr = _pick(sr, (256, 352, 128, 64, 32, 16))
    nr = sr // tr
    full = (sr * N_CHIPS, sc) if ax == 0 else (sr, sc * N_CHIPS)

    def body(me_ref, s_ref, o_ref):
        o_ref[...] = s_ref[...].astype(o_ref.dtype)

    if ax == 1:
        out_spec = pl.BlockSpec((tr, sc), lambda r, me_ref: (r, me_ref[0]))
    else:
        out_spec = pl.BlockSpec((tr, sc), lambda r, me_ref: (me_ref[0] * nr + r, 0))
    return pl.pallas_call(
        body, name=name,
        grid_spec=pltpu.PrefetchScalarGridSpec(
            num_scalar_prefetch=1, grid=(nr,), in_specs=[pl.BlockSpec((tr, sc), lambda r, me_ref: (r, 0))],
            out_specs=out_spec),
        out_shape=_sds(full, dtype),
        compiler_params=_cparams("parallel"),
    )(me, shard)


HBM = pl.BlockSpec(memory_space=pltpu.HBM)
SEM = pl.BlockSpec(memory_space=pltpu.SEMAPHORE)
EFFECT = pltpu.SideEffectType.DATAFLOW_SIDE_EFFECTING


def _in_hbm(a):
    return pltpu.with_memory_space_constraint(a, pltpu.HBM)


def _gather_window(ref, ax, shard_shape, s, h):
    if h is not None:
        return _window(ref, ax, shard_shape, s, h)
    sr, sc = shard_shape
    if ax == 1:
        return ref.at[:, pl.ds(pl.multiple_of(s * sc, LANES), sc)]
    return ref.at[pl.ds(pl.multiple_of(s * sr, 8), sr), :]


def _ag_start(fulls, axes, shard_shapes, whole):
    n = len(fulls)

    def body(*refs):
        src = refs[:n]
        send_sems, recv_sems = refs[n], refs[n + 1]
        token = refs[2 * n + 2]
        x, y, c, chips = _place()
        me = 2 * x + y
        for i in range(n):
            h = None if whole[i] else c
            for j, chip in enumerate(chips):
                blk = _gather_window(src[i], axes[i], shard_shapes[i], me, h)
                _remote(blk, blk, send_sems.at[3 * i + j], recv_sems.at[3 * i + j], (*chip, c)).start()
        token[...] = jnp.zeros_like(token)

    res = pl.pallas_call(
        body, name="ag_start_" + str(n),
        out_shape=(pltpu.SemaphoreType.DMA((3 * n,)), pltpu.SemaphoreType.DMA((3 * n,)),
                   *[pltpu.HBM(f.shape, f.dtype) for f in fulls], _sds((8, LANES), F32)),
        in_specs=[HBM] * n, out_specs=(SEM, SEM, *[HBM] * n, pl.BlockSpec(memory_space=pltpu.VMEM)),
        input_output_aliases={i: 2 + i for i in range(n)},
        compiler_params=pltpu.CompilerParams(has_side_effects=EFFECT),
    )(*[_in_hbm(f) for f in fulls])
    return res[0], res[1], list(res[2:2 + n]), res[2 + n]


def _ag_mid(bufs, slots, axes, shard_shapes, whole, send_sems, recv_sems, after, name):
    ng = len(bufs)

    def body(*refs):
        src = refs[:ng]
        s_in, r_in = refs[ng], refs[ng + 1]
        fsend, frecv = refs[ng + 3], refs[ng + 4]
        x, y, c, chips = _place()
        me = 2 * x + y
        sib = (x, y, 1 - c)
        for k, i in enumerate(slots):
            h = None if whole[k] else c
            for j, chip in enumerate(chips):
                cj = 2 * chip[0] + chip[1]
                mine = _gather_window(src[k], axes[k], shard_shapes[k], me, h)
                theirs = _gather_window(src[k], axes[k], shard_shapes[k], cj, h)
                _remote(theirs, theirs, s_in.at[3 * i + j], r_in.at[3 * i + j], (*chip, c)).wait_recv()
                _remote(mine, mine, s_in.at[3 * i + j], r_in.at[3 * i + j], (*chip, c)).wait_send()
                if not whole[k]:
                    _remote(theirs, theirs, fsend.at[3 * k + j], frecv.at[3 * k + j], sib).start()

    res = pl.pallas_call(
        body, name=name,
        out_shape=(pltpu.SemaphoreType.DMA((3 * ng,)), pltpu.SemaphoreType.DMA((3 * ng,)),
                   *[pltpu.HBM(b.shape, b.dtype) for b in bufs]),
        in_specs=[HBM] * ng + [SEM, SEM, ANY], out_specs=(SEM, SEM, *[HBM] * ng),
        input_output_aliases={k: 2 + k for k in range(ng)},
        compiler_params=pltpu.CompilerParams(has_side_effects=EFFECT),
    )(*bufs, send_sems, recv_sems, after)
    return res[0], res[1], list(res[2:])


def _ag_wait(bufs, axes, shard_shapes, whole, fsend, frecv, after, name):
    ng = len(bufs)

    def body(*refs):
        src = refs[:ng]
        s_in, r_in = refs[ng], refs[ng + 1]
        x, y, c, chips = _place()
        sib = (x, y, 1 - c)
        for k in range(ng):
            if whole[k]:
                continue
            for j, chip in enumerate(chips):
                cj = 2 * chip[0] + chip[1]
                sent = _gather_window(src[k], axes[k], shard_shapes[k], cj, c)
                landed = _gather_window(src[k], axes[k], shard_shapes[k], cj, 1 - c)
                _remote(landed, landed, s_in.at[3 * k + j], r_in.at[3 * k + j], sib).wait_recv()
                _remote(sent, sent, s_in.at[3 * k + j], r_in.at[3 * k + j], sib).wait_send()

    res = pl.pallas_call(
        body, name=name,
        out_shape=tuple(pltpu.HBM(b.shape, b.dtype) for b in bufs),
        in_specs=[HBM] * ng + [SEM, SEM, ANY], out_specs=tuple([HBM] * ng),
        input_output_aliases={k: k for k in range(ng)},
        compiler_params=pltpu.CompilerParams(has_side_effects=EFFECT),
    )(*bufs, fsend, frecv, after)
    return list(res)


class _Schedule:
    GROUPS = ((("w_in", "conv_w"), "start", "u1"),
              (("w_attn_proj", "w_conv_proj", "w_mix_out"), "proj", "conv_o"),
              (("w_xq", "w_xkv", "w_xo"), "ya", "h1"),
              (("w_ffn_in",), "u2", "h2"),
              (("w_ffn_out",), "u3", "hid"))
    STARTS = ((0,), (1, 2, 3, 4))
    REDUCE = ((("w_ffn_out",), "dact", "grad:w_ffn_in", "du3"),
              (("w_ffn_in",), "grad:w_ffn_in", "grad:w_mix_out", "grad_x"),
              (("w_xo", "w_xq", "w_xkv"), "dh1", "dconv_o", "grad_x"),
              (("w_mix_out", "w_attn_proj", "w_conv_proj"), "dconv_o", "dq", "grad_x"),
              (("w_in",), "grad:w_in", "end", "end2"))

    def __init__(self, seed, axes, shard_shapes, place, on_ready):
        self.ax, self.shape, self.place, self.on_ready = axes, shard_shapes, place, on_ready
        self.stage, self.buf, self.slot, self.passes = {}, {}, {}, {}
        self.ready = set()
        self.grads = {}
        token = None
        for groups in self.STARTS:
            order = [nm for g in groups for nm in self.GROUPS[g][0]]
            send, recv, bufs, token = _ag_start([seed(nm, token) for nm in order], *self._meta(order))
            self.buf.update(zip(order, bufs))
            self.slot.update({nm: (send, recv, k) for k, nm in enumerate(order)})
        self.token = token
        self.mark("start", token)

    def _meta(self, names):
        return ([self.ax[nm] for nm in names], [self.shape[nm] for nm in names], [nm == "conv_w" for nm in names])

    def mark(self, tag, value):
        for g, (names, mid, wait) in enumerate(self.GROUPS):
            if tag == mid:
                send, recv, _ = self.slot[names[0]]
                fs, fr, bufs = _ag_mid([self.buf[nm] for nm in names], [self.slot[nm][2] for nm in names],
                                       *self._meta(names), send, recv, value, "ag_mid_%d" % g)
                self.buf.update(zip(names, bufs))
                self.passes[g] = (fs, fr)
            if tag == wait:
                fs, fr = self.passes[g]
                bufs = _ag_wait([self.buf[nm] for nm in names], *self._meta(names), fs, fr, value, "ag_wait_%d" % g)
                self.buf.update(zip(names, bufs))
                self.ready.update(names)
        for g, (names, send, total, finish) in enumerate(self.REDUCE):
            st = self.stage.get(g)
            if st is None:
                continue
            ng = len(names)
            if tag == send and st["at"] == "pair":
                arrs = _exchange_wait("rs_pair_wait_%d" % g, st["arrs"], *st["sems"], st["plan"], value)
                parts = [_pair_add(arrs[k], arrs[ng + k], self.ax[nm], self.shape[nm], self.place, "pair_add_" + nm)
                         for k, nm in enumerate(names)]
                plan, nsem = _plan_chip(ng)
                ss, rs, arrs = _exchange_start("rs_chip_start_%d" % g, parts + [lax.empty(p.shape, p.dtype) for p in parts],
                                               nsem, plan)
                self.stage[g] = dict(at="chip", arrs=arrs, sems=(ss, rs), plan=plan)
            elif tag == total and st["at"] == "chip":
                arrs = _exchange_wait("rs_chip_wait_%d" % g, st["arrs"], *st["sems"], st["plan"], value)
                halves = [_chip_add(arrs[k], arrs[ng + k], self.place, "chip_add_" + nm) for k, nm in enumerate(names)]
                plan, nsem = _plan_gather(ng)
                ss, rs, arrs = _exchange_start("rs_gather_start_%d" % g, halves, nsem, plan)
                self.stage[g] = dict(at="gather", arrs=arrs, sems=(ss, rs), plan=plan)
            elif tag == finish and st["at"] == "gather":
                arrs = _exchange_wait("rs_gather_wait_%d" % g, st["arrs"], *st["sems"], st["plan"], value)
                self.stage[g] = dict(at="done")
                for nm, shard in zip(names, arrs):
                    self.on_ready(nm, shard)
        return value

    def get(self, name):
        assert name in self.ready, name
        return self.buf[name]

    def grad(self, name, g):
        self.grads[name] = g
        for gi, (names, _, _, _) in enumerate(self.REDUCE):
            if name == names[-1]:
                gs = [self.grads[nm] for nm in names]
                plan, nsem = _plan_pair(len(names), [self.ax[nm] for nm in names], [self.shape[nm] for nm in names])
                ss, rs, arrs = _exchange_start("rs_pair_start_%d" % gi, gs + [lax.empty(a.shape, a.dtype) for a in gs],
                                               nsem, plan)
                self.stage[gi] = dict(at="pair", arrs=arrs, sems=(ss, rs), plan=plan)
        self.mark("grad:" + name, g)


def _exchange_start(name, arrays, nsem, plan):
    n = len(arrays)

    def body(*refs):
        send_sems, recv_sems, token = refs[n], refs[n + 1], refs[2 * n + 2]
        sends, _ = plan(refs[:n])
        for k, (src, dst, dev) in enumerate(sends):
            _remote(src, dst, send_sems.at[k], recv_sems.at[k], dev).start()
        token[...] = jnp.zeros_like(token)

    res = pl.pallas_call(
        body, name=name,
        out_shape=(pltpu.SemaphoreType.DMA((nsem,)), pltpu.SemaphoreType.DMA((nsem,)),
                   *[pltpu.HBM(a.shape, a.dtype) for a in arrays], _sds((8, LANES), F32)),
        in_specs=[HBM] * n, out_specs=(SEM, SEM, *[HBM] * n, pl.BlockSpec(memory_space=pltpu.VMEM)),
        input_output_aliases={i: 2 + i for i in range(n)},
        compiler_params=pltpu.CompilerParams(has_side_effects=EFFECT),
    )(*[_in_hbm(a) for a in arrays])
    return res[0], res[1], list(res[2:2 + n])


def _exchange_wait(name, arrays, send_sems, recv_sems, plan, after):
    n = len(arrays)

    def body(*refs):
        s_in, r_in = refs[n], refs[n + 1]
        sends, recvs = plan(refs[:n])
        for k, land in enumerate(recvs):
            _remote(land, land, s_in.at[k], r_in.at[k], sends[k][2]).wait_recv()
        for k, (src, _, dev) in enumerate(sends):
            _remote(src, src, s_in.at[k], r_in.at[k], dev).wait_send()

    res = pl.pallas_call(
        body, name=name,
        out_shape=tuple(pltpu.HBM(a.shape, a.dtype) for a in arrays),
        in_specs=[HBM] * n + [SEM, SEM, ANY], out_specs=tuple([HBM] * n),
        input_output_aliases={i: i for i in range(n)},
        compiler_params=pltpu.CompilerParams(has_side_effects=EFFECT),
    )(*arrays, send_sems, recv_sems, after)
    return list(res)


def _plan_pair(n, axes, shard_shapes):
    def plan(refs):
        g, ra = refs[:n], refs[n:]
        x, y, c, _ = _place()
        sib = (x, y, 1 - c)

        def pieces(ref, i, h):
            if axes[i] == 1:
                return [_half(ref, h)]
            return [_window(ref, 0, shard_shapes[i], s, h) for s in range(N_CHIPS)]

        sends, recvs = [], []
        for i in range(n):
            sends += [(src, dst, sib) for src, dst in zip(pieces(g[i], i, 1 - c), pieces(ra[i], i, 1 - c))]
            recvs += pieces(ra[i], i, c)
        return sends, recvs

    return plan, sum(1 if ax == 1 else N_CHIPS for ax in axes)


def _plan_chip(n):
    def plan(refs):
        p, rc = refs[:n], refs[n:]
        x, y, c, chips = _place()
        me = 2 * x + y
        sends, recvs = [], []
        for i in range(n):
            for chip in chips:
                cj = 2 * chip[0] + chip[1]
                sends.append((p[i].at[cj], rc[i].at[me], (*chip, c)))
                recvs.append(rc[i].at[cj])
        return sends, recvs

    return plan, 3 * n


def _plan_gather(n):
    def plan(refs):
        x, y, c, _ = _place()
        sib = (x, y, 1 - c)
        return ([(_half(r, c), _half(r, c), sib) for r in refs], [_half(r, 1 - c) for r in refs])

    return plan, n


def _pair_add(g, ra, ax, shard_shape, place, name):
    sr, sc = shard_shape
    hr = sr // 2
    wc = sc
    tr = _pick(hr, (256, 352, 128, 64, 32, 16))
    nr = hr // tr

    def body(p_ref, a_ref, b_ref, o_ref):
        o_ref[...] = (a_ref[...].astype(F32) + b_ref[...].astype(F32)).astype(o_ref.dtype)

    if ax == 1:
        src = pl.BlockSpec((tr, wc), lambda s, r, p_ref: (p_ref[0] * nr + r, s))
    else:
        src = pl.BlockSpec((tr, wc), lambda s, r, p_ref: (s * 2 * nr + p_ref[0] * nr + r, 0))
    return pl.pallas_call(
        body, name=name,
        grid_spec=pltpu.PrefetchScalarGridSpec(
            num_scalar_prefetch=1, grid=(N_CHIPS, nr), in_specs=[src, src],
            out_specs=pl.BlockSpec((None, tr, wc), lambda s, r, p_ref: (s, r, 0))),
        out_shape=_sds((N_CHIPS, hr, wc), BF16),
        compiler_params=_cparams("parallel", "parallel"),
    )(place, g, ra)


def _chip_add(part, rc, place, name):
    _, hr, wc = rc.shape
    tr = _pick(hr, (256, 352, 128, 64, 32, 16))
    nr = hr // tr

    def body(p_ref, own_ref, r1_ref, r2_ref, r3_ref, o_ref):
        acc = own_ref[...].astype(F32)
        for r_ref in (r1_ref, r2_ref, r3_ref):
            acc = acc + r_ref[...].astype(F32)
        o_ref[...] = acc

    def slot(k):
        return pl.BlockSpec((None, tr, wc), lambda r, p_ref: ((p_ref[1] + k) % N_CHIPS, r, 0))

    return pl.pallas_call(
        body, name=name,
        grid_spec=pltpu.PrefetchScalarGridSpec(
            num_scalar_prefetch=1, grid=(nr,), in_specs=[slot(0), slot(1), slot(2), slot(3)],
            out_specs=pl.BlockSpec((tr, wc), lambda r, p_ref: (p_ref[0] * nr + r, 0))),
        out_shape=_sds((2 * hr, wc), F32),
        compiler_params=_cparams("parallel"),
    )(place, part, rc, rc, rc)


N_DEV = 8


def _all_reduce_small(buf):
    r, cdim = buf.shape

    def body(x_ref, o_ref, land, send_sems, recv_sems):
        x, y, c, _ = _place()
        me = 4 * x + 2 * y + c
        land[me] = x_ref[...]
        sends = []
        for k in range(1, N_DEV):
            kx, ky, kc = (k >> 2) & 1, (k >> 1) & 1, k & 1
            peer = (1 - x if kx else x, 1 - y if ky else y, 1 - c if kc else c)
            cp = _remote(x_ref, land.at[me], send_sems.at[k - 1], recv_sems.at[k - 1], peer)
            cp.start()
            sends.append(cp)
        for k in range(1, N_DEV):
            kx, ky, kc = (k >> 2) & 1, (k >> 1) & 1, k & 1
            peer = (1 - x if kx else x, 1 - y if ky else y, 1 - c if kc else c)
            pidx = 4 * peer[0] + 2 * peer[1] + peer[2]
            _remote(land.at[pidx], land.at[pidx], send_sems.at[k - 1], recv_sems.at[k - 1], peer).wait_recv()
        for cp in sends:
            cp.wait_send()
        acc = land[0]
        for dev in range(1, N_DEV):
            acc = acc + land[dev]
        o_ref[...] = acc

    vm = pl.BlockSpec(memory_space=pltpu.VMEM)
    return pl.pallas_call(
        body, name="all_reduce_small", in_specs=[vm], out_specs=vm, out_shape=_sds((r, cdim), F32),
        scratch_shapes=[pltpu.VMEM((N_DEV, r, cdim), F32), pltpu.SemaphoreType.DMA((N_DEV - 1,)),
                        pltpu.SemaphoreType.DMA((N_DEV - 1,))],
    )(buf)


SMALL_ROWS = 16


def kernel(x, mem, g_mix, w_in, conv_w, attn_sinks, w_attn_proj, w_conv_proj, w_mix_out, g_xattn, g_mem, w_xq, w_xkv, w_xo, g_ffn, w_ffn_in, w_ffn_out, g_final, loss_target, m_g_mix, m_w_in, m_conv_w, m_attn_sinks, m_w_attn_proj, m_w_conv_proj, m_w_mix_out, m_g_xattn, m_g_mem, m_w_xq, m_w_xkv, m_w_xo, m_g_ffn, m_w_ffn_in, m_w_ffn_out, m_g_final, v_g_mix, v_w_in, v_conv_w, v_attn_sinks, v_w_attn_proj, v_w_conv_proj, v_w_mix_out, v_g_xattn, v_g_mem, v_w_xq, v_w_xkv, v_w_xo, v_g_ffn, v_w_ffn_in, v_w_ffn_out, v_g_final):
    w = dict(g_mix=g_mix, w_in=w_in[0], conv_w=conv_w[0], attn_sinks=attn_sinks, w_attn_proj=w_attn_proj[0],
             w_conv_proj=w_conv_proj[0], w_mix_out=w_mix_out[0], g_xattn=g_xattn, g_mem=g_mem, w_xq=w_xq[0],
             w_xkv=w_xkv[0], w_xo=w_xo[0], g_ffn=g_ffn, w_ffn_in=w_ffn_in[0], w_ffn_out=w_ffn_out[0],
             g_final=g_final[None])
    m = dict(g_mix=m_g_mix, w_in=m_w_in[0], conv_w=m_conv_w[0], attn_sinks=m_attn_sinks,
             w_attn_proj=m_w_attn_proj[0], w_conv_proj=m_w_conv_proj[0], w_mix_out=m_w_mix_out[0],
             g_xattn=m_g_xattn, g_mem=m_g_mem, w_xq=m_w_xq[0], w_xkv=m_w_xkv[0], w_xo=m_w_xo[0], g_ffn=m_g_ffn,
             w_ffn_in=m_w_ffn_in[0], w_ffn_out=m_w_ffn_out[0], g_final=m_g_final[None])
    v = dict(g_mix=v_g_mix, w_in=v_w_in[0], conv_w=v_conv_w[0], attn_sinks=v_attn_sinks,
             w_attn_proj=v_w_attn_proj[0], w_conv_proj=v_w_conv_proj[0], w_mix_out=v_w_mix_out[0],
             g_xattn=v_g_xattn, g_mem=v_g_mem, w_xq=v_w_xq[0], w_xkv=v_w_xkv[0], w_xo=v_w_xo[0], g_ffn=v_g_ffn,
             w_ffn_in=v_w_ffn_in[0], w_ffn_out=v_w_ffn_out[0], g_final=v_g_final[None])
    names = [nm for nm, _ in BIG]
    axes = [ax for _, ax in BIG]
    d = x.shape[2]
    cw = w["conv_w"].shape[1] * N_CHIPS
    chip = (2 * lax.axis_index("x") + lax.axis_index("y")).astype(jnp.int32)
    place = jnp.stack([lax.axis_index("c").astype(jnp.int32), chip])
    shard_shapes = [w[nm].shape for nm in names]

    def seed(nm, token):
        me1 = chip.reshape(1)
        if token is not None:
            me1 = me1 + token[0, 0].astype(jnp.int32)
        if nm == "conv_w":
            return _cast_to_full(w[nm], 1, me1, "place_conv_w", F32)
        return _cast_to_full(w[nm], dict(BIG)[nm], me1, "cast_" + nm)

    upd = {}

    def on_ready(nm, shard):
        grads[nm] = shard
        upd[nm] = _adamw(w[nm], shard, m[nm], v[nm], "adamw_" + nm)

    grads = {}
    wts = _Schedule(seed, dict(zip(names + ["conv_w"], axes + [1])),
                    dict(zip(names + ["conv_w"], shard_shapes + [w["conv_w"].shape])), place, on_ready)
    aw, cw = w["w_attn_proj"].shape[0], w["w_conv_proj"].shape[0]
    kvw = (w["w_in"].shape[1] * N_CHIPS - aw - 3 * cw - 2 * d) // 2
    grad_x, small = _local_step(
        x[0], mem[0], loss_target[0], w["g_mix"] + wts.token[0:1, 0:1], w["attn_sinks"], w["g_xattn"], w["g_mem"],
        w["g_ffn"], w["g_final"], (aw, cw, kvw), wts)

    pw = max(d, cw)

    def row(a):
        return jnp.pad(a, ((0, 0), (0, pw - a.shape[1])))

    gains = ("g_mix", "g_xattn", "g_mem", "g_ffn", "g_final")
    packed = jnp.concatenate(
        [row(small[nm]) for nm in gains] + [row(small["conv_w"]),
         row(jnp.concatenate([small["attn_sinks"], small["loss"]], axis=1)),
         jnp.zeros((SMALL_ROWS - 9, pw), F32)], axis=0)
    total = _all_reduce_small(packed)
    wts.mark("end", total)
    nsink = attn_sinks.shape[1]
    grads.update({nm: total[k:k + 1, :d] for k, nm in enumerate(gains)})
    grads.update(conv_w=lax.dynamic_slice(total, (5, chip * (cw // N_CHIPS)), (3, cw // N_CHIPS)),
                 attn_sinks=total[8:9, :nsink])
    loss = total[8, nsink]
    for nm in gains + ("conv_w", "attn_sinks"):
        upd[nm] = _adamw(w[nm], grads[nm], m[nm], v[nm], "adamw_" + nm)
    wts.mark("end2", upd["g_final"][0])

    order = ["g_mix", "w_in", "conv_w", "attn_sinks", "w_attn_proj", "w_conv_proj", "w_mix_out", "g_xattn", "g_mem",
             "w_xq", "w_xkv", "w_xo", "g_ffn", "w_ffn_in", "w_ffn_out", "g_final"]

    stacked = set(names) | {"conv_w"}

    def shaped(nm, a):
        if nm == "g_final":
            return a[0]
        return a[None] if nm in stacked else a

    outs = [loss, grad_x[None]]
    outs += [shaped(nm, grads[nm]) for nm in order]
    for k in range(3):
        outs += [shaped(nm, upd[nm][k]) for nm in order]
    return tuple(outs)
```

```python
import functools

import jax
import jax.numpy as jnp
from jax import lax
from jax.experimental import pallas as pl
from jax.experimental.pallas import tpu as pltpu

F32 = jnp.float32
BF16 = jnp.bfloat16

VMEM_LIMIT_BYTES = 56 * 1024 * 1024
LANES = 128
HEAD_DIM = 64
BLOCK = 128
X_HEAD_DIM = 128
ROPE_THETA = 10000.0
EPS = 1e-6
NEG = -1e30
ADAM_LR, ADAM_B1, ADAM_B2, ADAM_EPS, ADAM_WD, ADAM_STEP = 0.001, 0.9, 0.999, 1e-08, 0.01, 10
N_CHIPS = 4
MESH = pl.DeviceIdType.MESH
ANY = pl.BlockSpec(memory_space=pl.ANY)


def _pick(dim, prefs):
    for p in prefs:
        if dim % p == 0:
            return p
    return dim


def _cparams(*sem):
    return pltpu.CompilerParams(dimension_semantics=sem, vmem_limit_bytes=VMEM_LIMIT_BYTES)


def _sds(shape, dtype):
    return jax.ShapeDtypeStruct(shape, dtype)


def _sigmoid(v):
    return 1.0 / (1.0 + jnp.exp(-v))


MATMUL_VMEM_BUDGET = 40 * 1024 * 1024


def _tiles(mode, m, n, k, out_bytes, has_res):
    if mode == "tn":
        if m % 1024 == 0:
            return 1024, _pick(n, (512, 256, 128)), k, False
        return _pick(m, (512, 256, 128)), _pick(n, (1024, 512, 256, 128)), k, True
    tm = _pick(m, (1024, 512, 256, 128))
    if k <= 2048:
        return tm, _pick(n, (512, 256, 128)), k, False
    tn = n if (n <= 2048 and not has_res) else _pick(n, (1024, 512, 256, 128))
    fixed = tm * tn * (4 + 2 * out_bytes + (8 if has_res else 0))
    tk = LANES
    for cand in range(LANES, k + 1, LANES):
        if k % cand == 0 and fixed + (tm + tn) * 4 * cand <= MATMUL_VMEM_BUDGET:
            tk = cand
    return tm, tn, tk, False


def _matmul(a, b, *, mode, out_dtype, name, res=None, dep=None):
    if mode == "nn":
        (m, k), (k2, n) = a.shape, b.shape
    elif mode == "nt":
        (m, k), (n, k2) = a.shape, b.shape
    else:
        (k, m), (k2, n) = a.shape, b.shape
    assert k == k2, (a.shape, b.shape, mode)
    tm, tn, tk, swap = _tiles(mode, m, n, k, jnp.dtype(out_dtype).itemsize, res is not None)
    nk = k // tk
    dims = {"nn": (((1,), (0,)), ((), ())), "nt": (((1,), (1,)), ((), ())), "tn": (((0,), (0,)), ((), ()))}[mode]
    has_res = res is not None
    has_dep = dep is not None

    def body(*refs):
        a_ref, b_ref = refs[0], refs[1]
        r_ref = refs[2] if has_res else None
        o_ref = refs[2 + has_res + has_dep]
        part = lax.dot_general(a_ref[...], b_ref[...], dims, preferred_element_type=F32)

        def finish(val):
            if has_res:
                val = val + r_ref[...]
            o_ref[...] = val.astype(o_ref.dtype)

        if nk == 1:
            finish(part)
        else:
            acc = refs[3 + has_res + has_dep]
            kk = pl.program_id(2)

            @pl.when(kk == 0)
            def _():
                acc[...] = part

            @pl.when(kk > 0)
            def _():
                acc[...] += part

            @pl.when(kk == nk - 1)
            def _():
                finish(acc[...])

    def spec(shape, f):
        if swap:
            return pl.BlockSpec(shape, lambda j, i, kk: f(i, j, kk))
        return pl.BlockSpec(shape, f)

    if mode == "tn":
        a_spec = spec((tk, tm), lambda i, j, kk: (kk, i))
    else:
        a_spec = spec((tm, tk), lambda i, j, kk: (i, kk))
    if mode == "nt":
        b_spec = spec((tn, tk), lambda i, j, kk: (j, kk))
    else:
        b_spec = spec((tk, tn), lambda i, j, kk: (kk, j))
    o_spec = spec((tm, tn), lambda i, j, kk: (i, j))
    return pl.pallas_call(
        body,
        name=name,
        grid=(n // tn, m // tm, nk) if swap else (m // tm, n // tn, nk),
        in_specs=[a_spec, b_spec] + ([o_spec] if has_res else []) + ([ANY] if dep is not None else []),
        out_specs=o_spec,
        out_shape=_sds((m, n), out_dtype),
        scratch_shapes=[pltpu.VMEM((tm, tn), F32)] if nk > 1 else [],
        compiler_params=_cparams("parallel", "parallel", "arbitrary"),
    )(*([a, b] + ([res] if has_res else []) + ([dep] if dep is not None else [])))


def _rms_fwd(x, g, name):
    t, d = x.shape
    tm = _pick(t, (512, 256, 128))

    def body(x_ref, g_ref, o_ref):
        xf = x_ref[...]
        r = lax.rsqrt(jnp.mean(xf * xf, axis=-1, keepdims=True) + EPS)
        o_ref[...] = (xf * r * g_ref[...]).astype(o_ref.dtype)

    row = pl.BlockSpec((tm, d), lambda i: (i, 0))
    return pl.pallas_call(
        body, name=name, grid=(t // tm,),
        in_specs=[row, pl.BlockSpec((1, d), lambda i: (0, 0))],
        out_specs=row, out_shape=_sds((t, d), BF16),
        compiler_params=_cparams("parallel"),
    )(x, g)


def _rms_bwd_math(xf, g, du):
    r = lax.rsqrt(jnp.mean(xf * xf, axis=-1, keepdims=True) + EPS)
    xh = xf * r
    gdy = g * du
    dx = r * (gdy - xh * jnp.mean(gdy * xh, axis=-1, keepdims=True))
    dg = jnp.sum(du * xh, axis=0, keepdims=True)
    return dx, dg


def _rms_bwd(x, g, du, dh, name):
    t, d = x.shape
    tm = _pick(t, (256, 128))
    has_dh = dh is not None

    def body(*refs):
        x_ref, g_ref, du_ref = refs[0], refs[1], refs[2]
        o_ref, ob_ref, dg_ref = refs[3 + has_dh:]
        dx, dg = _rms_bwd_math(x_ref[...], g_ref[...], du_ref[...].astype(F32))
        if has_dh:
            dx = dx + refs[3][...]
        o_ref[...] = dx
        ob_ref[...] = dx.astype(BF16)

        @pl.when(pl.program_id(0) == 0)
        def _():
            dg_ref[...] = dg

        @pl.when(pl.program_id(0) > 0)
        def _():
            dg_ref[...] += dg

    row = pl.BlockSpec((tm, d), lambda i: (i, 0))
    vec = pl.BlockSpec((1, d), lambda i: (0, 0))
    return pl.pallas_call(
        body, name=name, grid=(t // tm,),
        in_specs=[row, vec, row] + ([row] if has_dh else []),
        out_specs=[row, row, vec],
        out_shape=[_sds((t, d), F32), _sds((t, d), BF16), _sds((1, d), F32)],
        compiler_params=_cparams("arbitrary"),
    )(*([x, g, du] + ([dh] if has_dh else [])))


def _loss_head(h, g, tgt):
    t, d = h.shape
    tm = _pick(t, (256, 128))

    def body(h_ref, g_ref, t_ref, o_ref, ob_ref, dg_ref, l_ref):
        xf = h_ref[...]
        gv = g_ref[...]
        r = lax.rsqrt(jnp.mean(xf * xf, axis=-1, keepdims=True) + EPS)
        err = xf * r * gv - t_ref[...]
        part = 0.5 * jnp.sum(jnp.mean(err * err, axis=-1, keepdims=True), axis=0, keepdims=True)
        dx, dg = _rms_bwd_math(xf, gv, err * (1.0 / d))
        o_ref[...] = dx
        ob_ref[...] = dx.astype(BF16)
        lrow = jnp.broadcast_to(part, (1, LANES))

        @pl.when(pl.program_id(0) == 0)
        def _():
            dg_ref[...] = dg
            l_ref[...] = lrow

        @pl.when(pl.program_id(0) > 0)
        def _():
            dg_ref[...] += dg
            l_ref[...] += lrow

    row = pl.BlockSpec((tm, d), lambda i: (i, 0))
    vec = pl.BlockSpec((1, d), lambda i: (0, 0))
    return pl.pallas_call(
        body, name="loss_head", grid=(t // tm,),
        in_specs=[row, vec, row],
        out_specs=[row, row, vec, pl.BlockSpec((1, LANES), lambda i: (0, 0))],
        out_shape=[_sds((t, d), F32), _sds((t, d), BF16), _sds((1, d), F32), _sds((1, LANES), F32)],
        compiler_params=_cparams("arbitrary"),
    )(h, g, tgt)


def _rope_tables(t):
    half = HEAD_DIM // 2
    inv_freq = ROPE_THETA ** (-jnp.arange(half, dtype=F32) / half)
    ang = jnp.arange(t, dtype=F32)[:, None] * inv_freq[None, :]
    cos = jnp.cos(ang)
    sin = jnp.sin(ang)
    reps = LANES // HEAD_DIM
    cos_t = jnp.tile(jnp.concatenate([cos, cos], axis=1), (1, reps))
    sin_t = jnp.tile(jnp.concatenate([-sin, sin], axis=1), (1, reps))
    return cos_t, sin_t


def _rope(v, cos, sin):
    w = v.shape[1]
    c = jnp.tile(cos, (1, w // LANES))
    s = jnp.tile(sin, (1, w // LANES))
    lane = lax.broadcasted_iota(jnp.int32, v.shape, 1)
    first = (lane % HEAD_DIM) < (HEAD_DIM // 2)
    partner = jnp.where(first, pltpu.roll(v, w - HEAD_DIM // 2, 1), pltpu.roll(v, HEAD_DIM // 2, 1))
    return v * c + partner * s


def _heads(v, first, count):
    return jnp.concatenate([v[:, (first + i) * HEAD_DIM:(first + i + 1) * HEAD_DIM] for i in range(count)], axis=0)


def _attn_probs(qs, kb, n, h, qpk, sinks_ref):
    s = lax.dot_general(qs, kb, (((1,), (1,)), ((), ())), preferred_element_type=F32) * (HEAD_DIM ** -0.5)
    qi = lax.broadcasted_iota(jnp.int32, (BLOCK, 2 * BLOCK), 0)
    kc = lax.broadcasted_iota(jnp.int32, (BLOCK, 2 * BLOCK), 1)
    valid = (kc > qi) & (kc <= qi + BLOCK) & ((kc >= BLOCK) | (n > 0))
    bias = jnp.tile(jnp.where(valid, 0.0, NEG).astype(F32), (qpk, 1))
    s = s + bias
    rowg = lax.broadcasted_iota(jnp.int32, (qpk * BLOCK, 1), 0) // BLOCK
    sink = jnp.zeros((qpk * BLOCK, 1), F32)
    for g in range(qpk):
        sink = jnp.where(rowg == g, sinks_ref[0, h * qpk + g], sink)
    m = jnp.maximum(jnp.max(s, axis=-1, keepdims=True), sink)
    e = jnp.exp(s - m)
    es = jnp.exp(sink - m)
    inv = 1.0 / (jnp.sum(e, axis=-1, keepdims=True) + es)
    return e * inv, es * inv, rowg


def _attn_specs(aw, kvw):
    koff = aw // kvw
    prev = lambda n: jnp.maximum(n - 1, 0)
    return [
        pl.BlockSpec((BLOCK, aw), lambda n: (n, 0)),
        pl.BlockSpec((BLOCK, kvw), lambda n: (n, koff)),
        pl.BlockSpec((BLOCK, kvw), lambda n: (prev(n), koff)),
        pl.BlockSpec((BLOCK, kvw), lambda n: (n, koff + 1)),
        pl.BlockSpec((BLOCK, kvw), lambda n: (prev(n), koff + 1)),
        pl.BlockSpec((BLOCK, LANES), lambda n: (n, 0)),
        pl.BlockSpec((BLOCK, LANES), lambda n: (n, 0)),
        pl.BlockSpec((BLOCK, LANES), lambda n: (prev(n), 0)),
        pl.BlockSpec((BLOCK, LANES), lambda n: (prev(n), 0)),
        pl.BlockSpec(memory_space=pltpu.SMEM),
    ]


def _attn_fwd(proj, cos, sin, sinks, aw, kvw):
    t = proj.shape[0]
    nkv = kvw // HEAD_DIM
    qpk = aw // kvw

    def body(q_ref, kc_ref, kp_ref, vc_ref, vp_ref, cc_ref, sc_ref, cp_ref, sp_ref, sinks_ref, o_ref):
        n = pl.program_id(0)
        q = _rope(q_ref[...].astype(F32), cc_ref[...], sc_ref[...]).astype(BF16)
        kc = _rope(kc_ref[...].astype(F32), cc_ref[...], sc_ref[...]).astype(BF16)
        kp = _rope(kp_ref[...].astype(F32), cp_ref[...], sp_ref[...]).astype(BF16)
        vc = vc_ref[...].astype(BF16)
        vp = vp_ref[...].astype(BF16)
        outs = []
        for h in range(nkv):
            hs = slice(h * HEAD_DIM, (h + 1) * HEAD_DIM)
            kb = jnp.concatenate([kp[:, hs], kc[:, hs]], axis=0)
            vb = jnp.concatenate([vp[:, hs], vc[:, hs]], axis=0)
            p, _, _ = _attn_probs(_heads(q, h * qpk, qpk), kb, n, h, qpk, sinks_ref)
            o = jnp.dot(p.astype(BF16), vb, preferred_element_type=F32)
            outs += [o[g * BLOCK:(g + 1) * BLOCK] for g in range(qpk)]
        o_ref[...] = jnp.concatenate(outs, axis=1).astype(o_ref.dtype)

    return pl.pallas_call(
        body, name="attn_fwd", grid=(t // BLOCK,),
        in_specs=_attn_specs(aw, kvw),
        out_specs=pl.BlockSpec((BLOCK, aw), lambda n: (n, 0)),
        out_shape=_sds((t, aw), BF16),
        compiler_params=_cparams("parallel"),
    )(proj, proj, proj, proj, proj, cos, sin, cos, sin, sinks)


def _attn_bwd(proj, cos, sin, sinks, do, aw, kvw):
    t = proj.shape[0]
    nkv = kvw // HEAD_DIM
    qpk = aw // kvw
    scale = HEAD_DIM ** -0.5

    def body(q_ref, kc_ref, kp_ref, vc_ref, vp_ref, cc_ref, sc_ref, cp_ref, sp_ref, sinks_ref, do_ref,
             dq_ref, dkc_ref, dkp_ref, dvc_ref, dvp_ref, ds_ref):
        n = pl.program_id(0)
        cc, sc, cp, sp = cc_ref[...], sc_ref[...], cp_ref[...], sp_ref[...]
        q = _rope(q_ref[...].astype(F32), cc, sc).astype(BF16)
        kc = _rope(kc_ref[...].astype(F32), cc, sc).astype(BF16)
        kp = _rope(kp_ref[...].astype(F32), cp, sp).astype(BF16)
        vc = vc_ref[...].astype(BF16)
        vp = vp_ref[...].astype(BF16)
        dout = do_ref[...]
        dqs, dks, dvs = [], [], []
        lane = lax.broadcasted_iota(jnp.int32, (8, LANES), 1)
        row0 = lax.broadcasted_iota(jnp.int32, (8, LANES), 0) == 0
        dsink = jnp.zeros((8, LANES), F32)
        for h in range(nkv):
            hs = slice(h * HEAD_DIM, (h + 1) * HEAD_DIM)
            kb = jnp.concatenate([kp[:, hs], kc[:, hs]], axis=0)
            vb = jnp.concatenate([vp[:, hs], vc[:, hs]], axis=0)
            qs = _heads(q, h * qpk, qpk)
            dos = _heads(dout, h * qpk, qpk)
            p, psink, rowg = _attn_probs(qs, kb, n, h, qpk, sinks_ref)
            pb = p.astype(BF16)
            o = jnp.dot(pb, vb, preferred_element_type=F32)
            delta = jnp.sum(dos.astype(F32) * o, axis=-1, keepdims=True)
            dvs.append(lax.dot_general(pb, dos, (((0,), (0,)), ((), ())), preferred_element_type=F32))
            dp = lax.dot_general(dos, vb, (((1,), (1,)), ((), ())), preferred_element_type=F32)
            dsc = (p * (dp - delta)).astype(BF16)
            dq_h = jnp.dot(dsc, kb, preferred_element_type=F32) * scale
            dqs += [dq_h[g * BLOCK:(g + 1) * BLOCK] for g in range(qpk)]
            dks.append(lax.dot_general(dsc, qs, (((0,), (0,)), ((), ())), preferred_element_type=F32) * scale)
            sink_term = psink * delta
            for g in range(qpk):
                val = -jnp.sum(jnp.where(rowg == g, sink_term, 0.0))
                dsink = jnp.where(row0 & (lane == h * qpk + g), val, dsink)
        dq = jnp.concatenate(dqs, axis=1)
        dq_ref[...] = _rope(dq, cc, -sc).astype(dq_ref.dtype)
        dk = jnp.concatenate(dks, axis=1)
        dv = jnp.concatenate(dvs, axis=1)
        dkp_ref[...] = _rope(dk[:BLOCK], cp, -sp)
        dkc_ref[...] = _rope(dk[BLOCK:], cc, -sc)
        dvp_ref[...] = dv[:BLOCK]
        dvc_ref[...] = dv[BLOCK:]

        @pl.when(n == 0)
        def _():
            ds_ref[...] = dsink

        @pl.when(n > 0)
        def _():
            ds_ref[...] += dsink

    kv_spec = pl.BlockSpec((BLOCK, kvw), lambda n: (n, 0))
    return pl.pallas_call(
        body, name="attn_bwd", grid=(t // BLOCK,),
        in_specs=_attn_specs(aw, kvw) + [pl.BlockSpec((BLOCK, aw), lambda n: (n, 0))],
        out_specs=[pl.BlockSpec((BLOCK, aw), lambda n: (n, 0)), kv_spec, kv_spec, kv_spec, kv_spec,
                   pl.BlockSpec((8, LANES), lambda n: (0, 0))],
        out_shape=[_sds((t, aw), BF16)] + [_sds((t, kvw), F32)] * 4 + [_sds((8, LANES), F32)],
        compiler_params=_cparams("arbitrary"),
    )(proj, proj, proj, proj, proj, cos, sin, cos, sin, sinks, do)


HALO = 16


def _shift_down(v, k, halo):
    rows = lax.broadcasted_iota(jnp.int32, v.shape, 0)
    out = pltpu.roll(v, k, 0)
    for r in range(k):
        out = jnp.where(rows == r, halo[HALO - k + r:HALO - k + r + 1, :], out)
    return out


def _shift_up(v, k, halo):
    tm = v.shape[0]
    rows = lax.broadcasted_iota(jnp.int32, v.shape, 0)
    out = pltpu.roll(v, tm - k, 0)
    for r in range(k):
        out = jnp.where(rows == tm - k + r, halo[r:r + 1, :], out)
    return out


def _conv_fwd(proj, conv_w, zoff, cw, cb):
    t = proj.shape[0]
    tm = _pick(t, (512, 256, 128))
    zb, nb = zoff // cb, cw // cb
    hb = tm // HALO

    def body(z_ref, gb_ref, gc_ref, zp_ref, gcp_ref, w_ref, o_ref):
        i = pl.program_id(0)
        cz = gc_ref[...].astype(F32) * z_ref[...].astype(F32)
        czp = gcp_ref[...].astype(F32) * zp_ref[...].astype(F32) * (i > 0).astype(F32)
        w = w_ref[...]
        y = w[0:1] * _shift_down(cz, 2, czp) + w[1:2] * _shift_down(cz, 1, czp) + w[2:3] * cz
        o_ref[...] = (gb_ref[...].astype(F32) * y).astype(o_ref.dtype)

    def col(k):
        return pl.BlockSpec((tm, cb), lambda i, j: (i, zb + k * nb + j))

    def halo(k):
        return pl.BlockSpec((HALO, cb), lambda i, j: (jnp.maximum(i * hb - 1, 0), zb + k * nb + j))

    return pl.pallas_call(
        body, name="conv_fwd", grid=(t // tm, nb),
        in_specs=[col(0), col(1), col(2), halo(0), halo(2), pl.BlockSpec((3, cb), lambda i, j: (0, j))],
        out_specs=pl.BlockSpec((tm, cb), lambda i, j: (i, j)),
        out_shape=_sds((t, cw), BF16),
        compiler_params=_cparams("parallel", "parallel"),
    )(proj, proj, proj, proj, proj, conv_w)


def _conv_bwd(proj, conv_w, dco, zoff, cw, cb):
    t = proj.shape[0]
    tm = _pick(t, (512, 256, 128))
    zb, nb = zoff // cb, cw // cb
    hb = tm // HALO
    nt = t // tm

    def body(z_ref, gb_ref, gc_ref, zp_ref, gcp_ref, gbn_ref, w_ref, d_ref, dn_ref, dz_ref, dgb_ref, dgc_ref, dw_ref):
        i = pl.program_id(1)
        z, gb, gc = z_ref[...].astype(F32), gb_ref[...].astype(F32), gc_ref[...].astype(F32)
        d = d_ref[...].astype(F32)
        cz = gc * z
        czp = gcp_ref[...].astype(F32) * zp_ref[...].astype(F32) * (i > 0).astype(F32)
        w = w_ref[...]
        cz1 = _shift_down(cz, 1, czp)
        cz2 = _shift_down(cz, 2, czp)
        y = w[0:1] * cz2 + w[1:2] * cz1 + w[2:3] * cz
        dgb_ref[...] = (d * y).astype(dgb_ref.dtype)
        dy = d * gb
        dyn = dn_ref[...].astype(F32) * gbn_ref[...].astype(F32) * (i < nt - 1).astype(F32)
        dcz = w[2:3] * dy + w[1:2] * _shift_up(dy, 1, dyn) + w[0:1] * _shift_up(dy, 2, dyn)
        dgc_ref[...] = (dcz * z).astype(dgc_ref.dtype)
        dz_ref[...] = (dcz * gc).astype(dz_ref.dtype)
        rows = lax.broadcasted_iota(jnp.int32, (8, cb), 0)
        dw = jnp.zeros((8, cb), F32)
        for r, tap in enumerate((cz2, cz1, cz)):
            dw = jnp.where(rows == r, jnp.sum(dy * tap, axis=0, keepdims=True), dw)

        @pl.when(i == 0)
        def _():
            dw_ref[...] = dw

        @pl.when(i > 0)
        def _():
            dw_ref[...] += dw

    def col(k):
        return pl.BlockSpec((tm, cb), lambda j, i: (i, zb + k * nb + j))

    def halo_prev(k):
        return pl.BlockSpec((HALO, cb), lambda j, i: (jnp.maximum(i * hb - 1, 0), zb + k * nb + j))

    own = pl.BlockSpec((tm, cb), lambda j, i: (i, j))
    nxt = lambda i: jnp.minimum((i + 1) * hb, t // HALO - 1)
    return pl.pallas_call(
        body, name="conv_bwd", grid=(nb, nt),
        in_specs=[col(0), col(1), col(2), halo_prev(0), halo_prev(2),
                  pl.BlockSpec((HALO, cb), lambda j, i: (nxt(i), zb + nb + j)),
                  pl.BlockSpec((3, cb), lambda j, i: (0, j)), own,
                  pl.BlockSpec((HALO, cb), lambda j, i: (nxt(i), j))],
        out_specs=[own, own, own, pl.BlockSpec((8, cb), lambda j, i: (0, j))],
        out_shape=[_sds((t, cw), BF16)] * 3 + [_sds((8, cw), F32)],
        compiler_params=_cparams("parallel", "arbitrary"),
    )(proj, proj, proj, proj, proj, proj, conv_w, dco, dco)


def _merge_fwd(proj, ya, yc, goff, cb):
    t, d = ya.shape
    tm = _pick(t, (512, 256, 128))
    gb_, nb = goff // cb, d // cb

    def body(ga_ref, gc_ref, ya_ref, yc_ref, o_ref):
        f = lambda r: r[...].astype(F32)
        o_ref[...] = (_sigmoid(f(ga_ref)) * f(ya_ref) + _sigmoid(f(gc_ref)) * f(yc_ref)).astype(o_ref.dtype)

    own = pl.BlockSpec((tm, cb), lambda i, j: (i, j))
    return pl.pallas_call(
        body, name="merge_fwd", grid=(t // tm, nb),
        in_specs=[pl.BlockSpec((tm, cb), lambda i, j: (i, gb_ + j)),
                  pl.BlockSpec((tm, cb), lambda i, j: (i, gb_ + nb + j)), own, own],
        out_specs=own, out_shape=_sds((t, d), BF16),
        compiler_params=_cparams("parallel", "parallel"),
    )(proj, proj, ya, yc)


def _merge_bwd(proj, ya, yc, dm, goff, cb):
    t, d = ya.shape
    tm = _pick(t, (512, 256, 128))
    gb_, nb = goff // cb, d // cb

    def body(ga_ref, gc_ref, ya_ref, yc_ref, dm_ref, dya_ref, dyc_ref, dga_ref, dgc_ref):
        dmv = dm_ref[...].astype(F32)
        sa = _sigmoid(ga_ref[...].astype(F32))
        sc = _sigmoid(gc_ref[...].astype(F32))
        dya_ref[...] = (dmv * sa).astype(BF16)
        dyc_ref[...] = (dmv * sc).astype(BF16)
        dga_ref[...] = (dmv * ya_ref[...].astype(F32) * sa * (1.0 - sa)).astype(BF16)
        dgc_ref[...] = (dmv * yc_ref[...].astype(F32) * sc * (1.0 - sc)).astype(BF16)

    own = pl.BlockSpec((tm, cb), lambda i, j: (i, j))
    return pl.pallas_call(
        body, name="merge_bwd", grid=(t // tm, nb),
        in_specs=[pl.BlockSpec((tm, cb), lambda i, j: (i, gb_ + j)),
                  pl.BlockSpec((tm, cb), lambda i, j: (i, gb_ + nb + j)), own, own, own],
        out_specs=[own] * 4, out_shape=[_sds((t, d), BF16)] * 4,
        compiler_params=_cparams("parallel", "parallel"),
    )(proj, proj, ya, yc, dm)


def _assemble_dproj(dq, dkc, dkp, dvc, dvp, dz, dgb, dgc, dga, dgg):
    t, aw = dq.shape
    kvw, cw, d = dkc.shape[1], dz.shape[1], dga.shape[1]
    nblk = t // BLOCK
    width = aw + 2 * kvw + 3 * cw + 2 * d

    def body(dq_ref, dkc_ref, dkp_ref, dvc_ref, dvp_ref, dz_ref, dgb_ref, dgc_ref, dga_ref, dgg_ref, o_ref):
        keep = (pl.program_id(0) < nblk - 1).astype(F32)
        dk = dkc_ref[...] + dkp_ref[...] * keep
        dv = dvc_ref[...] + dvp_ref[...] * keep
        o_ref[...] = jnp.concatenate(
            [dq_ref[...], dk.astype(BF16), dv.astype(BF16), dz_ref[...], dgb_ref[...], dgc_ref[...],
             dga_ref[...], dgg_ref[...]], axis=1)

    def cur(w):
        return pl.BlockSpec((BLOCK, w), lambda n: (n, 0))

    def nxt(w):
        return pl.BlockSpec((BLOCK, w), lambda n: (jnp.minimum(n + 1, nblk - 1), 0))

    return pl.pallas_call(
        body, name="assemble_dproj", grid=(nblk,),
        in_specs=[cur(aw), cur(kvw), nxt(kvw), cur(kvw), nxt(kvw), cur(cw), cur(cw), cur(cw), cur(d), cur(d)],
        out_specs=cur(width), out_shape=_sds((t, width), BF16),
        compiler_params=_cparams("parallel"),
    )(dq, dkc, dkp, dvc, dvp, dz, dgb, dgc, dga, dgg)


def _xattn_probs(qh, kh):
    s = lax.dot_general(qh, kh, (((1,), (1,)), ((), ())), preferred_element_type=F32) * (X_HEAD_DIM ** -0.5)
    e = jnp.exp(s - jnp.max(s, axis=-1, keepdims=True))
    return e * (1.0 / jnp.sum(e, axis=-1, keepdims=True))


def _xattn_fwd(xq, kv):
    t, xw = xq.shape
    mt = kv.shape[0]
    tm = _pick(t, (512, 256, 128))

    def body(q_ref, kv_ref, o_ref):
        outs = []
        for hd in range(xw // X_HEAD_DIM):
            hs = slice(hd * X_HEAD_DIM, (hd + 1) * X_HEAD_DIM)
            vs = slice(xw + hd * X_HEAD_DIM, xw + (hd + 1) * X_HEAD_DIM)
            p = _xattn_probs(q_ref[:, hs], kv_ref[:, hs])
            outs.append(jnp.dot(p.astype(BF16), kv_ref[:, vs], preferred_element_type=F32))
        o_ref[...] = jnp.concatenate(outs, axis=1).astype(o_ref.dtype)

    return pl.pallas_call(
        body, name="xattn_fwd", grid=(t // tm,),
        in_specs=[pl.BlockSpec((tm, xw), lambda i: (i, 0)), pl.BlockSpec((mt, 2 * xw), lambda i: (0, 0))],
        out_specs=pl.BlockSpec((tm, xw), lambda i: (i, 0)), out_shape=_sds((t, xw), BF16),
        compiler_params=_cparams("parallel"),
    )(xq, kv)


def _xattn_bwd(xq, kv, do):
    t, xw = xq.shape
    mt = kv.shape[0]
    tm = _pick(t, (512, 256, 128))
    scale = X_HEAD_DIM ** -0.5

    def body(q_ref, kv_ref, do_ref, dq_ref, dkv_ref):
        dqs, dks, dvs = [], [], []
        for hd in range(xw // X_HEAD_DIM):
            hs = slice(hd * X_HEAD_DIM, (hd + 1) * X_HEAD_DIM)
            vs = slice(xw + hd * X_HEAD_DIM, xw + (hd + 1) * X_HEAD_DIM)
            qh, kh, vh, doh = q_ref[:, hs], kv_ref[:, hs], kv_ref[:, vs], do_ref[:, hs]
            p = _xattn_probs(qh, kh)
            pb = p.astype(BF16)
            o = jnp.dot(pb, vh, preferred_element_type=F32)
            delta = jnp.sum(doh.astype(F32) * o, axis=-1, keepdims=True)
            dvs.append(lax.dot_general(pb, doh, (((0,), (0,)), ((), ())), preferred_element_type=F32))
            dp = lax.dot_general(doh, vh, (((1,), (1,)), ((), ())), preferred_element_type=F32)
            dsc = (p * (dp - delta)).astype(BF16)
            dqs.append(jnp.dot(dsc, kh, preferred_element_type=F32) * scale)
            dks.append(lax.dot_general(dsc, qh, (((0,), (0,)), ((), ())), preferred_element_type=F32) * scale)
        dq_ref[...] = jnp.concatenate(dqs, axis=1).astype(dq_ref.dtype)
        dkv = jnp.concatenate(dks + dvs, axis=1)

        @pl.when(pl.program_id(0) == 0)
        def _():
            dkv_ref[...] = dkv

        @pl.when(pl.program_id(0) > 0)
        def _():
            dkv_ref[...] += dkv

    row = pl.BlockSpec((tm, xw), lambda i: (i, 0))
    whole = pl.BlockSpec((mt, 2 * xw), lambda i: (0, 0))
    return pl.pallas_call(
        body, name="xattn_bwd", grid=(t // tm,),
        in_specs=[row, whole, row], out_specs=[row, whole],
        out_shape=[_sds((t, xw), BF16), _sds((mt, 2 * xw), F32)],
        compiler_params=_cparams("arbitrary"),
    )(xq, kv, do)


def _swiglu_fwd(hid):
    t, f2 = hid.shape
    f = f2 // 2
    tm = _pick(t, (256, 128))

    def body(h_ref, o_ref):
        a = h_ref[:, :f].astype(F32)
        b = h_ref[:, f:].astype(F32)
        o_ref[...] = (a * _sigmoid(a) * b).astype(o_ref.dtype)

    return pl.pallas_call(
        body, name="swiglu_fwd", grid=(t // tm,),
        in_specs=[pl.BlockSpec((tm, f2), lambda i: (i, 0))],
        out_specs=pl.BlockSpec((tm, f), lambda i: (i, 0)), out_shape=_sds((t, f), BF16),
        compiler_params=_cparams("parallel"),
    )(hid)


def _swiglu_bwd(hid, dact):
    t, f2 = hid.shape
    f = f2 // 2
    tm = _pick(t, (128,))

    def body(h_ref, d_ref, o_ref):
        a = h_ref[:, :f].astype(F32)
        b = h_ref[:, f:].astype(F32)
        d = d_ref[...].astype(F32)
        sg = _sigmoid(a)
        o_ref[:, :f] = (d * b * sg * (1.0 + a * (1.0 - sg))).astype(o_ref.dtype)
        o_ref[:, f:] = (d * a * sg).astype(o_ref.dtype)

    return pl.pallas_call(
        body, name="swiglu_bwd", grid=(t // tm,),
        in_specs=[pl.BlockSpec((tm, f2), lambda i: (i, 0)), pl.BlockSpec((tm, f), lambda i: (i, 0))],
        out_specs=pl.BlockSpec((tm, f2), lambda i: (i, 0)), out_shape=_sds((t, f2), BF16),
        compiler_params=_cparams("parallel"),
    )(hid, dact)


def _adamw(w, g, m, v, name):
    r, c = w.shape
    tr = _pick(r, (256, 128, 64, 32, 16, 8)) if r * c > 65536 else r

    def body(w_ref, g_ref, m_ref, v_ref, d_ref, nm_ref, nv_ref):
        gv = g_ref[...]
        m2 = ADAM_B1 * m_ref[...] + (1.0 - ADAM_B1) * gv
        v2 = ADAM_B2 * v_ref[...] + (1.0 - ADAM_B2) * (gv * gv)
        m_hat = m2 / (1.0 - ADAM_B1 ** ADAM_STEP)
        v_hat = v2 / (1.0 - ADAM_B2 ** ADAM_STEP)
        d_ref[...] = -ADAM_LR * (m_hat / (jnp.sqrt(v_hat) + ADAM_EPS) + ADAM_WD * w_ref[...])
        nm_ref[...] = m2
        nv_ref[...] = v2

    blk = pl.BlockSpec((tr, c), lambda i: (i, 0))
    return pl.pallas_call(
        body, name=name, grid=(r // tr,),
        in_specs=[blk] * 4, out_specs=[blk] * 3, out_shape=[_sds((r, c), F32)] * 3,
        compiler_params=_cparams("parallel"),
    )(w, g, m, v)


class _Weights:
    def __init__(self, full):
        self.full = full
        self.grads = {}

    def get(self, name):
        return self.full[name]

    def mark(self, tag, value):
        return value

    def grad(self, name, g):
        self.grads[name] = g

    def dep(self):
        return None


def _local_step(x, mem, tgt, g_mix, sinks, g_xattn, g_mem, g_ffn, g_final, dims, wts):
    t, d = x.shape
    aw, cw, kvw = dims
    cb = 2 * kvw
    zoff = aw + 2 * kvw
    goff = zoff + 3 * cw
    cos, sin = _rope_tables(t)
    mark, get = wts.mark, wts.get

    def mm(a, b, **kw):
        return _matmul(a, b, dep=wts.dep(), **kw)

    u1 = mark("u1", _rms_fwd(x, g_mix, "rms_mix"))
    proj = mark("proj", mm(u1, get("w_in"), mode="nn", out_dtype=F32, name="mm_proj"))
    attn_o = mark("attn_o", _attn_fwd(proj, cos, sin, sinks, aw, kvw))
    conv_o = mark("conv_o", _conv_fwd(proj, get("conv_w"), zoff, cw, cb))
    ya = mark("ya", mm(attn_o, get("w_attn_proj"), mode="nn", out_dtype=BF16, name="mm_yattn"))
    yc = mark("yc", mm(conv_o, get("w_conv_proj"), mode="nn", out_dtype=BF16, name="mm_yconv"))
    merged = mark("merged", _merge_fwd(proj, ya, yc, goff, cb))
    h1 = mark("h1", mm(merged, get("w_mix_out"), mode="nn", out_dtype=F32, name="mm_mix", res=x))
    u2 = mark("u2", _rms_fwd(h1, g_xattn, "rms_xattn"))
    mem_n = _rms_fwd(mem, g_mem, "rms_mem")
    xq = mark("xq", mm(u2, get("w_xq"), mode="nn", out_dtype=BF16, name="mm_xq"))
    kv = mm(mem_n, get("w_xkv"), mode="nn", out_dtype=BF16, name="mm_xkv")
    xo = mark("xo", _xattn_fwd(xq, kv))
    h2 = mark("h2", mm(xo, get("w_xo"), mode="nn", out_dtype=F32, name="mm_xo", res=h1))
    u3 = mark("u3", _rms_fwd(h2, g_ffn, "rms_ffn"))
    hid = mark("hid", mm(u3, get("w_ffn_in"), mode="nn", out_dtype=BF16, name="mm_ffn_in"))
    act = mark("act", _swiglu_fwd(hid))
    h3 = mm(act, get("w_ffn_out"), mode="nn", out_dtype=F32, name="mm_ffn_out", res=h2)
    dh3, dh3b, dg_final, loss = _loss_head(h3, g_final, tgt)
    mark("dh3", dh3b)

    wts.grad("w_ffn_out", mm(act, dh3b, mode="tn", out_dtype=BF16, name="mm_dw_ffn_out"))
    dact = mark("dact", mm(dh3b, get("w_ffn_out"), mode="nt", out_dtype=BF16, name="mm_dact"))
    dhid = mark("dhid", _swiglu_bwd(hid, dact))
    wts.grad("w_ffn_in", mm(u3, dhid, mode="tn", out_dtype=BF16, name="mm_dw_ffn_in"))
    du3 = mark("du3", mm(dhid, get("w_ffn_in"), mode="nt", out_dtype=F32, name="mm_du3"))
    dh2, dh2b, dg_ffn = _rms_bwd(h2, g_ffn, du3, dh3, "rms_bwd_ffn")
    mark("dh2", dh2b)
    wts.grad("w_xo", mm(xo, dh2b, mode="tn", out_dtype=BF16, name="mm_dw_xo"))
    dxo = mm(dh2b, get("w_xo"), mode="nt", out_dtype=BF16, name="mm_dxo")
    dxq, dkv = _xattn_bwd(xq, kv, dxo)
    dkvb = dkv.astype(BF16)
    wts.grad("w_xq", mm(u2, dxq, mode="tn", out_dtype=BF16, name="mm_dw_xq"))
    du2 = mm(dxq, get("w_xq"), mode="nt", out_dtype=F32, name="mm_du2")
    wts.grad("w_xkv", mm(mem_n, dkvb, mode="tn", out_dtype=BF16, name="mm_dw_xkv"))
    dmem_n = mm(dkvb, get("w_xkv"), mode="nt", out_dtype=F32, name="mm_dmem")
    _, _, dg_mem = _rms_bwd(mem, g_mem, dmem_n, None, "rms_bwd_mem")
    dh1, dh1b, dg_xattn = _rms_bwd(h1, g_xattn, du2, dh2, "rms_bwd_xattn")
    mark("dh1", dh1b)
    wts.grad("w_mix_out", mm(merged, dh1b, mode="tn", out_dtype=BF16, name="mm_dw_mix"))
    dmerged = mm(dh1b, get("w_mix_out"), mode="nt", out_dtype=BF16, name="mm_dmerged")
    dya, dyc, dga, dgg = _merge_bwd(proj, ya, yc, dmerged, goff, cb)
    mark("dya", dya)
    wts.grad("w_attn_proj", mm(attn_o, dya, mode="tn", out_dtype=BF16, name="mm_dw_attn_proj"))
    dattn_o = mm(dya, get("w_attn_proj"), mode="nt", out_dtype=BF16, name="mm_dattn")
    wts.grad("w_conv_proj", mm(conv_o, dyc, mode="tn", out_dtype=BF16, name="mm_dw_conv_proj"))
    dconv_o = mark("dconv_o", mm(dyc, get("w_conv_proj"), mode="nt", out_dtype=BF16, name="mm_dconv"))
    dz, dgb, dgc, dconv_w = _conv_bwd(proj, get("conv_w"), dconv_o, zoff, cw, cb)
    mark("dz", dz)
    dq, dkc, dkp, dvc, dvp, dsinks = _attn_bwd(proj, cos, sin, sinks, dattn_o, aw, kvw)
    mark("dq", dq)
    dproj = mark("dproj", _assemble_dproj(dq, dkc, dkp, dvc, dvp, dz, dgb, dgc, dga, dgg))
    wts.grad("w_in", mm(u1, dproj, mode="tn", out_dtype=BF16, name="mm_dw_in"))
    du1 = mark("du1", mm(dproj, get("w_in"), mode="nt", out_dtype=F32, name="mm_du1"))
    grad_x, _, dg_mix = _rms_bwd(x, g_mix, du1, dh1, "rms_bwd_mix")
    mark("grad_x", grad_x)

    small = dict(g_mix=dg_mix, g_xattn=dg_xattn, g_mem=dg_mem, g_ffn=dg_ffn, g_final=dg_final,
                 conv_w=dconv_w[:3], attn_sinks=dsinks[0:1, :sinks.shape[1]], loss=loss[0:1, 0:1])
    return grad_x, small


BIG = (("w_in", 1), ("w_attn_proj", 1), ("w_conv_proj", 1), ("w_mix_out", 0), ("w_xq", 0), ("w_xkv", 0),
       ("w_xo", 1), ("w_ffn_in", 1), ("w_ffn_out", 0))


def _place():
    x, y, c = lax.axis_index("x"), lax.axis_index("y"), lax.axis_index("c")
    chips = [(1 - x, y), (x, 1 - y), (1 - x, 1 - y)]
    return x, y, c, chips


def _window(ref, ax, shard_shape, s, h):
    sr, sc = shard_shape
    hr = sr // 2
    if ax == 1:
        return ref.at[pl.ds(pl.multiple_of(h * hr, 16), hr), pl.ds(pl.multiple_of(s * sc, LANES), sc)]
    return ref.at[pl.ds(pl.multiple_of(s * sr + h * hr, 16), hr), :]


def _half(ref, h):
    hr = ref.shape[0] // 2
    return ref.at[pl.ds(pl.multiple_of(h * hr, 16), hr), :]


def _remote(src, dst, send_sem, recv_sem, dev):
    return pltpu.make_async_remote_copy(src_ref=src, dst_ref=dst, send_sem=send_sem, recv_sem=recv_sem,
                                        device_id=dev, device_id_type=MESH)


def _cast_to_full(shard, ax, me, name, dtype=BF16):
    sr, sc = shard.shape
    tr = _pick(sr, (256, 352, 128, 64, 32, 16))
    nr = sr // tr
    full = (sr * N_CHIPS, sc) if ax == 0 else (sr, sc * N_CHIPS)

    def body(me_ref, s_ref, o_ref):
        o_ref[...] = s_ref[...].astype(o_ref.dtype)

    if ax == 1:
        out_spec = pl.BlockSpec((tr, sc), lambda r, me_ref: (r, me_ref[0]))
    else:
        out_spec = pl.BlockSpec((tr, sc), lambda r, me_ref: (me_ref[0] * nr + r, 0))
    return pl.pallas_call(
        body, name=name,
        grid_spec=pltpu.PrefetchScalarGridSpec(
            num_scalar_prefetch=1, grid=(nr,), in_specs=[pl.BlockSpec((tr, sc), lambda r, me_ref: (r, 0))],
            out_specs=out_spec),
        out_shape=_sds(full, dtype),
        compiler_params=_cparams("parallel"),
    )(me, shard)


HBM = pl.BlockSpec(memory_space=pltpu.HBM)
SEM = pl.BlockSpec(memory_space=pltpu.SEMAPHORE)
EFFECT = pltpu.SideEffectType.DATAFLOW_SIDE_EFFECTING


def _in_hbm(a):
    return pltpu.with_memory_space_constraint(a, pltpu.HBM)


def _gather_window(ref, ax, shard_shape, s, h):
    if h is not None:
        return _window(ref, ax, shard_shape, s, h)
    sr, sc = shard_shape
    if ax == 1:
        return ref.at[:, pl.ds(pl.multiple_of(s * sc, LANES), sc)]
    return ref.at[pl.ds(pl.multiple_of(s * sr, 8), sr), :]


def _ag_start(fulls, axes, shard_shapes, whole):
    n = len(fulls)

    def body(*refs):
        src = refs[:n]
        send_sems, recv_sems = refs[n], refs[n + 1]
        token = refs[2 * n + 2]
        x, y, c, chips = _place()
        me = 2 * x + y
        for i in range(n):
            h = None if whole[i] else c
            for j, chip in enumerate(chips):
                blk = _gather_window(src[i], axes[i], shard_shapes[i], me, h)
                _remote(blk, blk, send_sems.at[3 * i + j], recv_sems.at[3 * i + j], (*chip, c)).start()
        token[...] = jnp.zeros_like(token)

    res = pl.pallas_call(
        body, name="ag_start_" + str(n),
        out_shape=(pltpu.SemaphoreType.DMA((3 * n,)), pltpu.SemaphoreType.DMA((3 * n,)),
                   *[pltpu.HBM(f.shape, f.dtype) for f in fulls], _sds((8, LANES), F32)),
        in_specs=[HBM] * n, out_specs=(SEM, SEM, *[HBM] * n, pl.BlockSpec(memory_space=pltpu.VMEM)),
        input_output_aliases={i: 2 + i for i in range(n)},
        compiler_params=pltpu.CompilerParams(has_side_effects=EFFECT),
    )(*[_in_hbm(f) for f in fulls])
    return res[0], res[1], list(res[2:2 + n]), res[2 + n]


def _ag_mid(bufs, slots, axes, shard_shapes, whole, send_sems, recv_sems, after, name):
    ng = len(bufs)

    def body(*refs):
        src = refs[:ng]
        s_in, r_in = refs[ng], refs[ng + 1]
        fsend, frecv = refs[ng + 3], refs[ng + 4]
        x, y, c, chips = _place()
        me = 2 * x + y
        sib = (x, y, 1 - c)
        for k, i in enumerate(slots):
            h = None if whole[k] else c
            for j, chip in enumerate(chips):
                cj = 2 * chip[0] + chip[1]
                mine = _gather_window(src[k], axes[k], shard_shapes[k], me, h)
                theirs = _gather_window(src[k], axes[k], shard_shapes[k], cj, h)
                _remote(theirs, theirs, s_in.at[3 * i + j], r_in.at[3 * i + j], (*chip, c)).wait_recv()
                _remote(mine, mine, s_in.at[3 * i + j], r_in.at[3 * i + j], (*chip, c)).wait_send()
                if not whole[k]:
                    _remote(theirs, theirs, fsend.at[3 * k + j], frecv.at[3 * k + j], sib).start()

    res = pl.pallas_call(
        body, name=name,
        out_shape=(pltpu.SemaphoreType.DMA((3 * ng,)), pltpu.SemaphoreType.DMA((3 * ng,)),
                   *[pltpu.HBM(b.shape, b.dtype) for b in bufs]),
        in_specs=[HBM] * ng + [SEM, SEM, ANY], out_specs=(SEM, SEM, *[HBM] * ng),
        input_output_aliases={k: 2 + k for k in range(ng)},
        compiler_params=pltpu.CompilerParams(has_side_effects=EFFECT),
    )(*bufs, send_sems, recv_sems, after)
    return res[0], res[1], list(res[2:])


def _ag_wait(bufs, axes, shard_shapes, whole, fsend, frecv, after, name):
    ng = len(bufs)

    def body(*refs):
        src = refs[:ng]
        s_in, r_in = refs[ng], refs[ng + 1]
        x, y, c, chips = _place()
        sib = (x, y, 1 - c)
        for k in range(ng):
            if whole[k]:
                continue
            for j, chip in enumerate(chips):
                cj = 2 * chip[0] + chip[1]
                sent = _gather_window(src[k], axes[k], shard_shapes[k], cj, c)
                landed = _gather_window(src[k], axes[k], shard_shapes[k], cj, 1 - c)
                _remote(landed, landed, s_in.at[3 * k + j], r_in.at[3 * k + j], sib).wait_recv()
                _remote(sent, sent, s_in.at[3 * k + j], r_in.at[3 * k + j], sib).wait_send()

    res = pl.pallas_call(
        body, name=name,
        out_shape=tuple(pltpu.HBM(b.shape, b.dtype) for b in bufs),
        in_specs=[HBM] * ng + [SEM, SEM, ANY], out_specs=tuple([HBM] * ng),
        input_output_aliases={k: k for k in range(ng)},
        compiler_params=pltpu.CompilerParams(has_side_effects=EFFECT),
    )(*bufs, fsend, frecv, after)
    return list(res)


class _Schedule:
    GROUPS = ((("w_in", "conv_w"), "start", "u1"),
              (("w_attn_proj", "w_conv_proj", "w_mix_out"), "proj", "conv_o"),
              (("w_xq", "w_xkv", "w_xo"), "ya", "h1"),
              (("w_ffn_in",), "h1", "h2"),
              (("w_ffn_out",), "u3", "hid"))
    STARTS = ((0,), (1, 2, 3, 4))
    REDUCE = ((("w_ffn_out",), "dact", "grad:w_ffn_in", "du3"),
              (("w_ffn_in",), "grad:w_ffn_in", "dya", "grad_x"),
              (("w_xo", "w_xq", "w_xkv"), "dh1", "dconv_o", "grad_x"),
              (("w_mix_out", "w_attn_proj", "w_conv_proj"), "dconv_o", "dq", "grad_x"),
              (("w_in",), "grad:w_in", "end", "end2"))

    def __init__(self, seed, axes, shard_shapes, place, on_ready):
        self.ax, self.shape, self.place, self.on_ready = axes, shard_shapes, place, on_ready
        self.stage, self.buf, self.slot, self.passes = {}, {}, {}, {}
        self.ready = set()
        self.grads = {}
        token = None
        for groups in self.STARTS:
            order = [nm for g in groups for nm in self.GROUPS[g][0]]
            send, recv, bufs, token = _ag_start([seed(nm, token) for nm in order], *self._meta(order))
            self.buf.update(zip(order, bufs))
            self.slot.update({nm: (send, recv, k) for k, nm in enumerate(order)})
        self.token = self.latest = token
        self.mark("start", token)

    def _meta(self, names):
        return ([self.ax[nm] for nm in names], [self.shape[nm] for nm in names], [nm == "conv_w" for nm in names])

    def mark(self, tag, value):
        for g, (names, mid, wait) in enumerate(self.GROUPS):
            if tag == mid:
                send, recv, _ = self.slot[names[0]]
                fs, fr, bufs = _ag_mid([self.buf[nm] for nm in names], [self.slot[nm][2] for nm in names],
                                       *self._meta(names), send, recv, value, "ag_mid_%d" % g)
                self.buf.update(zip(names, bufs))
                self.passes[g] = (fs, fr)
            if tag == wait:
                fs, fr = self.passes[g]
                bufs = _ag_wait([self.buf[nm] for nm in names], *self._meta(names), fs, fr, value, "ag_wait_%d" % g)
                self.buf.update(zip(names, bufs))
                self.ready.update(names)
        for g, (names, send, total, finish) in enumerate(self.REDUCE):
            st = self.stage.get(g)
            if st is None:
                continue
            ng = len(names)
            if tag == send and st["at"] == "pair":
                arrs = _exchange_wait("rs_pair_wait_%d" % g, st["arrs"], *st["sems"], st["plan"], value)
                parts = [_pair_add(arrs[k], arrs[ng + k], self.ax[nm], self.shape[nm], self.place, "pair_add_" + nm)
                         for k, nm in enumerate(names)]
                plan, nsem = _plan_chip(ng)
                ss, rs, arrs, self.latest = _exchange_start(
                    "rs_chip_start_%d" % g, parts + [lax.empty(p.shape, p.dtype) for p in parts], nsem, plan)
                self.stage[g] = dict(at="chip", arrs=arrs, sems=(ss, rs), plan=plan)
            elif tag == total and st["at"] == "chip":
                arrs = _exchange_wait("rs_chip_wait_%d" % g, st["arrs"], *st["sems"], st["plan"], value)
                halves = [_chip_add(arrs[k], arrs[ng + k], self.place, "chip_add_" + nm) for k, nm in enumerate(names)]
                plan, nsem = _plan_gather(ng)
                ss, rs, arrs, self.latest = _exchange_start("rs_gather_start_%d" % g, halves, nsem, plan)
                self.stage[g] = dict(at="gather", arrs=arrs, sems=(ss, rs), plan=plan)
            elif tag == finish and st["at"] == "gather":
                arrs = _exchange_wait("rs_gather_wait_%d" % g, st["arrs"], *st["sems"], st["plan"], value)
                self.stage[g] = dict(at="done")
                for nm, shard in zip(names, arrs):
                    self.on_ready(nm, shard)
        return value

    def get(self, name):
        assert name in self.ready, name
        return self.buf[name]

    def grad(self, name, g):
        self.grads[name] = g
        for gi, (names, _, _, _) in enumerate(self.REDUCE):
            if name == names[-1]:
                gs = [self.grads[nm] for nm in names]
                plan, nsem = _plan_pair(len(names), [self.ax[nm] for nm in names], [self.shape[nm] for nm in names])
                ss, rs, arrs, self.latest = _exchange_start(
                    "rs_pair_start_%d" % gi, gs + [lax.empty(a.shape, a.dtype) for a in gs], nsem, plan)
                self.stage[gi] = dict(at="pair", arrs=arrs, sems=(ss, rs), plan=plan)
                g = self.latest
        self.mark("grad:" + name, g)

    def dep(self):
        return self.latest


def _exchange_start(name, arrays, nsem, plan):
    n = len(arrays)

    def body(*refs):
        send_sems, recv_sems, token = refs[n], refs[n + 1], refs[2 * n + 2]
        sends, _ = plan(refs[:n])
        for k, (src, dst, dev) in enumerate(sends):
            _remote(src, dst, send_sems.at[k], recv_sems.at[k], dev).start()
        token[...] = jnp.zeros_like(token)

    res = pl.pallas_call(
        body, name=name,
        out_shape=(pltpu.SemaphoreType.DMA((nsem,)), pltpu.SemaphoreType.DMA((nsem,)),
                   *[pltpu.HBM(a.shape, a.dtype) for a in arrays], _sds((8, LANES), F32)),
        in_specs=[HBM] * n, out_specs=(SEM, SEM, *[HBM] * n, pl.BlockSpec(memory_space=pltpu.VMEM)),
        input_output_aliases={i: 2 + i for i in range(n)},
        compiler_params=pltpu.CompilerParams(has_side_effects=EFFECT),
    )(*[_in_hbm(a) for a in arrays])
    return res[0], res[1], list(res[2:2 + n]), res[2 + n]


def _exchange_wait(name, arrays, send_sems, recv_sems, plan, after):
    n = len(arrays)

    def body(*refs):
        s_in, r_in = refs[n], refs[n + 1]
        sends, recvs = plan(refs[:n])
        for k, land in enumerate(recvs):
            _remote(land, land, s_in.at[k], r_in.at[k], sends[k][2]).wait_recv()
        for k, (src, _, dev) in enumerate(sends):
            _remote(src, src, s_in.at[k], r_in.at[k], dev).wait_send()

    res = pl.pallas_call(
        body, name=name,
        out_shape=tuple(pltpu.HBM(a.shape, a.dtype) for a in arrays),
        in_specs=[HBM] * n + [SEM, SEM, ANY], out_specs=tuple([HBM] * n),
        input_output_aliases={i: i for i in range(n)},
        compiler_params=pltpu.CompilerParams(has_side_effects=EFFECT),
    )(*arrays, send_sems, recv_sems, after)
    return list(res)


def _plan_pair(n, axes, shard_shapes):
    def plan(refs):
        g, ra = refs[:n], refs[n:]
        x, y, c, _ = _place()
        sib = (x, y, 1 - c)

        def pieces(ref, i, h):
            if axes[i] == 1:
                return [_half(ref, h)]
            return [_window(ref, 0, shard_shapes[i], s, h) for s in range(N_CHIPS)]

        sends, recvs = [], []
        for i in range(n):
            sends += [(src, dst, sib) for src, dst in zip(pieces(g[i], i, 1 - c), pieces(ra[i], i, 1 - c))]
            recvs += pieces(ra[i], i, c)
        return sends, recvs

    return plan, sum(1 if ax == 1 else N_CHIPS for ax in axes)


def _plan_chip(n):
    def plan(refs):
        p, rc = refs[:n], refs[n:]
        x, y, c, chips = _place()
        me = 2 * x + y
        sends, recvs = [], []
        for i in range(n):
            for chip in chips:
                cj = 2 * chip[0] + chip[1]
                sends.append((p[i].at[cj], rc[i].at[me], (*chip, c)))
                recvs.append(rc[i].at[cj])
        return sends, recvs

    return plan, 3 * n


def _plan_gather(n):
    def plan(refs):
        x, y, c, _ = _place()
        sib = (x, y, 1 - c)
        return ([(_half(r, c), _half(r, c), sib) for r in refs], [_half(r, 1 - c) for r in refs])

    return plan, n


def _pair_add(g, ra, ax, shard_shape, place, name):
    sr, sc = shard_shape
    hr = sr // 2
    wc = sc
    tr = _pick(hr, (256, 352, 128, 64, 32, 16))
    nr = hr // tr

    def body(p_ref, a_ref, b_ref, o_ref):
        o_ref[...] = (a_ref[...].astype(F32) + b_ref[...].astype(F32)).astype(o_ref.dtype)

    if ax == 1:
        src = pl.BlockSpec((tr, wc), lambda s, r, p_ref: (p_ref[0] * nr + r, s))
    else:
        src = pl.BlockSpec((tr, wc), lambda s, r, p_ref: (s * 2 * nr + p_ref[0] * nr + r, 0))
    return pl.pallas_call(
        body, name=name,
        grid_spec=pltpu.PrefetchScalarGridSpec(
            num_scalar_prefetch=1, grid=(N_CHIPS, nr), in_specs=[src, src],
            out_specs=pl.BlockSpec((None, tr, wc), lambda s, r, p_ref: (s, r, 0))),
        out_shape=_sds((N_CHIPS, hr, wc), BF16),
        compiler_params=_cparams("parallel", "parallel"),
    )(place, g, ra)


def _chip_add(part, rc, place, name):
    _, hr, wc = rc.shape
    tr = _pick(hr, (256, 352, 128, 64, 32, 16))
    nr = hr // tr

    def body(p_ref, own_ref, r1_ref, r2_ref, r3_ref, o_ref):
        acc = own_ref[...].astype(F32)
        for r_ref in (r1_ref, r2_ref, r3_ref):
            acc = acc + r_ref[...].astype(F32)
        o_ref[...] = acc

    def slot(k):
        return pl.BlockSpec((None, tr, wc), lambda r, p_ref: ((p_ref[1] + k) % N_CHIPS, r, 0))

    return pl.pallas_call(
        body, name=name,
        grid_spec=pltpu.PrefetchScalarGridSpec(
            num_scalar_prefetch=1, grid=(nr,), in_specs=[slot(0), slot(1), slot(2), slot(3)],
            out_specs=pl.BlockSpec((tr, wc), lambda r, p_ref: (p_ref[0] * nr + r, 0))),
        out_shape=_sds((2 * hr, wc), F32),
        compiler_params=_cparams("parallel"),
    )(place, part, rc, rc, rc)


N_DEV = 8


def _all_reduce_small(buf):
    r, cdim = buf.shape

    def body(x_ref, o_ref, land, send_sems, recv_sems):
        x, y, c, _ = _place()
        me = 4 * x + 2 * y + c
        land[me] = x_ref[...]
        sends = []
        for k in range(1, N_DEV):
            kx, ky, kc = (k >> 2) & 1, (k >> 1) & 1, k & 1
            peer = (1 - x if kx else x, 1 - y if ky else y, 1 - c if kc else c)
            cp = _remote(x_ref, land.at[me], send_sems.at[k - 1], recv_sems.at[k - 1], peer)
            cp.start()
            sends.append(cp)
        for k in range(1, N_DEV):
            kx, ky, kc = (k >> 2) & 1, (k >> 1) & 1, k & 1
            peer = (1 - x if kx else x, 1 - y if ky else y, 1 - c if kc else c)
            pidx = 4 * peer[0] + 2 * peer[1] + peer[2]
            _remote(land.at[pidx], land.at[pidx], send_sems.at[k - 1], recv_sems.at[k - 1], peer).wait_recv()
        for cp in sends:
            cp.wait_send()
        acc = land[0]
        for dev in range(1, N_DEV):
            acc = acc + land[dev]
        o_ref[...] = acc

    vm = pl.BlockSpec(memory_space=pltpu.VMEM)
    return pl.pallas_call(
        body, name="all_reduce_small", in_specs=[vm], out_specs=vm, out_shape=_sds((r, cdim), F32),
        scratch_shapes=[pltpu.VMEM((N_DEV, r, cdim), F32), pltpu.SemaphoreType.DMA((N_DEV - 1,)),
                        pltpu.SemaphoreType.DMA((N_DEV - 1,))],
    )(buf)


SMALL_ROWS = 16


def kernel(x, mem, g_mix, w_in, conv_w, attn_sinks, w_attn_proj, w_conv_proj, w_mix_out, g_xattn, g_mem, w_xq, w_xkv, w_xo, g_ffn, w_ffn_in, w_ffn_out, g_final, loss_target, m_g_mix, m_w_in, m_conv_w, m_attn_sinks, m_w_attn_proj, m_w_conv_proj, m_w_mix_out, m_g_xattn, m_g_mem, m_w_xq, m_w_xkv, m_w_xo, m_g_ffn, m_w_ffn_in, m_w_ffn_out, m_g_final, v_g_mix, v_w_in, v_conv_w, v_attn_sinks, v_w_attn_proj, v_w_conv_proj, v_w_mix_out, v_g_xattn, v_g_mem, v_w_xq, v_w_xkv, v_w_xo, v_g_ffn, v_w_ffn_in, v_w_ffn_out, v_g_final):
    w = dict(g_mix=g_mix, w_in=w_in[0], conv_w=conv_w[0], attn_sinks=attn_sinks, w_attn_proj=w_attn_proj[0],
             w_conv_proj=w_conv_proj[0], w_mix_out=w_mix_out[0], g_xattn=g_xattn, g_mem=g_mem, w_xq=w_xq[0],
             w_xkv=w_xkv[0], w_xo=w_xo[0], g_ffn=g_ffn, w_ffn_in=w_ffn_in[0], w_ffn_out=w_ffn_out[0],
             g_final=g_final[None])
    m = dict(g_mix=m_g_mix, w_in=m_w_in[0], conv_w=m_conv_w[0], attn_sinks=m_attn_sinks,
             w_attn_proj=m_w_attn_proj[0], w_conv_proj=m_w_conv_proj[0], w_mix_out=m_w_mix_out[0],
             g_xattn=m_g_xattn, g_mem=m_g_mem, w_xq=m_w_xq[0], w_xkv=m_w_xkv[0], w_xo=m_w_xo[0], g_ffn=m_g_ffn,
             w_ffn_in=m_w_ffn_in[0], w_ffn_out=m_w_ffn_out[0], g_final=m_g_final[None])
    v = dict(g_mix=v_g_mix, w_in=v_w_in[0], conv_w=v_conv_w[0], attn_sinks=v_attn_sinks,
             w_attn_proj=v_w_attn_proj[0], w_conv_proj=v_w_conv_proj[0], w_mix_out=v_w_mix_out[0],
             g_xattn=v_g_xattn, g_mem=v_g_mem, w_xq=v_w_xq[0], w_xkv=v_w_xkv[0], w_xo=v_w_xo[0], g_ffn=v_g_ffn,
             w_ffn_in=v_w_ffn_in[0], w_ffn_out=v_w_ffn_out[0], g_final=v_g_final[None])
    names = [nm for nm, _ in BIG]
    axes = [ax for _, ax in BIG]
    d = x.shape[2]
    cw = w["conv_w"].shape[1] * N_CHIPS
    chip = (2 * lax.axis_index("x") + lax.axis_index("y")).astype(jnp.int32)
    place = jnp.stack([lax.axis_index("c").astype(jnp.int32), chip])
    shard_shapes = [w[nm].shape for nm in names]

    def seed(nm, token):
        me1 = chip.reshape(1)
        if token is not None:
            me1 = me1 + token[0, 0].astype(jnp.int32)
        if nm == "conv_w":
            return _cast_to_full(w[nm], 1, me1, "place_conv_w", F32)
        return _cast_to_full(w[nm], dict(BIG)[nm], me1, "cast_" + nm)

    upd = {}

    def on_ready(nm, shard):
        grads[nm] = shard
        upd[nm] = _adamw(w[nm], shard, m[nm], v[nm], "adamw_" + nm)

    grads = {}
    wts = _Schedule(seed, dict(zip(names + ["conv_w"], axes + [1])),
                    dict(zip(names + ["conv_w"], shard_shapes + [w["conv_w"].shape])), place, on_ready)
    aw, cw = w["w_attn_proj"].shape[0], w["w_conv_proj"].shape[0]
    kvw = (w["w_in"].shape[1] * N_CHIPS - aw - 3 * cw - 2 * d) // 2
    grad_x, small = _local_step(
        x[0], mem[0], loss_target[0], w["g_mix"] + wts.token[0:1, 0:1], w["attn_sinks"], w["g_xattn"], w["g_mem"],
        w["g_ffn"], w["g_final"], (aw, cw, kvw), wts)

    pw = max(d, cw)

    def row(a):
        return jnp.pad(a, ((0, 0), (0, pw - a.shape[1])))

    gains = ("g_mix", "g_xattn", "g_mem", "g_ffn", "g_final")
    packed = jnp.concatenate(
        [row(small[nm]) for nm in gains] + [row(small["conv_w"]),
         row(jnp.concatenate([small["attn_sinks"], small["loss"]], axis=1)),
         jnp.zeros((SMALL_ROWS - 9, pw), F32)], axis=0)
    total = _all_reduce_small(packed)
    wts.mark("end", total)
    nsink = attn_sinks.shape[1]
    grads.update({nm: total[k:k + 1, :d] for k, nm in enumerate(gains)})
    grads.update(conv_w=lax.dynamic_slice(total, (5, chip * (cw // N_CHIPS)), (3, cw // N_CHIPS)),
                 attn_sinks=total[8:9, :nsink])
    loss = total[8, nsink]
    for nm in gains + ("conv_w", "attn_sinks"):
        upd[nm] = _adamw(w[nm], grads[nm], m[nm], v[nm], "adamw_" + nm)
    wts.mark("end2", upd["g_final"][0])

    order = ["g_mix", "w_in", "conv_w", "attn_sinks", "w_attn_proj", "w_conv_proj", "w_mix_out", "g_xattn", "g_mem",
             "w_xq", "w_xkv", "w_xo", "g_ffn", "w_ffn_in", "w_ffn_out", "g_final"]

    stacked = set(names) | {"conv_w"}

    def shaped(nm, a):
        if nm == "g_final":
            return a[0]
        return a[None] if nm in stacked else a

    outs = [loss, grad_x[None]]
    outs += [shaped(nm, grads[nm]) for nm in order]
    for k in range(3):
        outs += [shaped(nm, upd[nm][k]) for nm in order]
    return tuple(outs)
```

```python
import functools

import jax
import jax.numpy as jnp
from jax import lax
from jax.experimental import pallas as pl
from jax.experimental.pallas import tpu as pltpu

F32 = jnp.float32
BF16 = jnp.bfloat16

VMEM_LIMIT_BYTES = 56 * 1024 * 1024
LANES = 128
HEAD_DIM = 64
BLOCK = 128
X_HEAD_DIM = 128
ROPE_THETA = 10000.0
EPS = 1e-6
NEG = -1e30
ADAM_LR, ADAM_B1, ADAM_B2, ADAM_EPS, ADAM_WD, ADAM_STEP = 0.001, 0.9, 0.999, 1e-08, 0.01, 10
N_CHIPS = 4
MESH = pl.DeviceIdType.MESH
ANY = pl.BlockSpec(memory_space=pl.ANY)


def _pick(dim, prefs):
    for p in prefs:
        if dim % p == 0:
            return p
    return dim


def _cparams(*sem):
    return pltpu.CompilerParams(dimension_semantics=sem, vmem_limit_bytes=VMEM_LIMIT_BYTES)


def _sds(shape, dtype):
    return jax.ShapeDtypeStruct(shape, dtype)


def _sigmoid(v):
    return 1.0 / (1.0 + jnp.exp(-v))


MATMUL_VMEM_BUDGET = 46 * 1024 * 1024


def _tiles(mode, m, n):
    if mode == "tn" and m % 1024 != 0:
        return _pick(m, (512, 256, 128)), _pick(n, (1024, 512, 256, 128)), True
    return _pick(m, (1024, 512, 256, 128)), _pick(n, (512, 256, 128)), False


def _k_parts(m, n, k):
    tm, tn = _pick(m, (1024, 512, 256, 128)), _pick(n, (512, 256, 128))
    for parts in range(1, k // LANES + 1):
        if k % (parts * LANES) == 0 and 4 * (tm + tn) * (k // parts) + 16 * tm * tn <= MATMUL_VMEM_BUDGET:
            return parts
    return k // LANES


def _matmul(a, b, *, mode, out_dtype, name, res=None, dep=None):
    if mode == "nn":
        (m, k), (k2, n) = a.shape, b.shape
    elif mode == "nt":
        (m, k), (n, k2) = a.shape, b.shape
    else:
        (k, m), (k2, n) = a.shape, b.shape
    assert k == k2, (a.shape, b.shape, mode)
    parts = 1 if mode == "tn" else _k_parts(m, n, k)
    for p in range(parts):
        last = p == parts - 1
        res = _matmul_slice(a, b, mode=mode, out_dtype=out_dtype if last else F32, res=res, dep=dep,
                            kslice=(p, parts), name=name + ("_k%d" % p if parts > 1 else ""))
    return res


def _matmul_slice(a, b, *, mode, out_dtype, name, res, dep, kslice):
    part, parts = kslice
    if mode == "nn":
        (m, k), n = a.shape, b.shape[1]
    elif mode == "nt":
        (m, k), n = a.shape, b.shape[0]
    else:
        (k, m), n = a.shape, b.shape[1]
    tk = k // parts
    tm, tn, swap = _tiles(mode, m, n)
    dims = {"nn": (((1,), (0,)), ((), ())), "nt": (((1,), (1,)), ((), ())), "tn": (((0,), (0,)), ((), ()))}[mode]
    has_res = res is not None
    has_dep = dep is not None

    def body(*refs):
        a_ref, b_ref = refs[0], refs[1]
        o_ref = refs[2 + has_res + has_dep]
        val = lax.dot_general(a_ref[...], b_ref[...], dims, preferred_element_type=F32)
        if has_res:
            val = val + refs[2][...]
        o_ref[...] = val.astype(o_ref.dtype)

    def spec(shape, f):
        if swap:
            return pl.BlockSpec(shape, lambda j, i: f(i, j))
        return pl.BlockSpec(shape, f)

    if mode == "tn":
        a_spec = spec((tk, tm), lambda i, j: (part, i))
    else:
        a_spec = spec((tm, tk), lambda i, j: (i, part))
    if mode == "nt":
        b_spec = spec((tn, tk), lambda i, j: (j, part))
    else:
        b_spec = spec((tk, tn), lambda i, j: (part, j))
    o_spec = spec((tm, tn), lambda i, j: (i, j))
    return pl.pallas_call(
        body,
        name=name,
        grid=(n // tn, m // tm) if swap else (m // tm, n // tn),
        in_specs=[a_spec, b_spec] + ([o_spec] if has_res else []) + ([ANY] if has_dep else []),
        out_specs=o_spec,
        out_shape=_sds((m, n), out_dtype),
        compiler_params=_cparams("parallel", "parallel"),
    )(*([a, b] + ([res] if has_res else []) + ([dep] if has_dep else [])))


def _rms_fwd(x, g, name):
    t, d = x.shape
    tm = _pick(t, (512, 256, 128))

    def body(x_ref, g_ref, o_ref):
        xf = x_ref[...]
        r = lax.rsqrt(jnp.mean(xf * xf, axis=-1, keepdims=True) + EPS)
        o_ref[...] = (xf * r * g_ref[...]).astype(o_ref.dtype)

    row = pl.BlockSpec((tm, d), lambda i: (i, 0))
    return pl.pallas_call(
        body, name=name, grid=(t // tm,),
        in_specs=[row, pl.BlockSpec((1, d), lambda i: (0, 0))],
        out_specs=row, out_shape=_sds((t, d), BF16),
        compiler_params=_cparams("parallel"),
    )(x, g)


def _rms_bwd_math(xf, g, du):
    r = lax.rsqrt(jnp.mean(xf * xf, axis=-1, keepdims=True) + EPS)
    xh = xf * r
    gdy = g * du
    dx = r * (gdy - xh * jnp.mean(gdy * xh, axis=-1, keepdims=True))
    dg = jnp.sum(du * xh, axis=0, keepdims=True)
    return dx, dg


def _rms_bwd(x, g, du, dh, name):
    t, d = x.shape
    tm = _pick(t, (256, 128))
    has_dh = dh is not None

    def body(*refs):
        x_ref, g_ref, du_ref = refs[0], refs[1], refs[2]
        o_ref, ob_ref, dg_ref = refs[3 + has_dh:]
        dx, dg = _rms_bwd_math(x_ref[...], g_ref[...], du_ref[...].astype(F32))
        if has_dh:
            dx = dx + refs[3][...]
        o_ref[...] = dx
        ob_ref[...] = dx.astype(BF16)

        @pl.when(pl.program_id(0) == 0)
        def _():
            dg_ref[...] = dg

        @pl.when(pl.program_id(0) > 0)
        def _():
            dg_ref[...] += dg

    row = pl.BlockSpec((tm, d), lambda i: (i, 0))
    vec = pl.BlockSpec((1, d), lambda i: (0, 0))
    return pl.pallas_call(
        body, name=name, grid=(t // tm,),
        in_specs=[row, vec, row] + ([row] if has_dh else []),
        out_specs=[row, row, vec],
        out_shape=[_sds((t, d), F32), _sds((t, d), BF16), _sds((1, d), F32)],
        compiler_params=_cparams("arbitrary"),
    )(*([x, g, du] + ([dh] if has_dh else [])))


def _loss_head(h, g, tgt):
    t, d = h.shape
    tm = _pick(t, (256, 128))

    def body(h_ref, g_ref, t_ref, o_ref, ob_ref, dg_ref, l_ref):
        xf = h_ref[...]
        gv = g_ref[...]
        r = lax.rsqrt(jnp.mean(xf * xf, axis=-1, keepdims=True) + EPS)
        err = xf * r * gv - t_ref[...]
        part = 0.5 * jnp.sum(jnp.mean(err * err, axis=-1, keepdims=True), axis=0, keepdims=True)
        dx, dg = _rms_bwd_math(xf, gv, err * (1.0 / d))
        o_ref[...] = dx
        ob_ref[...] = dx.astype(BF16)
        lrow = jnp.broadcast_to(part, (1, LANES))

        @pl.when(pl.program_id(0) == 0)
        def _():
            dg_ref[...] = dg
            l_ref[...] = lrow

        @pl.when(pl.program_id(0) > 0)
        def _():
            dg_ref[...] += dg
            l_ref[...] += lrow

    row = pl.BlockSpec((tm, d), lambda i: (i, 0))
    vec = pl.BlockSpec((1, d), lambda i: (0, 0))
    return pl.pallas_call(
        body, name="loss_head", grid=(t // tm,),
        in_specs=[row, vec, row],
        out_specs=[row, row, vec, pl.BlockSpec((1, LANES), lambda i: (0, 0))],
        out_shape=[_sds((t, d), F32), _sds((t, d), BF16), _sds((1, d), F32), _sds((1, LANES), F32)],
        compiler_params=_cparams("arbitrary"),
    )(h, g, tgt)


def _rope_tables(t):
    half = HEAD_DIM // 2
    inv_freq = ROPE_THETA ** (-jnp.arange(half, dtype=F32) / half)
    ang = jnp.arange(t, dtype=F32)[:, None] * inv_freq[None, :]
    cos = jnp.cos(ang)
    sin = jnp.sin(ang)
    reps = LANES // HEAD_DIM
    cos_t = jnp.tile(jnp.concatenate([cos, cos], axis=1), (1, reps))
    sin_t = jnp.tile(jnp.concatenate([-sin, sin], axis=1), (1, reps))
    return cos_t, sin_t


def _rope(v, cos, sin):
    w = v.shape[1]
    c = jnp.tile(cos, (1, w // LANES))
    s = jnp.tile(sin, (1, w // LANES))
    lane = lax.broadcasted_iota(jnp.int32, v.shape, 1)
    first = (lane % HEAD_DIM) < (HEAD_DIM // 2)
    partner = jnp.where(first, pltpu.roll(v, w - HEAD_DIM // 2, 1), pltpu.roll(v, HEAD_DIM // 2, 1))
    return v * c + partner * s


def _heads(v, first, count):
    return jnp.concatenate([v[:, (first + i) * HEAD_DIM:(first + i + 1) * HEAD_DIM] for i in range(count)], axis=0)


def _attn_probs(qs, kb, n, h, qpk, sinks_ref):
    s = lax.dot_general(qs, kb, (((1,), (1,)), ((), ())), preferred_element_type=F32) * (HEAD_DIM ** -0.5)
    qi = lax.broadcasted_iota(jnp.int32, (BLOCK, 2 * BLOCK), 0)
    kc = lax.broadcasted_iota(jnp.int32, (BLOCK, 2 * BLOCK), 1)
    valid = (kc > qi) & (kc <= qi + BLOCK) & ((kc >= BLOCK) | (n > 0))
    bias = jnp.tile(jnp.where(valid, 0.0, NEG).astype(F32), (qpk, 1))
    s = s + bias
    rowg = lax.broadcasted_iota(jnp.int32, (qpk * BLOCK, 1), 0) // BLOCK
    sink = jnp.zeros((qpk * BLOCK, 1), F32)
    for g in range(qpk):
        sink = jnp.where(rowg == g, sinks_ref[0, h * qpk + g], sink)
    m = jnp.maximum(jnp.max(s, axis=-1, keepdims=True), sink)
    e = jnp.exp(s - m)
    es = jnp.exp(sink - m)
    inv = 1.0 / (jnp.sum(e, axis=-1, keepdims=True) + es)
    return e * inv, es * inv, rowg


def _attn_specs(aw, kvw):
    koff = aw // kvw
    prev = lambda n: jnp.maximum(n - 1, 0)
    return [
        pl.BlockSpec((BLOCK, aw), lambda n: (n, 0)),
        pl.BlockSpec((BLOCK, kvw), lambda n: (n, koff)),
        pl.BlockSpec((BLOCK, kvw), lambda n: (prev(n), koff)),
        pl.BlockSpec((BLOCK, kvw), lambda n: (n, koff + 1)),
        pl.BlockSpec((BLOCK, kvw), lambda n: (prev(n), koff + 1)),
        pl.BlockSpec((BLOCK, LANES), lambda n: (n, 0)),
        pl.BlockSpec((BLOCK, LANES), lambda n: (n, 0)),
        pl.BlockSpec((BLOCK, LANES), lambda n: (prev(n), 0)),
        pl.BlockSpec((BLOCK, LANES), lambda n: (prev(n), 0)),
        pl.BlockSpec(memory_space=pltpu.SMEM),
    ]


def _attn_fwd(proj, cos, sin, sinks, aw, kvw):
    t = proj.shape[0]
    nkv = kvw // HEAD_DIM
    qpk = aw // kvw

    def body(q_ref, kc_ref, kp_ref, vc_ref, vp_ref, cc_ref, sc_ref, cp_ref, sp_ref, sinks_ref, o_ref):
        n = pl.program_id(0)
        q = _rope(q_ref[...].astype(F32), cc_ref[...], sc_ref[...]).astype(BF16)
        kc = _rope(kc_ref[...].astype(F32), cc_ref[...], sc_ref[...]).astype(BF16)
        kp = _rope(kp_ref[...].astype(F32), cp_ref[...], sp_ref[...]).astype(BF16)
        vc = vc_ref[...].astype(BF16)
        vp = vp_ref[...].astype(BF16)
        outs = []
        for h in range(nkv):
            hs = slice(h * HEAD_DIM, (h + 1) * HEAD_DIM)
            kb = jnp.concatenate([kp[:, hs], kc[:, hs]], axis=0)
            vb = jnp.concatenate([vp[:, hs], vc[:, hs]], axis=0)
            p, _, _ = _attn_probs(_heads(q, h * qpk, qpk), kb, n, h, qpk, sinks_ref)
            o = jnp.dot(p.astype(BF16), vb, preferred_element_type=F32)
            outs += [o[g * BLOCK:(g + 1) * BLOCK] for g in range(qpk)]
        o_ref[...] = jnp.concatenate(outs, axis=1).astype(o_ref.dtype)

    return pl.pallas_call(
        body, name="attn_fwd", grid=(t // BLOCK,),
        in_specs=_attn_specs(aw, kvw),
        out_specs=pl.BlockSpec((BLOCK, aw), lambda n: (n, 0)),
        out_shape=_sds((t, aw), BF16),
        compiler_params=_cparams("parallel"),
    )(proj, proj, proj, proj, proj, cos, sin, cos, sin, sinks)


def _attn_bwd(proj, cos, sin, sinks, do, aw, kvw):
    t = proj.shape[0]
    nkv = kvw // HEAD_DIM
    qpk = aw // kvw
    scale = HEAD_DIM ** -0.5

    def body(q_ref, kc_ref, kp_ref, vc_ref, vp_ref, cc_ref, sc_ref, cp_ref, sp_ref, sinks_ref, do_ref,
             dq_ref, dkc_ref, dkp_ref, dvc_ref, dvp_ref, ds_ref):
        n = pl.program_id(0)
        cc, sc, cp, sp = cc_ref[...], sc_ref[...], cp_ref[...], sp_ref[...]
        q = _rope(q_ref[...].astype(F32), cc, sc).astype(BF16)
        kc = _rope(kc_ref[...].astype(F32), cc, sc).astype(BF16)
        kp = _rope(kp_ref[...].astype(F32), cp, sp).astype(BF16)
        vc = vc_ref[...].astype(BF16)
        vp = vp_ref[...].astype(BF16)
        dout = do_ref[...]
        dqs, dks, dvs = [], [], []
        lane = lax.broadcasted_iota(jnp.int32, (8, LANES), 1)
        row0 = lax.broadcasted_iota(jnp.int32, (8, LANES), 0) == 0
        dsink = jnp.zeros((8, LANES), F32)
        for h in range(nkv):
            hs = slice(h * HEAD_DIM, (h + 1) * HEAD_DIM)
            kb = jnp.concatenate([kp[:, hs], kc[:, hs]], axis=0)
            vb = jnp.concatenate([vp[:, hs], vc[:, hs]], axis=0)
            qs = _heads(q, h * qpk, qpk)
            dos = _heads(dout, h * qpk, qpk)
            p, psink, rowg = _attn_probs(qs, kb, n, h, qpk, sinks_ref)
            pb = p.astype(BF16)
            o = jnp.dot(pb, vb, preferred_element_type=F32)
            delta = jnp.sum(dos.astype(F32) * o, axis=-1, keepdims=True)
            dvs.append(lax.dot_general(pb, dos, (((0,), (0,)), ((), ())), preferred_element_type=F32))
            dp = lax.dot_general(dos, vb, (((1,), (1,)), ((), ())), preferred_element_type=F32)
            dsc = (p * (dp - delta)).astype(BF16)
            dq_h = jnp.dot(dsc, kb, preferred_element_type=F32) * scale
            dqs += [dq_h[g * BLOCK:(g + 1) * BLOCK] for g in range(qpk)]
            dks.append(lax.dot_general(dsc, qs, (((0,), (0,)), ((), ())), preferred_element_type=F32) * scale)
            sink_term = psink * delta
            for g in range(qpk):
                val = -jnp.sum(jnp.where(rowg == g, sink_term, 0.0))
                dsink = jnp.where(row0 & (lane == h * qpk + g), val, dsink)
        dq = jnp.concatenate(dqs, axis=1)
        dq_ref[...] = _rope(dq, cc, -sc).astype(dq_ref.dtype)
        dk = jnp.concatenate(dks, axis=1)
        dv = jnp.concatenate(dvs, axis=1)
        dkp_ref[...] = _rope(dk[:BLOCK], cp, -sp)
        dkc_ref[...] = _rope(dk[BLOCK:], cc, -sc)
        dvp_ref[...] = dv[:BLOCK]
        dvc_ref[...] = dv[BLOCK:]

        @pl.when(n == 0)
        def _():
            ds_ref[...] = dsink

        @pl.when(n > 0)
        def _():
            ds_ref[...] += dsink

    kv_spec = pl.BlockSpec((BLOCK, kvw), lambda n: (n, 0))
    return pl.pallas_call(
        body, name="attn_bwd", grid=(t // BLOCK,),
        in_specs=_attn_specs(aw, kvw) + [pl.BlockSpec((BLOCK, aw), lambda n: (n, 0))],
        out_specs=[pl.BlockSpec((BLOCK, aw), lambda n: (n, 0)), kv_spec, kv_spec, kv_spec, kv_spec,
                   pl.BlockSpec((8, LANES), lambda n: (0, 0))],
        out_shape=[_sds((t, aw), BF16)] + [_sds((t, kvw), F32)] * 4 + [_sds((8, LANES), F32)],
        compiler_params=_cparams("arbitrary"),
    )(proj, proj, proj, proj, proj, cos, sin, cos, sin, sinks, do)


HALO = 16


def _shift_down(v, k, halo):
    rows = lax.broadcasted_iota(jnp.int32, v.shape, 0)
    out = pltpu.roll(v, k, 0)
    for r in range(k):
        out = jnp.where(rows == r, halo[HALO - k + r:HALO - k + r + 1, :], out)
    return out


def _shift_up(v, k, halo):
    tm = v.shape[0]
    rows = lax.broadcasted_iota(jnp.int32, v.shape, 0)
    out = pltpu.roll(v, tm - k, 0)
    for r in range(k):
        out = jnp.where(rows == tm - k + r, halo[r:r + 1, :], out)
    return out


def _conv_fwd(proj, conv_w, zoff, cw, cb):
    t = proj.shape[0]
    tm = _pick(t, (512, 256, 128))
    zb, nb = zoff // cb, cw // cb
    hb = tm // HALO

    def body(z_ref, gb_ref, gc_ref, zp_ref, gcp_ref, w_ref, o_ref):
        i = pl.program_id(0)
        cz = gc_ref[...].astype(F32) * z_ref[...].astype(F32)
        czp = gcp_ref[...].astype(F32) * zp_ref[...].astype(F32) * (i > 0).astype(F32)
        w = w_ref[...]
        y = w[0:1] * _shift_down(cz, 2, czp) + w[1:2] * _shift_down(cz, 1, czp) + w[2:3] * cz
        o_ref[...] = (gb_ref[...].astype(F32) * y).astype(o_ref.dtype)

    def col(k):
        return pl.BlockSpec((tm, cb), lambda i, j: (i, zb + k * nb + j))

    def halo(k):
        return pl.BlockSpec((HALO, cb), lambda i, j: (jnp.maximum(i * hb - 1, 0), zb + k * nb + j))

    return pl.pallas_call(
        body, name="conv_fwd", grid=(t // tm, nb),
        in_specs=[col(0), col(1), col(2), halo(0), halo(2), pl.BlockSpec((3, cb), lambda i, j: (0, j))],
        out_specs=pl.BlockSpec((tm, cb), lambda i, j: (i, j)),
        out_shape=_sds((t, cw), BF16),
        compiler_params=_cparams("parallel", "parallel"),
    )(proj, proj, proj, proj, proj, conv_w)


def _conv_bwd(proj, conv_w, dco, zoff, cw, cb):
    t = proj.shape[0]
    tm = _pick(t, (512, 256, 128))
    zb, nb = zoff // cb, cw // cb
    hb = tm // HALO
    nt = t // tm

    def body(z_ref, gb_ref, gc_ref, zp_ref, gcp_ref, gbn_ref, w_ref, d_ref, dn_ref, dz_ref, dgb_ref, dgc_ref, dw_ref):
        i = pl.program_id(1)
        z, gb, gc = z_ref[...].astype(F32), gb_ref[...].astype(F32), gc_ref[...].astype(F32)
        d = d_ref[...].astype(F32)
        cz = gc * z
        czp = gcp_ref[...].astype(F32) * zp_ref[...].astype(F32) * (i > 0).astype(F32)
        w = w_ref[...]
        cz1 = _shift_down(cz, 1, czp)
        cz2 = _shift_down(cz, 2, czp)
        y = w[0:1] * cz2 + w[1:2] * cz1 + w[2:3] * cz
        dgb_ref[...] = (d * y).astype(dgb_ref.dtype)
        dy = d * gb
        dyn = dn_ref[...].astype(F32) * gbn_ref[...].astype(F32) * (i < nt - 1).astype(F32)
        dcz = w[2:3] * dy + w[1:2] * _shift_up(dy, 1, dyn) + w[0:1] * _shift_up(dy, 2, dyn)
        dgc_ref[...] = (dcz * z).astype(dgc_ref.dtype)
        dz_ref[...] = (dcz * gc).astype(dz_ref.dtype)
        rows = lax.broadcasted_iota(jnp.int32, (8, cb), 0)
        dw = jnp.zeros((8, cb), F32)
        for r, tap in enumerate((cz2, cz1, cz)):
            dw = jnp.where(rows == r, jnp.sum(dy * tap, axis=0, keepdims=True), dw)

        @pl.when(i == 0)
        def _():
            dw_ref[...] = dw

        @pl.when(i > 0)
        def _():
            dw_ref[...] += dw

    def col(k):
        return pl.BlockSpec((tm, cb), lambda j, i: (i, zb + k * nb + j))

    def halo_prev(k):
        return pl.BlockSpec((HALO, cb), lambda j, i: (jnp.maximum(i * hb - 1, 0), zb + k * nb + j))

    own = pl.BlockSpec((tm, cb), lambda j, i: (i, j))
    nxt = lambda i: jnp.minimum((i + 1) * hb, t // HALO - 1)
    return pl.pallas_call(
        body, name="conv_bwd", grid=(nb, nt),
        in_specs=[col(0), col(1), col(2), halo_prev(0), halo_prev(2),
                  pl.BlockSpec((HALO, cb), lambda j, i: (nxt(i), zb + nb + j)),
                  pl.BlockSpec((3, cb), lambda j, i: (0, j)), own,
                  pl.BlockSpec((HALO, cb), lambda j, i: (nxt(i), j))],
        out_specs=[own, own, own, pl.BlockSpec((8, cb), lambda j, i: (0, j))],
        out_shape=[_sds((t, cw), BF16)] * 3 + [_sds((8, cw), F32)],
        compiler_params=_cparams("parallel", "arbitrary"),
    )(proj, proj, proj, proj, proj, proj, conv_w, dco, dco)


def _merge_fwd(proj, ya, yc, goff, cb):
    t, d = ya.shape
    tm = _pick(t, (512, 256, 128))
    gb_, nb = goff // cb, d // cb

    def body(ga_ref, gc_ref, ya_ref, yc_ref, o_ref):
        f = lambda r: r[...].astype(F32)
        o_ref[...] = (_sigmoid(f(ga_ref)) * f(ya_ref) + _sigmoid(f(gc_ref)) * f(yc_ref)).astype(o_ref.dtype)

    own = pl.BlockSpec((tm, cb), lambda i, j: (i, j))
    return pl.pallas_call(
        body, name="merge_fwd", grid=(t // tm, nb),
        in_specs=[pl.BlockSpec((tm, cb), lambda i, j: (i, gb_ + j)),
                  pl.BlockSpec((tm, cb), lambda i, j: (i, gb_ + nb + j)), own, own],
        out_specs=own, out_shape=_sds((t, d), BF16),
        compiler_params=_cparams("parallel", "parallel"),
    )(proj, proj, ya, yc)


def _merge_bwd(proj, ya, yc, dm, goff, cb):
    t, d = ya.shape
    tm = _pick(t, (512, 256, 128))
    gb_, nb = goff // cb, d // cb

    def body(ga_ref, gc_ref, ya_ref, yc_ref, dm_ref, dya_ref, dyc_ref, dga_ref, dgc_ref):
        dmv = dm_ref[...].astype(F32)
        sa = _sigmoid(ga_ref[...].astype(F32))
        sc = _sigmoid(gc_ref[...].astype(F32))
        dya_ref[...] = (dmv * sa).astype(BF16)
        dyc_ref[...] = (dmv * sc).astype(BF16)
        dga_ref[...] = (dmv * ya_ref[...].astype(F32) * sa * (1.0 - sa)).astype(BF16)
        dgc_ref[...] = (dmv * yc_ref[...].astype(F32) * sc * (1.0 - sc)).astype(BF16)

    own = pl.BlockSpec((tm, cb), lambda i, j: (i, j))
    return pl.pallas_call(
        body, name="merge_bwd", grid=(t // tm, nb),
        in_specs=[pl.BlockSpec((tm, cb), lambda i, j: (i, gb_ + j)),
                  pl.BlockSpec((tm, cb), lambda i, j: (i, gb_ + nb + j)), own, own, own],
        out_specs=[own] * 4, out_shape=[_sds((t, d), BF16)] * 4,
        compiler_params=_cparams("parallel", "parallel"),
    )(proj, proj, ya, yc, dm)


def _assemble_dproj(dq, dkc, dkp, dvc, dvp, dz, dgb, dgc, dga, dgg):
    t, aw = dq.shape
    kvw, cw, d = dkc.shape[1], dz.shape[1], dga.shape[1]
    nblk = t // BLOCK
    width = aw + 2 * kvw + 3 * cw + 2 * d

    def body(dq_ref, dkc_ref, dkp_ref, dvc_ref, dvp_ref, dz_ref, dgb_ref, dgc_ref, dga_ref, dgg_ref, o_ref):
        keep = (pl.program_id(0) < nblk - 1).astype(F32)
        dk = dkc_ref[...] + dkp_ref[...] * keep
        dv = dvc_ref[...] + dvp_ref[...] * keep
        o_ref[...] = jnp.concatenate(
            [dq_ref[...], dk.astype(BF16), dv.astype(BF16), dz_ref[...], dgb_ref[...], dgc_ref[...],
             dga_ref[...], dgg_ref[...]], axis=1)

    def cur(w):
        return pl.BlockSpec((BLOCK, w), lambda n: (n, 0))

    def nxt(w):
        return pl.BlockSpec((BLOCK, w), lambda n: (jnp.minimum(n + 1, nblk - 1), 0))

    return pl.pallas_call(
        body, name="assemble_dproj", grid=(nblk,),
        in_specs=[cur(aw), cur(kvw), nxt(kvw), cur(kvw), nxt(kvw), cur(cw), cur(cw), cur(cw), cur(d), cur(d)],
        out_specs=cur(width), out_shape=_sds((t, width), BF16),
        compiler_params=_cparams("parallel"),
    )(dq, dkc, dkp, dvc, dvp, dz, dgb, dgc, dga, dgg)


def _xattn_probs(qh, kh):
    s = lax.dot_general(qh, kh, (((1,), (1,)), ((), ())), preferred_element_type=F32) * (X_HEAD_DIM ** -0.5)
    e = jnp.exp(s - jnp.max(s, axis=-1, keepdims=True))
    return e * (1.0 / jnp.sum(e, axis=-1, keepdims=True))


def _xattn_fwd(xq, kv):
    t, xw = xq.shape
    mt = kv.shape[0]
    tm = _pick(t, (512, 256, 128))

    def body(q_ref, kv_ref, o_ref):
        outs = []
        for hd in range(xw // X_HEAD_DIM):
            hs = slice(hd * X_HEAD_DIM, (hd + 1) * X_HEAD_DIM)
            vs = slice(xw + hd * X_HEAD_DIM, xw + (hd + 1) * X_HEAD_DIM)
            p = _xattn_probs(q_ref[:, hs], kv_ref[:, hs])
            outs.append(jnp.dot(p.astype(BF16), kv_ref[:, vs], preferred_element_type=F32))
        o_ref[...] = jnp.concatenate(outs, axis=1).astype(o_ref.dtype)

    return pl.pallas_call(
        body, name="xattn_fwd", grid=(t // tm,),
        in_specs=[pl.BlockSpec((tm, xw), lambda i: (i, 0)), pl.BlockSpec((mt, 2 * xw), lambda i: (0, 0))],
        out_specs=pl.BlockSpec((tm, xw), lambda i: (i, 0)), out_shape=_sds((t, xw), BF16),
        compiler_params=_cparams("parallel"),
    )(xq, kv)


def _xattn_bwd(xq, kv, do):
    t, xw = xq.shape
    mt = kv.shape[0]
    tm = _pick(t, (512, 256, 128))
    scale = X_HEAD_DIM ** -0.5

    def body(q_ref, kv_ref, do_ref, dq_ref, dkv_ref):
        dqs, dks, dvs = [], [], []
        for hd in range(xw // X_HEAD_DIM):
            hs = slice(hd * X_HEAD_DIM, (hd + 1) * X_HEAD_DIM)
            vs = slice(xw + hd * X_HEAD_DIM, xw + (hd + 1) * X_HEAD_DIM)
            qh, kh, vh, doh = q_ref[:, hs], kv_ref[:, hs], kv_ref[:, vs], do_ref[:, hs]
            p = _xattn_probs(qh, kh)
            pb = p.astype(BF16)
            o = jnp.dot(pb, vh, preferred_element_type=F32)
            delta = jnp.sum(doh.astype(F32) * o, axis=-1, keepdims=True)
            dvs.append(lax.dot_general(pb, doh, (((0,), (0,)), ((), ())), preferred_element_type=F32))
            dp = lax.dot_general(doh, vh, (((1,), (1,)), ((), ())), preferred_element_type=F32)
            dsc = (p * (dp - delta)).astype(BF16)
            dqs.append(jnp.dot(dsc, kh, preferred_element_type=F32) * scale)
            dks.append(lax.dot_general(dsc, qh, (((0,), (0,)), ((), ())), preferred_element_type=F32) * scale)
        dq_ref[...] = jnp.concatenate(dqs, axis=1).astype(dq_ref.dtype)
        dkv = jnp.concatenate(dks + dvs, axis=1)

        @pl.when(pl.program_id(0) == 0)
        def _():
            dkv_ref[...] = dkv

        @pl.when(pl.program_id(0) > 0)
        def _():
            dkv_ref[...] += dkv

    row = pl.BlockSpec((tm, xw), lambda i: (i, 0))
    whole = pl.BlockSpec((mt, 2 * xw), lambda i: (0, 0))
    return pl.pallas_call(
        body, name="xattn_bwd", grid=(t // tm,),
        in_specs=[row, whole, row], out_specs=[row, whole],
        out_shape=[_sds((t, xw), BF16), _sds((mt, 2 * xw), F32)],
        compiler_params=_cparams("arbitrary"),
    )(xq, kv, do)


def _swiglu_fwd(hid):
    t, f2 = hid.shape
    f = f2 // 2
    tm = _pick(t, (256, 128))

    def body(h_ref, o_ref):
        a = h_ref[:, :f].astype(F32)
        b = h_ref[:, f:].astype(F32)
        o_ref[...] = (a * _sigmoid(a) * b).astype(o_ref.dtype)

    return pl.pallas_call(
        body, name="swiglu_fwd", grid=(t // tm,),
        in_specs=[pl.BlockSpec((tm, f2), lambda i: (i, 0))],
        out_specs=pl.BlockSpec((tm, f), lambda i: (i, 0)), out_shape=_sds((t, f), BF16),
        compiler_params=_cparams("parallel"),
    )(hid)


def _swiglu_bwd(hid, dact):
    t, f2 = hid.shape
    f = f2 // 2
    tm = _pick(t, (128,))

    def body(h_ref, d_ref, o_ref):
        a = h_ref[:, :f].astype(F32)
        b = h_ref[:, f:].astype(F32)
        d = d_ref[...].astype(F32)
        sg = _sigmoid(a)
        o_ref[:, :f] = (d * b * sg * (1.0 + a * (1.0 - sg))).astype(o_ref.dtype)
        o_ref[:, f:] = (d * a * sg).astype(o_ref.dtype)

    return pl.pallas_call(
        body, name="swiglu_bwd", grid=(t // tm,),
        in_specs=[pl.BlockSpec((tm, f2), lambda i: (i, 0)), pl.BlockSpec((tm, f), lambda i: (i, 0))],
        out_specs=pl.BlockSpec((tm, f2), lambda i: (i, 0)), out_shape=_sds((t, f2), BF16),
        compiler_params=_cparams("parallel"),
    )(hid, dact)


def _adamw(w, g, m, v, name):
    r, c = w.shape
    tr = _pick(r, (256, 128, 64, 32, 16, 8)) if r * c > 65536 else r

    def body(w_ref, g_ref, m_ref, v_ref, d_ref, nm_ref, nv_ref):
        gv = g_ref[...]
        m2 = ADAM_B1 * m_ref[...] + (1.0 - ADAM_B1) * gv
        v2 = ADAM_B2 * v_ref[...] + (1.0 - ADAM_B2) * (gv * gv)
        m_hat = m2 / (1.0 - ADAM_B1 ** ADAM_STEP)
        v_hat = v2 / (1.0 - ADAM_B2 ** ADAM_STEP)
        d_ref[...] = -ADAM_LR * (m_hat / (jnp.sqrt(v_hat) + ADAM_EPS) + ADAM_WD * w_ref[...])
        nm_ref[...] = m2
        nv_ref[...] = v2

    blk = pl.BlockSpec((tr, c), lambda i: (i, 0))
    return pl.pallas_call(
        body, name=name, grid=(r // tr,),
        in_specs=[blk] * 4, out_specs=[blk] * 3, out_shape=[_sds((r, c), F32)] * 3,
        compiler_params=_cparams("parallel"),
    )(w, g, m, v)


class _Weights:
    def __init__(self, full):
        self.full = full
        self.grads = {}

    def get(self, name):
        return self.full[name]

    def mark(self, tag, value):
        return value

    def grad(self, name, g):
        self.grads[name] = g

    def dep(self):
        return None


def _local_step(x, mem, tgt, g_mix, sinks, g_xattn, g_mem, g_ffn, g_final, dims, wts):
    t, d = x.shape
    aw, cw, kvw = dims
    cb = 2 * kvw
    zoff = aw + 2 * kvw
    goff = zoff + 3 * cw
    cos, sin = _rope_tables(t)
    mark, get = wts.mark, wts.get

    def mm(a, b, **kw):
        return _matmul(a, b, dep=wts.dep(), **kw)

    u1 = mark("u1", _rms_fwd(x, g_mix, "rms_mix"))
    proj = mark("proj", mm(u1, get("w_in"), mode="nn", out_dtype=F32, name="mm_proj"))
    attn_o = mark("attn_o", _attn_fwd(proj, cos, sin, sinks, aw, kvw))
    conv_o = mark("conv_o", _conv_fwd(proj, get("conv_w"), zoff, cw, cb))
    ya = mark("ya", mm(attn_o, get("w_attn_proj"), mode="nn", out_dtype=BF16, name="mm_yattn"))
    yc = mark("yc", mm(conv_o, get("w_conv_proj"), mode="nn", out_dtype=BF16, name="mm_yconv"))
    merged = mark("merged", _merge_fwd(proj, ya, yc, goff, cb))
    h1 = mark("h1", mm(merged, get("w_mix_out"), mode="nn", out_dtype=F32, name="mm_mix", res=x))
    u2 = mark("u2", _rms_fwd(h1, g_xattn, "rms_xattn"))
    mem_n = _rms_fwd(mem, g_mem, "rms_mem")
    xq = mark("xq", mm(u2, get("w_xq"), mode="nn", out_dtype=BF16, name="mm_xq"))
    kv = mm(mem_n, get("w_xkv"), mode="nn", out_dtype=BF16, name="mm_xkv")
    xo = mark("xo", _xattn_fwd(xq, kv))
    h2 = mark("h2", mm(xo, get("w_xo"), mode="nn", out_dtype=F32, name="mm_xo", res=h1))
    u3 = mark("u3", _rms_fwd(h2, g_ffn, "rms_ffn"))
    hid = mark("hid", mm(u3, get("w_ffn_in"), mode="nn", out_dtype=BF16, name="mm_ffn_in"))
    act = mark("act", _swiglu_fwd(hid))
    h3 = mm(act, get("w_ffn_out"), mode="nn", out_dtype=F32, name="mm_ffn_out", res=h2)
    dh3, dh3b, dg_final, loss = _loss_head(h3, g_final, tgt)
    mark("dh3", dh3b)

    wts.grad("w_ffn_out", mm(act, dh3b, mode="tn", out_dtype=BF16, name="mm_dw_ffn_out"))
    dact = mark("dact", mm(dh3b, get("w_ffn_out"), mode="nt", out_dtype=BF16, name="mm_dact"))
    dhid = mark("dhid", _swiglu_bwd(hid, dact))
    wts.grad("w_ffn_in", mm(u3, dhid, mode="tn", out_dtype=BF16, name="mm_dw_ffn_in"))
    du3 = mark("du3", mm(dhid, get("w_ffn_in"), mode="nt", out_dtype=F32, name="mm_du3"))
    dh2, dh2b, dg_ffn = _rms_bwd(h2, g_ffn, du3, dh3, "rms_bwd_ffn")
    mark("dh2", dh2b)
    wts.grad("w_xo", mm(xo, dh2b, mode="tn", out_dtype=BF16, name="mm_dw_xo"))
    dxo = mm(dh2b, get("w_xo"), mode="nt", out_dtype=BF16, name="mm_dxo")
    dxq, dkv = _xattn_bwd(xq, kv, dxo)
    dkvb = dkv.astype(BF16)
    wts.grad("w_xq", mm(u2, dxq, mode="tn", out_dtype=BF16, name="mm_dw_xq"))
    du2 = mm(dxq, get("w_xq"), mode="nt", out_dtype=F32, name="mm_du2")
    wts.grad("w_xkv", mm(mem_n, dkvb, mode="tn", out_dtype=BF16, name="mm_dw_xkv"))
    dmem_n = mm(dkvb, get("w_xkv"), mode="nt", out_dtype=F32, name="mm_dmem")
    _, _, dg_mem = _rms_bwd(mem, g_mem, dmem_n, None, "rms_bwd_mem")
    dh1, dh1b, dg_xattn = _rms_bwd(h1, g_xattn, du2, dh2, "rms_bwd_xattn")
    mark("dh1", dh1b)
    wts.grad("w_mix_out", mm(merged, dh1b, mode="tn", out_dtype=BF16, name="mm_dw_mix"))
    dmerged = mm(dh1b, get("w_mix_out"), mode="nt", out_dtype=BF16, name="mm_dmerged")
    dya, dyc, dga, dgg = _merge_bwd(proj, ya, yc, dmerged, goff, cb)
    mark("dya", dya)
    wts.grad("w_attn_proj", mm(attn_o, dya, mode="tn", out_dtype=BF16, name="mm_dw_attn_proj"))
    dattn_o = mm(dya, get("w_attn_proj"), mode="nt", out_dtype=BF16, name="mm_dattn")
    wts.grad("w_conv_proj", mm(conv_o, dyc, mode="tn", out_dtype=BF16, name="mm_dw_conv_proj"))
    dconv_o = mark("dconv_o", mm(dyc, get("w_conv_proj"), mode="nt", out_dtype=BF16, name="mm_dconv"))
    dz, dgb, dgc, dconv_w = _conv_bwd(proj, get("conv_w"), dconv_o, zoff, cw, cb)
    mark("dz", dz)
    dq, dkc, dkp, dvc, dvp, dsinks = _attn_bwd(proj, cos, sin, sinks, dattn_o, aw, kvw)
    mark("dq", dq)
    dproj = mark("dproj", _assemble_dproj(dq, dkc, dkp, dvc, dvp, dz, dgb, dgc, dga, dgg))
    wts.grad("w_in", mm(u1, dproj, mode="tn", out_dtype=BF16, name="mm_dw_in"))
    du1 = mark("du1", mm(dproj, get("w_in"), mode="nt", out_dtype=F32, name="mm_du1"))
    grad_x, _, dg_mix = _rms_bwd(x, g_mix, du1, dh1, "rms_bwd_mix")
    mark("grad_x", grad_x)

    small = dict(g_mix=dg_mix, g_xattn=dg_xattn, g_mem=dg_mem, g_ffn=dg_ffn, g_final=dg_final,
                 conv_w=dconv_w[:3], attn_sinks=dsinks[0:1, :sinks.shape[1]], loss=loss[0:1, 0:1])
    return grad_x, small


BIG = (("w_in", 1), ("w_attn_proj", 1), ("w_conv_proj", 1), ("w_mix_out", 0), ("w_xq", 0), ("w_xkv", 0),
       ("w_xo", 1), ("w_ffn_in", 1), ("w_ffn_out", 0))


def _place():
    x, y, c = lax.axis_index("x"), lax.axis_index("y"), lax.axis_index("c")
    chips = [(1 - x, y), (x, 1 - y), (1 - x, 1 - y)]
    return x, y, c, chips


def _window(ref, ax, shard_shape, s, h):
    sr, sc = shard_shape
    hr = sr // 2
    if ax == 1:
        return ref.at[pl.ds(pl.multiple_of(h * hr, 16), hr), pl.ds(pl.multiple_of(s * sc, LANES), sc)]
    return ref.at[pl.ds(pl.multiple_of(s * sr + h * hr, 16), hr), :]


def _half(ref, h):
    hr = ref.shape[0] // 2
    return ref.at[pl.ds(pl.multiple_of(h * hr, 16), hr), :]


def _remote(src, dst, send_sem, recv_sem, dev):
    return pltpu.make_async_remote_copy(src_ref=src, dst_ref=dst, send_sem=send_sem, recv_sem=recv_sem,
                                        device_id=dev, device_id_type=MESH)


def _cast_to_full(shard, ax, me, name, dtype=BF16):
    sr, sc = shard.shape
    tr = _pick(sr, (256, 352, 128, 64, 32, 16))
    nr = sr // tr
    full = (sr * N_CHIPS, sc) if ax == 0 else (sr, sc * N_CHIPS)

    def body(me_ref, s_ref, o_ref):
        o_ref[...] = s_ref[...].astype(o_ref.dtype)

    if ax == 1:
        out_spec = pl.BlockSpec((tr, sc), lambda r, me_ref: (r, me_ref[0]))
    else:
        out_spec = pl.BlockSpec((tr, sc), lambda r, me_ref: (me_ref[0] * nr + r, 0))
    return pl.pallas_call(
        body, name=name,
        grid_spec=pltpu.PrefetchScalarGridSpec(
            num_scalar_prefetch=1, grid=(nr,), in_specs=[pl.BlockSpec((tr, sc), lambda r, me_ref: (r, 0))],
            out_specs=out_spec),
        out_shape=_sds(full, dtype),
        compiler_params=_cparams("parallel"),
    )(me, shard)


HBM = pl.BlockSpec(memory_space=pltpu.HBM)
SEM = pl.BlockSpec(memory_space=pltpu.SEMAPHORE)
EFFECT = pltpu.SideEffectType.DATAFLOW_SIDE_EFFECTING


def _in_hbm(a):
    return pltpu.with_memory_space_constraint(a, pltpu.HBM)


def _gather_window(ref, ax, shard_shape, s, h):
    if h is not None:
        return _window(ref, ax, shard_shape, s, h)
    sr, sc = shard_shape
    if ax == 1:
        return ref.at[:, pl.ds(pl.multiple_of(s * sc, LANES), sc)]
    return ref.at[pl.ds(pl.multiple_of(s * sr, 8), sr), :]


def _ag_start(fulls, axes, shard_shapes, whole):
    n = len(fulls)

    def body(*refs):
        src = refs[:n]
        send_sems, recv_sems = refs[n], refs[n + 1]
        token = refs[2 * n + 2]
        x, y, c, chips = _place()
        me = 2 * x + y
        for i in range(n):
            h = None if whole[i] else c
            for j, chip in enumerate(chips):
                blk = _gather_window(src[i], axes[i], shard_shapes[i], me, h)
                _remote(blk, blk, send_sems.at[3 * i + j], recv_sems.at[3 * i + j], (*chip, c)).start()
        token[...] = jnp.zeros_like(token)

    res = pl.pallas_call(
        body, name="ag_start_" + str(n),
        out_shape=(pltpu.SemaphoreType.DMA((3 * n,)), pltpu.SemaphoreType.DMA((3 * n,)),
                   *[pltpu.HBM(f.shape, f.dtype) for f in fulls], _sds((8, LANES), F32)),
        in_specs=[HBM] * n, out_specs=(SEM, SEM, *[HBM] * n, pl.BlockSpec(memory_space=pltpu.VMEM)),
        input_output_aliases={i: 2 + i for i in range(n)},
        compiler_params=pltpu.CompilerParams(has_side_effects=EFFECT),
    )(*[_in_hbm(f) for f in fulls])
    return res[0], res[1], list(res[2:2 + n]), res[2 + n]


def _ag_mid(bufs, slots, axes, shard_shapes, whole, send_sems, recv_sems, after, name):
    ng = len(bufs)

    def body(*refs):
        src = refs[:ng]
        s_in, r_in = refs[ng], refs[ng + 1]
        fsend, frecv = refs[ng + 3], refs[ng + 4]
        x, y, c, chips = _place()
        me = 2 * x + y
        sib = (x, y, 1 - c)
        for k, i in enumerate(slots):
            h = None if whole[k] else c
            for j, chip in enumerate(chips):
                cj = 2 * chip[0] + chip[1]
                mine = _gather_window(src[k], axes[k], shard_shapes[k], me, h)
                theirs = _gather_window(src[k], axes[k], shard_shapes[k], cj, h)
                _remote(theirs, theirs, s_in.at[3 * i + j], r_in.at[3 * i + j], (*chip, c)).wait_recv()
                _remote(mine, mine, s_in.at[3 * i + j], r_in.at[3 * i + j], (*chip, c)).wait_send()
                if not whole[k]:
                    _remote(theirs, theirs, fsend.at[3 * k + j], frecv.at[3 * k + j], sib).start()
        token = refs[2 * ng + 5]
        token[...] = jnp.zeros_like(token)

    res = pl.pallas_call(
        body, name=name,
        out_shape=(pltpu.SemaphoreType.DMA((3 * ng,)), pltpu.SemaphoreType.DMA((3 * ng,)),
                   *[pltpu.HBM(b.shape, b.dtype) for b in bufs], _sds((8, LANES), F32)),
        in_specs=[HBM] * ng + [SEM, SEM, ANY],
        out_specs=(SEM, SEM, *[HBM] * ng, pl.BlockSpec(memory_space=pltpu.VMEM)),
        input_output_aliases={k: 2 + k for k in range(ng)},
        compiler_params=pltpu.CompilerParams(has_side_effects=EFFECT),
    )(*bufs, send_sems, recv_sems, after)
    return res[0], res[1], list(res[2:2 + ng]), res[2 + ng]


def _ag_wait(bufs, axes, shard_shapes, whole, fsend, frecv, after, name):
    ng = len(bufs)

    def body(*refs):
        src = refs[:ng]
        s_in, r_in = refs[ng], refs[ng + 1]
        x, y, c, chips = _place()
        sib = (x, y, 1 - c)
        for k in range(ng):
            if whole[k]:
                continue
            for j, chip in enumerate(chips):
                cj = 2 * chip[0] + chip[1]
                sent = _gather_window(src[k], axes[k], shard_shapes[k], cj, c)
                landed = _gather_window(src[k], axes[k], shard_shapes[k], cj, 1 - c)
                _remote(landed, landed, s_in.at[3 * k + j], r_in.at[3 * k + j], sib).wait_recv()
                _remote(sent, sent, s_in.at[3 * k + j], r_in.at[3 * k + j], sib).wait_send()

    res = pl.pallas_call(
        body, name=name,
        out_shape=tuple(pltpu.HBM(b.shape, b.dtype) for b in bufs),
        in_specs=[HBM] * ng + [SEM, SEM, ANY], out_specs=tuple([HBM] * ng),
        input_output_aliases={k: k for k in range(ng)},
        compiler_params=pltpu.CompilerParams(has_side_effects=EFFECT),
    )(*bufs, fsend, frecv, after)
    return list(res)


class _Schedule:
    GROUPS = ((("w_in", "conv_w"), "start", "u1"),
              (("w_attn_proj", "w_conv_proj", "w_mix_out"), "proj", "conv_o"),
              (("w_xq", "w_xkv", "w_xo"), "ya", "h1"),
              (("w_ffn_in",), "h1", "h2"),
              (("w_ffn_out",), "u3", "hid"))
    STARTS = ((0,), (1, 2, 3, 4))
    REDUCE = ((("w_ffn_out",), "dact", "grad:w_ffn_in", "du3"),
              (("w_ffn_in",), "grad:w_ffn_in", "dya", "grad_x"),
              (("w_xo", "w_xq", "w_xkv"), "dh1", "dconv_o", "grad_x"),
              (("w_mix_out", "w_attn_proj", "w_conv_proj"), "dconv_o", "dq", "grad_x"),
              (("w_in",), "grad:w_in", "end", "end2"))

    def __init__(self, seed, axes, shard_shapes, place, on_ready):
        self.ax, self.shape, self.place, self.on_ready = axes, shard_shapes, place, on_ready
        self.stage, self.buf, self.slot, self.passes = {}, {}, {}, {}
        self.ready = set()
        self.grads = {}
        token = None
        for groups in self.STARTS:
            order = [nm for g in groups for nm in self.GROUPS[g][0]]
            send, recv, bufs, token = _ag_start([seed(nm, token) for nm in order], *self._meta(order))
            self.buf.update(zip(order, bufs))
            self.slot.update({nm: (send, recv, k) for k, nm in enumerate(order)})
        self.token = self.latest = token
        self.mark("start", token)

    def _meta(self, names):
        return ([self.ax[nm] for nm in names], [self.shape[nm] for nm in names], [nm == "conv_w" for nm in names])

    def mark(self, tag, value):
        for g, (names, mid, wait) in enumerate(self.GROUPS):
            if tag == mid:
                send, recv, _ = self.slot[names[0]]
                fs, fr, bufs, self.latest = _ag_mid([self.buf[nm] for nm in names], [self.slot[nm][2] for nm in names],
                                                    *self._meta(names), send, recv, value, "ag_mid_%d" % g)
                self.buf.update(zip(names, bufs))
                self.passes[g] = (fs, fr)
            if tag == wait:
                fs, fr = self.passes[g]
                bufs = _ag_wait([self.buf[nm] for nm in names], *self._meta(names), fs, fr, value, "ag_wait_%d" % g)
                self.buf.update(zip(names, bufs))
                self.ready.update(names)
        for g, (names, send, total, finish) in enumerate(self.REDUCE):
            st = self.stage.get(g)
            if st is None:
                continue
            ng = len(names)
            if tag == send and st["at"] == "pair":
                arrs = _exchange_wait("rs_pair_wait_%d" % g, st["arrs"], *st["sems"], st["plan"], value)
                parts = [_pair_add(arrs[k], arrs[ng + k], self.ax[nm], self.shape[nm], self.place, "pair_add_" + nm)
                         for k, nm in enumerate(names)]
                plan, nsem = _plan_chip(ng)
                ss, rs, arrs, self.latest = _exchange_start(
                    "rs_chip_start_%d" % g, parts + [lax.empty(p.shape, p.dtype) for p in parts], nsem, plan)
                self.stage[g] = dict(at="chip", arrs=arrs, sems=(ss, rs), plan=plan)
            elif tag == total and st["at"] == "chip":
                arrs = _exchange_wait("rs_chip_wait_%d" % g, st["arrs"], *st["sems"], st["plan"], value)
                halves = [_chip_add(arrs[k], arrs[ng + k], self.place, "chip_add_" + nm) for k, nm in enumerate(names)]
                plan, nsem = _plan_gather(ng)
                ss, rs, arrs, self.latest = _exchange_start("rs_gather_start_%d" % g, halves, nsem, plan)
                self.stage[g] = dict(at="gather", arrs=arrs, sems=(ss, rs), plan=plan)
            elif tag == finish and st["at"] == "gather":
                arrs = _exchange_wait("rs_gather_wait_%d" % g, st["arrs"], *st["sems"], st["plan"], value)
                self.stage[g] = dict(at="done")
                for nm, shard in zip(names, arrs):
                    self.on_ready(nm, shard)
        return value

    def get(self, name):
        assert name in self.ready, name
        return self.buf[name]

    def grad(self, name, g):
        self.grads[name] = g
        for gi, (names, _, _, _) in enumerate(self.REDUCE):
            if name == names[-1]:
                gs = [self.grads[nm] for nm in names]
                plan, nsem = _plan_pair(len(names), [self.ax[nm] for nm in names], [self.shape[nm] for nm in names])
                ss, rs, arrs, self.latest = _exchange_start(
                    "rs_pair_start_%d" % gi, gs + [lax.empty(a.shape, a.dtype) for a in gs], nsem, plan)
                self.stage[gi] = dict(at="pair", arrs=arrs, sems=(ss, rs), plan=plan)
                g = self.latest
        self.mark("grad:" + name, g)

    def dep(self):
        return self.latest


def _exchange_start(name, arrays, nsem, plan):
    n = len(arrays)

    def body(*refs):
        send_sems, recv_sems, token = refs[n], refs[n + 1], refs[2 * n + 2]
        sends, _ = plan(refs[:n])
        for k, (src, dst, dev) in enumerate(sends):
            _remote(src, dst, send_sems.at[k], recv_sems.at[k], dev).start()
        token[...] = jnp.zeros_like(token)

    res = pl.pallas_call(
        body, name=name,
        out_shape=(pltpu.SemaphoreType.DMA((nsem,)), pltpu.SemaphoreType.DMA((nsem,)),
                   *[pltpu.HBM(a.shape, a.dtype) for a in arrays], _sds((8, LANES), F32)),
        in_specs=[HBM] * n, out_specs=(SEM, SEM, *[HBM] * n, pl.BlockSpec(memory_space=pltpu.VMEM)),
        input_output_aliases={i: 2 + i for i in range(n)},
        compiler_params=pltpu.CompilerParams(has_side_effects=EFFECT),
    )(*[_in_hbm(a) for a in arrays])
    return res[0], res[1], list(res[2:2 + n]), res[2 + n]


def _exchange_wait(name, arrays, send_sems, recv_sems, plan, after):
    n = len(arrays)

    def body(*refs):
        s_in, r_in = refs[n], refs[n + 1]
        sends, recvs = plan(refs[:n])
        for k, land in enumerate(recvs):
            _remote(land, land, s_in.at[k], r_in.at[k], sends[k][2]).wait_recv()
        for k, (src, _, dev) in enumerate(sends):
            _remote(src, src, s_in.at[k], r_in.at[k], dev).wait_send()

    res = pl.pallas_call(
        body, name=name,
        out_shape=tuple(pltpu.HBM(a.shape, a.dtype) for a in arrays),
        in_specs=[HBM] * n + [SEM, SEM, ANY], out_specs=tuple([HBM] * n),
        input_output_aliases={i: i for i in range(n)},
        compiler_params=pltpu.CompilerParams(has_side_effects=EFFECT),
    )(*arrays, send_sems, recv_sems, after)
    return list(res)


def _plan_pair(n, axes, shard_shapes):
    def plan(refs):
        g, ra = refs[:n], refs[n:]
        x, y, c, _ = _place()
        sib = (x, y, 1 - c)

        def pieces(ref, i, h):
            if axes[i] == 1:
                return [_half(ref, h)]
            return [_window(ref, 0, shard_shapes[i], s, h) for s in range(N_CHIPS)]

        sends, recvs = [], []
        for i in range(n):
            sends += [(src, dst, sib) for src, dst in zip(pieces(g[i], i, 1 - c), pieces(ra[i], i, 1 - c))]
            recvs += pieces(ra[i], i, c)
        return sends, recvs

    return plan, sum(1 if ax == 1 else N_CHIPS for ax in axes)


def _plan_chip(n):
    def plan(refs):
        p, rc = refs[:n], refs[n:]
        x, y, c, chips = _place()
        me = 2 * x + y
        sends, recvs = [], []
        for i in range(n):
            for chip in chips:
                cj = 2 * chip[0] + chip[1]
                sends.append((p[i].at[cj], rc[i].at[me], (*chip, c)))
                recvs.append(rc[i].at[cj])
        return sends, recvs

    return plan, 3 * n


def _plan_gather(n):
    def plan(refs):
        x, y, c, _ = _place()
        sib = (x, y, 1 - c)
        return ([(_half(r, c), _half(r, c), sib) for r in refs], [_half(r, 1 - c) for r in refs])

    return plan, n


def _pair_add(g, ra, ax, shard_shape, place, name):
    sr, sc = shard_shape
    hr = sr // 2
    wc = sc
    tr = _pick(hr, (256, 352, 128, 64, 32, 16))
    nr = hr // tr

    def body(p_ref, a_ref, b_ref, o_ref):
        o_ref[...] = (a_ref[...].astype(F32) + b_ref[...].astype(F32)).astype(o_ref.dtype)

    if ax == 1:
        src = pl.BlockSpec((tr, wc), lambda s, r, p_ref: (p_ref[0] * nr + r, s))
    else:
        src = pl.BlockSpec((tr, wc), lambda s, r, p_ref: (s * 2 * nr + p_ref[0] * nr + r, 0))
    return pl.pallas_call(
        body, name=name,
        grid_spec=pltpu.PrefetchScalarGridSpec(
            num_scalar_prefetch=1, grid=(N_CHIPS, nr), in_specs=[src, src],
            out_specs=pl.BlockSpec((None, tr, wc), lambda s, r, p_ref: (s, r, 0))),
        out_shape=_sds((N_CHIPS, hr, wc), BF16),
        compiler_params=_cparams("parallel", "parallel"),
    )(place, g, ra)


def _chip_add(part, rc, place, name):
    _, hr, wc = rc.shape
    tr = _pick(hr, (256, 352, 128, 64, 32, 16))
    nr = hr // tr

    def body(p_ref, own_ref, r1_ref, r2_ref, r3_ref, o_ref):
        acc = own_ref[...].astype(F32)
        for r_ref in (r1_ref, r2_ref, r3_ref):
            acc = acc + r_ref[...].astype(F32)
        o_ref[...] = acc

    def slot(k):
        return pl.BlockSpec((None, tr, wc), lambda r, p_ref: ((p_ref[1] + k) % N_CHIPS, r, 0))

    return pl.pallas_call(
        body, name=name,
        grid_spec=pltpu.PrefetchScalarGridSpec(
            num_scalar_prefetch=1, grid=(nr,), in_specs=[slot(0), slot(1), slot(2), slot(3)],
            out_specs=pl.BlockSpec((tr, wc), lambda r, p_ref: (p_ref[0] * nr + r, 0))),
        out_shape=_sds((2 * hr, wc), F32),
        compiler_params=_cparams("parallel"),
    )(place, part, rc, rc, rc)


N_DEV = 8


def _all_reduce_small(buf):
    r, cdim = buf.shape

    def body(x_ref, o_ref, land, send_sems, recv_sems):
        x, y, c, _ = _place()
        me = 4 * x + 2 * y + c
        land[me] = x_ref[...]
        sends = []
        for k in range(1, N_DEV):
            kx, ky, kc = (k >> 2) & 1, (k >> 1) & 1, k & 1
            peer = (1 - x if kx else x, 1 - y if ky else y, 1 - c if kc else c)
            cp = _remote(x_ref, land.at[me], send_sems.at[k - 1], recv_sems.at[k - 1], peer)
            cp.start()
            sends.append(cp)
        for k in range(1, N_DEV):
            kx, ky, kc = (k >> 2) & 1, (k >> 1) & 1, k & 1
            peer = (1 - x if kx else x, 1 - y if ky else y, 1 - c if kc else c)
            pidx = 4 * peer[0] + 2 * peer[1] + peer[2]
            _remote(land.at[pidx], land.at[pidx], send_sems.at[k - 1], recv_sems.at[k - 1], peer).wait_recv()
        for cp in sends:
            cp.wait_send()
        acc = land[0]
        for dev in range(1, N_DEV):
            acc = acc + land[dev]
        o_ref[...] = acc

    vm = pl.BlockSpec(memory_space=pltpu.VMEM)
    return pl.pallas_call(
        body, name="all_reduce_small", in_specs=[vm], out_specs=vm, out_shape=_sds((r, cdim), F32),
        scratch_shapes=[pltpu.VMEM((N_DEV, r, cdim), F32), pltpu.SemaphoreType.DMA((N_DEV - 1,)),
                        pltpu.SemaphoreType.DMA((N_DEV - 1,))],
    )(buf)


SMALL_ROWS = 16


def kernel(x, mem, g_mix, w_in, conv_w, attn_sinks, w_attn_proj, w_conv_proj, w_mix_out, g_xattn, g_mem, w_xq, w_xkv, w_xo, g_ffn, w_ffn_in, w_ffn_out, g_final, loss_target, m_g_mix, m_w_in, m_conv_w, m_attn_sinks, m_w_attn_proj, m_w_conv_proj, m_w_mix_out, m_g_xattn, m_g_mem, m_w_xq, m_w_xkv, m_w_xo, m_g_ffn, m_w_ffn_in, m_w_ffn_out, m_g_final, v_g_mix, v_w_in, v_conv_w, v_attn_sinks, v_w_attn_proj, v_w_conv_proj, v_w_mix_out, v_g_xattn, v_g_mem, v_w_xq, v_w_xkv, v_w_xo, v_g_ffn, v_w_ffn_in, v_w_ffn_out, v_g_final):
    w = dict(g_mix=g_mix, w_in=w_in[0], conv_w=conv_w[0], attn_sinks=attn_sinks, w_attn_proj=w_attn_proj[0],
             w_conv_proj=w_conv_proj[0], w_mix_out=w_mix_out[0], g_xattn=g_xattn, g_mem=g_mem, w_xq=w_xq[0],
             w_xkv=w_xkv[0], w_xo=w_xo[0], g_ffn=g_ffn, w_ffn_in=w_ffn_in[0], w_ffn_out=w_ffn_out[0],
             g_final=g_final[None])
    m = dict(g_mix=m_g_mix, w_in=m_w_in[0], conv_w=m_conv_w[0], attn_sinks=m_attn_sinks,
             w_attn_proj=m_w_attn_proj[0], w_conv_proj=m_w_conv_proj[0], w_mix_out=m_w_mix_out[0],
             g_xattn=m_g_xattn, g_mem=m_g_mem, w_xq=m_w_xq[0], w_xkv=m_w_xkv[0], w_xo=m_w_xo[0], g_ffn=m_g_ffn,
             w_ffn_in=m_w_ffn_in[0], w_ffn_out=m_w_ffn_out[0], g_final=m_g_final[None])
    v = dict(g_mix=v_g_mix, w_in=v_w_in[0], conv_w=v_conv_w[0], attn_sinks=v_attn_sinks,
             w_attn_proj=v_w_attn_proj[0], w_conv_proj=v_w_conv_proj[0], w_mix_out=v_w_mix_out[0],
             g_xattn=v_g_xattn, g_mem=v_g_mem, w_xq=v_w_xq[0], w_xkv=v_w_xkv[0], w_xo=v_w_xo[0], g_ffn=v_g_ffn,
             w_ffn_in=v_w_ffn_in[0], w_ffn_out=v_w_ffn_out[0], g_final=v_g_final[None])
    names = [nm for nm, _ in BIG]
    axes = [ax for _, ax in BIG]
    d = x.shape[2]
    cw = w["conv_w"].shape[1] * N_CHIPS
    chip = (2 * lax.axis_index("x") + lax.axis_index("y")).astype(jnp.int32)
    place = jnp.stack([lax.axis_index("c").astype(jnp.int32), chip])
    shard_shapes = [w[nm].shape for nm in names]

    def seed(nm, token):
        me1 = chip.reshape(1)
        if token is not None:
            me1 = me1 + token[0, 0].astype(jnp.int32)
        if nm == "conv_w":
            return _cast_to_full(w[nm], 1, me1, "place_conv_w", F32)
        return _cast_to_full(w[nm], dict(BIG)[nm], me1, "cast_" + nm)

    upd = {}

    def on_ready(nm, shard):
        grads[nm] = shard
        upd[nm] = _adamw(w[nm], shard, m[nm], v[nm], "adamw_" + nm)

    grads = {}
    wts = _Schedule(seed, dict(zip(names + ["conv_w"], axes + [1])),
                    dict(zip(names + ["conv_w"], shard_shapes + [w["conv_w"].shape])), place, on_ready)
    aw, cw = w["w_attn_proj"].shape[0], w["w_conv_proj"].shape[0]
    kvw = (w["w_in"].shape[1] * N_CHIPS - aw - 3 * cw - 2 * d) // 2
    grad_x, small = _local_step(
        x[0], mem[0], loss_target[0], w["g_mix"] + wts.token[0:1, 0:1], w["attn_sinks"], w["g_xattn"], w["g_mem"],
        w["g_ffn"], w["g_final"], (aw, cw, kvw), wts)

    pw = max(d, cw)

    def row(a):
        return jnp.pad(a, ((0, 0), (0, pw - a.shape[1])))

    gains = ("g_mix", "g_xattn", "g_mem", "g_ffn", "g_final")
    packed = jnp.concatenate(
        [row(small[nm]) for nm in gains] + [row(small["conv_w"]),
         row(jnp.concatenate([small["attn_sinks"], small["loss"]], axis=1)),
         jnp.zeros((SMALL_ROWS - 9, pw), F32)], axis=0)
    total = _all_reduce_small(packed)
    wts.mark("end", total)
    nsink = attn_sinks.shape[1]
    grads.update({nm: total[k:k + 1, :d] for k, nm in enumerate(gains)})
    grads.update(conv_w=lax.dynamic_slice(total, (5, chip * (cw // N_CHIPS)), (3, cw // N_CHIPS)),
                 attn_sinks=total[8:9, :nsink])
    loss = total[8, nsink]
    for nm in gains + ("conv_w", "attn_sinks"):
        upd[nm] = _adamw(w[nm], grads[nm], m[nm], v[nm], "adamw_" + nm)
    wts.mark("end2", upd["g_final"][0])

    order = ["g_mix", "w_in", "conv_w", "attn_sinks", "w_attn_proj", "w_conv_proj", "w_mix_out", "g_xattn", "g_mem",
             "w_xq", "w_xkv", "w_xo", "g_ffn", "w_ffn_in", "w_ffn_out", "g_final"]

    stacked = set(names) | {"conv_w"}

    def shaped(nm, a):
        if nm == "g_final":
            return a[0]
        return a[None] if nm in stacked else a

    outs = [loss, grad_x[None]]
    outs += [shaped(nm, grads[nm]) for nm in order]
    for k in range(3):
        outs += [shaped(nm, upd[nm][k]) for nm in order]
    return tuple(outs)
```

```python
import functools

import jax
import jax.numpy as jnp
from jax import lax
from jax.experimental import pallas as pl
from jax.experimental.pallas import tpu as pltpu

F32 = jnp.float32
BF16 = jnp.bfloat16

VMEM_LIMIT_BYTES = 56 * 1024 * 1024
LANES = 128
HEAD_DIM = 64
BLOCK = 128
X_HEAD_DIM = 128
ROPE_THETA = 10000.0
EPS = 1e-6
NEG = -1e30
ADAM_LR, ADAM_B1, ADAM_B2, ADAM_EPS, ADAM_WD, ADAM_STEP = 0.001, 0.9, 0.999, 1e-08, 0.01, 10
N_CHIPS = 4
MESH = pl.DeviceIdType.MESH
ANY = pl.BlockSpec(memory_space=pl.ANY)


def _pick(dim, prefs):
    for p in prefs:
        if dim % p == 0:
            return p
    return dim


def _cparams(*sem):
    return pltpu.CompilerParams(dimension_semantics=sem, vmem_limit_bytes=VMEM_LIMIT_BYTES)


def _sds(shape, dtype):
    return jax.ShapeDtypeStruct(shape, dtype)


def _sigmoid(v):
    return 1.0 / (1.0 + jnp.exp(-v))


MATMUL_VMEM_BUDGET = 46 * 1024 * 1024


def _tiles(mode, m, n):
    if mode == "tn" and m % 1024 != 0:
        return _pick(m, (512, 256, 128)), _pick(n, (1024, 512, 256, 128)), True
    return _pick(m, (1024, 512, 256, 128)), _pick(n, (512, 256, 128)), False


def _k_parts(m, n, k):
    tm, tn = _pick(m, (1024, 512, 256, 128)), _pick(n, (512, 256, 128))
    for parts in range(1, k // LANES + 1):
        if k % (parts * LANES) == 0 and 4 * (tm + tn) * (k // parts) + 16 * tm * tn <= MATMUL_VMEM_BUDGET:
            return parts
    return k // LANES


def _matmul(a, b, *, mode, out_dtype, name, res=None, dep=None):
    if mode == "nn":
        (m, k), (k2, n) = a.shape, b.shape
    elif mode == "nt":
        (m, k), (n, k2) = a.shape, b.shape
    else:
        (k, m), (k2, n) = a.shape, b.shape
    assert k == k2, (a.shape, b.shape, mode)
    parts = 1 if mode == "tn" else _k_parts(m, n, k)
    for p in range(parts):
        last = p == parts - 1
        res = _matmul_slice(a, b, mode=mode, out_dtype=out_dtype if last else F32, res=res, dep=dep,
                            kslice=(p, parts), name=name + ("_k%d" % p if parts > 1 else ""))
    return res


def _matmul_slice(a, b, *, mode, out_dtype, name, res, dep, kslice):
    part, parts = kslice
    if mode == "nn":
        (m, k), n = a.shape, b.shape[1]
    elif mode == "nt":
        (m, k), n = a.shape, b.shape[0]
    else:
        (k, m), n = a.shape, b.shape[1]
    tk = k // parts
    tm, tn, swap = _tiles(mode, m, n)
    dims = {"nn": (((1,), (0,)), ((), ())), "nt": (((1,), (1,)), ((), ())), "tn": (((0,), (0,)), ((), ()))}[mode]
    has_res = res is not None
    has_dep = dep is not None

    def body(*refs):
        a_ref, b_ref = refs[0], refs[1]
        o_ref = refs[2 + has_res + has_dep]
        val = lax.dot_general(a_ref[...], b_ref[...], dims, preferred_element_type=F32)
        if has_res:
            val = val + refs[2][...]
        o_ref[...] = val.astype(o_ref.dtype)

    def spec(shape, f):
        if swap:
            return pl.BlockSpec(shape, lambda j, i: f(i, j))
        return pl.BlockSpec(shape, f)

    if mode == "tn":
        a_spec = spec((tk, tm), lambda i, j: (part, i))
    else:
        a_spec = spec((tm, tk), lambda i, j: (i, part))
    if mode == "nt":
        b_spec = spec((tn, tk), lambda i, j: (j, part))
    else:
        b_spec = spec((tk, tn), lambda i, j: (part, j))
    o_spec = spec((tm, tn), lambda i, j: (i, j))
    return pl.pallas_call(
        body,
        name=name,
        grid=(n // tn, m // tm) if swap else (m // tm, n // tn),
        in_specs=[a_spec, b_spec] + ([o_spec] if has_res else []) + ([ANY] if has_dep else []),
        out_specs=o_spec,
        out_shape=_sds((m, n), out_dtype),
        compiler_params=_cparams("parallel", "parallel"),
    )(*([a, b] + ([res] if has_res else []) + ([dep] if has_dep else [])))


def _rms_fwd(x, g, name):
    t, d = x.shape
    tm = _pick(t, (512, 256, 128))

    def body(x_ref, g_ref, o_ref):
        xf = x_ref[...]
        r = lax.rsqrt(jnp.mean(xf * xf, axis=-1, keepdims=True) + EPS)
        o_ref[...] = (xf * r * g_ref[...]).astype(o_ref.dtype)

    row = pl.BlockSpec((tm, d), lambda i: (i, 0))
    return pl.pallas_call(
        body, name=name, grid=(t // tm,),
        in_specs=[row, pl.BlockSpec((1, d), lambda i: (0, 0))],
        out_specs=row, out_shape=_sds((t, d), BF16),
        compiler_params=_cparams("parallel"),
    )(x, g)


def _rms_bwd_math(xf, g, du):
    r = lax.rsqrt(jnp.mean(xf * xf, axis=-1, keepdims=True) + EPS)
    xh = xf * r
    gdy = g * du
    dx = r * (gdy - xh * jnp.mean(gdy * xh, axis=-1, keepdims=True))
    dg = jnp.sum(du * xh, axis=0, keepdims=True)
    return dx, dg


def _rms_bwd(x, g, du, dh, name):
    t, d = x.shape
    tm = _pick(t, (256, 128))
    has_dh = dh is not None

    def body(*refs):
        x_ref, g_ref, du_ref = refs[0], refs[1], refs[2]
        o_ref, ob_ref, dg_ref = refs[3 + has_dh:]
        dx, dg = _rms_bwd_math(x_ref[...], g_ref[...], du_ref[...].astype(F32))
        if has_dh:
            dx = dx + refs[3][...]
        o_ref[...] = dx
        ob_ref[...] = dx.astype(BF16)

        @pl.when(pl.program_id(0) == 0)
        def _():
            dg_ref[...] = dg

        @pl.when(pl.program_id(0) > 0)
        def _():
            dg_ref[...] += dg

    row = pl.BlockSpec((tm, d), lambda i: (i, 0))
    vec = pl.BlockSpec((1, d), lambda i: (0, 0))
    return pl.pallas_call(
        body, name=name, grid=(t // tm,),
        in_specs=[row, vec, row] + ([row] if has_dh else []),
        out_specs=[row, row, vec],
        out_shape=[_sds((t, d), F32), _sds((t, d), BF16), _sds((1, d), F32)],
        compiler_params=_cparams("arbitrary"),
    )(*([x, g, du] + ([dh] if has_dh else [])))


def _loss_head(h, g, tgt):
    t, d = h.shape
    tm = _pick(t, (256, 128))

    def body(h_ref, g_ref, t_ref, o_ref, ob_ref, dg_ref, l_ref):
        xf = h_ref[...]
        gv = g_ref[...]
        r = lax.rsqrt(jnp.mean(xf * xf, axis=-1, keepdims=True) + EPS)
        err = xf * r * gv - t_ref[...]
        part = 0.5 * jnp.sum(jnp.mean(err * err, axis=-1, keepdims=True), axis=0, keepdims=True)
        dx, dg = _rms_bwd_math(xf, gv, err * (1.0 / d))
        o_ref[...] = dx
        ob_ref[...] = dx.astype(BF16)
        lrow = jnp.broadcast_to(part, (1, LANES))

        @pl.when(pl.program_id(0) == 0)
        def _():
            dg_ref[...] = dg
            l_ref[...] = lrow

        @pl.when(pl.program_id(0) > 0)
        def _():
            dg_ref[...] += dg
            l_ref[...] += lrow

    row = pl.BlockSpec((tm, d), lambda i: (i, 0))
    vec = pl.BlockSpec((1, d), lambda i: (0, 0))
    return pl.pallas_call(
        body, name="loss_head", grid=(t // tm,),
        in_specs=[row, vec, row],
        out_specs=[row, row, vec, pl.BlockSpec((1, LANES), lambda i: (0, 0))],
        out_shape=[_sds((t, d), F32), _sds((t, d), BF16), _sds((1, d), F32), _sds((1, LANES), F32)],
        compiler_params=_cparams("arbitrary"),
    )(h, g, tgt)


def _rope_tables(t):
    half = HEAD_DIM // 2
    inv_freq = ROPE_THETA ** (-jnp.arange(half, dtype=F32) / half)
    ang = jnp.arange(t, dtype=F32)[:, None] * inv_freq[None, :]
    cos = jnp.cos(ang)
    sin = jnp.sin(ang)
    reps = LANES // HEAD_DIM
    cos_t = jnp.tile(jnp.concatenate([cos, cos], axis=1), (1, reps))
    sin_t = jnp.tile(jnp.concatenate([-sin, sin], axis=1), (1, reps))
    return cos_t, sin_t


def _rope(v, cos, sin):
    w = v.shape[1]
    c = jnp.tile(cos, (1, w // LANES))
    s = jnp.tile(sin, (1, w // LANES))
    lane = lax.broadcasted_iota(jnp.int32, v.shape, 1)
    first = (lane % HEAD_DIM) < (HEAD_DIM // 2)
    partner = jnp.where(first, pltpu.roll(v, w - HEAD_DIM // 2, 1), pltpu.roll(v, HEAD_DIM // 2, 1))
    return v * c + partner * s


def _heads(v, count):
    return jnp.concatenate([v[:, i * HEAD_DIM:(i + 1) * HEAD_DIM] for i in range(count)], axis=0)


def _unheads(v, count):
    r = v.shape[0] // count
    return jnp.concatenate([v[g * r:(g + 1) * r] for g in range(count)], axis=1)


def _rope_qkv(proj, cos, sin, aw, kvw):
    t = proj.shape[0]
    nkv = kvw // HEAD_DIM
    koff = aw // kvw
    tm = _pick(t, (256, 128))

    def body(q_ref, k_ref, v_ref, c_ref, s_ref, qo_ref, ko_ref, vo_ref):
        c, s = c_ref[...], s_ref[...]
        qo_ref[...] = _rope(q_ref[...].astype(F32), c, s).astype(BF16)
        k = _rope(k_ref[...].astype(F32), c, s).astype(BF16)
        v = v_ref[...].astype(BF16)
        for h in range(nkv):
            ko_ref[h] = k[:, h * HEAD_DIM:(h + 1) * HEAD_DIM]
            vo_ref[h] = v[:, h * HEAD_DIM:(h + 1) * HEAD_DIM]

    def row(w, j):
        return pl.BlockSpec((tm, w), lambda i: (i, j))

    hm = pl.BlockSpec((nkv, tm, HEAD_DIM), lambda i: (0, i, 0))
    return pl.pallas_call(
        body, name="rope_qkv", grid=(t // tm,),
        in_specs=[row(aw, 0), row(kvw, koff), row(kvw, koff + 1), row(LANES, 0), row(LANES, 0)],
        out_specs=[row(aw, 0), hm, hm],
        out_shape=[_sds((t, aw), BF16), _sds((nkv, t, HEAD_DIM), BF16), _sds((nkv, t, HEAD_DIM), BF16)],
        compiler_params=_cparams("parallel"),
    )(proj, proj, proj, cos, sin)


def _attn_probs(qs, kb, n, h, qpk, sinks_ref):
    s = lax.dot_general(qs, kb, (((1,), (1,)), ((), ())), preferred_element_type=F32) * (HEAD_DIM ** -0.5)
    qi = lax.broadcasted_iota(jnp.int32, (BLOCK, 2 * BLOCK), 0)
    kc = lax.broadcasted_iota(jnp.int32, (BLOCK, 2 * BLOCK), 1)
    valid = (kc > qi) & (kc <= qi + BLOCK) & ((kc >= BLOCK) | (n > 0))
    bias = jnp.tile(jnp.where(valid, 0.0, NEG).astype(F32), (qpk, 1))
    s = s + bias
    rowg = lax.broadcasted_iota(jnp.int32, (qpk * BLOCK, 1), 0) // BLOCK
    sink = jnp.zeros((qpk * BLOCK, 1), F32)
    for g in range(qpk):
        sink = jnp.where(rowg == g, sinks_ref[0, h * qpk + g], sink)
    m = jnp.maximum(jnp.max(s, axis=-1, keepdims=True), sink)
    e = jnp.exp(s - m)
    es = jnp.exp(sink - m)
    inv = 1.0 / (jnp.sum(e, axis=-1, keepdims=True) + es)
    return e * inv, es * inv, rowg


HEADS_PER_STEP = 4


def _attn_specs(qw, hp):
    def head(f):
        return pl.BlockSpec((hp, BLOCK, HEAD_DIM), lambda n, h: (h, f(n), 0))

    cur = lambda n: n
    prev = lambda n: jnp.maximum(n - 1, 0)
    return [pl.BlockSpec((BLOCK, hp * qw), lambda n, h: (n, h)), head(cur), head(prev), head(cur), head(prev),
            pl.BlockSpec(memory_space=pltpu.SMEM)]


def _attn_fwd(q_r, k_r, v_h, sinks):
    t, aw = q_r.shape
    nkv = k_r.shape[0]
    qpk = aw // (nkv * HEAD_DIM)
    qw = qpk * HEAD_DIM
    hp = HEADS_PER_STEP if nkv % HEADS_PER_STEP == 0 else 1

    def body(q_ref, kc_ref, kp_ref, vc_ref, vp_ref, sinks_ref, o_ref):
        n, hg = pl.program_id(0), pl.program_id(1)
        outs = []
        for j in range(hp):
            kb = jnp.concatenate([kp_ref[j], kc_ref[j]], axis=0)
            vb = jnp.concatenate([vp_ref[j], vc_ref[j]], axis=0)
            qs = _heads(q_ref[:, j * qw:(j + 1) * qw], qpk)
            p, _, _ = _attn_probs(qs, kb, n, hg * hp + j, qpk, sinks_ref)
            outs.append(_unheads(jnp.dot(p.astype(BF16), vb, preferred_element_type=F32), qpk))
        o_ref[...] = jnp.concatenate(outs, axis=1).astype(o_ref.dtype)

    return pl.pallas_call(
        body, name="attn_fwd", grid=(t // BLOCK, nkv // hp),
        in_specs=_attn_specs(qw, hp),
        out_specs=pl.BlockSpec((BLOCK, hp * qw), lambda n, h: (n, h)),
        out_shape=_sds((t, aw), BF16),
        compiler_params=_cparams("parallel", "parallel"),
    )(q_r, k_r, k_r, v_h, v_h, sinks)


def _attn_bwd(q_r, k_r, v_h, sinks, o, do):
    t, aw = q_r.shape
    nkv = k_r.shape[0]
    qpk = aw // (nkv * HEAD_DIM)
    qw = qpk * HEAD_DIM
    hp = HEADS_PER_STEP if nkv % HEADS_PER_STEP == 0 else 1
    scale = HEAD_DIM ** -0.5

    def body(q_ref, kc_ref, kp_ref, vc_ref, vp_ref, sinks_ref, o_ref, do_ref,
             dq_ref, dkc_ref, dkp_ref, dvc_ref, dvp_ref, ds_ref):
        n, hg = pl.program_id(0), pl.program_id(1)
        lane = lax.broadcasted_iota(jnp.int32, (8, LANES), 1)
        row0 = lax.broadcasted_iota(jnp.int32, (8, LANES), 0) == 0
        dsink = jnp.zeros((8, LANES), F32)
        dqs = []
        for j in range(hp):
            h = hg * hp + j
            cols = slice(j * qw, (j + 1) * qw)
            kb = jnp.concatenate([kp_ref[j], kc_ref[j]], axis=0)
            vb = jnp.concatenate([vp_ref[j], vc_ref[j]], axis=0)
            qs = _heads(q_ref[:, cols], qpk)
            dos = _heads(do_ref[:, cols], qpk)
            p, psink, rowg = _attn_probs(qs, kb, n, h, qpk, sinks_ref)
            pb = p.astype(BF16)
            delta = jnp.sum(dos.astype(F32) * _heads(o_ref[:, cols], qpk).astype(F32), axis=-1, keepdims=True)
            dv = lax.dot_general(pb, dos, (((0,), (0,)), ((), ())), preferred_element_type=F32)
            dp = lax.dot_general(dos, vb, (((1,), (1,)), ((), ())), preferred_element_type=F32)
            dsc = (p * (dp - delta)).astype(BF16)
            dqs.append(_unheads(jnp.dot(dsc, kb, preferred_element_type=F32) * scale, qpk))
            dk = lax.dot_general(dsc, qs, (((0,), (0,)), ((), ())), preferred_element_type=F32) * scale
            dkp_ref[j] = dk[:BLOCK]
            dkc_ref[j] = dk[BLOCK:]
            dvp_ref[j] = dv[:BLOCK]
            dvc_ref[j] = dv[BLOCK:]
            sink_term = psink * delta
            for g in range(qpk):
                val = -jnp.sum(jnp.where(rowg == g, sink_term, 0.0))
                dsink = jnp.where(row0 & (lane == h * qpk + g), val, dsink)
        dq_ref[...] = jnp.concatenate(dqs, axis=1).astype(dq_ref.dtype)
        first = (n == 0) & (hg == 0)

        @pl.when(first)
        def _():
            ds_ref[...] = dsink

        @pl.when(jnp.logical_not(first))
        def _():
            ds_ref[...] += dsink

    qblk = pl.BlockSpec((BLOCK, hp * qw), lambda n, h: (n, h))
    kvblk = pl.BlockSpec((hp, BLOCK, HEAD_DIM), lambda n, h: (h, n, 0))
    return pl.pallas_call(
        body, name="attn_bwd", grid=(t // BLOCK, nkv // hp),
        in_specs=_attn_specs(qw, hp) + [qblk, qblk],
        out_specs=[qblk, kvblk, kvblk, kvblk, kvblk, pl.BlockSpec((8, LANES), lambda n, h: (0, 0))],
        out_shape=[_sds((t, aw), BF16)] + [_sds((nkv, t, HEAD_DIM), F32)] * 4 + [_sds((8, LANES), F32)],
        compiler_params=_cparams("arbitrary", "arbitrary"),
    )(q_r, k_r, k_r, v_h, v_h, sinks, o, do)


HALO = 16


def _shift_down(v, k, halo):
    rows = lax.broadcasted_iota(jnp.int32, v.shape, 0)
    out = pltpu.roll(v, k, 0)
    for r in range(k):
        out = jnp.where(rows == r, halo[HALO - k + r:HALO - k + r + 1, :], out)
    return out


def _shift_up(v, k, halo):
    tm = v.shape[0]
    rows = lax.broadcasted_iota(jnp.int32, v.shape, 0)
    out = pltpu.roll(v, tm - k, 0)
    for r in range(k):
        out = jnp.where(rows == tm - k + r, halo[r:r + 1, :], out)
    return out


def _conv_fwd(proj, conv_w, zoff, cw, cb):
    t = proj.shape[0]
    tm = _pick(t, (512, 256, 128))
    zb, nb = zoff // cb, cw // cb
    hb = tm // HALO

    def body(z_ref, gb_ref, gc_ref, zp_ref, gcp_ref, w_ref, o_ref):
        i = pl.program_id(0)
        cz = gc_ref[...].astype(F32) * z_ref[...].astype(F32)
        czp = gcp_ref[...].astype(F32) * zp_ref[...].astype(F32) * (i > 0).astype(F32)
        w = w_ref[...]
        y = w[0:1] * _shift_down(cz, 2, czp) + w[1:2] * _shift_down(cz, 1, czp) + w[2:3] * cz
        o_ref[...] = (gb_ref[...].astype(F32) * y).astype(o_ref.dtype)

    def col(k):
        return pl.BlockSpec((tm, cb), lambda i, j: (i, zb + k * nb + j))

    def halo(k):
        return pl.BlockSpec((HALO, cb), lambda i, j: (jnp.maximum(i * hb - 1, 0), zb + k * nb + j))

    return pl.pallas_call(
        body, name="conv_fwd", grid=(t // tm, nb),
        in_specs=[col(0), col(1), col(2), halo(0), halo(2), pl.BlockSpec((3, cb), lambda i, j: (0, j))],
        out_specs=pl.BlockSpec((tm, cb), lambda i, j: (i, j)),
        out_shape=_sds((t, cw), BF16),
        compiler_params=_cparams("parallel", "parallel"),
    )(proj, proj, proj, proj, proj, conv_w)


def _conv_bwd(proj, conv_w, dco, zoff, cw, cb):
    t = proj.shape[0]
    tm = _pick(t, (512, 256, 128))
    zb, nb = zoff // cb, cw // cb
    hb = tm // HALO
    nt = t // tm

    def body(z_ref, gb_ref, gc_ref, zp_ref, gcp_ref, gbn_ref, w_ref, d_ref, dn_ref, dz_ref, dgb_ref, dgc_ref, dw_ref):
        i = pl.program_id(1)
        z, gb, gc = z_ref[...].astype(F32), gb_ref[...].astype(F32), gc_ref[...].astype(F32)
        d = d_ref[...].astype(F32)
        cz = gc * z
        czp = gcp_ref[...].astype(F32) * zp_ref[...].astype(F32) * (i > 0).astype(F32)
        w = w_ref[...]
        cz1 = _shift_down(cz, 1, czp)
        cz2 = _shift_down(cz, 2, czp)
        y = w[0:1] * cz2 + w[1:2] * cz1 + w[2:3] * cz
        dgb_ref[...] = (d * y).astype(dgb_ref.dtype)
        dy = d * gb
        dyn = dn_ref[...].astype(F32) * gbn_ref[...].astype(F32) * (i < nt - 1).astype(F32)
        dcz = w[2:3] * dy + w[1:2] * _shift_up(dy, 1, dyn) + w[0:1] * _shift_up(dy, 2, dyn)
        dgc_ref[...] = (dcz * z).astype(dgc_ref.dtype)
        dz_ref[...] = (dcz * gc).astype(dz_ref.dtype)
        rows = lax.broadcasted_iota(jnp.int32, (8, cb), 0)
        dw = jnp.zeros((8, cb), F32)
        for r, tap in enumerate((cz2, cz1, cz)):
            dw = jnp.where(rows == r, jnp.sum(dy * tap, axis=0, keepdims=True), dw)

        @pl.when(i == 0)
        def _():
            dw_ref[...] = dw

        @pl.when(i > 0)
        def _():
            dw_ref[...] += dw

    def col(k):
        return pl.BlockSpec((tm, cb), lambda j, i: (i, zb + k * nb + j))

    def halo_prev(k):
        return pl.BlockSpec((HALO, cb), lambda j, i: (jnp.maximum(i * hb - 1, 0), zb + k * nb + j))

    own = pl.BlockSpec((tm, cb), lambda j, i: (i, j))
    nxt = lambda i: jnp.minimum((i + 1) * hb, t // HALO - 1)
    return pl.pallas_call(
        body, name="conv_bwd", grid=(nb, nt),
        in_specs=[col(0), col(1), col(2), halo_prev(0), halo_prev(2),
                  pl.BlockSpec((HALO, cb), lambda j, i: (nxt(i), zb + nb + j)),
                  pl.BlockSpec((3, cb), lambda j, i: (0, j)), own,
                  pl.BlockSpec((HALO, cb), lambda j, i: (nxt(i), j))],
        out_specs=[own, own, own, pl.BlockSpec((8, cb), lambda j, i: (0, j))],
        out_shape=[_sds((t, cw), BF16)] * 3 + [_sds((8, cw), F32)],
        compiler_params=_cparams("parallel", "arbitrary"),
    )(proj, proj, proj, proj, proj, proj, conv_w, dco, dco)


def _merge_fwd(proj, ya, yc, goff, cb):
    t, d = ya.shape
    tm = _pick(t, (512, 256, 128))
    gb_, nb = goff // cb, d // cb

    def body(ga_ref, gc_ref, ya_ref, yc_ref, o_ref):
        f = lambda r: r[...].astype(F32)
        o_ref[...] = (_sigmoid(f(ga_ref)) * f(ya_ref) + _sigmoid(f(gc_ref)) * f(yc_ref)).astype(o_ref.dtype)

    own = pl.BlockSpec((tm, cb), lambda i, j: (i, j))
    return pl.pallas_call(
        body, name="merge_fwd", grid=(t // tm, nb),
        in_specs=[pl.BlockSpec((tm, cb), lambda i, j: (i, gb_ + j)),
                  pl.BlockSpec((tm, cb), lambda i, j: (i, gb_ + nb + j)), own, own],
        out_specs=own, out_shape=_sds((t, d), BF16),
        compiler_params=_cparams("parallel", "parallel"),
    )(proj, proj, ya, yc)


def _merge_bwd(proj, ya, yc, dm, goff, cb):
    t, d = ya.shape
    tm = _pick(t, (512, 256, 128))
    gb_, nb = goff // cb, d // cb

    def body(ga_ref, gc_ref, ya_ref, yc_ref, dm_ref, dya_ref, dyc_ref, dga_ref, dgc_ref):
        dmv = dm_ref[...].astype(F32)
        sa = _sigmoid(ga_ref[...].astype(F32))
        sc = _sigmoid(gc_ref[...].astype(F32))
        dya_ref[...] = (dmv * sa).astype(BF16)
        dyc_ref[...] = (dmv * sc).astype(BF16)
        dga_ref[...] = (dmv * ya_ref[...].astype(F32) * sa * (1.0 - sa)).astype(BF16)
        dgc_ref[...] = (dmv * yc_ref[...].astype(F32) * sc * (1.0 - sc)).astype(BF16)

    own = pl.BlockSpec((tm, cb), lambda i, j: (i, j))
    return pl.pallas_call(
        body, name="merge_bwd", grid=(t // tm, nb),
        in_specs=[pl.BlockSpec((tm, cb), lambda i, j: (i, gb_ + j)),
                  pl.BlockSpec((tm, cb), lambda i, j: (i, gb_ + nb + j)), own, own, own],
        out_specs=[own] * 4, out_shape=[_sds((t, d), BF16)] * 4,
        compiler_params=_cparams("parallel", "parallel"),
    )(proj, proj, ya, yc, dm)


def _assemble_dproj(dq, dkc, dkp, dvc, dvp, cos, sin, dz, dgb, dgc, dga, dgg):
    t, aw = dq.shape
    nkv, cw, d = dkc.shape[0], dz.shape[1], dga.shape[1]
    kvw = nkv * HEAD_DIM
    nblk = t // BLOCK
    width = aw + 2 * kvw + 3 * cw + 2 * d

    def body(dq_ref, dkc_ref, dkp_ref, dvc_ref, dvp_ref, c_ref, s_ref, dz_ref, dgb_ref, dgc_ref, dga_ref, dgg_ref,
             o_ref):
        keep = (pl.program_id(0) < nblk - 1).astype(F32)
        c, s = c_ref[...], s_ref[...]
        dk = jnp.concatenate([dkc_ref[h] + dkp_ref[h] * keep for h in range(nkv)], axis=1)
        dv = jnp.concatenate([dvc_ref[h] + dvp_ref[h] * keep for h in range(nkv)], axis=1)
        o_ref[...] = jnp.concatenate(
            [_rope(dq_ref[...].astype(F32), c, -s).astype(BF16), _rope(dk, c, -s).astype(BF16), dv.astype(BF16),
             dz_ref[...], dgb_ref[...], dgc_ref[...], dga_ref[...], dgg_ref[...]], axis=1)

    def cur(w):
        return pl.BlockSpec((BLOCK, w), lambda n: (n, 0))

    head_cur = pl.BlockSpec((nkv, BLOCK, HEAD_DIM), lambda n: (0, n, 0))
    head_nxt = pl.BlockSpec((nkv, BLOCK, HEAD_DIM), lambda n: (0, jnp.minimum(n + 1, nblk - 1), 0))
    return pl.pallas_call(
        body, name="assemble_dproj", grid=(nblk,),
        in_specs=[cur(aw), head_cur, head_nxt, head_cur, head_nxt, cur(LANES), cur(LANES),
                  cur(cw), cur(cw), cur(cw), cur(d), cur(d)],
        out_specs=cur(width), out_shape=_sds((t, width), BF16),
        compiler_params=_cparams("parallel"),
    )(dq, dkc, dkp, dvc, dvp, cos, sin, dz, dgb, dgc, dga, dgg)


def _xattn_probs(qh, kh):
    s = lax.dot_general(qh, kh, (((1,), (1,)), ((), ())), preferred_element_type=F32) * (X_HEAD_DIM ** -0.5)
    e = jnp.exp(s - jnp.max(s, axis=-1, keepdims=True))
    return e * (1.0 / jnp.sum(e, axis=-1, keepdims=True))


def _xattn_fwd(xq, kv):
    t, xw = xq.shape
    mt = kv.shape[0]
    tm = _pick(t, (512, 256, 128))

    def body(q_ref, kv_ref, o_ref):
        outs = []
        for hd in range(xw // X_HEAD_DIM):
            hs = slice(hd * X_HEAD_DIM, (hd + 1) * X_HEAD_DIM)
            vs = slice(xw + hd * X_HEAD_DIM, xw + (hd + 1) * X_HEAD_DIM)
            p = _xattn_probs(q_ref[:, hs], kv_ref[:, hs])
            outs.append(jnp.dot(p.astype(BF16), kv_ref[:, vs], preferred_element_type=F32))
        o_ref[...] = jnp.concatenate(outs, axis=1).astype(o_ref.dtype)

    return pl.pallas_call(
        body, name="xattn_fwd", grid=(t // tm,),
        in_specs=[pl.BlockSpec((tm, xw), lambda i: (i, 0)), pl.BlockSpec((mt, 2 * xw), lambda i: (0, 0))],
        out_specs=pl.BlockSpec((tm, xw), lambda i: (i, 0)), out_shape=_sds((t, xw), BF16),
        compiler_params=_cparams("parallel"),
    )(xq, kv)


def _xattn_bwd(xq, kv, do):
    t, xw = xq.shape
    mt = kv.shape[0]
    tm = _pick(t, (512, 256, 128))
    scale = X_HEAD_DIM ** -0.5

    def body(q_ref, kv_ref, do_ref, dq_ref, dkv_ref):
        dqs, dks, dvs = [], [], []
        for hd in range(xw // X_HEAD_DIM):
            hs = slice(hd * X_HEAD_DIM, (hd + 1) * X_HEAD_DIM)
            vs = slice(xw + hd * X_HEAD_DIM, xw + (hd + 1) * X_HEAD_DIM)
            qh, kh, vh, doh = q_ref[:, hs], kv_ref[:, hs], kv_ref[:, vs], do_ref[:, hs]
            p = _xattn_probs(qh, kh)
            pb = p.astype(BF16)
            o = jnp.dot(pb, vh, preferred_element_type=F32)
            delta = jnp.sum(doh.astype(F32) * o, axis=-1, keepdims=True)
            dvs.append(lax.dot_general(pb, doh, (((0,), (0,)), ((), ())), preferred_element_type=F32))
            dp = lax.dot_general(doh, vh, (((1,), (1,)), ((), ())), preferred_element_type=F32)
            dsc = (p * (dp - delta)).astype(BF16)
            dqs.append(jnp.dot(dsc, kh, preferred_element_type=F32) * scale)
            dks.append(lax.dot_general(dsc, qh, (((0,), (0,)), ((), ())), preferred_element_type=F32) * scale)
        dq_ref[...] = jnp.concatenate(dqs, axis=1).astype(dq_ref.dtype)
        dkv = jnp.concatenate(dks + dvs, axis=1)

        @pl.when(pl.program_id(0) == 0)
        def _():
            dkv_ref[...] = dkv

        @pl.when(pl.program_id(0) > 0)
        def _():
            dkv_ref[...] += dkv

    row = pl.BlockSpec((tm, xw), lambda i: (i, 0))
    whole = pl.BlockSpec((mt, 2 * xw), lambda i: (0, 0))
    return pl.pallas_call(
        body, name="xattn_bwd", grid=(t // tm,),
        in_specs=[row, whole, row], out_specs=[row, whole],
        out_shape=[_sds((t, xw), BF16), _sds((mt, 2 * xw), F32)],
        compiler_params=_cparams("arbitrary"),
    )(xq, kv, do)


def _swiglu_fwd(hid):
    t, f2 = hid.shape
    f = f2 // 2
    tm = _pick(t, (256, 128))

    def body(h_ref, o_ref):
        a = h_ref[:, :f].astype(F32)
        b = h_ref[:, f:].astype(F32)
        o_ref[...] = (a * _sigmoid(a) * b).astype(o_ref.dtype)

    return pl.pallas_call(
        body, name="swiglu_fwd", grid=(t // tm,),
        in_specs=[pl.BlockSpec((tm, f2), lambda i: (i, 0))],
        out_specs=pl.BlockSpec((tm, f), lambda i: (i, 0)), out_shape=_sds((t, f), BF16),
        compiler_params=_cparams("parallel"),
    )(hid)


def _swiglu_bwd(hid, dact):
    t, f2 = hid.shape
    f = f2 // 2
    tm = _pick(t, (128,))

    def body(h_ref, d_ref, o_ref):
        a = h_ref[:, :f].astype(F32)
        b = h_ref[:, f:].astype(F32)
        d = d_ref[...].astype(F32)
        sg = _sigmoid(a)
        o_ref[:, :f] = (d * b * sg * (1.0 + a * (1.0 - sg))).astype(o_ref.dtype)
        o_ref[:, f:] = (d * a * sg).astype(o_ref.dtype)

    return pl.pallas_call(
        body, name="swiglu_bwd", grid=(t // tm,),
        in_specs=[pl.BlockSpec((tm, f2), lambda i: (i, 0)), pl.BlockSpec((tm, f), lambda i: (i, 0))],
        out_specs=pl.BlockSpec((tm, f2), lambda i: (i, 0)), out_shape=_sds((t, f2), BF16),
        compiler_params=_cparams("parallel"),
    )(hid, dact)


def _adamw(w, g, m, v, name):
    r, c = w.shape
    tr = _pick(r, (256, 128, 64, 32, 16, 8)) if r * c > 65536 else r

    def body(w_ref, g_ref, m_ref, v_ref, d_ref, nm_ref, nv_ref):
        gv = g_ref[...]
        m2 = ADAM_B1 * m_ref[...] + (1.0 - ADAM_B1) * gv
        v2 = ADAM_B2 * v_ref[...] + (1.0 - ADAM_B2) * (gv * gv)
        m_hat = m2 / (1.0 - ADAM_B1 ** ADAM_STEP)
        v_hat = v2 / (1.0 - ADAM_B2 ** ADAM_STEP)
        d_ref[...] = -ADAM_LR * (m_hat / (jnp.sqrt(v_hat) + ADAM_EPS) + ADAM_WD * w_ref[...])
        nm_ref[...] = m2
        nv_ref[...] = v2

    blk = pl.BlockSpec((tr, c), lambda i: (i, 0))
    return pl.pallas_call(
        body, name=name, grid=(r // tr,),
        in_specs=[blk] * 4, out_specs=[blk] * 3, out_shape=[_sds((r, c), F32)] * 3,
        compiler_params=_cparams("parallel"),
    )(w, g, m, v)


class _Weights:
    def __init__(self, full):
        self.full = full
        self.grads = {}

    def get(self, name):
        return self.full[name]

    def mark(self, tag, value):
        return value

    def grad(self, name, g):
        self.grads[name] = g

    def dep(self):
        return None


def _local_step(x, mem, tgt, g_mix, sinks, g_xattn, g_mem, g_ffn, g_final, dims, wts):
    t, d = x.shape
    aw, cw, kvw = dims
    cb = 2 * kvw
    zoff = aw + 2 * kvw
    goff = zoff + 3 * cw
    cos, sin = _rope_tables(t)
    mark, get = wts.mark, wts.get

    def mm(a, b, **kw):
        return _matmul(a, b, dep=wts.dep(), **kw)

    u1 = mark("u1", _rms_fwd(x, g_mix, "rms_mix"))
    proj = mark("proj", mm(u1, get("w_in"), mode="nn", out_dtype=F32, name="mm_proj"))
    q_r, k_r, v_h = _rope_qkv(proj, cos, sin, aw, kvw)
    attn_o = mark("attn_o", _attn_fwd(q_r, k_r, v_h, sinks))
    conv_o = mark("conv_o", _conv_fwd(proj, get("conv_w"), zoff, cw, cb))
    ya = mark("ya", mm(attn_o, get("w_attn_proj"), mode="nn", out_dtype=BF16, name="mm_yattn"))
    yc = mark("yc", mm(conv_o, get("w_conv_proj"), mode="nn", out_dtype=BF16, name="mm_yconv"))
    merged = mark("merged", _merge_fwd(proj, ya, yc, goff, cb))
    h1 = mark("h1", mm(merged, get("w_mix_out"), mode="nn", out_dtype=F32, name="mm_mix", res=x))
    u2 = mark("u2", _rms_fwd(h1, g_xattn, "rms_xattn"))
    mem_n = _rms_fwd(mem, g_mem, "rms_mem")
    xq = mark("xq", mm(u2, get("w_xq"), mode="nn", out_dtype=BF16, name="mm_xq"))
    kv = mm(mem_n, get("w_xkv"), mode="nn", out_dtype=BF16, name="mm_xkv")
    xo = mark("xo", _xattn_fwd(xq, kv))
    h2 = mark("h2", mm(xo, get("w_xo"), mode="nn", out_dtype=F32, name="mm_xo", res=h1))
    u3 = mark("u3", _rms_fwd(h2, g_ffn, "rms_ffn"))
    hid = mark("hid", mm(u3, get("w_ffn_in"), mode="nn", out_dtype=BF16, name="mm_ffn_in"))
    act = mark("act", _swiglu_fwd(hid))
    h3 = mm(act, get("w_ffn_out"), mode="nn", out_dtype=F32, name="mm_ffn_out", res=h2)
    dh3, dh3b, dg_final, loss = _loss_head(h3, g_final, tgt)
    mark("dh3", dh3b)

    wts.grad("w_ffn_out", mm(act, dh3b, mode="tn", out_dtype=BF16, name="mm_dw_ffn_out"))
    dact = mark("dact", mm(dh3b, get("w_ffn_out"), mode="nt", out_dtype=BF16, name="mm_dact"))
    dhid = mark("dhid", _swiglu_bwd(hid, dact))
    wts.grad("w_ffn_in", mm(u3, dhid, mode="tn", out_dtype=BF16, name="mm_dw_ffn_in"))
    du3 = mark("du3", mm(dhid, get("w_ffn_in"), mode="nt", out_dtype=F32, name="mm_du3"))
    dh2, dh2b, dg_ffn = _rms_bwd(h2, g_ffn, du3, dh3, "rms_bwd_ffn")
    mark("dh2", dh2b)
    wts.grad("w_xo", mm(xo, dh2b, mode="tn", out_dtype=BF16, name="mm_dw_xo"))
    dxo = mm(dh2b, get("w_xo"), mode="nt", out_dtype=BF16, name="mm_dxo")
    dxq, dkv = _xattn_bwd(xq, kv, dxo)
    dkvb = dkv.astype(BF16)
    wts.grad("w_xq", mm(u2, dxq, mode="tn", out_dtype=BF16, name="mm_dw_xq"))
    du2 = mm(dxq, get("w_xq"), mode="nt", out_dtype=F32, name="mm_du2")
    wts.grad("w_xkv", mm(mem_n, dkvb, mode="tn", out_dtype=BF16, name="mm_dw_xkv"))
    dmem_n = mm(dkvb, get("w_xkv"), mode="nt", out_dtype=F32, name="mm_dmem")
    _, _, dg_mem = _rms_bwd(mem, g_mem, dmem_n, None, "rms_bwd_mem")
    dh1, dh1b, dg_xattn = _rms_bwd(h1, g_xattn, du2, dh2, "rms_bwd_xattn")
    mark("dh1", dh1b)
    wts.grad("w_mix_out", mm(merged, dh1b, mode="tn", out_dtype=BF16, name="mm_dw_mix"))
    dmerged = mm(dh1b, get("w_mix_out"), mode="nt", out_dtype=BF16, name="mm_dmerged")
    dya, dyc, dga, dgg = _merge_bwd(proj, ya, yc, dmerged, goff, cb)
    mark("dya", dya)
    wts.grad("w_attn_proj", mm(attn_o, dya, mode="tn", out_dtype=BF16, name="mm_dw_attn_proj"))
    dattn_o = mm(dya, get("w_attn_proj"), mode="nt", out_dtype=BF16, name="mm_dattn")
    wts.grad("w_conv_proj", mm(conv_o, dyc, mode="tn", out_dtype=BF16, name="mm_dw_conv_proj"))
    dconv_o = mark("dconv_o", mm(dyc, get("w_conv_proj"), mode="nt", out_dtype=BF16, name="mm_dconv"))
    dz, dgb, dgc, dconv_w = _conv_bwd(proj, get("conv_w"), dconv_o, zoff, cw, cb)
    mark("dz", dz)
    dq, dkc, dkp, dvc, dvp, dsinks = _attn_bwd(q_r, k_r, v_h, sinks, attn_o, dattn_o)
    mark("dq", dq)
    dproj = mark("dproj", _assemble_dproj(dq, dkc, dkp, dvc, dvp, cos, sin, dz, dgb, dgc, dga, dgg))
    wts.grad("w_in", mm(u1, dproj, mode="tn", out_dtype=BF16, name="mm_dw_in"))
    du1 = mark("du1", mm(dproj, get("w_in"), mode="nt", out_dtype=F32, name="mm_du1"))
    grad_x, _, dg_mix = _rms_bwd(x, g_mix, du1, dh1, "rms_bwd_mix")
    mark("grad_x", grad_x)

    small = dict(g_mix=dg_mix, g_xattn=dg_xattn, g_mem=dg_mem, g_ffn=dg_ffn, g_final=dg_final,
                 conv_w=dconv_w[:3], attn_sinks=dsinks[0:1, :sinks.shape[1]], loss=loss[0:1, 0:1])
    return grad_x, small


BIG = (("w_in", 1), ("w_attn_proj", 1), ("w_conv_proj", 1), ("w_mix_out", 0), ("w_xq", 0), ("w_xkv", 0),
       ("w_xo", 1), ("w_ffn_in", 1), ("w_ffn_out", 0))


def _place():
    x, y, c = lax.axis_index("x"), lax.axis_index("y"), lax.axis_index("c")
    chips = [(1 - x, y), (x, 1 - y), (1 - x, 1 - y)]
    return x, y, c, chips


def _window(ref, ax, shard_shape, s, h):
    sr, sc = shard_shape
    hr = sr // 2
    if ax == 1:
        return ref.at[pl.ds(pl.multiple_of(h * hr, 16), hr), pl.ds(pl.multiple_of(s * sc, LANES), sc)]
    return ref.at[pl.ds(pl.multiple_of(s * sr + h * hr, 16), hr), :]


def _half(ref, h):
    hr = ref.shape[0] // 2
    return ref.at[pl.ds(pl.multiple_of(h * hr, 16), hr), :]


def _remote(src, dst, send_sem, recv_sem, dev):
    return pltpu.make_async_remote_copy(src_ref=src, dst_ref=dst, send_sem=send_sem, recv_sem=recv_sem,
                                        device_id=dev, device_id_type=MESH)


def _cast_to_full(shard, ax, me, name, dtype=BF16):
    sr, sc = shard.shape
    tr = _pick(sr, (256, 352, 128, 64, 32, 16))
    nr = sr // tr
    full = (sr * N_CHIPS, sc) if ax == 0 else (sr, sc * N_CHIPS)

    def body(me_ref, s_ref, o_ref):
        o_ref[...] = s_ref[...].astype(o_ref.dtype)

    if ax == 1:
        out_spec = pl.BlockSpec((tr, sc), lambda r, me_ref: (r, me_ref[0]))
    else:
        out_spec = pl.BlockSpec((tr, sc), lambda r, me_ref: (me_ref[0] * nr + r, 0))
    return pl.pallas_call(
        body, name=name,
        grid_spec=pltpu.PrefetchScalarGridSpec(
            num_scalar_prefetch=1, grid=(nr,), in_specs=[pl.BlockSpec((tr, sc), lambda r, me_ref: (r, 0))],
            out_specs=out_spec),
        out_shape=_sds(full, dtype),
        compiler_params=_cparams("parallel"),
    )(me, shard)


HBM = pl.BlockSpec(memory_space=pltpu.HBM)
SEM = pl.BlockSpec(memory_space=pltpu.SEMAPHORE)
EFFECT = pltpu.SideEffectType.DATAFLOW_SIDE_EFFECTING


def _in_hbm(a):
    return pltpu.with_memory_space_constraint(a, pltpu.HBM)


def _gather_window(ref, ax, shard_shape, s, h):
    if h is not None:
        return _window(ref, ax, shard_shape, s, h)
    sr, sc = shard_shape
    if ax == 1:
        return ref.at[:, pl.ds(pl.multiple_of(s * sc, LANES), sc)]
    return ref.at[pl.ds(pl.multiple_of(s * sr, 8), sr), :]


def _ag_start(fulls, axes, shard_shapes, whole):
    n = len(fulls)

    def body(*refs):
        src = refs[:n]
        send_sems, recv_sems = refs[n], refs[n + 1]
        token = refs[2 * n + 2]
        x, y, c, chips = _place()
        me = 2 * x + y
        for i in range(n):
            h = None if whole[i] else c
            for j, chip in enumerate(chips):
                blk = _gather_window(src[i], axes[i], shard_shapes[i], me, h)
                _remote(blk, blk, send_sems.at[3 * i + j], recv_sems.at[3 * i + j], (*chip, c)).start()
        token[...] = jnp.zeros_like(token)

    res = pl.pallas_call(
        body, name="ag_start_" + str(n),
        out_shape=(pltpu.SemaphoreType.DMA((3 * n,)), pltpu.SemaphoreType.DMA((3 * n,)),
                   *[pltpu.HBM(f.shape, f.dtype) for f in fulls], _sds((8, LANES), F32)),
        in_specs=[HBM] * n, out_specs=(SEM, SEM, *[HBM] * n, pl.BlockSpec(memory_space=pltpu.VMEM)),
        input_output_aliases={i: 2 + i for i in range(n)},
        compiler_params=pltpu.CompilerParams(has_side_effects=EFFECT),
    )(*[_in_hbm(f) for f in fulls])
    return res[0], res[1], list(res[2:2 + n]), res[2 + n]


def _ag_mid(bufs, slots, axes, shard_shapes, whole, send_sems, recv_sems, after, name):
    ng = len(bufs)

    def body(*refs):
        src = refs[:ng]
        s_in, r_in = refs[ng], refs[ng + 1]
        fsend, frecv = refs[ng + 3], refs[ng + 4]
        x, y, c, chips = _place()
        me = 2 * x + y
        sib = (x, y, 1 - c)
        for k, i in enumerate(slots):
            h = None if whole[k] else c
            for j, chip in enumerate(chips):
                cj = 2 * chip[0] + chip[1]
                mine = _gather_window(src[k], axes[k], shard_shapes[k], me, h)
                theirs = _gather_window(src[k], axes[k], shard_shapes[k], cj, h)
                _remote(theirs, theirs, s_in.at[3 * i + j], r_in.at[3 * i + j], (*chip, c)).wait_recv()
                _remote(mine, mine, s_in.at[3 * i + j], r_in.at[3 * i + j], (*chip, c)).wait_send()
                if not whole[k]:
                    _remote(theirs, theirs, fsend.at[3 * k + j], frecv.at[3 * k + j], sib).start()
        token = refs[2 * ng + 5]
        token[...] = jnp.zeros_like(token)

    res = pl.pallas_call(
        body, name=name,
        out_shape=(pltpu.SemaphoreType.DMA((3 * ng,)), pltpu.SemaphoreType.DMA((3 * ng,)),
                   *[pltpu.HBM(b.shape, b.dtype) for b in bufs], _sds((8, LANES), F32)),
        in_specs=[HBM] * ng + [SEM, SEM, ANY],
        out_specs=(SEM, SEM, *[HBM] * ng, pl.BlockSpec(memory_space=pltpu.VMEM)),
        input_output_aliases={k: 2 + k for k in range(ng)},
        compiler_params=pltpu.CompilerParams(has_side_effects=EFFECT),
    )(*bufs, send_sems, recv_sems, after)
    return res[0], res[1], list(res[2:2 + ng]), res[2 + ng]


def _ag_wait(bufs, axes, shard_shapes, whole, fsend, frecv, after, name):
    ng = len(bufs)

    def body(*refs):
        src = refs[:ng]
        s_in, r_in = refs[ng], refs[ng + 1]
        x, y, c, chips = _place()
        sib = (x, y, 1 - c)
        for k in range(ng):
            if whole[k]:
                continue
            for j, chip in enumerate(chips):
                cj = 2 * chip[0] + chip[1]
                sent = _gather_window(src[k], axes[k], shard_shapes[k], cj, c)
                landed = _gather_window(src[k], axes[k], shard_shapes[k], cj, 1 - c)
                _remote(landed, landed, s_in.at[3 * k + j], r_in.at[3 * k + j], sib).wait_recv()
                _remote(sent, sent, s_in.at[3 * k + j], r_in.at[3 * k + j], sib).wait_send()

    res = pl.pallas_call(
        body, name=name,
        out_shape=tuple(pltpu.HBM(b.shape, b.dtype) for b in bufs),
        in_specs=[HBM] * ng + [SEM, SEM, ANY], out_specs=tuple([HBM] * ng),
        input_output_aliases={k: k for k in range(ng)},
        compiler_params=pltpu.CompilerParams(has_side_effects=EFFECT),
    )(*bufs, fsend, frecv, after)
    return list(res)


class _Schedule:
    GROUPS = ((("w_in", "conv_w"), "start", "u1"),
              (("w_attn_proj", "w_conv_proj", "w_mix_out", "w_xq", "w_xkv", "w_xo"), "proj", "conv_o"),
              (("w_ffn_in",), "h1", "h2"),
              (("w_ffn_out",), "u3", "hid"))
    STARTS = ((0,), (1, 2, 3))
    REDUCE = ((("w_ffn_out",), "dact", "grad:w_ffn_in", "du3"),
              (("w_ffn_in",), "grad:w_ffn_in", "dya", "grad_x"),
              (("w_xo", "w_xq", "w_xkv", "w_mix_out", "w_attn_proj", "w_conv_proj"), "dconv_o", "dq", "grad_x"),
              (("w_in",), "grad:w_in", "end", "end2"))

    def __init__(self, seed, axes, shard_shapes, place, on_ready):
        self.ax, self.shape, self.place, self.on_ready = axes, shard_shapes, place, on_ready
        self.stage, self.buf, self.slot, self.passes = {}, {}, {}, {}
        self.ready = set()
        self.grads = {}
        token = None
        for groups in self.STARTS:
            order = [nm for g in groups for nm in self.GROUPS[g][0]]
            send, recv, bufs, token = _ag_start([seed(nm, token) for nm in order], *self._meta(order))
            self.buf.update(zip(order, bufs))
            self.slot.update({nm: (send, recv, k) for k, nm in enumerate(order)})
        self.token = self.latest = token
        self.mark("start", token)

    def _meta(self, names):
        return ([self.ax[nm] for nm in names], [self.shape[nm] for nm in names], [nm == "conv_w" for nm in names])

    def mark(self, tag, value):
        for g, (names, mid, wait) in enumerate(self.GROUPS):
            if tag == mid:
                send, recv, _ = self.slot[names[0]]
                fs, fr, bufs, self.latest = _ag_mid([self.buf[nm] for nm in names], [self.slot[nm][2] for nm in names],
                                                    *self._meta(names), send, recv, value, "ag_mid_%d" % g)
                self.buf.update(zip(names, bufs))
                self.passes[g] = (fs, fr)
            if tag == wait:
                fs, fr = self.passes[g]
                bufs = _ag_wait([self.buf[nm] for nm in names], *self._meta(names), fs, fr, value, "ag_wait_%d" % g)
                self.buf.update(zip(names, bufs))
                self.ready.update(names)
        for g, (names, send, total, finish) in enumerate(self.REDUCE):
            st = self.stage.get(g)
            if st is None:
                continue
            ng = len(names)
            if tag == send and st["at"] == "pair":
                arrs = _exchange_wait("rs_pair_wait_%d" % g, st["arrs"], *st["sems"], st["plan"], value)
                parts = [_pair_add(arrs[k], arrs[ng + k], self.ax[nm], self.shape[nm], self.place, "pair_add_" + nm)
                         for k, nm in enumerate(names)]
                plan, nsem = _plan_chip(ng)
                ss, rs, arrs, self.latest = _exchange_start(
                    "rs_chip_start_%d" % g, parts + [lax.empty(p.shape, p.dtype) for p in parts], nsem, plan)
                self.stage[g] = dict(at="chip", arrs=arrs, sems=(ss, rs), plan=plan)
            elif tag == total and st["at"] == "chip":
                arrs = _exchange_wait("rs_chip_wait_%d" % g, st["arrs"], *st["sems"], st["plan"], value)
                halves = [_chip_add(arrs[k], arrs[ng + k], self.place, "chip_add_" + nm) for k, nm in enumerate(names)]
                plan, nsem = _plan_gather(ng)
                ss, rs, arrs, self.latest = _exchange_start("rs_gather_start_%d" % g, halves, nsem, plan)
                self.stage[g] = dict(at="gather", arrs=arrs, sems=(ss, rs), plan=plan)
            elif tag == finish and st["at"] == "gather":
                arrs = _exchange_wait("rs_gather_wait_%d" % g, st["arrs"], *st["sems"], st["plan"], value)
                self.stage[g] = dict(at="done")
                for nm, shard in zip(names, arrs):
                    self.on_ready(nm, shard)
        return value

    def get(self, name):
        assert name in self.ready, name
        return self.buf[name]

    def grad(self, name, g):
        self.grads[name] = g
        for gi, (names, _, _, _) in enumerate(self.REDUCE):
            if name == names[-1]:
                gs = [self.grads[nm] for nm in names]
                plan, nsem = _plan_pair(len(names), [self.ax[nm] for nm in names], [self.shape[nm] for nm in names])
                ss, rs, arrs, self.latest = _exchange_start(
                    "rs_pair_start_%d" % gi, gs + [lax.empty(a.shape, a.dtype) for a in gs], nsem, plan)
                self.stage[gi] = dict(at="pair", arrs=arrs, sems=(ss, rs), plan=plan)
                g = self.latest
        self.mark("grad:" + name, g)

    def dep(self):
        return self.latest


def _exchange_start(name, arrays, nsem, plan):
    n = len(arrays)

    def body(*refs):
        send_sems, recv_sems, token = refs[n], refs[n + 1], refs[2 * n + 2]
        sends, _ = plan(refs[:n])
        for k, (src, dst, dev) in enumerate(sends):
            _remote(src, dst, send_sems.at[k], recv_sems.at[k], dev).start()
        token[...] = jnp.zeros_like(token)

    res = pl.pallas_call(
        body, name=name,
        out_shape=(pltpu.SemaphoreType.DMA((nsem,)), pltpu.SemaphoreType.DMA((nsem,)),
                   *[pltpu.HBM(a.shape, a.dtype) for a in arrays], _sds((8, LANES), F32)),
        in_specs=[HBM] * n, out_specs=(SEM, SEM, *[HBM] * n, pl.BlockSpec(memory_space=pltpu.VMEM)),
        input_output_aliases={i: 2 + i for i in range(n)},
        compiler_params=pltpu.CompilerParams(has_side_effects=EFFECT),
    )(*[_in_hbm(a) for a in arrays])
    return res[0], res[1], list(res[2:2 + n]), res[2 + n]


def _exchange_wait(name, arrays, send_sems, recv_sems, plan, after):
    n = len(arrays)

    def body(*refs):
        s_in, r_in = refs[n], refs[n + 1]
        sends, recvs = plan(refs[:n])
        for k, land in enumerate(recvs):
            _remote(land, land, s_in.at[k], r_in.at[k], sends[k][2]).wait_recv()
        for k, (src, _, dev) in enumerate(sends):
            _remote(src, src, s_in.at[k], r_in.at[k], dev).wait_send()

    res = pl.pallas_call(
        body, name=name,
        out_shape=tuple(pltpu.HBM(a.shape, a.dtype) for a in arrays),
        in_specs=[HBM] * n + [SEM, SEM, ANY], out_specs=tuple([HBM] * n),
        input_output_aliases={i: i for i in range(n)},
        compiler_params=pltpu.CompilerParams(has_side_effects=EFFECT),
    )(*arrays, send_sems, recv_sems, after)
    return list(res)


def _plan_pair(n, axes, shard_shapes):
    def plan(refs):
        g, ra = refs[:n], refs[n:]
        x, y, c, _ = _place()
        sib = (x, y, 1 - c)

        def pieces(ref, i, h):
            if axes[i] == 1:
                return [_half(ref, h)]
            return [_window(ref, 0, shard_shapes[i], s, h) for s in range(N_CHIPS)]

        sends, recvs = [], []
        for i in range(n):
            sends += [(src, dst, sib) for src, dst in zip(pieces(g[i], i, 1 - c), pieces(ra[i], i, 1 - c))]
            recvs += pieces(ra[i], i, c)
        return sends, recvs

    return plan, sum(1 if ax == 1 else N_CHIPS for ax in axes)


def _plan_chip(n):
    def plan(refs):
        p, rc = refs[:n], refs[n:]
        x, y, c, chips = _place()
        me = 2 * x + y
        sends, recvs = [], []
        for i in range(n):
            for chip in chips:
                cj = 2 * chip[0] + chip[1]
                sends.append((p[i].at[cj], rc[i].at[me], (*chip, c)))
                recvs.append(rc[i].at[cj])
        return sends, recvs

    return plan, 3 * n


def _plan_gather(n):
    def plan(refs):
        x, y, c, _ = _place()
        sib = (x, y, 1 - c)
        return ([(_half(r, c), _half(r, c), sib) for r in refs], [_half(r, 1 - c) for r in refs])

    return plan, n


def _pair_add(g, ra, ax, shard_shape, place, name):
    sr, sc = shard_shape
    hr = sr // 2
    wc = sc
    tr = _pick(hr, (256, 352, 128, 64, 32, 16))
    nr = hr // tr

    def body(p_ref, a_ref, b_ref, o_ref):
        o_ref[...] = (a_ref[...].astype(F32) + b_ref[...].astype(F32)).astype(o_ref.dtype)

    if ax == 1:
        src = pl.BlockSpec((tr, wc), lambda s, r, p_ref: (p_ref[0] * nr + r, s))
    else:
        src = pl.BlockSpec((tr, wc), lambda s, r, p_ref: (s * 2 * nr + p_ref[0] * nr + r, 0))
    return pl.pallas_call(
        body, name=name,
        grid_spec=pltpu.PrefetchScalarGridSpec(
            num_scalar_prefetch=1, grid=(N_CHIPS, nr), in_specs=[src, src],
            out_specs=pl.BlockSpec((None, tr, wc), lambda s, r, p_ref: (s, r, 0))),
        out_shape=_sds((N_CHIPS, hr, wc), BF16),
        compiler_params=_cparams("parallel", "parallel"),
    )(place, g, ra)


def _chip_add(part, rc, place, name):
    _, hr, wc = rc.shape
    tr = _pick(hr, (256, 352, 128, 64, 32, 16))
    nr = hr // tr

    def body(p_ref, own_ref, r1_ref, r2_ref, r3_ref, o_ref):
        acc = own_ref[...].astype(F32)
        for r_ref in (r1_ref, r2_ref, r3_ref):
            acc = acc + r_ref[...].astype(F32)
        o_ref[...] = acc

    def slot(k):
        return pl.BlockSpec((None, tr, wc), lambda r, p_ref: ((p_ref[1] + k) % N_CHIPS, r, 0))

    return pl.pallas_call(
        body, name=name,
        grid_spec=pltpu.PrefetchScalarGridSpec(
            num_scalar_prefetch=1, grid=(nr,), in_specs=[slot(0), slot(1), slot(2), slot(3)],
            out_specs=pl.BlockSpec((tr, wc), lambda r, p_ref: (p_ref[0] * nr + r, 0))),
        out_shape=_sds((2 * hr, wc), F32),
        compiler_params=_cparams("parallel"),
    )(place, part, rc, rc, rc)


N_DEV = 8


def _all_reduce_small(buf):
    r, cdim = buf.shape

    def body(x_ref, o_ref, land, send_sems, recv_sems):
        x, y, c, _ = _place()
        me = 4 * x + 2 * y + c
        land[me] = x_ref[...]
        sends = []
        for k in range(1, N_DEV):
            kx, ky, kc = (k >> 2) & 1, (k >> 1) & 1, k & 1
            peer = (1 - x if kx else x, 1 - y if ky else y, 1 - c if kc else c)
            cp = _remote(x_ref, land.at[me], send_sems.at[k - 1], recv_sems.at[k - 1], peer)
            cp.start()
            sends.append(cp)
        for k in range(1, N_DEV):
            kx, ky, kc = (k >> 2) & 1, (k >> 1) & 1, k & 1
            peer = (1 - x if kx else x, 1 - y if ky else y, 1 - c if kc else c)
            pidx = 4 * peer[0] + 2 * peer[1] + peer[2]
            _remote(land.at[pidx], land.at[pidx], send_sems.at[k - 1], recv_sems.at[k - 1], peer).wait_recv()
        for cp in sends:
            cp.wait_send()
        acc = land[0]
        for dev in range(1, N_DEV):
            acc = acc + land[dev]
        o_ref[...] = acc

    vm = pl.BlockSpec(memory_space=pltpu.VMEM)
    return pl.pallas_call(
        body, name="all_reduce_small", in_specs=[vm], out_specs=vm, out_shape=_sds((r, cdim), F32),
        scratch_shapes=[pltpu.VMEM((N_DEV, r, cdim), F32), pltpu.SemaphoreType.DMA((N_DEV - 1,)),
                        pltpu.SemaphoreType.DMA((N_DEV - 1,))],
    )(buf)


SMALL_ROWS = 16


def kernel(x, mem, g_mix, w_in, conv_w, attn_sinks, w_attn_proj, w_conv_proj, w_mix_out, g_xattn, g_mem, w_xq, w_xkv, w_xo, g_ffn, w_ffn_in, w_ffn_out, g_final, loss_target, m_g_mix, m_w_in, m_conv_w, m_attn_sinks, m_w_attn_proj, m_w_conv_proj, m_w_mix_out, m_g_xattn, m_g_mem, m_w_xq, m_w_xkv, m_w_xo, m_g_ffn, m_w_ffn_in, m_w_ffn_out, m_g_final, v_g_mix, v_w_in, v_conv_w, v_attn_sinks, v_w_attn_proj, v_w_conv_proj, v_w_mix_out, v_g_xattn, v_g_mem, v_w_xq, v_w_xkv, v_w_xo, v_g_ffn, v_w_ffn_in, v_w_ffn_out, v_g_final):
    w = dict(g_mix=g_mix, w_in=w_in[0], conv_w=conv_w[0], attn_sinks=attn_sinks, w_attn_proj=w_attn_proj[0],
             w_conv_proj=w_conv_proj[0], w_mix_out=w_mix_out[0], g_xattn=g_xattn, g_mem=g_mem, w_xq=w_xq[0],
             w_xkv=w_xkv[0], w_xo=w_xo[0], g_ffn=g_ffn, w_ffn_in=w_ffn_in[0], w_ffn_out=w_ffn_out[0],
             g_final=g_final[None])
    m = dict(g_mix=m_g_mix, w_in=m_w_in[0], conv_w=m_conv_w[0], attn_sinks=m_attn_sinks,
             w_attn_proj=m_w_attn_proj[0], w_conv_proj=m_w_conv_proj[0], w_mix_out=m_w_mix_out[0],
             g_xattn=m_g_xattn, g_mem=m_g_mem, w_xq=m_w_xq[0], w_xkv=m_w_xkv[0], w_xo=m_w_xo[0], g_ffn=m_g_ffn,
             w_ffn_in=m_w_ffn_in[0], w_ffn_out=m_w_ffn_out[0], g_final=m_g_final[None])
    v = dict(g_mix=v_g_mix, w_in=v_w_in[0], conv_w=v_conv_w[0], attn_sinks=v_attn_sinks,
             w_attn_proj=v_w_attn_proj[0], w_conv_proj=v_w_conv_proj[0], w_mix_out=v_w_mix_out[0],
             g_xattn=v_g_xattn, g_mem=v_g_mem, w_xq=v_w_xq[0], w_xkv=v_w_xkv[0], w_xo=v_w_xo[0], g_ffn=v_g_ffn,
             w_ffn_in=v_w_ffn_in[0], w_ffn_out=v_w_ffn_out[0], g_final=v_g_final[None])
    names = [nm for nm, _ in BIG]
    axes = [ax for _, ax in BIG]
    d = x.shape[2]
    cw = w["conv_w"].shape[1] * N_CHIPS
    chip = (2 * lax.axis_index("x") + lax.axis_index("y")).astype(jnp.int32)
    place = jnp.stack([lax.axis_index("c").astype(jnp.int32), chip])
    shard_shapes = [w[nm].shape for nm in names]

    def seed(nm, token):
        me1 = chip.reshape(1)
        if token is not None:
            me1 = me1 + token[0, 0].astype(jnp.int32)
        if nm == "conv_w":
            return _cast_to_full(w[nm], 1, me1, "place_conv_w", F32)
        return _cast_to_full(w[nm], dict(BIG)[nm], me1, "cast_" + nm)

    upd = {}

    def on_ready(nm, shard):
        grads[nm] = shard
        upd[nm] = _adamw(w[nm], shard, m[nm], v[nm], "adamw_" + nm)

    grads = {}
    wts = _Schedule(seed, dict(zip(names + ["conv_w"], axes + [1])),
                    dict(zip(names + ["conv_w"], shard_shapes + [w["conv_w"].shape])), place, on_ready)
    aw, cw = w["w_attn_proj"].shape[0], w["w_conv_proj"].shape[0]
    kvw = (w["w_in"].shape[1] * N_CHIPS - aw - 3 * cw - 2 * d) // 2
    grad_x, small = _local_step(
        x[0], mem[0], loss_target[0], w["g_mix"] + wts.token[0:1, 0:1], w["attn_sinks"], w["g_xattn"], w["g_mem"],
        w["g_ffn"], w["g_final"], (aw, cw, kvw), wts)

    pw = max(d, cw)

    def row(a):
        return jnp.pad(a, ((0, 0), (0, pw - a.shape[1])))

    gains = ("g_mix", "g_xattn", "g_mem", "g_ffn", "g_final")
    packed = jnp.concatenate(
        [row(small[nm]) for nm in gains] + [row(small["conv_w"]),
         row(jnp.concatenate([small["attn_sinks"], small["loss"]], axis=1)),
         jnp.zeros((SMALL_ROWS - 9, pw), F32)], axis=0)
    total = _all_reduce_small(packed)
    wts.mark("end", total)
    nsink = attn_sinks.shape[1]
    grads.update({nm: total[k:k + 1, :d] for k, nm in enumerate(gains)})
    grads.update(conv_w=lax.dynamic_slice(total, (5, chip * (cw // N_CHIPS)), (3, cw // N_CHIPS)),
                 attn_sinks=total[8:9, :nsink])
    loss = total[8, nsink]
    for nm in gains + ("conv_w", "attn_sinks"):
        upd[nm] = _adamw(w[nm], grads[nm], m[nm], v[nm], "adamw_" + nm)
    wts.mark("end2", upd["g_final"][0])

    order = ["g_mix", "w_in", "conv_w", "attn_sinks", "w_attn_proj", "w_conv_proj", "w_mix_out", "g_xattn", "g_mem",
             "w_xq", "w_xkv", "w_xo", "g_ffn", "w_ffn_in", "w_ffn_out", "g_final"]

    stacked = set(names) | {"conv_w"}

    def shaped(nm, a):
        if nm == "g_final":
            return a[0]
        return a[None] if nm in stacked else a

    outs = [loss, grad_x[None]]
    outs += [shaped(nm, grads[nm]) for nm in order]
    for k in range(3):
        outs += [shaped(nm, upd[nm][k]) for nm in order]
    return tuple(outs)
```

```python
import functools

import jax
import jax.numpy as jnp
from jax import lax
from jax.experimental import pallas as pl
from jax.experimental.pallas import tpu as pltpu

F32 = jnp.float32
BF16 = jnp.bfloat16

VMEM_LIMIT_BYTES = 56 * 1024 * 1024
LANES = 128
HEAD_DIM = 64
BLOCK = 128
X_HEAD_DIM = 128
ROPE_THETA = 10000.0
EPS = 1e-6
NEG = -1e30
ADAM_LR, ADAM_B1, ADAM_B2, ADAM_EPS, ADAM_WD, ADAM_STEP = 0.001, 0.9, 0.999, 1e-08, 0.01, 10
N_CHIPS = 4
MESH = pl.DeviceIdType.MESH
ANY = pl.BlockSpec(memory_space=pl.ANY)


def _pick(dim, prefs):
    for p in prefs:
        if dim % p == 0:
            return p
    return dim


def _cparams(*sem):
    return pltpu.CompilerParams(dimension_semantics=sem, vmem_limit_bytes=VMEM_LIMIT_BYTES)


def _sds(shape, dtype):
    return jax.ShapeDtypeStruct(shape, dtype)


def _sigmoid(v):
    return 1.0 / (1.0 + jnp.exp(-v))


MATMUL_VMEM_BUDGET = 46 * 1024 * 1024


def _tiles(mode, m, n):
    if mode == "tn" and m % 1024 != 0:
        return _pick(m, (512, 256, 128)), _pick(n, (1024, 512, 256, 128)), True
    return _pick(m, (1024, 512, 256, 128)), _pick(n, (512, 256, 128)), False


def _k_parts(m, n, k):
    tm, tn = _pick(m, (1024, 512, 256, 128)), _pick(n, (512, 256, 128))
    for parts in range(1, k // LANES + 1):
        if k % (parts * LANES) == 0 and 4 * (tm + tn) * (k // parts) + 16 * tm * tn <= MATMUL_VMEM_BUDGET:
            return parts
    return k // LANES


def _matmul(a, b, *, mode, out_dtype, name, res=None, dep=None, cols=None):
    if mode == "nn":
        (m, k), (k2, n) = a.shape, b.shape
    elif mode == "nt":
        (m, k), (n, k2) = a.shape, b.shape
    else:
        (k, m), (k2, n) = a.shape, b.shape
    assert k == k2, (a.shape, b.shape, mode)
    assert cols is None or mode != "nt"
    cols = cols or (0, n)
    parts = 1 if mode == "tn" else _k_parts(m, cols[1], k)
    for p in range(parts):
        last = p == parts - 1
        res = _matmul_slice(a, b, mode=mode, out_dtype=out_dtype if last else F32, res=res, dep=dep, cols=cols,
                            kslice=(p, parts), name=name + ("_k%d" % p if parts > 1 else ""))
    return res


def _matmul_slice(a, b, *, mode, out_dtype, name, res, dep, kslice, cols):
    part, parts = kslice
    (m, k) = a.shape if mode != "tn" else a.shape[::-1]
    col0, n = cols
    tk = k // parts
    tm, tn, swap = _tiles(mode, m, n)
    while col0 % tn:
        tn //= 2
    joff = col0 // tn
    dims = {"nn": (((1,), (0,)), ((), ())), "nt": (((1,), (1,)), ((), ())), "tn": (((0,), (0,)), ((), ()))}[mode]
    has_res = res is not None
    has_dep = dep is not None

    def body(*refs):
        a_ref, b_ref = refs[0], refs[1]
        o_ref = refs[2 + has_res + has_dep]
        val = lax.dot_general(a_ref[...], b_ref[...], dims, preferred_element_type=F32)
        if has_res:
            val = val + refs[2][...]
        o_ref[...] = val.astype(o_ref.dtype)

    def spec(shape, f):
        if swap:
            return pl.BlockSpec(shape, lambda j, i: f(i, j))
        return pl.BlockSpec(shape, f)

    if mode == "tn":
        a_spec = spec((tk, tm), lambda i, j: (part, i))
    else:
        a_spec = spec((tm, tk), lambda i, j: (i, part))
    if mode == "nt":
        b_spec = spec((tn, tk), lambda i, j: (j, part))
    else:
        b_spec = spec((tk, tn), lambda i, j: (part, joff + j))
    o_spec = spec((tm, tn), lambda i, j: (i, j))
    return pl.pallas_call(
        body,
        name=name,
        grid=(n // tn, m // tm) if swap else (m // tm, n // tn),
        in_specs=[a_spec, b_spec] + ([o_spec] if has_res else []) + ([ANY] if has_dep else []),
        out_specs=o_spec,
        out_shape=_sds((m, n), out_dtype),
        compiler_params=_cparams("parallel", "parallel"),
    )(*([a, b] + ([res] if has_res else []) + ([dep] if has_dep else [])))


def _rms_fwd(x, g, name):
    t, d = x.shape
    tm = _pick(t, (512, 256, 128))

    def body(x_ref, g_ref, o_ref):
        xf = x_ref[...]
        r = lax.rsqrt(jnp.mean(xf * xf, axis=-1, keepdims=True) + EPS)
        o_ref[...] = (xf * r * g_ref[...]).astype(o_ref.dtype)

    row = pl.BlockSpec((tm, d), lambda i: (i, 0))
    return pl.pallas_call(
        body, name=name, grid=(t // tm,),
        in_specs=[row, pl.BlockSpec((1, d), lambda i: (0, 0))],
        out_specs=row, out_shape=_sds((t, d), BF16),
        compiler_params=_cparams("parallel"),
    )(x, g)


def _rms_bwd_math(xf, g, du):
    r = lax.rsqrt(jnp.mean(xf * xf, axis=-1, keepdims=True) + EPS)
    xh = xf * r
    gdy = g * du
    dx = r * (gdy - xh * jnp.mean(gdy * xh, axis=-1, keepdims=True))
    dg = jnp.sum(du * xh, axis=0, keepdims=True)
    return dx, dg


def _rms_bwd(x, g, du, dh, name):
    t, d = x.shape
    tm = _pick(t, (256, 128))
    has_dh = dh is not None

    def body(*refs):
        x_ref, g_ref, du_ref = refs[0], refs[1], refs[2]
        o_ref, ob_ref, dg_ref = refs[3 + has_dh:]
        dx, dg = _rms_bwd_math(x_ref[...], g_ref[...], du_ref[...].astype(F32))
        if has_dh:
            dx = dx + refs[3][...]
        o_ref[...] = dx
        ob_ref[...] = dx.astype(BF16)

        @pl.when(pl.program_id(0) == 0)
        def _():
            dg_ref[...] = dg

        @pl.when(pl.program_id(0) > 0)
        def _():
            dg_ref[...] += dg

    row = pl.BlockSpec((tm, d), lambda i: (i, 0))
    vec = pl.BlockSpec((1, d), lambda i: (0, 0))
    return pl.pallas_call(
        body, name=name, grid=(t // tm,),
        in_specs=[row, vec, row] + ([row] if has_dh else []),
        out_specs=[row, row, vec],
        out_shape=[_sds((t, d), F32), _sds((t, d), BF16), _sds((1, d), F32)],
        compiler_params=_cparams("arbitrary"),
    )(*([x, g, du] + ([dh] if has_dh else [])))


def _loss_head(h, g, tgt):
    t, d = h.shape
    tm = _pick(t, (256, 128))

    def body(h_ref, g_ref, t_ref, o_ref, ob_ref, dg_ref, l_ref):
        xf = h_ref[...]
        gv = g_ref[...]
        r = lax.rsqrt(jnp.mean(xf * xf, axis=-1, keepdims=True) + EPS)
        err = xf * r * gv - t_ref[...]
        part = 0.5 * jnp.sum(jnp.mean(err * err, axis=-1, keepdims=True), axis=0, keepdims=True)
        dx, dg = _rms_bwd_math(xf, gv, err * (1.0 / d))
        o_ref[...] = dx
        ob_ref[...] = dx.astype(BF16)
        lrow = jnp.broadcast_to(part, (1, LANES))

        @pl.when(pl.program_id(0) == 0)
        def _():
            dg_ref[...] = dg
            l_ref[...] = lrow

        @pl.when(pl.program_id(0) > 0)
        def _():
            dg_ref[...] += dg
            l_ref[...] += lrow

    row = pl.BlockSpec((tm, d), lambda i: (i, 0))
    vec = pl.BlockSpec((1, d), lambda i: (0, 0))
    return pl.pallas_call(
        body, name="loss_head", grid=(t // tm,),
        in_specs=[row, vec, row],
        out_specs=[row, row, vec, pl.BlockSpec((1, LANES), lambda i: (0, 0))],
        out_shape=[_sds((t, d), F32), _sds((t, d), BF16), _sds((1, d), F32), _sds((1, LANES), F32)],
        compiler_params=_cparams("arbitrary"),
    )(h, g, tgt)


def _rope_tables(t):
    half = HEAD_DIM // 2
    inv_freq = ROPE_THETA ** (-jnp.arange(half, dtype=F32) / half)
    ang = jnp.arange(t, dtype=F32)[:, None] * inv_freq[None, :]
    cos = jnp.cos(ang)
    sin = jnp.sin(ang)
    reps = LANES // HEAD_DIM
    cos_t = jnp.tile(jnp.concatenate([cos, cos], axis=1), (1, reps))
    sin_t = jnp.tile(jnp.concatenate([-sin, sin], axis=1), (1, reps))
    return cos_t, sin_t


def _rope(v, cos, sin):
    w = v.shape[1]
    c = jnp.tile(cos, (1, w // LANES))
    s = jnp.tile(sin, (1, w // LANES))
    lane = lax.broadcasted_iota(jnp.int32, v.shape, 1)
    first = (lane % HEAD_DIM) < (HEAD_DIM // 2)
    partner = jnp.where(first, pltpu.roll(v, w - HEAD_DIM // 2, 1), pltpu.roll(v, HEAD_DIM // 2, 1))
    return v * c + partner * s


def _heads(v, count):
    return jnp.concatenate([v[:, i * HEAD_DIM:(i + 1) * HEAD_DIM] for i in range(count)], axis=0)


def _unheads(v, count):
    r = v.shape[0] // count
    return jnp.concatenate([v[g * r:(g + 1) * r] for g in range(count)], axis=1)


def _rope_qkv(proj, cos, sin, aw, kvw):
    t = proj.shape[0]
    nkv = kvw // HEAD_DIM
    koff = aw // kvw
    tm = _pick(t, (256, 128))

    def body(q_ref, k_ref, v_ref, c_ref, s_ref, qo_ref, ko_ref, vo_ref):
        c, s = c_ref[...], s_ref[...]
        qo_ref[...] = _rope(q_ref[...].astype(F32), c, s).astype(BF16)
        k = _rope(k_ref[...].astype(F32), c, s).astype(BF16)
        v = v_ref[...].astype(BF16)
        for h in range(nkv):
            ko_ref[h] = k[:, h * HEAD_DIM:(h + 1) * HEAD_DIM]
            vo_ref[h] = v[:, h * HEAD_DIM:(h + 1) * HEAD_DIM]

    def row(w, j):
        return pl.BlockSpec((tm, w), lambda i: (i, j))

    hm = pl.BlockSpec((nkv, tm, HEAD_DIM), lambda i: (0, i, 0))
    return pl.pallas_call(
        body, name="rope_qkv", grid=(t // tm,),
        in_specs=[row(aw, 0), row(kvw, koff), row(kvw, koff + 1), row(LANES, 0), row(LANES, 0)],
        out_specs=[row(aw, 0), hm, hm],
        out_shape=[_sds((t, aw), BF16), _sds((nkv, t, HEAD_DIM), BF16), _sds((nkv, t, HEAD_DIM), BF16)],
        compiler_params=_cparams("parallel"),
    )(proj, proj, proj, cos, sin)


def _attn_probs(qs, kb, n, h, qpk, sinks_ref):
    s = lax.dot_general(qs, kb, (((1,), (1,)), ((), ())), preferred_element_type=F32) * (HEAD_DIM ** -0.5)
    qi = lax.broadcasted_iota(jnp.int32, (BLOCK, 2 * BLOCK), 0)
    kc = lax.broadcasted_iota(jnp.int32, (BLOCK, 2 * BLOCK), 1)
    valid = (kc > qi) & (kc <= qi + BLOCK) & ((kc >= BLOCK) | (n > 0))
    bias = jnp.tile(jnp.where(valid, 0.0, NEG).astype(F32), (qpk, 1))
    s = s + bias
    rowg = lax.broadcasted_iota(jnp.int32, (qpk * BLOCK, 1), 0) // BLOCK
    sink = jnp.zeros((qpk * BLOCK, 1), F32)
    for g in range(qpk):
        sink = jnp.where(rowg == g, sinks_ref[0, h * qpk + g], sink)
    m = jnp.maximum(jnp.max(s, axis=-1, keepdims=True), sink)
    e = jnp.exp(s - m)
    es = jnp.exp(sink - m)
    inv = 1.0 / (jnp.sum(e, axis=-1, keepdims=True) + es)
    return e * inv, es * inv, rowg


HEADS_PER_STEP = 4


def _attn_specs(qw, hp):
    def head(f):
        return pl.BlockSpec((hp, BLOCK, HEAD_DIM), lambda n, h: (h, f(n), 0))

    cur = lambda n: n
    prev = lambda n: jnp.maximum(n - 1, 0)
    return [pl.BlockSpec((BLOCK, hp * qw), lambda n, h: (n, h)), head(cur), head(prev), head(cur), head(prev),
            pl.BlockSpec(memory_space=pltpu.SMEM)]


def _attn_fwd(q_r, k_r, v_h, sinks):
    t, aw = q_r.shape
    nkv = k_r.shape[0]
    qpk = aw // (nkv * HEAD_DIM)
    qw = qpk * HEAD_DIM
    hp = HEADS_PER_STEP if nkv % HEADS_PER_STEP == 0 else 1

    def body(q_ref, kc_ref, kp_ref, vc_ref, vp_ref, sinks_ref, o_ref):
        n, hg = pl.program_id(0), pl.program_id(1)
        outs = []
        for j in range(hp):
            kb = jnp.concatenate([kp_ref[j], kc_ref[j]], axis=0)
            vb = jnp.concatenate([vp_ref[j], vc_ref[j]], axis=0)
            qs = _heads(q_ref[:, j * qw:(j + 1) * qw], qpk)
            p, _, _ = _attn_probs(qs, kb, n, hg * hp + j, qpk, sinks_ref)
            outs.append(_unheads(jnp.dot(p.astype(BF16), vb, preferred_element_type=F32), qpk))
        o_ref[...] = jnp.concatenate(outs, axis=1).astype(o_ref.dtype)

    return pl.pallas_call(
        body, name="attn_fwd", grid=(t // BLOCK, nkv // hp),
        in_specs=_attn_specs(qw, hp),
        out_specs=pl.BlockSpec((BLOCK, hp * qw), lambda n, h: (n, h)),
        out_shape=_sds((t, aw), BF16),
        compiler_params=_cparams("parallel", "parallel"),
    )(q_r, k_r, k_r, v_h, v_h, sinks)


def _attn_bwd(q_r, k_r, v_h, sinks, o, do):
    t, aw = q_r.shape
    nkv = k_r.shape[0]
    qpk = aw // (nkv * HEAD_DIM)
    qw = qpk * HEAD_DIM
    hp = HEADS_PER_STEP if nkv % HEADS_PER_STEP == 0 else 1
    scale = HEAD_DIM ** -0.5

    def body(q_ref, kc_ref, kp_ref, vc_ref, vp_ref, sinks_ref, o_ref, do_ref,
             dq_ref, dkc_ref, dkp_ref, dvc_ref, dvp_ref, ds_ref):
        n, hg = pl.program_id(0), pl.program_id(1)
        lane = lax.broadcasted_iota(jnp.int32, (8, LANES), 1)
        row0 = lax.broadcasted_iota(jnp.int32, (8, LANES), 0) == 0
        dsink = jnp.zeros((8, LANES), F32)
        dqs = []
        for j in range(hp):
            h = hg * hp + j
            cols = slice(j * qw, (j + 1) * qw)
            kb = jnp.concatenate([kp_ref[j], kc_ref[j]], axis=0)
            vb = jnp.concatenate([vp_ref[j], vc_ref[j]], axis=0)
            qs = _heads(q_ref[:, cols], qpk)
            dos = _heads(do_ref[:, cols], qpk)
            p, psink, rowg = _attn_probs(qs, kb, n, h, qpk, sinks_ref)
            pb = p.astype(BF16)
            delta = jnp.sum(dos.astype(F32) * _heads(o_ref[:, cols], qpk).astype(F32), axis=-1, keepdims=True)
            dv = lax.dot_general(pb, dos, (((0,), (0,)), ((), ())), preferred_element_type=F32)
            dp = lax.dot_general(dos, vb, (((1,), (1,)), ((), ())), preferred_element_type=F32)
            dsc = (p * (dp - delta)).astype(BF16)
            dqs.append(_unheads(jnp.dot(dsc, kb, preferred_element_type=F32) * scale, qpk))
            dk = lax.dot_general(dsc, qs, (((0,), (0,)), ((), ())), preferred_element_type=F32) * scale
            dkp_ref[j] = dk[:BLOCK]
            dkc_ref[j] = dk[BLOCK:]
            dvp_ref[j] = dv[:BLOCK]
            dvc_ref[j] = dv[BLOCK:]
            sink_term = psink * delta
            for g in range(qpk):
                val = -jnp.sum(jnp.where(rowg == g, sink_term, 0.0))
                dsink = jnp.where(row0 & (lane == h * qpk + g), val, dsink)
        dq_ref[...] = jnp.concatenate(dqs, axis=1).astype(dq_ref.dtype)
        first = (n == 0) & (hg == 0)

        @pl.when(first)
        def _():
            ds_ref[...] = dsink

        @pl.when(jnp.logical_not(first))
        def _():
            ds_ref[...] += dsink

    qblk = pl.BlockSpec((BLOCK, hp * qw), lambda n, h: (n, h))
    kvblk = pl.BlockSpec((hp, BLOCK, HEAD_DIM), lambda n, h: (h, n, 0))
    return pl.pallas_call(
        body, name="attn_bwd", grid=(t // BLOCK, nkv // hp),
        in_specs=_attn_specs(qw, hp) + [qblk, qblk],
        out_specs=[qblk, kvblk, kvblk, kvblk, kvblk, pl.BlockSpec((8, LANES), lambda n, h: (0, 0))],
        out_shape=[_sds((t, aw), BF16)] + [_sds((nkv, t, HEAD_DIM), F32)] * 4 + [_sds((8, LANES), F32)],
        compiler_params=_cparams("arbitrary", "arbitrary"),
    )(q_r, k_r, k_r, v_h, v_h, sinks, o, do)


HALO = 16


def _shift_down(v, k, halo):
    rows = lax.broadcasted_iota(jnp.int32, v.shape, 0)
    out = pltpu.roll(v, k, 0)
    for r in range(k):
        out = jnp.where(rows == r, halo[HALO - k + r:HALO - k + r + 1, :], out)
    return out


def _shift_up(v, k, halo):
    tm = v.shape[0]
    rows = lax.broadcasted_iota(jnp.int32, v.shape, 0)
    out = pltpu.roll(v, tm - k, 0)
    for r in range(k):
        out = jnp.where(rows == tm - k + r, halo[r:r + 1, :], out)
    return out


def _conv_fwd(proj, conv_w, zoff, cw, cb):
    t = proj.shape[0]
    tm = _pick(t, (512, 256, 128))
    zb, nb = zoff // cb, cw // cb
    hb = tm // HALO

    def body(z_ref, gb_ref, gc_ref, zp_ref, gcp_ref, w_ref, o_ref):
        i = pl.program_id(0)
        cz = gc_ref[...].astype(F32) * z_ref[...].astype(F32)
        czp = gcp_ref[...].astype(F32) * zp_ref[...].astype(F32) * (i > 0).astype(F32)
        w = w_ref[...]
        y = w[0:1] * _shift_down(cz, 2, czp) + w[1:2] * _shift_down(cz, 1, czp) + w[2:3] * cz
        o_ref[...] = (gb_ref[...].astype(F32) * y).astype(o_ref.dtype)

    def col(k):
        return pl.BlockSpec((tm, cb), lambda i, j: (i, zb + k * nb + j))

    def halo(k):
        return pl.BlockSpec((HALO, cb), lambda i, j: (jnp.maximum(i * hb - 1, 0), zb + k * nb + j))

    return pl.pallas_call(
        body, name="conv_fwd", grid=(t // tm, nb),
        in_specs=[col(0), col(1), col(2), halo(0), halo(2), pl.BlockSpec((3, cb), lambda i, j: (0, j))],
        out_specs=pl.BlockSpec((tm, cb), lambda i, j: (i, j)),
        out_shape=_sds((t, cw), BF16),
        compiler_params=_cparams("parallel", "parallel"),
    )(proj, proj, proj, proj, proj, conv_w)


def _conv_bwd(proj, conv_w, dco, zoff, cw, cb):
    t = proj.shape[0]
    tm = _pick(t, (512, 256, 128))
    zb, nb = zoff // cb, cw // cb
    hb = tm // HALO
    nt = t // tm

    def body(z_ref, gb_ref, gc_ref, zp_ref, gcp_ref, gbn_ref, w_ref, d_ref, dn_ref, dz_ref, dgb_ref, dgc_ref, dw_ref):
        i = pl.program_id(1)
        z, gb, gc = z_ref[...].astype(F32), gb_ref[...].astype(F32), gc_ref[...].astype(F32)
        d = d_ref[...].astype(F32)
        cz = gc * z
        czp = gcp_ref[...].astype(F32) * zp_ref[...].astype(F32) * (i > 0).astype(F32)
        w = w_ref[...]
        cz1 = _shift_down(cz, 1, czp)
        cz2 = _shift_down(cz, 2, czp)
        y = w[0:1] * cz2 + w[1:2] * cz1 + w[2:3] * cz
        dgb_ref[...] = (d * y).astype(dgb_ref.dtype)
        dy = d * gb
        dyn = dn_ref[...].astype(F32) * gbn_ref[...].astype(F32) * (i < nt - 1).astype(F32)
        dcz = w[2:3] * dy + w[1:2] * _shift_up(dy, 1, dyn) + w[0:1] * _shift_up(dy, 2, dyn)
        dgc_ref[...] = (dcz * z).astype(dgc_ref.dtype)
        dz_ref[...] = (dcz * gc).astype(dz_ref.dtype)
        rows = lax.broadcasted_iota(jnp.int32, (8, cb), 0)
        dw = jnp.zeros((8, cb), F32)
        for r, tap in enumerate((cz2, cz1, cz)):
            dw = jnp.where(rows == r, jnp.sum(dy * tap, axis=0, keepdims=True), dw)

        @pl.when(i == 0)
        def _():
            dw_ref[...] = dw

        @pl.when(i > 0)
        def _():
            dw_ref[...] += dw

    def col(k):
        return pl.BlockSpec((tm, cb), lambda j, i: (i, zb + k * nb + j))

    def halo_prev(k):
        return pl.BlockSpec((HALO, cb), lambda j, i: (jnp.maximum(i * hb - 1, 0), zb + k * nb + j))

    own = pl.BlockSpec((tm, cb), lambda j, i: (i, j))
    nxt = lambda i: jnp.minimum((i + 1) * hb, t // HALO - 1)
    return pl.pallas_call(
        body, name="conv_bwd", grid=(nb, nt),
        in_specs=[col(0), col(1), col(2), halo_prev(0), halo_prev(2),
                  pl.BlockSpec((HALO, cb), lambda j, i: (nxt(i), zb + nb + j)),
                  pl.BlockSpec((3, cb), lambda j, i: (0, j)), own,
                  pl.BlockSpec((HALO, cb), lambda j, i: (nxt(i), j))],
        out_specs=[own, own, own, pl.BlockSpec((8, cb), lambda j, i: (0, j))],
        out_shape=[_sds((t, cw), BF16)] * 3 + [_sds((8, cw), F32)],
        compiler_params=_cparams("parallel", "arbitrary"),
    )(proj, proj, proj, proj, proj, proj, conv_w, dco, dco)


def _merge_fwd(proj, ya, yc, goff, cb):
    t, d = ya.shape
    tm = _pick(t, (512, 256, 128))
    gb_, nb = goff // cb, d // cb

    def body(ga_ref, gc_ref, ya_ref, yc_ref, o_ref):
        f = lambda r: r[...].astype(F32)
        o_ref[...] = (_sigmoid(f(ga_ref)) * f(ya_ref) + _sigmoid(f(gc_ref)) * f(yc_ref)).astype(o_ref.dtype)

    own = pl.BlockSpec((tm, cb), lambda i, j: (i, j))
    return pl.pallas_call(
        body, name="merge_fwd", grid=(t // tm, nb),
        in_specs=[pl.BlockSpec((tm, cb), lambda i, j: (i, gb_ + j)),
                  pl.BlockSpec((tm, cb), lambda i, j: (i, gb_ + nb + j)), own, own],
        out_specs=own, out_shape=_sds((t, d), BF16),
        compiler_params=_cparams("parallel", "parallel"),
    )(proj, proj, ya, yc)


def _merge_bwd(proj, ya, yc, dm, goff, cb):
    t, d = ya.shape
    tm = _pick(t, (512, 256, 128))
    gb_, nb = goff // cb, d // cb

    def body(ga_ref, gc_ref, ya_ref, yc_ref, dm_ref, dya_ref, dyc_ref, dga_ref, dgc_ref):
        dmv = dm_ref[...].astype(F32)
        sa = _sigmoid(ga_ref[...].astype(F32))
        sc = _sigmoid(gc_ref[...].astype(F32))
        dya_ref[...] = (dmv * sa).astype(BF16)
        dyc_ref[...] = (dmv * sc).astype(BF16)
        dga_ref[...] = (dmv * ya_ref[...].astype(F32) * sa * (1.0 - sa)).astype(BF16)
        dgc_ref[...] = (dmv * yc_ref[...].astype(F32) * sc * (1.0 - sc)).astype(BF16)

    own = pl.BlockSpec((tm, cb), lambda i, j: (i, j))
    return pl.pallas_call(
        body, name="merge_bwd", grid=(t // tm, nb),
        in_specs=[pl.BlockSpec((tm, cb), lambda i, j: (i, gb_ + j)),
                  pl.BlockSpec((tm, cb), lambda i, j: (i, gb_ + nb + j)), own, own, own],
        out_specs=[own] * 4, out_shape=[_sds((t, d), BF16)] * 4,
        compiler_params=_cparams("parallel", "parallel"),
    )(proj, proj, ya, yc, dm)


def _assemble_dproj(dq, dkc, dkp, dvc, dvp, cos, sin, dz, dgb, dgc, dga, dgg):
    t, aw = dq.shape
    nkv, cw, d = dkc.shape[0], dz.shape[1], dga.shape[1]
    kvw = nkv * HEAD_DIM
    nblk = t // BLOCK
    width = aw + 2 * kvw + 3 * cw + 2 * d

    def body(dq_ref, dkc_ref, dkp_ref, dvc_ref, dvp_ref, c_ref, s_ref, dz_ref, dgb_ref, dgc_ref, dga_ref, dgg_ref,
             o_ref):
        keep = (pl.program_id(0) < nblk - 1).astype(F32)
        c, s = c_ref[...], s_ref[...]
        dk = jnp.concatenate([dkc_ref[h] + dkp_ref[h] * keep for h in range(nkv)], axis=1)
        dv = jnp.concatenate([dvc_ref[h] + dvp_ref[h] * keep for h in range(nkv)], axis=1)
        o_ref[...] = jnp.concatenate(
            [_rope(dq_ref[...].astype(F32), c, -s).astype(BF16), _rope(dk, c, -s).astype(BF16), dv.astype(BF16),
             dz_ref[...], dgb_ref[...], dgc_ref[...], dga_ref[...], dgg_ref[...]], axis=1)

    def cur(w):
        return pl.BlockSpec((BLOCK, w), lambda n: (n, 0))

    head_cur = pl.BlockSpec((nkv, BLOCK, HEAD_DIM), lambda n: (0, n, 0))
    head_nxt = pl.BlockSpec((nkv, BLOCK, HEAD_DIM), lambda n: (0, jnp.minimum(n + 1, nblk - 1), 0))
    return pl.pallas_call(
        body, name="assemble_dproj", grid=(nblk,),
        in_specs=[cur(aw), head_cur, head_nxt, head_cur, head_nxt, cur(LANES), cur(LANES),
                  cur(cw), cur(cw), cur(cw), cur(d), cur(d)],
        out_specs=cur(width), out_shape=_sds((t, width), BF16),
        compiler_params=_cparams("parallel"),
    )(dq, dkc, dkp, dvc, dvp, cos, sin, dz, dgb, dgc, dga, dgg)


def _xattn_probs(qh, kh):
    s = lax.dot_general(qh, kh, (((1,), (1,)), ((), ())), preferred_element_type=F32) * (X_HEAD_DIM ** -0.5)
    e = jnp.exp(s - jnp.max(s, axis=-1, keepdims=True))
    return e * (1.0 / jnp.sum(e, axis=-1, keepdims=True))


def _xattn_fwd(xq, kv):
    t, xw = xq.shape
    mt = kv.shape[0]
    tm = _pick(t, (512, 256, 128))

    def body(q_ref, kv_ref, o_ref):
        outs = []
        for hd in range(xw // X_HEAD_DIM):
            hs = slice(hd * X_HEAD_DIM, (hd + 1) * X_HEAD_DIM)
            vs = slice(xw + hd * X_HEAD_DIM, xw + (hd + 1) * X_HEAD_DIM)
            p = _xattn_probs(q_ref[:, hs], kv_ref[:, hs])
            outs.append(jnp.dot(p.astype(BF16), kv_ref[:, vs], preferred_element_type=F32))
        o_ref[...] = jnp.concatenate(outs, axis=1).astype(o_ref.dtype)

    return pl.pallas_call(
        body, name="xattn_fwd", grid=(t // tm,),
        in_specs=[pl.BlockSpec((tm, xw), lambda i: (i, 0)), pl.BlockSpec((mt, 2 * xw), lambda i: (0, 0))],
        out_specs=pl.BlockSpec((tm, xw), lambda i: (i, 0)), out_shape=_sds((t, xw), BF16),
        compiler_params=_cparams("parallel"),
    )(xq, kv)


def _xattn_bwd(xq, kv, do):
    t, xw = xq.shape
    mt = kv.shape[0]
    tm = _pick(t, (512, 256, 128))
    scale = X_HEAD_DIM ** -0.5

    def body(q_ref, kv_ref, do_ref, dq_ref, dkv_ref):
        dqs, dks, dvs = [], [], []
        for hd in range(xw // X_HEAD_DIM):
            hs = slice(hd * X_HEAD_DIM, (hd + 1) * X_HEAD_DIM)
            vs = slice(xw + hd * X_HEAD_DIM, xw + (hd + 1) * X_HEAD_DIM)
            qh, kh, vh, doh = q_ref[:, hs], kv_ref[:, hs], kv_ref[:, vs], do_ref[:, hs]
            p = _xattn_probs(qh, kh)
            pb = p.astype(BF16)
            o = jnp.dot(pb, vh, preferred_element_type=F32)
            delta = jnp.sum(doh.astype(F32) * o, axis=-1, keepdims=True)
            dvs.append(lax.dot_general(pb, doh, (((0,), (0,)), ((), ())), preferred_element_type=F32))
            dp = lax.dot_general(doh, vh, (((1,), (1,)), ((), ())), preferred_element_type=F32)
            dsc = (p * (dp - delta)).astype(BF16)
            dqs.append(jnp.dot(dsc, kh, preferred_element_type=F32) * scale)
            dks.append(lax.dot_general(dsc, qh, (((0,), (0,)), ((), ())), preferred_element_type=F32) * scale)
        dq_ref[...] = jnp.concatenate(dqs, axis=1).astype(dq_ref.dtype)
        dkv = jnp.concatenate(dks + dvs, axis=1)

        @pl.when(pl.program_id(0) == 0)
        def _():
            dkv_ref[...] = dkv

        @pl.when(pl.program_id(0) > 0)
        def _():
            dkv_ref[...] += dkv

    row = pl.BlockSpec((tm, xw), lambda i: (i, 0))
    whole = pl.BlockSpec((mt, 2 * xw), lambda i: (0, 0))
    return pl.pallas_call(
        body, name="xattn_bwd", grid=(t // tm,),
        in_specs=[row, whole, row], out_specs=[row, whole],
        out_shape=[_sds((t, xw), BF16), _sds((mt, 2 * xw), F32)],
        compiler_params=_cparams("arbitrary"),
    )(xq, kv, do)


def _swiglu_fwd(hid):
    t, f2 = hid.shape
    f = f2 // 2
    tm = _pick(t, (256, 128))

    def body(h_ref, o_ref):
        a = h_ref[:, :f].astype(F32)
        b = h_ref[:, f:].astype(F32)
        o_ref[...] = (a * _sigmoid(a) * b).astype(o_ref.dtype)

    return pl.pallas_call(
        body, name="swiglu_fwd", grid=(t // tm,),
        in_specs=[pl.BlockSpec((tm, f2), lambda i: (i, 0))],
        out_specs=pl.BlockSpec((tm, f), lambda i: (i, 0)), out_shape=_sds((t, f), BF16),
        compiler_params=_cparams("parallel"),
    )(hid)


def _swiglu_bwd(hid, dact):
    t, f2 = hid.shape
    f = f2 // 2
    tm = _pick(t, (128,))

    def body(h_ref, d_ref, o_ref):
        a = h_ref[:, :f].astype(F32)
        b = h_ref[:, f:].astype(F32)
        d = d_ref[...].astype(F32)
        sg = _sigmoid(a)
        o_ref[:, :f] = (d * b * sg * (1.0 + a * (1.0 - sg))).astype(o_ref.dtype)
        o_ref[:, f:] = (d * a * sg).astype(o_ref.dtype)

    return pl.pallas_call(
        body, name="swiglu_bwd", grid=(t // tm,),
        in_specs=[pl.BlockSpec((tm, f2), lambda i: (i, 0)), pl.BlockSpec((tm, f), lambda i: (i, 0))],
        out_specs=pl.BlockSpec((tm, f2), lambda i: (i, 0)), out_shape=_sds((t, f2), BF16),
        compiler_params=_cparams("parallel"),
    )(hid, dact)


def _adamw(w, g, m, v, name):
    r, c = w.shape
    tr = _pick(r, (256, 128, 64, 32, 16, 8)) if r * c > 65536 else r

    def body(w_ref, g_ref, m_ref, v_ref, d_ref, nm_ref, nv_ref, go_ref):
        gv = g_ref[...]
        go_ref[...] = gv
        m2 = ADAM_B1 * m_ref[...] + (1.0 - ADAM_B1) * gv
        v2 = ADAM_B2 * v_ref[...] + (1.0 - ADAM_B2) * (gv * gv)
        m_hat = m2 / (1.0 - ADAM_B1 ** ADAM_STEP)
        v_hat = v2 / (1.0 - ADAM_B2 ** ADAM_STEP)
        d_ref[...] = -ADAM_LR * (m_hat / (jnp.sqrt(v_hat) + ADAM_EPS) + ADAM_WD * w_ref[...])
        nm_ref[...] = m2
        nv_ref[...] = v2

    blk = pl.BlockSpec((tr, c), lambda i: (i, 0))
    return pl.pallas_call(
        body, name=name, grid=(r // tr,),
        in_specs=[blk] * 4, out_specs=[blk] * 4, out_shape=[_sds((r, c), F32)] * 4,
        compiler_params=_cparams("parallel"),
    )(w, g, m, v)


class _Weights:
    def __init__(self, full):
        self.full = full
        self.grads = {}

    def get(self, name):
        return self.full[name]

    def mark(self, tag, value):
        return value

    def grad(self, name, g):
        self.grads[name] = g

    def dep(self):
        return None


def _local_step(x, mem, tgt, g_mix, sinks, g_xattn, g_mem, g_ffn, g_final, dims, wts):
    t, d = x.shape
    aw, cw, kvw = dims
    cb = 2 * kvw
    zoff = aw + 2 * kvw
    goff = zoff + 3 * cw
    cos, sin = _rope_tables(t)
    mark, get = wts.mark, wts.get

    def mm(a, b, **kw):
        return _matmul(a, b, dep=wts.dep(), **kw)

    u1 = mark("u1", _rms_fwd(x, g_mix, "rms_mix"))
    qkv = mm(u1, get("w_in"), mode="nn", out_dtype=F32, name="mm_qkv", cols=(0, zoff))
    proj = mark("proj", mm(u1, get("w_in"), mode="nn", out_dtype=BF16, name="mm_gates", cols=(zoff, goff - zoff + 2 * d)))
    zoff, goff = 0, goff - zoff
    q_r, k_r, v_h = _rope_qkv(qkv, cos, sin, aw, kvw)
    attn_o = mark("attn_o", _attn_fwd(q_r, k_r, v_h, sinks))
    conv_o = mark("conv_o", _conv_fwd(proj, get("conv_w"), zoff, cw, cb))
    ya = mark("ya", mm(attn_o, get("w_attn_proj"), mode="nn", out_dtype=BF16, name="mm_yattn"))
    yc = mark("yc", mm(conv_o, get("w_conv_proj"), mode="nn", out_dtype=BF16, name="mm_yconv"))
    merged = mark("merged", _merge_fwd(proj, ya, yc, goff, cb))
    h1 = mark("h1", mm(merged, get("w_mix_out"), mode="nn", out_dtype=F32, name="mm_mix", res=x))
    u2 = mark("u2", _rms_fwd(h1, g_xattn, "rms_xattn"))
    mem_n = _rms_fwd(mem, g_mem, "rms_mem")
    xq = mark("xq", mm(u2, get("w_xq"), mode="nn", out_dtype=BF16, name="mm_xq"))
    kv = mm(mem_n, get("w_xkv"), mode="nn", out_dtype=BF16, name="mm_xkv")
    xo = mark("xo", _xattn_fwd(xq, kv))
    h2 = mark("h2", mm(xo, get("w_xo"), mode="nn", out_dtype=F32, name="mm_xo", res=h1))
    u3 = mark("u3", _rms_fwd(h2, g_ffn, "rms_ffn"))
    hid = mark("hid", mm(u3, get("w_ffn_in"), mode="nn", out_dtype=BF16, name="mm_ffn_in"))
    act = mark("act", _swiglu_fwd(hid))
    h3 = mm(act, get("w_ffn_out"), mode="nn", out_dtype=F32, name="mm_ffn_out", res=h2)
    dh3, dh3b, dg_final, loss = _loss_head(h3, g_final, tgt)
    mark("dh3", dh3b)

    wts.grad("w_ffn_out", mm(act, dh3b, mode="tn", out_dtype=BF16, name="mm_dw_ffn_out"))
    dact = mark("dact", mm(dh3b, get("w_ffn_out"), mode="nt", out_dtype=BF16, name="mm_dact"))
    dhid = mark("dhid", _swiglu_bwd(hid, dact))
    wts.grad("w_ffn_in", mm(u3, dhid, mode="tn", out_dtype=BF16, name="mm_dw_ffn_in"))
    du3 = mark("du3", mm(dhid, get("w_ffn_in"), mode="nt", out_dtype=F32, name="mm_du3"))
    dh2, dh2b, dg_ffn = _rms_bwd(h2, g_ffn, du3, dh3, "rms_bwd_ffn")
    mark("dh2", dh2b)
    wts.grad("w_xo", mm(xo, dh2b, mode="tn", out_dtype=BF16, name="mm_dw_xo"))
    dxo = mm(dh2b, get("w_xo"), mode="nt", out_dtype=BF16, name="mm_dxo")
    dxq, dkv = _xattn_bwd(xq, kv, dxo)
    dkvb = dkv.astype(BF16)
    wts.grad("w_xq", mm(u2, dxq, mode="tn", out_dtype=BF16, name="mm_dw_xq"))
    du2 = mm(dxq, get("w_xq"), mode="nt", out_dtype=F32, name="mm_du2")
    wts.grad("w_xkv", mm(mem_n, dkvb, mode="tn", out_dtype=BF16, name="mm_dw_xkv"))
    dmem_n = mm(dkvb, get("w_xkv"), mode="nt", out_dtype=F32, name="mm_dmem")
    _, _, dg_mem = _rms_bwd(mem, g_mem, dmem_n, None, "rms_bwd_mem")
    dh1, dh1b, dg_xattn = _rms_bwd(h1, g_xattn, du2, dh2, "rms_bwd_xattn")
    mark("dh1", dh1b)
    wts.grad("w_mix_out", mm(merged, dh1b, mode="tn", out_dtype=BF16, name="mm_dw_mix"))
    dmerged = mm(dh1b, get("w_mix_out"), mode="nt", out_dtype=BF16, name="mm_dmerged")
    dya, dyc, dga, dgg = _merge_bwd(proj, ya, yc, dmerged, goff, cb)
    mark("dya", dya)
    wts.grad("w_attn_proj", mm(attn_o, dya, mode="tn", out_dtype=BF16, name="mm_dw_attn_proj"))
    dattn_o = mm(dya, get("w_attn_proj"), mode="nt", out_dtype=BF16, name="mm_dattn")
    wts.grad("w_conv_proj", mm(conv_o, dyc, mode="tn", out_dtype=BF16, name="mm_dw_conv_proj"))
    dconv_o = mark("dconv_o", mm(dyc, get("w_conv_proj"), mode="nt", out_dtype=BF16, name="mm_dconv"))
    dz, dgb, dgc, dconv_w = _conv_bwd(proj, get("conv_w"), dconv_o, zoff, cw, cb)
    mark("dz", dz)
    dq, dkc, dkp, dvc, dvp, dsinks = _attn_bwd(q_r, k_r, v_h, sinks, attn_o, dattn_o)
    mark("dq", dq)
    dproj = mark("dproj", _assemble_dproj(dq, dkc, dkp, dvc, dvp, cos, sin, dz, dgb, dgc, dga, dgg))
    wts.grad("w_in", mm(u1, dproj, mode="tn", out_dtype=BF16, name="mm_dw_in"))
    du1 = mark("du1", mm(dproj, get("w_in"), mode="nt", out_dtype=F32, name="mm_du1"))
    grad_x, _, dg_mix = _rms_bwd(x, g_mix, du1, dh1, "rms_bwd_mix")
    mark("grad_x", grad_x)

    small = dict(g_mix=dg_mix, g_xattn=dg_xattn, g_mem=dg_mem, g_ffn=dg_ffn, g_final=dg_final,
                 conv_w=dconv_w[:3], attn_sinks=dsinks[0:1, :sinks.shape[1]], loss=loss[0:1, 0:1])
    return grad_x, small


BIG = (("w_in", 1), ("w_attn_proj", 1), ("w_conv_proj", 1), ("w_mix_out", 0), ("w_xq", 0), ("w_xkv", 0),
       ("w_xo", 1), ("w_ffn_in", 1), ("w_ffn_out", 0))


def _place():
    x, y, c = lax.axis_index("x"), lax.axis_index("y"), lax.axis_index("c")
    chips = [(1 - x, y), (x, 1 - y), (1 - x, 1 - y)]
    return x, y, c, chips


def _window(ref, ax, shard_shape, s, h):
    sr, sc = shard_shape
    hr = sr // 2
    if ax == 1:
        return ref.at[pl.ds(pl.multiple_of(h * hr, 16), hr), pl.ds(pl.multiple_of(s * sc, LANES), sc)]
    return ref.at[pl.ds(pl.multiple_of(s * sr + h * hr, 16), hr), :]


def _half(ref, h):
    hr = ref.shape[0] // 2
    return ref.at[pl.ds(pl.multiple_of(h * hr, 16), hr), :]


def _remote(src, dst, send_sem, recv_sem, dev):
    return pltpu.make_async_remote_copy(src_ref=src, dst_ref=dst, send_sem=send_sem, recv_sem=recv_sem,
                                        device_id=dev, device_id_type=MESH)


def _cast_to_full(shard, ax, me, name, dtype=BF16):
    sr, sc = shard.shape
    tr = _pick(sr, (256, 352, 128, 64, 32, 16))
    nr = sr // tr
    full = (sr * N_CHIPS, sc) if ax == 0 else (sr, sc * N_CHIPS)

    def body(me_ref, s_ref, o_ref):
        o_ref[...] = s_ref[...].astype(o_ref.dtype)

    if ax == 1:
        out_spec = pl.BlockSpec((tr, sc), lambda r, me_ref: (r, me_ref[0]))
    else:
        out_spec = pl.BlockSpec((tr, sc), lambda r, me_ref: (me_ref[0] * nr + r, 0))
    return pl.pallas_call(
        body, name=name,
        grid_spec=pltpu.PrefetchScalarGridSpec(
            num_scalar_prefetch=1, grid=(nr,), in_specs=[pl.BlockSpec((tr, sc), lambda r, me_ref: (r, 0))],
            out_specs=out_spec),
        out_shape=_sds(full, dtype),
        compiler_params=_cparams("parallel"),
    )(me, shard)


HBM = pl.BlockSpec(memory_space=pltpu.HBM)
SEM = pl.BlockSpec(memory_space=pltpu.SEMAPHORE)
EFFECT = pltpu.SideEffectType.DATAFLOW_SIDE_EFFECTING


def _in_hbm(a):
    return pltpu.with_memory_space_constraint(a, pltpu.HBM)


def _gather_window(ref, ax, shard_shape, s, h):
    if h is not None:
        return _window(ref, ax, shard_shape, s, h)
    sr, sc = shard_shape
    if ax == 1:
        return ref.at[:, pl.ds(pl.multiple_of(s * sc, LANES), sc)]
    return ref.at[pl.ds(pl.multiple_of(s * sr, 8), sr), :]


def _ag_start(fulls, axes, shard_shapes, whole):
    n = len(fulls)

    def body(*refs):
        src = refs[:n]
        send_sems, recv_sems = refs[n], refs[n + 1]
        token = refs[2 * n + 2]
        x, y, c, chips = _place()
        me = 2 * x + y
        for i in range(n):
            h = None if whole[i] else c
            for j, chip in enumerate(chips):
                blk = _gather_window(src[i], axes[i], shard_shapes[i], me, h)
                _remote(blk, blk, send_sems.at[3 * i + j], recv_sems.at[3 * i + j], (*chip, c)).start()
        token[...] = jnp.zeros_like(token)

    res = pl.pallas_call(
        body, name="ag_start_" + str(n),
        out_shape=(pltpu.SemaphoreType.DMA((3 * n,)), pltpu.SemaphoreType.DMA((3 * n,)),
                   *[pltpu.HBM(f.shape, f.dtype) for f in fulls], _sds((8, LANES), F32)),
        in_specs=[HBM] * n, out_specs=(SEM, SEM, *[HBM] * n, pl.BlockSpec(memory_space=pltpu.VMEM)),
        input_output_aliases={i: 2 + i for i in range(n)},
        compiler_params=pltpu.CompilerParams(has_side_effects=EFFECT),
    )(*[_in_hbm(f) for f in fulls])
    return res[0], res[1], list(res[2:2 + n]), res[2 + n]


def _ag_mid(bufs, slots, axes, shard_shapes, whole, send_sems, recv_sems, after, name):
    ng = len(bufs)

    def body(*refs):
        src = refs[:ng]
        s_in, r_in = refs[ng], refs[ng + 1]
        fsend, frecv = refs[ng + 3], refs[ng + 4]
        x, y, c, chips = _place()
        me = 2 * x + y
        sib = (x, y, 1 - c)
        for k, i in enumerate(slots):
            h = None if whole[k] else c
            for j, chip in enumerate(chips):
                cj = 2 * chip[0] + chip[1]
                mine = _gather_window(src[k], axes[k], shard_shapes[k], me, h)
                theirs = _gather_window(src[k], axes[k], shard_shapes[k], cj, h)
                _remote(theirs, theirs, s_in.at[3 * i + j], r_in.at[3 * i + j], (*chip, c)).wait_recv()
                _remote(mine, mine, s_in.at[3 * i + j], r_in.at[3 * i + j], (*chip, c)).wait_send()
                if not whole[k]:
                    _remote(theirs, theirs, fsend.at[3 * k + j], frecv.at[3 * k + j], sib).start()
        token = refs[2 * ng + 5]
        token[...] = jnp.zeros_like(token)

    res = pl.pallas_call(
        body, name=name,
        out_shape=(pltpu.SemaphoreType.DMA((3 * ng,)), pltpu.SemaphoreType.DMA((3 * ng,)),
                   *[pltpu.HBM(b.shape, b.dtype) for b in bufs], _sds((8, LANES), F32)),
        in_specs=[HBM] * ng + [SEM, SEM, ANY],
        out_specs=(SEM, SEM, *[HBM] * ng, pl.BlockSpec(memory_space=pltpu.VMEM)),
        input_output_aliases={k: 2 + k for k in range(ng)},
        compiler_params=pltpu.CompilerParams(has_side_effects=EFFECT),
    )(*bufs, send_sems, recv_sems, after)
    return res[0], res[1], list(res[2:2 + ng]), res[2 + ng]


def _ag_wait(bufs, axes, shard_shapes, whole, fsend, frecv, after, name):
    ng = len(bufs)

    def body(*refs):
        src = refs[:ng]
        s_in, r_in = refs[ng], refs[ng + 1]
        x, y, c, chips = _place()
        sib = (x, y, 1 - c)
        for k in range(ng):
            if whole[k]:
                continue
            for j, chip in enumerate(chips):
                cj = 2 * chip[0] + chip[1]
                sent = _gather_window(src[k], axes[k], shard_shapes[k], cj, c)
                landed = _gather_window(src[k], axes[k], shard_shapes[k], cj, 1 - c)
                _remote(landed, landed, s_in.at[3 * k + j], r_in.at[3 * k + j], sib).wait_recv()
                _remote(sent, sent, s_in.at[3 * k + j], r_in.at[3 * k + j], sib).wait_send()

    res = pl.pallas_call(
        body, name=name,
        out_shape=tuple(pltpu.HBM(b.shape, b.dtype) for b in bufs),
        in_specs=[HBM] * ng + [SEM, SEM, ANY], out_specs=tuple([HBM] * ng),
        input_output_aliases={k: k for k in range(ng)},
        compiler_params=pltpu.CompilerParams(has_side_effects=EFFECT),
    )(*bufs, fsend, frecv, after)
    return list(res)


class _Schedule:
    GROUPS = ((("w_in", "conv_w"), "u1", "u1"),
              (("w_attn_proj", "w_conv_proj", "w_mix_out", "w_xq", "w_xkv", "w_xo"), "proj", "conv_o"),
              (("w_ffn_in",), "h1", "h2"),
              (("w_ffn_out",), "u3", "hid"))
    STARTS = ((0,), (1, 2, 3))
    REDUCE = ((("w_ffn_out",), "dact", "grad:w_ffn_in", "du3"),
              (("w_ffn_in",), "grad:w_ffn_in", "dya", "grad_x"),
              (("w_xo", "w_xq", "w_xkv", "w_mix_out", "w_attn_proj", "w_conv_proj"), "dconv_o", "dq", "grad_x"),
              (("w_in",), "grad:w_in", "end", "end2"))

    def __init__(self, seed, axes, shard_shapes, place, on_ready):
        self.ax, self.shape, self.place, self.on_ready = axes, shard_shapes, place, on_ready
        self.stage, self.buf, self.slot, self.passes = {}, {}, {}, {}
        self.ready = set()
        self.grads = {}
        token = None
        for groups in self.STARTS:
            order = [nm for g in groups for nm in self.GROUPS[g][0]]
            send, recv, bufs, token = _ag_start([seed(nm, token) for nm in order], *self._meta(order))
            self.buf.update(zip(order, bufs))
            self.slot.update({nm: (send, recv, k) for k, nm in enumerate(order)})
        self.token = self.latest = token
        self.mark("start", token)

    def _meta(self, names):
        return ([self.ax[nm] for nm in names], [self.shape[nm] for nm in names], [nm == "conv_w" for nm in names])

    def mark(self, tag, value):
        for g, (names, mid, wait) in enumerate(self.GROUPS):
            if tag == mid:
                send, recv, _ = self.slot[names[0]]
                fs, fr, bufs, self.latest = _ag_mid([self.buf[nm] for nm in names], [self.slot[nm][2] for nm in names],
                                                    *self._meta(names), send, recv, value, "ag_mid_%d" % g)
                self.buf.update(zip(names, bufs))
                self.passes[g] = (fs, fr)
            if tag == wait:
                fs, fr = self.passes[g]
                bufs = _ag_wait([self.buf[nm] for nm in names], *self._meta(names), fs, fr, value, "ag_wait_%d" % g)
                self.buf.update(zip(names, bufs))
                self.ready.update(names)
        for g, (names, send, total, finish) in enumerate(self.REDUCE):
            st = self.stage.get(g)
            if st is None:
                continue
            ng = len(names)
            if tag == send and st["at"] == "pair":
                arrs = _exchange_wait("rs_pair_wait_%d" % g, st["arrs"], *st["sems"], st["plan"], value)
                parts = [_pair_add(arrs[k], arrs[ng + k], self.ax[nm], self.shape[nm], self.place, "pair_add_" + nm)
                         for k, nm in enumerate(names)]
                plan, nsem = _plan_chip(ng)
                ss, rs, arrs, self.latest = _exchange_start(
                    "rs_chip_start_%d" % g, parts + [lax.empty(p.shape, p.dtype) for p in parts], nsem, plan)
                self.stage[g] = dict(at="chip", arrs=arrs, sems=(ss, rs), plan=plan)
            elif tag == total and st["at"] == "chip":
                arrs = _exchange_wait("rs_chip_wait_%d" % g, st["arrs"], *st["sems"], st["plan"], value)
                halves = [_chip_add(arrs[k], arrs[ng + k], self.place, "chip_add_" + nm) for k, nm in enumerate(names)]
                plan, nsem = _plan_gather(ng)
                ss, rs, arrs, self.latest = _exchange_start("rs_gather_start_%d" % g, halves, nsem, plan)
                self.stage[g] = dict(at="gather", arrs=arrs, sems=(ss, rs), plan=plan)
            elif tag == finish and st["at"] == "gather":
                arrs = _exchange_wait("rs_gather_wait_%d" % g, st["arrs"], *st["sems"], st["plan"], value)
                self.stage[g] = dict(at="done")
                for nm, shard in zip(names, arrs):
                    self.on_ready(nm, shard)
        return value

    def get(self, name):
        assert name in self.ready, name
        return self.buf[name]

    def grad(self, name, g):
        self.grads[name] = g
        for gi, (names, _, _, _) in enumerate(self.REDUCE):
            if name == names[-1]:
                gs = [self.grads[nm] for nm in names]
                plan, nsem = _plan_pair(len(names), [self.ax[nm] for nm in names], [self.shape[nm] for nm in names])
                ss, rs, arrs, self.latest = _exchange_start(
                    "rs_pair_start_%d" % gi, gs + [lax.empty(a.shape, a.dtype) for a in gs], nsem, plan)
                self.stage[gi] = dict(at="pair", arrs=arrs, sems=(ss, rs), plan=plan)
                g = self.latest
        self.mark("grad:" + name, g)

    def dep(self):
        return self.latest


def _exchange_start(name, arrays, nsem, plan):
    n = len(arrays)

    def body(*refs):
        send_sems, recv_sems, token = refs[n], refs[n + 1], refs[2 * n + 2]
        sends, _ = plan(refs[:n])
        for k, (src, dst, dev) in enumerate(sends):
            _remote(src, dst, send_sems.at[k], recv_sems.at[k], dev).start()
        token[...] = jnp.zeros_like(token)

    res = pl.pallas_call(
        body, name=name,
        out_shape=(pltpu.SemaphoreType.DMA((nsem,)), pltpu.SemaphoreType.DMA((nsem,)),
                   *[pltpu.HBM(a.shape, a.dtype) for a in arrays], _sds((8, LANES), F32)),
        in_specs=[HBM] * n, out_specs=(SEM, SEM, *[HBM] * n, pl.BlockSpec(memory_space=pltpu.VMEM)),
        input_output_aliases={i: 2 + i for i in range(n)},
        compiler_params=pltpu.CompilerParams(has_side_effects=EFFECT),
    )(*[_in_hbm(a) for a in arrays])
    return res[0], res[1], list(res[2:2 + n]), res[2 + n]


def _exchange_wait(name, arrays, send_sems, recv_sems, plan, after):
    n = len(arrays)

    def body(*refs):
        s_in, r_in = refs[n], refs[n + 1]
        sends, recvs = plan(refs[:n])
        for k, land in enumerate(recvs):
            _remote(land, land, s_in.at[k], r_in.at[k], sends[k][2]).wait_recv()
        for k, (src, _, dev) in enumerate(sends):
            _remote(src, src, s_in.at[k], r_in.at[k], dev).wait_send()

    res = pl.pallas_call(
        body, name=name,
        out_shape=tuple(pltpu.HBM(a.shape, a.dtype) for a in arrays),
        in_specs=[HBM] * n + [SEM, SEM, ANY], out_specs=tuple([HBM] * n),
        input_output_aliases={i: i for i in range(n)},
        compiler_params=pltpu.CompilerParams(has_side_effects=EFFECT),
    )(*arrays, send_sems, recv_sems, after)
    return list(res)


def _plan_pair(n, axes, shard_shapes):
    def plan(refs):
        g, ra = refs[:n], refs[n:]
        x, y, c, _ = _place()
        sib = (x, y, 1 - c)

        def pieces(ref, i, h):
            if axes[i] == 1:
                return [_half(ref, h)]
            return [_window(ref, 0, shard_shapes[i], s, h) for s in range(N_CHIPS)]

        sends, recvs = [], []
        for i in range(n):
            sends += [(src, dst, sib) for src, dst in zip(pieces(g[i], i, 1 - c), pieces(ra[i], i, 1 - c))]
            recvs += pieces(ra[i], i, c)
        return sends, recvs

    return plan, sum(1 if ax == 1 else N_CHIPS for ax in axes)


def _plan_chip(n):
    def plan(refs):
        p, rc = refs[:n], refs[n:]
        x, y, c, chips = _place()
        me = 2 * x + y
        sends, recvs = [], []
        for i in range(n):
            for chip in chips:
                cj = 2 * chip[0] + chip[1]
                sends.append((p[i].at[cj], rc[i].at[me], (*chip, c)))
                recvs.append(rc[i].at[cj])
        return sends, recvs

    return plan, 3 * n


def _plan_gather(n):
    def plan(refs):
        x, y, c, _ = _place()
        sib = (x, y, 1 - c)
        return ([(_half(r, c), _half(r, c), sib) for r in refs], [_half(r, 1 - c) for r in refs])

    return plan, n


def _pair_add(g, ra, ax, shard_shape, place, name):
    sr, sc = shard_shape
    hr = sr // 2
    wc = sc
    tr = _pick(hr, (256, 352, 128, 64, 32, 16))
    nr = hr // tr

    def body(p_ref, a_ref, b_ref, o_ref):
        o_ref[...] = (a_ref[...].astype(F32) + b_ref[...].astype(F32)).astype(o_ref.dtype)

    if ax == 1:
        src = pl.BlockSpec((tr, wc), lambda s, r, p_ref: (p_ref[0] * nr + r, s))
    else:
        src = pl.BlockSpec((tr, wc), lambda s, r, p_ref: (s * 2 * nr + p_ref[0] * nr + r, 0))
    return pl.pallas_call(
        body, name=name,
        grid_spec=pltpu.PrefetchScalarGridSpec(
            num_scalar_prefetch=1, grid=(N_CHIPS, nr), in_specs=[src, src],
            out_specs=pl.BlockSpec((None, tr, wc), lambda s, r, p_ref: (s, r, 0))),
        out_shape=_sds((N_CHIPS, hr, wc), BF16),
        compiler_params=_cparams("parallel", "parallel"),
    )(place, g, ra)


def _chip_add(part, rc, place, name):
    _, hr, wc = rc.shape
    tr = _pick(hr, (256, 352, 128, 64, 32, 16))
    nr = hr // tr

    def body(p_ref, own_ref, r1_ref, r2_ref, r3_ref, o_ref):
        acc = own_ref[...].astype(F32)
        for r_ref in (r1_ref, r2_ref, r3_ref):
            acc = acc + r_ref[...].astype(F32)
        o_ref[...] = acc

    def slot(k):
        return pl.BlockSpec((None, tr, wc), lambda r, p_ref: ((p_ref[1] + k) % N_CHIPS, r, 0))

    return pl.pallas_call(
        body, name=name,
        grid_spec=pltpu.PrefetchScalarGridSpec(
            num_scalar_prefetch=1, grid=(nr,), in_specs=[slot(0), slot(1), slot(2), slot(3)],
            out_specs=pl.BlockSpec((tr, wc), lambda r, p_ref: (p_ref[0] * nr + r, 0))),
        out_shape=_sds((2 * hr, wc), F32),
        compiler_params=_cparams("parallel"),
    )(place, part, rc, rc, rc)


N_DEV = 8


def _all_reduce_small(buf):
    r, cdim = buf.shape

    def body(x_ref, o_ref, land, send_sems, recv_sems):
        x, y, c, _ = _place()
        me = 4 * x + 2 * y + c
        land[me] = x_ref[...]
        sends = []
        for k in range(1, N_DEV):
            kx, ky, kc = (k >> 2) & 1, (k >> 1) & 1, k & 1
            peer = (1 - x if kx else x, 1 - y if ky else y, 1 - c if kc else c)
            cp = _remote(x_ref, land.at[me], send_sems.at[k - 1], recv_sems.at[k - 1], peer)
            cp.start()
            sends.append(cp)
        for k in range(1, N_DEV):
            kx, ky, kc = (k >> 2) & 1, (k >> 1) & 1, k & 1
            peer = (1 - x if kx else x, 1 - y if ky else y, 1 - c if kc else c)
            pidx = 4 * peer[0] + 2 * peer[1] + peer[2]
            _remote(land.at[pidx], land.at[pidx], send_sems.at[k - 1], recv_sems.at[k - 1], peer).wait_recv()
        for cp in sends:
            cp.wait_send()
        acc = land[0]
        for dev in range(1, N_DEV):
            acc = acc + land[dev]
        o_ref[...] = acc

    vm = pl.BlockSpec(memory_space=pltpu.VMEM)
    return pl.pallas_call(
        body, name="all_reduce_small", in_specs=[vm], out_specs=vm, out_shape=_sds((r, cdim), F32),
        scratch_shapes=[pltpu.VMEM((N_DEV, r, cdim), F32), pltpu.SemaphoreType.DMA((N_DEV - 1,)),
                        pltpu.SemaphoreType.DMA((N_DEV - 1,))],
    )(buf)


SMALL_ROWS = 16


def kernel(x, mem, g_mix, w_in, conv_w, attn_sinks, w_attn_proj, w_conv_proj, w_mix_out, g_xattn, g_mem, w_xq, w_xkv, w_xo, g_ffn, w_ffn_in, w_ffn_out, g_final, loss_target, m_g_mix, m_w_in, m_conv_w, m_attn_sinks, m_w_attn_proj, m_w_conv_proj, m_w_mix_out, m_g_xattn, m_g_mem, m_w_xq, m_w_xkv, m_w_xo, m_g_ffn, m_w_ffn_in, m_w_ffn_out, m_g_final, v_g_mix, v_w_in, v_conv_w, v_attn_sinks, v_w_attn_proj, v_w_conv_proj, v_w_mix_out, v_g_xattn, v_g_mem, v_w_xq, v_w_xkv, v_w_xo, v_g_ffn, v_w_ffn_in, v_w_ffn_out, v_g_final):
    w = dict(g_mix=g_mix, w_in=w_in[0], conv_w=conv_w[0], attn_sinks=attn_sinks, w_attn_proj=w_attn_proj[0],
             w_conv_proj=w_conv_proj[0], w_mix_out=w_mix_out[0], g_xattn=g_xattn, g_mem=g_mem, w_xq=w_xq[0],
             w_xkv=w_xkv[0], w_xo=w_xo[0], g_ffn=g_ffn, w_ffn_in=w_ffn_in[0], w_ffn_out=w_ffn_out[0],
             g_final=g_final[None])
    m = dict(g_mix=m_g_mix, w_in=m_w_in[0], conv_w=m_conv_w[0], attn_sinks=m_attn_sinks,
             w_attn_proj=m_w_attn_proj[0], w_conv_proj=m_w_conv_proj[0], w_mix_out=m_w_mix_out[0],
             g_xattn=m_g_xattn, g_mem=m_g_mem, w_xq=m_w_xq[0], w_xkv=m_w_xkv[0], w_xo=m_w_xo[0], g_ffn=m_g_ffn,
             w_ffn_in=m_w_ffn_in[0], w_ffn_out=m_w_ffn_out[0], g_final=m_g_final[None])
    v = dict(g_mix=v_g_mix, w_in=v_w_in[0], conv_w=v_conv_w[0], attn_sinks=v_attn_sinks,
             w_attn_proj=v_w_attn_proj[0], w_conv_proj=v_w_conv_proj[0], w_mix_out=v_w_mix_out[0],
             g_xattn=v_g_xattn, g_mem=v_g_mem, w_xq=v_w_xq[0], w_xkv=v_w_xkv[0], w_xo=v_w_xo[0], g_ffn=v_g_ffn,
             w_ffn_in=v_w_ffn_in[0], w_ffn_out=v_w_ffn_out[0], g_final=v_g_final[None])
    names = [nm for nm, _ in BIG]
    axes = [ax for _, ax in BIG]
    d = x.shape[2]
    cw = w["conv_w"].shape[1] * N_CHIPS
    chip = (2 * lax.axis_index("x") + lax.axis_index("y")).astype(jnp.int32)
    place = jnp.stack([lax.axis_index("c").astype(jnp.int32), chip])
    shard_shapes = [w[nm].shape for nm in names]

    def seed(nm, token):
        me1 = chip.reshape(1)
        if token is not None:
            me1 = me1 + token[0, 0].astype(jnp.int32)
        if nm == "conv_w":
            return _cast_to_full(w[nm], 1, me1, "place_conv_w", F32)
        return _cast_to_full(w[nm], dict(BIG)[nm], me1, "cast_" + nm)

    upd = {}

    def on_ready(nm, shard):
        upd[nm] = _adamw(w[nm], shard, m[nm], v[nm], "adamw_" + nm)
        grads[nm] = upd[nm][3]

    grads = {}
    wts = _Schedule(seed, dict(zip(names + ["conv_w"], axes + [1])),
                    dict(zip(names + ["conv_w"], shard_shapes + [w["conv_w"].shape])), place, on_ready)
    aw, cw = w["w_attn_proj"].shape[0], w["w_conv_proj"].shape[0]
    kvw = (w["w_in"].shape[1] * N_CHIPS - aw - 3 * cw - 2 * d) // 2
    grad_x, small = _local_step(
        x[0], mem[0], loss_target[0], w["g_mix"] + wts.token[0:1, 0:1], w["attn_sinks"], w["g_xattn"], w["g_mem"],
        w["g_ffn"], w["g_final"], (aw, cw, kvw), wts)

    pw = max(d, cw)

    def row(a):
        return jnp.pad(a, ((0, 0), (0, pw - a.shape[1])))

    gains = ("g_mix", "g_xattn", "g_mem", "g_ffn", "g_final")
    packed = jnp.concatenate(
        [row(small[nm]) for nm in gains] + [row(small["conv_w"]),
         row(jnp.concatenate([small["attn_sinks"], small["loss"]], axis=1)),
         jnp.zeros((SMALL_ROWS - 9, pw), F32)], axis=0)
    total = _all_reduce_small(packed)
    wts.mark("end", total)
    nsink = attn_sinks.shape[1]
    grads.update({nm: total[k:k + 1, :d] for k, nm in enumerate(gains)})
    grads.update(conv_w=lax.dynamic_slice(total, (5, chip * (cw // N_CHIPS)), (3, cw // N_CHIPS)),
                 attn_sinks=total[8:9, :nsink])
    loss = total[8, nsink]
    for nm in gains + ("conv_w", "attn_sinks"):
        upd[nm] = _adamw(w[nm], grads[nm], m[nm], v[nm], "adamw_" + nm)
    wts.mark("end2", upd["g_final"][0])

    order = ["g_mix", "w_in", "conv_w", "attn_sinks", "w_attn_proj", "w_conv_proj", "w_mix_out", "g_xattn", "g_mem",
             "w_xq", "w_xkv", "w_xo", "g_ffn", "w_ffn_in", "w_ffn_out", "g_final"]

    stacked = set(names) | {"conv_w"}

    def shaped(nm, a):
        if nm == "g_final":
            return a[0]
        return a[None] if nm in stacked else a

    outs = [loss, grad_x[None]]
    outs += [shaped(nm, grads[nm]) for nm in order]
    for k in range(3):
        outs += [shaped(nm, upd[nm][k]) for nm in order]
    return tuple(outs)
```

```python
import functools

import jax
import jax.numpy as jnp
from jax import lax
from jax.experimental import pallas as pl
from jax.experimental.pallas import tpu as pltpu

F32 = jnp.float32
BF16 = jnp.bfloat16

VMEM_LIMIT_BYTES = 56 * 1024 * 1024
LANES = 128
HEAD_DIM = 64
BLOCK = 128
X_HEAD_DIM = 128
ROPE_THETA = 10000.0
EPS = 1e-6
NEG = -1e30
ADAM_LR, ADAM_B1, ADAM_B2, ADAM_EPS, ADAM_WD, ADAM_STEP = 0.001, 0.9, 0.999, 1e-08, 0.01, 10
N_CHIPS = 4
MESH = pl.DeviceIdType.MESH
ANY = pl.BlockSpec(memory_space=pl.ANY)


def _pick(dim, prefs):
    for p in prefs:
        if dim % p == 0:
            return p
    return dim


def _cparams(*sem):
    return pltpu.CompilerParams(dimension_semantics=sem, vmem_limit_bytes=VMEM_LIMIT_BYTES)


def _sds(shape, dtype):
    return jax.ShapeDtypeStruct(shape, dtype)


def _sigmoid(v):
    return 1.0 / (1.0 + jnp.exp(-v))


MATMUL_VMEM_BUDGET = 46 * 1024 * 1024


def _tiles(mode, m, n):
    if mode == "tn" and m % 1024 != 0:
        return _pick(m, (512, 256, 128)), _pick(n, (1024, 512, 256, 128)), True
    return _pick(m, (1024, 512, 256, 128)), _pick(n, (512, 256, 128)), False


def _k_parts(m, n, k):
    tm, tn = _pick(m, (1024, 512, 256, 128)), _pick(n, (512, 256, 128))
    for parts in range(1, k // LANES + 1):
        if k % (parts * LANES) == 0 and 4 * (tm + tn) * (k // parts) + 16 * tm * tn <= MATMUL_VMEM_BUDGET:
            return parts
    return k // LANES


def _matmul(a, b, *, mode, out_dtype, name, res=None, dep=None, cols=None):
    if mode == "nn":
        (m, k), (k2, n) = a.shape, b.shape
    elif mode == "nt":
        (m, k), (n, k2) = (a.shape[-2], a.shape[-1] * (a.shape[0] if a.ndim == 3 else 1)), b.shape
    else:
        (k, m), (k2, n) = a.shape, (b.shape[-2], b.shape[-1] * (b.shape[0] if b.ndim == 3 else 1))
    assert k == k2, (a.shape, b.shape, mode)
    assert cols is None or mode != "nt"
    cols = cols or (0, n)
    parts = 1 if mode == "tn" else (a.shape[0] if a.ndim == 3 else _k_parts(m, cols[1], k))
    for p in range(parts):
        last = p == parts - 1
        res = _matmul_slice(a, b, mode=mode, out_dtype=out_dtype if last else F32, res=res, dep=dep, cols=cols,
                            kslice=(p, parts), name=name + ("_k%d" % p if parts > 1 else ""))
    return res


def _matmul_slice(a, b, *, mode, out_dtype, name, res, dep, kslice, cols):
    part, parts = kslice
    (m, k) = a.shape[-2:] if mode != "tn" else a.shape[::-1]
    col0, n = cols
    tk = k if a.ndim == 3 else k // parts
    tm, tn, swap = _tiles(mode, m, b.shape[2] if (mode == "tn" and b.ndim == 3) else n)
    while col0 % tn:
        tn //= 2
    joff = col0 // tn
    dims = {"nn": (((1,), (0,)), ((), ())), "nt": (((1,), (1,)), ((), ())), "tn": (((0,), (0,)), ((), ()))}[mode]
    has_res = res is not None
    has_dep = dep is not None

    def body(*refs):
        a_ref, b_ref = refs[0], refs[1]
        o_ref = refs[2 + has_res + has_dep]
        val = lax.dot_general(a_ref[...], b_ref[...], dims, preferred_element_type=F32)
        if has_res:
            val = val + refs[2][...]
        o_ref[...] = val.astype(o_ref.dtype)

    def spec(shape, f):
        if swap:
            return pl.BlockSpec(shape, lambda j, i: f(i, j))
        return pl.BlockSpec(shape, f)

    if mode == "tn":
        a_spec = spec((tk, tm), lambda i, j: (part, i))
    elif a.ndim == 3:
        a_spec = spec((None, tm, tk), lambda i, j: (part, i, 0))
    else:
        a_spec = spec((tm, tk), lambda i, j: (i, part))
    if mode == "nt":
        b_spec = spec((tn, tk), lambda i, j: (j, part))
    elif b.ndim == 3:
        per = b.shape[2] // tn
        b_spec = spec((None, tk, tn), lambda i, j: (j // per, part, j % per))
    else:
        b_spec = spec((tk, tn), lambda i, j: (part, joff + j))
    o_spec = spec((tm, tn), lambda i, j: (i, j))
    return pl.pallas_call(
        body,
        name=name,
        grid=(n // tn, m // tm) if swap else (m // tm, n // tn),
        in_specs=[a_spec, b_spec] + ([o_spec] if has_res else []) + ([ANY] if has_dep else []),
        out_specs=o_spec,
        out_shape=_sds((m, n), out_dtype),
        compiler_params=_cparams("parallel", "parallel"),
    )(*([a, b] + ([res] if has_res else []) + ([dep] if has_dep else [])))


def _rms_fwd(x, g, name):
    t, d = x.shape
    tm = _pick(t, (512, 256, 128))

    def body(x_ref, g_ref, o_ref):
        xf = x_ref[...]
        r = lax.rsqrt(jnp.mean(xf * xf, axis=-1, keepdims=True) + EPS)
        o_ref[...] = (xf * r * g_ref[...]).astype(o_ref.dtype)

    row = pl.BlockSpec((tm, d), lambda i: (i, 0))
    return pl.pallas_call(
        body, name=name, grid=(t // tm,),
        in_specs=[row, pl.BlockSpec((1, d), lambda i: (0, 0))],
        out_specs=row, out_shape=_sds((t, d), BF16),
        compiler_params=_cparams("parallel"),
    )(x, g)


def _rms_bwd_math(xf, g, du):
    r = lax.rsqrt(jnp.mean(xf * xf, axis=-1, keepdims=True) + EPS)
    xh = xf * r
    gdy = g * du
    dx = r * (gdy - xh * jnp.mean(gdy * xh, axis=-1, keepdims=True))
    dg = jnp.sum(du * xh, axis=0, keepdims=True)
    return dx, dg


def _rms_bwd(x, g, du, dh, name):
    t, d = x.shape
    tm = _pick(t, (256, 128))
    has_dh = dh is not None

    def body(*refs):
        x_ref, g_ref, du_ref = refs[0], refs[1], refs[2]
        o_ref, ob_ref, dg_ref = refs[3 + has_dh:]
        dx, dg = _rms_bwd_math(x_ref[...], g_ref[...], du_ref[...].astype(F32))
        if has_dh:
            dx = dx + refs[3][...]
        o_ref[...] = dx
        ob_ref[...] = dx.astype(BF16)

        @pl.when(pl.program_id(0) == 0)
        def _():
            dg_ref[...] = dg

        @pl.when(pl.program_id(0) > 0)
        def _():
            dg_ref[...] += dg

    row = pl.BlockSpec((tm, d), lambda i: (i, 0))
    vec = pl.BlockSpec((1, d), lambda i: (0, 0))
    return pl.pallas_call(
        body, name=name, grid=(t // tm,),
        in_specs=[row, vec, row] + ([row] if has_dh else []),
        out_specs=[row, row, vec],
        out_shape=[_sds((t, d), F32), _sds((t, d), BF16), _sds((1, d), F32)],
        compiler_params=_cparams("arbitrary"),
    )(*([x, g, du] + ([dh] if has_dh else [])))


def _loss_head(h, g, tgt):
    t, d = h.shape
    tm = _pick(t, (256, 128))

    def body(h_ref, g_ref, t_ref, o_ref, ob_ref, dg_ref, l_ref):
        xf = h_ref[...]
        gv = g_ref[...]
        r = lax.rsqrt(jnp.mean(xf * xf, axis=-1, keepdims=True) + EPS)
        err = xf * r * gv - t_ref[...]
        part = 0.5 * jnp.sum(jnp.mean(err * err, axis=-1, keepdims=True), axis=0, keepdims=True)
        dx, dg = _rms_bwd_math(xf, gv, err * (1.0 / d))
        o_ref[...] = dx
        ob_ref[...] = dx.astype(BF16)
        lrow = jnp.broadcast_to(part, (1, LANES))

        @pl.when(pl.program_id(0) == 0)
        def _():
            dg_ref[...] = dg
            l_ref[...] = lrow

        @pl.when(pl.program_id(0) > 0)
        def _():
            dg_ref[...] += dg
            l_ref[...] += lrow

    row = pl.BlockSpec((tm, d), lambda i: (i, 0))
    vec = pl.BlockSpec((1, d), lambda i: (0, 0))
    return pl.pallas_call(
        body, name="loss_head", grid=(t // tm,),
        in_specs=[row, vec, row],
        out_specs=[row, row, vec, pl.BlockSpec((1, LANES), lambda i: (0, 0))],
        out_shape=[_sds((t, d), F32), _sds((t, d), BF16), _sds((1, d), F32), _sds((1, LANES), F32)],
        compiler_params=_cparams("arbitrary"),
    )(h, g, tgt)


def _rope_tables(t):
    half = HEAD_DIM // 2
    inv_freq = ROPE_THETA ** (-jnp.arange(half, dtype=F32) / half)
    ang = jnp.arange(t, dtype=F32)[:, None] * inv_freq[None, :]
    cos = jnp.cos(ang)
    sin = jnp.sin(ang)
    reps = LANES // HEAD_DIM
    cos_t = jnp.tile(jnp.concatenate([cos, cos], axis=1), (1, reps))
    sin_t = jnp.tile(jnp.concatenate([-sin, sin], axis=1), (1, reps))
    return cos_t, sin_t


def _rope(v, cos, sin):
    w = v.shape[1]
    c = jnp.tile(cos, (1, w // LANES))
    s = jnp.tile(sin, (1, w // LANES))
    lane = lax.broadcasted_iota(jnp.int32, v.shape, 1)
    first = (lane % HEAD_DIM) < (HEAD_DIM // 2)
    partner = jnp.where(first, pltpu.roll(v, w - HEAD_DIM // 2, 1), pltpu.roll(v, HEAD_DIM // 2, 1))
    return v * c + partner * s


def _heads(v, count):
    return jnp.concatenate([v[:, i * HEAD_DIM:(i + 1) * HEAD_DIM] for i in range(count)], axis=0)


def _unheads(v, count):
    r = v.shape[0] // count
    return jnp.concatenate([v[g * r:(g + 1) * r] for g in range(count)], axis=1)


def _rope_qkv(proj, cos, sin, aw, kvw):
    t = proj.shape[0]
    nkv = kvw // HEAD_DIM
    koff = aw // kvw
    tm = _pick(t, (256, 128))

    def body(q_ref, k_ref, v_ref, c_ref, s_ref, qo_ref, ko_ref, vo_ref):
        c, s = c_ref[...], s_ref[...]
        qo_ref[...] = _rope(q_ref[...].astype(F32), c, s).astype(BF16)
        k = _rope(k_ref[...].astype(F32), c, s).astype(BF16)
        v = v_ref[...].astype(BF16)
        for h in range(nkv):
            ko_ref[h] = k[:, h * HEAD_DIM:(h + 1) * HEAD_DIM]
            vo_ref[h] = v[:, h * HEAD_DIM:(h + 1) * HEAD_DIM]

    def row(w, j):
        return pl.BlockSpec((tm, w), lambda i: (i, j))

    hm = pl.BlockSpec((nkv, tm, HEAD_DIM), lambda i: (0, i, 0))
    return pl.pallas_call(
        body, name="rope_qkv", grid=(t // tm,),
        in_specs=[row(aw, 0), row(kvw, koff), row(kvw, koff + 1), row(LANES, 0), row(LANES, 0)],
        out_specs=[row(aw, 0), hm, hm],
        out_shape=[_sds((t, aw), BF16), _sds((nkv, t, HEAD_DIM), BF16), _sds((nkv, t, HEAD_DIM), BF16)],
        compiler_params=_cparams("parallel"),
    )(proj, proj, proj, cos, sin)


def _attn_probs(qs, kb, n, h, qpk, sinks_ref):
    s = lax.dot_general(qs, kb, (((1,), (1,)), ((), ())), preferred_element_type=F32) * (HEAD_DIM ** -0.5)
    qi = lax.broadcasted_iota(jnp.int32, (BLOCK, 2 * BLOCK), 0)
    kc = lax.broadcasted_iota(jnp.int32, (BLOCK, 2 * BLOCK), 1)
    valid = (kc > qi) & (kc <= qi + BLOCK) & ((kc >= BLOCK) | (n > 0))
    bias = jnp.tile(jnp.where(valid, 0.0, NEG).astype(F32), (qpk, 1))
    s = s + bias
    rowg = lax.broadcasted_iota(jnp.int32, (qpk * BLOCK, 1), 0) // BLOCK
    sink = jnp.zeros((qpk * BLOCK, 1), F32)
    for g in range(qpk):
        sink = jnp.where(rowg == g, sinks_ref[0, h * qpk + g], sink)
    m = jnp.maximum(jnp.max(s, axis=-1, keepdims=True), sink)
    e = jnp.exp(s - m)
    es = jnp.exp(sink - m)
    inv = 1.0 / (jnp.sum(e, axis=-1, keepdims=True) + es)
    return e * inv, es * inv, rowg


HEADS_PER_STEP = 4


def _attn_specs(qw, hp):
    def head(f):
        return pl.BlockSpec((hp, BLOCK, HEAD_DIM), lambda n, h: (h, f(n), 0))

    cur = lambda n: n
    prev = lambda n: jnp.maximum(n - 1, 0)
    return [pl.BlockSpec((BLOCK, hp * qw), lambda n, h: (n, h)), head(cur), head(prev), head(cur), head(prev),
            pl.BlockSpec(memory_space=pltpu.SMEM)]


def _attn_fwd(q_r, k_r, v_h, sinks):
    t, aw = q_r.shape
    nkv = k_r.shape[0]
    qpk = aw // (nkv * HEAD_DIM)
    qw = qpk * HEAD_DIM
    hp = HEADS_PER_STEP if nkv % HEADS_PER_STEP == 0 else 1

    def body(q_ref, kc_ref, kp_ref, vc_ref, vp_ref, sinks_ref, o_ref):
        n, hg = pl.program_id(0), pl.program_id(1)
        outs = []
        for j in range(hp):
            kb = jnp.concatenate([kp_ref[j], kc_ref[j]], axis=0)
            vb = jnp.concatenate([vp_ref[j], vc_ref[j]], axis=0)
            qs = _heads(q_ref[:, j * qw:(j + 1) * qw], qpk)
            p, _, _ = _attn_probs(qs, kb, n, hg * hp + j, qpk, sinks_ref)
            outs.append(_unheads(jnp.dot(p.astype(BF16), vb, preferred_element_type=F32), qpk))
        o_ref[...] = jnp.concatenate(outs, axis=1).astype(o_ref.dtype)

    return pl.pallas_call(
        body, name="attn_fwd", grid=(t // BLOCK, nkv // hp),
        in_specs=_attn_specs(qw, hp),
        out_specs=pl.BlockSpec((BLOCK, hp * qw), lambda n, h: (n, h)),
        out_shape=_sds((t, aw), BF16),
        compiler_params=_cparams("parallel", "parallel"),
    )(q_r, k_r, k_r, v_h, v_h, sinks)


def _attn_bwd(q_r, k_r, v_h, sinks, o, do):
    t, aw = q_r.shape
    nkv = k_r.shape[0]
    qpk = aw // (nkv * HEAD_DIM)
    qw = qpk * HEAD_DIM
    hp = HEADS_PER_STEP if nkv % HEADS_PER_STEP == 0 else 1
    scale = HEAD_DIM ** -0.5

    def body(q_ref, kc_ref, kp_ref, vc_ref, vp_ref, sinks_ref, o_ref, do_ref,
             dq_ref, dkc_ref, dkp_ref, dvc_ref, dvp_ref, ds_ref):
        n, hg = pl.program_id(0), pl.program_id(1)
        lane = lax.broadcasted_iota(jnp.int32, (8, LANES), 1)
        row0 = lax.broadcasted_iota(jnp.int32, (8, LANES), 0) == 0
        dsink = jnp.zeros((8, LANES), F32)
        dqs = []
        for j in range(hp):
            h = hg * hp + j
            cols = slice(j * qw, (j + 1) * qw)
            kb = jnp.concatenate([kp_ref[j], kc_ref[j]], axis=0)
            vb = jnp.concatenate([vp_ref[j], vc_ref[j]], axis=0)
            qs = _heads(q_ref[:, cols], qpk)
            dos = _heads(do_ref[:, cols], qpk)
            p, psink, rowg = _attn_probs(qs, kb, n, h, qpk, sinks_ref)
            pb = p.astype(BF16)
            delta = jnp.sum(dos.astype(F32) * _heads(o_ref[:, cols], qpk).astype(F32), axis=-1, keepdims=True)
            dv = lax.dot_general(pb, dos, (((0,), (0,)), ((), ())), preferred_element_type=F32)
            dp = lax.dot_general(dos, vb, (((1,), (1,)), ((), ())), preferred_element_type=F32)
            dsc = (p * (dp - delta)).astype(BF16)
            dqs.append(_unheads(jnp.dot(dsc, kb, preferred_element_type=F32) * scale, qpk))
            dk = lax.dot_general(dsc, qs, (((0,), (0,)), ((), ())), preferred_element_type=F32) * scale
            dkp_ref[j] = dk[:BLOCK]
            dkc_ref[j] = dk[BLOCK:]
            dvp_ref[j] = dv[:BLOCK]
            dvc_ref[j] = dv[BLOCK:]
            sink_term = psink * delta
            for g in range(qpk):
                val = -jnp.sum(jnp.where(rowg == g, sink_term, 0.0))
                dsink = jnp.where(row0 & (lane == h * qpk + g), val, dsink)
        dq_ref[...] = jnp.concatenate(dqs, axis=1).astype(dq_ref.dtype)
        first = (n == 0) & (hg == 0)

        @pl.when(first)
        def _():
            ds_ref[...] = dsink

        @pl.when(jnp.logical_not(first))
        def _():
            ds_ref[...] += dsink

    qblk = pl.BlockSpec((BLOCK, hp * qw), lambda n, h: (n, h))
    kvblk = pl.BlockSpec((hp, BLOCK, HEAD_DIM), lambda n, h: (h, n, 0))
    return pl.pallas_call(
        body, name="attn_bwd", grid=(t // BLOCK, nkv // hp),
        in_specs=_attn_specs(qw, hp) + [qblk, qblk],
        out_specs=[qblk, kvblk, kvblk, kvblk, kvblk, pl.BlockSpec((8, LANES), lambda n, h: (0, 0))],
        out_shape=[_sds((t, aw), BF16)] + [_sds((nkv, t, HEAD_DIM), F32)] * 4 + [_sds((8, LANES), F32)],
        compiler_params=_cparams("arbitrary", "arbitrary"),
    )(q_r, k_r, k_r, v_h, v_h, sinks, o, do)


HALO = 16


def _shift_down(v, k, halo):
    rows = lax.broadcasted_iota(jnp.int32, v.shape, 0)
    out = pltpu.roll(v, k, 0)
    for r in range(k):
        out = jnp.where(rows == r, halo[HALO - k + r:HALO - k + r + 1, :], out)
    return out


def _shift_up(v, k, halo):
    tm = v.shape[0]
    rows = lax.broadcasted_iota(jnp.int32, v.shape, 0)
    out = pltpu.roll(v, tm - k, 0)
    for r in range(k):
        out = jnp.where(rows == tm - k + r, halo[r:r + 1, :], out)
    return out


def _conv_fwd(proj, conv_w, zoff, cw, cb):
    t = proj.shape[0]
    tm = _pick(t, (512, 256, 128))
    zb, nb = zoff // cb, cw // cb
    hb = tm // HALO

    def body(z_ref, gb_ref, gc_ref, zp_ref, gcp_ref, w_ref, o_ref):
        i = pl.program_id(0)
        cz = gc_ref[...].astype(F32) * z_ref[...].astype(F32)
        czp = gcp_ref[...].astype(F32) * zp_ref[...].astype(F32) * (i > 0).astype(F32)
        w = w_ref[...]
        y = w[0:1] * _shift_down(cz, 2, czp) + w[1:2] * _shift_down(cz, 1, czp) + w[2:3] * cz
        o_ref[...] = (gb_ref[...].astype(F32) * y).astype(o_ref.dtype)

    def col(k):
        return pl.BlockSpec((tm, cb), lambda i, j: (i, zb + k * nb + j))

    def halo(k):
        return pl.BlockSpec((HALO, cb), lambda i, j: (jnp.maximum(i * hb - 1, 0), zb + k * nb + j))

    return pl.pallas_call(
        body, name="conv_fwd", grid=(t // tm, nb),
        in_specs=[col(0), col(1), col(2), halo(0), halo(2), pl.BlockSpec((3, cb), lambda i, j: (0, j))],
        out_specs=pl.BlockSpec((tm, cb), lambda i, j: (i, j)),
        out_shape=_sds((t, cw), BF16),
        compiler_params=_cparams("parallel", "parallel"),
    )(proj, proj, proj, proj, proj, conv_w)


def _conv_bwd(proj, conv_w, dco, zoff, cw, cb):
    t = proj.shape[0]
    tm = _pick(t, (512, 256, 128))
    zb, nb = zoff // cb, cw // cb
    hb = tm // HALO
    nt = t // tm

    def body(z_ref, gb_ref, gc_ref, zp_ref, gcp_ref, gbn_ref, w_ref, d_ref, dn_ref, dz_ref, dgb_ref, dgc_ref, dw_ref):
        i = pl.program_id(1)
        z, gb, gc = z_ref[...].astype(F32), gb_ref[...].astype(F32), gc_ref[...].astype(F32)
        d = d_ref[...].astype(F32)
        cz = gc * z
        czp = gcp_ref[...].astype(F32) * zp_ref[...].astype(F32) * (i > 0).astype(F32)
        w = w_ref[...]
        cz1 = _shift_down(cz, 1, czp)
        cz2 = _shift_down(cz, 2, czp)
        y = w[0:1] * cz2 + w[1:2] * cz1 + w[2:3] * cz
        dgb_ref[...] = (d * y).astype(dgb_ref.dtype)
        dy = d * gb
        dyn = dn_ref[...].astype(F32) * gbn_ref[...].astype(F32) * (i < nt - 1).astype(F32)
        dcz = w[2:3] * dy + w[1:2] * _shift_up(dy, 1, dyn) + w[0:1] * _shift_up(dy, 2, dyn)
        dgc_ref[...] = (dcz * z).astype(dgc_ref.dtype)
        dz_ref[...] = (dcz * gc).astype(dz_ref.dtype)
        rows = lax.broadcasted_iota(jnp.int32, (8, cb), 0)
        dw = jnp.zeros((8, cb), F32)
        for r, tap in enumerate((cz2, cz1, cz)):
            dw = jnp.where(rows == r, jnp.sum(dy * tap, axis=0, keepdims=True), dw)

        @pl.when(i == 0)
        def _():
            dw_ref[...] = dw

        @pl.when(i > 0)
        def _():
            dw_ref[...] += dw

    def col(k):
        return pl.BlockSpec((tm, cb), lambda j, i: (i, zb + k * nb + j))

    def halo_prev(k):
        return pl.BlockSpec((HALO, cb), lambda j, i: (jnp.maximum(i * hb - 1, 0), zb + k * nb + j))

    own = pl.BlockSpec((tm, cb), lambda j, i: (i, j))
    nxt = lambda i: jnp.minimum((i + 1) * hb, t // HALO - 1)
    return pl.pallas_call(
        body, name="conv_bwd", grid=(nb, nt),
        in_specs=[col(0), col(1), col(2), halo_prev(0), halo_prev(2),
                  pl.BlockSpec((HALO, cb), lambda j, i: (nxt(i), zb + nb + j)),
                  pl.BlockSpec((3, cb), lambda j, i: (0, j)), own,
                  pl.BlockSpec((HALO, cb), lambda j, i: (nxt(i), j))],
        out_specs=[own, own, own, pl.BlockSpec((8, cb), lambda j, i: (0, j))],
        out_shape=[_sds((t, cw), BF16)] * 3 + [_sds((8, cw), F32)],
        compiler_params=_cparams("parallel", "arbitrary"),
    )(proj, proj, proj, proj, proj, proj, conv_w, dco, dco)


def _merge_fwd(proj, ya, yc, goff, cb):
    t, d = ya.shape
    tm = _pick(t, (512, 256, 128))
    gb_, nb = goff // cb, d // cb

    def body(ga_ref, gc_ref, ya_ref, yc_ref, o_ref):
        f = lambda r: r[...].astype(F32)
        o_ref[...] = (_sigmoid(f(ga_ref)) * f(ya_ref) + _sigmoid(f(gc_ref)) * f(yc_ref)).astype(o_ref.dtype)

    own = pl.BlockSpec((tm, cb), lambda i, j: (i, j))
    return pl.pallas_call(
        body, name="merge_fwd", grid=(t // tm, nb),
        in_specs=[pl.BlockSpec((tm, cb), lambda i, j: (i, gb_ + j)),
                  pl.BlockSpec((tm, cb), lambda i, j: (i, gb_ + nb + j)), own, own],
        out_specs=own, out_shape=_sds((t, d), BF16),
        compiler_params=_cparams("parallel", "parallel"),
    )(proj, proj, ya, yc)


def _merge_bwd(proj, ya, yc, dm, goff, cb):
    t, d = ya.shape
    tm = _pick(t, (512, 256, 128))
    gb_, nb = goff // cb, d // cb

    def body(ga_ref, gc_ref, ya_ref, yc_ref, dm_ref, dya_ref, dyc_ref, dga_ref, dgc_ref):
        dmv = dm_ref[...].astype(F32)
        sa = _sigmoid(ga_ref[...].astype(F32))
        sc = _sigmoid(gc_ref[...].astype(F32))
        dya_ref[...] = (dmv * sa).astype(BF16)
        dyc_ref[...] = (dmv * sc).astype(BF16)
        dga_ref[...] = (dmv * ya_ref[...].astype(F32) * sa * (1.0 - sa)).astype(BF16)
        dgc_ref[...] = (dmv * yc_ref[...].astype(F32) * sc * (1.0 - sc)).astype(BF16)

    own = pl.BlockSpec((tm, cb), lambda i, j: (i, j))
    return pl.pallas_call(
        body, name="merge_bwd", grid=(t // tm, nb),
        in_specs=[pl.BlockSpec((tm, cb), lambda i, j: (i, gb_ + j)),
                  pl.BlockSpec((tm, cb), lambda i, j: (i, gb_ + nb + j)), own, own, own],
        out_specs=[own] * 4, out_shape=[_sds((t, d), BF16)] * 4,
        compiler_params=_cparams("parallel", "parallel"),
    )(proj, proj, ya, yc, dm)


def _assemble_dproj(dq, dkc, dkp, dvc, dvp, cos, sin, dz, dgb, dgc, dga, dgg):
    t, aw = dq.shape
    nkv, cw, d = dkc.shape[0], dz.shape[1], dga.shape[1]
    kvw = nkv * HEAD_DIM
    nblk = t // BLOCK
    width = aw + 2 * kvw + 3 * cw + 2 * d

    def body(dq_ref, dkc_ref, dkp_ref, dvc_ref, dvp_ref, c_ref, s_ref, dz_ref, dgb_ref, dgc_ref, dga_ref, dgg_ref,
             o_ref):
        keep = (pl.program_id(0) < nblk - 1).astype(F32)
        c, s = c_ref[...], s_ref[...]
        dk = jnp.concatenate([dkc_ref[h] + dkp_ref[h] * keep for h in range(nkv)], axis=1)
        dv = jnp.concatenate([dvc_ref[h] + dvp_ref[h] * keep for h in range(nkv)], axis=1)
        o_ref[...] = jnp.concatenate(
            [_rope(dq_ref[...].astype(F32), c, -s).astype(BF16), _rope(dk, c, -s).astype(BF16), dv.astype(BF16),
             dz_ref[...], dgb_ref[...], dgc_ref[...], dga_ref[...], dgg_ref[...]], axis=1)

    def cur(w):
        return pl.BlockSpec((BLOCK, w), lambda n: (n, 0))

    head_cur = pl.BlockSpec((nkv, BLOCK, HEAD_DIM), lambda n: (0, n, 0))
    head_nxt = pl.BlockSpec((nkv, BLOCK, HEAD_DIM), lambda n: (0, jnp.minimum(n + 1, nblk - 1), 0))
    return pl.pallas_call(
        body, name="assemble_dproj", grid=(nblk,),
        in_specs=[cur(aw), head_cur, head_nxt, head_cur, head_nxt, cur(LANES), cur(LANES),
                  cur(cw), cur(cw), cur(cw), cur(d), cur(d)],
        out_specs=cur(width), out_shape=_sds((t, width), BF16),
        compiler_params=_cparams("parallel"),
    )(dq, dkc, dkp, dvc, dvp, cos, sin, dz, dgb, dgc, dga, dgg)


def _xattn_probs(qh, kh):
    s = lax.dot_general(qh, kh, (((1,), (1,)), ((), ())), preferred_element_type=F32) * (X_HEAD_DIM ** -0.5)
    e = jnp.exp(s - jnp.max(s, axis=-1, keepdims=True))
    return e * (1.0 / jnp.sum(e, axis=-1, keepdims=True))


def _xattn_fwd(xq, kv):
    t, xw = xq.shape
    mt = kv.shape[0]
    tm = _pick(t, (512, 256, 128))

    def body(q_ref, kv_ref, o_ref):
        outs = []
        for hd in range(xw // X_HEAD_DIM):
            hs = slice(hd * X_HEAD_DIM, (hd + 1) * X_HEAD_DIM)
            vs = slice(xw + hd * X_HEAD_DIM, xw + (hd + 1) * X_HEAD_DIM)
            p = _xattn_probs(q_ref[:, hs], kv_ref[:, hs])
            outs.append(jnp.dot(p.astype(BF16), kv_ref[:, vs], preferred_element_type=F32))
        o_ref[...] = jnp.concatenate(outs, axis=1).astype(o_ref.dtype)

    return pl.pallas_call(
        body, name="xattn_fwd", grid=(t // tm,),
        in_specs=[pl.BlockSpec((tm, xw), lambda i: (i, 0)), pl.BlockSpec((mt, 2 * xw), lambda i: (0, 0))],
        out_specs=pl.BlockSpec((tm, xw), lambda i: (i, 0)), out_shape=_sds((t, xw), BF16),
        compiler_params=_cparams("parallel"),
    )(xq, kv)


def _xattn_bwd(xq, kv, do):
    t, xw = xq.shape
    mt = kv.shape[0]
    tm = _pick(t, (512, 256, 128))
    scale = X_HEAD_DIM ** -0.5

    def body(q_ref, kv_ref, do_ref, dq_ref, dkv_ref):
        dqs, dks, dvs = [], [], []
        for hd in range(xw // X_HEAD_DIM):
            hs = slice(hd * X_HEAD_DIM, (hd + 1) * X_HEAD_DIM)
            vs = slice(xw + hd * X_HEAD_DIM, xw + (hd + 1) * X_HEAD_DIM)
            qh, kh, vh, doh = q_ref[:, hs], kv_ref[:, hs], kv_ref[:, vs], do_ref[:, hs]
            p = _xattn_probs(qh, kh)
            pb = p.astype(BF16)
            o = jnp.dot(pb, vh, preferred_element_type=F32)
            delta = jnp.sum(doh.astype(F32) * o, axis=-1, keepdims=True)
            dvs.append(lax.dot_general(pb, doh, (((0,), (0,)), ((), ())), preferred_element_type=F32))
            dp = lax.dot_general(doh, vh, (((1,), (1,)), ((), ())), preferred_element_type=F32)
            dsc = (p * (dp - delta)).astype(BF16)
            dqs.append(jnp.dot(dsc, kh, preferred_element_type=F32) * scale)
            dks.append(lax.dot_general(dsc, qh, (((0,), (0,)), ((), ())), preferred_element_type=F32) * scale)
        dq_ref[...] = jnp.concatenate(dqs, axis=1).astype(dq_ref.dtype)
        dkv = jnp.concatenate(dks + dvs, axis=1)

        @pl.when(pl.program_id(0) == 0)
        def _():
            dkv_ref[...] = dkv

        @pl.when(pl.program_id(0) > 0)
        def _():
            dkv_ref[...] += dkv

    row = pl.BlockSpec((tm, xw), lambda i: (i, 0))
    whole = pl.BlockSpec((mt, 2 * xw), lambda i: (0, 0))
    return pl.pallas_call(
        body, name="xattn_bwd", grid=(t // tm,),
        in_specs=[row, whole, row], out_specs=[row, whole],
        out_shape=[_sds((t, xw), BF16), _sds((mt, 2 * xw), F32)],
        compiler_params=_cparams("arbitrary"),
    )(xq, kv, do)


def _ffn_in_swiglu(u, w, dep, name):
    t, d = u.shape
    f = w.shape[1] // 2
    tm, tn, _ = _tiles("nn", t, f)
    nf = f // tn
    deps = [] if dep is None else [dep]

    def body(u_ref, wa_ref, wb_ref, *rest):
        h_ref, o_ref = rest[len(deps):]
        a = jnp.dot(u_ref[...], wa_ref[...], preferred_element_type=F32)
        b = jnp.dot(u_ref[...], wb_ref[...], preferred_element_type=F32)
        h_ref[0] = a.astype(h_ref.dtype)
        h_ref[1] = b.astype(h_ref.dtype)
        o_ref[...] = (a * _sigmoid(a) * b).astype(o_ref.dtype)

    return pl.pallas_call(
        body, name=name, grid=(t // tm, nf),
        in_specs=[pl.BlockSpec((tm, d), lambda i, j: (i, 0)), pl.BlockSpec((d, tn), lambda i, j: (0, j)),
                  pl.BlockSpec((d, tn), lambda i, j: (0, nf + j))] + [ANY] * len(deps),
        out_specs=[pl.BlockSpec((2, tm, tn), lambda i, j: (0, i, j)), pl.BlockSpec((tm, tn), lambda i, j: (i, j))],
        out_shape=[_sds((2, t, f), BF16), _sds((t, f), BF16)],
        compiler_params=_cparams("parallel", "parallel"),
    )(u, w, w, *deps)


def _dact_swiglu(dh, w_out, hid, dep, name):
    t, d = dh.shape
    f = w_out.shape[0]
    tm, tn, _ = _tiles("nt", t, f)
    deps = [] if dep is None else [dep]

    def body(dh_ref, w_ref, h_ref, *rest):
        o_ref = rest[len(deps)]
        g = lax.dot_general(dh_ref[...], w_ref[...], (((1,), (1,)), ((), ())), preferred_element_type=F32)
        a = h_ref[0].astype(F32)
        b = h_ref[1].astype(F32)
        sg = _sigmoid(a)
        o_ref[0] = (g * b * sg * (1.0 + a * (1.0 - sg))).astype(o_ref.dtype)
        o_ref[1] = (g * a * sg).astype(o_ref.dtype)

    pair = pl.BlockSpec((2, tm, tn), lambda i, j: (0, i, j))
    return pl.pallas_call(
        body, name=name, grid=(t // tm, f // tn),
        in_specs=[pl.BlockSpec((tm, d), lambda i, j: (i, 0)), pl.BlockSpec((tn, d), lambda i, j: (j, 0)), pair]
        + [ANY] * len(deps),
        out_specs=pair, out_shape=_sds((2, t, f), BF16),
        compiler_params=_cparams("parallel", "parallel"),
    )(dh, w_out, hid, *deps)


def _adamw(w, g, m, v, name):
    r, c = w.shape
    tr = _pick(r, (256, 128, 64, 32, 16, 8)) if r * c > 65536 else r

    def body(w_ref, g_ref, m_ref, v_ref, d_ref, nm_ref, nv_ref, go_ref):
        gv = g_ref[...]
        go_ref[...] = gv
        m2 = ADAM_B1 * m_ref[...] + (1.0 - ADAM_B1) * gv
        v2 = ADAM_B2 * v_ref[...] + (1.0 - ADAM_B2) * (gv * gv)
        m_hat = m2 / (1.0 - ADAM_B1 ** ADAM_STEP)
        v_hat = v2 / (1.0 - ADAM_B2 ** ADAM_STEP)
        d_ref[...] = -ADAM_LR * (m_hat / (jnp.sqrt(v_hat) + ADAM_EPS) + ADAM_WD * w_ref[...])
        nm_ref[...] = m2
        nv_ref[...] = v2

    blk = pl.BlockSpec((tr, c), lambda i: (i, 0))
    return pl.pallas_call(
        body, name=name, grid=(r // tr,),
        in_specs=[blk] * 4, out_specs=[blk] * 4, out_shape=[_sds((r, c), F32)] * 4,
        compiler_params=_cparams("parallel"),
    )(w, g, m, v)


class _Weights:
    def __init__(self, full):
        self.full = full
        self.grads = {}

    def get(self, name):
        return self.full[name]

    def mark(self, tag, value):
        return value

    def grad(self, name, g):
        self.grads[name] = g

    def dep(self):
        return None


def _local_step(x, mem, tgt, g_mix, sinks, g_xattn, g_mem, g_ffn, g_final, dims, wts):
    t, d = x.shape
    aw, cw, kvw = dims
    cb = 2 * kvw
    zoff = aw + 2 * kvw
    goff = zoff + 3 * cw
    cos, sin = _rope_tables(t)
    mark, get = wts.mark, wts.get

    def mm(a, b, **kw):
        return _matmul(a, b, dep=wts.dep(), **kw)

    u1 = mark("u1", _rms_fwd(x, g_mix, "rms_mix"))
    qkv = mm(u1, get("w_in"), mode="nn", out_dtype=F32, name="mm_qkv", cols=(0, zoff))
    proj = mark("proj", mm(u1, get("w_in"), mode="nn", out_dtype=BF16, name="mm_gates", cols=(zoff, goff - zoff + 2 * d)))
    zoff, goff = 0, goff - zoff
    q_r, k_r, v_h = _rope_qkv(qkv, cos, sin, aw, kvw)
    attn_o = mark("attn_o", _attn_fwd(q_r, k_r, v_h, sinks))
    conv_o = mark("conv_o", _conv_fwd(proj, get("conv_w"), zoff, cw, cb))
    ya = mark("ya", mm(attn_o, get("w_attn_proj"), mode="nn", out_dtype=BF16, name="mm_yattn"))
    yc = mark("yc", mm(conv_o, get("w_conv_proj"), mode="nn", out_dtype=BF16, name="mm_yconv"))
    merged = mark("merged", _merge_fwd(proj, ya, yc, goff, cb))
    h1 = mark("h1", mm(merged, get("w_mix_out"), mode="nn", out_dtype=F32, name="mm_mix", res=x))
    u2 = mark("u2", _rms_fwd(h1, g_xattn, "rms_xattn"))
    mem_n = _rms_fwd(mem, g_mem, "rms_mem")
    xq = mark("xq", mm(u2, get("w_xq"), mode="nn", out_dtype=BF16, name="mm_xq"))
    kv = mm(mem_n, get("w_xkv"), mode="nn", out_dtype=BF16, name="mm_xkv")
    xo = mark("xo", _xattn_fwd(xq, kv))
    h2 = mark("h2", mm(xo, get("w_xo"), mode="nn", out_dtype=F32, name="mm_xo", res=h1))
    u3 = mark("u3", _rms_fwd(h2, g_ffn, "rms_ffn"))
    hid, act = _ffn_in_swiglu(u3, get("w_ffn_in"), wts.dep(), "mm_ffn_in")
    mark("hid", hid)
    h3 = mm(act, get("w_ffn_out"), mode="nn", out_dtype=F32, name="mm_ffn_out", res=h2)
    dh3, dh3b, dg_final, loss = _loss_head(h3, g_final, tgt)
    mark("dh3", dh3b)

    wts.grad("w_ffn_out", mm(act, dh3b, mode="tn", out_dtype=BF16, name="mm_dw_ffn_out"))
    dhid = mark("dhid", _dact_swiglu(dh3b, get("w_ffn_out"), hid, wts.dep(), "mm_dact"))
    wts.grad("w_ffn_in", mm(u3, dhid, mode="tn", out_dtype=BF16, name="mm_dw_ffn_in"))
    du3 = mark("du3", mm(dhid, get("w_ffn_in"), mode="nt", out_dtype=F32, name="mm_du3"))
    dh2, dh2b, dg_ffn = _rms_bwd(h2, g_ffn, du3, dh3, "rms_bwd_ffn")
    mark("dh2", dh2b)
    wts.grad("w_xo", mm(xo, dh2b, mode="tn", out_dtype=BF16, name="mm_dw_xo"))
    dxo = mm(dh2b, get("w_xo"), mode="nt", out_dtype=BF16, name="mm_dxo")
    dxq, dkv = _xattn_bwd(xq, kv, dxo)
    dkvb = dkv.astype(BF16)
    wts.grad("w_xq", mm(u2, dxq, mode="tn", out_dtype=BF16, name="mm_dw_xq"))
    du2 = mm(dxq, get("w_xq"), mode="nt", out_dtype=F32, name="mm_du2")
    wts.grad("w_xkv", mm(mem_n, dkvb, mode="tn", out_dtype=BF16, name="mm_dw_xkv"))
    dmem_n = mm(dkvb, get("w_xkv"), mode="nt", out_dtype=F32, name="mm_dmem")
    _, _, dg_mem = _rms_bwd(mem, g_mem, dmem_n, None, "rms_bwd_mem")
    dh1, dh1b, dg_xattn = _rms_bwd(h1, g_xattn, du2, dh2, "rms_bwd_xattn")
    mark("dh1", dh1b)
    wts.grad("w_mix_out", mm(merged, dh1b, mode="tn", out_dtype=BF16, name="mm_dw_mix"))
    dmerged = mm(dh1b, get("w_mix_out"), mode="nt", out_dtype=BF16, name="mm_dmerged")
    dya, dyc, dga, dgg = _merge_bwd(proj, ya, yc, dmerged, goff, cb)
    mark("dya", dya)
    wts.grad("w_attn_proj", mm(attn_o, dya, mode="tn", out_dtype=BF16, name="mm_dw_attn_proj"))
    dattn_o = mm(dya, get("w_attn_proj"), mode="nt", out_dtype=BF16, name="mm_dattn")
    wts.grad("w_conv_proj", mm(conv_o, dyc, mode="tn", out_dtype=BF16, name="mm_dw_conv_proj"))
    dconv_o = mark("dconv_o", mm(dyc, get("w_conv_proj"), mode="nt", out_dtype=BF16, name="mm_dconv"))
    dz, dgb, dgc, dconv_w = _conv_bwd(proj, get("conv_w"), dconv_o, zoff, cw, cb)
    mark("dz", dz)
    dq, dkc, dkp, dvc, dvp, dsinks = _attn_bwd(q_r, k_r, v_h, sinks, attn_o, dattn_o)
    mark("dq", dq)
    dproj = mark("dproj", _assemble_dproj(dq, dkc, dkp, dvc, dvp, cos, sin, dz, dgb, dgc, dga, dgg))
    wts.grad("w_in", mm(u1, dproj, mode="tn", out_dtype=BF16, name="mm_dw_in"))
    du1 = mark("du1", mm(dproj, get("w_in"), mode="nt", out_dtype=F32, name="mm_du1"))
    grad_x, _, dg_mix = _rms_bwd(x, g_mix, du1, dh1, "rms_bwd_mix")
    mark("grad_x", grad_x)

    small = dict(g_mix=dg_mix, g_xattn=dg_xattn, g_mem=dg_mem, g_ffn=dg_ffn, g_final=dg_final,
                 conv_w=dconv_w[:3], attn_sinks=dsinks[0:1, :sinks.shape[1]], loss=loss[0:1, 0:1])
    return grad_x, small


BIG = (("w_in", 1), ("w_attn_proj", 1), ("w_conv_proj", 1), ("w_mix_out", 0), ("w_xq", 0), ("w_xkv", 0),
       ("w_xo", 1), ("w_ffn_in", 1), ("w_ffn_out", 0))


def _place():
    x, y, c = lax.axis_index("x"), lax.axis_index("y"), lax.axis_index("c")
    chips = [(1 - x, y), (x, 1 - y), (1 - x, 1 - y)]
    return x, y, c, chips


def _window(ref, ax, shard_shape, s, h):
    sr, sc = shard_shape
    hr = sr // 2
    if ax == 1:
        return ref.at[pl.ds(pl.multiple_of(h * hr, 16), hr), pl.ds(pl.multiple_of(s * sc, LANES), sc)]
    return ref.at[pl.ds(pl.multiple_of(s * sr + h * hr, 16), hr), :]


def _half(ref, h):
    hr = ref.shape[0] // 2
    return ref.at[pl.ds(pl.multiple_of(h * hr, 16), hr), :]


def _remote(src, dst, send_sem, recv_sem, dev):
    return pltpu.make_async_remote_copy(src_ref=src, dst_ref=dst, send_sem=send_sem, recv_sem=recv_sem,
                                        device_id=dev, device_id_type=MESH)


def _cast_to_full(shard, ax, me, name, dtype=BF16):
    sr, sc = shard.shape
    tr = _pick(sr, (256, 352, 128, 64, 32, 16))
    nr = sr // tr
    full = (sr * N_CHIPS, sc) if ax == 0 else (sr, sc * N_CHIPS)

    def body(me_ref, s_ref, o_ref):
        o_ref[...] = s_ref[...].astype(o_ref.dtype)

    if ax == 1:
        out_spec = pl.BlockSpec((tr, sc), lambda r, me_ref: (r, me_ref[0]))
    else:
        out_spec = pl.BlockSpec((tr, sc), lambda r, me_ref: (me_ref[0] * nr + r, 0))
    return pl.pallas_call(
        body, name=name,
        grid_spec=pltpu.PrefetchScalarGridSpec(
            num_scalar_prefetch=1, grid=(nr,), in_specs=[pl.BlockSpec((tr, sc), lambda r, me_ref: (r, 0))],
            out_specs=out_spec),
        out_shape=_sds(full, dtype),
        compiler_params=_cparams("parallel"),
    )(me, shard)


HBM = pl.BlockSpec(memory_space=pltpu.HBM)
SEM = pl.BlockSpec(memory_space=pltpu.SEMAPHORE)
EFFECT = pltpu.SideEffectType.DATAFLOW_SIDE_EFFECTING


def _in_hbm(a):
    return pltpu.with_memory_space_constraint(a, pltpu.HBM)


def _gather_window(ref, ax, shard_shape, s, h):
    if h is not None:
        return _window(ref, ax, shard_shape, s, h)
    sr, sc = shard_shape
    if ax == 1:
        return ref.at[:, pl.ds(pl.multiple_of(s * sc, LANES), sc)]
    return ref.at[pl.ds(pl.multiple_of(s * sr, 8), sr), :]


def _ag_start(fulls, axes, shard_shapes, whole):
    n = len(fulls)

    def body(*refs):
        src = refs[:n]
        send_sems, recv_sems = refs[n], refs[n + 1]
        token = refs[2 * n + 2]
        x, y, c, chips = _place()
        me = 2 * x + y
        for i in range(n):
            h = None if whole[i] else c
            for j, chip in enumerate(chips):
                blk = _gather_window(src[i], axes[i], shard_shapes[i], me, h)
                _remote(blk, blk, send_sems.at[3 * i + j], recv_sems.at[3 * i + j], (*chip, c)).start()
        token[...] = jnp.zeros_like(token)

    res = pl.pallas_call(
        body, name="ag_start_" + str(n),
        out_shape=(pltpu.SemaphoreType.DMA((3 * n,)), pltpu.SemaphoreType.DMA((3 * n,)),
                   *[pltpu.HBM(f.shape, f.dtype) for f in fulls], _sds((8, LANES), F32)),
        in_specs=[HBM] * n, out_specs=(SEM, SEM, *[HBM] * n, pl.BlockSpec(memory_space=pltpu.VMEM)),
        input_output_aliases={i: 2 + i for i in range(n)},
        compiler_params=pltpu.CompilerParams(has_side_effects=EFFECT),
    )(*[_in_hbm(f) for f in fulls])
    return res[0], res[1], list(res[2:2 + n]), res[2 + n]


def _ag_mid(bufs, slots, axes, shard_shapes, whole, send_sems, recv_sems, after, name):
    ng = len(bufs)

    def body(*refs):
        src = refs[:ng]
        s_in, r_in = refs[ng], refs[ng + 1]
        fsend, frecv = refs[ng + 3], refs[ng + 4]
        x, y, c, chips = _place()
        me = 2 * x + y
        sib = (x, y, 1 - c)
        for k, i in enumerate(slots):
            h = None if whole[k] else c
            for j, chip in enumerate(chips):
                cj = 2 * chip[0] + chip[1]
                mine = _gather_window(src[k], axes[k], shard_shapes[k], me, h)
                theirs = _gather_window(src[k], axes[k], shard_shapes[k], cj, h)
                _remote(theirs, theirs, s_in.at[3 * i + j], r_in.at[3 * i + j], (*chip, c)).wait_recv()
                _remote(mine, mine, s_in.at[3 * i + j], r_in.at[3 * i + j], (*chip, c)).wait_send()
                if not whole[k]:
                    _remote(theirs, theirs, fsend.at[3 * k + j], frecv.at[3 * k + j], sib).start()
        token = refs[2 * ng + 5]
        token[...] = jnp.zeros_like(token)

    res = pl.pallas_call(
        body, name=name,
        out_shape=(pltpu.SemaphoreType.DMA((3 * ng,)), pltpu.SemaphoreType.DMA((3 * ng,)),
                   *[pltpu.HBM(b.shape, b.dtype) for b in bufs], _sds((8, LANES), F32)),
        in_specs=[HBM] * ng + [SEM, SEM, ANY],
        out_specs=(SEM, SEM, *[HBM] * ng, pl.BlockSpec(memory_space=pltpu.VMEM)),
        input_output_aliases={k: 2 + k for k in range(ng)},
        compiler_params=pltpu.CompilerParams(has_side_effects=EFFECT),
    )(*bufs, send_sems, recv_sems, after)
    return res[0], res[1], list(res[2:2 + ng]), res[2 + ng]


def _ag_wait(bufs, axes, shard_shapes, whole, fsend, frecv, after, name):
    ng = len(bufs)

    def body(*refs):
        src = refs[:ng]
        s_in, r_in = refs[ng], refs[ng + 1]
        x, y, c, chips = _place()
        sib = (x, y, 1 - c)
        for k in range(ng):
            if whole[k]:
                continue
            for j, chip in enumerate(chips):
                cj = 2 * chip[0] + chip[1]
                sent = _gather_window(src[k], axes[k], shard_shapes[k], cj, c)
                landed = _gather_window(src[k], axes[k], shard_shapes[k], cj, 1 - c)
                _remote(landed, landed, s_in.at[3 * k + j], r_in.at[3 * k + j], sib).wait_recv()
                _remote(sent, sent, s_in.at[3 * k + j], r_in.at[3 * k + j], sib).wait_send()

    res = pl.pallas_call(
        body, name=name,
        out_shape=tuple(pltpu.HBM(b.shape, b.dtype) for b in bufs),
        in_specs=[HBM] * ng + [SEM, SEM, ANY], out_specs=tuple([HBM] * ng),
        input_output_aliases={k: k for k in range(ng)},
        compiler_params=pltpu.CompilerParams(has_side_effects=EFFECT),
    )(*bufs, fsend, frecv, after)
    return list(res)


class _Schedule:
    GROUPS = ((("w_in", "conv_w"), "u1", "u1"),
              (("w_attn_proj", "w_conv_proj", "w_mix_out", "w_xq", "w_xkv", "w_xo"), "proj", "conv_o"),
              (("w_ffn_in",), "h1", "h2"),
              (("w_ffn_out",), "u3", "hid"))
    STARTS = ((0,), (1, 2, 3))
    REDUCE = ((("w_ffn_out",), "dhid", "grad:w_ffn_in", "du3"),
              (("w_ffn_in",), "grad:w_ffn_in", "dya", "grad_x"),
              (("w_xo", "w_xq", "w_xkv", "w_mix_out", "w_attn_proj", "w_conv_proj"), "dconv_o", "dq", "grad_x"),
              (("w_in",), "grad:w_in", "end", "end2"))

    def __init__(self, seed, axes, shard_shapes, place, on_ready):
        self.ax, self.shape, self.place, self.on_ready = axes, shard_shapes, place, on_ready
        self.stage, self.buf, self.slot, self.passes = {}, {}, {}, {}
        self.ready = set()
        self.grads = {}
        token = None
        for groups in self.STARTS:
            order = [nm for g in groups for nm in self.GROUPS[g][0]]
            send, recv, bufs, token = _ag_start([seed(nm, token) for nm in order], *self._meta(order))
            self.buf.update(zip(order, bufs))
            self.slot.update({nm: (send, recv, k) for k, nm in enumerate(order)})
        self.token = self.latest = token
        self.mark("start", token)

    def _meta(self, names):
        return ([self.ax[nm] for nm in names], [self.shape[nm] for nm in names], [nm == "conv_w" for nm in names])

    def mark(self, tag, value):
        for g, (names, mid, wait) in enumerate(self.GROUPS):
            if tag == mid:
                send, recv, _ = self.slot[names[0]]
                fs, fr, bufs, self.latest = _ag_mid([self.buf[nm] for nm in names], [self.slot[nm][2] for nm in names],
                                                    *self._meta(names), send, recv, value, "ag_mid_%d" % g)
                self.buf.update(zip(names, bufs))
                self.passes[g] = (fs, fr)
            if tag == wait:
                fs, fr = self.passes[g]
                bufs = _ag_wait([self.buf[nm] for nm in names], *self._meta(names), fs, fr, value, "ag_wait_%d" % g)
                self.buf.update(zip(names, bufs))
                self.ready.update(names)
        for g, (names, send, total, finish) in enumerate(self.REDUCE):
            st = self.stage.get(g)
            if st is None:
                continue
            ng = len(names)
            if tag == send and st["at"] == "pair":
                arrs = _exchange_wait("rs_pair_wait_%d" % g, st["arrs"], *st["sems"], st["plan"], value)
                parts = [_pair_add(arrs[k], arrs[ng + k], self.ax[nm], self.shape[nm], self.place, "pair_add_" + nm)
                         for k, nm in enumerate(names)]
                plan, nsem = _plan_chip(ng)
                ss, rs, arrs, self.latest = _exchange_start(
                    "rs_chip_start_%d" % g, parts + [lax.empty(p.shape, p.dtype) for p in parts], nsem, plan)
                self.stage[g] = dict(at="chip", arrs=arrs, sems=(ss, rs), plan=plan)
            elif tag == total and st["at"] == "chip":
                arrs = _exchange_wait("rs_chip_wait_%d" % g, st["arrs"], *st["sems"], st["plan"], value)
                halves = [_chip_add(arrs[k], arrs[ng + k], self.place, "chip_add_" + nm) for k, nm in enumerate(names)]
                plan, nsem = _plan_gather(ng)
                ss, rs, arrs, self.latest = _exchange_start("rs_gather_start_%d" % g, halves, nsem, plan)
                self.stage[g] = dict(at="gather", arrs=arrs, sems=(ss, rs), plan=plan)
            elif tag == finish and st["at"] == "gather":
                arrs = _exchange_wait("rs_gather_wait_%d" % g, st["arrs"], *st["sems"], st["plan"], value)
                self.stage[g] = dict(at="done")
                for nm, shard in zip(names, arrs):
                    self.on_ready(nm, shard)
        return value

    def get(self, name):
        assert name in self.ready, name
        return self.buf[name]

    def grad(self, name, g):
        self.grads[name] = g
        for gi, (names, _, _, _) in enumerate(self.REDUCE):
            if name == names[-1]:
                gs = [self.grads[nm] for nm in names]
                plan, nsem = _plan_pair(len(names), [self.ax[nm] for nm in names], [self.shape[nm] for nm in names])
                ss, rs, arrs, self.latest = _exchange_start(
                    "rs_pair_start_%d" % gi, gs + [lax.empty(a.shape, a.dtype) for a in gs], nsem, plan)
                self.stage[gi] = dict(at="pair", arrs=arrs, sems=(ss, rs), plan=plan)
                g = self.latest
        self.mark("grad:" + name, g)

    def dep(self):
        return self.latest


def _exchange_start(name, arrays, nsem, plan):
    n = len(arrays)

    def body(*refs):
        send_sems, recv_sems, token = refs[n], refs[n + 1], refs[2 * n + 2]
        sends, _ = plan(refs[:n])
        for k, (src, dst, dev) in enumerate(sends):
            _remote(src, dst, send_sems.at[k], recv_sems.at[k], dev).start()
        token[...] = jnp.zeros_like(token)

    res = pl.pallas_call(
        body, name=name,
        out_shape=(pltpu.SemaphoreType.DMA((nsem,)), pltpu.SemaphoreType.DMA((nsem,)),
                   *[pltpu.HBM(a.shape, a.dtype) for a in arrays], _sds((8, LANES), F32)),
        in_specs=[HBM] * n, out_specs=(SEM, SEM, *[HBM] * n, pl.BlockSpec(memory_space=pltpu.VMEM)),
        input_output_aliases={i: 2 + i for i in range(n)},
        compiler_params=pltpu.CompilerParams(has_side_effects=EFFECT),
    )(*[_in_hbm(a) for a in arrays])
    return res[0], res[1], list(res[2:2 + n]), res[2 + n]


def _exchange_wait(name, arrays, send_sems, recv_sems, plan, after):
    n = len(arrays)

    def body(*refs):
        s_in, r_in = refs[n], refs[n + 1]
        sends, recvs = plan(refs[:n])
        for k, land in enumerate(recvs):
            _remote(land, land, s_in.at[k], r_in.at[k], sends[k][2]).wait_recv()
        for k, (src, _, dev) in enumerate(sends):
            _remote(src, src, s_in.at[k], r_in.at[k], dev).wait_send()

    res = pl.pallas_call(
        body, name=name,
        out_shape=tuple(pltpu.HBM(a.shape, a.dtype) for a in arrays),
        in_specs=[HBM] * n + [SEM, SEM, ANY], out_specs=tuple([HBM] * n),
        input_output_aliases={i: i for i in range(n)},
        compiler_params=pltpu.CompilerParams(has_side_effects=EFFECT),
    )(*arrays, send_sems, recv_sems, after)
    return list(res)


def _plan_pair(n, axes, shard_shapes):
    def plan(refs):
        g, ra = refs[:n], refs[n:]
        x, y, c, _ = _place()
        sib = (x, y, 1 - c)

        def pieces(ref, i, h):
            if axes[i] == 1:
                return [_half(ref, h)]
            return [_window(ref, 0, shard_shapes[i], s, h) for s in range(N_CHIPS)]

        sends, recvs = [], []
        for i in range(n):
            sends += [(src, dst, sib) for src, dst in zip(pieces(g[i], i, 1 - c), pieces(ra[i], i, 1 - c))]
            recvs += pieces(ra[i], i, c)
        return sends, recvs

    return plan, sum(1 if ax == 1 else N_CHIPS for ax in axes)


def _plan_chip(n):
    def plan(refs):
        p, rc = refs[:n], refs[n:]
        x, y, c, chips = _place()
        me = 2 * x + y
        sends, recvs = [], []
        for i in range(n):
            for chip in chips:
                cj = 2 * chip[0] + chip[1]
                sends.append((p[i].at[cj], rc[i].at[me], (*chip, c)))
                recvs.append(rc[i].at[cj])
        return sends, recvs

    return plan, 3 * n


def _plan_gather(n):
    def plan(refs):
        x, y, c, _ = _place()
        sib = (x, y, 1 - c)
        return ([(_half(r, c), _half(r, c), sib) for r in refs], [_half(r, 1 - c) for r in refs])

    return plan, n


def _pair_add(g, ra, ax, shard_shape, place, name):
    sr, sc = shard_shape
    hr = sr // 2
    wc = sc
    tr = _pick(hr, (256, 352, 128, 64, 32, 16))
    nr = hr // tr

    def body(p_ref, a_ref, b_ref, o_ref):
        o_ref[...] = (a_ref[...].astype(F32) + b_ref[...].astype(F32)).astype(o_ref.dtype)

    if ax == 1:
        src = pl.BlockSpec((tr, wc), lambda s, r, p_ref: (p_ref[0] * nr + r, s))
    else:
        src = pl.BlockSpec((tr, wc), lambda s, r, p_ref: (s * 2 * nr + p_ref[0] * nr + r, 0))
    return pl.pallas_call(
        body, name=name,
        grid_spec=pltpu.PrefetchScalarGridSpec(
            num_scalar_prefetch=1, grid=(N_CHIPS, nr), in_specs=[src, src],
            out_specs=pl.BlockSpec((None, tr, wc), lambda s, r, p_ref: (s, r, 0))),
        out_shape=_sds((N_CHIPS, hr, wc), BF16),
        compiler_params=_cparams("parallel", "parallel"),
    )(place, g, ra)


def _chip_add(part, rc, place, name):
    _, hr, wc = rc.shape
    tr = _pick(hr, (256, 352, 128, 64, 32, 16))
    nr = hr // tr

    def body(p_ref, own_ref, r1_ref, r2_ref, r3_ref, o_ref):
        acc = own_ref[...].astype(F32)
        for r_ref in (r1_ref, r2_ref, r3_ref):
            acc = acc + r_ref[...].astype(F32)
        o_ref[...] = acc

    def slot(k):
        return pl.BlockSpec((None, tr, wc), lambda r, p_ref: ((p_ref[1] + k) % N_CHIPS, r, 0))

    return pl.pallas_call(
        body, name=name,
        grid_spec=pltpu.PrefetchScalarGridSpec(
            num_scalar_prefetch=1, grid=(nr,), in_specs=[slot(0), slot(1), slot(2), slot(3)],
            out_specs=pl.BlockSpec((tr, wc), lambda r, p_ref: (p_ref[0] * nr + r, 0))),
        out_shape=_sds((2 * hr, wc), F32),
        compiler_params=_cparams("parallel"),
    )(place, part, rc, rc, rc)


N_DEV = 8


def _all_reduce_small(buf):
    r, cdim = buf.shape

    def body(x_ref, o_ref, land, send_sems, recv_sems):
        x, y, c, _ = _place()
        me = 4 * x + 2 * y + c
        land[me] = x_ref[...]
        sends = []
        for k in range(1, N_DEV):
            kx, ky, kc = (k >> 2) & 1, (k >> 1) & 1, k & 1
            peer = (1 - x if kx else x, 1 - y if ky else y, 1 - c if kc else c)
            cp = _remote(x_ref, land.at[me], send_sems.at[k - 1], recv_sems.at[k - 1], peer)
            cp.start()
            sends.append(cp)
        for k in range(1, N_DEV):
            kx, ky, kc = (k >> 2) & 1, (k >> 1) & 1, k & 1
            peer = (1 - x if kx else x, 1 - y if ky else y, 1 - c if kc else c)
            pidx = 4 * peer[0] + 2 * peer[1] + peer[2]
            _remote(land.at[pidx], land.at[pidx], send_sems.at[k - 1], recv_sems.at[k - 1], peer).wait_recv()
        for cp in sends:
            cp.wait_send()
        acc = land[0]
        for dev in range(1, N_DEV):
            acc = acc + land[dev]
        o_ref[...] = acc

    vm = pl.BlockSpec(memory_space=pltpu.VMEM)
    return pl.pallas_call(
        body, name="all_reduce_small", in_specs=[vm], out_specs=vm, out_shape=_sds((r, cdim), F32),
        scratch_shapes=[pltpu.VMEM((N_DEV, r, cdim), F32), pltpu.SemaphoreType.DMA((N_DEV - 1,)),
                        pltpu.SemaphoreType.DMA((N_DEV - 1,))],
    )(buf)


SMALL_ROWS = 16


def kernel(x, mem, g_mix, w_in, conv_w, attn_sinks, w_attn_proj, w_conv_proj, w_mix_out, g_xattn, g_mem, w_xq, w_xkv, w_xo, g_ffn, w_ffn_in, w_ffn_out, g_final, loss_target, m_g_mix, m_w_in, m_conv_w, m_attn_sinks, m_w_attn_proj, m_w_conv_proj, m_w_mix_out, m_g_xattn, m_g_mem, m_w_xq, m_w_xkv, m_w_xo, m_g_ffn, m_w_ffn_in, m_w_ffn_out, m_g_final, v_g_mix, v_w_in, v_conv_w, v_attn_sinks, v_w_attn_proj, v_w_conv_proj, v_w_mix_out, v_g_xattn, v_g_mem, v_w_xq, v_w_xkv, v_w_xo, v_g_ffn, v_w_ffn_in, v_w_ffn_out, v_g_final):
    w = dict(g_mix=g_mix, w_in=w_in[0], conv_w=conv_w[0], attn_sinks=attn_sinks, w_attn_proj=w_attn_proj[0],
             w_conv_proj=w_conv_proj[0], w_mix_out=w_mix_out[0], g_xattn=g_xattn, g_mem=g_mem, w_xq=w_xq[0],
             w_xkv=w_xkv[0], w_xo=w_xo[0], g_ffn=g_ffn, w_ffn_in=w_ffn_in[0], w_ffn_out=w_ffn_out[0],
             g_final=g_final[None])
    m = dict(g_mix=m_g_mix, w_in=m_w_in[0], conv_w=m_conv_w[0], attn_sinks=m_attn_sinks,
             w_attn_proj=m_w_attn_proj[0], w_conv_proj=m_w_conv_proj[0], w_mix_out=m_w_mix_out[0],
             g_xattn=m_g_xattn, g_mem=m_g_mem, w_xq=m_w_xq[0], w_xkv=m_w_xkv[0], w_xo=m_w_xo[0], g_ffn=m_g_ffn,
             w_ffn_in=m_w_ffn_in[0], w_ffn_out=m_w_ffn_out[0], g_final=m_g_final[None])
    v = dict(g_mix=v_g_mix, w_in=v_w_in[0], conv_w=v_conv_w[0], attn_sinks=v_attn_sinks,
             w_attn_proj=v_w_attn_proj[0], w_conv_proj=v_w_conv_proj[0], w_mix_out=v_w_mix_out[0],
             g_xattn=v_g_xattn, g_mem=v_g_mem, w_xq=v_w_xq[0], w_xkv=v_w_xkv[0], w_xo=v_w_xo[0], g_ffn=v_g_ffn,
             w_ffn_in=v_w_ffn_in[0], w_ffn_out=v_w_ffn_out[0], g_final=v_g_final[None])
    names = [nm for nm, _ in BIG]
    axes = [ax for _, ax in BIG]
    d = x.shape[2]
    cw = w["conv_w"].shape[1] * N_CHIPS
    chip = (2 * lax.axis_index("x") + lax.axis_index("y")).astype(jnp.int32)
    place = jnp.stack([lax.axis_index("c").astype(jnp.int32), chip])
    shard_shapes = [w[nm].shape for nm in names]

    def seed(nm, token):
        me1 = chip.reshape(1)
        if token is not None:
            me1 = me1 + token[0, 0].astype(jnp.int32)
        if nm == "conv_w":
            return _cast_to_full(w[nm], 1, me1, "place_conv_w", F32)
        return _cast_to_full(w[nm], dict(BIG)[nm], me1, "cast_" + nm)

    upd = {}

    def on_ready(nm, shard):
        upd[nm] = _adamw(w[nm], shard, m[nm], v[nm], "adamw_" + nm)
        grads[nm] = upd[nm][3]

    grads = {}
    wts = _Schedule(seed, dict(zip(names + ["conv_w"], axes + [1])),
                    dict(zip(names + ["conv_w"], shard_shapes + [w["conv_w"].shape])), place, on_ready)
    aw, cw = w["w_attn_proj"].shape[0], w["w_conv_proj"].shape[0]
    kvw = (w["w_in"].shape[1] * N_CHIPS - aw - 3 * cw - 2 * d) // 2
    grad_x, small = _local_step(
        x[0], mem[0], loss_target[0], w["g_mix"] + wts.token[0:1, 0:1], w["attn_sinks"], w["g_xattn"], w["g_mem"],
        w["g_ffn"], w["g_final"], (aw, cw, kvw), wts)

    pw = max(d, cw)

    def row(a):
        return jnp.pad(a, ((0, 0), (0, pw - a.shape[1])))

    gains = ("g_mix", "g_xattn", "g_mem", "g_ffn", "g_final")
    packed = jnp.concatenate(
        [row(small[nm]) for nm in gains] + [row(small["conv_w"]),
         row(jnp.concatenate([small["attn_sinks"], small["loss"]], axis=1)),
         jnp.zeros((SMALL_ROWS - 9, pw), F32)], axis=0)
    total = _all_reduce_small(packed)
    wts.mark("end", total)
    nsink = attn_sinks.shape[1]
    grads.update({nm: total[k:k + 1, :d] for k, nm in enumerate(gains)})
    grads.update(conv_w=lax.dynamic_slice(total, (5, chip * (cw // N_CHIPS)), (3, cw // N_CHIPS)),
                 attn_sinks=total[8:9, :nsink])
    loss = total[8, nsink]
    for nm in gains + ("conv_w", "attn_sinks"):
        upd[nm] = _adamw(w[nm], grads[nm], m[nm], v[nm], "adamw_" + nm)
    wts.mark("end2", upd["g_final"][0])

    order = ["g_mix", "w_in", "conv_w", "attn_sinks", "w_attn_proj", "w_conv_proj", "w_mix_out", "g_xattn", "g_mem",
             "w_xq", "w_xkv", "w_xo", "g_ffn", "w_ffn_in", "w_ffn_out", "g_final"]

    stacked = set(names) | {"conv_w"}

    def shaped(nm, a):
        if nm == "g_final":
            return a[0]
        return a[None] if nm in stacked else a

    outs = [loss, grad_x[None]]
    outs += [shaped(nm, grads[nm]) for nm in order]
    for k in range(3):
        outs += [shaped(nm, upd[nm][k]) for nm in order]
    return tuple(outs)
```

```python
import functools

import jax
import jax.numpy as jnp
from jax import lax
from jax.experimental import pallas as pl
from jax.experimental.pallas import tpu as pltpu

F32 = jnp.float32
BF16 = jnp.bfloat16

VMEM_LIMIT_BYTES = 56 * 1024 * 1024
LANES = 128
HEAD_DIM = 64
BLOCK = 128
X_HEAD_DIM = 128
ROPE_THETA = 10000.0
EPS = 1e-6
NEG = -1e30
ADAM_LR, ADAM_B1, ADAM_B2, ADAM_EPS, ADAM_WD, ADAM_STEP = 0.001, 0.9, 0.999, 1e-08, 0.01, 10
N_CHIPS = 4
MESH = pl.DeviceIdType.MESH
ANY = pl.BlockSpec(memory_space=pl.ANY)


def _pick(dim, prefs):
    for p in prefs:
        if dim % p == 0:
            return p
    return dim


def _cparams(*sem):
    return pltpu.CompilerParams(dimension_semantics=sem, vmem_limit_bytes=VMEM_LIMIT_BYTES)


def _sds(shape, dtype):
    return jax.ShapeDtypeStruct(shape, dtype)


def _sigmoid(v):
    return 1.0 / (1.0 + jnp.exp(-v))


MATMUL_VMEM_BUDGET = 46 * 1024 * 1024


def _tiles(mode, m, n):
    if mode == "tn" and m % 1024 != 0:
        return _pick(m, (512, 256, 128)), _pick(n, (1024, 512, 256, 128)), True
    return _pick(m, (1024, 512, 256, 128)), _pick(n, (512, 256, 128)), False


def _k_parts(m, n, k):
    tm, tn = _pick(m, (1024, 512, 256, 128)), _pick(n, (512, 256, 128))
    for parts in range(1, k // LANES + 1):
        if k % (parts * LANES) == 0 and 4 * (tm + tn) * (k // parts) + 16 * tm * tn <= MATMUL_VMEM_BUDGET:
            return parts
    return k // LANES


def _matmul(a, b, *, mode, out_dtype, name, res=None, dep=None, cols=None):
    if mode == "nn":
        (m, k), (k2, n) = a.shape, b.shape
    elif mode == "nt":
        (m, k), (n, k2) = (a.shape[-2], a.shape[-1] * (a.shape[0] if a.ndim == 3 else 1)), b.shape
    else:
        (k, m), (k2, n) = a.shape, (b.shape[-2], b.shape[-1] * (b.shape[0] if b.ndim == 3 else 1))
    assert k == k2, (a.shape, b.shape, mode)
    assert cols is None or mode != "nt"
    cols = cols or (0, n)
    parts = 1 if mode == "tn" else (a.shape[0] if a.ndim == 3 else _k_parts(m, cols[1], k))
    for p in range(parts):
        last = p == parts - 1
        res = _matmul_slice(a, b, mode=mode, out_dtype=out_dtype if last else F32, res=res, dep=dep, cols=cols,
                            kslice=(p, parts), name=name + ("_k%d" % p if parts > 1 else ""))
    return res


def _matmul_slice(a, b, *, mode, out_dtype, name, res, dep, kslice, cols):
    part, parts = kslice
    (m, k) = a.shape[-2:] if mode != "tn" else a.shape[::-1]
    col0, n = cols
    tk = k if a.ndim == 3 else k // parts
    tm, tn, swap = _tiles(mode, m, b.shape[2] if (mode == "tn" and b.ndim == 3) else n)
    while col0 % tn:
        tn //= 2
    joff = col0 // tn
    dims = {"nn": (((1,), (0,)), ((), ())), "nt": (((1,), (1,)), ((), ())), "tn": (((0,), (0,)), ((), ()))}[mode]
    has_res = res is not None
    has_dep = dep is not None

    def body(*refs):
        a_ref, b_ref = refs[0], refs[1]
        o_ref = refs[2 + has_res + has_dep]
        val = lax.dot_general(a_ref[...], b_ref[...], dims, preferred_element_type=F32)
        if has_res:
            val = val + refs[2][...]
        o_ref[...] = val.astype(o_ref.dtype)

    def spec(shape, f):
        if swap:
            return pl.BlockSpec(shape, lambda j, i: f(i, j))
        return pl.BlockSpec(shape, f)

    if mode == "tn":
        a_spec = spec((tk, tm), lambda i, j: (part, i))
    elif a.ndim == 3:
        a_spec = spec((None, tm, tk), lambda i, j: (part, i, 0))
    else:
        a_spec = spec((tm, tk), lambda i, j: (i, part))
    if mode == "nt":
        b_spec = spec((tn, tk), lambda i, j: (j, part))
    elif b.ndim == 3:
        per = b.shape[2] // tn
        b_spec = spec((None, tk, tn), lambda i, j: (j // per, part, j % per))
    else:
        b_spec = spec((tk, tn), lambda i, j: (part, joff + j))
    o_spec = spec((tm, tn), lambda i, j: (i, j))
    return pl.pallas_call(
        body,
        name=name,
        grid=(n // tn, m // tm) if swap else (m // tm, n // tn),
        in_specs=[a_spec, b_spec] + ([o_spec] if has_res else []) + ([ANY] if has_dep else []),
        out_specs=o_spec,
        out_shape=_sds((m, n), out_dtype),
        compiler_params=_cparams("parallel", "parallel"),
    )(*([a, b] + ([res] if has_res else []) + ([dep] if has_dep else [])))


def _rms_fwd(x, g, name):
    t, d = x.shape
    tm = _pick(t, (512, 256, 128))

    def body(x_ref, g_ref, o_ref):
        xf = x_ref[...]
        r = lax.rsqrt(jnp.mean(xf * xf, axis=-1, keepdims=True) + EPS)
        o_ref[...] = (xf * r * g_ref[...]).astype(o_ref.dtype)

    row = pl.BlockSpec((tm, d), lambda i: (i, 0))
    return pl.pallas_call(
        body, name=name, grid=(t // tm,),
        in_specs=[row, pl.BlockSpec((1, d), lambda i: (0, 0))],
        out_specs=row, out_shape=_sds((t, d), BF16),
        compiler_params=_cparams("parallel"),
    )(x, g)


def _rms_bwd_math(xf, g, du):
    r = lax.rsqrt(jnp.mean(xf * xf, axis=-1, keepdims=True) + EPS)
    xh = xf * r
    gdy = g * du
    dx = r * (gdy - xh * jnp.mean(gdy * xh, axis=-1, keepdims=True))
    dg = jnp.sum(du * xh, axis=0, keepdims=True)
    return dx, dg


def _rms_bwd(x, g, du, dh, name):
    t, d = x.shape
    tm = _pick(t, (256, 128))
    has_dh = dh is not None

    def body(*refs):
        x_ref, g_ref, du_ref = refs[0], refs[1], refs[2]
        o_ref, ob_ref, dg_ref = refs[3 + has_dh:]
        dx, dg = _rms_bwd_math(x_ref[...], g_ref[...], du_ref[...].astype(F32))
        if has_dh:
            dx = dx + refs[3][...]
        o_ref[...] = dx
        ob_ref[...] = dx.astype(BF16)

        @pl.when(pl.program_id(0) == 0)
        def _():
            dg_ref[...] = dg

        @pl.when(pl.program_id(0) > 0)
        def _():
            dg_ref[...] += dg

    row = pl.BlockSpec((tm, d), lambda i: (i, 0))
    vec = pl.BlockSpec((1, d), lambda i: (0, 0))
    return pl.pallas_call(
        body, name=name, grid=(t // tm,),
        in_specs=[row, vec, row] + ([row] if has_dh else []),
        out_specs=[row, row, vec],
        out_shape=[_sds((t, d), F32), _sds((t, d), BF16), _sds((1, d), F32)],
        compiler_params=_cparams("arbitrary"),
    )(*([x, g, du] + ([dh] if has_dh else [])))


def _loss_head(h, g, tgt):
    t, d = h.shape
    tm = _pick(t, (256, 128))

    def body(h_ref, g_ref, t_ref, o_ref, ob_ref, dg_ref, l_ref):
        xf = h_ref[...]
        gv = g_ref[...]
        r = lax.rsqrt(jnp.mean(xf * xf, axis=-1, keepdims=True) + EPS)
        err = xf * r * gv - t_ref[...]
        part = 0.5 * jnp.sum(jnp.mean(err * err, axis=-1, keepdims=True), axis=0, keepdims=True)
        dx, dg = _rms_bwd_math(xf, gv, err * (1.0 / d))
        o_ref[...] = dx
        ob_ref[...] = dx.astype(BF16)
        lrow = jnp.broadcast_to(part, (1, LANES))

        @pl.when(pl.program_id(0) == 0)
        def _():
            dg_ref[...] = dg
            l_ref[...] = lrow

        @pl.when(pl.program_id(0) > 0)
        def _():
            dg_ref[...] += dg
            l_ref[...] += lrow

    row = pl.BlockSpec((tm, d), lambda i: (i, 0))
    vec = pl.BlockSpec((1, d), lambda i: (0, 0))
    return pl.pallas_call(
        body, name="loss_head", grid=(t // tm,),
        in_specs=[row, vec, row],
        out_specs=[row, row, vec, pl.BlockSpec((1, LANES), lambda i: (0, 0))],
        out_shape=[_sds((t, d), F32), _sds((t, d), BF16), _sds((1, d), F32), _sds((1, LANES), F32)],
        compiler_params=_cparams("arbitrary"),
    )(h, g, tgt)


def _rope_tables(t):
    half = HEAD_DIM // 2
    inv_freq = ROPE_THETA ** (-jnp.arange(half, dtype=F32) / half)
    ang = jnp.arange(t, dtype=F32)[:, None] * inv_freq[None, :]
    cos = jnp.cos(ang)
    sin = jnp.sin(ang)
    reps = LANES // HEAD_DIM
    cos_t = jnp.tile(jnp.concatenate([cos, cos], axis=1), (1, reps))
    sin_t = jnp.tile(jnp.concatenate([-sin, sin], axis=1), (1, reps))
    return cos_t, sin_t


def _rope(v, cos, sin):
    w = v.shape[1]
    c = jnp.tile(cos, (1, w // LANES))
    s = jnp.tile(sin, (1, w // LANES))
    lane = lax.broadcasted_iota(jnp.int32, v.shape, 1)
    first = (lane % HEAD_DIM) < (HEAD_DIM // 2)
    partner = jnp.where(first, pltpu.roll(v, w - HEAD_DIM // 2, 1), pltpu.roll(v, HEAD_DIM // 2, 1))
    return v * c + partner * s


def _heads(v, count):
    return jnp.concatenate([v[:, i * HEAD_DIM:(i + 1) * HEAD_DIM] for i in range(count)], axis=0)


def _unheads(v, count):
    r = v.shape[0] // count
    return jnp.concatenate([v[g * r:(g + 1) * r] for g in range(count)], axis=1)


def _rope_qkv(proj, cos, sin, aw, kvw):
    t = proj.shape[0]
    nkv = kvw // HEAD_DIM
    koff = aw // kvw
    tm = _pick(t, (256, 128))

    def body(q_ref, k_ref, v_ref, c_ref, s_ref, qo_ref, ko_ref, vo_ref):
        c, s = c_ref[...], s_ref[...]
        qo_ref[...] = _rope(q_ref[...].astype(F32), c, s).astype(BF16)
        k = _rope(k_ref[...].astype(F32), c, s).astype(BF16)
        v = v_ref[...].astype(BF16)
        for h in range(nkv):
            ko_ref[h] = k[:, h * HEAD_DIM:(h + 1) * HEAD_DIM]
            vo_ref[h] = v[:, h * HEAD_DIM:(h + 1) * HEAD_DIM]

    def row(w, j):
        return pl.BlockSpec((tm, w), lambda i: (i, j))

    hm = pl.BlockSpec((nkv, tm, HEAD_DIM), lambda i: (0, i, 0))
    return pl.pallas_call(
        body, name="rope_qkv", grid=(t // tm,),
        in_specs=[row(aw, 0), row(kvw, koff), row(kvw, koff + 1), row(LANES, 0), row(LANES, 0)],
        out_specs=[row(aw, 0), hm, hm],
        out_shape=[_sds((t, aw), BF16), _sds((nkv, t, HEAD_DIM), BF16), _sds((nkv, t, HEAD_DIM), BF16)],
        compiler_params=_cparams("parallel"),
    )(proj, proj, proj, cos, sin)


def _attn_probs(qs, kb, n, h, qpk, sinks_ref):
    s = lax.dot_general(qs, kb, (((1,), (1,)), ((), ())), preferred_element_type=F32) * (HEAD_DIM ** -0.5)
    qi = lax.broadcasted_iota(jnp.int32, (BLOCK, 2 * BLOCK), 0)
    kc = lax.broadcasted_iota(jnp.int32, (BLOCK, 2 * BLOCK), 1)
    valid = (kc > qi) & (kc <= qi + BLOCK) & ((kc >= BLOCK) | (n > 0))
    bias = jnp.tile(jnp.where(valid, 0.0, NEG).astype(F32), (qpk, 1))
    s = s + bias
    rowg = lax.broadcasted_iota(jnp.int32, (qpk * BLOCK, 1), 0) // BLOCK
    sink = jnp.zeros((qpk * BLOCK, 1), F32)
    for g in range(qpk):
        sink = jnp.where(rowg == g, sinks_ref[0, h * qpk + g], sink)
    m = jnp.maximum(jnp.max(s, axis=-1, keepdims=True), sink)
    e = jnp.exp(s - m)
    es = jnp.exp(sink - m)
    inv = 1.0 / (jnp.sum(e, axis=-1, keepdims=True) + es)
    return e * inv, es * inv, rowg


HEADS_PER_STEP = 4


def _attn_specs(qw, hp):
    def head(f):
        return pl.BlockSpec((hp, BLOCK, HEAD_DIM), lambda n, h: (h, f(n), 0))

    cur = lambda n: n
    prev = lambda n: jnp.maximum(n - 1, 0)
    return [pl.BlockSpec((BLOCK, hp * qw), lambda n, h: (n, h)), head(cur), head(prev), head(cur), head(prev),
            pl.BlockSpec(memory_space=pltpu.SMEM)]


def _attn_fwd(q_r, k_r, v_h, sinks):
    t, aw = q_r.shape
    nkv = k_r.shape[0]
    qpk = aw // (nkv * HEAD_DIM)
    qw = qpk * HEAD_DIM
    hp = HEADS_PER_STEP if nkv % HEADS_PER_STEP == 0 else 1

    def body(q_ref, kc_ref, kp_ref, vc_ref, vp_ref, sinks_ref, o_ref):
        n, hg = pl.program_id(0), pl.program_id(1)
        outs = []
        for j in range(hp):
            kb = jnp.concatenate([kp_ref[j], kc_ref[j]], axis=0)
            vb = jnp.concatenate([vp_ref[j], vc_ref[j]], axis=0)
            qs = _heads(q_ref[:, j * qw:(j + 1) * qw], qpk)
            p, _, _ = _attn_probs(qs, kb, n, hg * hp + j, qpk, sinks_ref)
            outs.append(_unheads(jnp.dot(p.astype(BF16), vb, preferred_element_type=F32), qpk))
        o_ref[...] = jnp.concatenate(outs, axis=1).astype(o_ref.dtype)

    return pl.pallas_call(
        body, name="attn_fwd", grid=(t // BLOCK, nkv // hp),
        in_specs=_attn_specs(qw, hp),
        out_specs=pl.BlockSpec((BLOCK, hp * qw), lambda n, h: (n, h)),
        out_shape=_sds((t, aw), BF16),
        compiler_params=_cparams("parallel", "parallel"),
    )(q_r, k_r, k_r, v_h, v_h, sinks)


def _attn_bwd(q_r, k_r, v_h, sinks, o, do):
    t, aw = q_r.shape
    nkv = k_r.shape[0]
    qpk = aw // (nkv * HEAD_DIM)
    qw = qpk * HEAD_DIM
    hp = HEADS_PER_STEP if nkv % HEADS_PER_STEP == 0 else 1
    scale = HEAD_DIM ** -0.5

    def body(q_ref, kc_ref, kp_ref, vc_ref, vp_ref, sinks_ref, o_ref, do_ref,
             dq_ref, dkc_ref, dkp_ref, dvc_ref, dvp_ref, ds_ref):
        n, hg = pl.program_id(0), pl.program_id(1)
        lane = lax.broadcasted_iota(jnp.int32, (8, LANES), 1)
        row0 = lax.broadcasted_iota(jnp.int32, (8, LANES), 0) == 0
        dsink = jnp.zeros((8, LANES), F32)
        dqs = []
        for j in range(hp):
            h = hg * hp + j
            cols = slice(j * qw, (j + 1) * qw)
            kb = jnp.concatenate([kp_ref[j], kc_ref[j]], axis=0)
            vb = jnp.concatenate([vp_ref[j], vc_ref[j]], axis=0)
            qs = _heads(q_ref[:, cols], qpk)
            dos = _heads(do_ref[:, cols], qpk)
            p, psink, rowg = _attn_probs(qs, kb, n, h, qpk, sinks_ref)
            pb = p.astype(BF16)
            delta = jnp.sum(dos.astype(F32) * _heads(o_ref[:, cols], qpk).astype(F32), axis=-1, keepdims=True)
            dv = lax.dot_general(pb, dos, (((0,), (0,)), ((), ())), preferred_element_type=F32)
            dp = lax.dot_general(dos, vb, (((1,), (1,)), ((), ())), preferred_element_type=F32)
            dsc = (p * (dp - delta)).astype(BF16)
            dqs.append(_unheads(jnp.dot(dsc, kb, preferred_element_type=F32) * scale, qpk))
            dk = lax.dot_general(dsc, qs, (((0,), (0,)), ((), ())), preferred_element_type=F32) * scale
            dkp_ref[j] = dk[:BLOCK]
            dkc_ref[j] = dk[BLOCK:]
            dvp_ref[j] = dv[:BLOCK]
            dvc_ref[j] = dv[BLOCK:]
            sink_term = psink * delta
            for g in range(qpk):
                val = -jnp.sum(jnp.where(rowg == g, sink_term, 0.0))
                dsink = jnp.where(row0 & (lane == h * qpk + g), val, dsink)
        dq_ref[...] = jnp.concatenate(dqs, axis=1).astype(dq_ref.dtype)
        first = (n == 0) & (hg == 0)

        @pl.when(first)
        def _():
            ds_ref[...] = dsink

        @pl.when(jnp.logical_not(first))
        def _():
            ds_ref[...] += dsink

    qblk = pl.BlockSpec((BLOCK, hp * qw), lambda n, h: (n, h))
    kvblk = pl.BlockSpec((hp, BLOCK, HEAD_DIM), lambda n, h: (h, n, 0))
    return pl.pallas_call(
        body, name="attn_bwd", grid=(t // BLOCK, nkv // hp),
        in_specs=_attn_specs(qw, hp) + [qblk, qblk],
        out_specs=[qblk, kvblk, kvblk, kvblk, kvblk, pl.BlockSpec((8, LANES), lambda n, h: (0, 0))],
        out_shape=[_sds((t, aw), BF16)] + [_sds((nkv, t, HEAD_DIM), F32)] * 4 + [_sds((8, LANES), F32)],
        compiler_params=_cparams("arbitrary", "arbitrary"),
    )(q_r, k_r, k_r, v_h, v_h, sinks, o, do)


HALO = 16


def _shift_down(v, k, halo):
    rows = lax.broadcasted_iota(jnp.int32, v.shape, 0)
    out = pltpu.roll(v, k, 0)
    for r in range(k):
        out = jnp.where(rows == r, halo[HALO - k + r:HALO - k + r + 1, :], out)
    return out


def _shift_up(v, k, halo):
    tm = v.shape[0]
    rows = lax.broadcasted_iota(jnp.int32, v.shape, 0)
    out = pltpu.roll(v, tm - k, 0)
    for r in range(k):
        out = jnp.where(rows == tm - k + r, halo[r:r + 1, :], out)
    return out


def _conv_fwd(proj, conv_w, zoff, cw, cb):
    t = proj.shape[0]
    tm = _pick(t, (512, 256, 128))
    zb, nb = zoff // cb, cw // cb
    hb = tm // HALO

    def body(z_ref, gb_ref, gc_ref, zp_ref, gcp_ref, w_ref, o_ref):
        i = pl.program_id(0)
        cz = gc_ref[...].astype(F32) * z_ref[...].astype(F32)
        czp = gcp_ref[...].astype(F32) * zp_ref[...].astype(F32) * (i > 0).astype(F32)
        w = w_ref[...]
        y = w[0:1] * _shift_down(cz, 2, czp) + w[1:2] * _shift_down(cz, 1, czp) + w[2:3] * cz
        o_ref[...] = (gb_ref[...].astype(F32) * y).astype(o_ref.dtype)

    def col(k):
        return pl.BlockSpec((tm, cb), lambda i, j: (i, zb + k * nb + j))

    def halo(k):
        return pl.BlockSpec((HALO, cb), lambda i, j: (jnp.maximum(i * hb - 1, 0), zb + k * nb + j))

    return pl.pallas_call(
        body, name="conv_fwd", grid=(t // tm, nb),
        in_specs=[col(0), col(1), col(2), halo(0), halo(2), pl.BlockSpec((3, cb), lambda i, j: (0, j))],
        out_specs=pl.BlockSpec((tm, cb), lambda i, j: (i, j)),
        out_shape=_sds((t, cw), BF16),
        compiler_params=_cparams("parallel", "parallel"),
    )(proj, proj, proj, proj, proj, conv_w)


def _conv_bwd(proj, conv_w, dco, zoff, cw, cb):
    t = proj.shape[0]
    tm = _pick(t, (512, 256, 128))
    zb, nb = zoff // cb, cw // cb
    hb = tm // HALO
    nt = t // tm

    def body(z_ref, gb_ref, gc_ref, zp_ref, gcp_ref, gbn_ref, w_ref, d_ref, dn_ref, dz_ref, dgb_ref, dgc_ref, dw_ref):
        i = pl.program_id(1)
        z, gb, gc = z_ref[...].astype(F32), gb_ref[...].astype(F32), gc_ref[...].astype(F32)
        d = d_ref[...].astype(F32)
        cz = gc * z
        czp = gcp_ref[...].astype(F32) * zp_ref[...].astype(F32) * (i > 0).astype(F32)
        w = w_ref[...]
        cz1 = _shift_down(cz, 1, czp)
        cz2 = _shift_down(cz, 2, czp)
        y = w[0:1] * cz2 + w[1:2] * cz1 + w[2:3] * cz
        dgb_ref[...] = (d * y).astype(dgb_ref.dtype)
        dy = d * gb
        dyn = dn_ref[...].astype(F32) * gbn_ref[...].astype(F32) * (i < nt - 1).astype(F32)
        dcz = w[2:3] * dy + w[1:2] * _shift_up(dy, 1, dyn) + w[0:1] * _shift_up(dy, 2, dyn)
        dgc_ref[...] = (dcz * z).astype(dgc_ref.dtype)
        dz_ref[...] = (dcz * gc).astype(dz_ref.dtype)
        rows = lax.broadcasted_iota(jnp.int32, (8, cb), 0)
        dw = jnp.zeros((8, cb), F32)
        for r, tap in enumerate((cz2, cz1, cz)):
            dw = jnp.where(rows == r, jnp.sum(dy * tap, axis=0, keepdims=True), dw)

        @pl.when(i == 0)
        def _():
            dw_ref[...] = dw

        @pl.when(i > 0)
        def _():
            dw_ref[...] += dw

    def col(k):
        return pl.BlockSpec((tm, cb), lambda j, i: (i, zb + k * nb + j))

    def halo_prev(k):
        return pl.BlockSpec((HALO, cb), lambda j, i: (jnp.maximum(i * hb - 1, 0), zb + k * nb + j))

    own = pl.BlockSpec((tm, cb), lambda j, i: (i, j))
    nxt = lambda i: jnp.minimum((i + 1) * hb, t // HALO - 1)
    return pl.pallas_call(
        body, name="conv_bwd", grid=(nb, nt),
        in_specs=[col(0), col(1), col(2), halo_prev(0), halo_prev(2),
                  pl.BlockSpec((HALO, cb), lambda j, i: (nxt(i), zb + nb + j)),
                  pl.BlockSpec((3, cb), lambda j, i: (0, j)), own,
                  pl.BlockSpec((HALO, cb), lambda j, i: (nxt(i), j))],
        out_specs=[own, own, own, pl.BlockSpec((8, cb), lambda j, i: (0, j))],
        out_shape=[_sds((t, cw), BF16)] * 3 + [_sds((8, cw), F32)],
        compiler_params=_cparams("parallel", "arbitrary"),
    )(proj, proj, proj, proj, proj, proj, conv_w, dco, dco)


def _yconv_merge(conv_o, w_cp, ya, proj, goff, cb, dep, name):
    t, k = conv_o.shape
    d = w_cp.shape[1]
    tm = _pick(t, (1024, 512, 256, 128))
    gb_, nb = goff // cb, d // cb
    deps = [] if dep is None else [dep]

    def body(a_ref, w_ref, ya_ref, ga_ref, gc_ref, *rest):
        yc_ref, o_ref = rest[len(deps):]
        yc = jnp.dot(a_ref[...], w_ref[...], preferred_element_type=F32)
        f = lambda r: r[...].astype(F32)
        yc_ref[...] = yc.astype(yc_ref.dtype)
        o_ref[...] = (_sigmoid(f(ga_ref)) * f(ya_ref) + _sigmoid(f(gc_ref)) * yc).astype(o_ref.dtype)

    own = pl.BlockSpec((tm, cb), lambda i, j: (i, j))
    return pl.pallas_call(
        body, name=name, grid=(t // tm, nb),
        in_specs=[pl.BlockSpec((tm, k), lambda i, j: (i, 0)), pl.BlockSpec((k, cb), lambda i, j: (0, j)), own,
                  pl.BlockSpec((tm, cb), lambda i, j: (i, gb_ + j)),
                  pl.BlockSpec((tm, cb), lambda i, j: (i, gb_ + nb + j))] + [ANY] * len(deps),
        out_specs=[own, own], out_shape=[_sds((t, d), BF16)] * 2,
        compiler_params=_cparams("parallel", "parallel"),
    )(conv_o, w_cp, ya, proj, proj, *deps)


def _dmerged_split(dh, w_mo, proj, ya, yc, goff, cb, dep, name):
    t, k = dh.shape
    d = w_mo.shape[0]
    tm = _pick(t, (1024, 512, 256, 128))
    gb_, nb = goff // cb, d // cb
    deps = [] if dep is None else [dep]

    def body(a_ref, w_ref, ga_ref, gc_ref, ya_ref, yc_ref, *rest):
        dya_ref, dyc_ref, dga_ref, dgc_ref = rest[len(deps):]
        dmv = lax.dot_general(a_ref[...], w_ref[...], (((1,), (1,)), ((), ())), preferred_element_type=F32)
        sa = _sigmoid(ga_ref[...].astype(F32))
        sc = _sigmoid(gc_ref[...].astype(F32))
        dya_ref[...] = (dmv * sa).astype(BF16)
        dyc_ref[...] = (dmv * sc).astype(BF16)
        dga_ref[...] = (dmv * ya_ref[...].astype(F32) * sa * (1.0 - sa)).astype(BF16)
        dgc_ref[...] = (dmv * yc_ref[...].astype(F32) * sc * (1.0 - sc)).astype(BF16)

    own = pl.BlockSpec((tm, cb), lambda i, j: (i, j))
    return pl.pallas_call(
        body, name=name, grid=(t // tm, nb),
        in_specs=[pl.BlockSpec((tm, k), lambda i, j: (i, 0)), pl.BlockSpec((cb, k), lambda i, j: (j, 0)),
                  pl.BlockSpec((tm, cb), lambda i, j: (i, gb_ + j)),
                  pl.BlockSpec((tm, cb), lambda i, j: (i, gb_ + nb + j)), own, own] + [ANY] * len(deps),
        out_specs=[own] * 4, out_shape=[_sds((t, d), BF16)] * 4,
        compiler_params=_cparams("parallel", "parallel"),
    )(dh, w_mo, proj, proj, ya, yc, *deps)


def _assemble_dproj(dq, dkc, dkp, dvc, dvp, cos, sin, dz, dgb, dgc, dga, dgg):
    t, aw = dq.shape
    nkv, cw, d = dkc.shape[0], dz.shape[1], dga.shape[1]
    kvw = nkv * HEAD_DIM
    nblk = t // BLOCK
    width = aw + 2 * kvw + 3 * cw + 2 * d

    def body(dq_ref, dkc_ref, dkp_ref, dvc_ref, dvp_ref, c_ref, s_ref, dz_ref, dgb_ref, dgc_ref, dga_ref, dgg_ref,
             o_ref):
        keep = (pl.program_id(0) < nblk - 1).astype(F32)
        c, s = c_ref[...], s_ref[...]
        dk = jnp.concatenate([dkc_ref[h] + dkp_ref[h] * keep for h in range(nkv)], axis=1)
        dv = jnp.concatenate([dvc_ref[h] + dvp_ref[h] * keep for h in range(nkv)], axis=1)
        o_ref[...] = jnp.concatenate(
            [_rope(dq_ref[...].astype(F32), c, -s).astype(BF16), _rope(dk, c, -s).astype(BF16), dv.astype(BF16),
             dz_ref[...], dgb_ref[...], dgc_ref[...], dga_ref[...], dgg_ref[...]], axis=1)

    def cur(w):
        return pl.BlockSpec((BLOCK, w), lambda n: (n, 0))

    head_cur = pl.BlockSpec((nkv, BLOCK, HEAD_DIM), lambda n: (0, n, 0))
    head_nxt = pl.BlockSpec((nkv, BLOCK, HEAD_DIM), lambda n: (0, jnp.minimum(n + 1, nblk - 1), 0))
    return pl.pallas_call(
        body, name="assemble_dproj", grid=(nblk,),
        in_specs=[cur(aw), head_cur, head_nxt, head_cur, head_nxt, cur(LANES), cur(LANES),
                  cur(cw), cur(cw), cur(cw), cur(d), cur(d)],
        out_specs=cur(width), out_shape=_sds((t, width), BF16),
        compiler_params=_cparams("parallel"),
    )(dq, dkc, dkp, dvc, dvp, cos, sin, dz, dgb, dgc, dga, dgg)


def _xattn_probs(qh, kh):
    s = lax.dot_general(qh, kh, (((1,), (1,)), ((), ())), preferred_element_type=F32) * (X_HEAD_DIM ** -0.5)
    e = jnp.exp(s - jnp.max(s, axis=-1, keepdims=True))
    return e * (1.0 / jnp.sum(e, axis=-1, keepdims=True))


def _xattn_fwd(xq, kv):
    t, xw = xq.shape
    mt = kv.shape[0]
    tm = _pick(t, (512, 256, 128))

    def body(q_ref, kv_ref, o_ref):
        outs = []
        for hd in range(xw // X_HEAD_DIM):
            hs = slice(hd * X_HEAD_DIM, (hd + 1) * X_HEAD_DIM)
            vs = slice(xw + hd * X_HEAD_DIM, xw + (hd + 1) * X_HEAD_DIM)
            p = _xattn_probs(q_ref[:, hs], kv_ref[:, hs])
            outs.append(jnp.dot(p.astype(BF16), kv_ref[:, vs], preferred_element_type=F32))
        o_ref[...] = jnp.concatenate(outs, axis=1).astype(o_ref.dtype)

    return pl.pallas_call(
        body, name="xattn_fwd", grid=(t // tm,),
        in_specs=[pl.BlockSpec((tm, xw), lambda i: (i, 0)), pl.BlockSpec((mt, 2 * xw), lambda i: (0, 0))],
        out_specs=pl.BlockSpec((tm, xw), lambda i: (i, 0)), out_shape=_sds((t, xw), BF16),
        compiler_params=_cparams("parallel"),
    )(xq, kv)


def _xattn_bwd(xq, kv, do):
    t, xw = xq.shape
    mt = kv.shape[0]
    tm = _pick(t, (512, 256, 128))
    scale = X_HEAD_DIM ** -0.5

    def body(q_ref, kv_ref, do_ref, dq_ref, dkv_ref):
        dqs, dks, dvs = [], [], []
        for hd in range(xw // X_HEAD_DIM):
            hs = slice(hd * X_HEAD_DIM, (hd + 1) * X_HEAD_DIM)
            vs = slice(xw + hd * X_HEAD_DIM, xw + (hd + 1) * X_HEAD_DIM)
            qh, kh, vh, doh = q_ref[:, hs], kv_ref[:, hs], kv_ref[:, vs], do_ref[:, hs]
            p = _xattn_probs(qh, kh)
            pb = p.astype(BF16)
            o = jnp.dot(pb, vh, preferred_element_type=F32)
            delta = jnp.sum(doh.astype(F32) * o, axis=-1, keepdims=True)
            dvs.append(lax.dot_general(pb, doh, (((0,), (0,)), ((), ())), preferred_element_type=F32))
            dp = lax.dot_general(doh, vh, (((1,), (1,)), ((), ())), preferred_element_type=F32)
            dsc = (p * (dp - delta)).astype(BF16)
            dqs.append(jnp.dot(dsc, kh, preferred_element_type=F32) * scale)
            dks.append(lax.dot_general(dsc, qh, (((0,), (0,)), ((), ())), preferred_element_type=F32) * scale)
        dq_ref[...] = jnp.concatenate(dqs, axis=1).astype(dq_ref.dtype)
        dkv = jnp.concatenate(dks + dvs, axis=1)

        @pl.when(pl.program_id(0) == 0)
        def _():
            dkv_ref[...] = dkv

        @pl.when(pl.program_id(0) > 0)
        def _():
            dkv_ref[...] += dkv

    row = pl.BlockSpec((tm, xw), lambda i: (i, 0))
    whole = pl.BlockSpec((mt, 2 * xw), lambda i: (0, 0))
    return pl.pallas_call(
        body, name="xattn_bwd", grid=(t // tm,),
        in_specs=[row, whole, row], out_specs=[row, whole],
        out_shape=[_sds((t, xw), BF16), _sds((mt, 2 * xw), F32)],
        compiler_params=_cparams("arbitrary"),
    )(xq, kv, do)


def _ffn_in_swiglu(u, w, dep, name):
    t, d = u.shape
    f = w.shape[1] // 2
    tm, tn, _ = _tiles("nn", t, f)
    nf = f // tn
    deps = [] if dep is None else [dep]

    def body(u_ref, wa_ref, wb_ref, *rest):
        h_ref, o_ref = rest[len(deps):]
        a = jnp.dot(u_ref[...], wa_ref[...], preferred_element_type=F32)
        b = jnp.dot(u_ref[...], wb_ref[...], preferred_element_type=F32)
        h_ref[0] = a.astype(h_ref.dtype)
        h_ref[1] = b.astype(h_ref.dtype)
        o_ref[...] = (a * _sigmoid(a) * b).astype(o_ref.dtype)

    return pl.pallas_call(
        body, name=name, grid=(t // tm, nf),
        in_specs=[pl.BlockSpec((tm, d), lambda i, j: (i, 0)), pl.BlockSpec((d, tn), lambda i, j: (0, j)),
                  pl.BlockSpec((d, tn), lambda i, j: (0, nf + j))] + [ANY] * len(deps),
        out_specs=[pl.BlockSpec((2, tm, tn), lambda i, j: (0, i, j)), pl.BlockSpec((tm, tn), lambda i, j: (i, j))],
        out_shape=[_sds((2, t, f), BF16), _sds((t, f), BF16)],
        compiler_params=_cparams("parallel", "parallel"),
    )(u, w, w, *deps)


def _dact_swiglu(dh, w_out, hid, dep, name):
    t, d = dh.shape
    f = w_out.shape[0]
    tm, tn, _ = _tiles("nt", t, f)
    deps = [] if dep is None else [dep]

    def body(dh_ref, w_ref, h_ref, *rest):
        o_ref = rest[len(deps)]
        g = lax.dot_general(dh_ref[...], w_ref[...], (((1,), (1,)), ((), ())), preferred_element_type=F32)
        a = h_ref[0].astype(F32)
        b = h_ref[1].astype(F32)
        sg = _sigmoid(a)
        o_ref[0] = (g * b * sg * (1.0 + a * (1.0 - sg))).astype(o_ref.dtype)
        o_ref[1] = (g * a * sg).astype(o_ref.dtype)

    pair = pl.BlockSpec((2, tm, tn), lambda i, j: (0, i, j))
    return pl.pallas_call(
        body, name=name, grid=(t // tm, f // tn),
        in_specs=[pl.BlockSpec((tm, d), lambda i, j: (i, 0)), pl.BlockSpec((tn, d), lambda i, j: (j, 0)), pair]
        + [ANY] * len(deps),
        out_specs=pair, out_shape=_sds((2, t, f), BF16),
        compiler_params=_cparams("parallel", "parallel"),
    )(dh, w_out, hid, *deps)


def _adamw(w, g, m, v, name):
    r, c = w.shape
    tr = _pick(r, (256, 128, 64, 32, 16, 8)) if r * c > 65536 else r

    def body(w_ref, g_ref, m_ref, v_ref, d_ref, nm_ref, nv_ref, go_ref):
        gv = g_ref[...]
        go_ref[...] = gv
        m2 = ADAM_B1 * m_ref[...] + (1.0 - ADAM_B1) * gv
        v2 = ADAM_B2 * v_ref[...] + (1.0 - ADAM_B2) * (gv * gv)
        m_hat = m2 / (1.0 - ADAM_B1 ** ADAM_STEP)
        v_hat = v2 / (1.0 - ADAM_B2 ** ADAM_STEP)
        d_ref[...] = -ADAM_LR * (m_hat / (jnp.sqrt(v_hat) + ADAM_EPS) + ADAM_WD * w_ref[...])
        nm_ref[...] = m2
        nv_ref[...] = v2

    blk = pl.BlockSpec((tr, c), lambda i: (i, 0))
    return pl.pallas_call(
        body, name=name, grid=(r // tr,),
        in_specs=[blk] * 4, out_specs=[blk] * 4, out_shape=[_sds((r, c), F32)] * 4,
        compiler_params=_cparams("parallel"),
    )(w, g, m, v)


class _Weights:
    def __init__(self, full):
        self.full = full
        self.grads = {}

    def get(self, name):
        return self.full[name]

    def mark(self, tag, value):
        return value

    def grad(self, name, g):
        self.grads[name] = g

    def dep(self):
        return None


def _local_step(x, mem, tgt, g_mix, sinks, g_xattn, g_mem, g_ffn, g_final, dims, wts):
    t, d = x.shape
    aw, cw, kvw = dims
    cb = 2 * kvw
    zoff = aw + 2 * kvw
    goff = zoff + 3 * cw
    cos, sin = _rope_tables(t)
    mark, get = wts.mark, wts.get

    def mm(a, b, **kw):
        return _matmul(a, b, dep=wts.dep(), **kw)

    u1 = mark("u1", _rms_fwd(x, g_mix, "rms_mix"))
    qkv = mm(u1, get("w_in"), mode="nn", out_dtype=F32, name="mm_qkv", cols=(0, zoff))
    proj = mark("proj", mm(u1, get("w_in"), mode="nn", out_dtype=BF16, name="mm_gates", cols=(zoff, goff - zoff + 2 * d)))
    zoff, goff = 0, goff - zoff
    q_r, k_r, v_h = _rope_qkv(qkv, cos, sin, aw, kvw)
    attn_o = mark("attn_o", _attn_fwd(q_r, k_r, v_h, sinks))
    conv_o = mark("conv_o", _conv_fwd(proj, get("conv_w"), zoff, cw, cb))
    ya = mark("ya", mm(attn_o, get("w_attn_proj"), mode="nn", out_dtype=BF16, name="mm_yattn"))
    yc, merged = _yconv_merge(conv_o, get("w_conv_proj"), ya, proj, goff, cb, wts.dep(), "mm_yconv")
    mark("merged", merged)
    h1 = mark("h1", mm(merged, get("w_mix_out"), mode="nn", out_dtype=F32, name="mm_mix", res=x))
    u2 = mark("u2", _rms_fwd(h1, g_xattn, "rms_xattn"))
    mem_n = _rms_fwd(mem, g_mem, "rms_mem")
    xq = mark("xq", mm(u2, get("w_xq"), mode="nn", out_dtype=BF16, name="mm_xq"))
    kv = mm(mem_n, get("w_xkv"), mode="nn", out_dtype=BF16, name="mm_xkv")
    xo = mark("xo", _xattn_fwd(xq, kv))
    h2 = mark("h2", mm(xo, get("w_xo"), mode="nn", out_dtype=F32, name="mm_xo", res=h1))
    u3 = mark("u3", _rms_fwd(h2, g_ffn, "rms_ffn"))
    hid, act = _ffn_in_swiglu(u3, get("w_ffn_in"), wts.dep(), "mm_ffn_in")
    mark("hid", hid)
    h3 = mm(act, get("w_ffn_out"), mode="nn", out_dtype=F32, name="mm_ffn_out", res=h2)
    dh3, dh3b, dg_final, loss = _loss_head(h3, g_final, tgt)
    mark("dh3", dh3b)

    wts.grad("w_ffn_out", mm(act, dh3b, mode="tn", out_dtype=BF16, name="mm_dw_ffn_out"))
    dhid = mark("dhid", _dact_swiglu(dh3b, get("w_ffn_out"), hid, wts.dep(), "mm_dact"))
    wts.grad("w_ffn_in", mm(u3, dhid, mode="tn", out_dtype=BF16, name="mm_dw_ffn_in"))
    du3 = mark("du3", mm(dhid, get("w_ffn_in"), mode="nt", out_dtype=F32, name="mm_du3"))
    dh2, dh2b, dg_ffn = _rms_bwd(h2, g_ffn, du3, dh3, "rms_bwd_ffn")
    mark("dh2", dh2b)
    wts.grad("w_xo", mm(xo, dh2b, mode="tn", out_dtype=BF16, name="mm_dw_xo"))
    dxo = mm(dh2b, get("w_xo"), mode="nt", out_dtype=BF16, name="mm_dxo")
    dxq, dkv = _xattn_bwd(xq, kv, dxo)
    dkvb = dkv.astype(BF16)
    wts.grad("w_xq", mm(u2, dxq, mode="tn", out_dtype=BF16, name="mm_dw_xq"))
    du2 = mm(dxq, get("w_xq"), mode="nt", out_dtype=F32, name="mm_du2")
    wts.grad("w_xkv", mm(mem_n, dkvb, mode="tn", out_dtype=BF16, name="mm_dw_xkv"))
    dmem_n = mm(dkvb, get("w_xkv"), mode="nt", out_dtype=F32, name="mm_dmem")
    _, _, dg_mem = _rms_bwd(mem, g_mem, dmem_n, None, "rms_bwd_mem")
    dh1, dh1b, dg_xattn = _rms_bwd(h1, g_xattn, du2, dh2, "rms_bwd_xattn")
    mark("dh1", dh1b)
    wts.grad("w_mix_out", mm(merged, dh1b, mode="tn", out_dtype=BF16, name="mm_dw_mix"))
    dya, dyc, dga, dgg = _dmerged_split(dh1b, get("w_mix_out"), proj, ya, yc, goff, cb, wts.dep(), "mm_dmerged")
    mark("dya", dya)
    wts.grad("w_attn_proj", mm(attn_o, dya, mode="tn", out_dtype=BF16, name="mm_dw_attn_proj"))
    dattn_o = mm(dya, get("w_attn_proj"), mode="nt", out_dtype=BF16, name="mm_dattn")
    wts.grad("w_conv_proj", mm(conv_o, dyc, mode="tn", out_dtype=BF16, name="mm_dw_conv_proj"))
    dconv_o = mark("dconv_o", mm(dyc, get("w_conv_proj"), mode="nt", out_dtype=BF16, name="mm_dconv"))
    dz, dgb, dgc, dconv_w = _conv_bwd(proj, get("conv_w"), dconv_o, zoff, cw, cb)
    mark("dz", dz)
    dq, dkc, dkp, dvc, dvp, dsinks = _attn_bwd(q_r, k_r, v_h, sinks, attn_o, dattn_o)
    mark("dq", dq)
    dproj = mark("dproj", _assemble_dproj(dq, dkc, dkp, dvc, dvp, cos, sin, dz, dgb, dgc, dga, dgg))
    wts.grad("w_in", mm(u1, dproj, mode="tn", out_dtype=BF16, name="mm_dw_in"))
    du1 = mark("du1", mm(dproj, get("w_in"), mode="nt", out_dtype=F32, name="mm_du1"))
    grad_x, _, dg_mix = _rms_bwd(x, g_mix, du1, dh1, "rms_bwd_mix")
    mark("grad_x", grad_x)

    small = dict(g_mix=dg_mix, g_xattn=dg_xattn, g_mem=dg_mem, g_ffn=dg_ffn, g_final=dg_final,
                 conv_w=dconv_w[:3], attn_sinks=dsinks[0:1, :sinks.shape[1]], loss=loss[0:1, 0:1])
    return grad_x, small


BIG = (("w_in", 1), ("w_attn_proj", 1), ("w_conv_proj", 1), ("w_mix_out", 0), ("w_xq", 0), ("w_xkv", 0),
       ("w_xo", 1), ("w_ffn_in", 1), ("w_ffn_out", 0))


def _place():
    x, y, c = lax.axis_index("x"), lax.axis_index("y"), lax.axis_index("c")
    chips = [(1 - x, y), (x, 1 - y), (1 - x, 1 - y)]
    return x, y, c, chips


def _window(ref, ax, shard_shape, s, h):
    sr, sc = shard_shape
    hr = sr // 2
    if ax == 1:
        return ref.at[pl.ds(pl.multiple_of(h * hr, 16), hr), pl.ds(pl.multiple_of(s * sc, LANES), sc)]
    return ref.at[pl.ds(pl.multiple_of(s * sr + h * hr, 16), hr), :]


def _half(ref, h):
    hr = ref.shape[0] // 2
    return ref.at[pl.ds(pl.multiple_of(h * hr, 16), hr), :]


def _remote(src, dst, send_sem, recv_sem, dev):
    return pltpu.make_async_remote_copy(src_ref=src, dst_ref=dst, send_sem=send_sem, recv_sem=recv_sem,
                                        device_id=dev, device_id_type=MESH)


def _cast_to_full(shard, ax, me, name, dtype=BF16):
    sr, sc = shard.shape
    tr = _pick(sr, (256, 352, 128, 64, 32, 16))
    nr = sr // tr
    full = (sr * N_CHIPS, sc) if ax == 0 else (sr, sc * N_CHIPS)

    def body(me_ref, s_ref, o_ref):
        o_ref[...] = s_ref[...].astype(o_ref.dtype)

    if ax == 1:
        out_spec = pl.BlockSpec((tr, sc), lambda r, me_ref: (r, me_ref[0]))
    else:
        out_spec = pl.BlockSpec((tr, sc), lambda r, me_ref: (me_ref[0] * nr + r, 0))
    return pl.pallas_call(
        body, name=name,
        grid_spec=pltpu.PrefetchScalarGridSpec(
            num_scalar_prefetch=1, grid=(nr,), in_specs=[pl.BlockSpec((tr, sc), lambda r, me_ref: (r, 0))],
            out_specs=out_spec),
        out_shape=_sds(full, dtype),
        compiler_params=_cparams("parallel"),
    )(me, shard)


HBM = pl.BlockSpec(memory_space=pltpu.HBM)
SEM = pl.BlockSpec(memory_space=pltpu.SEMAPHORE)
EFFECT = pltpu.SideEffectType.DATAFLOW_SIDE_EFFECTING


def _in_hbm(a):
    return pltpu.with_memory_space_constraint(a, pltpu.HBM)


def _gather_window(ref, ax, shard_shape, s, h):
    if h is not None:
        return _window(ref, ax, shard_shape, s, h)
    sr, sc = shard_shape
    if ax == 1:
        return ref.at[:, pl.ds(pl.multiple_of(s * sc, LANES), sc)]
    return ref.at[pl.ds(pl.multiple_of(s * sr, 8), sr), :]


def _ag_start(fulls, axes, shard_shapes, whole):
    n = len(fulls)

    def body(*refs):
        src = refs[:n]
        send_sems, recv_sems = refs[n], refs[n + 1]
        token = refs[2 * n + 2]
        x, y, c, chips = _place()
        me = 2 * x + y
        for i in range(n):
            h = None if whole[i] else c
            for j, chip in enumerate(chips):
                blk = _gather_window(src[i], axes[i], shard_shapes[i], me, h)
                _remote(blk, blk, send_sems.at[3 * i + j], recv_sems.at[3 * i + j], (*chip, c)).start()
        token[...] = jnp.zeros_like(token)

    res = pl.pallas_call(
        body, name="ag_start_" + str(n),
        out_shape=(pltpu.SemaphoreType.DMA((3 * n,)), pltpu.SemaphoreType.DMA((3 * n,)),
                   *[pltpu.HBM(f.shape, f.dtype) for f in fulls], _sds((8, LANES), F32)),
        in_specs=[HBM] * n, out_specs=(SEM, SEM, *[HBM] * n, pl.BlockSpec(memory_space=pltpu.VMEM)),
        input_output_aliases={i: 2 + i for i in range(n)},
        compiler_params=pltpu.CompilerParams(has_side_effects=EFFECT),
    )(*[_in_hbm(f) for f in fulls])
    return res[0], res[1], list(res[2:2 + n]), res[2 + n]


def _ag_mid(bufs, slots, axes, shard_shapes, whole, send_sems, recv_sems, after, name):
    ng = len(bufs)

    def body(*refs):
        src = refs[:ng]
        s_in, r_in = refs[ng], refs[ng + 1]
        fsend, frecv = refs[ng + 3], refs[ng + 4]
        x, y, c, chips = _place()
        me = 2 * x + y
        sib = (x, y, 1 - c)
        for k, i in enumerate(slots):
            h = None if whole[k] else c
            for j, chip in enumerate(chips):
                cj = 2 * chip[0] + chip[1]
                mine = _gather_window(src[k], axes[k], shard_shapes[k], me, h)
                theirs = _gather_window(src[k], axes[k], shard_shapes[k], cj, h)
                _remote(theirs, theirs, s_in.at[3 * i + j], r_in.at[3 * i + j], (*chip, c)).wait_recv()
                _remote(mine, mine, s_in.at[3 * i + j], r_in.at[3 * i + j], (*chip, c)).wait_send()
                if not whole[k]:
                    _remote(theirs, theirs, fsend.at[3 * k + j], frecv.at[3 * k + j], sib).start()
        token = refs[2 * ng + 5]
        token[...] = jnp.zeros_like(token)

    res = pl.pallas_call(
        body, name=name,
        out_shape=(pltpu.SemaphoreType.DMA((3 * ng,)), pltpu.SemaphoreType.DMA((3 * ng,)),
                   *[pltpu.HBM(b.shape, b.dtype) for b in bufs], _sds((8, LANES), F32)),
        in_specs=[HBM] * ng + [SEM, SEM, ANY],
        out_specs=(SEM, SEM, *[HBM] * ng, pl.BlockSpec(memory_space=pltpu.VMEM)),
        input_output_aliases={k: 2 + k for k in range(ng)},
        compiler_params=pltpu.CompilerParams(has_side_effects=EFFECT),
    )(*bufs, send_sems, recv_sems, after)
    return res[0], res[1], list(res[2:2 + ng]), res[2 + ng]


def _ag_wait(bufs, axes, shard_shapes, whole, fsend, frecv, after, name):
    ng = len(bufs)

    def body(*refs):
        src = refs[:ng]
        s_in, r_in = refs[ng], refs[ng + 1]
        x, y, c, chips = _place()
        sib = (x, y, 1 - c)
        for k in range(ng):
            if whole[k]:
                continue
            for j, chip in enumerate(chips):
                cj = 2 * chip[0] + chip[1]
                sent = _gather_window(src[k], axes[k], shard_shapes[k], cj, c)
                landed = _gather_window(src[k], axes[k], shard_shapes[k], cj, 1 - c)
                _remote(landed, landed, s_in.at[3 * k + j], r_in.at[3 * k + j], sib).wait_recv()
                _remote(sent, sent, s_in.at[3 * k + j], r_in.at[3 * k + j], sib).wait_send()

    res = pl.pallas_call(
        body, name=name,
        out_shape=tuple(pltpu.HBM(b.shape, b.dtype) for b in bufs),
        in_specs=[HBM] * ng + [SEM, SEM, ANY], out_specs=tuple([HBM] * ng),
        input_output_aliases={k: k for k in range(ng)},
        compiler_params=pltpu.CompilerParams(has_side_effects=EFFECT),
    )(*bufs, fsend, frecv, after)
    return list(res)


class _Schedule:
    GROUPS = ((("w_in", "conv_w"), "u1", "u1"),
              (("w_attn_proj", "w_conv_proj", "w_mix_out", "w_xq", "w_xkv", "w_xo"), "proj", "conv_o"),
              (("w_ffn_in",), "h1", "h2"),
              (("w_ffn_out",), "u3", "hid"))
    STARTS = ((0,), (1, 2, 3))
    REDUCE = ((("w_ffn_out",), "dhid", "grad:w_ffn_in", "du3"),
              (("w_ffn_in",), "grad:w_ffn_in", "dya", "grad_x"),
              (("w_xo", "w_xq", "w_xkv", "w_mix_out", "w_attn_proj", "w_conv_proj"), "dconv_o", "dq", "grad_x"),
              (("w_in",), "grad:w_in", "end", "end2"))

    def __init__(self, seed, axes, shard_shapes, place, on_ready):
        self.ax, self.shape, self.place, self.on_ready = axes, shard_shapes, place, on_ready
        self.stage, self.buf, self.slot, self.passes = {}, {}, {}, {}
        self.ready = set()
        self.grads = {}
        token = None
        for groups in self.STARTS:
            order = [nm for g in groups for nm in self.GROUPS[g][0]]
            send, recv, bufs, token = _ag_start([seed(nm, token) for nm in order], *self._meta(order))
            self.buf.update(zip(order, bufs))
            self.slot.update({nm: (send, recv, k) for k, nm in enumerate(order)})
        self.token = self.latest = token
        self.mark("start", token)

    def _meta(self, names):
        return ([self.ax[nm] for nm in names], [self.shape[nm] for nm in names], [nm == "conv_w" for nm in names])

    def mark(self, tag, value):
        for g, (names, mid, wait) in enumerate(self.GROUPS):
            if tag == mid:
                send, recv, _ = self.slot[names[0]]
                fs, fr, bufs, self.latest = _ag_mid([self.buf[nm] for nm in names], [self.slot[nm][2] for nm in names],
                                                    *self._meta(names), send, recv, value, "ag_mid_%d" % g)
                self.buf.update(zip(names, bufs))
                self.passes[g] = (fs, fr)
            if tag == wait:
                fs, fr = self.passes[g]
                bufs = _ag_wait([self.buf[nm] for nm in names], *self._meta(names), fs, fr, value, "ag_wait_%d" % g)
                self.buf.update(zip(names, bufs))
                self.ready.update(names)
        for g, (names, send, total, finish) in enumerate(self.REDUCE):
            st = self.stage.get(g)
            if st is None:
                continue
            ng = len(names)
            if tag == send and st["at"] == "pair":
                arrs = _exchange_wait("rs_pair_wait_%d" % g, st["arrs"], *st["sems"], st["plan"], value)
                parts = [_pair_add(arrs[k], arrs[ng + k], self.ax[nm], self.shape[nm], self.place, "pair_add_" + nm)
                         for k, nm in enumerate(names)]
                plan, nsem = _plan_chip(ng)
                ss, rs, arrs, self.latest = _exchange_start(
                    "rs_chip_start_%d" % g, parts + [lax.empty(p.shape, p.dtype) for p in parts], nsem, plan)
                self.stage[g] = dict(at="chip", arrs=arrs, sems=(ss, rs), plan=plan)
            elif tag == total and st["at"] == "chip":
                arrs = _exchange_wait("rs_chip_wait_%d" % g, st["arrs"], *st["sems"], st["plan"], value)
                halves = [_chip_add(arrs[k], arrs[ng + k], self.place, "chip_add_" + nm) for k, nm in enumerate(names)]
                plan, nsem = _plan_gather(ng)
                ss, rs, arrs, self.latest = _exchange_start("rs_gather_start_%d" % g, halves, nsem, plan)
                self.stage[g] = dict(at="gather", arrs=arrs, sems=(ss, rs), plan=plan)
            elif tag == finish and st["at"] == "gather":
                arrs = _exchange_wait("rs_gather_wait_%d" % g, st["arrs"], *st["sems"], st["plan"], value)
                self.stage[g] = dict(at="done")
                for nm, shard in zip(names, arrs):
                    self.on_ready(nm, shard)
        return value

    def get(self, name):
        assert name in self.ready, name
        return self.buf[name]

    def grad(self, name, g):
        self.grads[name] = g
        for gi, (names, _, _, _) in enumerate(self.REDUCE):
            if name == names[-1]:
                gs = [self.grads[nm] for nm in names]
                plan, nsem = _plan_pair(len(names), [self.ax[nm] for nm in names], [self.shape[nm] for nm in names])
                ss, rs, arrs, self.latest = _exchange_start(
                    "rs_pair_start_%d" % gi, gs + [lax.empty(a.shape, a.dtype) for a in gs], nsem, plan)
                self.stage[gi] = dict(at="pair", arrs=arrs, sems=(ss, rs), plan=plan)
                g = self.latest
        self.mark("grad:" + name, g)

    def dep(self):
        return self.latest


def _exchange_start(name, arrays, nsem, plan):
    n = len(arrays)

    def body(*refs):
        send_sems, recv_sems, token = refs[n], refs[n + 1], refs[2 * n + 2]
        sends, _ = plan(refs[:n])
        for k, (src, dst, dev) in enumerate(sends):
            _remote(src, dst, send_sems.at[k], recv_sems.at[k], dev).start()
        token[...] = jnp.zeros_like(token)

    res = pl.pallas_call(
        body, name=name,
        out_shape=(pltpu.SemaphoreType.DMA((nsem,)), pltpu.SemaphoreType.DMA((nsem,)),
                   *[pltpu.HBM(a.shape, a.dtype) for a in arrays], _sds((8, LANES), F32)),
        in_specs=[HBM] * n, out_specs=(SEM, SEM, *[HBM] * n, pl.BlockSpec(memory_space=pltpu.VMEM)),
        input_output_aliases={i: 2 + i for i in range(n)},
        compiler_params=pltpu.CompilerParams(has_side_effects=EFFECT),
    )(*[_in_hbm(a) for a in arrays])
    return res[0], res[1], list(res[2:2 + n]), res[2 + n]


def _exchange_wait(name, arrays, send_sems, recv_sems, plan, after):
    n = len(arrays)

    def body(*refs):
        s_in, r_in = refs[n], refs[n + 1]
        sends, recvs = plan(refs[:n])
        for k, land in enumerate(recvs):
            _remote(land, land, s_in.at[k], r_in.at[k], sends[k][2]).wait_recv()
        for k, (src, _, dev) in enumerate(sends):
            _remote(src, src, s_in.at[k], r_in.at[k], dev).wait_send()

    res = pl.pallas_call(
        body, name=name,
        out_shape=tuple(pltpu.HBM(a.shape, a.dtype) for a in arrays),
        in_specs=[HBM] * n + [SEM, SEM, ANY], out_specs=tuple([HBM] * n),
        input_output_aliases={i: i for i in range(n)},
        compiler_params=pltpu.CompilerParams(has_side_effects=EFFECT),
    )(*arrays, send_sems, recv_sems, after)
    return list(res)


def _plan_pair(n, axes, shard_shapes):
    def plan(refs):
        g, ra = refs[:n], refs[n:]
        x, y, c, _ = _place()
        sib = (x, y, 1 - c)

        def pieces(ref, i, h):
            if axes[i] == 1:
                return [_half(ref, h)]
            return [_window(ref, 0, shard_shapes[i], s, h) for s in range(N_CHIPS)]

        sends, recvs = [], []
        for i in range(n):
            sends += [(src, dst, sib) for src, dst in zip(pieces(g[i], i, 1 - c), pieces(ra[i], i, 1 - c))]
            recvs += pieces(ra[i], i, c)
        return sends, recvs

    return plan, sum(1 if ax == 1 else N_CHIPS for ax in axes)


def _plan_chip(n):
    def plan(refs):
        p, rc = refs[:n], refs[n:]
        x, y, c, chips = _place()
        me = 2 * x + y
        sends, recvs = [], []
        for i in range(n):
            for chip in chips:
                cj = 2 * chip[0] + chip[1]
                sends.append((p[i].at[cj], rc[i].at[me], (*chip, c)))
                recvs.append(rc[i].at[cj])
        return sends, recvs

    return plan, 3 * n


def _plan_gather(n):
    def plan(refs):
        x, y, c, _ = _place()
        sib = (x, y, 1 - c)
        return ([(_half(r, c), _half(r, c), sib) for r in refs], [_half(r, 1 - c) for r in refs])

    return plan, n


def _pair_add(g, ra, ax, shard_shape, place, name):
    sr, sc = shard_shape
    hr = sr // 2
    wc = sc
    tr = _pick(hr, (256, 352, 128, 64, 32, 16))
    nr = hr // tr

    def body(p_ref, a_ref, b_ref, o_ref):
        o_ref[...] = (a_ref[...].astype(F32) + b_ref[...].astype(F32)).astype(o_ref.dtype)

    if ax == 1:
        src = pl.BlockSpec((tr, wc), lambda s, r, p_ref: (p_ref[0] * nr + r, s))
    else:
        src = pl.BlockSpec((tr, wc), lambda s, r, p_ref: (s * 2 * nr + p_ref[0] * nr + r, 0))
    return pl.pallas_call(
        body, name=name,
        grid_spec=pltpu.PrefetchScalarGridSpec(
            num_scalar_prefetch=1, grid=(N_CHIPS, nr), in_specs=[src, src],
            out_specs=pl.BlockSpec((None, tr, wc), lambda s, r, p_ref: (s, r, 0))),
        out_shape=_sds((N_CHIPS, hr, wc), BF16),
        compiler_params=_cparams("parallel", "parallel"),
    )(place, g, ra)


def _chip_add(part, rc, place, name):
    _, hr, wc = rc.shape
    tr = _pick(hr, (256, 352, 128, 64, 32, 16))
    nr = hr // tr

    def body(p_ref, own_ref, r1_ref, r2_ref, r3_ref, o_ref):
        acc = own_ref[...].astype(F32)
        for r_ref in (r1_ref, r2_ref, r3_ref):
            acc = acc + r_ref[...].astype(F32)
        o_ref[...] = acc

    def slot(k):
        return pl.BlockSpec((None, tr, wc), lambda r, p_ref: ((p_ref[1] + k) % N_CHIPS, r, 0))

    return pl.pallas_call(
        body, name=name,
        grid_spec=pltpu.PrefetchScalarGridSpec(
            num_scalar_prefetch=1, grid=(nr,), in_specs=[slot(0), slot(1), slot(2), slot(3)],
            out_specs=pl.BlockSpec((tr, wc), lambda r, p_ref: (p_ref[0] * nr + r, 0))),
        out_shape=_sds((2 * hr, wc), F32),
        compiler_params=_cparams("parallel"),
    )(place, part, rc, rc, rc)


N_DEV = 8


def _all_reduce_small(buf):
    r, cdim = buf.shape

    def body(x_ref, o_ref, land, send_sems, recv_sems):
        x, y, c, _ = _place()
        me = 4 * x + 2 * y + c
        land[me] = x_ref[...]
        sends = []
        for k in range(1, N_DEV):
            kx, ky, kc = (k >> 2) & 1, (k >> 1) & 1, k & 1
            peer = (1 - x if kx else x, 1 - y if ky else y, 1 - c if kc else c)
            cp = _remote(x_ref, land.at[me], send_sems.at[k - 1], recv_sems.at[k - 1], peer)
            cp.start()
            sends.append(cp)
        for k in range(1, N_DEV):
            kx, ky, kc = (k >> 2) & 1, (k >> 1) & 1, k & 1
            peer = (1 - x if kx else x, 1 - y if ky else y, 1 - c if kc else c)
            pidx = 4 * peer[0] + 2 * peer[1] + peer[2]
            _remote(land.at[pidx], land.at[pidx], send_sems.at[k - 1], recv_sems.at[k - 1], peer).wait_recv()
        for cp in sends:
            cp.wait_send()
        acc = land[0]
        for dev in range(1, N_DEV):
            acc = acc + land[dev]
        o_ref[...] = acc

    vm = pl.BlockSpec(memory_space=pltpu.VMEM)
    return pl.pallas_call(
        body, name="all_reduce_small", in_specs=[vm], out_specs=vm, out_shape=_sds((r, cdim), F32),
        scratch_shapes=[pltpu.VMEM((N_DEV, r, cdim), F32), pltpu.SemaphoreType.DMA((N_DEV - 1,)),
                        pltpu.SemaphoreType.DMA((N_DEV - 1,))],
    )(buf)


SMALL_ROWS = 16


def kernel(x, mem, g_mix, w_in, conv_w, attn_sinks, w_attn_proj, w_conv_proj, w_mix_out, g_xattn, g_mem, w_xq, w_xkv, w_xo, g_ffn, w_ffn_in, w_ffn_out, g_final, loss_target, m_g_mix, m_w_in, m_conv_w, m_attn_sinks, m_w_attn_proj, m_w_conv_proj, m_w_mix_out, m_g_xattn, m_g_mem, m_w_xq, m_w_xkv, m_w_xo, m_g_ffn, m_w_ffn_in, m_w_ffn_out, m_g_final, v_g_mix, v_w_in, v_conv_w, v_attn_sinks, v_w_attn_proj, v_w_conv_proj, v_w_mix_out, v_g_xattn, v_g_mem, v_w_xq, v_w_xkv, v_w_xo, v_g_ffn, v_w_ffn_in, v_w_ffn_out, v_g_final):
    w = dict(g_mix=g_mix, w_in=w_in[0], conv_w=conv_w[0], attn_sinks=attn_sinks, w_attn_proj=w_attn_proj[0],
             w_conv_proj=w_conv_proj[0], w_mix_out=w_mix_out[0], g_xattn=g_xattn, g_mem=g_mem, w_xq=w_xq[0],
             w_xkv=w_xkv[0], w_xo=w_xo[0], g_ffn=g_ffn, w_ffn_in=w_ffn_in[0], w_ffn_out=w_ffn_out[0],
             g_final=g_final[None])
    m = dict(g_mix=m_g_mix, w_in=m_w_in[0], conv_w=m_conv_w[0], attn_sinks=m_attn_sinks,
             w_attn_proj=m_w_attn_proj[0], w_conv_proj=m_w_conv_proj[0], w_mix_out=m_w_mix_out[0],
             g_xattn=m_g_xattn, g_mem=m_g_mem, w_xq=m_w_xq[0], w_xkv=m_w_xkv[0], w_xo=m_w_xo[0], g_ffn=m_g_ffn,
             w_ffn_in=m_w_ffn_in[0], w_ffn_out=m_w_ffn_out[0], g_final=m_g_final[None])
    v = dict(g_mix=v_g_mix, w_in=v_w_in[0], conv_w=v_conv_w[0], attn_sinks=v_attn_sinks,
             w_attn_proj=v_w_attn_proj[0], w_conv_proj=v_w_conv_proj[0], w_mix_out=v_w_mix_out[0],
             g_xattn=v_g_xattn, g_mem=v_g_mem, w_xq=v_w_xq[0], w_xkv=v_w_xkv[0], w_xo=v_w_xo[0], g_ffn=v_g_ffn,
             w_ffn_in=v_w_ffn_in[0], w_ffn_out=v_w_ffn_out[0], g_final=v_g_final[None])
    names = [nm for nm, _ in BIG]
    axes = [ax for _, ax in BIG]
    d = x.shape[2]
    cw = w["conv_w"].shape[1] * N_CHIPS
    chip = (2 * lax.axis_index("x") + lax.axis_index("y")).astype(jnp.int32)
    place = jnp.stack([lax.axis_index("c").astype(jnp.int32), chip])
    shard_shapes = [w[nm].shape for nm in names]

    def seed(nm, token):
        me1 = chip.reshape(1)
        if token is not None:
            me1 = me1 + token[0, 0].astype(jnp.int32)
        if nm == "conv_w":
            return _cast_to_full(w[nm], 1, me1, "place_conv_w", F32)
        return _cast_to_full(w[nm], dict(BIG)[nm], me1, "cast_" + nm)

    upd = {}

    def on_ready(nm, shard):
        upd[nm] = _adamw(w[nm], shard, m[nm], v[nm], "adamw_" + nm)
        grads[nm] = upd[nm][3]

    grads = {}
    wts = _Schedule(seed, dict(zip(names + ["conv_w"], axes + [1])),
                    dict(zip(names + ["conv_w"], shard_shapes + [w["conv_w"].shape])), place, on_ready)
    aw, cw = w["w_attn_proj"].shape[0], w["w_conv_proj"].shape[0]
    kvw = (w["w_in"].shape[1] * N_CHIPS - aw - 3 * cw - 2 * d) // 2
    grad_x, small = _local_step(
        x[0], mem[0], loss_target[0], w["g_mix"] + wts.token[0:1, 0:1], w["attn_sinks"], w["g_xattn"], w["g_mem"],
        w["g_ffn"], w["g_final"], (aw, cw, kvw), wts)

    pw = max(d, cw)

    def row(a):
        return jnp.pad(a, ((0, 0), (0, pw - a.shape[1])))

    gains = ("g_mix", "g_xattn", "g_mem", "g_ffn", "g_final")
    packed = jnp.concatenate(
        [row(small[nm]) for nm in gains] + [row(small["conv_w"]),
         row(jnp.concatenate([small["attn_sinks"], small["loss"]], axis=1)),
         jnp.zeros((SMALL_ROWS - 9, pw), F32)], axis=0)
    total = _all_reduce_small(packed)
    wts.mark("end", total)
    nsink = attn_sinks.shape[1]
    grads.update({nm: total[k:k + 1, :d] for k, nm in enumerate(gains)})
    grads.update(conv_w=lax.dynamic_slice(total, (5, chip * (cw // N_CHIPS)), (3, cw // N_CHIPS)),
                 attn_sinks=total[8:9, :nsink])
    loss = total[8, nsink]
    for nm in gains + ("conv_w", "attn_sinks"):
        upd[nm] = _adamw(w[nm], grads[nm], m[nm], v[nm], "adamw_" + nm)
    wts.mark("end2", upd["g_final"][0])

    order = ["g_mix", "w_in", "conv_w", "attn_sinks", "w_attn_proj", "w_conv_proj", "w_mix_out", "g_xattn", "g_mem",
             "w_xq", "w_xkv", "w_xo", "g_ffn", "w_ffn_in", "w_ffn_out", "g_final"]

    stacked = set(names) | {"conv_w"}

    def shaped(nm, a):
        if nm == "g_final":
            return a[0]
        return a[None] if nm in stacked else a

    outs = [loss, grad_x[None]]
    outs += [shaped(nm, grads[nm]) for nm in order]
    for k in range(3):
        outs += [shaped(nm, upd[nm][k]) for nm in order]
    return tuple(outs)
```

```python
import functools

import jax
import jax.numpy as jnp
from jax import lax
from jax.experimental import pallas as pl
from jax.experimental.pallas import tpu as pltpu

F32 = jnp.float32
BF16 = jnp.bfloat16

VMEM_LIMIT_BYTES = 56 * 1024 * 1024
LANES = 128
HEAD_DIM = 64
BLOCK = 128
X_HEAD_DIM = 128
ROPE_THETA = 10000.0
EPS = 1e-6
NEG = -1e30
ADAM_LR, ADAM_B1, ADAM_B2, ADAM_EPS, ADAM_WD, ADAM_STEP = 0.001, 0.9, 0.999, 1e-08, 0.01, 10
N_CHIPS = 4
MESH = pl.DeviceIdType.MESH
ANY = pl.BlockSpec(memory_space=pl.ANY)


def _pick(dim, prefs):
    for p in prefs:
        if dim % p == 0:
            return p
    return dim


def _cparams(*sem):
    return pltpu.CompilerParams(dimension_semantics=sem, vmem_limit_bytes=VMEM_LIMIT_BYTES)


def _sds(shape, dtype):
    return jax.ShapeDtypeStruct(shape, dtype)


def _sigmoid(v):
    return 1.0 / (1.0 + jnp.exp(-v))


MATMUL_VMEM_BUDGET = 46 * 1024 * 1024


def _tiles(mode, m, n):
    if mode == "tn" and m % 1024 != 0:
        return _pick(m, (512, 256, 128)), _pick(n, (1024, 512, 256, 128)), True
    return _pick(m, (1024, 512, 256, 128)), _pick(n, (512, 256, 128)), False


def _k_parts(m, n, k):
    tm, tn = _pick(m, (1024, 512, 256, 128)), _pick(n, (512, 256, 128))
    for parts in range(1, k // LANES + 1):
        if k % (parts * LANES) == 0 and 4 * (tm + tn) * (k // parts) + 16 * tm * tn <= MATMUL_VMEM_BUDGET:
            return parts
    return k // LANES


def _matmul(a, b, *, mode, out_dtype, name, res=None, dep=None, cols=None):
    if mode == "nn":
        (m, k), (k2, n) = a.shape, b.shape
    elif mode == "nt":
        (m, k), (n, k2) = (a.shape[-2], a.shape[-1] * (a.shape[0] if a.ndim == 3 else 1)), b.shape
    else:
        (k, m), (k2, n) = a.shape, (b.shape[-2], b.shape[-1] * (b.shape[0] if b.ndim == 3 else 1))
    assert k == k2, (a.shape, b.shape, mode)
    assert cols is None or mode != "nt"
    cols = cols or (0, n)
    parts = 1 if mode == "tn" else (a.shape[0] if a.ndim == 3 else _k_parts(m, cols[1], k))
    for p in range(parts):
        last = p == parts - 1
        res = _matmul_slice(a, b, mode=mode, out_dtype=out_dtype if last else F32, res=res, dep=dep, cols=cols,
                            kslice=(p, parts), name=name + ("_k%d" % p if parts > 1 else ""))
    return res


def _matmul_slice(a, b, *, mode, out_dtype, name, res, dep, kslice, cols):
    part, parts = kslice
    (m, k) = a.shape[-2:] if mode != "tn" else a.shape[::-1]
    col0, n = cols
    tk = k if a.ndim == 3 else k // parts
    tm, tn, swap = _tiles(mode, m, b.shape[2] if (mode == "tn" and b.ndim == 3) else n)
    while col0 % tn:
        tn //= 2
    joff = col0 // tn
    dims = {"nn": (((1,), (0,)), ((), ())), "nt": (((1,), (1,)), ((), ())), "tn": (((0,), (0,)), ((), ()))}[mode]
    has_res = res is not None
    has_dep = dep is not None

    def body(*refs):
        a_ref, b_ref = refs[0], refs[1]
        o_ref = refs[2 + has_res + has_dep]
        val = lax.dot_general(a_ref[...], b_ref[...], dims, preferred_element_type=F32)
        if has_res:
            val = val + refs[2][...]
        o_ref[...] = val.astype(o_ref.dtype)

    def spec(shape, f):
        if swap:
            return pl.BlockSpec(shape, lambda j, i: f(i, j))
        return pl.BlockSpec(shape, f)

    if mode == "tn":
        a_spec = spec((tk, tm), lambda i, j: (part, i))
    elif a.ndim == 3:
        a_spec = spec((None, tm, tk), lambda i, j: (part, i, 0))
    else:
        a_spec = spec((tm, tk), lambda i, j: (i, part))
    if mode == "nt":
        b_spec = spec((tn, tk), lambda i, j: (j, part))
    elif b.ndim == 3:
        per = b.shape[2] // tn
        b_spec = spec((None, tk, tn), lambda i, j: (j // per, part, j % per))
    else:
        b_spec = spec((tk, tn), lambda i, j: (part, joff + j))
    o_spec = spec((tm, tn), lambda i, j: (i, j))
    return pl.pallas_call(
        body,
        name=name,
        grid=(n // tn, m // tm) if swap else (m // tm, n // tn),
        in_specs=[a_spec, b_spec] + ([o_spec] if has_res else []) + ([ANY] if has_dep else []),
        out_specs=o_spec,
        out_shape=_sds((m, n), out_dtype),
        compiler_params=_cparams("parallel", "parallel"),
    )(*([a, b] + ([res] if has_res else []) + ([dep] if has_dep else [])))


def _rms_fwd(x, g, name):
    t, d = x.shape
    tm = _pick(t, (512, 256, 128))

    def body(x_ref, g_ref, o_ref):
        xf = x_ref[...]
        r = lax.rsqrt(jnp.mean(xf * xf, axis=-1, keepdims=True) + EPS)
        o_ref[...] = (xf * r * g_ref[...]).astype(o_ref.dtype)

    row = pl.BlockSpec((tm, d), lambda i: (i, 0))
    return pl.pallas_call(
        body, name=name, grid=(t // tm,),
        in_specs=[row, pl.BlockSpec((1, d), lambda i: (0, 0))],
        out_specs=row, out_shape=_sds((t, d), BF16),
        compiler_params=_cparams("parallel"),
    )(x, g)


def _rms_bwd_math(xf, g, du):
    r = lax.rsqrt(jnp.mean(xf * xf, axis=-1, keepdims=True) + EPS)
    xh = xf * r
    gdy = g * du
    dx = r * (gdy - xh * jnp.mean(gdy * xh, axis=-1, keepdims=True))
    dg = jnp.sum(du * xh, axis=0, keepdims=True)
    return dx, dg


def _rms_bwd(x, g, du, dh, name):
    t, d = x.shape
    tm = _pick(t, (256, 128))
    has_dh = dh is not None

    def body(*refs):
        x_ref, g_ref, du_ref = refs[0], refs[1], refs[2]
        o_ref, ob_ref, dg_ref = refs[3 + has_dh:]
        dx, dg = _rms_bwd_math(x_ref[...], g_ref[...], du_ref[...].astype(F32))
        if has_dh:
            dx = dx + refs[3][...]
        o_ref[...] = dx
        ob_ref[...] = dx.astype(BF16)

        @pl.when(pl.program_id(0) == 0)
        def _():
            dg_ref[...] = dg

        @pl.when(pl.program_id(0) > 0)
        def _():
            dg_ref[...] += dg

    row = pl.BlockSpec((tm, d), lambda i: (i, 0))
    vec = pl.BlockSpec((1, d), lambda i: (0, 0))
    return pl.pallas_call(
        body, name=name, grid=(t // tm,),
        in_specs=[row, vec, row] + ([row] if has_dh else []),
        out_specs=[row, row, vec],
        out_shape=[_sds((t, d), F32), _sds((t, d), BF16), _sds((1, d), F32)],
        compiler_params=_cparams("arbitrary"),
    )(*([x, g, du] + ([dh] if has_dh else [])))


def _loss_head(h, g, tgt):
    t, d = h.shape
    tm = _pick(t, (256, 128))

    def body(h_ref, g_ref, t_ref, o_ref, ob_ref, dg_ref, l_ref):
        xf = h_ref[...]
        gv = g_ref[...]
        r = lax.rsqrt(jnp.mean(xf * xf, axis=-1, keepdims=True) + EPS)
        err = xf * r * gv - t_ref[...]
        part = 0.5 * jnp.sum(jnp.mean(err * err, axis=-1, keepdims=True), axis=0, keepdims=True)
        dx, dg = _rms_bwd_math(xf, gv, err * (1.0 / d))
        o_ref[...] = dx
        ob_ref[...] = dx.astype(BF16)
        lrow = jnp.broadcast_to(part, (1, LANES))

        @pl.when(pl.program_id(0) == 0)
        def _():
            dg_ref[...] = dg
            l_ref[...] = lrow

        @pl.when(pl.program_id(0) > 0)
        def _():
            dg_ref[...] += dg
            l_ref[...] += lrow

    row = pl.BlockSpec((tm, d), lambda i: (i, 0))
    vec = pl.BlockSpec((1, d), lambda i: (0, 0))
    return pl.pallas_call(
        body, name="loss_head", grid=(t // tm,),
        in_specs=[row, vec, row],
        out_specs=[row, row, vec, pl.BlockSpec((1, LANES), lambda i: (0, 0))],
        out_shape=[_sds((t, d), F32), _sds((t, d), BF16), _sds((1, d), F32), _sds((1, LANES), F32)],
        compiler_params=_cparams("arbitrary"),
    )(h, g, tgt)


ROW_TILE = 256


def _rows_matmul(a, b, *, mode, kslice, res, rows_in, vec, epilogue, outs, name, dep):
    part, parts = kslice
    t = a.shape[-2]
    n = b.shape[1] if mode == "nn" else b.shape[0]
    kc = a.shape[-1] if a.ndim == 3 else a.shape[1] // parts
    tm = _pick(t, (ROW_TILE, 128))
    dims = (((1,), (0,)), ((), ())) if mode == "nn" else (((1,), (1,)), ((), ()))
    has_res = res is not None
    deps = [] if dep is None else [dep]
    n_in = 2 + has_res + len(rows_in) + 1 + len(deps)

    def body(*refs):
        val = lax.dot_general(refs[0][...], refs[1][...], dims, preferred_element_type=F32)
        if has_res:
            val = val + refs[2][...]
        row_refs = refs[2 + has_res:2 + has_res + len(rows_in)]
        epilogue(val, row_refs, refs[2 + has_res + len(rows_in)], refs[n_in:], pl.program_id(0) == 0)

    if a.ndim == 3:
        a_spec = pl.BlockSpec((None, tm, kc), lambda i: (part, i, 0))
    else:
        a_spec = pl.BlockSpec((tm, kc), lambda i: (i, part))
    if mode == "nn":
        b_spec = pl.BlockSpec((kc, n), lambda i: (part, 0), pipeline_mode=pl.Buffered(1))
    else:
        b_spec = pl.BlockSpec((n, kc), lambda i: (0, part), pipeline_mode=pl.Buffered(1))
    row = pl.BlockSpec((tm, n), lambda i: (i, 0))
    kinds = {"row": row, "vec": pl.BlockSpec((1, n), lambda i: (0, 0)), "lane": pl.BlockSpec((1, LANES), lambda i: (0, 0))}
    shapes = {"row": (t, n), "vec": (1, n), "lane": (1, LANES)}
    return pl.pallas_call(
        body, name=name, grid=(t // tm,),
        in_specs=[a_spec, b_spec] + [row] * (has_res + len(rows_in)) + [kinds["vec"]] + [ANY] * len(deps),
        out_specs=[kinds[o[0]] for o in outs],
        out_shape=[_sds(shapes[o[0]], o[1] if len(o) > 1 else F32) for o in outs],
        compiler_params=_cparams("arbitrary"),
    )(*([a, b] + ([res] if has_res else []) + list(rows_in) + [vec] + deps))


def _accumulate(ref, value, first):
    @pl.when(first)
    def _():
        ref[...] = value

    @pl.when(jnp.logical_not(first))
    def _():
        ref[...] += value


def _ep_norm(val, rows, g_ref, outs, first):
    h_ref, u_ref = outs
    h_ref[...] = val
    r = lax.rsqrt(jnp.mean(val * val, axis=-1, keepdims=True) + EPS)
    u_ref[...] = (val * r * g_ref[...]).astype(u_ref.dtype)


def _ep_loss(val, rows, g_ref, outs, first):
    o_ref, ob_ref, dg_ref, l_ref = outs
    gv = g_ref[...]
    r = lax.rsqrt(jnp.mean(val * val, axis=-1, keepdims=True) + EPS)
    err = val * r * gv - rows[0][...]
    part = 0.5 * jnp.sum(jnp.mean(err * err, axis=-1, keepdims=True), axis=0, keepdims=True)
    dx, dg = _rms_bwd_math(val, gv, err * (1.0 / val.shape[1]))
    o_ref[...] = dx
    ob_ref[...] = dx.astype(BF16)
    _accumulate(dg_ref, dg, first)
    _accumulate(l_ref, jnp.broadcast_to(part, (1, LANES)), first)


def _ep_rms_bwd(val, rows, g_ref, outs, first):
    o_ref, ob_ref, dg_ref = outs
    dx, dg = _rms_bwd_math(rows[0][...], g_ref[...], val)
    dx = dx + rows[1][...]
    o_ref[...] = dx
    ob_ref[...] = dx.astype(BF16)
    _accumulate(dg_ref, dg, first)


def _rope_tables(t):
    half = HEAD_DIM // 2
    inv_freq = ROPE_THETA ** (-jnp.arange(half, dtype=F32) / half)
    ang = jnp.arange(t, dtype=F32)[:, None] * inv_freq[None, :]
    cos = jnp.cos(ang)
    sin = jnp.sin(ang)
    reps = LANES // HEAD_DIM
    cos_t = jnp.tile(jnp.concatenate([cos, cos], axis=1), (1, reps))
    sin_t = jnp.tile(jnp.concatenate([-sin, sin], axis=1), (1, reps))
    return cos_t, sin_t


def _rope(v, cos, sin):
    w = v.shape[1]
    c = jnp.tile(cos, (1, w // LANES))
    s = jnp.tile(sin, (1, w // LANES))
    lane = lax.broadcasted_iota(jnp.int32, v.shape, 1)
    first = (lane % HEAD_DIM) < (HEAD_DIM // 2)
    partner = jnp.where(first, pltpu.roll(v, w - HEAD_DIM // 2, 1), pltpu.roll(v, HEAD_DIM // 2, 1))
    return v * c + partner * s


def _heads(v, count):
    return jnp.concatenate([v[:, i * HEAD_DIM:(i + 1) * HEAD_DIM] for i in range(count)], axis=0)


def _unheads(v, count):
    r = v.shape[0] // count
    return jnp.concatenate([v[g * r:(g + 1) * r] for g in range(count)], axis=1)


def _rope_qkv(proj, cos, sin, aw, kvw):
    t = proj.shape[0]
    nkv = kvw // HEAD_DIM
    koff = aw // kvw
    tm = _pick(t, (256, 128))

    def body(q_ref, k_ref, v_ref, c_ref, s_ref, qo_ref, ko_ref, vo_ref):
        c, s = c_ref[...], s_ref[...]
        qo_ref[...] = _rope(q_ref[...].astype(F32), c, s).astype(BF16)
        k = _rope(k_ref[...].astype(F32), c, s).astype(BF16)
        v = v_ref[...].astype(BF16)
        for h in range(nkv):
            ko_ref[h] = k[:, h * HEAD_DIM:(h + 1) * HEAD_DIM]
            vo_ref[h] = v[:, h * HEAD_DIM:(h + 1) * HEAD_DIM]

    def row(w, j):
        return pl.BlockSpec((tm, w), lambda i: (i, j))

    hm = pl.BlockSpec((nkv, tm, HEAD_DIM), lambda i: (0, i, 0))
    return pl.pallas_call(
        body, name="rope_qkv", grid=(t // tm,),
        in_specs=[row(aw, 0), row(kvw, koff), row(kvw, koff + 1), row(LANES, 0), row(LANES, 0)],
        out_specs=[row(aw, 0), hm, hm],
        out_shape=[_sds((t, aw), BF16), _sds((nkv, t, HEAD_DIM), BF16), _sds((nkv, t, HEAD_DIM), BF16)],
        compiler_params=_cparams("parallel"),
    )(proj, proj, proj, cos, sin)


def _attn_probs(qs, kb, n, h, qpk, sinks_ref):
    s = lax.dot_general(qs, kb, (((1,), (1,)), ((), ())), preferred_element_type=F32) * (HEAD_DIM ** -0.5)
    qi = lax.broadcasted_iota(jnp.int32, (BLOCK, 2 * BLOCK), 0)
    kc = lax.broadcasted_iota(jnp.int32, (BLOCK, 2 * BLOCK), 1)
    valid = (kc > qi) & (kc <= qi + BLOCK) & ((kc >= BLOCK) | (n > 0))
    bias = jnp.tile(jnp.where(valid, 0.0, NEG).astype(F32), (qpk, 1))
    s = s + bias
    rowg = lax.broadcasted_iota(jnp.int32, (qpk * BLOCK, 1), 0) // BLOCK
    sink = jnp.zeros((qpk * BLOCK, 1), F32)
    for g in range(qpk):
        sink = jnp.where(rowg == g, sinks_ref[0, h * qpk + g], sink)
    m = jnp.maximum(jnp.max(s, axis=-1, keepdims=True), sink)
    e = jnp.exp(s - m)
    es = jnp.exp(sink - m)
    inv = 1.0 / (jnp.sum(e, axis=-1, keepdims=True) + es)
    return e * inv, es * inv, rowg


HEADS_PER_STEP = 4


def _attn_specs(qw, hp):
    def head(f):
        return pl.BlockSpec((hp, BLOCK, HEAD_DIM), lambda n, h: (h, f(n), 0))

    cur = lambda n: n
    prev = lambda n: jnp.maximum(n - 1, 0)
    return [pl.BlockSpec((BLOCK, hp * qw), lambda n, h: (n, h)), head(cur), head(prev), head(cur), head(prev),
            pl.BlockSpec(memory_space=pltpu.SMEM)]


def _attn_fwd(q_r, k_r, v_h, sinks):
    t, aw = q_r.shape
    nkv = k_r.shape[0]
    qpk = aw // (nkv * HEAD_DIM)
    qw = qpk * HEAD_DIM
    hp = HEADS_PER_STEP if nkv % HEADS_PER_STEP == 0 else 1

    def body(q_ref, kc_ref, kp_ref, vc_ref, vp_ref, sinks_ref, o_ref):
        n, hg = pl.program_id(0), pl.program_id(1)
        outs = []
        for j in range(hp):
            kb = jnp.concatenate([kp_ref[j], kc_ref[j]], axis=0)
            vb = jnp.concatenate([vp_ref[j], vc_ref[j]], axis=0)
            qs = _heads(q_ref[:, j * qw:(j + 1) * qw], qpk)
            p, _, _ = _attn_probs(qs, kb, n, hg * hp + j, qpk, sinks_ref)
            outs.append(_unheads(jnp.dot(p.astype(BF16), vb, preferred_element_type=F32), qpk))
        o_ref[...] = jnp.concatenate(outs, axis=1).astype(o_ref.dtype)

    return pl.pallas_call(
        body, name="attn_fwd", grid=(t // BLOCK, nkv // hp),
        in_specs=_attn_specs(qw, hp),
        out_specs=pl.BlockSpec((BLOCK, hp * qw), lambda n, h: (n, h)),
        out_shape=_sds((t, aw), BF16),
        compiler_params=_cparams("parallel", "parallel"),
    )(q_r, k_r, k_r, v_h, v_h, sinks)


def _attn_bwd(q_r, k_r, v_h, sinks, o, do):
    t, aw = q_r.shape
    nkv = k_r.shape[0]
    qpk = aw // (nkv * HEAD_DIM)
    qw = qpk * HEAD_DIM
    hp = HEADS_PER_STEP if nkv % HEADS_PER_STEP == 0 else 1
    scale = HEAD_DIM ** -0.5

    def body(q_ref, kc_ref, kp_ref, vc_ref, vp_ref, sinks_ref, o_ref, do_ref,
             dq_ref, dkc_ref, dkp_ref, dvc_ref, dvp_ref, ds_ref):
        n, hg = pl.program_id(0), pl.program_id(1)
        lane = lax.broadcasted_iota(jnp.int32, (8, LANES), 1)
        row0 = lax.broadcasted_iota(jnp.int32, (8, LANES), 0) == 0
        dsink = jnp.zeros((8, LANES), F32)
        dqs = []
        for j in range(hp):
            h = hg * hp + j
            cols = slice(j * qw, (j + 1) * qw)
            kb = jnp.concatenate([kp_ref[j], kc_ref[j]], axis=0)
            vb = jnp.concatenate([vp_ref[j], vc_ref[j]], axis=0)
            qs = _heads(q_ref[:, cols], qpk)
            dos = _heads(do_ref[:, cols], qpk)
            p, psink, rowg = _attn_probs(qs, kb, n, h, qpk, sinks_ref)
            pb = p.astype(BF16)
            delta = jnp.sum(dos.astype(F32) * _heads(o_ref[:, cols], qpk).astype(F32), axis=-1, keepdims=True)
            dv = lax.dot_general(pb, dos, (((0,), (0,)), ((), ())), preferred_element_type=F32)
            dp = lax.dot_general(dos, vb, (((1,), (1,)), ((), ())), preferred_element_type=F32)
            dsc = (p * (dp - delta)).astype(BF16)
            dqs.append(_unheads(jnp.dot(dsc, kb, preferred_element_type=F32) * scale, qpk))
            dk = lax.dot_general(dsc, qs, (((0,), (0,)), ((), ())), preferred_element_type=F32) * scale
            dkp_ref[j] = dk[:BLOCK]
            dkc_ref[j] = dk[BLOCK:]
            dvp_ref[j] = dv[:BLOCK]
            dvc_ref[j] = dv[BLOCK:]
            sink_term = psink * delta
            for g in range(qpk):
                val = -jnp.sum(jnp.where(rowg == g, sink_term, 0.0))
                dsink = jnp.where(row0 & (lane == h * qpk + g), val, dsink)
        dq_ref[...] = jnp.concatenate(dqs, axis=1).astype(dq_ref.dtype)
        first = (n == 0) & (hg == 0)

        @pl.when(first)
        def _():
            ds_ref[...] = dsink

        @pl.when(jnp.logical_not(first))
        def _():
            ds_ref[...] += dsink

    qblk = pl.BlockSpec((BLOCK, hp * qw), lambda n, h: (n, h))
    kvblk = pl.BlockSpec((hp, BLOCK, HEAD_DIM), lambda n, h: (h, n, 0))
    return pl.pallas_call(
        body, name="attn_bwd", grid=(t // BLOCK, nkv // hp),
        in_specs=_attn_specs(qw, hp) + [qblk, qblk],
        out_specs=[qblk, kvblk, kvblk, kvblk, kvblk, pl.BlockSpec((8, LANES), lambda n, h: (0, 0))],
        out_shape=[_sds((t, aw), BF16)] + [_sds((nkv, t, HEAD_DIM), F32)] * 4 + [_sds((8, LANES), F32)],
        compiler_params=_cparams("arbitrary", "arbitrary"),
    )(q_r, k_r, k_r, v_h, v_h, sinks, o, do)


HALO = 16


def _shift_down(v, k, halo):
    rows = lax.broadcasted_iota(jnp.int32, v.shape, 0)
    out = pltpu.roll(v, k, 0)
    for r in range(k):
        out = jnp.where(rows == r, halo[HALO - k + r:HALO - k + r + 1, :], out)
    return out


def _shift_up(v, k, halo):
    tm = v.shape[0]
    rows = lax.broadcasted_iota(jnp.int32, v.shape, 0)
    out = pltpu.roll(v, tm - k, 0)
    for r in range(k):
        out = jnp.where(rows == tm - k + r, halo[r:r + 1, :], out)
    return out


def _conv_fwd(proj, conv_w, zoff, cw, cb):
    t = proj.shape[0]
    tm = _pick(t, (512, 256, 128))
    zb, nb = zoff // cb, cw // cb
    hb = tm // HALO

    def body(z_ref, gb_ref, gc_ref, zp_ref, gcp_ref, w_ref, o_ref):
        i = pl.program_id(0)
        cz = gc_ref[...].astype(F32) * z_ref[...].astype(F32)
        czp = gcp_ref[...].astype(F32) * zp_ref[...].astype(F32) * (i > 0).astype(F32)
        w = w_ref[...]
        y = w[0:1] * _shift_down(cz, 2, czp) + w[1:2] * _shift_down(cz, 1, czp) + w[2:3] * cz
        o_ref[...] = (gb_ref[...].astype(F32) * y).astype(o_ref.dtype)

    def col(k):
        return pl.BlockSpec((tm, cb), lambda i, j: (i, zb + k * nb + j))

    def halo(k):
        return pl.BlockSpec((HALO, cb), lambda i, j: (jnp.maximum(i * hb - 1, 0), zb + k * nb + j))

    return pl.pallas_call(
        body, name="conv_fwd", grid=(t // tm, nb),
        in_specs=[col(0), col(1), col(2), halo(0), halo(2), pl.BlockSpec((3, cb), lambda i, j: (0, j))],
        out_specs=pl.BlockSpec((tm, cb), lambda i, j: (i, j)),
        out_shape=_sds((t, cw), BF16),
        compiler_params=_cparams("parallel", "parallel"),
    )(proj, proj, proj, proj, proj, conv_w)


def _conv_bwd(proj, conv_w, dco, zoff, cw, cb):
    t = proj.shape[0]
    tm = _pick(t, (512, 256, 128))
    zb, nb = zoff // cb, cw // cb
    hb = tm // HALO
    nt = t // tm

    def body(z_ref, gb_ref, gc_ref, zp_ref, gcp_ref, gbn_ref, w_ref, d_ref, dn_ref, dz_ref, dgb_ref, dgc_ref, dw_ref):
        i = pl.program_id(1)
        z, gb, gc = z_ref[...].astype(F32), gb_ref[...].astype(F32), gc_ref[...].astype(F32)
        d = d_ref[...].astype(F32)
        cz = gc * z
        czp = gcp_ref[...].astype(F32) * zp_ref[...].astype(F32) * (i > 0).astype(F32)
        w = w_ref[...]
        cz1 = _shift_down(cz, 1, czp)
        cz2 = _shift_down(cz, 2, czp)
        y = w[0:1] * cz2 + w[1:2] * cz1 + w[2:3] * cz
        dgb_ref[...] = (d * y).astype(dgb_ref.dtype)
        dy = d * gb
        dyn = dn_ref[...].astype(F32) * gbn_ref[...].astype(F32) * (i < nt - 1).astype(F32)
        dcz = w[2:3] * dy + w[1:2] * _shift_up(dy, 1, dyn) + w[0:1] * _shift_up(dy, 2, dyn)
        dgc_ref[...] = (dcz * z).astype(dgc_ref.dtype)
        dz_ref[...] = (dcz * gc).astype(dz_ref.dtype)
        rows = lax.broadcasted_iota(jnp.int32, (8, cb), 0)
        dw = jnp.zeros((8, cb), F32)
        for r, tap in enumerate((cz2, cz1, cz)):
            dw = jnp.where(rows == r, jnp.sum(dy * tap, axis=0, keepdims=True), dw)

        @pl.when(i == 0)
        def _():
            dw_ref[...] = dw

        @pl.when(i > 0)
        def _():
            dw_ref[...] += dw

    def col(k):
        return pl.BlockSpec((tm, cb), lambda j, i: (i, zb + k * nb + j))

    def halo_prev(k):
        return pl.BlockSpec((HALO, cb), lambda j, i: (jnp.maximum(i * hb - 1, 0), zb + k * nb + j))

    own = pl.BlockSpec((tm, cb), lambda j, i: (i, j))
    nxt = lambda i: jnp.minimum((i + 1) * hb, t // HALO - 1)
    return pl.pallas_call(
        body, name="conv_bwd", grid=(nb, nt),
        in_specs=[col(0), col(1), col(2), halo_prev(0), halo_prev(2),
                  pl.BlockSpec((HALO, cb), lambda j, i: (nxt(i), zb + nb + j)),
                  pl.BlockSpec((3, cb), lambda j, i: (0, j)), own,
                  pl.BlockSpec((HALO, cb), lambda j, i: (nxt(i), j))],
        out_specs=[own, own, own, pl.BlockSpec((8, cb), lambda j, i: (0, j))],
        out_shape=[_sds((t, cw), BF16)] * 3 + [_sds((8, cw), F32)],
        compiler_params=_cparams("parallel", "arbitrary"),
    )(proj, proj, proj, proj, proj, proj, conv_w, dco, dco)


def _yconv_merge(conv_o, w_cp, ya, proj, goff, cb, dep, name):
    t, k = conv_o.shape
    d = w_cp.shape[1]
    tm = _pick(t, (1024, 512, 256, 128))
    gb_, nb = goff // cb, d // cb
    deps = [] if dep is None else [dep]

    def body(a_ref, w_ref, ya_ref, ga_ref, gc_ref, *rest):
        yc_ref, o_ref = rest[len(deps):]
        yc = jnp.dot(a_ref[...], w_ref[...], preferred_element_type=F32)
        f = lambda r: r[...].astype(F32)
        yc_ref[...] = yc.astype(yc_ref.dtype)
        o_ref[...] = (_sigmoid(f(ga_ref)) * f(ya_ref) + _sigmoid(f(gc_ref)) * yc).astype(o_ref.dtype)

    own = pl.BlockSpec((tm, cb), lambda i, j: (i, j))
    return pl.pallas_call(
        body, name=name, grid=(t // tm, nb),
        in_specs=[pl.BlockSpec((tm, k), lambda i, j: (i, 0)), pl.BlockSpec((k, cb), lambda i, j: (0, j)), own,
                  pl.BlockSpec((tm, cb), lambda i, j: (i, gb_ + j)),
                  pl.BlockSpec((tm, cb), lambda i, j: (i, gb_ + nb + j))] + [ANY] * len(deps),
        out_specs=[own, own], out_shape=[_sds((t, d), BF16)] * 2,
        compiler_params=_cparams("parallel", "parallel"),
    )(conv_o, w_cp, ya, proj, proj, *deps)


def _dmerged_split(dh, w_mo, proj, ya, yc, goff, cb, dep, name):
    t, k = dh.shape
    d = w_mo.shape[0]
    tm = _pick(t, (1024, 512, 256, 128))
    gb_, nb = goff // cb, d // cb
    deps = [] if dep is None else [dep]

    def body(a_ref, w_ref, ga_ref, gc_ref, ya_ref, yc_ref, *rest):
        dya_ref, dyc_ref, dga_ref, dgc_ref = rest[len(deps):]
        dmv = lax.dot_general(a_ref[...], w_ref[...], (((1,), (1,)), ((), ())), preferred_element_type=F32)
        sa = _sigmoid(ga_ref[...].astype(F32))
        sc = _sigmoid(gc_ref[...].astype(F32))
        dya_ref[...] = (dmv * sa).astype(BF16)
        dyc_ref[...] = (dmv * sc).astype(BF16)
        dga_ref[...] = (dmv * ya_ref[...].astype(F32) * sa * (1.0 - sa)).astype(BF16)
        dgc_ref[...] = (dmv * yc_ref[...].astype(F32) * sc * (1.0 - sc)).astype(BF16)

    own = pl.BlockSpec((tm, cb), lambda i, j: (i, j))
    return pl.pallas_call(
        body, name=name, grid=(t // tm, nb),
        in_specs=[pl.BlockSpec((tm, k), lambda i, j: (i, 0)), pl.BlockSpec((cb, k), lambda i, j: (j, 0)),
                  pl.BlockSpec((tm, cb), lambda i, j: (i, gb_ + j)),
                  pl.BlockSpec((tm, cb), lambda i, j: (i, gb_ + nb + j)), own, own] + [ANY] * len(deps),
        out_specs=[own] * 4, out_shape=[_sds((t, d), BF16)] * 4,
        compiler_params=_cparams("parallel", "parallel"),
    )(dh, w_mo, proj, proj, ya, yc, *deps)


def _assemble_dproj(dq, dkc, dkp, dvc, dvp, cos, sin, dz, dgb, dgc, dga, dgg):
    t, aw = dq.shape
    nkv, cw, d = dkc.shape[0], dz.shape[1], dga.shape[1]
    kvw = nkv * HEAD_DIM
    nblk = t // BLOCK
    width = aw + 2 * kvw + 3 * cw + 2 * d

    def body(dq_ref, dkc_ref, dkp_ref, dvc_ref, dvp_ref, c_ref, s_ref, dz_ref, dgb_ref, dgc_ref, dga_ref, dgg_ref,
             o_ref):
        keep = (pl.program_id(0) < nblk - 1).astype(F32)
        c, s = c_ref[...], s_ref[...]
        dk = jnp.concatenate([dkc_ref[h] + dkp_ref[h] * keep for h in range(nkv)], axis=1)
        dv = jnp.concatenate([dvc_ref[h] + dvp_ref[h] * keep for h in range(nkv)], axis=1)
        o_ref[...] = jnp.concatenate(
            [_rope(dq_ref[...].astype(F32), c, -s).astype(BF16), _rope(dk, c, -s).astype(BF16), dv.astype(BF16),
             dz_ref[...], dgb_ref[...], dgc_ref[...], dga_ref[...], dgg_ref[...]], axis=1)

    def cur(w):
        return pl.BlockSpec((BLOCK, w), lambda n: (n, 0))

    head_cur = pl.BlockSpec((nkv, BLOCK, HEAD_DIM), lambda n: (0, n, 0))
    head_nxt = pl.BlockSpec((nkv, BLOCK, HEAD_DIM), lambda n: (0, jnp.minimum(n + 1, nblk - 1), 0))
    return pl.pallas_call(
        body, name="assemble_dproj", grid=(nblk,),
        in_specs=[cur(aw), head_cur, head_nxt, head_cur, head_nxt, cur(LANES), cur(LANES),
                  cur(cw), cur(cw), cur(cw), cur(d), cur(d)],
        out_specs=cur(width), out_shape=_sds((t, width), BF16),
        compiler_params=_cparams("parallel"),
    )(dq, dkc, dkp, dvc, dvp, cos, sin, dz, dgb, dgc, dga, dgg)


def _xattn_probs(qh, kh):
    s = lax.dot_general(qh, kh, (((1,), (1,)), ((), ())), preferred_element_type=F32) * (X_HEAD_DIM ** -0.5)
    e = jnp.exp(s - jnp.max(s, axis=-1, keepdims=True))
    return e * (1.0 / jnp.sum(e, axis=-1, keepdims=True))


def _xattn_fwd(xq, kv):
    t, xw = xq.shape
    mt = kv.shape[0]
    tm = _pick(t, (512, 256, 128))

    def body(q_ref, kv_ref, o_ref):
        outs = []
        for hd in range(xw // X_HEAD_DIM):
            hs = slice(hd * X_HEAD_DIM, (hd + 1) * X_HEAD_DIM)
            vs = slice(xw + hd * X_HEAD_DIM, xw + (hd + 1) * X_HEAD_DIM)
            p = _xattn_probs(q_ref[:, hs], kv_ref[:, hs])
            outs.append(jnp.dot(p.astype(BF16), kv_ref[:, vs], preferred_element_type=F32))
        o_ref[...] = jnp.concatenate(outs, axis=1).astype(o_ref.dtype)

    return pl.pallas_call(
        body, name="xattn_fwd", grid=(t // tm,),
        in_specs=[pl.BlockSpec((tm, xw), lambda i: (i, 0)), pl.BlockSpec((mt, 2 * xw), lambda i: (0, 0))],
        out_specs=pl.BlockSpec((tm, xw), lambda i: (i, 0)), out_shape=_sds((t, xw), BF16),
        compiler_params=_cparams("parallel"),
    )(xq, kv)


def _xattn_bwd(xq, kv, do):
    t, xw = xq.shape
    mt = kv.shape[0]
    tm = _pick(t, (512, 256, 128))
    scale = X_HEAD_DIM ** -0.5

    def body(q_ref, kv_ref, do_ref, dq_ref, dkv_ref):
        dqs, dks, dvs = [], [], []
        for hd in range(xw // X_HEAD_DIM):
            hs = slice(hd * X_HEAD_DIM, (hd + 1) * X_HEAD_DIM)
            vs = slice(xw + hd * X_HEAD_DIM, xw + (hd + 1) * X_HEAD_DIM)
            qh, kh, vh, doh = q_ref[:, hs], kv_ref[:, hs], kv_ref[:, vs], do_ref[:, hs]
            p = _xattn_probs(qh, kh)
            pb = p.astype(BF16)
            o = jnp.dot(pb, vh, preferred_element_type=F32)
            delta = jnp.sum(doh.astype(F32) * o, axis=-1, keepdims=True)
            dvs.append(lax.dot_general(pb, doh, (((0,), (0,)), ((), ())), preferred_element_type=F32))
            dp = lax.dot_general(doh, vh, (((1,), (1,)), ((), ())), preferred_element_type=F32)
            dsc = (p * (dp - delta)).astype(BF16)
            dqs.append(jnp.dot(dsc, kh, preferred_element_type=F32) * scale)
            dks.append(lax.dot_general(dsc, qh, (((0,), (0,)), ((), ())), preferred_element_type=F32) * scale)
        dq_ref[...] = jnp.concatenate(dqs, axis=1).astype(dq_ref.dtype)
        dkv = jnp.concatenate(dks + dvs, axis=1)

        @pl.when(pl.program_id(0) == 0)
        def _():
            dkv_ref[...] = dkv

        @pl.when(pl.program_id(0) > 0)
        def _():
            dkv_ref[...] += dkv

    row = pl.BlockSpec((tm, xw), lambda i: (i, 0))
    whole = pl.BlockSpec((mt, 2 * xw), lambda i: (0, 0))
    return pl.pallas_call(
        body, name="xattn_bwd", grid=(t // tm,),
        in_specs=[row, whole, row], out_specs=[row, whole],
        out_shape=[_sds((t, xw), BF16), _sds((mt, 2 * xw), F32)],
        compiler_params=_cparams("arbitrary"),
    )(xq, kv, do)


def _ffn_in_swiglu(u, w, dep, name):
    t, d = u.shape
    f = w.shape[1] // 2
    tm, tn, _ = _tiles("nn", t, f)
    nf = f // tn
    deps = [] if dep is None else [dep]

    def body(u_ref, wa_ref, wb_ref, *rest):
        h_ref, o_ref = rest[len(deps):]
        a = jnp.dot(u_ref[...], wa_ref[...], preferred_element_type=F32)
        b = jnp.dot(u_ref[...], wb_ref[...], preferred_element_type=F32)
        h_ref[0] = a.astype(h_ref.dtype)
        h_ref[1] = b.astype(h_ref.dtype)
        o_ref[...] = (a * _sigmoid(a) * b).astype(o_ref.dtype)

    return pl.pallas_call(
        body, name=name, grid=(t // tm, nf),
        in_specs=[pl.BlockSpec((tm, d), lambda i, j: (i, 0)), pl.BlockSpec((d, tn), lambda i, j: (0, j)),
                  pl.BlockSpec((d, tn), lambda i, j: (0, nf + j))] + [ANY] * len(deps),
        out_specs=[pl.BlockSpec((2, tm, tn), lambda i, j: (0, i, j)), pl.BlockSpec((tm, tn), lambda i, j: (i, j))],
        out_shape=[_sds((2, t, f), BF16), _sds((t, f), BF16)],
        compiler_params=_cparams("parallel", "parallel"),
    )(u, w, w, *deps)


def _dact_swiglu(dh, w_out, hid, dep, name):
    t, d = dh.shape
    f = w_out.shape[0]
    tm, tn, _ = _tiles("nt", t, f)
    deps = [] if dep is None else [dep]

    def body(dh_ref, w_ref, h_ref, *rest):
        o_ref = rest[len(deps)]
        g = lax.dot_general(dh_ref[...], w_ref[...], (((1,), (1,)), ((), ())), preferred_element_type=F32)
        a = h_ref[0].astype(F32)
        b = h_ref[1].astype(F32)
        sg = _sigmoid(a)
        o_ref[0] = (g * b * sg * (1.0 + a * (1.0 - sg))).astype(o_ref.dtype)
        o_ref[1] = (g * a * sg).astype(o_ref.dtype)

    pair = pl.BlockSpec((2, tm, tn), lambda i, j: (0, i, j))
    return pl.pallas_call(
        body, name=name, grid=(t // tm, f // tn),
        in_specs=[pl.BlockSpec((tm, d), lambda i, j: (i, 0)), pl.BlockSpec((tn, d), lambda i, j: (j, 0)), pair]
        + [ANY] * len(deps),
        out_specs=pair, out_shape=_sds((2, t, f), BF16),
        compiler_params=_cparams("parallel", "parallel"),
    )(dh, w_out, hid, *deps)


def _adamw(w, g, m, v, name):
    r, c = w.shape
    tr = _pick(r, (256, 128, 64, 32, 16, 8)) if r * c > 65536 else r

    def body(w_ref, g_ref, m_ref, v_ref, d_ref, nm_ref, nv_ref, go_ref):
        gv = g_ref[...]
        go_ref[...] = gv
        m2 = ADAM_B1 * m_ref[...] + (1.0 - ADAM_B1) * gv
        v2 = ADAM_B2 * v_ref[...] + (1.0 - ADAM_B2) * (gv * gv)
        m_hat = m2 / (1.0 - ADAM_B1 ** ADAM_STEP)
        v_hat = v2 / (1.0 - ADAM_B2 ** ADAM_STEP)
        d_ref[...] = -ADAM_LR * (m_hat / (jnp.sqrt(v_hat) + ADAM_EPS) + ADAM_WD * w_ref[...])
        nm_ref[...] = m2
        nv_ref[...] = v2

    blk = pl.BlockSpec((tr, c), lambda i: (i, 0))
    return pl.pallas_call(
        body, name=name, grid=(r // tr,),
        in_specs=[blk] * 4, out_specs=[blk] * 4, out_shape=[_sds((r, c), F32)] * 4,
        compiler_params=_cparams("parallel"),
    )(w, g, m, v)


class _Weights:
    def __init__(self, full):
        self.full = full
        self.grads = {}

    def get(self, name):
        return self.full[name]

    def mark(self, tag, value):
        return value

    def grad(self, name, g):
        self.grads[name] = g

    def dep(self):
        return None


def _local_step(x, mem, tgt, g_mix, sinks, g_xattn, g_mem, g_ffn, g_final, dims, wts):
    t, d = x.shape
    aw, cw, kvw = dims
    cb = 2 * kvw
    zoff = aw + 2 * kvw
    goff = zoff + 3 * cw
    cos, sin = _rope_tables(t)
    mark, get = wts.mark, wts.get

    def mm(a, b, **kw):
        return _matmul(a, b, dep=wts.dep(), **kw)

    u1 = mark("u1", _rms_fwd(x, g_mix, "rms_mix"))
    qkv = mm(u1, get("w_in"), mode="nn", out_dtype=F32, name="mm_qkv", cols=(0, zoff))
    proj = mark("proj", mm(u1, get("w_in"), mode="nn", out_dtype=BF16, name="mm_gates", cols=(zoff, goff - zoff + 2 * d)))
    zoff, goff = 0, goff - zoff
    q_r, k_r, v_h = _rope_qkv(qkv, cos, sin, aw, kvw)
    attn_o = mark("attn_o", _attn_fwd(q_r, k_r, v_h, sinks))
    conv_o = mark("conv_o", _conv_fwd(proj, get("conv_w"), zoff, cw, cb))
    ya = mark("ya", mm(attn_o, get("w_attn_proj"), mode="nn", out_dtype=BF16, name="mm_yattn"))
    yc, merged = _yconv_merge(conv_o, get("w_conv_proj"), ya, proj, goff, cb, wts.dep(), "mm_yconv")
    mark("merged", merged)
    norm_outs = [("row", F32), ("row", BF16)]
    bwd_outs = [("row", F32), ("row", BF16), ("vec",)]
    h1, u2 = _rows_matmul(merged, get("w_mix_out"), mode="nn", kslice=(0, 1), res=x, rows_in=[], vec=g_xattn,
                          epilogue=_ep_norm, outs=norm_outs, name="mm_mix", dep=wts.dep())
    mark("h1", h1)
    mem_n = _rms_fwd(mem, g_mem, "rms_mem")
    xq = mark("xq", mm(u2, get("w_xq"), mode="nn", out_dtype=BF16, name="mm_xq"))
    kv = mm(mem_n, get("w_xkv"), mode="nn", out_dtype=BF16, name="mm_xkv")
    xo = mark("xo", _xattn_fwd(xq, kv))
    h2, u3 = _rows_matmul(xo, get("w_xo"), mode="nn", kslice=(0, 1), res=h1, rows_in=[], vec=g_ffn,
                          epilogue=_ep_norm, outs=norm_outs, name="mm_xo", dep=wts.dep())
    mark("h2", h2)
    mark("u3", u3)
    hid, act = _ffn_in_swiglu(u3, get("w_ffn_in"), wts.dep(), "mm_ffn_in")
    mark("hid", hid)
    dh3, dh3b, dg_final, loss = _rows_matmul(
        act, get("w_ffn_out"), mode="nn", kslice=(0, 1), res=h2, rows_in=[tgt], vec=g_final, epilogue=_ep_loss,
        outs=bwd_outs + [("lane",)], name="mm_ffn_out", dep=wts.dep())
    mark("dh3", dh3b)

    wts.grad("w_ffn_out", mm(act, dh3b, mode="tn", out_dtype=BF16, name="mm_dw_ffn_out"))
    dhid = mark("dhid", _dact_swiglu(dh3b, get("w_ffn_out"), hid, wts.dep(), "mm_dact"))
    wts.grad("w_ffn_in", mm(u3, dhid, mode="tn", out_dtype=BF16, name="mm_dw_ffn_in"))
    du3 = mark("du3", _matmul_slice(dhid, get("w_ffn_in"), mode="nt", out_dtype=F32, name="mm_du3_k0", res=None,
                                    dep=wts.dep(), kslice=(0, 2), cols=(0, d)))
    dh2, dh2b, dg_ffn = _rows_matmul(dhid, get("w_ffn_in"), mode="nt", kslice=(1, 2), res=du3, rows_in=[h2, dh3],
                                     vec=g_ffn, epilogue=_ep_rms_bwd, outs=bwd_outs, name="mm_du3_k1", dep=wts.dep())
    mark("dh2", dh2b)
    wts.grad("w_xo", mm(xo, dh2b, mode="tn", out_dtype=BF16, name="mm_dw_xo"))
    dxo = mm(dh2b, get("w_xo"), mode="nt", out_dtype=BF16, name="mm_dxo")
    dxq, dkv = _xattn_bwd(xq, kv, dxo)
    dkvb = dkv.astype(BF16)
    wts.grad("w_xq", mm(u2, dxq, mode="tn", out_dtype=BF16, name="mm_dw_xq"))
    wts.grad("w_xkv", mm(mem_n, dkvb, mode="tn", out_dtype=BF16, name="mm_dw_xkv"))
    dmem_n = mm(dkvb, get("w_xkv"), mode="nt", out_dtype=F32, name="mm_dmem")
    _, _, dg_mem = _rms_bwd(mem, g_mem, dmem_n, None, "rms_bwd_mem")
    dh1, dh1b, dg_xattn = _rows_matmul(dxq, get("w_xq"), mode="nt", kslice=(0, 1), res=None, rows_in=[h1, dh2],
                                       vec=g_xattn, epilogue=_ep_rms_bwd, outs=bwd_outs, name="mm_du2", dep=wts.dep())
    mark("dh1", dh1b)
    wts.grad("w_mix_out", mm(merged, dh1b, mode="tn", out_dtype=BF16, name="mm_dw_mix"))
    dya, dyc, dga, dgg = _dmerged_split(dh1b, get("w_mix_out"), proj, ya, yc, goff, cb, wts.dep(), "mm_dmerged")
    mark("dya", dya)
    wts.grad("w_attn_proj", mm(attn_o, dya, mode="tn", out_dtype=BF16, name="mm_dw_attn_proj"))
    dattn_o = mm(dya, get("w_attn_proj"), mode="nt", out_dtype=BF16, name="mm_dattn")
    wts.grad("w_conv_proj", mm(conv_o, dyc, mode="tn", out_dtype=BF16, name="mm_dw_conv_proj"))
    dconv_o = mark("dconv_o", mm(dyc, get("w_conv_proj"), mode="nt", out_dtype=BF16, name="mm_dconv"))
    dz, dgb, dgc, dconv_w = _conv_bwd(proj, get("conv_w"), dconv_o, zoff, cw, cb)
    mark("dz", dz)
    dq, dkc, dkp, dvc, dvp, dsinks = _attn_bwd(q_r, k_r, v_h, sinks, attn_o, dattn_o)
    mark("dq", dq)
    dproj = mark("dproj", _assemble_dproj(dq, dkc, dkp, dvc, dvp, cos, sin, dz, dgb, dgc, dga, dgg))
    wts.grad("w_in", mm(u1, dproj, mode="tn", out_dtype=BF16, name="mm_dw_in"))
    du1 = mark("du1", _matmul_slice(dproj, get("w_in"), mode="nt", out_dtype=F32, name="mm_du1_k0", res=None,
                                    dep=wts.dep(), kslice=(0, 2), cols=(0, d)))
    grad_x, _, dg_mix = _rows_matmul(dproj, get("w_in"), mode="nt", kslice=(1, 2), res=du1, rows_in=[x, dh1],
                                     vec=g_mix, epilogue=_ep_rms_bwd, outs=bwd_outs, name="mm_du1_k1", dep=wts.dep())
    mark("grad_x", grad_x)

    small = dict(g_mix=dg_mix, g_xattn=dg_xattn, g_mem=dg_mem, g_ffn=dg_ffn, g_final=dg_final,
                 conv_w=dconv_w[:3], attn_sinks=dsinks[0:1, :sinks.shape[1]], loss=loss[0:1, 0:1])
    return grad_x, small


BIG = (("w_in", 1), ("w_attn_proj", 1), ("w_conv_proj", 1), ("w_mix_out", 0), ("w_xq", 0), ("w_xkv", 0),
       ("w_xo", 1), ("w_ffn_in", 1), ("w_ffn_out", 0))


def _place():
    x, y, c = lax.axis_index("x"), lax.axis_index("y"), lax.axis_index("c")
    chips = [(1 - x, y), (x, 1 - y), (1 - x, 1 - y)]
    return x, y, c, chips


def _window(ref, ax, shard_shape, s, h):
    sr, sc = shard_shape
    hr = sr // 2
    if ax == 1:
        return ref.at[pl.ds(pl.multiple_of(h * hr, 16), hr), pl.ds(pl.multiple_of(s * sc, LANES), sc)]
    return ref.at[pl.ds(pl.multiple_of(s * sr + h * hr, 16), hr), :]


def _half(ref, h):
    hr = ref.shape[0] // 2
    return ref.at[pl.ds(pl.multiple_of(h * hr, 16), hr), :]


def _remote(src, dst, send_sem, recv_sem, dev):
    return pltpu.make_async_remote_copy(src_ref=src, dst_ref=dst, send_sem=send_sem, recv_sem=recv_sem,
                                        device_id=dev, device_id_type=MESH)


def _cast_to_full(shard, ax, me, name, dtype=BF16):
    sr, sc = shard.shape
    tr = _pick(sr, (256, 352, 128, 64, 32, 16))
    nr = sr // tr
    full = (sr * N_CHIPS, sc) if ax == 0 else (sr, sc * N_CHIPS)

    def body(me_ref, s_ref, o_ref):
        o_ref[...] = s_ref[...].astype(o_ref.dtype)

    if ax == 1:
        out_spec = pl.BlockSpec((tr, sc), lambda r, me_ref: (r, me_ref[0]))
    else:
        out_spec = pl.BlockSpec((tr, sc), lambda r, me_ref: (me_ref[0] * nr + r, 0))
    return pl.pallas_call(
        body, name=name,
        grid_spec=pltpu.PrefetchScalarGridSpec(
            num_scalar_prefetch=1, grid=(nr,), in_specs=[pl.BlockSpec((tr, sc), lambda r, me_ref: (r, 0))],
            out_specs=out_spec),
        out_shape=_sds(full, dtype),
        compiler_params=_cparams("parallel"),
    )(me, shard)


HBM = pl.BlockSpec(memory_space=pltpu.HBM)
SEM = pl.BlockSpec(memory_space=pltpu.SEMAPHORE)
EFFECT = pltpu.SideEffectType.DATAFLOW_SIDE_EFFECTING


def _in_hbm(a):
    return pltpu.with_memory_space_constraint(a, pltpu.HBM)


def _gather_window(ref, ax, shard_shape, s, h):
    if h is not None:
        return _window(ref, ax, shard_shape, s, h)
    sr, sc = shard_shape
    if ax == 1:
        return ref.at[:, pl.ds(pl.multiple_of(s * sc, LANES), sc)]
    return ref.at[pl.ds(pl.multiple_of(s * sr, 8), sr), :]


def _ag_start(fulls, axes, shard_shapes, whole):
    n = len(fulls)

    def body(*refs):
        src = refs[:n]
        send_sems, recv_sems = refs[n], refs[n + 1]
        token = refs[2 * n + 2]
        x, y, c, chips = _place()
        me = 2 * x + y
        for i in range(n):
            h = None if whole[i] else c
            for j, chip in enumerate(chips):
                blk = _gather_window(src[i], axes[i], shard_shapes[i], me, h)
                _remote(blk, blk, send_sems.at[3 * i + j], recv_sems.at[3 * i + j], (*chip, c)).start()
        token[...] = jnp.zeros_like(token)

    res = pl.pallas_call(
        body, name="ag_start_" + str(n),
        out_shape=(pltpu.SemaphoreType.DMA((3 * n,)), pltpu.SemaphoreType.DMA((3 * n,)),
                   *[pltpu.HBM(f.shape, f.dtype) for f in fulls], _sds((8, LANES), F32)),
        in_specs=[HBM] * n, out_specs=(SEM, SEM, *[HBM] * n, pl.BlockSpec(memory_space=pltpu.VMEM)),
        input_output_aliases={i: 2 + i for i in range(n)},
        compiler_params=pltpu.CompilerParams(has_side_effects=EFFECT),
    )(*[_in_hbm(f) for f in fulls])
    return res[0], res[1], list(res[2:2 + n]), res[2 + n]


def _ag_mid(bufs, slots, axes, shard_shapes, whole, send_sems, recv_sems, after, name):
    ng = len(bufs)

    def body(*refs):
        src = refs[:ng]
        s_in, r_in = refs[ng], refs[ng + 1]
        fsend, frecv = refs[ng + 3], refs[ng + 4]
        x, y, c, chips = _place()
        me = 2 * x + y
        sib = (x, y, 1 - c)
        for k, i in enumerate(slots):
            h = None if whole[k] else c
            for j, chip in enumerate(chips):
                cj = 2 * chip[0] + chip[1]
                mine = _gather_window(src[k], axes[k], shard_shapes[k], me, h)
                theirs = _gather_window(src[k], axes[k], shard_shapes[k], cj, h)
                _remote(theirs, theirs, s_in.at[3 * i + j], r_in.at[3 * i + j], (*chip, c)).wait_recv()
                _remote(mine, mine, s_in.at[3 * i + j], r_in.at[3 * i + j], (*chip, c)).wait_send()
                if not whole[k]:
                    _remote(theirs, theirs, fsend.at[3 * k + j], frecv.at[3 * k + j], sib).start()
        token = refs[2 * ng + 5]
        token[...] = jnp.zeros_like(token)

    res = pl.pallas_call(
        body, name=name,
        out_shape=(pltpu.SemaphoreType.DMA((3 * ng,)), pltpu.SemaphoreType.DMA((3 * ng,)),
                   *[pltpu.HBM(b.shape, b.dtype) for b in bufs], _sds((8, LANES), F32)),
        in_specs=[HBM] * ng + [SEM, SEM, ANY],
        out_specs=(SEM, SEM, *[HBM] * ng, pl.BlockSpec(memory_space=pltpu.VMEM)),
        input_output_aliases={k: 2 + k for k in range(ng)},
        compiler_params=pltpu.CompilerParams(has_side_effects=EFFECT),
    )(*bufs, send_sems, recv_sems, after)
    return res[0], res[1], list(res[2:2 + ng]), res[2 + ng]


def _ag_wait(bufs, axes, shard_shapes, whole, fsend, frecv, after, name):
    ng = len(bufs)

    def body(*refs):
        src = refs[:ng]
        s_in, r_in = refs[ng], refs[ng + 1]
        x, y, c, chips = _place()
        sib = (x, y, 1 - c)
        for k in range(ng):
            if whole[k]:
                continue
            for j, chip in enumerate(chips):
                cj = 2 * chip[0] + chip[1]
                sent = _gather_window(src[k], axes[k], shard_shapes[k], cj, c)
                landed = _gather_window(src[k], axes[k], shard_shapes[k], cj, 1 - c)
                _remote(landed, landed, s_in.at[3 * k + j], r_in.at[3 * k + j], sib).wait_recv()
                _remote(sent, sent, s_in.at[3 * k + j], r_in.at[3 * k + j], sib).wait_send()

    res = pl.pallas_call(
        body, name=name,
        out_shape=tuple(pltpu.HBM(b.shape, b.dtype) for b in bufs),
        in_specs=[HBM] * ng + [SEM, SEM, ANY], out_specs=tuple([HBM] * ng),
        input_output_aliases={k: k for k in range(ng)},
        compiler_params=pltpu.CompilerParams(has_side_effects=EFFECT),
    )(*bufs, fsend, frecv, after)
    return list(res)


class _Schedule:
    GROUPS = ((("w_in", "conv_w"), "u1", "u1"),
              (("w_attn_proj", "w_conv_proj", "w_mix_out", "w_xq", "w_xkv", "w_xo"), "proj", "conv_o"),
              (("w_ffn_in",), "h1", "h2"),
              (("w_ffn_out",), "u3", "hid"))
    STARTS = ((0,), (1, 2, 3))
    REDUCE = ((("w_ffn_out",), "dhid", "grad:w_ffn_in", "du3"),
              (("w_ffn_in",), "grad:w_ffn_in", "dya", "grad_x"),
              (("w_xo", "w_xq", "w_xkv", "w_mix_out", "w_attn_proj", "w_conv_proj"), "dconv_o", "dq", "grad_x"),
              (("w_in",), "grad:w_in", "end", "end2"))

    def __init__(self, seed, axes, shard_shapes, place, on_ready):
        self.ax, self.shape, self.place, self.on_ready = axes, shard_shapes, place, on_ready
        self.stage, self.buf, self.slot, self.passes = {}, {}, {}, {}
        self.ready = set()
        self.grads = {}
        token = None
        for groups in self.STARTS:
            order = [nm for g in groups for nm in self.GROUPS[g][0]]
            send, recv, bufs, token = _ag_start([seed(nm, token) for nm in order], *self._meta(order))
            self.buf.update(zip(order, bufs))
            self.slot.update({nm: (send, recv, k) for k, nm in enumerate(order)})
        self.token = self.latest = token
        self.mark("start", token)

    def _meta(self, names):
        return ([self.ax[nm] for nm in names], [self.shape[nm] for nm in names], [nm == "conv_w" for nm in names])

    def mark(self, tag, value):
        for g, (names, mid, wait) in enumerate(self.GROUPS):
            if tag == mid:
                send, recv, _ = self.slot[names[0]]
                fs, fr, bufs, self.latest = _ag_mid([self.buf[nm] for nm in names], [self.slot[nm][2] for nm in names],
                                                    *self._meta(names), send, recv, value, "ag_mid_%d" % g)
                self.buf.update(zip(names, bufs))
                self.passes[g] = (fs, fr)
            if tag == wait:
                fs, fr = self.passes[g]
                bufs = _ag_wait([self.buf[nm] for nm in names], *self._meta(names), fs, fr, value, "ag_wait_%d" % g)
                self.buf.update(zip(names, bufs))
                self.ready.update(names)
        for g, (names, send, total, finish) in enumerate(self.REDUCE):
            st = self.stage.get(g)
            if st is None:
                continue
            ng = len(names)
            if tag == send and st["at"] == "pair":
                arrs = _exchange_wait("rs_pair_wait_%d" % g, st["arrs"], *st["sems"], st["plan"], value)
                parts = [_pair_add(arrs[k], arrs[ng + k], self.ax[nm], self.shape[nm], self.place, "pair_add_" + nm)
                         for k, nm in enumerate(names)]
                plan, nsem = _plan_chip(ng)
                ss, rs, arrs, self.latest = _exchange_start(
                    "rs_chip_start_%d" % g, parts + [lax.empty(p.shape, p.dtype) for p in parts], nsem, plan)
                self.stage[g] = dict(at="chip", arrs=arrs, sems=(ss, rs), plan=plan)
            elif tag == total and st["at"] == "chip":
                arrs = _exchange_wait("rs_chip_wait_%d" % g, st["arrs"], *st["sems"], st["plan"], value)
                halves = [_chip_add(arrs[k], arrs[ng + k], self.place, "chip_add_" + nm) for k, nm in enumerate(names)]
                plan, nsem = _plan_gather(ng)
                ss, rs, arrs, self.latest = _exchange_start("rs_gather_start_%d" % g, halves, nsem, plan)
                self.stage[g] = dict(at="gather", arrs=arrs, sems=(ss, rs), plan=plan)
            elif tag == finish and st["at"] == "gather":
                arrs = _exchange_wait("rs_gather_wait_%d" % g, st["arrs"], *st["sems"], st["plan"], value)
                self.stage[g] = dict(at="done")
                for nm, shard in zip(names, arrs):
                    self.on_ready(nm, shard)
        return value

    def get(self, name):
        assert name in self.ready, name
        return self.buf[name]

    def grad(self, name, g):
        self.grads[name] = g
        for gi, (names, _, _, _) in enumerate(self.REDUCE):
            if name == names[-1]:
                gs = [self.grads[nm] for nm in names]
                plan, nsem = _plan_pair(len(names), [self.ax[nm] for nm in names], [self.shape[nm] for nm in names])
                ss, rs, arrs, self.latest = _exchange_start(
                    "rs_pair_start_%d" % gi, gs + [lax.empty(a.shape, a.dtype) for a in gs], nsem, plan)
                self.stage[gi] = dict(at="pair", arrs=arrs, sems=(ss, rs), plan=plan)
                g = self.latest
        self.mark("grad:" + name, g)

    def dep(self):
        return self.latest


def _exchange_start(name, arrays, nsem, plan):
    n = len(arrays)

    def body(*refs):
        send_sems, recv_sems, token = refs[n], refs[n + 1], refs[2 * n + 2]
        sends, _ = plan(refs[:n])
        for k, (src, dst, dev) in enumerate(sends):
            _remote(src, dst, send_sems.at[k], recv_sems.at[k], dev).start()
        token[...] = jnp.zeros_like(token)

    res = pl.pallas_call(
        body, name=name,
        out_shape=(pltpu.SemaphoreType.DMA((nsem,)), pltpu.SemaphoreType.DMA((nsem,)),
                   *[pltpu.HBM(a.shape, a.dtype) for a in arrays], _sds((8, LANES), F32)),
        in_specs=[HBM] * n, out_specs=(SEM, SEM, *[HBM] * n, pl.BlockSpec(memory_space=pltpu.VMEM)),
        input_output_aliases={i: 2 + i for i in range(n)},
        compiler_params=pltpu.CompilerParams(has_side_effects=EFFECT),
    )(*[_in_hbm(a) for a in arrays])
    return res[0], res[1], list(res[2:2 + n]), res[2 + n]


def _exchange_wait(name, arrays, send_sems, recv_sems, plan, after):
    n = len(arrays)

    def body(*refs):
        s_in, r_in = refs[n], refs[n + 1]
        sends, recvs = plan(refs[:n])
        for k, land in enumerate(recvs):
            _remote(land, land, s_in.at[k], r_in.at[k], sends[k][2]).wait_recv()
        for k, (src, _, dev) in enumerate(sends):
            _remote(src, src, s_in.at[k], r_in.at[k], dev).wait_send()

    res = pl.pallas_call(
        body, name=name,
        out_shape=tuple(pltpu.HBM(a.shape, a.dtype) for a in arrays),
        in_specs=[HBM] * n + [SEM, SEM, ANY], out_specs=tuple([HBM] * n),
        input_output_aliases={i: i for i in range(n)},
        compiler_params=pltpu.CompilerParams(has_side_effects=EFFECT),
    )(*arrays, send_sems, recv_sems, after)
    return list(res)


def _plan_pair(n, axes, shard_shapes):
    def plan(refs):
        g, ra = refs[:n], refs[n:]
        x, y, c, _ = _place()
        sib = (x, y, 1 - c)

        def pieces(ref, i, h):
            if axes[i] == 1:
                return [_half(ref, h)]
            return [_window(ref, 0, shard_shapes[i], s, h) for s in range(N_CHIPS)]

        sends, recvs = [], []
        for i in range(n):
            sends += [(src, dst, sib) for src, dst in zip(pieces(g[i], i, 1 - c), pieces(ra[i], i, 1 - c))]
            recvs += pieces(ra[i], i, c)
        return sends, recvs

    return plan, sum(1 if ax == 1 else N_CHIPS for ax in axes)


def _plan_chip(n):
    def plan(refs):
        p, rc = refs[:n], refs[n:]
        x, y, c, chips = _place()
        me = 2 * x + y
        sends, recvs = [], []
        for i in range(n):
            for chip in chips:
                cj = 2 * chip[0] + chip[1]
                sends.append((p[i].at[cj], rc[i].at[me], (*chip, c)))
                recvs.append(rc[i].at[cj])
        return sends, recvs

    return plan, 3 * n


def _plan_gather(n):
    def plan(refs):
        x, y, c, _ = _place()
        sib = (x, y, 1 - c)
        return ([(_half(r, c), _half(r, c), sib) for r in refs], [_half(r, 1 - c) for r in refs])

    return plan, n


def _pair_add(g, ra, ax, shard_shape, place, name):
    sr, sc = shard_shape
    hr = sr // 2
    wc = sc
    tr = _pick(hr, (256, 352, 128, 64, 32, 16))
    nr = hr // tr

    def body(p_ref, a_ref, b_ref, o_ref):
        o_ref[...] = (a_ref[...].astype(F32) + b_ref[...].astype(F32)).astype(o_ref.dtype)

    if ax == 1:
        src = pl.BlockSpec((tr, wc), lambda s, r, p_ref: (p_ref[0] * nr + r, s))
    else:
        src = pl.BlockSpec((tr, wc), lambda s, r, p_ref: (s * 2 * nr + p_ref[0] * nr + r, 0))
    return pl.pallas_call(
        body, name=name,
        grid_spec=pltpu.PrefetchScalarGridSpec(
            num_scalar_prefetch=1, grid=(N_CHIPS, nr), in_specs=[src, src],
            out_specs=pl.BlockSpec((None, tr, wc), lambda s, r, p_ref: (s, r, 0))),
        out_shape=_sds((N_CHIPS, hr, wc), BF16),
        compiler_params=_cparams("parallel", "parallel"),
    )(place, g, ra)


def _chip_add(part, rc, place, name):
    _, hr, wc = rc.shape
    tr = _pick(hr, (256, 352, 128, 64, 32, 16))
    nr = hr // tr

    def body(p_ref, own_ref, r1_ref, r2_ref, r3_ref, o_ref):
        acc = own_ref[...].astype(F32)
        for r_ref in (r1_ref, r2_ref, r3_ref):
            acc = acc + r_ref[...].astype(F32)
        o_ref[...] = acc

    def slot(k):
        return pl.BlockSpec((None, tr, wc), lambda r, p_ref: ((p_ref[1] + k) % N_CHIPS, r, 0))

    return pl.pallas_call(
        body, name=name,
        grid_spec=pltpu.PrefetchScalarGridSpec(
            num_scalar_prefetch=1, grid=(nr,), in_specs=[slot(0), slot(1), slot(2), slot(3)],
            out_specs=pl.BlockSpec((tr, wc), lambda r, p_ref: (p_ref[0] * nr + r, 0))),
        out_shape=_sds((2 * hr, wc), F32),
        compiler_params=_cparams("parallel"),
    )(place, part, rc, rc, rc)


N_DEV = 8


def _all_reduce_small(buf):
    r, cdim = buf.shape

    def body(x_ref, o_ref, land, send_sems, recv_sems):
        x, y, c, _ = _place()
        me = 4 * x + 2 * y + c
        land[me] = x_ref[...]
        sends = []
        for k in range(1, N_DEV):
            kx, ky, kc = (k >> 2) & 1, (k >> 1) & 1, k & 1
            peer = (1 - x if kx else x, 1 - y if ky else y, 1 - c if kc else c)
            cp = _remote(x_ref, land.at[me], send_sems.at[k - 1], recv_sems.at[k - 1], peer)
            cp.start()
            sends.append(cp)
        for k in range(1, N_DEV):
            kx, ky, kc = (k >> 2) & 1, (k >> 1) & 1, k & 1
            peer = (1 - x if kx else x, 1 - y if ky else y, 1 - c if kc else c)
            pidx = 4 * peer[0] + 2 * peer[1] + peer[2]
            _remote(land.at[pidx], land.at[pidx], send_sems.at[k - 1], recv_sems.at[k - 1], peer).wait_recv()
        for cp in sends:
            cp.wait_send()
        acc = land[0]
        for dev in range(1, N_DEV):
            acc = acc + land[dev]
        o_ref[...] = acc

    vm = pl.BlockSpec(memory_space=pltpu.VMEM)
    return pl.pallas_call(
        body, name="all_reduce_small", in_specs=[vm], out_specs=vm, out_shape=_sds((r, cdim), F32),
        scratch_shapes=[pltpu.VMEM((N_DEV, r, cdim), F32), pltpu.SemaphoreType.DMA((N_DEV - 1,)),
                        pltpu.SemaphoreType.DMA((N_DEV - 1,))],
    )(buf)


SMALL_ROWS = 16


def kernel(x, mem, g_mix, w_in, conv_w, attn_sinks, w_attn_proj, w_conv_proj, w_mix_out, g_xattn, g_mem, w_xq, w_xkv, w_xo, g_ffn, w_ffn_in, w_ffn_out, g_final, loss_target, m_g_mix, m_w_in, m_conv_w, m_attn_sinks, m_w_attn_proj, m_w_conv_proj, m_w_mix_out, m_g_xattn, m_g_mem, m_w_xq, m_w_xkv, m_w_xo, m_g_ffn, m_w_ffn_in, m_w_ffn_out, m_g_final, v_g_mix, v_w_in, v_conv_w, v_attn_sinks, v_w_attn_proj, v_w_conv_proj, v_w_mix_out, v_g_xattn, v_g_mem, v_w_xq, v_w_xkv, v_w_xo, v_g_ffn, v_w_ffn_in, v_w_ffn_out, v_g_final):
    w = dict(g_mix=g_mix, w_in=w_in[0], conv_w=conv_w[0], attn_sinks=attn_sinks, w_attn_proj=w_attn_proj[0],
             w_conv_proj=w_conv_proj[0], w_mix_out=w_mix_out[0], g_xattn=g_xattn, g_mem=g_mem, w_xq=w_xq[0],
             w_xkv=w_xkv[0], w_xo=w_xo[0], g_ffn=g_ffn, w_ffn_in=w_ffn_in[0], w_ffn_out=w_ffn_out[0],
             g_final=g_final[None])
    m = dict(g_mix=m_g_mix, w_in=m_w_in[0], conv_w=m_conv_w[0], attn_sinks=m_attn_sinks,
             w_attn_proj=m_w_attn_proj[0], w_conv_proj=m_w_conv_proj[0], w_mix_out=m_w_mix_out[0],
             g_xattn=m_g_xattn, g_mem=m_g_mem, w_xq=m_w_xq[0], w_xkv=m_w_xkv[0], w_xo=m_w_xo[0], g_ffn=m_g_ffn,
             w_ffn_in=m_w_ffn_in[0], w_ffn_out=m_w_ffn_out[0], g_final=m_g_final[None])
    v = dict(g_mix=v_g_mix, w_in=v_w_in[0], conv_w=v_conv_w[0], attn_sinks=v_attn_sinks,
             w_attn_proj=v_w_attn_proj[0], w_conv_proj=v_w_conv_proj[0], w_mix_out=v_w_mix_out[0],
             g_xattn=v_g_xattn, g_mem=v_g_mem, w_xq=v_w_xq[0], w_xkv=v_w_xkv[0], w_xo=v_w_xo[0], g_ffn=v_g_ffn,
             w_ffn_in=v_w_ffn_in[0], w_ffn_out=v_w_ffn_out[0], g_final=v_g_final[None])
    names = [nm for nm, _ in BIG]
    axes = [ax for _, ax in BIG]
    d = x.shape[2]
    cw = w["conv_w"].shape[1] * N_CHIPS
    chip = (2 * lax.axis_index("x") + lax.axis_index("y")).astype(jnp.int32)
    place = jnp.stack([lax.axis_index("c").astype(jnp.int32), chip])
    shard_shapes = [w[nm].shape for nm in names]

    def seed(nm, token):
        me1 = chip.reshape(1)
        if token is not None:
            me1 = me1 + token[0, 0].astype(jnp.int32)
        if nm == "conv_w":
            return _cast_to_full(w[nm], 1, me1, "place_conv_w", F32)
        return _cast_to_full(w[nm], dict(BIG)[nm], me1, "cast_" + nm)

    upd = {}

    def on_ready(nm, shard):
        upd[nm] = _adamw(w[nm], shard, m[nm], v[nm], "adamw_" + nm)
        grads[nm] = upd[nm][3]

    grads = {}
    wts = _Schedule(seed, dict(zip(names + ["conv_w"], axes + [1])),
                    dict(zip(names + ["conv_w"], shard_shapes + [w["conv_w"].shape])), place, on_ready)
    aw, cw = w["w_attn_proj"].shape[0], w["w_conv_proj"].shape[0]
    kvw = (w["w_in"].shape[1] * N_CHIPS - aw - 3 * cw - 2 * d) // 2
    grad_x, small = _local_step(
        x[0], mem[0], loss_target[0], w["g_mix"] + wts.token[0:1, 0:1], w["attn_sinks"], w["g_xattn"], w["g_mem"],
        w["g_ffn"], w["g_final"], (aw, cw, kvw), wts)

    pw = max(d, cw)

    def row(a):
        return jnp.pad(a, ((0, 0), (0, pw - a.shape[1])))

    gains = ("g_mix", "g_xattn", "g_mem", "g_ffn", "g_final")
    packed = jnp.concatenate(
        [row(small[nm]) for nm in gains] + [row(small["conv_w"]),
         row(jnp.concatenate([small["attn_sinks"], small["loss"]], axis=1)),
         jnp.zeros((SMALL_ROWS - 9, pw), F32)], axis=0)
    total = _all_reduce_small(packed)
    wts.mark("end", total)
    nsink = attn_sinks.shape[1]
    grads.update({nm: total[k:k + 1, :d] for k, nm in enumerate(gains)})
    grads.update(conv_w=lax.dynamic_slice(total, (5, chip * (cw // N_CHIPS)), (3, cw // N_CHIPS)),
                 attn_sinks=total[8:9, :nsink])
    loss = total[8, nsink]
    for nm in gains + ("conv_w", "attn_sinks"):
        upd[nm] = _adamw(w[nm], grads[nm], m[nm], v[nm], "adamw_" + nm)
    wts.mark("end2", upd["g_final"][0])

    order = ["g_mix", "w_in", "conv_w", "attn_sinks", "w_attn_proj", "w_conv_proj", "w_mix_out", "g_xattn", "g_mem",
             "w_xq", "w_xkv", "w_xo", "g_ffn", "w_ffn_in", "w_ffn_out", "g_final"]

    stacked = set(names) | {"conv_w"}

    def shaped(nm, a):
        if nm == "g_final":
            return a[0]
        return a[None] if nm in stacked else a

    outs = [loss, grad_x[None]]
    outs += [shaped(nm, grads[nm]) for nm in order]
    for k in range(3):
        outs += [shaped(nm, upd[nm][k]) for nm in order]
    return tuple(outs)
```

```python
import functools

import jax
import jax.numpy as jnp
from jax import lax
from jax.experimental import pallas as pl
from jax.experimental.pallas import tpu as pltpu

F32 = jnp.float32
BF16 = jnp.bfloat16

VMEM_LIMIT_BYTES = 56 * 1024 * 1024
LANES = 128
HEAD_DIM = 64
BLOCK = 128
X_HEAD_DIM = 128
ROPE_THETA = 10000.0
EPS = 1e-6
NEG = -1e30
ADAM_LR, ADAM_B1, ADAM_B2, ADAM_EPS, ADAM_WD, ADAM_STEP = 0.001, 0.9, 0.999, 1e-08, 0.01, 10
N_CHIPS = 4
MESH = pl.DeviceIdType.MESH
ANY = pl.BlockSpec(memory_space=pl.ANY)


def _pick(dim, prefs):
    for p in prefs:
        if dim % p == 0:
            return p
    return dim


def _cparams(*sem):
    return pltpu.CompilerParams(dimension_semantics=sem, vmem_limit_bytes=VMEM_LIMIT_BYTES)


def _sds(shape, dtype):
    return jax.ShapeDtypeStruct(shape, dtype)


def _sigmoid(v):
    return 0.5 * jnp.tanh(0.5 * v) + 0.5


MATMUL_VMEM_BUDGET = 46 * 1024 * 1024


def _tiles(mode, m, n):
    if mode == "tn" and m % 1024 != 0:
        return _pick(m, (512, 256, 128)), _pick(n, (1024, 512, 256, 128)), True
    return _pick(m, (1024, 512, 256, 128)), _pick(n, (512, 256, 128)), False


def _k_parts(m, n, k):
    tm, tn = _pick(m, (1024, 512, 256, 128)), _pick(n, (512, 256, 128))
    for parts in range(1, k // LANES + 1):
        if k % (parts * LANES) == 0 and 4 * (tm + tn) * (k // parts) + 16 * tm * tn <= MATMUL_VMEM_BUDGET:
            return parts
    return k // LANES


def _matmul(a, b, *, mode, out_dtype, name, res=None, dep=None, cols=None):
    if mode == "nn":
        (m, k), (k2, n) = a.shape, b.shape
    elif mode == "nt":
        (m, k), (n, k2) = (a.shape[-2], a.shape[-1] * (a.shape[0] if a.ndim == 3 else 1)), b.shape
    else:
        (k, m), (k2, n) = a.shape, (b.shape[-2], b.shape[-1] * (b.shape[0] if b.ndim == 3 else 1))
    assert k == k2, (a.shape, b.shape, mode)
    assert cols is None or mode != "nt"
    cols = cols or (0, n)
    parts = 1 if mode == "tn" else (a.shape[0] if a.ndim == 3 else _k_parts(m, cols[1], k))
    for p in range(parts):
        last = p == parts - 1
        res = _matmul_slice(a, b, mode=mode, out_dtype=out_dtype if last else F32, res=res, dep=dep, cols=cols,
                            kslice=(p, parts), name=name + ("_k%d" % p if parts > 1 else ""))
    return res


def _matmul_slice(a, b, *, mode, out_dtype, name, res, dep, kslice, cols):
    part, parts = kslice
    (m, k) = a.shape[-2:] if mode != "tn" else a.shape[::-1]
    col0, n = cols
    tk = k if a.ndim == 3 else k // parts
    tm, tn, swap = _tiles(mode, m, b.shape[2] if (mode == "tn" and b.ndim == 3) else n)
    while col0 % tn:
        tn //= 2
    joff = col0 // tn
    dims = {"nn": (((1,), (0,)), ((), ())), "nt": (((1,), (1,)), ((), ())), "tn": (((0,), (0,)), ((), ()))}[mode]
    has_res = res is not None
    has_dep = dep is not None

    def body(*refs):
        a_ref, b_ref = refs[0], refs[1]
        o_ref = refs[2 + has_res + has_dep]
        val = lax.dot_general(a_ref[...], b_ref[...], dims, preferred_element_type=F32)
        if has_res:
            val = val + refs[2][...]
        o_ref[...] = val.astype(o_ref.dtype)

    def spec(shape, f):
        if swap:
            return pl.BlockSpec(shape, lambda j, i: f(i, j))
        return pl.BlockSpec(shape, f)

    if mode == "tn":
        a_spec = spec((tk, tm), lambda i, j: (part, i))
    elif a.ndim == 3:
        a_spec = spec((None, tm, tk), lambda i, j: (part, i, 0))
    else:
        a_spec = spec((tm, tk), lambda i, j: (i, part))
    if mode == "nt":
        b_spec = spec((tn, tk), lambda i, j: (j, part))
    elif b.ndim == 3:
        per = b.shape[2] // tn
        b_spec = spec((None, tk, tn), lambda i, j: (j // per, part, j % per))
    else:
        b_spec = spec((tk, tn), lambda i, j: (part, joff + j))
    o_spec = spec((tm, tn), lambda i, j: (i, j))
    return pl.pallas_call(
        body,
        name=name,
        grid=(n // tn, m // tm) if swap else (m // tm, n // tn),
        in_specs=[a_spec, b_spec] + ([o_spec] if has_res else []) + ([ANY] if has_dep else []),
        out_specs=o_spec,
        out_shape=_sds((m, n), out_dtype),
        compiler_params=_cparams("parallel", "parallel"),
    )(*([a, b] + ([res] if has_res else []) + ([dep] if has_dep else [])))


def _rms_fwd(x, g, name):
    t, d = x.shape
    tm = _pick(t, (512, 256, 128))

    def body(x_ref, g_ref, o_ref):
        xf = x_ref[...]
        r = lax.rsqrt(jnp.mean(xf * xf, axis=-1, keepdims=True) + EPS)
        o_ref[...] = (xf * r * g_ref[...]).astype(o_ref.dtype)

    row = pl.BlockSpec((tm, d), lambda i: (i, 0))
    return pl.pallas_call(
        body, name=name, grid=(t // tm,),
        in_specs=[row, pl.BlockSpec((1, d), lambda i: (0, 0))],
        out_specs=row, out_shape=_sds((t, d), BF16),
        compiler_params=_cparams("parallel"),
    )(x, g)


def _rms_bwd_math(xf, g, du):
    r = lax.rsqrt(jnp.mean(xf * xf, axis=-1, keepdims=True) + EPS)
    xh = xf * r
    gdy = g * du
    dx = r * (gdy - xh * jnp.mean(gdy * xh, axis=-1, keepdims=True))
    dg = jnp.sum(du * xh, axis=0, keepdims=True)
    return dx, dg


def _rms_bwd(x, g, du, dh, name):
    t, d = x.shape
    tm = _pick(t, (256, 128))
    has_dh = dh is not None

    def body(*refs):
        x_ref, g_ref, du_ref = refs[0], refs[1], refs[2]
        o_ref, ob_ref, dg_ref = refs[3 + has_dh:]
        dx, dg = _rms_bwd_math(x_ref[...], g_ref[...], du_ref[...].astype(F32))
        if has_dh:
            dx = dx + refs[3][...]
        o_ref[...] = dx
        ob_ref[...] = dx.astype(BF16)

        @pl.when(pl.program_id(0) == 0)
        def _():
            dg_ref[...] = dg

        @pl.when(pl.program_id(0) > 0)
        def _():
            dg_ref[...] += dg

    row = pl.BlockSpec((tm, d), lambda i: (i, 0))
    vec = pl.BlockSpec((1, d), lambda i: (0, 0))
    return pl.pallas_call(
        body, name=name, grid=(t // tm,),
        in_specs=[row, vec, row] + ([row] if has_dh else []),
        out_specs=[row, row, vec],
        out_shape=[_sds((t, d), F32), _sds((t, d), BF16), _sds((1, d), F32)],
        compiler_params=_cparams("arbitrary"),
    )(*([x, g, du] + ([dh] if has_dh else [])))


def _loss_head(h, g, tgt):
    t, d = h.shape
    tm = _pick(t, (256, 128))

    def body(h_ref, g_ref, t_ref, o_ref, ob_ref, dg_ref, l_ref):
        xf = h_ref[...]
        gv = g_ref[...]
        r = lax.rsqrt(jnp.mean(xf * xf, axis=-1, keepdims=True) + EPS)
        err = xf * r * gv - t_ref[...]
        part = 0.5 * jnp.sum(jnp.mean(err * err, axis=-1, keepdims=True), axis=0, keepdims=True)
        dx, dg = _rms_bwd_math(xf, gv, err * (1.0 / d))
        o_ref[...] = dx
        ob_ref[...] = dx.astype(BF16)
        lrow = jnp.broadcast_to(part, (1, LANES))

        @pl.when(pl.program_id(0) == 0)
        def _():
            dg_ref[...] = dg
            l_ref[...] = lrow

        @pl.when(pl.program_id(0) > 0)
        def _():
            dg_ref[...] += dg
            l_ref[...] += lrow

    row = pl.BlockSpec((tm, d), lambda i: (i, 0))
    vec = pl.BlockSpec((1, d), lambda i: (0, 0))
    return pl.pallas_call(
        body, name="loss_head", grid=(t // tm,),
        in_specs=[row, vec, row],
        out_specs=[row, row, vec, pl.BlockSpec((1, LANES), lambda i: (0, 0))],
        out_shape=[_sds((t, d), F32), _sds((t, d), BF16), _sds((1, d), F32), _sds((1, LANES), F32)],
        compiler_params=_cparams("arbitrary"),
    )(h, g, tgt)


ROW_TILE = 256


def _rows_matmul(a, b, *, mode, kslice, res, rows_in, vec, epilogue, outs, name, dep):
    part, parts = kslice
    t = a.shape[-2]
    n = b.shape[1] if mode == "nn" else b.shape[0]
    kc = a.shape[-1] if a.ndim == 3 else a.shape[1] // parts
    tm = _pick(t, (ROW_TILE, 128))
    dims = (((1,), (0,)), ((), ())) if mode == "nn" else (((1,), (1,)), ((), ()))
    has_res = res is not None
    deps = [] if dep is None else [dep]
    n_in = 2 + has_res + len(rows_in) + 1 + len(deps)

    def body(*refs):
        val = lax.dot_general(refs[0][...], refs[1][...], dims, preferred_element_type=F32)
        if has_res:
            val = val + refs[2][...]
        row_refs = refs[2 + has_res:2 + has_res + len(rows_in)]
        epilogue(val, row_refs, refs[2 + has_res + len(rows_in)], refs[n_in:], pl.program_id(0) == 0)

    if a.ndim == 3:
        a_spec = pl.BlockSpec((None, tm, kc), lambda i: (part, i, 0))
    else:
        a_spec = pl.BlockSpec((tm, kc), lambda i: (i, part))
    if mode == "nn":
        b_spec = pl.BlockSpec((kc, n), lambda i: (part, 0), pipeline_mode=pl.Buffered(1))
    else:
        b_spec = pl.BlockSpec((n, kc), lambda i: (0, part), pipeline_mode=pl.Buffered(1))
    row = pl.BlockSpec((tm, n), lambda i: (i, 0))
    kinds = {"row": row, "vec": pl.BlockSpec((1, n), lambda i: (0, 0)), "lane": pl.BlockSpec((1, LANES), lambda i: (0, 0))}
    shapes = {"row": (t, n), "vec": (1, n), "lane": (1, LANES)}
    return pl.pallas_call(
        body, name=name, grid=(t // tm,),
        in_specs=[a_spec, b_spec] + [row] * (has_res + len(rows_in)) + [kinds["vec"]] + [ANY] * len(deps),
        out_specs=[kinds[o[0]] for o in outs],
        out_shape=[_sds(shapes[o[0]], o[1] if len(o) > 1 else F32) for o in outs],
        compiler_params=_cparams("arbitrary"),
    )(*([a, b] + ([res] if has_res else []) + list(rows_in) + [vec] + deps))


def _accumulate(ref, value, first):
    @pl.when(first)
    def _():
        ref[...] = value

    @pl.when(jnp.logical_not(first))
    def _():
        ref[...] += value


def _ep_norm(val, rows, g_ref, outs, first):
    h_ref, u_ref = outs
    h_ref[...] = val
    r = lax.rsqrt(jnp.mean(val * val, axis=-1, keepdims=True) + EPS)
    u_ref[...] = (val * r * g_ref[...]).astype(u_ref.dtype)


def _ep_loss(val, rows, g_ref, outs, first):
    o_ref, ob_ref, dg_ref, l_ref = outs
    gv = g_ref[...]
    r = lax.rsqrt(jnp.mean(val * val, axis=-1, keepdims=True) + EPS)
    err = val * r * gv - rows[0][...]
    part = 0.5 * jnp.sum(jnp.mean(err * err, axis=-1, keepdims=True), axis=0, keepdims=True)
    dx, dg = _rms_bwd_math(val, gv, err * (1.0 / val.shape[1]))
    o_ref[...] = dx
    ob_ref[...] = dx.astype(BF16)
    _accumulate(dg_ref, dg, first)
    _accumulate(l_ref, jnp.broadcast_to(part, (1, LANES)), first)


def _ep_rms_bwd(val, rows, g_ref, outs, first):
    o_ref, ob_ref, dg_ref = outs
    dx, dg = _rms_bwd_math(rows[0][...], g_ref[...], val)
    dx = dx + rows[1][...]
    o_ref[...] = dx
    ob_ref[...] = dx.astype(BF16)
    _accumulate(dg_ref, dg, first)


def _rope_tables(t):
    half = HEAD_DIM // 2
    inv_freq = ROPE_THETA ** (-jnp.arange(half, dtype=F32) / half)
    ang = jnp.arange(t, dtype=F32)[:, None] * inv_freq[None, :]
    cos = jnp.cos(ang)
    sin = jnp.sin(ang)
    reps = LANES // HEAD_DIM
    cos_t = jnp.tile(jnp.concatenate([cos, cos], axis=1), (1, reps))
    sin_t = jnp.tile(jnp.concatenate([-sin, sin], axis=1), (1, reps))
    return cos_t, sin_t


def _rope(v, cos, sin):
    w = v.shape[1]
    c = jnp.tile(cos, (1, w // LANES))
    s = jnp.tile(sin, (1, w // LANES))
    lane = lax.broadcasted_iota(jnp.int32, v.shape, 1)
    first = (lane % HEAD_DIM) < (HEAD_DIM // 2)
    partner = jnp.where(first, pltpu.roll(v, w - HEAD_DIM // 2, 1), pltpu.roll(v, HEAD_DIM // 2, 1))
    return v * c + partner * s


def _heads(v, count):
    return jnp.concatenate([v[:, i * HEAD_DIM:(i + 1) * HEAD_DIM] for i in range(count)], axis=0)


def _unheads(v, count):
    r = v.shape[0] // count
    return jnp.concatenate([v[g * r:(g + 1) * r] for g in range(count)], axis=1)


def _rope_qkv(proj, cos, sin, aw, kvw):
    t = proj.shape[0]
    nkv = kvw // HEAD_DIM
    koff = aw // kvw
    tm = _pick(t, (256, 128))

    def body(q_ref, k_ref, v_ref, c_ref, s_ref, qo_ref, ko_ref, vo_ref):
        c, s = c_ref[...], s_ref[...]
        qo_ref[...] = _rope(q_ref[...].astype(F32), c, s).astype(BF16)
        k = _rope(k_ref[...].astype(F32), c, s).astype(BF16)
        v = v_ref[...].astype(BF16)
        for h in range(nkv):
            ko_ref[h] = k[:, h * HEAD_DIM:(h + 1) * HEAD_DIM]
            vo_ref[h] = v[:, h * HEAD_DIM:(h + 1) * HEAD_DIM]

    def row(w, j):
        return pl.BlockSpec((tm, w), lambda i: (i, j))

    hm = pl.BlockSpec((nkv, tm, HEAD_DIM), lambda i: (0, i, 0))
    return pl.pallas_call(
        body, name="rope_qkv", grid=(t // tm,),
        in_specs=[row(aw, 0), row(kvw, koff), row(kvw, koff + 1), row(LANES, 0), row(LANES, 0)],
        out_specs=[row(aw, 0), hm, hm],
        out_shape=[_sds((t, aw), BF16), _sds((nkv, t, HEAD_DIM), BF16), _sds((nkv, t, HEAD_DIM), BF16)],
        compiler_params=_cparams("parallel"),
    )(proj, proj, proj, cos, sin)


def _attn_probs(qs, kb, n, h, qpk, sinks_ref):
    s = lax.dot_general(qs, kb, (((1,), (1,)), ((), ())), preferred_element_type=F32) * (HEAD_DIM ** -0.5)
    qi = lax.broadcasted_iota(jnp.int32, (BLOCK, 2 * BLOCK), 0)
    kc = lax.broadcasted_iota(jnp.int32, (BLOCK, 2 * BLOCK), 1)
    valid = (kc > qi) & (kc <= qi + BLOCK) & ((kc >= BLOCK) | (n > 0))
    bias = jnp.tile(jnp.where(valid, 0.0, NEG).astype(F32), (qpk, 1))
    s = s + bias
    rowg = lax.broadcasted_iota(jnp.int32, (qpk * BLOCK, 1), 0) // BLOCK
    sink = jnp.zeros((qpk * BLOCK, 1), F32)
    for g in range(qpk):
        sink = jnp.where(rowg == g, sinks_ref[0, h * qpk + g], sink)
    m = jnp.maximum(jnp.max(s, axis=-1, keepdims=True), sink)
    e = jnp.exp(s - m)
    es = jnp.exp(sink - m)
    inv = 1.0 / (jnp.sum(e, axis=-1, keepdims=True) + es)
    return e * inv, es * inv, rowg


HEADS_PER_STEP = 4


def _attn_specs(qw, hp):
    def head(f):
        return pl.BlockSpec((hp, BLOCK, HEAD_DIM), lambda n, h: (h, f(n), 0))

    cur = lambda n: n
    prev = lambda n: jnp.maximum(n - 1, 0)
    return [pl.BlockSpec((BLOCK, hp * qw), lambda n, h: (n, h)), head(cur), head(prev), head(cur), head(prev),
            pl.BlockSpec(memory_space=pltpu.SMEM)]


def _attn_fwd(q_r, k_r, v_h, sinks):
    t, aw = q_r.shape
    nkv = k_r.shape[0]
    qpk = aw // (nkv * HEAD_DIM)
    qw = qpk * HEAD_DIM
    hp = HEADS_PER_STEP if nkv % HEADS_PER_STEP == 0 else 1

    def body(q_ref, kc_ref, kp_ref, vc_ref, vp_ref, sinks_ref, o_ref):
        n, hg = pl.program_id(0), pl.program_id(1)
        outs = []
        for j in range(hp):
            kb = jnp.concatenate([kp_ref[j], kc_ref[j]], axis=0)
            vb = jnp.concatenate([vp_ref[j], vc_ref[j]], axis=0)
            qs = _heads(q_ref[:, j * qw:(j + 1) * qw], qpk)
            p, _, _ = _attn_probs(qs, kb, n, hg * hp + j, qpk, sinks_ref)
            outs.append(_unheads(jnp.dot(p.astype(BF16), vb, preferred_element_type=F32), qpk))
        o_ref[...] = jnp.concatenate(outs, axis=1).astype(o_ref.dtype)

    return pl.pallas_call(
        body, name="attn_fwd", grid=(t // BLOCK, nkv // hp),
        in_specs=_attn_specs(qw, hp),
        out_specs=pl.BlockSpec((BLOCK, hp * qw), lambda n, h: (n, h)),
        out_shape=_sds((t, aw), BF16),
        compiler_params=_cparams("parallel", "parallel"),
    )(q_r, k_r, k_r, v_h, v_h, sinks)


def _attn_bwd(q_r, k_r, v_h, sinks, o, do):
    t, aw = q_r.shape
    nkv = k_r.shape[0]
    qpk = aw // (nkv * HEAD_DIM)
    qw = qpk * HEAD_DIM
    hp = HEADS_PER_STEP if nkv % HEADS_PER_STEP == 0 else 1
    scale = HEAD_DIM ** -0.5

    def body(q_ref, kc_ref, kp_ref, vc_ref, vp_ref, sinks_ref, o_ref, do_ref,
             dq_ref, dkc_ref, dkp_ref, dvc_ref, dvp_ref, ds_ref):
        n, hg = pl.program_id(0), pl.program_id(1)
        lane = lax.broadcasted_iota(jnp.int32, (8, LANES), 1)
        row0 = lax.broadcasted_iota(jnp.int32, (8, LANES), 0) == 0
        dsink = jnp.zeros((8, LANES), F32)
        dqs = []
        for j in range(hp):
            h = hg * hp + j
            cols = slice(j * qw, (j + 1) * qw)
            kb = jnp.concatenate([kp_ref[j], kc_ref[j]], axis=0)
            vb = jnp.concatenate([vp_ref[j], vc_ref[j]], axis=0)
            qs = _heads(q_ref[:, cols], qpk)
            dos = _heads(do_ref[:, cols], qpk)
            p, psink, rowg = _attn_probs(qs, kb, n, h, qpk, sinks_ref)
            pb = p.astype(BF16)
            delta = jnp.sum(dos.astype(F32) * _heads(o_ref[:, cols], qpk).astype(F32), axis=-1, keepdims=True)
            dv = lax.dot_general(pb, dos, (((0,), (0,)), ((), ())), preferred_element_type=F32)
            dp = lax.dot_general(dos, vb, (((1,), (1,)), ((), ())), preferred_element_type=F32)
            dsc = (p * (dp - delta)).astype(BF16)
            dqs.append(_unheads(jnp.dot(dsc, kb, preferred_element_type=F32) * scale, qpk))
            dk = lax.dot_general(dsc, qs, (((0,), (0,)), ((), ())), preferred_element_type=F32) * scale
            dkp_ref[j] = dk[:BLOCK]
            dkc_ref[j] = dk[BLOCK:]
            dvp_ref[j] = dv[:BLOCK]
            dvc_ref[j] = dv[BLOCK:]
            sink_term = psink * delta
            for g in range(qpk):
                val = -jnp.sum(jnp.where(rowg == g, sink_term, 0.0))
                dsink = jnp.where(row0 & (lane == h * qpk + g), val, dsink)
        dq_ref[...] = jnp.concatenate(dqs, axis=1).astype(dq_ref.dtype)
        first = (n == 0) & (hg == 0)

        @pl.when(first)
        def _():
            ds_ref[...] = dsink

        @pl.when(jnp.logical_not(first))
        def _():
            ds_ref[...] += dsink

    qblk = pl.BlockSpec((BLOCK, hp * qw), lambda n, h: (n, h))
    kvblk = pl.BlockSpec((hp, BLOCK, HEAD_DIM), lambda n, h: (h, n, 0))
    return pl.pallas_call(
        body, name="attn_bwd", grid=(t // BLOCK, nkv // hp),
        in_specs=_attn_specs(qw, hp) + [qblk, qblk],
        out_specs=[qblk, kvblk, kvblk, kvblk, kvblk, pl.BlockSpec((8, LANES), lambda n, h: (0, 0))],
        out_shape=[_sds((t, aw), BF16)] + [_sds((nkv, t, HEAD_DIM), F32)] * 4 + [_sds((8, LANES), F32)],
        compiler_params=_cparams("arbitrary", "arbitrary"),
    )(q_r, k_r, k_r, v_h, v_h, sinks, o, do)


HALO = 16


def _shift_down(v, k, halo):
    rows = lax.broadcasted_iota(jnp.int32, v.shape, 0)
    out = pltpu.roll(v, k, 0)
    for r in range(k):
        out = jnp.where(rows == r, halo[HALO - k + r:HALO - k + r + 1, :], out)
    return out


def _shift_up(v, k, halo):
    tm = v.shape[0]
    rows = lax.broadcasted_iota(jnp.int32, v.shape, 0)
    out = pltpu.roll(v, tm - k, 0)
    for r in range(k):
        out = jnp.where(rows == tm - k + r, halo[r:r + 1, :], out)
    return out


def _conv_fwd(proj, conv_w, zoff, cw, cb):
    t = proj.shape[0]
    tm = _pick(t, (512, 256, 128))
    zb, nb = zoff // cb, cw // cb
    hb = tm // HALO

    def body(z_ref, gb_ref, gc_ref, zp_ref, gcp_ref, w_ref, o_ref):
        i = pl.program_id(0)
        cz = gc_ref[...].astype(F32) * z_ref[...].astype(F32)
        czp = gcp_ref[...].astype(F32) * zp_ref[...].astype(F32) * (i > 0).astype(F32)
        w = w_ref[...]
        y = w[0:1] * _shift_down(cz, 2, czp) + w[1:2] * _shift_down(cz, 1, czp) + w[2:3] * cz
        o_ref[...] = (gb_ref[...].astype(F32) * y).astype(o_ref.dtype)

    def col(k):
        return pl.BlockSpec((tm, cb), lambda i, j: (i, zb + k * nb + j))

    def halo(k):
        return pl.BlockSpec((HALO, cb), lambda i, j: (jnp.maximum(i * hb - 1, 0), zb + k * nb + j))

    return pl.pallas_call(
        body, name="conv_fwd", grid=(t // tm, nb),
        in_specs=[col(0), col(1), col(2), halo(0), halo(2), pl.BlockSpec((3, cb), lambda i, j: (0, j))],
        out_specs=pl.BlockSpec((tm, cb), lambda i, j: (i, j)),
        out_shape=_sds((t, cw), BF16),
        compiler_params=_cparams("parallel", "parallel"),
    )(proj, proj, proj, proj, proj, conv_w)


def _conv_bwd(proj, conv_w, dco, zoff, cw, cb):
    t = proj.shape[0]
    tm = _pick(t, (512, 256, 128))
    zb, nb = zoff // cb, cw // cb
    hb = tm // HALO
    nt = t // tm

    def body(z_ref, gb_ref, gc_ref, zp_ref, gcp_ref, gbn_ref, w_ref, d_ref, dn_ref, dz_ref, dgb_ref, dgc_ref, dw_ref):
        i = pl.program_id(1)
        z, gb, gc = z_ref[...].astype(F32), gb_ref[...].astype(F32), gc_ref[...].astype(F32)
        d = d_ref[...].astype(F32)
        cz = gc * z
        czp = gcp_ref[...].astype(F32) * zp_ref[...].astype(F32) * (i > 0).astype(F32)
        w = w_ref[...]
        cz1 = _shift_down(cz, 1, czp)
        cz2 = _shift_down(cz, 2, czp)
        y = w[0:1] * cz2 + w[1:2] * cz1 + w[2:3] * cz
        dgb_ref[...] = (d * y).astype(dgb_ref.dtype)
        dy = d * gb
        dyn = dn_ref[...].astype(F32) * gbn_ref[...].astype(F32) * (i < nt - 1).astype(F32)
        dcz = w[2:3] * dy + w[1:2] * _shift_up(dy, 1, dyn) + w[0:1] * _shift_up(dy, 2, dyn)
        dgc_ref[...] = (dcz * z).astype(dgc_ref.dtype)
        dz_ref[...] = (dcz * gc).astype(dz_ref.dtype)
        rows = lax.broadcasted_iota(jnp.int32, (8, cb), 0)
        dw = jnp.zeros((8, cb), F32)
        for r, tap in enumerate((cz2, cz1, cz)):
            dw = jnp.where(rows == r, jnp.sum(dy * tap, axis=0, keepdims=True), dw)

        @pl.when(i == 0)
        def _():
            dw_ref[...] = dw

        @pl.when(i > 0)
        def _():
            dw_ref[...] += dw

    def col(k):
        return pl.BlockSpec((tm, cb), lambda j, i: (i, zb + k * nb + j))

    def halo_prev(k):
        return pl.BlockSpec((HALO, cb), lambda j, i: (jnp.maximum(i * hb - 1, 0), zb + k * nb + j))

    own = pl.BlockSpec((tm, cb), lambda j, i: (i, j))
    nxt = lambda i: jnp.minimum((i + 1) * hb, t // HALO - 1)
    return pl.pallas_call(
        body, name="conv_bwd", grid=(nb, nt),
        in_specs=[col(0), col(1), col(2), halo_prev(0), halo_prev(2),
                  pl.BlockSpec((HALO, cb), lambda j, i: (nxt(i), zb + nb + j)),
                  pl.BlockSpec((3, cb), lambda j, i: (0, j)), own,
                  pl.BlockSpec((HALO, cb), lambda j, i: (nxt(i), j))],
        out_specs=[own, own, own, pl.BlockSpec((8, cb), lambda j, i: (0, j))],
        out_shape=[_sds((t, cw), BF16)] * 3 + [_sds((8, cw), F32)],
        compiler_params=_cparams("parallel", "arbitrary"),
    )(proj, proj, proj, proj, proj, proj, conv_w, dco, dco)


def _yconv_merge(conv_o, w_cp, ya, proj, goff, cb, dep, name):
    t, k = conv_o.shape
    d = w_cp.shape[1]
    tm = _pick(t, (1024, 512, 256, 128))
    gb_, nb = goff // cb, d // cb
    deps = [] if dep is None else [dep]

    def body(a_ref, w_ref, ya_ref, ga_ref, gc_ref, *rest):
        yc_ref, o_ref = rest[len(deps):]
        yc = jnp.dot(a_ref[...], w_ref[...], preferred_element_type=F32)
        f = lambda r: r[...].astype(F32)
        yc_ref[...] = yc.astype(yc_ref.dtype)
        o_ref[...] = (_sigmoid(f(ga_ref)) * f(ya_ref) + _sigmoid(f(gc_ref)) * yc).astype(o_ref.dtype)

    own = pl.BlockSpec((tm, cb), lambda i, j: (i, j))
    return pl.pallas_call(
        body, name=name, grid=(t // tm, nb),
        in_specs=[pl.BlockSpec((tm, k), lambda i, j: (i, 0)), pl.BlockSpec((k, cb), lambda i, j: (0, j)), own,
                  pl.BlockSpec((tm, cb), lambda i, j: (i, gb_ + j)),
                  pl.BlockSpec((tm, cb), lambda i, j: (i, gb_ + nb + j))] + [ANY] * len(deps),
        out_specs=[own, own], out_shape=[_sds((t, d), BF16)] * 2,
        compiler_params=_cparams("parallel", "parallel"),
    )(conv_o, w_cp, ya, proj, proj, *deps)


def _dmerged_split(dh, w_mo, proj, ya, yc, goff, cb, dep, name):
    t, k = dh.shape
    d = w_mo.shape[0]
    tm = _pick(t, (1024, 512, 256, 128))
    gb_, nb = goff // cb, d // cb
    deps = [] if dep is None else [dep]

    def body(a_ref, w_ref, ga_ref, gc_ref, ya_ref, yc_ref, *rest):
        dya_ref, dyc_ref, dga_ref, dgc_ref = rest[len(deps):]
        dmv = lax.dot_general(a_ref[...], w_ref[...], (((1,), (1,)), ((), ())), preferred_element_type=F32)
        sa = _sigmoid(ga_ref[...].astype(F32))
        sc = _sigmoid(gc_ref[...].astype(F32))
        dya_ref[...] = (dmv * sa).astype(BF16)
        dyc_ref[...] = (dmv * sc).astype(BF16)
        dga_ref[...] = (dmv * ya_ref[...].astype(F32) * sa * (1.0 - sa)).astype(BF16)
        dgc_ref[...] = (dmv * yc_ref[...].astype(F32) * sc * (1.0 - sc)).astype(BF16)

    own = pl.BlockSpec((tm, cb), lambda i, j: (i, j))
    return pl.pallas_call(
        body, name=name, grid=(t // tm, nb),
        in_specs=[pl.BlockSpec((tm, k), lambda i, j: (i, 0)), pl.BlockSpec((cb, k), lambda i, j: (j, 0)),
                  pl.BlockSpec((tm, cb), lambda i, j: (i, gb_ + j)),
                  pl.BlockSpec((tm, cb), lambda i, j: (i, gb_ + nb + j)), own, own] + [ANY] * len(deps),
        out_specs=[own] * 4, out_shape=[_sds((t, d), BF16)] * 4,
        compiler_params=_cparams("parallel", "parallel"),
    )(dh, w_mo, proj, proj, ya, yc, *deps)


def _assemble_dproj(dq, dkc, dkp, dvc, dvp, cos, sin, dz, dgb, dgc, dga, dgg):
    t, aw = dq.shape
    nkv, cw, d = dkc.shape[0], dz.shape[1], dga.shape[1]
    kvw = nkv * HEAD_DIM
    nblk = t // BLOCK
    width = aw + 2 * kvw + 3 * cw + 2 * d

    def body(dq_ref, dkc_ref, dkp_ref, dvc_ref, dvp_ref, c_ref, s_ref, dz_ref, dgb_ref, dgc_ref, dga_ref, dgg_ref,
             o_ref):
        keep = (pl.program_id(0) < nblk - 1).astype(F32)
        c, s = c_ref[...], s_ref[...]
        dk = jnp.concatenate([dkc_ref[h] + dkp_ref[h] * keep for h in range(nkv)], axis=1)
        dv = jnp.concatenate([dvc_ref[h] + dvp_ref[h] * keep for h in range(nkv)], axis=1)
        o_ref[...] = jnp.concatenate(
            [_rope(dq_ref[...].astype(F32), c, -s).astype(BF16), _rope(dk, c, -s).astype(BF16), dv.astype(BF16),
             dz_ref[...], dgb_ref[...], dgc_ref[...], dga_ref[...], dgg_ref[...]], axis=1)

    def cur(w):
        return pl.BlockSpec((BLOCK, w), lambda n: (n, 0))

    head_cur = pl.BlockSpec((nkv, BLOCK, HEAD_DIM), lambda n: (0, n, 0))
    head_nxt = pl.BlockSpec((nkv, BLOCK, HEAD_DIM), lambda n: (0, jnp.minimum(n + 1, nblk - 1), 0))
    return pl.pallas_call(
        body, name="assemble_dproj", grid=(nblk,),
        in_specs=[cur(aw), head_cur, head_nxt, head_cur, head_nxt, cur(LANES), cur(LANES),
                  cur(cw), cur(cw), cur(cw), cur(d), cur(d)],
        out_specs=cur(width), out_shape=_sds((t, width), BF16),
        compiler_params=_cparams("parallel"),
    )(dq, dkc, dkp, dvc, dvp, cos, sin, dz, dgb, dgc, dga, dgg)


def _xattn_probs(qh, kh):
    s = lax.dot_general(qh, kh, (((1,), (1,)), ((), ())), preferred_element_type=F32) * (X_HEAD_DIM ** -0.5)
    e = jnp.exp(s - jnp.max(s, axis=-1, keepdims=True))
    return e * (1.0 / jnp.sum(e, axis=-1, keepdims=True))


def _xattn_fwd(xq, kv):
    t, xw = xq.shape
    mt = kv.shape[0]
    tm = _pick(t, (512, 256, 128))

    def body(q_ref, kv_ref, o_ref):
        outs = []
        for hd in range(xw // X_HEAD_DIM):
            hs = slice(hd * X_HEAD_DIM, (hd + 1) * X_HEAD_DIM)
            vs = slice(xw + hd * X_HEAD_DIM, xw + (hd + 1) * X_HEAD_DIM)
            p = _xattn_probs(q_ref[:, hs], kv_ref[:, hs])
            outs.append(jnp.dot(p.astype(BF16), kv_ref[:, vs], preferred_element_type=F32))
        o_ref[...] = jnp.concatenate(outs, axis=1).astype(o_ref.dtype)

    return pl.pallas_call(
        body, name="xattn_fwd", grid=(t // tm,),
        in_specs=[pl.BlockSpec((tm, xw), lambda i: (i, 0)), pl.BlockSpec((mt, 2 * xw), lambda i: (0, 0))],
        out_specs=pl.BlockSpec((tm, xw), lambda i: (i, 0)), out_shape=_sds((t, xw), BF16),
        compiler_params=_cparams("parallel"),
    )(xq, kv)


def _xattn_bwd(xq, kv, do):
    t, xw = xq.shape
    mt = kv.shape[0]
    tm = _pick(t, (512, 256, 128))
    scale = X_HEAD_DIM ** -0.5

    def body(q_ref, kv_ref, do_ref, dq_ref, dkv_ref):
        dqs, dks, dvs = [], [], []
        for hd in range(xw // X_HEAD_DIM):
            hs = slice(hd * X_HEAD_DIM, (hd + 1) * X_HEAD_DIM)
            vs = slice(xw + hd * X_HEAD_DIM, xw + (hd + 1) * X_HEAD_DIM)
            qh, kh, vh, doh = q_ref[:, hs], kv_ref[:, hs], kv_ref[:, vs], do_ref[:, hs]
            p = _xattn_probs(qh, kh)
            pb = p.astype(BF16)
            o = jnp.dot(pb, vh, preferred_element_type=F32)
            delta = jnp.sum(doh.astype(F32) * o, axis=-1, keepdims=True)
            dvs.append(lax.dot_general(pb, doh, (((0,), (0,)), ((), ())), preferred_element_type=F32))
            dp = lax.dot_general(doh, vh, (((1,), (1,)), ((), ())), preferred_element_type=F32)
            dsc = (p * (dp - delta)).astype(BF16)
            dqs.append(jnp.dot(dsc, kh, preferred_element_type=F32) * scale)
            dks.append(lax.dot_general(dsc, qh, (((0,), (0,)), ((), ())), preferred_element_type=F32) * scale)
        dq_ref[...] = jnp.concatenate(dqs, axis=1).astype(dq_ref.dtype)
        dkv = jnp.concatenate(dks + dvs, axis=1)

        @pl.when(pl.program_id(0) == 0)
        def _():
            dkv_ref[...] = dkv

        @pl.when(pl.program_id(0) > 0)
        def _():
            dkv_ref[...] += dkv

    row = pl.BlockSpec((tm, xw), lambda i: (i, 0))
    whole = pl.BlockSpec((mt, 2 * xw), lambda i: (0, 0))
    return pl.pallas_call(
        body, name="xattn_bwd", grid=(t // tm,),
        in_specs=[row, whole, row], out_specs=[row, whole],
        out_shape=[_sds((t, xw), BF16), _sds((mt, 2 * xw), F32)],
        compiler_params=_cparams("arbitrary"),
    )(xq, kv, do)


def _ffn_in_swiglu(u, w, dep, name):
    t, d = u.shape
    f = w.shape[1] // 2
    tm, tn, _ = _tiles("nn", t, f)
    nf = f // tn
    deps = [] if dep is None else [dep]

    def body(u_ref, wa_ref, wb_ref, *rest):
        h_ref, o_ref = rest[len(deps):]
        a = jnp.dot(u_ref[...], wa_ref[...], preferred_element_type=F32)
        b = jnp.dot(u_ref[...], wb_ref[...], preferred_element_type=F32)
        h_ref[0] = a.astype(h_ref.dtype)
        h_ref[1] = b.astype(h_ref.dtype)
        o_ref[...] = (a * _sigmoid(a) * b).astype(o_ref.dtype)

    return pl.pallas_call(
        body, name=name, grid=(t // tm, nf),
        in_specs=[pl.BlockSpec((tm, d), lambda i, j: (i, 0)), pl.BlockSpec((d, tn), lambda i, j: (0, j)),
                  pl.BlockSpec((d, tn), lambda i, j: (0, nf + j))] + [ANY] * len(deps),
        out_specs=[pl.BlockSpec((2, tm, tn), lambda i, j: (0, i, j)), pl.BlockSpec((tm, tn), lambda i, j: (i, j))],
        out_shape=[_sds((2, t, f), BF16), _sds((t, f), BF16)],
        compiler_params=_cparams("parallel", "parallel"),
    )(u, w, w, *deps)


def _dact_swiglu(dh, w_out, hid, dep, name):
    t, d = dh.shape
    f = w_out.shape[0]
    tm, tn, _ = _tiles("nt", t, f)
    deps = [] if dep is None else [dep]

    def body(dh_ref, w_ref, h_ref, *rest):
        o_ref = rest[len(deps)]
        g = lax.dot_general(dh_ref[...], w_ref[...], (((1,), (1,)), ((), ())), preferred_element_type=F32)
        a = h_ref[0].astype(F32)
        b = h_ref[1].astype(F32)
        sg = _sigmoid(a)
        o_ref[0] = (g * b * sg * (1.0 + a * (1.0 - sg))).astype(o_ref.dtype)
        o_ref[1] = (g * a * sg).astype(o_ref.dtype)

    pair = pl.BlockSpec((2, tm, tn), lambda i, j: (0, i, j))
    return pl.pallas_call(
        body, name=name, grid=(t // tm, f // tn),
        in_specs=[pl.BlockSpec((tm, d), lambda i, j: (i, 0)), pl.BlockSpec((tn, d), lambda i, j: (j, 0)), pair]
        + [ANY] * len(deps),
        out_specs=pair, out_shape=_sds((2, t, f), BF16),
        compiler_params=_cparams("parallel", "parallel"),
    )(dh, w_out, hid, *deps)


def _adamw(w, g, m, v, name):
    r, c = w.shape
    tr = _pick(r, (256, 128, 64, 32, 16, 8)) if r * c > 65536 else r

    def body(w_ref, g_ref, m_ref, v_ref, d_ref, nm_ref, nv_ref, go_ref):
        gv = g_ref[...]
        go_ref[...] = gv
        m2 = ADAM_B1 * m_ref[...] + (1.0 - ADAM_B1) * gv
        v2 = ADAM_B2 * v_ref[...] + (1.0 - ADAM_B2) * (gv * gv)
        m_hat = m2 / (1.0 - ADAM_B1 ** ADAM_STEP)
        v_hat = v2 / (1.0 - ADAM_B2 ** ADAM_STEP)
        d_ref[...] = -ADAM_LR * (m_hat / (jnp.sqrt(v_hat) + ADAM_EPS) + ADAM_WD * w_ref[...])
        nm_ref[...] = m2
        nv_ref[...] = v2

    blk = pl.BlockSpec((tr, c), lambda i: (i, 0))
    return pl.pallas_call(
        body, name=name, grid=(r // tr,),
        in_specs=[blk] * 4, out_specs=[blk] * 4, out_shape=[_sds((r, c), F32)] * 4,
        compiler_params=_cparams("parallel"),
    )(w, g, m, v)


class _Weights:
    def __init__(self, full):
        self.full = full
        self.grads = {}

    def get(self, name):
        return self.full[name]

    def mark(self, tag, value):
        return value

    def grad(self, name, g):
        self.grads[name] = g

    def dep(self):
        return None


def _local_step(x, mem, tgt, g_mix, sinks, g_xattn, g_mem, g_ffn, g_final, dims, wts):
    t, d = x.shape
    aw, cw, kvw = dims
    cb = 2 * kvw
    zoff = aw + 2 * kvw
    goff = zoff + 3 * cw
    cos, sin = _rope_tables(t)
    mark, get = wts.mark, wts.get

    def mm(a, b, **kw):
        return _matmul(a, b, dep=wts.dep(), **kw)

    u1 = mark("u1", _rms_fwd(x, g_mix, "rms_mix"))
    qkv = mm(u1, get("w_in"), mode="nn", out_dtype=F32, name="mm_qkv", cols=(0, zoff))
    proj = mark("proj", mm(u1, get("w_in"), mode="nn", out_dtype=BF16, name="mm_gates", cols=(zoff, goff - zoff + 2 * d)))
    zoff, goff = 0, goff - zoff
    q_r, k_r, v_h = _rope_qkv(qkv, cos, sin, aw, kvw)
    attn_o = mark("attn_o", _attn_fwd(q_r, k_r, v_h, sinks))
    conv_o = mark("conv_o", _conv_fwd(proj, get("conv_w"), zoff, cw, cb))
    ya = mark("ya", mm(attn_o, get("w_attn_proj"), mode="nn", out_dtype=BF16, name="mm_yattn"))
    yc, merged = _yconv_merge(conv_o, get("w_conv_proj"), ya, proj, goff, cb, wts.dep(), "mm_yconv")
    mark("merged", merged)
    norm_outs = [("row", F32), ("row", BF16)]
    bwd_outs = [("row", F32), ("row", BF16), ("vec",)]
    h1, u2 = _rows_matmul(merged, get("w_mix_out"), mode="nn", kslice=(0, 1), res=x, rows_in=[], vec=g_xattn,
                          epilogue=_ep_norm, outs=norm_outs, name="mm_mix", dep=wts.dep())
    mark("h1", h1)
    mem_n = _rms_fwd(mem, g_mem, "rms_mem")
    xq = mark("xq", mm(u2, get("w_xq"), mode="nn", out_dtype=BF16, name="mm_xq"))
    kv = mm(mem_n, get("w_xkv"), mode="nn", out_dtype=BF16, name="mm_xkv")
    xo = mark("xo", _xattn_fwd(xq, kv))
    h2, u3 = _rows_matmul(xo, get("w_xo"), mode="nn", kslice=(0, 1), res=h1, rows_in=[], vec=g_ffn,
                          epilogue=_ep_norm, outs=norm_outs, name="mm_xo", dep=wts.dep())
    mark("h2", h2)
    mark("u3", u3)
    hid, act = _ffn_in_swiglu(u3, get("w_ffn_in"), wts.dep(), "mm_ffn_in")
    mark("hid", hid)
    dh3, dh3b, dg_final, loss = _rows_matmul(
        act, get("w_ffn_out"), mode="nn", kslice=(0, 1), res=h2, rows_in=[tgt], vec=g_final, epilogue=_ep_loss,
        outs=bwd_outs + [("lane",)], name="mm_ffn_out", dep=wts.dep())
    mark("dh3", dh3b)

    wts.grad("w_ffn_out", mm(act, dh3b, mode="tn", out_dtype=BF16, name="mm_dw_ffn_out"))
    dhid = mark("dhid", _dact_swiglu(dh3b, get("w_ffn_out"), hid, wts.dep(), "mm_dact"))
    wts.grad("w_ffn_in", mm(u3, dhid, mode="tn", out_dtype=BF16, name="mm_dw_ffn_in"))
    du3 = mark("du3", _matmul_slice(dhid, get("w_ffn_in"), mode="nt", out_dtype=F32, name="mm_du3_k0", res=None,
                                    dep=wts.dep(), kslice=(0, 2), cols=(0, d)))
    dh2, dh2b, dg_ffn = _rows_matmul(dhid, get("w_ffn_in"), mode="nt", kslice=(1, 2), res=du3, rows_in=[h2, dh3],
                                     vec=g_ffn, epilogue=_ep_rms_bwd, outs=bwd_outs, name="mm_du3_k1", dep=wts.dep())
    mark("dh2", dh2b)
    wts.grad("w_xo", mm(xo, dh2b, mode="tn", out_dtype=BF16, name="mm_dw_xo"))
    dxo = mm(dh2b, get("w_xo"), mode="nt", out_dtype=BF16, name="mm_dxo")
    dxq, dkv = _xattn_bwd(xq, kv, dxo)
    dkvb = dkv.astype(BF16)
    wts.grad("w_xq", mm(u2, dxq, mode="tn", out_dtype=BF16, name="mm_dw_xq"))
    wts.grad("w_xkv", mm(mem_n, dkvb, mode="tn", out_dtype=BF16, name="mm_dw_xkv"))
    dmem_n = mm(dkvb, get("w_xkv"), mode="nt", out_dtype=F32, name="mm_dmem")
    _, _, dg_mem = _rms_bwd(mem, g_mem, dmem_n, None, "rms_bwd_mem")
    dh1, dh1b, dg_xattn = _rows_matmul(dxq, get("w_xq"), mode="nt", kslice=(0, 1), res=None, rows_in=[h1, dh2],
                                       vec=g_xattn, epilogue=_ep_rms_bwd, outs=bwd_outs, name="mm_du2", dep=wts.dep())
    mark("dh1", dh1b)
    wts.grad("w_mix_out", mm(merged, dh1b, mode="tn", out_dtype=BF16, name="mm_dw_mix"))
    dya, dyc, dga, dgg = _dmerged_split(dh1b, get("w_mix_out"), proj, ya, yc, goff, cb, wts.dep(), "mm_dmerged")
    mark("dya", dya)
    wts.grad("w_attn_proj", mm(attn_o, dya, mode="tn", out_dtype=BF16, name="mm_dw_attn_proj"))
    dattn_o = mm(dya, get("w_attn_proj"), mode="nt", out_dtype=BF16, name="mm_dattn")
    wts.grad("w_conv_proj", mm(conv_o, dyc, mode="tn", out_dtype=BF16, name="mm_dw_conv_proj"))
    dconv_o = mark("dconv_o", mm(dyc, get("w_conv_proj"), mode="nt", out_dtype=BF16, name="mm_dconv"))
    dz, dgb, dgc, dconv_w = _conv_bwd(proj, get("conv_w"), dconv_o, zoff, cw, cb)
    mark("dz", dz)
    dq, dkc, dkp, dvc, dvp, dsinks = _attn_bwd(q_r, k_r, v_h, sinks, attn_o, dattn_o)
    mark("dq", dq)
    dproj = mark("dproj", _assemble_dproj(dq, dkc, dkp, dvc, dvp, cos, sin, dz, dgb, dgc, dga, dgg))
    wts.grad("w_in", mm(u1, dproj, mode="tn", out_dtype=BF16, name="mm_dw_in"))
    du1 = mark("du1", _matmul_slice(dproj, get("w_in"), mode="nt", out_dtype=F32, name="mm_du1_k0", res=None,
                                    dep=wts.dep(), kslice=(0, 2), cols=(0, d)))
    grad_x, _, dg_mix = _rows_matmul(dproj, get("w_in"), mode="nt", kslice=(1, 2), res=du1, rows_in=[x, dh1],
                                     vec=g_mix, epilogue=_ep_rms_bwd, outs=bwd_outs, name="mm_du1_k1", dep=wts.dep())
    mark("grad_x", grad_x)

    small = dict(g_mix=dg_mix, g_xattn=dg_xattn, g_mem=dg_mem, g_ffn=dg_ffn, g_final=dg_final,
                 conv_w=dconv_w[:3], attn_sinks=dsinks[0:1, :sinks.shape[1]], loss=loss[0:1, 0:1])
    return grad_x, small


BIG = (("w_in", 1), ("w_attn_proj", 1), ("w_conv_proj", 1), ("w_mix_out", 0), ("w_xq", 0), ("w_xkv", 0),
       ("w_xo", 1), ("w_ffn_in", 1), ("w_ffn_out", 0))


def _place():
    x, y, c = lax.axis_index("x"), lax.axis_index("y"), lax.axis_index("c")
    chips = [(1 - x, y), (x, 1 - y), (1 - x, 1 - y)]
    return x, y, c, chips


def _window(ref, ax, shard_shape, s, h):
    sr, sc = shard_shape
    hr = sr // 2
    if ax == 1:
        return ref.at[pl.ds(pl.multiple_of(h * hr, 16), hr), pl.ds(pl.multiple_of(s * sc, LANES), sc)]
    return ref.at[pl.ds(pl.multiple_of(s * sr + h * hr, 16), hr), :]


def _half(ref, h):
    hr = ref.shape[0] // 2
    return ref.at[pl.ds(pl.multiple_of(h * hr, 16), hr), :]


def _remote(src, dst, send_sem, recv_sem, dev):
    return pltpu.make_async_remote_copy(src_ref=src, dst_ref=dst, send_sem=send_sem, recv_sem=recv_sem,
                                        device_id=dev, device_id_type=MESH)


def _cast_to_full(shard, ax, me, name, dtype=BF16):
    sr, sc = shard.shape
    tr = _pick(sr, (256, 352, 128, 64, 32, 16))
    nr = sr // tr
    full = (sr * N_CHIPS, sc) if ax == 0 else (sr, sc * N_CHIPS)

    def body(me_ref, s_ref, o_ref):
        o_ref[...] = s_ref[...].astype(o_ref.dtype)

    if ax == 1:
        out_spec = pl.BlockSpec((tr, sc), lambda r, me_ref: (r, me_ref[0]))
    else:
        out_spec = pl.BlockSpec((tr, sc), lambda r, me_ref: (me_ref[0] * nr + r, 0))
    return pl.pallas_call(
        body, name=name,
        grid_spec=pltpu.PrefetchScalarGridSpec(
            num_scalar_prefetch=1, grid=(nr,), in_specs=[pl.BlockSpec((tr, sc), lambda r, me_ref: (r, 0))],
            out_specs=out_spec),
        out_shape=_sds(full, dtype),
        compiler_params=_cparams("parallel"),
    )(me, shard)


HBM = pl.BlockSpec(memory_space=pltpu.HBM)
SEM = pl.BlockSpec(memory_space=pltpu.SEMAPHORE)
EFFECT = pltpu.SideEffectType.DATAFLOW_SIDE_EFFECTING


def _in_hbm(a):
    return pltpu.with_memory_space_constraint(a, pltpu.HBM)


def _gather_window(ref, ax, shard_shape, s, h):
    if h is not None:
        return _window(ref, ax, shard_shape, s, h)
    sr, sc = shard_shape
    if ax == 1:
        return ref.at[:, pl.ds(pl.multiple_of(s * sc, LANES), sc)]
    return ref.at[pl.ds(pl.multiple_of(s * sr, 8), sr), :]


def _ag_start(fulls, axes, shard_shapes, whole):
    n = len(fulls)

    def body(*refs):
        src = refs[:n]
        send_sems, recv_sems = refs[n], refs[n + 1]
        token = refs[2 * n + 2]
        x, y, c, chips = _place()
        me = 2 * x + y
        for i in range(n):
            h = None if whole[i] else c
            for j, chip in enumerate(chips):
                blk = _gather_window(src[i], axes[i], shard_shapes[i], me, h)
                _remote(blk, blk, send_sems.at[3 * i + j], recv_sems.at[3 * i + j], (*chip, c)).start()
        token[...] = jnp.zeros_like(token)

    res = pl.pallas_call(
        body, name="ag_start_" + str(n),
        out_shape=(pltpu.SemaphoreType.DMA((3 * n,)), pltpu.SemaphoreType.DMA((3 * n,)),
                   *[pltpu.HBM(f.shape, f.dtype) for f in fulls], _sds((8, LANES), F32)),
        in_specs=[HBM] * n, out_specs=(SEM, SEM, *[HBM] * n, pl.BlockSpec(memory_space=pltpu.VMEM)),
        input_output_aliases={i: 2 + i for i in range(n)},
        compiler_params=pltpu.CompilerParams(has_side_effects=EFFECT),
    )(*[_in_hbm(f) for f in fulls])
    return res[0], res[1], list(res[2:2 + n]), res[2 + n]


def _ag_mid(bufs, slots, axes, shard_shapes, whole, send_sems, recv_sems, after, name):
    ng = len(bufs)

    def body(*refs):
        src = refs[:ng]
        s_in, r_in = refs[ng], refs[ng + 1]
        fsend, frecv = refs[ng + 3], refs[ng + 4]
        x, y, c, chips = _place()
        me = 2 * x + y
        sib = (x, y, 1 - c)
        for k, i in enumerate(slots):
            h = None if whole[k] else c
            for j, chip in enumerate(chips):
                cj = 2 * chip[0] + chip[1]
                mine = _gather_window(src[k], axes[k], shard_shapes[k], me, h)
                theirs = _gather_window(src[k], axes[k], shard_shapes[k], cj, h)
                _remote(theirs, theirs, s_in.at[3 * i + j], r_in.at[3 * i + j], (*chip, c)).wait_recv()
                _remote(mine, mine, s_in.at[3 * i + j], r_in.at[3 * i + j], (*chip, c)).wait_send()
                if not whole[k]:
                    _remote(theirs, theirs, fsend.at[3 * k + j], frecv.at[3 * k + j], sib).start()
        token = refs[2 * ng + 5]
        token[...] = jnp.zeros_like(token)

    res = pl.pallas_call(
        body, name=name,
        out_shape=(pltpu.SemaphoreType.DMA((3 * ng,)), pltpu.SemaphoreType.DMA((3 * ng,)),
                   *[pltpu.HBM(b.shape, b.dtype) for b in bufs], _sds((8, LANES), F32)),
        in_specs=[HBM] * ng + [SEM, SEM, ANY],
        out_specs=(SEM, SEM, *[HBM] * ng, pl.BlockSpec(memory_space=pltpu.VMEM)),
        input_output_aliases={k: 2 + k for k in range(ng)},
        compiler_params=pltpu.CompilerParams(has_side_effects=EFFECT),
    )(*bufs, send_sems, recv_sems, after)
    return res[0], res[1], list(res[2:2 + ng]), res[2 + ng]


def _ag_wait(bufs, axes, shard_shapes, whole, fsend, frecv, after, name):
    ng = len(bufs)

    def body(*refs):
        src = refs[:ng]
        s_in, r_in = refs[ng], refs[ng + 1]
        x, y, c, chips = _place()
        sib = (x, y, 1 - c)
        for k in range(ng):
            if whole[k]:
                continue
            for j, chip in enumerate(chips):
                cj = 2 * chip[0] + chip[1]
                sent = _gather_window(src[k], axes[k], shard_shapes[k], cj, c)
                landed = _gather_window(src[k], axes[k], shard_shapes[k], cj, 1 - c)
                _remote(landed, landed, s_in.at[3 * k + j], r_in.at[3 * k + j], sib).wait_recv()
                _remote(sent, sent, s_in.at[3 * k + j], r_in.at[3 * k + j], sib).wait_send()

    res = pl.pallas_call(
        body, name=name,
        out_shape=tuple(pltpu.HBM(b.shape, b.dtype) for b in bufs),
        in_specs=[HBM] * ng + [SEM, SEM, ANY], out_specs=tuple([HBM] * ng),
        input_output_aliases={k: k for k in range(ng)},
        compiler_params=pltpu.CompilerParams(has_side_effects=EFFECT),
    )(*bufs, fsend, frecv, after)
    return list(res)


class _Schedule:
    GROUPS = ((("w_in", "conv_w"), "u1", "u1"),
              (("w_attn_proj", "w_conv_proj", "w_mix_out", "w_xq", "w_xkv", "w_xo"), "proj", "conv_o"),
              (("w_ffn_in",), "h1", "h2"),
              (("w_ffn_out",), "u3", "hid"))
    STARTS = ((0,), (1, 2, 3))
    REDUCE = ((("w_ffn_out",), "dhid", "grad:w_ffn_in", "du3"),
              (("w_ffn_in",), "du3", "dya", "grad_x"),
              (("w_xo", "w_xq", "w_xkv", "w_mix_out", "w_attn_proj", "w_conv_proj"), "dconv_o", "dq", "grad_x"),
              (("w_in",), "du1", "end", "end2"))

    def __init__(self, seed, axes, shard_shapes, place, on_ready):
        self.ax, self.shape, self.place, self.on_ready = axes, shard_shapes, place, on_ready
        self.stage, self.buf, self.slot, self.passes = {}, {}, {}, {}
        self.ready = set()
        self.grads = {}
        token = None
        for groups in self.STARTS:
            order = [nm for g in groups for nm in self.GROUPS[g][0]]
            send, recv, bufs, token = _ag_start([seed(nm, token) for nm in order], *self._meta(order))
            self.buf.update(zip(order, bufs))
            self.slot.update({nm: (send, recv, k) for k, nm in enumerate(order)})
        self.token = self.latest = token
        self.mark("start", token)

    def _meta(self, names):
        return ([self.ax[nm] for nm in names], [self.shape[nm] for nm in names], [nm == "conv_w" for nm in names])

    def mark(self, tag, value):
        for g, (names, mid, wait) in enumerate(self.GROUPS):
            if tag == mid:
                send, recv, _ = self.slot[names[0]]
                fs, fr, bufs, self.latest = _ag_mid([self.buf[nm] for nm in names], [self.slot[nm][2] for nm in names],
                                                    *self._meta(names), send, recv, value, "ag_mid_%d" % g)
                self.buf.update(zip(names, bufs))
                self.passes[g] = (fs, fr)
            if tag == wait:
                fs, fr = self.passes[g]
                bufs = _ag_wait([self.buf[nm] for nm in names], *self._meta(names), fs, fr, value, "ag_wait_%d" % g)
                self.buf.update(zip(names, bufs))
                self.ready.update(names)
        for g, (names, send, total, finish) in enumerate(self.REDUCE):
            st = self.stage.get(g)
            if st is None:
                continue
            ng = len(names)
            if tag == send and st["at"] == "pair":
                arrs = _exchange_wait("rs_pair_wait_%d" % g, st["arrs"], *st["sems"], st["plan"], value)
                parts = [_pair_add(arrs[k], arrs[ng + k], self.ax[nm], self.shape[nm], self.place, "pair_add_" + nm)
                         for k, nm in enumerate(names)]
                plan, nsem = _plan_chip(ng)
                ss, rs, arrs, self.latest = _exchange_start(
                    "rs_chip_start_%d" % g, parts + [lax.empty(p.shape, p.dtype) for p in parts], nsem, plan)
                self.stage[g] = dict(at="chip", arrs=arrs, sems=(ss, rs), plan=plan)
            elif tag == total and st["at"] == "chip":
                arrs = _exchange_wait("rs_chip_wait_%d" % g, st["arrs"], *st["sems"], st["plan"], value)
                halves = [_chip_add(arrs[k], arrs[ng + k], self.place, "chip_add_" + nm) for k, nm in enumerate(names)]
                plan, nsem = _plan_gather(ng)
                ss, rs, arrs, self.latest = _exchange_start("rs_gather_start_%d" % g, halves, nsem, plan)
                self.stage[g] = dict(at="gather", arrs=arrs, sems=(ss, rs), plan=plan)
            elif tag == finish and st["at"] == "gather":
                arrs = _exchange_wait("rs_gather_wait_%d" % g, st["arrs"], *st["sems"], st["plan"], value)
                self.stage[g] = dict(at="done")
                for nm, shard in zip(names, arrs):
                    self.on_ready(nm, shard)
        return value

    def get(self, name):
        assert name in self.ready, name
        return self.buf[name]

    def grad(self, name, g):
        self.grads[name] = g
        for gi, (names, _, _, _) in enumerate(self.REDUCE):
            if name == names[-1]:
                gs = [self.grads[nm] for nm in names]
                plan, nsem = _plan_pair(len(names), [self.ax[nm] for nm in names], [self.shape[nm] for nm in names])
                ss, rs, arrs, self.latest = _exchange_start(
                    "rs_pair_start_%d" % gi, gs + [lax.empty(a.shape, a.dtype) for a in gs], nsem, plan)
                self.stage[gi] = dict(at="pair", arrs=arrs, sems=(ss, rs), plan=plan)
                g = self.latest
        self.mark("grad:" + name, g)

    def dep(self):
        return self.latest


def _exchange_start(name, arrays, nsem, plan):
    n = len(arrays)

    def body(*refs):
        send_sems, recv_sems, token = refs[n], refs[n + 1], refs[2 * n + 2]
        sends, _ = plan(refs[:n])
        for k, (src, dst, dev) in enumerate(sends):
            _remote(src, dst, send_sems.at[k], recv_sems.at[k], dev).start()
        token[...] = jnp.zeros_like(token)

    res = pl.pallas_call(
        body, name=name,
        out_shape=(pltpu.SemaphoreType.DMA((nsem,)), pltpu.SemaphoreType.DMA((nsem,)),
                   *[pltpu.HBM(a.shape, a.dtype) for a in arrays], _sds((8, LANES), F32)),
        in_specs=[HBM] * n, out_specs=(SEM, SEM, *[HBM] * n, pl.BlockSpec(memory_space=pltpu.VMEM)),
        input_output_aliases={i: 2 + i for i in range(n)},
        compiler_params=pltpu.CompilerParams(has_side_effects=EFFECT),
    )(*[_in_hbm(a) for a in arrays])
    return res[0], res[1], list(res[2:2 + n]), res[2 + n]


def _exchange_wait(name, arrays, send_sems, recv_sems, plan, after):
    n = len(arrays)

    def body(*refs):
        s_in, r_in = refs[n], refs[n + 1]
        sends, recvs = plan(refs[:n])
        for k, land in enumerate(recvs):
            _remote(land, land, s_in.at[k], r_in.at[k], sends[k][2]).wait_recv()
        for k, (src, _, dev) in enumerate(sends):
            _remote(src, src, s_in.at[k], r_in.at[k], dev).wait_send()

    res = pl.pallas_call(
        body, name=name,
        out_shape=tuple(pltpu.HBM(a.shape, a.dtype) for a in arrays),
        in_specs=[HBM] * n + [SEM, SEM, ANY], out_specs=tuple([HBM] * n),
        input_output_aliases={i: i for i in range(n)},
        compiler_params=pltpu.CompilerParams(has_side_effects=EFFECT),
    )(*arrays, send_sems, recv_sems, after)
    return list(res)


def _plan_pair(n, axes, shard_shapes):
    def plan(refs):
        g, ra = refs[:n], refs[n:]
        x, y, c, _ = _place()
        sib = (x, y, 1 - c)

        def pieces(ref, i, h):
            if axes[i] == 1:
                return [_half(ref, h)]
            return [_window(ref, 0, shard_shapes[i], s, h) for s in range(N_CHIPS)]

        sends, recvs = [], []
        for i in range(n):
            sends += [(src, dst, sib) for src, dst in zip(pieces(g[i], i, 1 - c), pieces(ra[i], i, 1 - c))]
            recvs += pieces(ra[i], i, c)
        return sends, recvs

    return plan, sum(1 if ax == 1 else N_CHIPS for ax in axes)


def _plan_chip(n):
    def plan(refs):
        p, rc = refs[:n], refs[n:]
        x, y, c, chips = _place()
        me = 2 * x + y
        sends, recvs = [], []
        for i in range(n):
            for chip in chips:
                cj = 2 * chip[0] + chip[1]
                sends.append((p[i].at[cj], rc[i].at[me], (*chip, c)))
                recvs.append(rc[i].at[cj])
        return sends, recvs

    return plan, 3 * n


def _plan_gather(n):
    def plan(refs):
        x, y, c, _ = _place()
        sib = (x, y, 1 - c)
        return ([(_half(r, c), _half(r, c), sib) for r in refs], [_half(r, 1 - c) for r in refs])

    return plan, n


def _pair_add(g, ra, ax, shard_shape, place, name):
    sr, sc = shard_shape
    hr = sr // 2
    wc = sc
    tr = _pick(hr, (256, 352, 128, 64, 32, 16))
    nr = hr // tr

    def body(p_ref, a_ref, b_ref, o_ref):
        o_ref[...] = (a_ref[...].astype(F32) + b_ref[...].astype(F32)).astype(o_ref.dtype)

    if ax == 1:
        src = pl.BlockSpec((tr, wc), lambda s, r, p_ref: (p_ref[0] * nr + r, s))
    else:
        src = pl.BlockSpec((tr, wc), lambda s, r, p_ref: (s * 2 * nr + p_ref[0] * nr + r, 0))
    return pl.pallas_call(
        body, name=name,
        grid_spec=pltpu.PrefetchScalarGridSpec(
            num_scalar_prefetch=1, grid=(N_CHIPS, nr), in_specs=[src, src],
            out_specs=pl.BlockSpec((None, tr, wc), lambda s, r, p_ref: (s, r, 0))),
        out_shape=_sds((N_CHIPS, hr, wc), BF16),
        compiler_params=_cparams("parallel", "parallel"),
    )(place, g, ra)


def _chip_add(part, rc, place, name):
    _, hr, wc = rc.shape
    tr = _pick(hr, (256, 352, 128, 64, 32, 16))
    nr = hr // tr

    def body(p_ref, own_ref, r1_ref, r2_ref, r3_ref, o_ref):
        acc = own_ref[...].astype(F32)
        for r_ref in (r1_ref, r2_ref, r3_ref):
            acc = acc + r_ref[...].astype(F32)
        o_ref[...] = acc

    def slot(k):
        return pl.BlockSpec((None, tr, wc), lambda r, p_ref: ((p_ref[1] + k) % N_CHIPS, r, 0))

    return pl.pallas_call(
        body, name=name,
        grid_spec=pltpu.PrefetchScalarGridSpec(
            num_scalar_prefetch=1, grid=(nr,), in_specs=[slot(0), slot(1), slot(2), slot(3)],
            out_specs=pl.BlockSpec((tr, wc), lambda r, p_ref: (p_ref[0] * nr + r, 0))),
        out_shape=_sds((2 * hr, wc), F32),
        compiler_params=_cparams("parallel"),
    )(place, part, rc, rc, rc)


N_DEV = 8


def _all_reduce_small(buf):
    r, cdim = buf.shape

    def body(x_ref, o_ref, land, send_sems, recv_sems):
        x, y, c, _ = _place()
        me = 4 * x + 2 * y + c
        land[me] = x_ref[...]
        sends = []
        for k in range(1, N_DEV):
            kx, ky, kc = (k >> 2) & 1, (k >> 1) & 1, k & 1
            peer = (1 - x if kx else x, 1 - y if ky else y, 1 - c if kc else c)
            cp = _remote(x_ref, land.at[me], send_sems.at[k - 1], recv_sems.at[k - 1], peer)
            cp.start()
            sends.append(cp)
        for k in range(1, N_DEV):
            kx, ky, kc = (k >> 2) & 1, (k >> 1) & 1, k & 1
            peer = (1 - x if kx else x, 1 - y if ky else y, 1 - c if kc else c)
            pidx = 4 * peer[0] + 2 * peer[1] + peer[2]
            _remote(land.at[pidx], land.at[pidx], send_sems.at[k - 1], recv_sems.at[k - 1], peer).wait_recv()
        for cp in sends:
            cp.wait_send()
        acc = land[0]
        for dev in range(1, N_DEV):
            acc = acc + land[dev]
        o_ref[...] = acc

    vm = pl.BlockSpec(memory_space=pltpu.VMEM)
    return pl.pallas_call(
        body, name="all_reduce_small", in_specs=[vm], out_specs=vm, out_shape=_sds((r, cdim), F32),
        scratch_shapes=[pltpu.VMEM((N_DEV, r, cdim), F32), pltpu.SemaphoreType.DMA((N_DEV - 1,)),
                        pltpu.SemaphoreType.DMA((N_DEV - 1,))],
    )(buf)


SMALL_ROWS = 16


def kernel(x, mem, g_mix, w_in, conv_w, attn_sinks, w_attn_proj, w_conv_proj, w_mix_out, g_xattn, g_mem, w_xq, w_xkv, w_xo, g_ffn, w_ffn_in, w_ffn_out, g_final, loss_target, m_g_mix, m_w_in, m_conv_w, m_attn_sinks, m_w_attn_proj, m_w_conv_proj, m_w_mix_out, m_g_xattn, m_g_mem, m_w_xq, m_w_xkv, m_w_xo, m_g_ffn, m_w_ffn_in, m_w_ffn_out, m_g_final, v_g_mix, v_w_in, v_conv_w, v_attn_sinks, v_w_attn_proj, v_w_conv_proj, v_w_mix_out, v_g_xattn, v_g_mem, v_w_xq, v_w_xkv, v_w_xo, v_g_ffn, v_w_ffn_in, v_w_ffn_out, v_g_final):
    w = dict(g_mix=g_mix, w_in=w_in[0], conv_w=conv_w[0], attn_sinks=attn_sinks, w_attn_proj=w_attn_proj[0],
             w_conv_proj=w_conv_proj[0], w_mix_out=w_mix_out[0], g_xattn=g_xattn, g_mem=g_mem, w_xq=w_xq[0],
             w_xkv=w_xkv[0], w_xo=w_xo[0], g_ffn=g_ffn, w_ffn_in=w_ffn_in[0], w_ffn_out=w_ffn_out[0],
             g_final=g_final[None])
    m = dict(g_mix=m_g_mix, w_in=m_w_in[0], conv_w=m_conv_w[0], attn_sinks=m_attn_sinks,
             w_attn_proj=m_w_attn_proj[0], w_conv_proj=m_w_conv_proj[0], w_mix_out=m_w_mix_out[0],
             g_xattn=m_g_xattn, g_mem=m_g_mem, w_xq=m_w_xq[0], w_xkv=m_w_xkv[0], w_xo=m_w_xo[0], g_ffn=m_g_ffn,
             w_ffn_in=m_w_ffn_in[0], w_ffn_out=m_w_ffn_out[0], g_final=m_g_final[None])
    v = dict(g_mix=v_g_mix, w_in=v_w_in[0], conv_w=v_conv_w[0], attn_sinks=v_attn_sinks,
             w_attn_proj=v_w_attn_proj[0], w_conv_proj=v_w_conv_proj[0], w_mix_out=v_w_mix_out[0],
             g_xattn=v_g_xattn, g_mem=v_g_mem, w_xq=v_w_xq[0], w_xkv=v_w_xkv[0], w_xo=v_w_xo[0], g_ffn=v_g_ffn,
             w_ffn_in=v_w_ffn_in[0], w_ffn_out=v_w_ffn_out[0], g_final=v_g_final[None])
    names = [nm for nm, _ in BIG]
    axes = [ax for _, ax in BIG]
    d = x.shape[2]
    cw = w["conv_w"].shape[1] * N_CHIPS
    chip = (2 * lax.axis_index("x") + lax.axis_index("y")).astype(jnp.int32)
    place = jnp.stack([lax.axis_index("c").astype(jnp.int32), chip])
    shard_shapes = [w[nm].shape for nm in names]

    def seed(nm, token):
        me1 = chip.reshape(1)
        if token is not None:
            me1 = me1 + token[0, 0].astype(jnp.int32)
        if nm == "conv_w":
            return _cast_to_full(w[nm], 1, me1, "place_conv_w", F32)
        return _cast_to_full(w[nm], dict(BIG)[nm], me1, "cast_" + nm)

    upd = {}

    def on_ready(nm, shard):
        upd[nm] = _adamw(w[nm], shard, m[nm], v[nm], "adamw_" + nm)
        grads[nm] = upd[nm][3]

    grads = {}
    wts = _Schedule(seed, dict(zip(names + ["conv_w"], axes + [1])),
                    dict(zip(names + ["conv_w"], shard_shapes + [w["conv_w"].shape])), place, on_ready)
    aw, cw = w["w_attn_proj"].shape[0], w["w_conv_proj"].shape[0]
    kvw = (w["w_in"].shape[1] * N_CHIPS - aw - 3 * cw - 2 * d) // 2
    grad_x, small = _local_step(
        x[0], mem[0], loss_target[0], w["g_mix"] + wts.token[0:1, 0:1], w["attn_sinks"], w["g_xattn"], w["g_mem"],
        w["g_ffn"], w["g_final"], (aw, cw, kvw), wts)

    pw = max(d, cw)

    def row(a):
        return jnp.pad(a, ((0, 0), (0, pw - a.shape[1])))

    gains = ("g_mix", "g_xattn", "g_mem", "g_ffn", "g_final")
    packed = jnp.concatenate(
        [row(small[nm]) for nm in gains] + [row(small["conv_w"]),
         row(jnp.concatenate([small["attn_sinks"], small["loss"]], axis=1)),
         jnp.zeros((SMALL_ROWS - 9, pw), F32)], axis=0)
    total = _all_reduce_small(packed)
    wts.mark("end", total)
    nsink = attn_sinks.shape[1]
    grads.update({nm: total[k:k + 1, :d] for k, nm in enumerate(gains)})
    grads.update(conv_w=lax.dynamic_slice(total, (5, chip * (cw // N_CHIPS)), (3, cw // N_CHIPS)),
                 attn_sinks=total[8:9, :nsink])
    loss = total[8, nsink]
    for nm in gains + ("conv_w", "attn_sinks"):
        upd[nm] = _adamw(w[nm], grads[nm], m[nm], v[nm], "adamw_" + nm)
    wts.mark("end2", upd["g_final"][0])

    order = ["g_mix", "w_in", "conv_w", "attn_sinks", "w_attn_proj", "w_conv_proj", "w_mix_out", "g_xattn", "g_mem",
             "w_xq", "w_xkv", "w_xo", "g_ffn", "w_ffn_in", "w_ffn_out", "g_final"]

    stacked = set(names) | {"conv_w"}

    def shaped(nm, a):
        if nm == "g_final":
            return a[0]
        return a[None] if nm in stacked else a

    outs = [loss, grad_x[None]]
    outs += [shaped(nm, grads[nm]) for nm in order]
    for k in range(3):
        outs += [shaped(nm, upd[nm][k]) for nm in order]
    return tuple(outs)
```

```python
import functools

import jax
import jax.numpy as jnp
from jax import lax
from jax.experimental import pallas as pl
from jax.experimental.pallas import tpu as pltpu

F32 = jnp.float32
BF16 = jnp.bfloat16

VMEM_LIMIT_BYTES = 56 * 1024 * 1024
LANES = 128
HEAD_DIM = 64
BLOCK = 128
X_HEAD_DIM = 128
ROPE_THETA = 10000.0
EPS = 1e-6
NEG = -1e30
ADAM_LR, ADAM_B1, ADAM_B2, ADAM_EPS, ADAM_WD, ADAM_STEP = 0.001, 0.9, 0.999, 1e-08, 0.01, 10
N_CHIPS = 4
MESH = pl.DeviceIdType.MESH
ANY = pl.BlockSpec(memory_space=pl.ANY)


def _pick(dim, prefs):
    for p in prefs:
        if dim % p == 0:
            return p
    return dim


def _cparams(*sem):
    return pltpu.CompilerParams(dimension_semantics=sem, vmem_limit_bytes=VMEM_LIMIT_BYTES)


def _sds(shape, dtype):
    return jax.ShapeDtypeStruct(shape, dtype)


def _sigmoid(v):
    return 0.5 * jnp.tanh(0.5 * v) + 0.5


MATMUL_VMEM_BUDGET = 46 * 1024 * 1024


def _tiles(mode, m, n):
    if mode == "tn" and m % 1024 != 0:
        return _pick(m, (512, 256, 128)), _pick(n, (1024, 512, 256, 128)), True
    return _pick(m, (1024, 512, 256, 128)), _pick(n, (512, 256, 128)), False


def _k_parts(m, n, k):
    tm, tn = _pick(m, (1024, 512, 256, 128)), _pick(n, (512, 256, 128))
    for parts in range(1, k // LANES + 1):
        if k % (parts * LANES) == 0 and 4 * (tm + tn) * (k // parts) + 16 * tm * tn <= MATMUL_VMEM_BUDGET:
            return parts
    return k // LANES


def _matmul(a, b, *, mode, out_dtype, name, res=None, dep=None, cols=None):
    if mode == "nn":
        (m, k), (k2, n) = a.shape, b.shape
    elif mode == "nt":
        (m, k), (n, k2) = (a.shape[-2], a.shape[-1] * (a.shape[0] if a.ndim == 3 else 1)), b.shape
    else:
        (k, m), (k2, n) = a.shape, (b.shape[-2], b.shape[-1] * (b.shape[0] if b.ndim == 3 else 1))
    assert k == k2, (a.shape, b.shape, mode)
    assert cols is None or mode != "nt"
    cols = cols or (0, n)
    parts = 1 if mode == "tn" else (a.shape[0] if a.ndim == 3 else _k_parts(m, cols[1], k))
    for p in range(parts):
        last = p == parts - 1
        res = _matmul_slice(a, b, mode=mode, out_dtype=out_dtype if last else F32, res=res, dep=dep, cols=cols,
                            kslice=(p, parts), name=name + ("_k%d" % p if parts > 1 else ""))
    return res


def _matmul_slice(a, b, *, mode, out_dtype, name, res, dep, kslice, cols):
    part, parts = kslice
    (m, k) = a.shape[-2:] if mode != "tn" else a.shape[::-1]
    col0, n = cols
    tk = k if a.ndim == 3 else k // parts
    tm, tn, swap = _tiles(mode, m, b.shape[2] if (mode == "tn" and b.ndim == 3) else n)
    while col0 % tn:
        tn //= 2
    joff = col0 // tn
    dims = {"nn": (((1,), (0,)), ((), ())), "nt": (((1,), (1,)), ((), ())), "tn": (((0,), (0,)), ((), ()))}[mode]
    has_res = res is not None
    has_dep = dep is not None

    def body(*refs):
        a_ref, b_ref = refs[0], refs[1]
        o_ref = refs[2 + has_res + has_dep]
        val = lax.dot_general(a_ref[...], b_ref[...], dims, preferred_element_type=F32)
        if has_res:
            val = val + refs[2][...]
        o_ref[...] = val.astype(o_ref.dtype)

    def spec(shape, f):
        if swap:
            return pl.BlockSpec(shape, lambda j, i: f(i, j))
        return pl.BlockSpec(shape, f)

    if mode == "tn":
        a_spec = spec((tk, tm), lambda i, j: (part, i))
    elif a.ndim == 3:
        a_spec = spec((None, tm, tk), lambda i, j: (part, i, 0))
    else:
        a_spec = spec((tm, tk), lambda i, j: (i, part))
    if mode == "nt":
        b_spec = spec((tn, tk), lambda i, j: (j, part))
    elif b.ndim == 3:
        per = b.shape[2] // tn
        b_spec = spec((None, tk, tn), lambda i, j: (j // per, part, j % per))
    else:
        b_spec = spec((tk, tn), lambda i, j: (part, joff + j))
    o_spec = spec((tm, tn), lambda i, j: (i, j))
    return pl.pallas_call(
        body,
        name=name,
        grid=(n // tn, m // tm) if swap else (m // tm, n // tn),
        in_specs=[a_spec, b_spec] + ([o_spec] if has_res else []) + ([ANY] if has_dep else []),
        out_specs=o_spec,
        out_shape=_sds((m, n), out_dtype),
        compiler_params=_cparams("parallel", "parallel"),
    )(*([a, b] + ([res] if has_res else []) + ([dep] if has_dep else [])))


def _rms_fwd(x, g, name):
    t, d = x.shape
    tm = _pick(t, (512, 256, 128))

    def body(x_ref, g_ref, o_ref):
        xf = x_ref[...]
        r = lax.rsqrt(jnp.mean(xf * xf, axis=-1, keepdims=True) + EPS)
        o_ref[...] = (xf * r * g_ref[...]).astype(o_ref.dtype)

    row = pl.BlockSpec((tm, d), lambda i: (i, 0))
    return pl.pallas_call(
        body, name=name, grid=(t // tm,),
        in_specs=[row, pl.BlockSpec((1, d), lambda i: (0, 0))],
        out_specs=row, out_shape=_sds((t, d), BF16),
        compiler_params=_cparams("parallel"),
    )(x, g)


def _rms_bwd_math(xf, g, du):
    r = lax.rsqrt(jnp.mean(xf * xf, axis=-1, keepdims=True) + EPS)
    xh = xf * r
    gdy = g * du
    dx = r * (gdy - xh * jnp.mean(gdy * xh, axis=-1, keepdims=True))
    dg = jnp.sum(du * xh, axis=0, keepdims=True)
    return dx, dg


def _rms_bwd(x, g, du, dh, name):
    t, d = x.shape
    tm = _pick(t, (256, 128))
    has_dh = dh is not None

    def body(*refs):
        x_ref, g_ref, du_ref = refs[0], refs[1], refs[2]
        o_ref, ob_ref, dg_ref = refs[3 + has_dh:]
        dx, dg = _rms_bwd_math(x_ref[...], g_ref[...], du_ref[...].astype(F32))
        if has_dh:
            dx = dx + refs[3][...]
        o_ref[...] = dx
        ob_ref[...] = dx.astype(BF16)

        @pl.when(pl.program_id(0) == 0)
        def _():
            dg_ref[...] = dg

        @pl.when(pl.program_id(0) > 0)
        def _():
            dg_ref[...] += dg

    row = pl.BlockSpec((tm, d), lambda i: (i, 0))
    vec = pl.BlockSpec((1, d), lambda i: (0, 0))
    return pl.pallas_call(
        body, name=name, grid=(t // tm,),
        in_specs=[row, vec, row] + ([row] if has_dh else []),
        out_specs=[row, row, vec],
        out_shape=[_sds((t, d), F32), _sds((t, d), BF16), _sds((1, d), F32)],
        compiler_params=_cparams("arbitrary"),
    )(*([x, g, du] + ([dh] if has_dh else [])))


ROW_TILE = 256


def _rows_matmul(a, b, *, mode, kslice, res, rows_in, vec, epilogue, outs, name, dep):
    part, parts = kslice
    t = a.shape[-2]
    n = b.shape[1] if mode == "nn" else b.shape[0]
    kc = a.shape[-1] if a.ndim == 3 else a.shape[1] // parts
    tm = _pick(t, (ROW_TILE, 128))
    dims = (((1,), (0,)), ((), ())) if mode == "nn" else (((1,), (1,)), ((), ()))
    has_res = res is not None
    deps = [] if dep is None else [dep]
    n_in = 2 + has_res + len(rows_in) + 1 + len(deps)

    def body(*refs):
        val = lax.dot_general(refs[0][...], refs[1][...], dims, preferred_element_type=F32)
        if has_res:
            val = val + refs[2][...]
        row_refs = refs[2 + has_res:2 + has_res + len(rows_in)]
        epilogue(val, row_refs, refs[2 + has_res + len(rows_in)], refs[n_in:], pl.program_id(0) == 0)

    if a.ndim == 3:
        a_spec = pl.BlockSpec((None, tm, kc), lambda i: (part, i, 0))
    else:
        a_spec = pl.BlockSpec((tm, kc), lambda i: (i, part))
    if mode == "nn":
        b_spec = pl.BlockSpec((kc, n), lambda i: (part, 0), pipeline_mode=pl.Buffered(1))
    else:
        b_spec = pl.BlockSpec((n, kc), lambda i: (0, part), pipeline_mode=pl.Buffered(1))
    row = pl.BlockSpec((tm, n), lambda i: (i, 0))
    kinds = {"row": row, "vec": pl.BlockSpec((1, n), lambda i: (0, 0)), "lane": pl.BlockSpec((1, LANES), lambda i: (0, 0))}
    shapes = {"row": (t, n), "vec": (1, n), "lane": (1, LANES)}
    return pl.pallas_call(
        body, name=name, grid=(t // tm,),
        in_specs=[a_spec, b_spec] + [row] * (has_res + len(rows_in)) + [kinds["vec"]] + [ANY] * len(deps),
        out_specs=[kinds[o[0]] for o in outs],
        out_shape=[_sds(shapes[o[0]], o[1] if len(o) > 1 else F32) for o in outs],
        compiler_params=_cparams("arbitrary"),
    )(*([a, b] + ([res] if has_res else []) + list(rows_in) + [vec] + deps))


def _accumulate(ref, value, first):
    @pl.when(first)
    def _():
        ref[...] = value

    @pl.when(jnp.logical_not(first))
    def _():
        ref[...] += value


def _ep_norm(val, rows, g_ref, outs, first):
    h_ref, u_ref = outs
    h_ref[...] = val
    r = lax.rsqrt(jnp.mean(val * val, axis=-1, keepdims=True) + EPS)
    u_ref[...] = (val * r * g_ref[...]).astype(u_ref.dtype)


def _ep_loss(val, rows, g_ref, outs, first):
    o_ref, ob_ref, dg_ref, l_ref = outs
    gv = g_ref[...]
    r = lax.rsqrt(jnp.mean(val * val, axis=-1, keepdims=True) + EPS)
    err = val * r * gv - rows[0][...]
    part = 0.5 * jnp.sum(jnp.mean(err * err, axis=-1, keepdims=True), axis=0, keepdims=True)
    dx, dg = _rms_bwd_math(val, gv, err * (1.0 / val.shape[1]))
    o_ref[...] = dx
    ob_ref[...] = dx.astype(BF16)
    _accumulate(dg_ref, dg, first)
    _accumulate(l_ref, jnp.broadcast_to(part, (1, LANES)), first)


def _ep_rms_bwd(val, rows, g_ref, outs, first):
    o_ref, ob_ref, dg_ref = outs
    dx, dg = _rms_bwd_math(rows[0][...], g_ref[...], val)
    dx = dx + rows[1][...]
    o_ref[...] = dx
    ob_ref[...] = dx.astype(BF16)
    _accumulate(dg_ref, dg, first)


def _rope_tables(t):
    half = HEAD_DIM // 2
    inv_freq = ROPE_THETA ** (-jnp.arange(half, dtype=F32) / half)
    ang = jnp.arange(t, dtype=F32)[:, None] * inv_freq[None, :]
    cos = jnp.cos(ang)
    sin = jnp.sin(ang)
    reps = LANES // HEAD_DIM
    cos_t = jnp.tile(jnp.concatenate([cos, cos], axis=1), (1, reps))
    sin_t = jnp.tile(jnp.concatenate([-sin, sin], axis=1), (1, reps))
    return cos_t, sin_t


def _rope(v, cos, sin):
    w = v.shape[1]
    c = jnp.tile(cos, (1, w // LANES))
    s = jnp.tile(sin, (1, w // LANES))
    lane = lax.broadcasted_iota(jnp.int32, v.shape, 1)
    first = (lane % HEAD_DIM) < (HEAD_DIM // 2)
    partner = jnp.where(first, pltpu.roll(v, w - HEAD_DIM // 2, 1), pltpu.roll(v, HEAD_DIM // 2, 1))
    return v * c + partner * s


def _heads(v, count):
    return jnp.concatenate([v[:, i * HEAD_DIM:(i + 1) * HEAD_DIM] for i in range(count)], axis=0)


def _unheads(v, count):
    r = v.shape[0] // count
    return jnp.concatenate([v[g * r:(g + 1) * r] for g in range(count)], axis=1)


def _rope_qkv(proj, cos, sin, aw, kvw):
    t = proj.shape[0]
    nkv = kvw // HEAD_DIM
    koff = aw // kvw
    tm = _pick(t, (256, 128))

    def body(q_ref, k_ref, v_ref, c_ref, s_ref, qo_ref, ko_ref, vo_ref):
        c, s = c_ref[...], s_ref[...]
        qo_ref[...] = _rope(q_ref[...].astype(F32), c, s).astype(BF16)
        k = _rope(k_ref[...].astype(F32), c, s).astype(BF16)
        v = v_ref[...].astype(BF16)
        for h in range(nkv):
            ko_ref[h] = k[:, h * HEAD_DIM:(h + 1) * HEAD_DIM]
            vo_ref[h] = v[:, h * HEAD_DIM:(h + 1) * HEAD_DIM]

    def row(w, j):
        return pl.BlockSpec((tm, w), lambda i: (i, j))

    hm = pl.BlockSpec((nkv, tm, HEAD_DIM), lambda i: (0, i, 0))
    return pl.pallas_call(
        body, name="rope_qkv", grid=(t // tm,),
        in_specs=[row(aw, 0), row(kvw, koff), row(kvw, koff + 1), row(LANES, 0), row(LANES, 0)],
        out_specs=[row(aw, 0), hm, hm],
        out_shape=[_sds((t, aw), BF16), _sds((nkv, t, HEAD_DIM), BF16), _sds((nkv, t, HEAD_DIM), BF16)],
        compiler_params=_cparams("parallel"),
    )(proj, proj, proj, cos, sin)


def _attn_probs(qs, kb, n, h, qpk, sinks_ref):
    s = lax.dot_general(qs, kb, (((1,), (1,)), ((), ())), preferred_element_type=F32) * (HEAD_DIM ** -0.5)
    qi = lax.broadcasted_iota(jnp.int32, (BLOCK, 2 * BLOCK), 0)
    kc = lax.broadcasted_iota(jnp.int32, (BLOCK, 2 * BLOCK), 1)
    valid = (kc > qi) & (kc <= qi + BLOCK) & ((kc >= BLOCK) | (n > 0))
    bias = jnp.tile(jnp.where(valid, 0.0, NEG).astype(F32), (qpk, 1))
    s = s + bias
    rowg = lax.broadcasted_iota(jnp.int32, (qpk * BLOCK, 1), 0) // BLOCK
    sink = jnp.zeros((qpk * BLOCK, 1), F32)
    for g in range(qpk):
        sink = jnp.where(rowg == g, sinks_ref[0, h * qpk + g], sink)
    m = jnp.maximum(jnp.max(s, axis=-1, keepdims=True), sink)
    e = jnp.exp(s - m)
    es = jnp.exp(sink - m)
    inv = 1.0 / (jnp.sum(e, axis=-1, keepdims=True) + es)
    return e * inv, es * inv, rowg


HEADS_PER_STEP = 4


def _attn_specs(qw, hp):
    def head(f):
        return pl.BlockSpec((hp, BLOCK, HEAD_DIM), lambda n, h: (h, f(n), 0))

    cur = lambda n: n
    prev = lambda n: jnp.maximum(n - 1, 0)
    return [pl.BlockSpec((BLOCK, hp * qw), lambda n, h: (n, h)), head(cur), head(prev), head(cur), head(prev),
            pl.BlockSpec(memory_space=pltpu.SMEM)]


def _attn_fwd(q_r, k_r, v_h, sinks):
    t, aw = q_r.shape
    nkv = k_r.shape[0]
    qpk = aw // (nkv * HEAD_DIM)
    qw = qpk * HEAD_DIM
    hp = HEADS_PER_STEP if nkv % HEADS_PER_STEP == 0 else 1

    def body(q_ref, kc_ref, kp_ref, vc_ref, vp_ref, sinks_ref, o_ref):
        n, hg = pl.program_id(0), pl.program_id(1)
        outs = []
        for j in range(hp):
            kb = jnp.concatenate([kp_ref[j], kc_ref[j]], axis=0)
            vb = jnp.concatenate([vp_ref[j], vc_ref[j]], axis=0)
            qs = _heads(q_ref[:, j * qw:(j + 1) * qw], qpk)
            p, _, _ = _attn_probs(qs, kb, n, hg * hp + j, qpk, sinks_ref)
            outs.append(_unheads(jnp.dot(p.astype(BF16), vb, preferred_element_type=F32), qpk))
        o_ref[...] = jnp.concatenate(outs, axis=1).astype(o_ref.dtype)

    return pl.pallas_call(
        body, name="attn_fwd", grid=(t // BLOCK, nkv // hp),
        in_specs=_attn_specs(qw, hp),
        out_specs=pl.BlockSpec((BLOCK, hp * qw), lambda n, h: (n, h)),
        out_shape=_sds((t, aw), BF16),
        compiler_params=_cparams("parallel", "parallel"),
    )(q_r, k_r, k_r, v_h, v_h, sinks)


def _attn_bwd(q_r, k_r, v_h, sinks, o, do):
    t, aw = q_r.shape
    nkv = k_r.shape[0]
    qpk = aw // (nkv * HEAD_DIM)
    qw = qpk * HEAD_DIM
    hp = HEADS_PER_STEP if nkv % HEADS_PER_STEP == 0 else 1
    scale = HEAD_DIM ** -0.5

    def body(q_ref, kc_ref, kp_ref, vc_ref, vp_ref, sinks_ref, o_ref, do_ref,
             dq_ref, dkc_ref, dkp_ref, dvc_ref, dvp_ref, ds_ref):
        n, hg = pl.program_id(0), pl.program_id(1)
        lane = lax.broadcasted_iota(jnp.int32, (8, LANES), 1)
        row0 = lax.broadcasted_iota(jnp.int32, (8, LANES), 0) == 0
        dsink = jnp.zeros((8, LANES), F32)
        dqs = []
        for j in range(hp):
            h = hg * hp + j
            cols = slice(j * qw, (j + 1) * qw)
            kb = jnp.concatenate([kp_ref[j], kc_ref[j]], axis=0)
            vb = jnp.concatenate([vp_ref[j], vc_ref[j]], axis=0)
            qs = _heads(q_ref[:, cols], qpk)
            dos = _heads(do_ref[:, cols], qpk)
            p, psink, rowg = _attn_probs(qs, kb, n, h, qpk, sinks_ref)
            pb = p.astype(BF16)
            delta = jnp.sum(dos.astype(F32) * _heads(o_ref[:, cols], qpk).astype(F32), axis=-1, keepdims=True)
            dv = lax.dot_general(pb, dos, (((0,), (0,)), ((), ())), preferred_element_type=F32)
            dp = lax.dot_general(dos, vb, (((1,), (1,)), ((), ())), preferred_element_type=F32)
            dsc = (p * (dp - delta)).astype(BF16)
            dqs.append(_unheads(jnp.dot(dsc, kb, preferred_element_type=F32) * scale, qpk))
            dk = lax.dot_general(dsc, qs, (((0,), (0,)), ((), ())), preferred_element_type=F32) * scale
            dkp_ref[j] = dk[:BLOCK]
            dkc_ref[j] = dk[BLOCK:]
            dvp_ref[j] = dv[:BLOCK]
            dvc_ref[j] = dv[BLOCK:]
            sink_term = psink * delta
            for g in range(qpk):
                val = -jnp.sum(jnp.where(rowg == g, sink_term, 0.0))
                dsink = jnp.where(row0 & (lane == h * qpk + g), val, dsink)
        dq_ref[...] = jnp.concatenate(dqs, axis=1).astype(dq_ref.dtype)
        first = (n == 0) & (hg == 0)

        @pl.when(first)
        def _():
            ds_ref[...] = dsink

        @pl.when(jnp.logical_not(first))
        def _():
            ds_ref[...] += dsink

    qblk = pl.BlockSpec((BLOCK, hp * qw), lambda n, h: (n, h))
    kvblk = pl.BlockSpec((hp, BLOCK, HEAD_DIM), lambda n, h: (h, n, 0))
    return pl.pallas_call(
        body, name="attn_bwd", grid=(t // BLOCK, nkv // hp),
        in_specs=_attn_specs(qw, hp) + [qblk, qblk],
        out_specs=[qblk, kvblk, kvblk, kvblk, kvblk, pl.BlockSpec((8, LANES), lambda n, h: (0, 0))],
        out_shape=[_sds((t, aw), BF16)] + [_sds((nkv, t, HEAD_DIM), F32)] * 4 + [_sds((8, LANES), F32)],
        compiler_params=_cparams("arbitrary", "arbitrary"),
    )(q_r, k_r, k_r, v_h, v_h, sinks, o, do)


HALO = 16


def _shift_down(v, k, halo):
    rows = lax.broadcasted_iota(jnp.int32, v.shape, 0)
    out = pltpu.roll(v, k, 0)
    for r in range(k):
        out = jnp.where(rows == r, halo[HALO - k + r:HALO - k + r + 1, :], out)
    return out


def _shift_up(v, k, halo):
    tm = v.shape[0]
    rows = lax.broadcasted_iota(jnp.int32, v.shape, 0)
    out = pltpu.roll(v, tm - k, 0)
    for r in range(k):
        out = jnp.where(rows == tm - k + r, halo[r:r + 1, :], out)
    return out


def _conv_fwd(proj, conv_w, zoff, cw, cb):
    t = proj.shape[0]
    tm = _pick(t, (512, 256, 128))
    zb, nb = zoff // cb, cw // cb
    hb = tm // HALO

    def body(z_ref, gb_ref, gc_ref, zp_ref, gcp_ref, w_ref, o_ref):
        i = pl.program_id(0)
        cz = gc_ref[...].astype(F32) * z_ref[...].astype(F32)
        czp = gcp_ref[...].astype(F32) * zp_ref[...].astype(F32) * (i > 0).astype(F32)
        w = w_ref[...]
        y = w[0:1] * _shift_down(cz, 2, czp) + w[1:2] * _shift_down(cz, 1, czp) + w[2:3] * cz
        o_ref[...] = (gb_ref[...].astype(F32) * y).astype(o_ref.dtype)

    def col(k):
        return pl.BlockSpec((tm, cb), lambda i, j: (i, zb + k * nb + j))

    def halo(k):
        return pl.BlockSpec((HALO, cb), lambda i, j: (jnp.maximum(i * hb - 1, 0), zb + k * nb + j))

    return pl.pallas_call(
        body, name="conv_fwd", grid=(t // tm, nb),
        in_specs=[col(0), col(1), col(2), halo(0), halo(2), pl.BlockSpec((3, cb), lambda i, j: (0, j))],
        out_specs=pl.BlockSpec((tm, cb), lambda i, j: (i, j)),
        out_shape=_sds((t, cw), BF16),
        compiler_params=_cparams("parallel", "parallel"),
    )(proj, proj, proj, proj, proj, conv_w)


def _conv_bwd(proj, conv_w, dco, zoff, cw, cb):
    t = proj.shape[0]
    tm = _pick(t, (512, 256, 128))
    zb, nb = zoff // cb, cw // cb
    hb = tm // HALO
    nt = t // tm

    def body(z_ref, gb_ref, gc_ref, zp_ref, gcp_ref, gbn_ref, w_ref, d_ref, dn_ref, dz_ref, dgb_ref, dgc_ref, dw_ref):
        i = pl.program_id(1)
        z, gb, gc = z_ref[...].astype(F32), gb_ref[...].astype(F32), gc_ref[...].astype(F32)
        d = d_ref[...].astype(F32)
        cz = gc * z
        czp = gcp_ref[...].astype(F32) * zp_ref[...].astype(F32) * (i > 0).astype(F32)
        w = w_ref[...]
        cz1 = _shift_down(cz, 1, czp)
        cz2 = _shift_down(cz, 2, czp)
        y = w[0:1] * cz2 + w[1:2] * cz1 + w[2:3] * cz
        dgb_ref[...] = (d * y).astype(dgb_ref.dtype)
        dy = d * gb
        dyn = dn_ref[...].astype(F32) * gbn_ref[...].astype(F32) * (i < nt - 1).astype(F32)
        dcz = w[2:3] * dy + w[1:2] * _shift_up(dy, 1, dyn) + w[0:1] * _shift_up(dy, 2, dyn)
        dgc_ref[...] = (dcz * z).astype(dgc_ref.dtype)
        dz_ref[...] = (dcz * gc).astype(dz_ref.dtype)
        rows = lax.broadcasted_iota(jnp.int32, (8, cb), 0)
        dw = jnp.zeros((8, cb), F32)
        for r, tap in enumerate((cz2, cz1, cz)):
            dw = jnp.where(rows == r, jnp.sum(dy * tap, axis=0, keepdims=True), dw)

        @pl.when(i == 0)
        def _():
            dw_ref[...] = dw

        @pl.when(i > 0)
        def _():
            dw_ref[...] += dw

    def col(k):
        return pl.BlockSpec((tm, cb), lambda j, i: (i, zb + k * nb + j))

    def halo_prev(k):
        return pl.BlockSpec((HALO, cb), lambda j, i: (jnp.maximum(i * hb - 1, 0), zb + k * nb + j))

    own = pl.BlockSpec((tm, cb), lambda j, i: (i, j))
    nxt = lambda i: jnp.minimum((i + 1) * hb, t // HALO - 1)
    return pl.pallas_call(
        body, name="conv_bwd", grid=(nb, nt),
        in_specs=[col(0), col(1), col(2), halo_prev(0), halo_prev(2),
                  pl.BlockSpec((HALO, cb), lambda j, i: (nxt(i), zb + nb + j)),
                  pl.BlockSpec((3, cb), lambda j, i: (0, j)), own,
                  pl.BlockSpec((HALO, cb), lambda j, i: (nxt(i), j))],
        out_specs=[own, own, own, pl.BlockSpec((8, cb), lambda j, i: (0, j))],
        out_shape=[_sds((t, cw), BF16)] * 3 + [_sds((8, cw), F32)],
        compiler_params=_cparams("parallel", "arbitrary"),
    )(proj, proj, proj, proj, proj, proj, conv_w, dco, dco)


def _yconv_merge(conv_o, w_cp, ya, proj, goff, cb, dep, name):
    t, k = conv_o.shape
    d = w_cp.shape[1]
    tm = _pick(t, (1024, 512, 256, 128))
    gb_, nb = goff // cb, d // cb
    deps = [] if dep is None else [dep]

    def body(a_ref, w_ref, ya_ref, ga_ref, gc_ref, *rest):
        yc_ref, o_ref = rest[len(deps):]
        yc = jnp.dot(a_ref[...], w_ref[...], preferred_element_type=F32)
        f = lambda r: r[...].astype(F32)
        yc_ref[...] = yc.astype(yc_ref.dtype)
        o_ref[...] = (_sigmoid(f(ga_ref)) * f(ya_ref) + _sigmoid(f(gc_ref)) * yc).astype(o_ref.dtype)

    own = pl.BlockSpec((tm, cb), lambda i, j: (i, j))
    return pl.pallas_call(
        body, name=name, grid=(t // tm, nb),
        in_specs=[pl.BlockSpec((tm, k), lambda i, j: (i, 0)), pl.BlockSpec((k, cb), lambda i, j: (0, j)), own,
                  pl.BlockSpec((tm, cb), lambda i, j: (i, gb_ + j)),
                  pl.BlockSpec((tm, cb), lambda i, j: (i, gb_ + nb + j))] + [ANY] * len(deps),
        out_specs=[own, own], out_shape=[_sds((t, d), BF16)] * 2,
        compiler_params=_cparams("parallel", "parallel"),
    )(conv_o, w_cp, ya, proj, proj, *deps)


def _dmerged_split(dh, w_mo, proj, ya, yc, goff, cb, dep, name):
    t, k = dh.shape
    d = w_mo.shape[0]
    tm = _pick(t, (1024, 512, 256, 128))
    gb_, nb = goff // cb, d // cb
    deps = [] if dep is None else [dep]

    def body(a_ref, w_ref, ga_ref, gc_ref, ya_ref, yc_ref, *rest):
        dya_ref, dyc_ref, dga_ref, dgc_ref = rest[len(deps):]
        dmv = lax.dot_general(a_ref[...], w_ref[...], (((1,), (1,)), ((), ())), preferred_element_type=F32)
        sa = _sigmoid(ga_ref[...].astype(F32))
        sc = _sigmoid(gc_ref[...].astype(F32))
        dya_ref[...] = (dmv * sa).astype(BF16)
        dyc_ref[...] = (dmv * sc).astype(BF16)
        dga_ref[...] = (dmv * ya_ref[...].astype(F32) * sa * (1.0 - sa)).astype(BF16)
        dgc_ref[...] = (dmv * yc_ref[...].astype(F32) * sc * (1.0 - sc)).astype(BF16)

    own = pl.BlockSpec((tm, cb), lambda i, j: (i, j))
    return pl.pallas_call(
        body, name=name, grid=(t // tm, nb),
        in_specs=[pl.BlockSpec((tm, k), lambda i, j: (i, 0)), pl.BlockSpec((cb, k), lambda i, j: (j, 0)),
                  pl.BlockSpec((tm, cb), lambda i, j: (i, gb_ + j)),
                  pl.BlockSpec((tm, cb), lambda i, j: (i, gb_ + nb + j)), own, own] + [ANY] * len(deps),
        out_specs=[own] * 4, out_shape=[_sds((t, d), BF16)] * 4,
        compiler_params=_cparams("parallel", "parallel"),
    )(dh, w_mo, proj, proj, ya, yc, *deps)


def _assemble_dproj(dq, dkc, dkp, dvc, dvp, cos, sin, dz, dgb, dgc, dga, dgg):
    t, aw = dq.shape
    nkv, cw, d = dkc.shape[0], dz.shape[1], dga.shape[1]
    kvw = nkv * HEAD_DIM
    nblk = t // BLOCK
    width = aw + 2 * kvw + 3 * cw + 2 * d

    def body(dq_ref, dkc_ref, dkp_ref, dvc_ref, dvp_ref, c_ref, s_ref, dz_ref, dgb_ref, dgc_ref, dga_ref, dgg_ref,
             o_ref):
        keep = (pl.program_id(0) < nblk - 1).astype(F32)
        c, s = c_ref[...], s_ref[...]
        dk = jnp.concatenate([dkc_ref[h] + dkp_ref[h] * keep for h in range(nkv)], axis=1)
        dv = jnp.concatenate([dvc_ref[h] + dvp_ref[h] * keep for h in range(nkv)], axis=1)
        o_ref[...] = jnp.concatenate(
            [_rope(dq_ref[...].astype(F32), c, -s).astype(BF16), _rope(dk, c, -s).astype(BF16), dv.astype(BF16),
             dz_ref[...], dgb_ref[...], dgc_ref[...], dga_ref[...], dgg_ref[...]], axis=1)

    def cur(w):
        return pl.BlockSpec((BLOCK, w), lambda n: (n, 0))

    head_cur = pl.BlockSpec((nkv, BLOCK, HEAD_DIM), lambda n: (0, n, 0))
    head_nxt = pl.BlockSpec((nkv, BLOCK, HEAD_DIM), lambda n: (0, jnp.minimum(n + 1, nblk - 1), 0))
    return pl.pallas_call(
        body, name="assemble_dproj", grid=(nblk,),
        in_specs=[cur(aw), head_cur, head_nxt, head_cur, head_nxt, cur(LANES), cur(LANES),
                  cur(cw), cur(cw), cur(cw), cur(d), cur(d)],
        out_specs=cur(width), out_shape=_sds((t, width), BF16),
        compiler_params=_cparams("parallel"),
    )(dq, dkc, dkp, dvc, dvp, cos, sin, dz, dgb, dgc, dga, dgg)


def _xattn_probs(qh, kh):
    s = lax.dot_general(qh, kh, (((1,), (1,)), ((), ())), preferred_element_type=F32) * (X_HEAD_DIM ** -0.5)
    e = jnp.exp(s - jnp.max(s, axis=-1, keepdims=True))
    return e * (1.0 / jnp.sum(e, axis=-1, keepdims=True))


def _xattn_fwd(xq, kv):
    t, xw = xq.shape
    mt = kv.shape[0]
    tm = _pick(t, (512, 256, 128))

    def body(q_ref, kv_ref, o_ref):
        outs = []
        for hd in range(xw // X_HEAD_DIM):
            hs = slice(hd * X_HEAD_DIM, (hd + 1) * X_HEAD_DIM)
            vs = slice(xw + hd * X_HEAD_DIM, xw + (hd + 1) * X_HEAD_DIM)
            p = _xattn_probs(q_ref[:, hs], kv_ref[:, hs])
            outs.append(jnp.dot(p.astype(BF16), kv_ref[:, vs], preferred_element_type=F32))
        o_ref[...] = jnp.concatenate(outs, axis=1).astype(o_ref.dtype)

    return pl.pallas_call(
        body, name="xattn_fwd", grid=(t // tm,),
        in_specs=[pl.BlockSpec((tm, xw), lambda i: (i, 0)), pl.BlockSpec((mt, 2 * xw), lambda i: (0, 0))],
        out_specs=pl.BlockSpec((tm, xw), lambda i: (i, 0)), out_shape=_sds((t, xw), BF16),
        compiler_params=_cparams("parallel"),
    )(xq, kv)


def _xattn_bwd(xq, kv, do):
    t, xw = xq.shape
    mt = kv.shape[0]
    tm = _pick(t, (512, 256, 128))
    scale = X_HEAD_DIM ** -0.5

    def body(q_ref, kv_ref, do_ref, dq_ref, dkv_ref):
        dqs, dks, dvs = [], [], []
        for hd in range(xw // X_HEAD_DIM):
            hs = slice(hd * X_HEAD_DIM, (hd + 1) * X_HEAD_DIM)
            vs = slice(xw + hd * X_HEAD_DIM, xw + (hd + 1) * X_HEAD_DIM)
            qh, kh, vh, doh = q_ref[:, hs], kv_ref[:, hs], kv_ref[:, vs], do_ref[:, hs]
            p = _xattn_probs(qh, kh)
            pb = p.astype(BF16)
            o = jnp.dot(pb, vh, preferred_element_type=F32)
            delta = jnp.sum(doh.astype(F32) * o, axis=-1, keepdims=True)
            dvs.append(lax.dot_general(pb, doh, (((0,), (0,)), ((), ())), preferred_element_type=F32))
            dp = lax.dot_general(doh, vh, (((1,), (1,)), ((), ())), preferred_element_type=F32)
            dsc = (p * (dp - delta)).astype(BF16)
            dqs.append(jnp.dot(dsc, kh, preferred_element_type=F32) * scale)
            dks.append(lax.dot_general(dsc, qh, (((0,), (0,)), ((), ())), preferred_element_type=F32) * scale)
        dq_ref[...] = jnp.concatenate(dqs, axis=1).astype(dq_ref.dtype)
        dkv = jnp.concatenate(dks + dvs, axis=1)

        @pl.when(pl.program_id(0) == 0)
        def _():
            dkv_ref[...] = dkv

        @pl.when(pl.program_id(0) > 0)
        def _():
            dkv_ref[...] += dkv

    row = pl.BlockSpec((tm, xw), lambda i: (i, 0))
    whole = pl.BlockSpec((mt, 2 * xw), lambda i: (0, 0))
    return pl.pallas_call(
        body, name="xattn_bwd", grid=(t // tm,),
        in_specs=[row, whole, row], out_specs=[row, whole],
        out_shape=[_sds((t, xw), BF16), _sds((mt, 2 * xw), F32)],
        compiler_params=_cparams("arbitrary"),
    )(xq, kv, do)


def _ffn_in_swiglu(u, w, dep, name):
    t, d = u.shape
    f = w.shape[1] // 2
    tm, tn, _ = _tiles("nn", t, f)
    nf = f // tn
    deps = [] if dep is None else [dep]

    def body(u_ref, wa_ref, wb_ref, *rest):
        h_ref, o_ref = rest[len(deps):]
        a = jnp.dot(u_ref[...], wa_ref[...], preferred_element_type=F32)
        b = jnp.dot(u_ref[...], wb_ref[...], preferred_element_type=F32)
        h_ref[0] = a.astype(h_ref.dtype)
        h_ref[1] = b.astype(h_ref.dtype)
        o_ref[...] = (a * _sigmoid(a) * b).astype(o_ref.dtype)

    return pl.pallas_call(
        body, name=name, grid=(t // tm, nf),
        in_specs=[pl.BlockSpec((tm, d), lambda i, j: (i, 0)), pl.BlockSpec((d, tn), lambda i, j: (0, j)),
                  pl.BlockSpec((d, tn), lambda i, j: (0, nf + j))] + [ANY] * len(deps),
        out_specs=[pl.BlockSpec((2, tm, tn), lambda i, j: (0, i, j)), pl.BlockSpec((tm, tn), lambda i, j: (i, j))],
        out_shape=[_sds((2, t, f), BF16), _sds((t, f), BF16)],
        compiler_params=_cparams("parallel", "parallel"),
    )(u, w, w, *deps)


def _dact_swiglu(dh, w_out, hid, dep, name):
    t, d = dh.shape
    f = w_out.shape[0]
    tm, tn, _ = _tiles("nt", t, f)
    deps = [] if dep is None else [dep]

    def body(dh_ref, w_ref, h_ref, *rest):
        o_ref = rest[len(deps)]
        g = lax.dot_general(dh_ref[...], w_ref[...], (((1,), (1,)), ((), ())), preferred_element_type=F32)
        a = h_ref[0].astype(F32)
        b = h_ref[1].astype(F32)
        sg = _sigmoid(a)
        o_ref[0] = (g * b * sg * (1.0 + a * (1.0 - sg))).astype(o_ref.dtype)
        o_ref[1] = (g * a * sg).astype(o_ref.dtype)

    pair = pl.BlockSpec((2, tm, tn), lambda i, j: (0, i, j))
    return pl.pallas_call(
        body, name=name, grid=(t // tm, f // tn),
        in_specs=[pl.BlockSpec((tm, d), lambda i, j: (i, 0)), pl.BlockSpec((tn, d), lambda i, j: (j, 0)), pair]
        + [ANY] * len(deps),
        out_specs=pair, out_shape=_sds((2, t, f), BF16),
        compiler_params=_cparams("parallel", "parallel"),
    )(dh, w_out, hid, *deps)


def _adamw(w, g, m, v, name):
    r, c = w.shape
    tr = _pick(r, (256, 128, 64, 32, 16, 8)) if r * c > 65536 else r

    def body(w_ref, g_ref, m_ref, v_ref, d_ref, nm_ref, nv_ref, go_ref):
        gv = g_ref[...]
        go_ref[...] = gv
        m2 = ADAM_B1 * m_ref[...] + (1.0 - ADAM_B1) * gv
        v2 = ADAM_B2 * v_ref[...] + (1.0 - ADAM_B2) * (gv * gv)
        m_hat = m2 / (1.0 - ADAM_B1 ** ADAM_STEP)
        v_hat = v2 / (1.0 - ADAM_B2 ** ADAM_STEP)
        d_ref[...] = -ADAM_LR * (m_hat / (jnp.sqrt(v_hat) + ADAM_EPS) + ADAM_WD * w_ref[...])
        nm_ref[...] = m2
        nv_ref[...] = v2

    blk = pl.BlockSpec((tr, c), lambda i: (i, 0))
    return pl.pallas_call(
        body, name=name, grid=(r // tr,),
        in_specs=[blk] * 4, out_specs=[blk] * 4, out_shape=[_sds((r, c), F32)] * 4,
        compiler_params=_cparams("parallel"),
    )(w, g, m, v)


class _Weights:
    def __init__(self, full):
        self.full = full
        self.grads = {}

    def get(self, name):
        return self.full[name]

    def mark(self, tag, value):
        return value

    def grad(self, name, g):
        self.grads[name] = g

    def dep(self):
        return None


def _local_step(x, mem, tgt, g_mix, sinks, g_xattn, g_mem, g_ffn, g_final, dims, wts):
    t, d = x.shape
    aw, cw, kvw = dims
    cb = 2 * kvw
    zoff = aw + 2 * kvw
    goff = zoff + 3 * cw
    cos, sin = _rope_tables(t)
    mark, get = wts.mark, wts.get

    def mm(a, b, **kw):
        return _matmul(a, b, dep=wts.dep(), **kw)

    u1 = mark("u1", _rms_fwd(x, g_mix, "rms_mix"))
    qkv = mm(u1, get("w_in"), mode="nn", out_dtype=F32, name="mm_qkv", cols=(0, zoff))
    proj = mark("proj", mm(u1, get("w_in"), mode="nn", out_dtype=BF16, name="mm_gates", cols=(zoff, goff - zoff + 2 * d)))
    zoff, goff = 0, goff - zoff
    q_r, k_r, v_h = _rope_qkv(qkv, cos, sin, aw, kvw)
    attn_o = mark("attn_o", _attn_fwd(q_r, k_r, v_h, sinks))
    conv_o = mark("conv_o", _conv_fwd(proj, get("conv_w"), zoff, cw, cb))
    ya = mark("ya", mm(attn_o, get("w_attn_proj"), mode="nn", out_dtype=BF16, name="mm_yattn"))
    yc, merged = _yconv_merge(conv_o, get("w_conv_proj"), ya, proj, goff, cb, wts.dep(), "mm_yconv")
    mark("merged", merged)
    norm_outs = [("row", F32), ("row", BF16)]
    bwd_outs = [("row", F32), ("row", BF16), ("vec",)]
    h1, u2 = _rows_matmul(merged, get("w_mix_out"), mode="nn", kslice=(0, 1), res=x, rows_in=[], vec=g_xattn,
                          epilogue=_ep_norm, outs=norm_outs, name="mm_mix", dep=wts.dep())
    mark("h1", h1)
    mem_n = _rms_fwd(mem, g_mem, "rms_mem")
    xq = mark("xq", mm(u2, get("w_xq"), mode="nn", out_dtype=BF16, name="mm_xq"))
    kv = mm(mem_n, get("w_xkv"), mode="nn", out_dtype=BF16, name="mm_xkv")
    xo = mark("xo", _xattn_fwd(xq, kv))
    h2, u3 = _rows_matmul(xo, get("w_xo"), mode="nn", kslice=(0, 1), res=h1, rows_in=[], vec=g_ffn,
                          epilogue=_ep_norm, outs=norm_outs, name="mm_xo", dep=wts.dep())
    mark("h2", h2)
    mark("u3", u3)
    hid, act = _ffn_in_swiglu(u3, get("w_ffn_in"), wts.dep(), "mm_ffn_in")
    mark("hid", hid)
    dh3, dh3b, dg_final, loss = _rows_matmul(
        act, get("w_ffn_out"), mode="nn", kslice=(0, 1), res=h2, rows_in=[tgt], vec=g_final, epilogue=_ep_loss,
        outs=bwd_outs + [("lane",)], name="mm_ffn_out", dep=wts.dep())
    mark("dh3", dh3b)

    wts.grad("w_ffn_out", mm(act, dh3b, mode="tn", out_dtype=BF16, name="mm_dw_ffn_out"))
    dhid = mark("dhid", _dact_swiglu(dh3b, get("w_ffn_out"), hid, wts.dep(), "mm_dact"))
    wts.grad("w_ffn_in", mm(u3, dhid, mode="tn", out_dtype=BF16, name="mm_dw_ffn_in"))
    du3 = mark("du3", _matmul_slice(dhid, get("w_ffn_in"), mode="nt", out_dtype=F32, name="mm_du3_k0", res=None,
                                    dep=wts.dep(), kslice=(0, 2), cols=(0, d)))
    dh2, dh2b, dg_ffn = _rows_matmul(dhid, get("w_ffn_in"), mode="nt", kslice=(1, 2), res=du3, rows_in=[h2, dh3],
                                     vec=g_ffn, epilogue=_ep_rms_bwd, outs=bwd_outs, name="mm_du3_k1", dep=wts.dep())
    mark("dh2", dh2b)
    wts.grad("w_xo", mm(xo, dh2b, mode="tn", out_dtype=BF16, name="mm_dw_xo"))
    dxo = mm(dh2b, get("w_xo"), mode="nt", out_dtype=BF16, name="mm_dxo")
    dxq, dkv = _xattn_bwd(xq, kv, dxo)
    dkvb = dkv.astype(BF16)
    wts.grad("w_xq", mm(u2, dxq, mode="tn", out_dtype=BF16, name="mm_dw_xq"))
    wts.grad("w_xkv", mm(mem_n, dkvb, mode="tn", out_dtype=BF16, name="mm_dw_xkv"))
    dmem_n = mm(dkvb, get("w_xkv"), mode="nt", out_dtype=F32, name="mm_dmem")
    _, _, dg_mem = _rms_bwd(mem, g_mem, dmem_n, None, "rms_bwd_mem")
    dh1, dh1b, dg_xattn = _rows_matmul(dxq, get("w_xq"), mode="nt", kslice=(0, 1), res=None, rows_in=[h1, dh2],
                                       vec=g_xattn, epilogue=_ep_rms_bwd, outs=bwd_outs, name="mm_du2", dep=wts.dep())
    mark("dh1", dh1b)
    wts.grad("w_mix_out", mm(merged, dh1b, mode="tn", out_dtype=BF16, name="mm_dw_mix"))
    dya, dyc, dga, dgg = _dmerged_split(dh1b, get("w_mix_out"), proj, ya, yc, goff, cb, wts.dep(), "mm_dmerged")
    mark("dya", dya)
    wts.grad("w_attn_proj", mm(attn_o, dya, mode="tn", out_dtype=BF16, name="mm_dw_attn_proj"))
    dattn_o = mm(dya, get("w_attn_proj"), mode="nt", out_dtype=BF16, name="mm_dattn")
    wts.grad("w_conv_proj", mm(conv_o, dyc, mode="tn", out_dtype=BF16, name="mm_dw_conv_proj"))
    dconv_o = mark("dconv_o", mm(dyc, get("w_conv_proj"), mode="nt", out_dtype=BF16, name="mm_dconv"))
    dz, dgb, dgc, dconv_w = _conv_bwd(proj, get("conv_w"), dconv_o, zoff, cw, cb)
    mark("dz", dz)
    dq, dkc, dkp, dvc, dvp, dsinks = _attn_bwd(q_r, k_r, v_h, sinks, attn_o, dattn_o)
    mark("dq", dq)
    dproj = mark("dproj", _assemble_dproj(dq, dkc, dkp, dvc, dvp, cos, sin, dz, dgb, dgc, dga, dgg))
    wts.grad("w_in", mm(u1, dproj, mode="tn", out_dtype=BF16, name="mm_dw_in"))
    du1 = mark("du1", _matmul_slice(dproj, get("w_in"), mode="nt", out_dtype=F32, name="mm_du1_k0", res=None,
                                    dep=wts.dep(), kslice=(0, 2), cols=(0, d)))
    grad_x, _, dg_mix = _rows_matmul(dproj, get("w_in"), mode="nt", kslice=(1, 2), res=du1, rows_in=[x, dh1],
                                     vec=g_mix, epilogue=_ep_rms_bwd, outs=bwd_outs, name="mm_du1_k1", dep=wts.dep())
    mark("grad_x", grad_x)

    small = dict(g_mix=dg_mix, g_xattn=dg_xattn, g_mem=dg_mem, g_ffn=dg_ffn, g_final=dg_final,
                 conv_w=dconv_w[:3], attn_sinks=dsinks[0:1, :sinks.shape[1]], loss=loss[0:1, 0:1])
    return grad_x, small


BIG = (("w_in", 1), ("w_attn_proj", 1), ("w_conv_proj", 1), ("w_mix_out", 0), ("w_xq", 0), ("w_xkv", 0),
       ("w_xo", 1), ("w_ffn_in", 1), ("w_ffn_out", 0))


def _place():
    x, y, c = lax.axis_index("x"), lax.axis_index("y"), lax.axis_index("c")
    chips = [(1 - x, y), (x, 1 - y), (1 - x, 1 - y)]
    return x, y, c, chips


def _window(ref, ax, shard_shape, s, h):
    sr, sc = shard_shape
    hr = sr // 2
    if ax == 1:
        return ref.at[pl.ds(pl.multiple_of(h * hr, 16), hr), pl.ds(pl.multiple_of(s * sc, LANES), sc)]
    return ref.at[pl.ds(pl.multiple_of(s * sr + h * hr, 16), hr), :]


def _half(ref, h):
    hr = ref.shape[0] // 2
    return ref.at[pl.ds(pl.multiple_of(h * hr, 16), hr), :]


def _remote(src, dst, send_sem, recv_sem, dev):
    return pltpu.make_async_remote_copy(src_ref=src, dst_ref=dst, send_sem=send_sem, recv_sem=recv_sem,
                                        device_id=dev, device_id_type=MESH)


def _cast_to_full(shard, ax, me, name, dtype=BF16):
    sr, sc = shard.shape
    tr = _pick(sr, (256, 352, 128, 64, 32, 16))
    nr = sr // tr
    full = (sr * N_CHIPS, sc) if ax == 0 else (sr, sc * N_CHIPS)

    def body(me_ref, s_ref, o_ref):
        o_ref[...] = s_ref[...].astype(o_ref.dtype)

    if ax == 1:
        out_spec = pl.BlockSpec((tr, sc), lambda r, me_ref: (r, me_ref[0]))
    else:
        out_spec = pl.BlockSpec((tr, sc), lambda r, me_ref: (me_ref[0] * nr + r, 0))
    return pl.pallas_call(
        body, name=name,
        grid_spec=pltpu.PrefetchScalarGridSpec(
            num_scalar_prefetch=1, grid=(nr,), in_specs=[pl.BlockSpec((tr, sc), lambda r, me_ref: (r, 0))],
            out_specs=out_spec),
        out_shape=_sds(full, dtype),
        compiler_params=_cparams("parallel"),
    )(me, shard)


HBM = pl.BlockSpec(memory_space=pltpu.HBM)
SEM = pl.BlockSpec(memory_space=pltpu.SEMAPHORE)
EFFECT = pltpu.SideEffectType.DATAFLOW_SIDE_EFFECTING


def _in_hbm(a):
    return pltpu.with_memory_space_constraint(a, pltpu.HBM)


def _gather_window(ref, ax, shard_shape, s, h):
    if h is not None:
        return _window(ref, ax, shard_shape, s, h)
    sr, sc = shard_shape
    if ax == 1:
        return ref.at[:, pl.ds(pl.multiple_of(s * sc, LANES), sc)]
    return ref.at[pl.ds(pl.multiple_of(s * sr, 8), sr), :]


def _ag_start(fulls, axes, shard_shapes, whole):
    n = len(fulls)

    def body(*refs):
        src = refs[:n]
        send_sems, recv_sems = refs[n], refs[n + 1]
        token = refs[2 * n + 2]
        x, y, c, chips = _place()
        me = 2 * x + y
        for i in range(n):
            h = None if whole[i] else c
            for j, chip in enumerate(chips):
                blk = _gather_window(src[i], axes[i], shard_shapes[i], me, h)
                _remote(blk, blk, send_sems.at[3 * i + j], recv_sems.at[3 * i + j], (*chip, c)).start()
        token[...] = jnp.zeros_like(token)

    res = pl.pallas_call(
        body, name="ag_start_" + str(n),
        out_shape=(pltpu.SemaphoreType.DMA((3 * n,)), pltpu.SemaphoreType.DMA((3 * n,)),
                   *[pltpu.HBM(f.shape, f.dtype) for f in fulls], _sds((8, LANES), F32)),
        in_specs=[HBM] * n, out_specs=(SEM, SEM, *[HBM] * n, pl.BlockSpec(memory_space=pltpu.VMEM)),
        input_output_aliases={i: 2 + i for i in range(n)},
        compiler_params=pltpu.CompilerParams(has_side_effects=EFFECT),
    )(*[_in_hbm(f) for f in fulls])
    return res[0], res[1], list(res[2:2 + n]), res[2 + n]


def _ag_mid(bufs, slots, axes, shard_shapes, whole, send_sems, recv_sems, after, name):
    ng = len(bufs)

    def body(*refs):
        src = refs[:ng]
        s_in, r_in = refs[ng], refs[ng + 1]
        fsend, frecv = refs[ng + 3], refs[ng + 4]
        x, y, c, chips = _place()
        me = 2 * x + y
        sib = (x, y, 1 - c)
        for k, i in enumerate(slots):
            h = None if whole[k] else c
            for j, chip in enumerate(chips):
                cj = 2 * chip[0] + chip[1]
                mine = _gather_window(src[k], axes[k], shard_shapes[k], me, h)
                theirs = _gather_window(src[k], axes[k], shard_shapes[k], cj, h)
                _remote(theirs, theirs, s_in.at[3 * i + j], r_in.at[3 * i + j], (*chip, c)).wait_recv()
                _remote(mine, mine, s_in.at[3 * i + j], r_in.at[3 * i + j], (*chip, c)).wait_send()
                if not whole[k]:
                    _remote(theirs, theirs, fsend.at[3 * k + j], frecv.at[3 * k + j], sib).start()
        token = refs[2 * ng + 5]
        token[...] = jnp.zeros_like(token)

    res = pl.pallas_call(
        body, name=name,
        out_shape=(pltpu.SemaphoreType.DMA((3 * ng,)), pltpu.SemaphoreType.DMA((3 * ng,)),
                   *[pltpu.HBM(b.shape, b.dtype) for b in bufs], _sds((8, LANES), F32)),
        in_specs=[HBM] * ng + [SEM, SEM, ANY],
        out_specs=(SEM, SEM, *[HBM] * ng, pl.BlockSpec(memory_space=pltpu.VMEM)),
        input_output_aliases={k: 2 + k for k in range(ng)},
        compiler_params=pltpu.CompilerParams(has_side_effects=EFFECT),
    )(*bufs, send_sems, recv_sems, after)
    return res[0], res[1], list(res[2:2 + ng]), res[2 + ng]


def _ag_wait(bufs, axes, shard_shapes, whole, fsend, frecv, after, name):
    ng = len(bufs)

    def body(*refs):
        src = refs[:ng]
        s_in, r_in = refs[ng], refs[ng + 1]
        x, y, c, chips = _place()
        sib = (x, y, 1 - c)
        for k in range(ng):
            if whole[k]:
                continue
            for j, chip in enumerate(chips):
                cj = 2 * chip[0] + chip[1]
                sent = _gather_window(src[k], axes[k], shard_shapes[k], cj, c)
                landed = _gather_window(src[k], axes[k], shard_shapes[k], cj, 1 - c)
                _remote(landed, landed, s_in.at[3 * k + j], r_in.at[3 * k + j], sib).wait_recv()
                _remote(sent, sent, s_in.at[3 * k + j], r_in.at[3 * k + j], sib).wait_send()

    res = pl.pallas_call(
        body, name=name,
        out_shape=tuple(pltpu.HBM(b.shape, b.dtype) for b in bufs),
        in_specs=[HBM] * ng + [SEM, SEM, ANY], out_specs=tuple([HBM] * ng),
        input_output_aliases={k: k for k in range(ng)},
        compiler_params=pltpu.CompilerParams(has_side_effects=EFFECT),
    )(*bufs, fsend, frecv, after)
    return list(res)


class _Schedule:
    GROUPS = ((("w_in", "conv_w"), "u1", "u1"),
              (("w_attn_proj", "w_conv_proj", "w_mix_out", "w_xq", "w_xkv", "w_xo"), "proj", "conv_o"),
              (("w_ffn_in",), "h1", "h2"),
              (("w_ffn_out",), "u3", "hid"))
    STARTS = ((0,), (1, 2, 3))
    REDUCE = ((("w_ffn_out",), "dhid", "grad:w_ffn_in", "grad:w_ffn_in"),
              (("w_ffn_in",), "grad:w_ffn_in", "dya", "grad_x"),
              (("w_xo", "w_xq", "w_xkv", "w_mix_out", "w_attn_proj", "w_conv_proj"), "dconv_o", "dq", "grad:w_in"),
              (("w_in",), "grad:w_in", "end", "end2"))

    def __init__(self, seed, axes, shard_shapes, place, on_ready):
        self.ax, self.shape, self.place, self.on_ready = axes, shard_shapes, place, on_ready
        self.stage, self.buf, self.slot, self.passes = {}, {}, {}, {}
        self.ready = set()
        self.grads = {}
        token = None
        for groups in self.STARTS:
            order = [nm for g in groups for nm in self.GROUPS[g][0]]
            send, recv, bufs, token = _ag_start([seed(nm, token) for nm in order], *self._meta(order))
            self.buf.update(zip(order, bufs))
            self.slot.update({nm: (send, recv, k) for k, nm in enumerate(order)})
        self.token = self.latest = token
        self.mark("start", token)

    def _meta(self, names):
        return ([self.ax[nm] for nm in names], [self.shape[nm] for nm in names], [nm == "conv_w" for nm in names])

    def mark(self, tag, value):
        for g, (names, mid, wait) in enumerate(self.GROUPS):
            if tag == mid:
                send, recv, _ = self.slot[names[0]]
                fs, fr, bufs, self.latest = _ag_mid([self.buf[nm] for nm in names], [self.slot[nm][2] for nm in names],
                                                    *self._meta(names), send, recv, value, "ag_mid_%d" % g)
                self.buf.update(zip(names, bufs))
                self.passes[g] = (fs, fr)
            if tag == wait:
                fs, fr = self.passes[g]
                bufs = _ag_wait([self.buf[nm] for nm in names], *self._meta(names), fs, fr, value, "ag_wait_%d" % g)
                self.buf.update(zip(names, bufs))
                self.ready.update(names)
        for g, (names, send, total, finish) in enumerate(self.REDUCE):
            st = self.stage.get(g)
            if st is None:
                continue
            ng = len(names)
            if tag == send and st["at"] == "pair":
                arrs = _exchange_wait("rs_pair_wait_%d" % g, st["arrs"], *st["sems"], st["plan"], value)
                parts = [_pair_add(arrs[k], arrs[ng + k], self.ax[nm], self.shape[nm], self.place, "pair_add_" + nm)
                         for k, nm in enumerate(names)]
                plan, nsem = _plan_chip(ng)
                ss, rs, arrs, self.latest = _exchange_start(
                    "rs_chip_start_%d" % g, parts + [lax.empty(p.shape, p.dtype) for p in parts], nsem, plan)
                self.stage[g] = dict(at="chip", arrs=arrs, sems=(ss, rs), plan=plan)
            elif tag == total and st["at"] == "chip":
                arrs = _exchange_wait("rs_chip_wait_%d" % g, st["arrs"], *st["sems"], st["plan"], value)
                halves = [_chip_add(arrs[k], arrs[ng + k], self.place, "chip_add_" + nm) for k, nm in enumerate(names)]
                plan, nsem = _plan_gather(ng)
                ss, rs, arrs, self.latest = _exchange_start("rs_gather_start_%d" % g, halves, nsem, plan)
                self.stage[g] = dict(at="gather", arrs=arrs, sems=(ss, rs), plan=plan)
            st = self.stage[g]
            if tag == finish and st["at"] == "gather":
                arrs = _exchange_wait("rs_gather_wait_%d" % g, st["arrs"], *st["sems"], st["plan"], value)
                self.stage[g] = dict(at="done")
                for nm, shard in zip(names, arrs):
                    self.on_ready(nm, shard)
        return value

    def get(self, name):
        assert name in self.ready, name
        return self.buf[name]

    def grad(self, name, g):
        self.grads[name] = g
        for gi, (names, _, _, _) in enumerate(self.REDUCE):
            if name == names[-1]:
                gs = [self.grads[nm] for nm in names]
                plan, nsem = _plan_pair(len(names), [self.ax[nm] for nm in names], [self.shape[nm] for nm in names])
                ss, rs, arrs, self.latest = _exchange_start(
                    "rs_pair_start_%d" % gi, gs + [lax.empty(a.shape, a.dtype) for a in gs], nsem, plan)
                self.stage[gi] = dict(at="pair", arrs=arrs, sems=(ss, rs), plan=plan)
                g = self.latest
        self.mark("grad:" + name, g)

    def dep(self):
        return self.latest


def _exchange_start(name, arrays, nsem, plan):
    n = len(arrays)

    def body(*refs):
        send_sems, recv_sems, token = refs[n], refs[n + 1], refs[2 * n + 2]
        sends, _ = plan(refs[:n])
        for k, (src, dst, dev) in enumerate(sends):
            _remote(src, dst, send_sems.at[k], recv_sems.at[k], dev).start()
        token[...] = jnp.zeros_like(token)

    res = pl.pallas_call(
        body, name=name,
        out_shape=(pltpu.SemaphoreType.DMA((nsem,)), pltpu.SemaphoreType.DMA((nsem,)),
                   *[pltpu.HBM(a.shape, a.dtype) for a in arrays], _sds((8, LANES), F32)),
        in_specs=[HBM] * n, out_specs=(SEM, SEM, *[HBM] * n, pl.BlockSpec(memory_space=pltpu.VMEM)),
        input_output_aliases={i: 2 + i for i in range(n)},
        compiler_params=pltpu.CompilerParams(has_side_effects=EFFECT),
    )(*[_in_hbm(a) for a in arrays])
    return res[0], res[1], list(res[2:2 + n]), res[2 + n]


def _exchange_wait(name, arrays, send_sems, recv_sems, plan, after):
    n = len(arrays)

    def body(*refs):
        s_in, r_in = refs[n], refs[n + 1]
        sends, recvs = plan(refs[:n])
        for k, land in enumerate(recvs):
            _remote(land, land, s_in.at[k], r_in.at[k], sends[k][2]).wait_recv()
        for k, (src, _, dev) in enumerate(sends):
            _remote(src, src, s_in.at[k], r_in.at[k], dev).wait_send()

    res = pl.pallas_call(
        body, name=name,
        out_shape=tuple(pltpu.HBM(a.shape, a.dtype) for a in arrays),
        in_specs=[HBM] * n + [SEM, SEM, ANY], out_specs=tuple([HBM] * n),
        input_output_aliases={i: i for i in range(n)},
        compiler_params=pltpu.CompilerParams(has_side_effects=EFFECT),
    )(*arrays, send_sems, recv_sems, after)
    return list(res)


def _plan_pair(n, axes, shard_shapes):
    def plan(refs):
        g, ra = refs[:n], refs[n:]
        x, y, c, _ = _place()
        sib = (x, y, 1 - c)

        def pieces(ref, i, h):
            if axes[i] == 1:
                return [_half(ref, h)]
            return [_window(ref, 0, shard_shapes[i], s, h) for s in range(N_CHIPS)]

        sends, recvs = [], []
        for i in range(n):
            sends += [(src, dst, sib) for src, dst in zip(pieces(g[i], i, 1 - c), pieces(ra[i], i, 1 - c))]
            recvs += pieces(ra[i], i, c)
        return sends, recvs

    return plan, sum(1 if ax == 1 else N_CHIPS for ax in axes)


def _plan_chip(n):
    def plan(refs):
        p, rc = refs[:n], refs[n:]
        x, y, c, chips = _place()
        me = 2 * x + y
        sends, recvs = [], []
        for i in range(n):
            for chip in chips:
                cj = 2 * chip[0] + chip[1]
                sends.append((p[i].at[cj], rc[i].at[me], (*chip, c)))
                recvs.append(rc[i].at[cj])
        return sends, recvs

    return plan, 3 * n


def _plan_gather(n):
    def plan(refs):
        x, y, c, _ = _place()
        sib = (x, y, 1 - c)
        return ([(_half(r, c), _half(r, c), sib) for r in refs], [_half(r, 1 - c) for r in refs])

    return plan, n


def _pair_add(g, ra, ax, shard_shape, place, name):
    sr, sc = shard_shape
    hr = sr // 2
    wc = sc
    tr = _pick(hr, (256, 352, 128, 64, 32, 16))
    nr = hr // tr

    def body(p_ref, a_ref, b_ref, o_ref):
        o_ref[...] = (a_ref[...].astype(F32) + b_ref[...].astype(F32)).astype(o_ref.dtype)

    if ax == 1:
        src = pl.BlockSpec((tr, wc), lambda s, r, p_ref: (p_ref[0] * nr + r, s))
    else:
        src = pl.BlockSpec((tr, wc), lambda s, r, p_ref: (s * 2 * nr + p_ref[0] * nr + r, 0))
    return pl.pallas_call(
        body, name=name,
        grid_spec=pltpu.PrefetchScalarGridSpec(
            num_scalar_prefetch=1, grid=(N_CHIPS, nr), in_specs=[src, src],
            out_specs=pl.BlockSpec((None, tr, wc), lambda s, r, p_ref: (s, r, 0))),
        out_shape=_sds((N_CHIPS, hr, wc), BF16),
        compiler_params=_cparams("parallel", "parallel"),
    )(place, g, ra)


def _chip_add(part, rc, place, name):
    _, hr, wc = rc.shape
    tr = _pick(hr, (256, 352, 128, 64, 32, 16))
    nr = hr // tr

    def body(p_ref, own_ref, r1_ref, r2_ref, r3_ref, o_ref):
        acc = own_ref[...].astype(F32)
        for r_ref in (r1_ref, r2_ref, r3_ref):
            acc = acc + r_ref[...].astype(F32)
        o_ref[...] = acc

    def slot(k):
        return pl.BlockSpec((None, tr, wc), lambda r, p_ref: ((p_ref[1] + k) % N_CHIPS, r, 0))

    return pl.pallas_call(
        body, name=name,
        grid_spec=pltpu.PrefetchScalarGridSpec(
            num_scalar_prefetch=1, grid=(nr,), in_specs=[slot(0), slot(1), slot(2), slot(3)],
            out_specs=pl.BlockSpec((tr, wc), lambda r, p_ref: (p_ref[0] * nr + r, 0))),
        out_shape=_sds((2 * hr, wc), F32),
        compiler_params=_cparams("parallel"),
    )(place, part, rc, rc, rc)


N_DEV = 8


def _all_reduce_small(buf):
    r, cdim = buf.shape

    def body(x_ref, o_ref, land, send_sems, recv_sems):
        x, y, c, _ = _place()
        me = 4 * x + 2 * y + c
        land[me] = x_ref[...]
        sends = []
        for k in range(1, N_DEV):
            kx, ky, kc = (k >> 2) & 1, (k >> 1) & 1, k & 1
            peer = (1 - x if kx else x, 1 - y if ky else y, 1 - c if kc else c)
            cp = _remote(x_ref, land.at[me], send_sems.at[k - 1], recv_sems.at[k - 1], peer)
            cp.start()
            sends.append(cp)
        for k in range(1, N_DEV):
            kx, ky, kc = (k >> 2) & 1, (k >> 1) & 1, k & 1
            peer = (1 - x if kx else x, 1 - y if ky else y, 1 - c if kc else c)
            pidx = 4 * peer[0] + 2 * peer[1] + peer[2]
            _remote(land.at[pidx], land.at[pidx], send_sems.at[k - 1], recv_sems.at[k - 1], peer).wait_recv()
        for cp in sends:
            cp.wait_send()
        acc = land[0]
        for dev in range(1, N_DEV):
            acc = acc + land[dev]
        o_ref[...] = acc

    vm = pl.BlockSpec(memory_space=pltpu.VMEM)
    return pl.pallas_call(
        body, name="all_reduce_small", in_specs=[vm], out_specs=vm, out_shape=_sds((r, cdim), F32),
        scratch_shapes=[pltpu.VMEM((N_DEV, r, cdim), F32), pltpu.SemaphoreType.DMA((N_DEV - 1,)),
                        pltpu.SemaphoreType.DMA((N_DEV - 1,))],
    )(buf)


SMALL_ROWS = 16


def kernel(x, mem, g_mix, w_in, conv_w, attn_sinks, w_attn_proj, w_conv_proj, w_mix_out, g_xattn, g_mem, w_xq, w_xkv, w_xo, g_ffn, w_ffn_in, w_ffn_out, g_final, loss_target, m_g_mix, m_w_in, m_conv_w, m_attn_sinks, m_w_attn_proj, m_w_conv_proj, m_w_mix_out, m_g_xattn, m_g_mem, m_w_xq, m_w_xkv, m_w_xo, m_g_ffn, m_w_ffn_in, m_w_ffn_out, m_g_final, v_g_mix, v_w_in, v_conv_w, v_attn_sinks, v_w_attn_proj, v_w_conv_proj, v_w_mix_out, v_g_xattn, v_g_mem, v_w_xq, v_w_xkv, v_w_xo, v_g_ffn, v_w_ffn_in, v_w_ffn_out, v_g_final):
    w = dict(g_mix=g_mix, w_in=w_in[0], conv_w=conv_w[0], attn_sinks=attn_sinks, w_attn_proj=w_attn_proj[0],
             w_conv_proj=w_conv_proj[0], w_mix_out=w_mix_out[0], g_xattn=g_xattn, g_mem=g_mem, w_xq=w_xq[0],
             w_xkv=w_xkv[0], w_xo=w_xo[0], g_ffn=g_ffn, w_ffn_in=w_ffn_in[0], w_ffn_out=w_ffn_out[0],
             g_final=g_final[None])
    m = dict(g_mix=m_g_mix, w_in=m_w_in[0], conv_w=m_conv_w[0], attn_sinks=m_attn_sinks,
             w_attn_proj=m_w_attn_proj[0], w_conv_proj=m_w_conv_proj[0], w_mix_out=m_w_mix_out[0],
             g_xattn=m_g_xattn, g_mem=m_g_mem, w_xq=m_w_xq[0], w_xkv=m_w_xkv[0], w_xo=m_w_xo[0], g_ffn=m_g_ffn,
             w_ffn_in=m_w_ffn_in[0], w_ffn_out=m_w_ffn_out[0], g_final=m_g_final[None])
    v = dict(g_mix=v_g_mix, w_in=v_w_in[0], conv_w=v_conv_w[0], attn_sinks=v_attn_sinks,
             w_attn_proj=v_w_attn_proj[0], w_conv_proj=v_w_conv_proj[0], w_mix_out=v_w_mix_out[0],
             g_xattn=v_g_xattn, g_mem=v_g_mem, w_xq=v_w_xq[0], w_xkv=v_w_xkv[0], w_xo=v_w_xo[0], g_ffn=v_g_ffn,
             w_ffn_in=v_w_ffn_in[0], w_ffn_out=v_w_ffn_out[0], g_final=v_g_final[None])
    names = [nm for nm, _ in BIG]
    axes = [ax for _, ax in BIG]
    d = x.shape[2]
    cw = w["conv_w"].shape[1] * N_CHIPS
    chip = (2 * lax.axis_index("x") + lax.axis_index("y")).astype(jnp.int32)
    place = jnp.stack([lax.axis_index("c").astype(jnp.int32), chip])
    shard_shapes = [w[nm].shape for nm in names]

    def seed(nm, token):
        me1 = chip.reshape(1)
        if token is not None:
            me1 = me1 + token[0, 0].astype(jnp.int32)
        if nm == "conv_w":
            return _cast_to_full(w[nm], 1, me1, "place_conv_w", F32)
        return _cast_to_full(w[nm], dict(BIG)[nm], me1, "cast_" + nm)

    upd = {}

    def on_ready(nm, shard):
        upd[nm] = _adamw(w[nm], shard, m[nm], v[nm], "adamw_" + nm)
        grads[nm] = upd[nm][3]

    grads = {}
    wts = _Schedule(seed, dict(zip(names + ["conv_w"], axes + [1])),
                    dict(zip(names + ["conv_w"], shard_shapes + [w["conv_w"].shape])), place, on_ready)
    aw, cw = w["w_attn_proj"].shape[0], w["w_conv_proj"].shape[0]
    kvw = (w["w_in"].shape[1] * N_CHIPS - aw - 3 * cw - 2 * d) // 2
    grad_x, small = _local_step(
        x[0], mem[0], loss_target[0], w["g_mix"] + wts.token[0:1, 0:1], w["attn_sinks"], w["g_xattn"], w["g_mem"],
        w["g_ffn"], w["g_final"], (aw, cw, kvw), wts)

    pw = max(d, cw)

    def row(a):
        return jnp.pad(a, ((0, 0), (0, pw - a.shape[1])))

    gains = ("g_mix", "g_xattn", "g_mem", "g_ffn", "g_final")
    packed = jnp.concatenate(
        [row(small[nm]) for nm in gains] + [row(small["conv_w"]),
         row(jnp.concatenate([small["attn_sinks"], small["loss"]], axis=1)),
         jnp.zeros((SMALL_ROWS - 9, pw), F32)], axis=0)
    total = _all_reduce_small(packed)
    wts.mark("end", total)
    nsink = attn_sinks.shape[1]
    grads.update({nm: total[k:k + 1, :d] for k, nm in enumerate(gains)})
    grads.update(conv_w=lax.dynamic_slice(total, (5, chip * (cw // N_CHIPS)), (3, cw // N_CHIPS)),
                 attn_sinks=total[8:9, :nsink])
    loss = total[8, nsink]
    for nm in gains + ("conv_w", "attn_sinks"):
        upd[nm] = _adamw(w[nm], grads[nm], m[nm], v[nm], "adamw_" + nm)
    wts.mark("end2", upd["g_final"][0])

    order = ["g_mix", "w_in", "conv_w", "attn_sinks", "w_attn_proj", "w_conv_proj", "w_mix_out", "g_xattn", "g_mem",
             "w_xq", "w_xkv", "w_xo", "g_ffn", "w_ffn_in", "w_ffn_out", "g_final"]

    stacked = set(names) | {"conv_w"}

    def shaped(nm, a):
        if nm == "g_final":
            return a[0]
        return a[None] if nm in stacked else a

    outs = [loss, grad_x[None]]
    outs += [shaped(nm, grads[nm]) for nm in order]
    for k in range(3):
        outs += [shaped(nm, upd[nm][k]) for nm in order]
    return tuple(outs)
```

```python
import functools

import jax
import jax.numpy as jnp
from jax import lax
from jax.experimental import pallas as pl
from jax.experimental.pallas import tpu as pltpu

F32 = jnp.float32
BF16 = jnp.bfloat16

VMEM_LIMIT_BYTES = 56 * 1024 * 1024
LANES = 128
HEAD_DIM = 64
BLOCK = 128
X_HEAD_DIM = 128
ROPE_THETA = 10000.0
EPS = 1e-6
NEG = -1e30
ADAM_LR, ADAM_B1, ADAM_B2, ADAM_EPS, ADAM_WD, ADAM_STEP = 0.001, 0.9, 0.999, 1e-08, 0.01, 10
N_CHIPS = 4
MESH = pl.DeviceIdType.MESH
ANY = pl.BlockSpec(memory_space=pl.ANY)


def _pick(dim, prefs):
    for p in prefs:
        if dim % p == 0:
            return p
    return dim


def _cparams(*sem):
    return pltpu.CompilerParams(dimension_semantics=sem, vmem_limit_bytes=VMEM_LIMIT_BYTES)


def _sds(shape, dtype):
    return jax.ShapeDtypeStruct(shape, dtype)


def _sigmoid(v):
    return 0.5 * jnp.tanh(0.5 * v) + 0.5


MATMUL_VMEM_BUDGET = 46 * 1024 * 1024


def _tiles(mode, m, n):
    if mode == "tn" and m % 1024 != 0:
        return _pick(m, (512, 256, 128)), _pick(n, (1024, 512, 256, 128)), True
    return _pick(m, (1024, 512, 256, 128)), _pick(n, (512, 256, 128)), False


def _k_parts(m, n, k):
    tm, tn = _pick(m, (1024, 512, 256, 128)), _pick(n, (512, 256, 128))
    for parts in range(1, k // LANES + 1):
        if k % (parts * LANES) == 0 and 4 * (tm + tn) * (k // parts) + 16 * tm * tn <= MATMUL_VMEM_BUDGET:
            return parts
    return k // LANES


def _matmul(a, b, *, mode, out_dtype, name, res=None, dep=None, cols=None):
    if mode == "nn":
        (m, k), (k2, n) = a.shape, b.shape
    elif mode == "nt":
        (m, k), (n, k2) = (a.shape[-2], a.shape[-1] * (a.shape[0] if a.ndim == 3 else 1)), b.shape
    else:
        (k, m), (k2, n) = a.shape, (b.shape[-2], b.shape[-1] * (b.shape[0] if b.ndim == 3 else 1))
    assert k == k2, (a.shape, b.shape, mode)
    assert cols is None or mode != "nt"
    cols = cols or (0, n)
    parts = 1 if mode == "tn" else (a.shape[0] if a.ndim == 3 else _k_parts(m, cols[1], k))
    for p in range(parts):
        last = p == parts - 1
        res = _matmul_slice(a, b, mode=mode, out_dtype=out_dtype if last else F32, res=res, dep=dep, cols=cols,
                            kslice=(p, parts), name=name + ("_k%d" % p if parts > 1 else ""))
    return res


def _matmul_slice(a, b, *, mode, out_dtype, name, res, dep, kslice, cols):
    part, parts = kslice
    (m, k) = a.shape[-2:] if mode != "tn" else a.shape[::-1]
    col0, n = cols
    tk = k if a.ndim == 3 else k // parts
    tm, tn, swap = _tiles(mode, m, b.shape[2] if (mode == "tn" and b.ndim == 3) else n)
    while col0 % tn:
        tn //= 2
    joff = col0 // tn
    dims = {"nn": (((1,), (0,)), ((), ())), "nt": (((1,), (1,)), ((), ())), "tn": (((0,), (0,)), ((), ()))}[mode]
    has_res = res is not None
    has_dep = dep is not None

    def body(*refs):
        a_ref, b_ref = refs[0], refs[1]
        o_ref = refs[2 + has_res + has_dep]
        val = lax.dot_general(a_ref[...], b_ref[...], dims, preferred_element_type=F32)
        if has_res:
            val = val + refs[2][...]
        o_ref[...] = val.astype(o_ref.dtype)

    def spec(shape, f):
        if swap:
            return pl.BlockSpec(shape, lambda j, i: f(i, j))
        return pl.BlockSpec(shape, f)

    if mode == "tn":
        a_spec = spec((tk, tm), lambda i, j: (part, i))
    elif a.ndim == 3:
        a_spec = spec((None, tm, tk), lambda i, j: (part, i, 0))
    else:
        a_spec = spec((tm, tk), lambda i, j: (i, part))
    if mode == "nt":
        b_spec = spec((tn, tk), lambda i, j: (j, part))
    elif b.ndim == 3:
        per = b.shape[2] // tn
        b_spec = spec((None, tk, tn), lambda i, j: (j // per, part, j % per))
    else:
        b_spec = spec((tk, tn), lambda i, j: (part, joff + j))
    o_spec = spec((tm, tn), lambda i, j: (i, j))
    return pl.pallas_call(
        body,
        name=name,
        grid=(n // tn, m // tm) if swap else (m // tm, n // tn),
        in_specs=[a_spec, b_spec] + ([o_spec] if has_res else []) + ([ANY] if has_dep else []),
        out_specs=o_spec,
        out_shape=_sds((m, n), out_dtype),
        compiler_params=_cparams("parallel", "parallel"),
    )(*([a, b] + ([res] if has_res else []) + ([dep] if has_dep else [])))


def _rms_fwd(x, g, name):
    t, d = x.shape
    tm = _pick(t, (512, 256, 128))

    def body(x_ref, g_ref, o_ref):
        xf = x_ref[...]
        r = lax.rsqrt(jnp.mean(xf * xf, axis=-1, keepdims=True) + EPS)
        o_ref[...] = (xf * r * g_ref[...]).astype(o_ref.dtype)

    row = pl.BlockSpec((tm, d), lambda i: (i, 0))
    return pl.pallas_call(
        body, name=name, grid=(t // tm,),
        in_specs=[row, pl.BlockSpec((1, d), lambda i: (0, 0))],
        out_specs=row, out_shape=_sds((t, d), BF16),
        compiler_params=_cparams("parallel"),
    )(x, g)


def _rms_bwd_math(xf, g, du):
    r = lax.rsqrt(jnp.mean(xf * xf, axis=-1, keepdims=True) + EPS)
    xh = xf * r
    gdy = g * du
    dx = r * (gdy - xh * jnp.mean(gdy * xh, axis=-1, keepdims=True))
    dg = jnp.sum(du * xh, axis=0, keepdims=True)
    return dx, dg


def _rms_bwd(x, g, du, dh, name):
    t, d = x.shape
    tm = _pick(t, (256, 128))
    has_dh = dh is not None

    def body(*refs):
        x_ref, g_ref, du_ref = refs[0], refs[1], refs[2]
        o_ref, ob_ref, dg_ref = refs[3 + has_dh:]
        dx, dg = _rms_bwd_math(x_ref[...], g_ref[...], du_ref[...].astype(F32))
        if has_dh:
            dx = dx + refs[3][...]
        o_ref[...] = dx
        ob_ref[...] = dx.astype(BF16)

        @pl.when(pl.program_id(0) == 0)
        def _():
            dg_ref[...] = dg

        @pl.when(pl.program_id(0) > 0)
        def _():
            dg_ref[...] += dg

    row = pl.BlockSpec((tm, d), lambda i: (i, 0))
    vec = pl.BlockSpec((1, d), lambda i: (0, 0))
    return pl.pallas_call(
        body, name=name, grid=(t // tm,),
        in_specs=[row, vec, row] + ([row] if has_dh else []),
        out_specs=[row, row, vec],
        out_shape=[_sds((t, d), F32), _sds((t, d), BF16), _sds((1, d), F32)],
        compiler_params=_cparams("arbitrary"),
    )(*([x, g, du] + ([dh] if has_dh else [])))


ROW_TILE = 256


def _rows_matmul(a, b, *, mode, kslice, res, rows_in, vec, epilogue, outs, name, dep):
    part, parts = kslice
    t = a.shape[-2]
    n = b.shape[1] if mode == "nn" else b.shape[0]
    kc = a.shape[-1] if a.ndim == 3 else a.shape[1] // parts
    tm = _pick(t, (ROW_TILE, 128))
    dims = (((1,), (0,)), ((), ())) if mode == "nn" else (((1,), (1,)), ((), ()))
    has_res = res is not None
    deps = [] if dep is None else [dep]
    n_in = 2 + has_res + len(rows_in) + 1 + len(deps)

    def body(*refs):
        val = lax.dot_general(refs[0][...], refs[1][...], dims, preferred_element_type=F32)
        if has_res:
            val = val + refs[2][...]
        row_refs = refs[2 + has_res:2 + has_res + len(rows_in)]
        epilogue(val, row_refs, refs[2 + has_res + len(rows_in)], refs[n_in:], pl.program_id(0) == 0)

    if a.ndim == 3:
        a_spec = pl.BlockSpec((None, tm, kc), lambda i: (part, i, 0))
    else:
        a_spec = pl.BlockSpec((tm, kc), lambda i: (i, part))
    if mode == "nn":
        b_spec = pl.BlockSpec((kc, n), lambda i: (part, 0), pipeline_mode=pl.Buffered(1))
    else:
        b_spec = pl.BlockSpec((n, kc), lambda i: (0, part), pipeline_mode=pl.Buffered(1))
    row = pl.BlockSpec((tm, n), lambda i: (i, 0))
    kinds = {"row": row, "vec": pl.BlockSpec((1, n), lambda i: (0, 0)), "lane": pl.BlockSpec((1, LANES), lambda i: (0, 0))}
    shapes = {"row": (t, n), "vec": (1, n), "lane": (1, LANES)}
    return pl.pallas_call(
        body, name=name, grid=(t // tm,),
        in_specs=[a_spec, b_spec] + [row] * (has_res + len(rows_in)) + [kinds["vec"]] + [ANY] * len(deps),
        out_specs=[kinds[o[0]] for o in outs],
        out_shape=[_sds(shapes[o[0]], o[1] if len(o) > 1 else F32) for o in outs],
        compiler_params=_cparams("arbitrary"),
    )(*([a, b] + ([res] if has_res else []) + list(rows_in) + [vec] + deps))


def _accumulate(ref, value, first):
    @pl.when(first)
    def _():
        ref[...] = value

    @pl.when(jnp.logical_not(first))
    def _():
        ref[...] += value


def _ep_norm(val, rows, g_ref, outs, first):
    h_ref, u_ref = outs
    h_ref[...] = val
    r = lax.rsqrt(jnp.mean(val * val, axis=-1, keepdims=True) + EPS)
    u_ref[...] = (val * r * g_ref[...]).astype(u_ref.dtype)


def _ep_loss(val, rows, g_ref, outs, first):
    o_ref, ob_ref, dg_ref, l_ref = outs
    gv = g_ref[...]
    r = lax.rsqrt(jnp.mean(val * val, axis=-1, keepdims=True) + EPS)
    err = val * r * gv - rows[0][...]
    part = 0.5 * jnp.sum(jnp.mean(err * err, axis=-1, keepdims=True), axis=0, keepdims=True)
    dx, dg = _rms_bwd_math(val, gv, err * (1.0 / val.shape[1]))
    o_ref[...] = dx
    ob_ref[...] = dx.astype(BF16)
    _accumulate(dg_ref, dg, first)
    _accumulate(l_ref, jnp.broadcast_to(part, (1, LANES)), first)


def _ep_rms_bwd(val, rows, g_ref, outs, first):
    o_ref, ob_ref, dg_ref = outs
    dx, dg = _rms_bwd_math(rows[0][...], g_ref[...], val)
    dx = dx + rows[1][...]
    o_ref[...] = dx
    ob_ref[...] = dx.astype(BF16)
    _accumulate(dg_ref, dg, first)


def _rope_tables(t):
    half = HEAD_DIM // 2
    inv_freq = ROPE_THETA ** (-jnp.arange(half, dtype=F32) / half)
    ang = jnp.arange(t, dtype=F32)[:, None] * inv_freq[None, :]
    cos = jnp.cos(ang)
    sin = jnp.sin(ang)
    reps = LANES // HEAD_DIM
    cos_t = jnp.tile(jnp.concatenate([cos, cos], axis=1), (1, reps))
    sin_t = jnp.tile(jnp.concatenate([-sin, sin], axis=1), (1, reps))
    return cos_t, sin_t


def _rope(v, cos, sin):
    w = v.shape[1]
    c = jnp.tile(cos, (1, w // LANES))
    s = jnp.tile(sin, (1, w // LANES))
    lane = lax.broadcasted_iota(jnp.int32, v.shape, 1)
    first = (lane % HEAD_DIM) < (HEAD_DIM // 2)
    partner = jnp.where(first, pltpu.roll(v, w - HEAD_DIM // 2, 1), pltpu.roll(v, HEAD_DIM // 2, 1))
    return v * c + partner * s


def _heads(v, count):
    return jnp.concatenate([v[:, i * HEAD_DIM:(i + 1) * HEAD_DIM] for i in range(count)], axis=0)


def _unheads(v, count):
    r = v.shape[0] // count
    return jnp.concatenate([v[g * r:(g + 1) * r] for g in range(count)], axis=1)


def _rope_qkv(proj, cos, sin, aw, kvw):
    t = proj.shape[0]
    nkv = kvw // HEAD_DIM
    koff = aw // kvw
    tm = _pick(t, (256, 128))

    def body(q_ref, k_ref, v_ref, c_ref, s_ref, qo_ref, ko_ref, vo_ref):
        c, s = c_ref[...], s_ref[...]
        qo_ref[...] = _rope(q_ref[...].astype(F32), c, s).astype(BF16)
        k = _rope(k_ref[...].astype(F32), c, s).astype(BF16)
        v = v_ref[...].astype(BF16)
        for h in range(nkv):
            ko_ref[h] = k[:, h * HEAD_DIM:(h + 1) * HEAD_DIM]
            vo_ref[h] = v[:, h * HEAD_DIM:(h + 1) * HEAD_DIM]

    def row(w, j):
        return pl.BlockSpec((tm, w), lambda i: (i, j))

    hm = pl.BlockSpec((nkv, tm, HEAD_DIM), lambda i: (0, i, 0))
    return pl.pallas_call(
        body, name="rope_qkv", grid=(t // tm,),
        in_specs=[row(aw, 0), row(kvw, koff), row(kvw, koff + 1), row(LANES, 0), row(LANES, 0)],
        out_specs=[row(aw, 0), hm, hm],
        out_shape=[_sds((t, aw), BF16), _sds((nkv, t, HEAD_DIM), BF16), _sds((nkv, t, HEAD_DIM), BF16)],
        compiler_params=_cparams("parallel"),
    )(proj, proj, proj, cos, sin)


def _attn_probs(qs, kb, n, h, qpk, sinks_ref):
    s = lax.dot_general(qs, kb, (((1,), (1,)), ((), ())), preferred_element_type=F32) * (HEAD_DIM ** -0.5)
    qi = lax.broadcasted_iota(jnp.int32, (BLOCK, 2 * BLOCK), 0)
    kc = lax.broadcasted_iota(jnp.int32, (BLOCK, 2 * BLOCK), 1)
    valid = (kc > qi) & (kc <= qi + BLOCK) & ((kc >= BLOCK) | (n > 0))
    bias = jnp.tile(jnp.where(valid, 0.0, NEG).astype(F32), (qpk, 1))
    s = s + bias
    rowg = lax.broadcasted_iota(jnp.int32, (qpk * BLOCK, 1), 0) // BLOCK
    sink = jnp.zeros((qpk * BLOCK, 1), F32)
    for g in range(qpk):
        sink = jnp.where(rowg == g, sinks_ref[0, h * qpk + g], sink)
    m = jnp.maximum(jnp.max(s, axis=-1, keepdims=True), sink)
    e = jnp.exp(s - m)
    es = jnp.exp(sink - m)
    inv = 1.0 / (jnp.sum(e, axis=-1, keepdims=True) + es)
    return e * inv, es * inv, rowg


HEADS_PER_STEP = 4


def _attn_specs(qw, hp):
    def head(f):
        return pl.BlockSpec((hp, BLOCK, HEAD_DIM), lambda n, h: (h, f(n), 0))

    cur = lambda n: n
    prev = lambda n: jnp.maximum(n - 1, 0)
    return [pl.BlockSpec((BLOCK, hp * qw), lambda n, h: (n, h)), head(cur), head(prev), head(cur), head(prev),
            pl.BlockSpec(memory_space=pltpu.SMEM)]


def _attn_fwd(q_r, k_r, v_h, sinks):
    t, aw = q_r.shape
    nkv = k_r.shape[0]
    qpk = aw // (nkv * HEAD_DIM)
    qw = qpk * HEAD_DIM
    hp = HEADS_PER_STEP if nkv % HEADS_PER_STEP == 0 else 1

    def body(q_ref, kc_ref, kp_ref, vc_ref, vp_ref, sinks_ref, o_ref):
        n, hg = pl.program_id(0), pl.program_id(1)
        outs = []
        for j in range(hp):
            kb = jnp.concatenate([kp_ref[j], kc_ref[j]], axis=0)
            vb = jnp.concatenate([vp_ref[j], vc_ref[j]], axis=0)
            qs = _heads(q_ref[:, j * qw:(j + 1) * qw], qpk)
            p, _, _ = _attn_probs(qs, kb, n, hg * hp + j, qpk, sinks_ref)
            outs.append(_unheads(jnp.dot(p.astype(BF16), vb, preferred_element_type=F32), qpk))
        o_ref[...] = jnp.concatenate(outs, axis=1).astype(o_ref.dtype)

    return pl.pallas_call(
        body, name="attn_fwd", grid=(t // BLOCK, nkv // hp),
        in_specs=_attn_specs(qw, hp),
        out_specs=pl.BlockSpec((BLOCK, hp * qw), lambda n, h: (n, h)),
        out_shape=_sds((t, aw), BF16),
        compiler_params=_cparams("parallel", "parallel"),
    )(q_r, k_r, k_r, v_h, v_h, sinks)


def _attn_bwd(q_r, k_r, v_h, sinks, o, do):
    t, aw = q_r.shape
    nkv = k_r.shape[0]
    qpk = aw // (nkv * HEAD_DIM)
    qw = qpk * HEAD_DIM
    hp = HEADS_PER_STEP if nkv % HEADS_PER_STEP == 0 else 1
    scale = HEAD_DIM ** -0.5

    def body(q_ref, kc_ref, kp_ref, vc_ref, vp_ref, sinks_ref, o_ref, do_ref,
             dq_ref, dkc_ref, dkp_ref, dvc_ref, dvp_ref, ds_ref):
        n, hg = pl.program_id(0), pl.program_id(1)
        lane = lax.broadcasted_iota(jnp.int32, (8, LANES), 1)
        row0 = lax.broadcasted_iota(jnp.int32, (8, LANES), 0) == 0
        dsink = jnp.zeros((8, LANES), F32)
        dqs = []
        for j in range(hp):
            h = hg * hp + j
            cols = slice(j * qw, (j + 1) * qw)
            kb = jnp.concatenate([kp_ref[j], kc_ref[j]], axis=0)
            vb = jnp.concatenate([vp_ref[j], vc_ref[j]], axis=0)
            qs = _heads(q_ref[:, cols], qpk)
            dos = _heads(do_ref[:, cols], qpk)
            p, psink, rowg = _attn_probs(qs, kb, n, h, qpk, sinks_ref)
            pb = p.astype(BF16)
            delta = jnp.sum(dos.astype(F32) * _heads(o_ref[:, cols], qpk).astype(F32), axis=-1, keepdims=True)
            dv = lax.dot_general(pb, dos, (((0,), (0,)), ((), ())), preferred_element_type=F32)
            dp = lax.dot_general(dos, vb, (((1,), (1,)), ((), ())), preferred_element_type=F32)
            dsc = (p * (dp - delta)).astype(BF16)
            dqs.append(_unheads(jnp.dot(dsc, kb, preferred_element_type=F32) * scale, qpk))
            dk = lax.dot_general(dsc, qs, (((0,), (0,)), ((), ())), preferred_element_type=F32) * scale
            dkp_ref[j] = dk[:BLOCK]
            dkc_ref[j] = dk[BLOCK:]
            dvp_ref[j] = dv[:BLOCK]
            dvc_ref[j] = dv[BLOCK:]
            sink_term = psink * delta
            for g in range(qpk):
                val = -jnp.sum(jnp.where(rowg == g, sink_term, 0.0))
                dsink = jnp.where(row0 & (lane == h * qpk + g), val, dsink)
        dq_ref[...] = jnp.concatenate(dqs, axis=1).astype(dq_ref.dtype)
        first = (n == 0) & (hg == 0)

        @pl.when(first)
        def _():
            ds_ref[...] = dsink

        @pl.when(jnp.logical_not(first))
        def _():
            ds_ref[...] += dsink

    qblk = pl.BlockSpec((BLOCK, hp * qw), lambda n, h: (n, h))
    kvblk = pl.BlockSpec((hp, BLOCK, HEAD_DIM), lambda n, h: (h, n, 0))
    return pl.pallas_call(
        body, name="attn_bwd", grid=(t // BLOCK, nkv // hp),
        in_specs=_attn_specs(qw, hp) + [qblk, qblk],
        out_specs=[qblk, kvblk, kvblk, kvblk, kvblk, pl.BlockSpec((8, LANES), lambda n, h: (0, 0))],
        out_shape=[_sds((t, aw), BF16)] + [_sds((nkv, t, HEAD_DIM), F32)] * 4 + [_sds((8, LANES), F32)],
        compiler_params=_cparams("arbitrary", "arbitrary"),
    )(q_r, k_r, k_r, v_h, v_h, sinks, o, do)


HALO = 16


def _shift_down(v, k, halo):
    rows = lax.broadcasted_iota(jnp.int32, v.shape, 0)
    out = pltpu.roll(v, k, 0)
    for r in range(k):
        out = jnp.where(rows == r, halo[HALO - k + r:HALO - k + r + 1, :], out)
    return out


def _shift_up(v, k, halo):
    tm = v.shape[0]
    rows = lax.broadcasted_iota(jnp.int32, v.shape, 0)
    out = pltpu.roll(v, tm - k, 0)
    for r in range(k):
        out = jnp.where(rows == tm - k + r, halo[r:r + 1, :], out)
    return out


def _conv_fwd(proj, conv_w, zoff, cw, cb):
    t = proj.shape[0]
    tm = _pick(t, (512, 256, 128))
    zb, nb = zoff // cb, cw // cb
    hb = tm // HALO

    def body(z_ref, gb_ref, gc_ref, zp_ref, gcp_ref, w_ref, o_ref):
        i = pl.program_id(0)
        cz = gc_ref[...].astype(F32) * z_ref[...].astype(F32)
        czp = gcp_ref[...].astype(F32) * zp_ref[...].astype(F32) * (i > 0).astype(F32)
        w = w_ref[...]
        y = w[0:1] * _shift_down(cz, 2, czp) + w[1:2] * _shift_down(cz, 1, czp) + w[2:3] * cz
        o_ref[...] = (gb_ref[...].astype(F32) * y).astype(o_ref.dtype)

    def col(k):
        return pl.BlockSpec((tm, cb), lambda i, j: (i, zb + k * nb + j))

    def halo(k):
        return pl.BlockSpec((HALO, cb), lambda i, j: (jnp.maximum(i * hb - 1, 0), zb + k * nb + j))

    return pl.pallas_call(
        body, name="conv_fwd", grid=(t // tm, nb),
        in_specs=[col(0), col(1), col(2), halo(0), halo(2), pl.BlockSpec((3, cb), lambda i, j: (0, j))],
        out_specs=pl.BlockSpec((tm, cb), lambda i, j: (i, j)),
        out_shape=_sds((t, cw), BF16),
        compiler_params=_cparams("parallel", "parallel"),
    )(proj, proj, proj, proj, proj, conv_w)


def _conv_bwd(proj, conv_w, dco, zoff, cw, cb):
    t = proj.shape[0]
    tm = _pick(t, (512, 256, 128))
    zb, nb = zoff // cb, cw // cb
    hb = tm // HALO
    nt = t // tm

    def body(z_ref, gb_ref, gc_ref, zp_ref, gcp_ref, gbn_ref, w_ref, d_ref, dn_ref, dz_ref, dgb_ref, dgc_ref, dw_ref):
        i = pl.program_id(1)
        z, gb, gc = z_ref[...].astype(F32), gb_ref[...].astype(F32), gc_ref[...].astype(F32)
        d = d_ref[...].astype(F32)
        cz = gc * z
        czp = gcp_ref[...].astype(F32) * zp_ref[...].astype(F32) * (i > 0).astype(F32)
        w = w_ref[...]
        cz1 = _shift_down(cz, 1, czp)
        cz2 = _shift_down(cz, 2, czp)
        y = w[0:1] * cz2 + w[1:2] * cz1 + w[2:3] * cz
        dgb_ref[...] = (d * y).astype(dgb_ref.dtype)
        dy = d * gb
        dyn = dn_ref[...].astype(F32) * gbn_ref[...].astype(F32) * (i < nt - 1).astype(F32)
        dcz = w[2:3] * dy + w[1:2] * _shift_up(dy, 1, dyn) + w[0:1] * _shift_up(dy, 2, dyn)
        dgc_ref[...] = (dcz * z).astype(dgc_ref.dtype)
        dz_ref[...] = (dcz * gc).astype(dz_ref.dtype)
        rows = lax.broadcasted_iota(jnp.int32, (8, cb), 0)
        dw = jnp.zeros((8, cb), F32)
        for r, tap in enumerate((cz2, cz1, cz)):
            dw = jnp.where(rows == r, jnp.sum(dy * tap, axis=0, keepdims=True), dw)

        @pl.when(i == 0)
        def _():
            dw_ref[...] = dw

        @pl.when(i > 0)
        def _():
            dw_ref[...] += dw

    def col(k):
        return pl.BlockSpec((tm, cb), lambda j, i: (i, zb + k * nb + j))

    def halo_prev(k):
        return pl.BlockSpec((HALO, cb), lambda j, i: (jnp.maximum(i * hb - 1, 0), zb + k * nb + j))

    own = pl.BlockSpec((tm, cb), lambda j, i: (i, j))
    nxt = lambda i: jnp.minimum((i + 1) * hb, t // HALO - 1)
    return pl.pallas_call(
        body, name="conv_bwd", grid=(nb, nt),
        in_specs=[col(0), col(1), col(2), halo_prev(0), halo_prev(2),
                  pl.BlockSpec((HALO, cb), lambda j, i: (nxt(i), zb + nb + j)),
                  pl.BlockSpec((3, cb), lambda j, i: (0, j)), own,
                  pl.BlockSpec((HALO, cb), lambda j, i: (nxt(i), j))],
        out_specs=[own, own, own, pl.BlockSpec((8, cb), lambda j, i: (0, j))],
        out_shape=[_sds((t, cw), BF16)] * 3 + [_sds((8, cw), F32)],
        compiler_params=_cparams("parallel", "arbitrary"),
    )(proj, proj, proj, proj, proj, proj, conv_w, dco, dco)


def _yconv_merge(conv_o, w_cp, ya, proj, goff, cb, dep, name):
    t, k = conv_o.shape
    d = w_cp.shape[1]
    tm = _pick(t, (1024, 512, 256, 128))
    gb_, nb = goff // cb, d // cb
    deps = [] if dep is None else [dep]

    def body(a_ref, w_ref, ya_ref, ga_ref, gc_ref, *rest):
        yc_ref, o_ref = rest[len(deps):]
        yc = jnp.dot(a_ref[...], w_ref[...], preferred_element_type=F32)
        f = lambda r: r[...].astype(F32)
        yc_ref[...] = yc.astype(yc_ref.dtype)
        o_ref[...] = (_sigmoid(f(ga_ref)) * f(ya_ref) + _sigmoid(f(gc_ref)) * yc).astype(o_ref.dtype)

    own = pl.BlockSpec((tm, cb), lambda i, j: (i, j))
    return pl.pallas_call(
        body, name=name, grid=(t // tm, nb),
        in_specs=[pl.BlockSpec((tm, k), lambda i, j: (i, 0)), pl.BlockSpec((k, cb), lambda i, j: (0, j)), own,
                  pl.BlockSpec((tm, cb), lambda i, j: (i, gb_ + j)),
                  pl.BlockSpec((tm, cb), lambda i, j: (i, gb_ + nb + j))] + [ANY] * len(deps),
        out_specs=[own, own], out_shape=[_sds((t, d), BF16)] * 2,
        compiler_params=_cparams("parallel", "parallel"),
    )(conv_o, w_cp, ya, proj, proj, *deps)


def _dmerged_split(dh, w_mo, proj, ya, yc, goff, cb, dep, name):
    t, k = dh.shape
    d = w_mo.shape[0]
    tm = _pick(t, (1024, 512, 256, 128))
    gb_, nb = goff // cb, d // cb
    deps = [] if dep is None else [dep]

    def body(a_ref, w_ref, ga_ref, gc_ref, ya_ref, yc_ref, *rest):
        dya_ref, dyc_ref, dga_ref, dgc_ref = rest[len(deps):]
        dmv = lax.dot_general(a_ref[...], w_ref[...], (((1,), (1,)), ((), ())), preferred_element_type=F32)
        sa = _sigmoid(ga_ref[...].astype(F32))
        sc = _sigmoid(gc_ref[...].astype(F32))
        dya_ref[...] = (dmv * sa).astype(BF16)
        dyc_ref[...] = (dmv * sc).astype(BF16)
        dga_ref[...] = (dmv * ya_ref[...].astype(F32) * sa * (1.0 - sa)).astype(BF16)
        dgc_ref[...] = (dmv * yc_ref[...].astype(F32) * sc * (1.0 - sc)).astype(BF16)

    own = pl.BlockSpec((tm, cb), lambda i, j: (i, j))
    return pl.pallas_call(
        body, name=name, grid=(t // tm, nb),
        in_specs=[pl.BlockSpec((tm, k), lambda i, j: (i, 0)), pl.BlockSpec((cb, k), lambda i, j: (j, 0)),
                  pl.BlockSpec((tm, cb), lambda i, j: (i, gb_ + j)),
                  pl.BlockSpec((tm, cb), lambda i, j: (i, gb_ + nb + j)), own, own] + [ANY] * len(deps),
        out_specs=[own] * 4, out_shape=[_sds((t, d), BF16)] * 4,
        compiler_params=_cparams("parallel", "parallel"),
    )(dh, w_mo, proj, proj, ya, yc, *deps)


def _assemble_dproj(dq, dkc, dkp, dvc, dvp, cos, sin, dz, dgb, dgc, dga, dgg):
    t, aw = dq.shape
    nkv, cw, d = dkc.shape[0], dz.shape[1], dga.shape[1]
    kvw = nkv * HEAD_DIM
    nblk = t // BLOCK
    width = aw + 2 * kvw + 3 * cw + 2 * d

    def body(dq_ref, dkc_ref, dkp_ref, dvc_ref, dvp_ref, c_ref, s_ref, dz_ref, dgb_ref, dgc_ref, dga_ref, dgg_ref,
             o_ref):
        keep = (pl.program_id(0) < nblk - 1).astype(F32)
        c, s = c_ref[...], s_ref[...]
        dk = jnp.concatenate([dkc_ref[h] + dkp_ref[h] * keep for h in range(nkv)], axis=1)
        dv = jnp.concatenate([dvc_ref[h] + dvp_ref[h] * keep for h in range(nkv)], axis=1)
        o_ref[...] = jnp.concatenate(
            [_rope(dq_ref[...].astype(F32), c, -s).astype(BF16), _rope(dk, c, -s).astype(BF16), dv.astype(BF16),
             dz_ref[...], dgb_ref[...], dgc_ref[...], dga_ref[...], dgg_ref[...]], axis=1)

    def cur(w):
        return pl.BlockSpec((BLOCK, w), lambda n: (n, 0))

    head_cur = pl.BlockSpec((nkv, BLOCK, HEAD_DIM), lambda n: (0, n, 0))
    head_nxt = pl.BlockSpec((nkv, BLOCK, HEAD_DIM), lambda n: (0, jnp.minimum(n + 1, nblk - 1), 0))
    return pl.pallas_call(
        body, name="assemble_dproj", grid=(nblk,),
        in_specs=[cur(aw), head_cur, head_nxt, head_cur, head_nxt, cur(LANES), cur(LANES),
                  cur(cw), cur(cw), cur(cw), cur(d), cur(d)],
        out_specs=cur(width), out_shape=_sds((t, width), BF16),
        compiler_params=_cparams("parallel"),
    )(dq, dkc, dkp, dvc, dvp, cos, sin, dz, dgb, dgc, dga, dgg)


def _xattn_probs(qh, kh):
    s = lax.dot_general(qh, kh, (((1,), (1,)), ((), ())), preferred_element_type=F32) * (X_HEAD_DIM ** -0.5)
    e = jnp.exp(s - jnp.max(s, axis=-1, keepdims=True))
    return e * (1.0 / jnp.sum(e, axis=-1, keepdims=True))


def _xattn_fwd(xq, kv):
    t, xw = xq.shape
    mt = kv.shape[0]
    tm = _pick(t, (512, 256, 128))

    def body(q_ref, kv_ref, o_ref):
        outs = []
        for hd in range(xw // X_HEAD_DIM):
            hs = slice(hd * X_HEAD_DIM, (hd + 1) * X_HEAD_DIM)
            vs = slice(xw + hd * X_HEAD_DIM, xw + (hd + 1) * X_HEAD_DIM)
            p = _xattn_probs(q_ref[:, hs], kv_ref[:, hs])
            outs.append(jnp.dot(p.astype(BF16), kv_ref[:, vs], preferred_element_type=F32))
        o_ref[...] = jnp.concatenate(outs, axis=1).astype(o_ref.dtype)

    return pl.pallas_call(
        body, name="xattn_fwd", grid=(t // tm,),
        in_specs=[pl.BlockSpec((tm, xw), lambda i: (i, 0)), pl.BlockSpec((mt, 2 * xw), lambda i: (0, 0))],
        out_specs=pl.BlockSpec((tm, xw), lambda i: (i, 0)), out_shape=_sds((t, xw), BF16),
        compiler_params=_cparams("parallel"),
    )(xq, kv)


def _xattn_bwd(xq, kv, do):
    t, xw = xq.shape
    mt = kv.shape[0]
    tm = _pick(t, (512, 256, 128))
    scale = X_HEAD_DIM ** -0.5

    def body(q_ref, kv_ref, do_ref, dq_ref, dkv_ref):
        dqs, dks, dvs = [], [], []
        for hd in range(xw // X_HEAD_DIM):
            hs = slice(hd * X_HEAD_DIM, (hd + 1) * X_HEAD_DIM)
            vs = slice(xw + hd * X_HEAD_DIM, xw + (hd + 1) * X_HEAD_DIM)
            qh, kh, vh, doh = q_ref[:, hs], kv_ref[:, hs], kv_ref[:, vs], do_ref[:, hs]
            p = _xattn_probs(qh, kh)
            pb = p.astype(BF16)
            o = jnp.dot(pb, vh, preferred_element_type=F32)
            delta = jnp.sum(doh.astype(F32) * o, axis=-1, keepdims=True)
            dvs.append(lax.dot_general(pb, doh, (((0,), (0,)), ((), ())), preferred_element_type=F32))
            dp = lax.dot_general(doh, vh, (((1,), (1,)), ((), ())), preferred_element_type=F32)
            dsc = (p * (dp - delta)).astype(BF16)
            dqs.append(jnp.dot(dsc, kh, preferred_element_type=F32) * scale)
            dks.append(lax.dot_general(dsc, qh, (((0,), (0,)), ((), ())), preferred_element_type=F32) * scale)
        dq_ref[...] = jnp.concatenate(dqs, axis=1).astype(dq_ref.dtype)
        dkv = jnp.concatenate(dks + dvs, axis=1)

        @pl.when(pl.program_id(0) == 0)
        def _():
            dkv_ref[...] = dkv

        @pl.when(pl.program_id(0) > 0)
        def _():
            dkv_ref[...] += dkv

    row = pl.BlockSpec((tm, xw), lambda i: (i, 0))
    whole = pl.BlockSpec((mt, 2 * xw), lambda i: (0, 0))
    return pl.pallas_call(
        body, name="xattn_bwd", grid=(t // tm,),
        in_specs=[row, whole, row], out_specs=[row, whole],
        out_shape=[_sds((t, xw), BF16), _sds((mt, 2 * xw), F32)],
        compiler_params=_cparams("arbitrary"),
    )(xq, kv, do)


def _ffn_in_swiglu(u, w, dep, name):
    t, d = u.shape
    f = w.shape[1] // 2
    tm, tn, _ = _tiles("nn", t, f)
    nf = f // tn
    deps = [] if dep is None else [dep]

    def body(u_ref, wa_ref, wb_ref, *rest):
        h_ref, o_ref = rest[len(deps):]
        a = jnp.dot(u_ref[...], wa_ref[...], preferred_element_type=F32)
        b = jnp.dot(u_ref[...], wb_ref[...], preferred_element_type=F32)
        h_ref[0] = a.astype(h_ref.dtype)
        h_ref[1] = b.astype(h_ref.dtype)
        o_ref[...] = (a * _sigmoid(a) * b).astype(o_ref.dtype)

    return pl.pallas_call(
        body, name=name, grid=(t // tm, nf),
        in_specs=[pl.BlockSpec((tm, d), lambda i, j: (i, 0)), pl.BlockSpec((d, tn), lambda i, j: (0, j)),
                  pl.BlockSpec((d, tn), lambda i, j: (0, nf + j))] + [ANY] * len(deps),
        out_specs=[pl.BlockSpec((2, tm, tn), lambda i, j: (0, i, j)), pl.BlockSpec((tm, tn), lambda i, j: (i, j))],
        out_shape=[_sds((2, t, f), BF16), _sds((t, f), BF16)],
        compiler_params=_cparams("parallel", "parallel"),
    )(u, w, w, *deps)


def _dact_swiglu(dh, w_out, hid, dep, name):
    t, d = dh.shape
    f = w_out.shape[0]
    tm, tn, _ = _tiles("nt", t, f)
    deps = [] if dep is None else [dep]

    def body(dh_ref, w_ref, h_ref, *rest):
        o_ref = rest[len(deps)]
        g = lax.dot_general(dh_ref[...], w_ref[...], (((1,), (1,)), ((), ())), preferred_element_type=F32)
        a = h_ref[0].astype(F32)
        b = h_ref[1].astype(F32)
        sg = _sigmoid(a)
        o_ref[0] = (g * b * sg * (1.0 + a * (1.0 - sg))).astype(o_ref.dtype)
        o_ref[1] = (g * a * sg).astype(o_ref.dtype)

    pair = pl.BlockSpec((2, tm, tn), lambda i, j: (0, i, j))
    return pl.pallas_call(
        body, name=name, grid=(t // tm, f // tn),
        in_specs=[pl.BlockSpec((tm, d), lambda i, j: (i, 0)), pl.BlockSpec((tn, d), lambda i, j: (j, 0)), pair]
        + [ANY] * len(deps),
        out_specs=pair, out_shape=_sds((2, t, f), BF16),
        compiler_params=_cparams("parallel", "parallel"),
    )(dh, w_out, hid, *deps)


def _adamw(w, g, m, v, name):
    r, c = w.shape
    tr = _pick(r, (256, 128, 64, 32, 16, 8)) if r * c > 65536 else r

    def body(w_ref, g_ref, m_ref, v_ref, d_ref, nm_ref, nv_ref, go_ref):
        gv = g_ref[...]
        go_ref[...] = gv
        m2 = ADAM_B1 * m_ref[...] + (1.0 - ADAM_B1) * gv
        v2 = ADAM_B2 * v_ref[...] + (1.0 - ADAM_B2) * (gv * gv)
        m_hat = m2 / (1.0 - ADAM_B1 ** ADAM_STEP)
        v_hat = v2 / (1.0 - ADAM_B2 ** ADAM_STEP)
        d_ref[...] = -ADAM_LR * (m_hat / (jnp.sqrt(v_hat) + ADAM_EPS) + ADAM_WD * w_ref[...])
        nm_ref[...] = m2
        nv_ref[...] = v2

    blk = pl.BlockSpec((tr, c), lambda i: (i, 0))
    return pl.pallas_call(
        body, name=name, grid=(r // tr,),
        in_specs=[blk] * 4, out_specs=[blk] * 4, out_shape=[_sds((r, c), F32)] * 4,
        compiler_params=_cparams("parallel"),
    )(w, g, m, v)


class _Weights:
    def __init__(self, full):
        self.full = full
        self.grads = {}

    def get(self, name):
        return self.full[name]

    def mark(self, tag, value):
        return value

    def grad(self, name, g):
        self.grads[name] = g

    def dep(self):
        return None


def _local_step(x, mem, tgt, g_mix, sinks, g_xattn, g_mem, g_ffn, g_final, dims, wts):
    t, d = x.shape
    aw, cw, kvw = dims
    cb = 2 * kvw
    zoff = aw + 2 * kvw
    goff = zoff + 3 * cw
    cos, sin = _rope_tables(t)
    mark, get = wts.mark, wts.get

    def mm(a, b, **kw):
        return _matmul(a, b, dep=wts.dep(), **kw)

    u1 = mark("u1", _rms_fwd(x, g_mix, "rms_mix"))
    qkv = mm(u1, get("w_in"), mode="nn", out_dtype=F32, name="mm_qkv", cols=(0, zoff))
    proj = mark("proj", mm(u1, get("w_in"), mode="nn", out_dtype=BF16, name="mm_gates", cols=(zoff, goff - zoff + 2 * d)))
    zoff, goff = 0, goff - zoff
    q_r, k_r, v_h = _rope_qkv(qkv, cos, sin, aw, kvw)
    attn_o = mark("attn_o", _attn_fwd(q_r, k_r, v_h, sinks))
    conv_o = mark("conv_o", _conv_fwd(proj, get("conv_w"), zoff, cw, cb))
    ya = mark("ya", mm(attn_o, get("w_attn_proj"), mode="nn", out_dtype=BF16, name="mm_yattn"))
    yc, merged = _yconv_merge(conv_o, get("w_conv_proj"), ya, proj, goff, cb, wts.dep(), "mm_yconv")
    mark("merged", merged)
    norm_outs = [("row", F32), ("row", BF16)]
    bwd_outs = [("row", F32), ("row", BF16), ("vec",)]
    h1, u2 = _rows_matmul(merged, get("w_mix_out"), mode="nn", kslice=(0, 1), res=x, rows_in=[], vec=g_xattn,
                          epilogue=_ep_norm, outs=norm_outs, name="mm_mix", dep=wts.dep())
    mark("h1", h1)
    mem_n = _rms_fwd(mem, g_mem, "rms_mem")
    xq = mark("xq", mm(u2, get("w_xq"), mode="nn", out_dtype=BF16, name="mm_xq"))
    kv = mm(mem_n, get("w_xkv"), mode="nn", out_dtype=BF16, name="mm_xkv")
    xo = mark("xo", _xattn_fwd(xq, kv))
    h2, u3 = _rows_matmul(xo, get("w_xo"), mode="nn", kslice=(0, 1), res=h1, rows_in=[], vec=g_ffn,
                          epilogue=_ep_norm, outs=norm_outs, name="mm_xo", dep=wts.dep())
    mark("h2", h2)
    mark("u3", u3)
    hid, act = _ffn_in_swiglu(u3, get("w_ffn_in"), wts.dep(), "mm_ffn_in")
    mark("hid", hid)
    dh3, dh3b, dg_final, loss = _rows_matmul(
        act, get("w_ffn_out"), mode="nn", kslice=(0, 1), res=h2, rows_in=[tgt], vec=g_final, epilogue=_ep_loss,
        outs=bwd_outs + [("lane",)], name="mm_ffn_out", dep=wts.dep())
    mark("dh3", dh3b)

    wts.grad("w_ffn_out", mm(act, dh3b, mode="tn", out_dtype=BF16, name="mm_dw_ffn_out"))
    dhid = mark("dhid", _dact_swiglu(dh3b, get("w_ffn_out"), hid, wts.dep(), "mm_dact"))
    wts.grad("w_ffn_in", mm(u3, dhid, mode="tn", out_dtype=BF16, name="mm_dw_ffn_in"))
    du3 = mark("du3", _matmul_slice(dhid, get("w_ffn_in"), mode="nt", out_dtype=F32, name="mm_du3_k0", res=None,
                                    dep=wts.dep(), kslice=(0, 2), cols=(0, d)))
    dh2, dh2b, dg_ffn = _rows_matmul(dhid, get("w_ffn_in"), mode="nt", kslice=(1, 2), res=du3, rows_in=[h2, dh3],
                                     vec=g_ffn, epilogue=_ep_rms_bwd, outs=bwd_outs, name="mm_du3_k1", dep=wts.dep())
    mark("dh2", dh2b)
    wts.grad("w_xo", mm(xo, dh2b, mode="tn", out_dtype=BF16, name="mm_dw_xo"))
    dxo = mm(dh2b, get("w_xo"), mode="nt", out_dtype=BF16, name="mm_dxo")
    dxq, dkv = _xattn_bwd(xq, kv, dxo)
    dkvb = dkv.astype(BF16)
    wts.grad("w_xq", mm(u2, dxq, mode="tn", out_dtype=BF16, name="mm_dw_xq"))
    wts.grad("w_xkv", mm(mem_n, dkvb, mode="tn", out_dtype=BF16, name="mm_dw_xkv"))
    dmem_n = mm(dkvb, get("w_xkv"), mode="nt", out_dtype=F32, name="mm_dmem")
    _, _, dg_mem = _rms_bwd(mem, g_mem, dmem_n, None, "rms_bwd_mem")
    dh1, dh1b, dg_xattn = _rows_matmul(dxq, get("w_xq"), mode="nt", kslice=(0, 1), res=None, rows_in=[h1, dh2],
                                       vec=g_xattn, epilogue=_ep_rms_bwd, outs=bwd_outs, name="mm_du2", dep=wts.dep())
    mark("dh1", dh1b)
    wts.grad("w_mix_out", mm(merged, dh1b, mode="tn", out_dtype=BF16, name="mm_dw_mix"))
    dya, dyc, dga, dgg = _dmerged_split(dh1b, get("w_mix_out"), proj, ya, yc, goff, cb, wts.dep(), "mm_dmerged")
    mark("dya", dya)
    wts.grad("w_attn_proj", mm(attn_o, dya, mode="tn", out_dtype=BF16, name="mm_dw_attn_proj"))
    dattn_o = mm(dya, get("w_attn_proj"), mode="nt", out_dtype=BF16, name="mm_dattn")
    wts.grad("w_conv_proj", mm(conv_o, dyc, mode="tn", out_dtype=BF16, name="mm_dw_conv_proj"))
    dconv_o = mark("dconv_o", mm(dyc, get("w_conv_proj"), mode="nt", out_dtype=BF16, name="mm_dconv"))
    dz, dgb, dgc, dconv_w = _conv_bwd(proj, get("conv_w"), dconv_o, zoff, cw, cb)
    mark("dz", dz)
    dq, dkc, dkp, dvc, dvp, dsinks = _attn_bwd(q_r, k_r, v_h, sinks, attn_o, dattn_o)
    mark("dq", dq)
    dproj = mark("dproj", _assemble_dproj(dq, dkc, dkp, dvc, dvp, cos, sin, dz, dgb, dgc, dga, dgg))
    wts.grad("w_in", mm(u1, dproj, mode="tn", out_dtype=BF16, name="mm_dw_in"))
    du1 = mark("du1", _matmul_slice(dproj, get("w_in"), mode="nt", out_dtype=F32, name="mm_du1_k0", res=None,
                                    dep=wts.dep(), kslice=(0, 2), cols=(0, d)))
    grad_x, _, dg_mix = _rows_matmul(dproj, get("w_in"), mode="nt", kslice=(1, 2), res=du1, rows_in=[x, dh1],
                                     vec=g_mix, epilogue=_ep_rms_bwd, outs=bwd_outs, name="mm_du1_k1", dep=wts.dep())
    mark("grad_x", grad_x)

    small = dict(g_mix=dg_mix, g_xattn=dg_xattn, g_mem=dg_mem, g_ffn=dg_ffn, g_final=dg_final,
                 conv_w=dconv_w[:3], attn_sinks=dsinks[0:1, :sinks.shape[1]], loss=loss[0:1, 0:1])
    return grad_x, small


BIG = (("w_in", 1), ("w_attn_proj", 1), ("w_conv_proj", 1), ("w_mix_out", 0), ("w_xq", 0), ("w_xkv", 0),
       ("w_xo", 1), ("w_ffn_in", 1), ("w_ffn_out", 0))


def _place():
    x, y, c = lax.axis_index("x"), lax.axis_index("y"), lax.axis_index("c")
    chips = [(1 - x, y), (x, 1 - y), (1 - x, 1 - y)]
    return x, y, c, chips


def _window(ref, ax, shard_shape, s, h):
    sr, sc = shard_shape
    hr = sr // 2
    if ax == 1:
        return ref.at[pl.ds(pl.multiple_of(h * hr, 16), hr), pl.ds(pl.multiple_of(s * sc, LANES), sc)]
    return ref.at[pl.ds(pl.multiple_of(s * sr + h * hr, 16), hr), :]


def _half(ref, h):
    hr = ref.shape[0] // 2
    return ref.at[pl.ds(pl.multiple_of(h * hr, 16), hr), :]


def _remote(src, dst, send_sem, recv_sem, dev):
    return pltpu.make_async_remote_copy(src_ref=src, dst_ref=dst, send_sem=send_sem, recv_sem=recv_sem,
                                        device_id=dev, device_id_type=MESH)


def _cast_to_full(shard, ax, me, name, dtype=BF16):
    sr, sc = shard.shape
    tr = _pick(sr, (256, 352, 128, 64, 32, 16))
    nr = sr // tr
    full = (sr * N_CHIPS, sc) if ax == 0 else (sr, sc * N_CHIPS)

    def body(me_ref, s_ref, o_ref):
        o_ref[...] = s_ref[...].astype(o_ref.dtype)

    if ax == 1:
        out_spec = pl.BlockSpec((tr, sc), lambda r, me_ref: (r, me_ref[0]))
    else:
        out_spec = pl.BlockSpec((tr, sc), lambda r, me_ref: (me_ref[0] * nr + r, 0))
    return pl.pallas_call(
        body, name=name,
        grid_spec=pltpu.PrefetchScalarGridSpec(
            num_scalar_prefetch=1, grid=(nr,), in_specs=[pl.BlockSpec((tr, sc), lambda r, me_ref: (r, 0))],
            out_specs=out_spec),
        out_shape=_sds(full, dtype),
        compiler_params=_cparams("parallel"),
    )(me, shard)


HBM = pl.BlockSpec(memory_space=pltpu.HBM)
SEM = pl.BlockSpec(memory_space=pltpu.SEMAPHORE)
EFFECT = pltpu.SideEffectType.DATAFLOW_SIDE_EFFECTING


def _in_hbm(a):
    return pltpu.with_memory_space_constraint(a, pltpu.HBM)


def _gather_window(ref, ax, shard_shape, s, h):
    if h is not None:
        return _window(ref, ax, shard_shape, s, h)
    sr, sc = shard_shape
    if ax == 1:
        return ref.at[:, pl.ds(pl.multiple_of(s * sc, LANES), sc)]
    return ref.at[pl.ds(pl.multiple_of(s * sr, 8), sr), :]


def _ag_start(fulls, axes, shard_shapes, whole):
    n = len(fulls)

    def body(*refs):
        src = refs[:n]
        send_sems, recv_sems = refs[n], refs[n + 1]
        token = refs[2 * n + 2]
        x, y, c, chips = _place()
        me = 2 * x + y
        for i in range(n):
            h = None if whole[i] else c
            for j, chip in enumerate(chips):
                blk = _gather_window(src[i], axes[i], shard_shapes[i], me, h)
                _remote(blk, blk, send_sems.at[3 * i + j], recv_sems.at[3 * i + j], (*chip, c)).start()
        token[...] = jnp.zeros_like(token)

    res = pl.pallas_call(
        body, name="ag_start_" + str(n),
        out_shape=(pltpu.SemaphoreType.DMA((3 * n,)), pltpu.SemaphoreType.DMA((3 * n,)),
                   *[pltpu.HBM(f.shape, f.dtype) for f in fulls], _sds((8, LANES), F32)),
        in_specs=[HBM] * n, out_specs=(SEM, SEM, *[HBM] * n, pl.BlockSpec(memory_space=pltpu.VMEM)),
        input_output_aliases={i: 2 + i for i in range(n)},
        compiler_params=pltpu.CompilerParams(has_side_effects=EFFECT),
    )(*[_in_hbm(f) for f in fulls])
    return res[0], res[1], list(res[2:2 + n]), res[2 + n]


def _ag_mid(bufs, slots, axes, shard_shapes, whole, send_sems, recv_sems, after, name):
    ng = len(bufs)

    def body(*refs):
        src = refs[:ng]
        s_in, r_in = refs[ng], refs[ng + 1]
        fsend, frecv = refs[ng + 3], refs[ng + 4]
        x, y, c, chips = _place()
        me = 2 * x + y
        sib = (x, y, 1 - c)
        for k, i in enumerate(slots):
            h = None if whole[k] else c
            for j, chip in enumerate(chips):
                cj = 2 * chip[0] + chip[1]
                mine = _gather_window(src[k], axes[k], shard_shapes[k], me, h)
                theirs = _gather_window(src[k], axes[k], shard_shapes[k], cj, h)
                _remote(theirs, theirs, s_in.at[3 * i + j], r_in.at[3 * i + j], (*chip, c)).wait_recv()
                _remote(mine, mine, s_in.at[3 * i + j], r_in.at[3 * i + j], (*chip, c)).wait_send()
                if not whole[k]:
                    _remote(theirs, theirs, fsend.at[3 * k + j], frecv.at[3 * k + j], sib).start()
        token = refs[2 * ng + 5]
        token[...] = jnp.zeros_like(token)

    res = pl.pallas_call(
        body, name=name,
        out_shape=(pltpu.SemaphoreType.DMA((3 * ng,)), pltpu.SemaphoreType.DMA((3 * ng,)),
                   *[pltpu.HBM(b.shape, b.dtype) for b in bufs], _sds((8, LANES), F32)),
        in_specs=[HBM] * ng + [SEM, SEM, ANY],
        out_specs=(SEM, SEM, *[HBM] * ng, pl.BlockSpec(memory_space=pltpu.VMEM)),
        input_output_aliases={k: 2 + k for k in range(ng)},
        compiler_params=pltpu.CompilerParams(has_side_effects=EFFECT),
    )(*bufs, send_sems, recv_sems, after)
    return res[0], res[1], list(res[2:2 + ng]), res[2 + ng]


def _ag_wait(bufs, axes, shard_shapes, whole, fsend, frecv, after, name):
    ng = len(bufs)

    def body(*refs):
        src = refs[:ng]
        s_in, r_in = refs[ng], refs[ng + 1]
        x, y, c, chips = _place()
        sib = (x, y, 1 - c)
        for k in range(ng):
            if whole[k]:
                continue
            for j, chip in enumerate(chips):
                cj = 2 * chip[0] + chip[1]
                sent = _gather_window(src[k], axes[k], shard_shapes[k], cj, c)
                landed = _gather_window(src[k], axes[k], shard_shapes[k], cj, 1 - c)
                _remote(landed, landed, s_in.at[3 * k + j], r_in.at[3 * k + j], sib).wait_recv()
                _remote(sent, sent, s_in.at[3 * k + j], r_in.at[3 * k + j], sib).wait_send()

    res = pl.pallas_call(
        body, name=name,
        out_shape=tuple(pltpu.HBM(b.shape, b.dtype) for b in bufs),
        in_specs=[HBM] * ng + [SEM, SEM, ANY], out_specs=tuple([HBM] * ng),
        input_output_aliases={k: k for k in range(ng)},
        compiler_params=pltpu.CompilerParams(has_side_effects=EFFECT),
    )(*bufs, fsend, frecv, after)
    return list(res)


class _Schedule:
    GROUPS = ((("w_in", "conv_w"), "u1", "u1"),
              (("w_attn_proj", "w_conv_proj", "w_mix_out", "w_xq", "w_xkv", "w_xo"), "proj", "conv_o"),
              (("w_ffn_in",), "h1", "h2"),
              (("w_ffn_out",), "u3", "hid"))
    STARTS = ((0,), (1, 2, 3))
    REDUCE = ((("w_ffn_out",), "dhid", "grad:w_ffn_in", "du3"),
              (("w_ffn_in",), "grad:w_ffn_in", "dya", "grad_x"),
              (("w_xo", "w_xq", "w_xkv", "w_mix_out", "w_attn_proj", "w_conv_proj"), "dconv_o", "dq", "grad_x"),
              (("w_in",), "grad:w_in", "end", "end2"))

    def __init__(self, seed, axes, shard_shapes, place, on_ready):
        self.ax, self.shape, self.place, self.on_ready = axes, shard_shapes, place, on_ready
        self.stage, self.buf, self.slot, self.passes = {}, {}, {}, {}
        self.ready = set()
        self.grads = {}
        token = None
        for groups in self.STARTS:
            order = [nm for g in groups for nm in self.GROUPS[g][0]]
            send, recv, bufs, token = _ag_start([seed(nm, token) for nm in order], *self._meta(order))
            self.buf.update(zip(order, bufs))
            self.slot.update({nm: (send, recv, k) for k, nm in enumerate(order)})
        self.token = self.latest = token
        self.mark("start", token)

    def _meta(self, names):
        return ([self.ax[nm] for nm in names], [self.shape[nm] for nm in names], [nm == "conv_w" for nm in names])

    def mark(self, tag, value):
        for g, (names, mid, wait) in enumerate(self.GROUPS):
            if tag == mid:
                send, recv, _ = self.slot[names[0]]
                fs, fr, bufs, self.latest = _ag_mid([self.buf[nm] for nm in names], [self.slot[nm][2] for nm in names],
                                                    *self._meta(names), send, recv, value, "ag_mid_%d" % g)
                self.buf.update(zip(names, bufs))
                self.passes[g] = (fs, fr)
            if tag == wait:
                fs, fr = self.passes[g]
                bufs = _ag_wait([self.buf[nm] for nm in names], *self._meta(names), fs, fr, value, "ag_wait_%d" % g)
                self.buf.update(zip(names, bufs))
                self.ready.update(names)
        for g, (names, send, total, finish) in enumerate(self.REDUCE):
            st = self.stage.get(g)
            if st is None:
                continue
            ng = len(names)
            if tag == send and st["at"] == "pair":
                arrs = _exchange_wait("rs_pair_wait_%d" % g, st["arrs"], *st["sems"], st["plan"], value)
                parts = [_pair_add(arrs[k], arrs[ng + k], self.ax[nm], self.shape[nm], self.place, "pair_add_" + nm)
                         for k, nm in enumerate(names)]
                plan, nsem = _plan_chip(ng)
                ss, rs, arrs, self.latest = _exchange_start(
                    "rs_chip_start_%d" % g, parts + [lax.empty(p.shape, p.dtype) for p in parts], nsem, plan)
                self.stage[g] = dict(at="chip", arrs=arrs, sems=(ss, rs), plan=plan)
            elif tag == total and st["at"] == "chip":
                arrs = _exchange_wait("rs_chip_wait_%d" % g, st["arrs"], *st["sems"], st["plan"], value)
                halves = [_chip_add(arrs[k], arrs[ng + k], self.place, "chip_add_" + nm) for k, nm in enumerate(names)]
                plan, nsem = _plan_gather(ng)
                ss, rs, arrs, self.latest = _exchange_start("rs_gather_start_%d" % g, halves, nsem, plan)
                self.stage[g] = dict(at="gather", arrs=arrs, sems=(ss, rs), plan=plan)
            st = self.stage[g]
            if tag == finish and st["at"] == "gather":
                arrs = _exchange_wait("rs_gather_wait_%d" % g, st["arrs"], *st["sems"], st["plan"], value)
                self.stage[g] = dict(at="done")
                for nm, shard in zip(names, arrs):
                    self.on_ready(nm, shard)
        return value

    def get(self, name):
        assert name in self.ready, name
        return self.buf[name]

    def grad(self, name, g):
        self.grads[name] = g
        for gi, (names, _, _, _) in enumerate(self.REDUCE):
            if name == names[-1]:
                gs = [self.grads[nm] for nm in names]
                plan, nsem = _plan_pair(len(names), [self.ax[nm] for nm in names], [self.shape[nm] for nm in names])
                ss, rs, arrs, self.latest = _exchange_start(
                    "rs_pair_start_%d" % gi, gs + [lax.empty(a.shape, a.dtype) for a in gs], nsem, plan)
                self.stage[gi] = dict(at="pair", arrs=arrs, sems=(ss, rs), plan=plan)
                g = self.latest
        self.mark("grad:" + name, g)

    def dep(self):
        return self.latest


def _exchange_start(name, arrays, nsem, plan):
    n = len(arrays)

    def body(*refs):
        send_sems, recv_sems, token = refs[n], refs[n + 1], refs[2 * n + 2]
        sends, _ = plan(refs[:n])
        for k, (src, dst, dev) in enumerate(sends):
            _remote(src, dst, send_sems.at[k], recv_sems.at[k], dev).start()
        token[...] = jnp.zeros_like(token)

    res = pl.pallas_call(
        body, name=name,
        out_shape=(pltpu.SemaphoreType.DMA((nsem,)), pltpu.SemaphoreType.DMA((nsem,)),
                   *[pltpu.HBM(a.shape, a.dtype) for a in arrays], _sds((8, LANES), F32)),
        in_specs=[HBM] * n, out_specs=(SEM, SEM, *[HBM] * n, pl.BlockSpec(memory_space=pltpu.VMEM)),
        input_output_aliases={i: 2 + i for i in range(n)},
        compiler_params=pltpu.CompilerParams(has_side_effects=EFFECT),
    )(*[_in_hbm(a) for a in arrays])
    return res[0], res[1], list(res[2:2 + n]), res[2 + n]


def _exchange_wait(name, arrays, send_sems, recv_sems, plan, after):
    n = len(arrays)

    def body(*refs):
        s_in, r_in = refs[n], refs[n + 1]
        sends, recvs = plan(refs[:n])
        for k, land in enumerate(recvs):
            _remote(land, land, s_in.at[k], r_in.at[k], sends[k][2]).wait_recv()
        for k, (src, _, dev) in enumerate(sends):
            _remote(src, src, s_in.at[k], r_in.at[k], dev).wait_send()

    res = pl.pallas_call(
        body, name=name,
        out_shape=tuple(pltpu.HBM(a.shape, a.dtype) for a in arrays),
        in_specs=[HBM] * n + [SEM, SEM, ANY], out_specs=tuple([HBM] * n),
        input_output_aliases={i: i for i in range(n)},
        compiler_params=pltpu.CompilerParams(has_side_effects=EFFECT),
    )(*arrays, send_sems, recv_sems, after)
    return list(res)


def _plan_pair(n, axes, shard_shapes):
    def plan(refs):
        g, ra = refs[:n], refs[n:]
        x, y, c, _ = _place()
        sib = (x, y, 1 - c)

        def pieces(ref, i, h):
            if axes[i] == 1:
                return [_half(ref, h)]
            return [_window(ref, 0, shard_shapes[i], s, h) for s in range(N_CHIPS)]

        sends, recvs = [], []
        for i in range(n):
            sends += [(src, dst, sib) for src, dst in zip(pieces(g[i], i, 1 - c), pieces(ra[i], i, 1 - c))]
            recvs += pieces(ra[i], i, c)
        return sends, recvs

    return plan, sum(1 if ax == 1 else N_CHIPS for ax in axes)


def _plan_chip(n):
    def plan(refs):
        p, rc = refs[:n], refs[n:]
        x, y, c, chips = _place()
        me = 2 * x + y
        sends, recvs = [], []
        for i in range(n):
            for chip in chips:
                cj = 2 * chip[0] + chip[1]
                sends.append((p[i].at[cj], rc[i].at[me], (*chip, c)))
                recvs.append(rc[i].at[cj])
        return sends, recvs

    return plan, 3 * n


def _plan_gather(n):
    def plan(refs):
        x, y, c, _ = _place()
        sib = (x, y, 1 - c)
        return ([(_half(r, c), _half(r, c), sib) for r in refs], [_half(r, 1 - c) for r in refs])

    return plan, n


def _pair_add(g, ra, ax, shard_shape, place, name):
    sr, sc = shard_shape
    hr = sr // 2
    wc = sc
    tr = _pick(hr, (256, 352, 128, 64, 32, 16))
    nr = hr // tr

    def body(p_ref, a_ref, b_ref, o_ref):
        o_ref[...] = (a_ref[...].astype(F32) + b_ref[...].astype(F32)).astype(o_ref.dtype)

    if ax == 1:
        src = pl.BlockSpec((tr, wc), lambda s, r, p_ref: (p_ref[0] * nr + r, s))
    else:
        src = pl.BlockSpec((tr, wc), lambda s, r, p_ref: (s * 2 * nr + p_ref[0] * nr + r, 0))
    return pl.pallas_call(
        body, name=name,
        grid_spec=pltpu.PrefetchScalarGridSpec(
            num_scalar_prefetch=1, grid=(N_CHIPS, nr), in_specs=[src, src],
            out_specs=pl.BlockSpec((None, tr, wc), lambda s, r, p_ref: (s, r, 0))),
        out_shape=_sds((N_CHIPS, hr, wc), BF16),
        compiler_params=_cparams("parallel", "parallel"),
    )(place, g, ra)


def _chip_add(part, rc, place, name):
    _, hr, wc = rc.shape
    tr = _pick(hr, (256, 352, 128, 64, 32, 16))
    nr = hr // tr

    def body(p_ref, own_ref, r1_ref, r2_ref, r3_ref, o_ref):
        acc = own_ref[...].astype(F32)
        for r_ref in (r1_ref, r2_ref, r3_ref):
            acc = acc + r_ref[...].astype(F32)
        o_ref[...] = acc

    def slot(k):
        return pl.BlockSpec((None, tr, wc), lambda r, p_ref: ((p_ref[1] + k) % N_CHIPS, r, 0))

    return pl.pallas_call(
        body, name=name,
        grid_spec=pltpu.PrefetchScalarGridSpec(
            num_scalar_prefetch=1, grid=(nr,), in_specs=[slot(0), slot(1), slot(2), slot(3)],
            out_specs=pl.BlockSpec((tr, wc), lambda r, p_ref: (p_ref[0] * nr + r, 0))),
        out_shape=_sds((2 * hr, wc), F32),
        compiler_params=_cparams("parallel"),
    )(place, part, rc, rc, rc)


N_DEV = 8


def _all_reduce_small(buf):
    r, cdim = buf.shape

    def body(x_ref, o_ref, land, send_sems, recv_sems):
        x, y, c, _ = _place()
        me = 4 * x + 2 * y + c
        land[me] = x_ref[...]
        sends = []
        for k in range(1, N_DEV):
            kx, ky, kc = (k >> 2) & 1, (k >> 1) & 1, k & 1
            peer = (1 - x if kx else x, 1 - y if ky else y, 1 - c if kc else c)
            cp = _remote(x_ref, land.at[me], send_sems.at[k - 1], recv_sems.at[k - 1], peer)
            cp.start()
            sends.append(cp)
        for k in range(1, N_DEV):
            kx, ky, kc = (k >> 2) & 1, (k >> 1) & 1, k & 1
            peer = (1 - x if kx else x, 1 - y if ky else y, 1 - c if kc else c)
            pidx = 4 * peer[0] + 2 * peer[1] + peer[2]
            _remote(land.at[pidx], land.at[pidx], send_sems.at[k - 1], recv_sems.at[k - 1], peer).wait_recv()
        for cp in sends:
            cp.wait_send()
        acc = land[0]
        for dev in range(1, N_DEV):
            acc = acc + land[dev]
        o_ref[...] = acc

    vm = pl.BlockSpec(memory_space=pltpu.VMEM)
    return pl.pallas_call(
        body, name="all_reduce_small", in_specs=[vm], out_specs=vm, out_shape=_sds((r, cdim), F32),
        scratch_shapes=[pltpu.VMEM((N_DEV, r, cdim), F32), pltpu.SemaphoreType.DMA((N_DEV - 1,)),
                        pltpu.SemaphoreType.DMA((N_DEV - 1,))],
    )(buf)


SMALL_ROWS = 16


def kernel(x, mem, g_mix, w_in, conv_w, attn_sinks, w_attn_proj, w_conv_proj, w_mix_out, g_xattn, g_mem, w_xq, w_xkv, w_xo, g_ffn, w_ffn_in, w_ffn_out, g_final, loss_target, m_g_mix, m_w_in, m_conv_w, m_attn_sinks, m_w_attn_proj, m_w_conv_proj, m_w_mix_out, m_g_xattn, m_g_mem, m_w_xq, m_w_xkv, m_w_xo, m_g_ffn, m_w_ffn_in, m_w_ffn_out, m_g_final, v_g_mix, v_w_in, v_conv_w, v_attn_sinks, v_w_attn_proj, v_w_conv_proj, v_w_mix_out, v_g_xattn, v_g_mem, v_w_xq, v_w_xkv, v_w_xo, v_g_ffn, v_w_ffn_in, v_w_ffn_out, v_g_final):
    w = dict(g_mix=g_mix, w_in=w_in[0], conv_w=conv_w[0], attn_sinks=attn_sinks, w_attn_proj=w_attn_proj[0],
             w_conv_proj=w_conv_proj[0], w_mix_out=w_mix_out[0], g_xattn=g_xattn, g_mem=g_mem, w_xq=w_xq[0],
             w_xkv=w_xkv[0], w_xo=w_xo[0], g_ffn=g_ffn, w_ffn_in=w_ffn_in[0], w_ffn_out=w_ffn_out[0],
             g_final=g_final[None])
    m = dict(g_mix=m_g_mix, w_in=m_w_in[0], conv_w=m_conv_w[0], attn_sinks=m_attn_sinks,
             w_attn_proj=m_w_attn_proj[0], w_conv_proj=m_w_conv_proj[0], w_mix_out=m_w_mix_out[0],
             g_xattn=m_g_xattn, g_mem=m_g_mem, w_xq=m_w_xq[0], w_xkv=m_w_xkv[0], w_xo=m_w_xo[0], g_ffn=m_g_ffn,
             w_ffn_in=m_w_ffn_in[0], w_ffn_out=m_w_ffn_out[0], g_final=m_g_final[None])
    v = dict(g_mix=v_g_mix, w_in=v_w_in[0], conv_w=v_conv_w[0], attn_sinks=v_attn_sinks,
             w_attn_proj=v_w_attn_proj[0], w_conv_proj=v_w_conv_proj[0], w_mix_out=v_w_mix_out[0],
             g_xattn=v_g_xattn, g_mem=v_g_mem, w_xq=v_w_xq[0], w_xkv=v_w_xkv[0], w_xo=v_w_xo[0], g_ffn=v_g_ffn,
             w_ffn_in=v_w_ffn_in[0], w_ffn_out=v_w_ffn_out[0], g_final=v_g_final[None])
    names = [nm for nm, _ in BIG]
    axes = [ax for _, ax in BIG]
    d = x.shape[2]
    cw = w["conv_w"].shape[1] * N_CHIPS
    chip = (2 * lax.axis_index("x") + lax.axis_index("y")).astype(jnp.int32)
    place = jnp.stack([lax.axis_index("c").astype(jnp.int32), chip])
    shard_shapes = [w[nm].shape for nm in names]

    def seed(nm, token):
        me1 = chip.reshape(1)
        if token is not None:
            me1 = me1 + token[0, 0].astype(jnp.int32)
        if nm == "conv_w":
            return _cast_to_full(w[nm], 1, me1, "place_conv_w", F32)
        return _cast_to_full(w[nm], dict(BIG)[nm], me1, "cast_" + nm)

    upd = {}

    def on_ready(nm, shard):
        upd[nm] = _adamw(w[nm], shard, m[nm], v[nm], "adamw_" + nm)
        grads[nm] = upd[nm][3]

    grads = {}
    wts = _Schedule(seed, dict(zip(names + ["conv_w"], axes + [1])),
                    dict(zip(names + ["conv_w"], shard_shapes + [w["conv_w"].shape])), place, on_ready)
    aw, cw = w["w_attn_proj"].shape[0], w["w_conv_proj"].shape[0]
    kvw = (w["w_in"].shape[1] * N_CHIPS - aw - 3 * cw - 2 * d) // 2
    grad_x, small = _local_step(
        x[0], mem[0], loss_target[0], w["g_mix"] + wts.token[0:1, 0:1], w["attn_sinks"], w["g_xattn"], w["g_mem"],
        w["g_ffn"], w["g_final"], (aw, cw, kvw), wts)

    pw = max(d, cw)

    def row(a):
        return jnp.pad(a, ((0, 0), (0, pw - a.shape[1])))

    gains = ("g_mix", "g_xattn", "g_mem", "g_ffn", "g_final")
    packed = jnp.concatenate(
        [row(small[nm]) for nm in gains] + [row(small["conv_w"]),
         row(jnp.concatenate([small["attn_sinks"], small["loss"]], axis=1)),
         jnp.zeros((SMALL_ROWS - 9, pw), F32)], axis=0)
    total = _all_reduce_small(packed)
    wts.mark("end", total)
    nsink = attn_sinks.shape[1]
    grads.update({nm: total[k:k + 1, :d] for k, nm in enumerate(gains)})
    grads.update(conv_w=lax.dynamic_slice(total, (5, chip * (cw // N_CHIPS)), (3, cw // N_CHIPS)),
                 attn_sinks=total[8:9, :nsink])
    loss = total[8, nsink]
    for nm in gains + ("conv_w", "attn_sinks"):
        upd[nm] = _adamw(w[nm], grads[nm], m[nm], v[nm], "adamw_" + nm)
    wts.mark("end2", upd["g_final"][0])

    order = ["g_mix", "w_in", "conv_w", "attn_sinks", "w_attn_proj", "w_conv_proj", "w_mix_out", "g_xattn", "g_mem",
             "w_xq", "w_xkv", "w_xo", "g_ffn", "w_ffn_in", "w_ffn_out", "g_final"]

    stacked = set(names) | {"conv_w"}

    def shaped(nm, a):
        if nm == "g_final":
            return a[0]
        return a[None] if nm in stacked else a

    outs = [loss, grad_x[None]]
    outs += [shaped(nm, grads[nm]) for nm in order]
    for k in range(3):
        outs += [shaped(nm, upd[nm][k]) for nm in order]
    return tuple(outs)
```

```python
import functools

import jax
import jax.numpy as jnp
from jax import lax
from jax.experimental import pallas as pl
from jax.experimental.pallas import tpu as pltpu

F32 = jnp.float32
BF16 = jnp.bfloat16

VMEM_LIMIT_BYTES = 56 * 1024 * 1024
LANES = 128
HEAD_DIM = 64
BLOCK = 128
X_HEAD_DIM = 128
ROPE_THETA = 10000.0
EPS = 1e-6
NEG = -1e30
ADAM_LR, ADAM_B1, ADAM_B2, ADAM_EPS, ADAM_WD, ADAM_STEP = 0.001, 0.9, 0.999, 1e-08, 0.01, 10
N_CHIPS = 4
MESH = pl.DeviceIdType.MESH
ANY = pl.BlockSpec(memory_space=pl.ANY)


def _pick(dim, prefs):
    for p in prefs:
        if dim % p == 0:
            return p
    return dim


def _cparams(*sem):
    return pltpu.CompilerParams(dimension_semantics=sem, vmem_limit_bytes=VMEM_LIMIT_BYTES)


def _sds(shape, dtype):
    return jax.ShapeDtypeStruct(shape, dtype)


def _sigmoid(v):
    return 0.5 * jnp.tanh(0.5 * v) + 0.5


MATMUL_VMEM_BUDGET = 46 * 1024 * 1024


def _tiles(mode, m, n):
    if mode == "tn" and m % 1024 != 0:
        return _pick(m, (512, 256, 128)), _pick(n, (1024, 512, 256, 128)), True
    return _pick(m, (1024, 512, 256, 128)), _pick(n, (512, 256, 128)), False


def _k_parts(m, n, k):
    tm, tn = _pick(m, (1024, 512, 256, 128)), _pick(n, (512, 256, 128))
    for parts in range(1, k // LANES + 1):
        if k % (parts * LANES) == 0 and 4 * (tm + tn) * (k // parts) + 16 * tm * tn <= MATMUL_VMEM_BUDGET:
            return parts
    return k // LANES


def _matmul(a, b, *, mode, out_dtype, name, res=None, dep=None, cols=None):
    if mode == "nn":
        (m, k), (k2, n) = a.shape, b.shape
    elif mode == "nt":
        (m, k), (n, k2) = (a.shape[-2], a.shape[-1] * (a.shape[0] if a.ndim == 3 else 1)), b.shape
    else:
        (k, m), (k2, n) = a.shape, (b.shape[-2], b.shape[-1] * (b.shape[0] if b.ndim == 3 else 1))
    assert k == k2, (a.shape, b.shape, mode)
    assert cols is None or mode != "nt"
    cols = cols or (0, n)
    parts = 1 if mode == "tn" else (a.shape[0] if a.ndim == 3 else _k_parts(m, cols[1], k))
    for p in range(parts):
        last = p == parts - 1
        res = _matmul_slice(a, b, mode=mode, out_dtype=out_dtype if last else F32, res=res, dep=dep, cols=cols,
                            kslice=(p, parts), name=name + ("_k%d" % p if parts > 1 else ""))
    return res


def _matmul_slice(a, b, *, mode, out_dtype, name, res, dep, kslice, cols):
    part, parts = kslice
    (m, k) = a.shape[-2:] if mode != "tn" else a.shape[::-1]
    col0, n = cols
    tk = k if a.ndim == 3 else k // parts
    tm, tn, swap = _tiles(mode, m, b.shape[2] if (mode == "tn" and b.ndim == 3) else n)
    while col0 % tn:
        tn //= 2
    joff = col0 // tn
    dims = {"nn": (((1,), (0,)), ((), ())), "nt": (((1,), (1,)), ((), ())), "tn": (((0,), (0,)), ((), ()))}[mode]
    has_res = res is not None
    has_dep = dep is not None

    def body(*refs):
        a_ref, b_ref = refs[0], refs[1]
        o_ref = refs[2 + has_res + has_dep]
        val = lax.dot_general(a_ref[...], b_ref[...], dims, preferred_element_type=F32)
        if has_res:
            val = val + refs[2][...]
        o_ref[...] = val.astype(o_ref.dtype)

    def spec(shape, f):
        if swap:
            return pl.BlockSpec(shape, lambda j, i: f(i, j))
        return pl.BlockSpec(shape, f)

    if mode == "tn":
        a_spec = spec((tk, tm), lambda i, j: (part, i))
    elif a.ndim == 3:
        a_spec = spec((None, tm, tk), lambda i, j: (part, i, 0))
    else:
        a_spec = spec((tm, tk), lambda i, j: (i, part))
    if mode == "nt":
        b_spec = spec((tn, tk), lambda i, j: (j, part))
    elif b.ndim == 3:
        per = b.shape[2] // tn
        b_spec = spec((None, tk, tn), lambda i, j: (j // per, part, j % per))
    else:
        b_spec = spec((tk, tn), lambda i, j: (part, joff + j))
    o_spec = spec((tm, tn), lambda i, j: (i, j))
    return pl.pallas_call(
        body,
        name=name,
        grid=(n // tn, m // tm) if swap else (m // tm, n // tn),
        in_specs=[a_spec, b_spec] + ([o_spec] if has_res else []) + ([ANY] if has_dep else []),
        out_specs=o_spec,
        out_shape=_sds((m, n), out_dtype),
        compiler_params=_cparams("parallel", "parallel"),
    )(*([a, b] + ([res] if has_res else []) + ([dep] if has_dep else [])))


def _rms_fwd(x, g, name):
    t, d = x.shape
    tm = _pick(t, (512, 256, 128))

    def body(x_ref, g_ref, o_ref):
        xf = x_ref[...]
        r = lax.rsqrt(jnp.mean(xf * xf, axis=-1, keepdims=True) + EPS)
        o_ref[...] = (xf * r * g_ref[...]).astype(o_ref.dtype)

    row = pl.BlockSpec((tm, d), lambda i: (i, 0))
    return pl.pallas_call(
        body, name=name, grid=(t // tm,),
        in_specs=[row, pl.BlockSpec((1, d), lambda i: (0, 0))],
        out_specs=row, out_shape=_sds((t, d), BF16),
        compiler_params=_cparams("parallel"),
    )(x, g)


def _rms_bwd_math(xf, g, du):
    r = lax.rsqrt(jnp.mean(xf * xf, axis=-1, keepdims=True) + EPS)
    xh = xf * r
    gdy = g * du
    dx = r * (gdy - xh * jnp.mean(gdy * xh, axis=-1, keepdims=True))
    dg = jnp.sum(du * xh, axis=0, keepdims=True)
    return dx, dg


def _rms_bwd(x, g, du, dh, name):
    t, d = x.shape
    tm = _pick(t, (256, 128))
    has_dh = dh is not None

    def body(*refs):
        x_ref, g_ref, du_ref = refs[0], refs[1], refs[2]
        o_ref, ob_ref, dg_ref = refs[3 + has_dh:]
        dx, dg = _rms_bwd_math(x_ref[...], g_ref[...], du_ref[...].astype(F32))
        if has_dh:
            dx = dx + refs[3][...]
        o_ref[...] = dx
        ob_ref[...] = dx.astype(BF16)

        @pl.when(pl.program_id(0) == 0)
        def _():
            dg_ref[...] = dg

        @pl.when(pl.program_id(0) > 0)
        def _():
            dg_ref[...] += dg

    row = pl.BlockSpec((tm, d), lambda i: (i, 0))
    vec = pl.BlockSpec((1, d), lambda i: (0, 0))
    return pl.pallas_call(
        body, name=name, grid=(t // tm,),
        in_specs=[row, vec, row] + ([row] if has_dh else []),
        out_specs=[row, row, vec],
        out_shape=[_sds((t, d), F32), _sds((t, d), BF16), _sds((1, d), F32)],
        compiler_params=_cparams("arbitrary"),
    )(*([x, g, du] + ([dh] if has_dh else [])))


ROW_TILE = 256


def _rows_matmul(a, b, *, mode, kslice, res, rows_in, vec, epilogue, outs, name, dep):
    part, parts = kslice
    t = a.shape[-2]
    n = b.shape[1] if mode == "nn" else b.shape[0]
    kc = a.shape[-1] if a.ndim == 3 else a.shape[1] // parts
    tm = _pick(t, (ROW_TILE, 128))
    dims = (((1,), (0,)), ((), ())) if mode == "nn" else (((1,), (1,)), ((), ()))
    has_res = res is not None
    deps = [] if dep is None else [dep]
    n_in = 2 + has_res + len(rows_in) + 1 + len(deps)

    def body(*refs):
        val = lax.dot_general(refs[0][...], refs[1][...], dims, preferred_element_type=F32)
        if has_res:
            val = val + refs[2][...]
        row_refs = refs[2 + has_res:2 + has_res + len(rows_in)]
        epilogue(val, row_refs, refs[2 + has_res + len(rows_in)], refs[n_in:], pl.program_id(0) == 0)

    if a.ndim == 3:
        a_spec = pl.BlockSpec((None, tm, kc), lambda i: (part, i, 0))
    else:
        a_spec = pl.BlockSpec((tm, kc), lambda i: (i, part))
    if mode == "nn":
        b_spec = pl.BlockSpec((kc, n), lambda i: (part, 0), pipeline_mode=pl.Buffered(1))
    else:
        b_spec = pl.BlockSpec((n, kc), lambda i: (0, part), pipeline_mode=pl.Buffered(1))
    row = pl.BlockSpec((tm, n), lambda i: (i, 0))
    kinds = {"row": row, "vec": pl.BlockSpec((1, n), lambda i: (0, 0)), "lane": pl.BlockSpec((1, LANES), lambda i: (0, 0))}
    shapes = {"row": (t, n), "vec": (1, n), "lane": (1, LANES)}
    return pl.pallas_call(
        body, name=name, grid=(t // tm,),
        in_specs=[a_spec, b_spec] + [row] * (has_res + len(rows_in)) + [kinds["vec"]] + [ANY] * len(deps),
        out_specs=[kinds[o[0]] for o in outs],
        out_shape=[_sds(shapes[o[0]], o[1] if len(o) > 1 else F32) for o in outs],
        compiler_params=_cparams("arbitrary"),
    )(*([a, b] + ([res] if has_res else []) + list(rows_in) + [vec] + deps))


def _accumulate(ref, value, first):
    @pl.when(first)
    def _():
        ref[...] = value

    @pl.when(jnp.logical_not(first))
    def _():
        ref[...] += value


def _ep_norm(val, rows, g_ref, outs, first):
    h_ref, u_ref = outs
    h_ref[...] = val
    r = lax.rsqrt(jnp.mean(val * val, axis=-1, keepdims=True) + EPS)
    u_ref[...] = (val * r * g_ref[...]).astype(u_ref.dtype)


def _ep_loss(val, rows, g_ref, outs, first):
    o_ref, ob_ref, dg_ref, l_ref = outs
    gv = g_ref[...]
    r = lax.rsqrt(jnp.mean(val * val, axis=-1, keepdims=True) + EPS)
    err = val * r * gv - rows[0][...]
    part = 0.5 * jnp.sum(jnp.mean(err * err, axis=-1, keepdims=True), axis=0, keepdims=True)
    dx, dg = _rms_bwd_math(val, gv, err * (1.0 / val.shape[1]))
    o_ref[...] = dx
    ob_ref[...] = dx.astype(BF16)
    _accumulate(dg_ref, dg, first)
    _accumulate(l_ref, jnp.broadcast_to(part, (1, LANES)), first)


def _ep_rms_bwd(val, rows, g_ref, outs, first):
    o_ref, ob_ref, dg_ref = outs
    dx, dg = _rms_bwd_math(rows[0][...], g_ref[...], val)
    dx = dx + rows[1][...]
    o_ref[...] = dx
    ob_ref[...] = dx.astype(BF16)
    _accumulate(dg_ref, dg, first)


def _rope_tables(t):
    half = HEAD_DIM // 2
    inv_freq = ROPE_THETA ** (-jnp.arange(half, dtype=F32) / half)
    ang = jnp.arange(t, dtype=F32)[:, None] * inv_freq[None, :]
    cos = jnp.cos(ang)
    sin = jnp.sin(ang)
    reps = LANES // HEAD_DIM
    cos_t = jnp.tile(jnp.concatenate([cos, cos], axis=1), (1, reps))
    sin_t = jnp.tile(jnp.concatenate([-sin, sin], axis=1), (1, reps))
    return cos_t, sin_t


def _rope(v, cos, sin):
    w = v.shape[1]
    c = jnp.tile(cos, (1, w // LANES))
    s = jnp.tile(sin, (1, w // LANES))
    lane = lax.broadcasted_iota(jnp.int32, v.shape, 1)
    first = (lane % HEAD_DIM) < (HEAD_DIM // 2)
    partner = jnp.where(first, pltpu.roll(v, w - HEAD_DIM // 2, 1), pltpu.roll(v, HEAD_DIM // 2, 1))
    return v * c + partner * s


def _heads(v, count):
    return jnp.concatenate([v[:, i * HEAD_DIM:(i + 1) * HEAD_DIM] for i in range(count)], axis=0)


def _unheads(v, count):
    r = v.shape[0] // count
    return jnp.concatenate([v[g * r:(g + 1) * r] for g in range(count)], axis=1)


def _rope_qkv(proj, cos, sin, aw, kvw):
    t = proj.shape[0]
    nkv = kvw // HEAD_DIM
    koff = aw // kvw
    tm = _pick(t, (256, 128))

    def body(q_ref, k_ref, v_ref, c_ref, s_ref, qo_ref, ko_ref, vo_ref):
        c, s = c_ref[...], s_ref[...]
        qo_ref[...] = _rope(q_ref[...].astype(F32), c, s).astype(BF16)
        k = _rope(k_ref[...].astype(F32), c, s).astype(BF16)
        v = v_ref[...].astype(BF16)
        for h in range(nkv):
            ko_ref[h] = k[:, h * HEAD_DIM:(h + 1) * HEAD_DIM]
            vo_ref[h] = v[:, h * HEAD_DIM:(h + 1) * HEAD_DIM]

    def row(w, j):
        return pl.BlockSpec((tm, w), lambda i: (i, j))

    hm = pl.BlockSpec((nkv, tm, HEAD_DIM), lambda i: (0, i, 0))
    return pl.pallas_call(
        body, name="rope_qkv", grid=(t // tm,),
        in_specs=[row(aw, 0), row(kvw, koff), row(kvw, koff + 1), row(LANES, 0), row(LANES, 0)],
        out_specs=[row(aw, 0), hm, hm],
        out_shape=[_sds((t, aw), BF16), _sds((nkv, t, HEAD_DIM), BF16), _sds((nkv, t, HEAD_DIM), BF16)],
        compiler_params=_cparams("parallel"),
    )(proj, proj, proj, cos, sin)


def _attn_probs(qs, kb, n, h, qpk, sinks_ref):
    s = lax.dot_general(qs, kb, (((1,), (1,)), ((), ())), preferred_element_type=F32) * (HEAD_DIM ** -0.5)
    qi = lax.broadcasted_iota(jnp.int32, (BLOCK, 2 * BLOCK), 0)
    kc = lax.broadcasted_iota(jnp.int32, (BLOCK, 2 * BLOCK), 1)
    valid = (kc > qi) & (kc <= qi + BLOCK) & ((kc >= BLOCK) | (n > 0))
    bias = jnp.tile(jnp.where(valid, 0.0, NEG).astype(F32), (qpk, 1))
    s = s + bias
    rowg = lax.broadcasted_iota(jnp.int32, (qpk * BLOCK, 1), 0) // BLOCK
    sink = jnp.zeros((qpk * BLOCK, 1), F32)
    for g in range(qpk):
        sink = jnp.where(rowg == g, sinks_ref[0, h * qpk + g], sink)
    m = jnp.maximum(jnp.max(s, axis=-1, keepdims=True), sink)
    e = jnp.exp(s - m)
    es = jnp.exp(sink - m)
    inv = 1.0 / (jnp.sum(e, axis=-1, keepdims=True) + es)
    return e * inv, es * inv, rowg


HEADS_PER_STEP = 4


def _attn_specs(qw, hp):
    def head(f):
        return pl.BlockSpec((hp, BLOCK, HEAD_DIM), lambda n, h: (h, f(n), 0))

    cur = lambda n: n
    prev = lambda n: jnp.maximum(n - 1, 0)
    return [pl.BlockSpec((BLOCK, hp * qw), lambda n, h: (n, h)), head(cur), head(prev), head(cur), head(prev),
            pl.BlockSpec(memory_space=pltpu.SMEM)]


def _attn_fwd(q_r, k_r, v_h, sinks):
    t, aw = q_r.shape
    nkv = k_r.shape[0]
    qpk = aw // (nkv * HEAD_DIM)
    qw = qpk * HEAD_DIM
    hp = HEADS_PER_STEP if nkv % HEADS_PER_STEP == 0 else 1

    def body(q_ref, kc_ref, kp_ref, vc_ref, vp_ref, sinks_ref, o_ref):
        n, hg = pl.program_id(0), pl.program_id(1)
        outs = []
        for j in range(hp):
            kb = jnp.concatenate([kp_ref[j], kc_ref[j]], axis=0)
            vb = jnp.concatenate([vp_ref[j], vc_ref[j]], axis=0)
            qs = _heads(q_ref[:, j * qw:(j + 1) * qw], qpk)
            p, _, _ = _attn_probs(qs, kb, n, hg * hp + j, qpk, sinks_ref)
            outs.append(_unheads(jnp.dot(p.astype(BF16), vb, preferred_element_type=F32), qpk))
        o_ref[...] = jnp.concatenate(outs, axis=1).astype(o_ref.dtype)

    return pl.pallas_call(
        body, name="attn_fwd", grid=(t // BLOCK, nkv // hp),
        in_specs=_attn_specs(qw, hp),
        out_specs=pl.BlockSpec((BLOCK, hp * qw), lambda n, h: (n, h)),
        out_shape=_sds((t, aw), BF16),
        compiler_params=_cparams("parallel", "parallel"),
    )(q_r, k_r, k_r, v_h, v_h, sinks)


def _attn_bwd(q_r, k_r, v_h, sinks, o, do):
    t, aw = q_r.shape
    nkv = k_r.shape[0]
    qpk = aw // (nkv * HEAD_DIM)
    qw = qpk * HEAD_DIM
    hp = HEADS_PER_STEP if nkv % HEADS_PER_STEP == 0 else 1
    scale = HEAD_DIM ** -0.5

    def body(q_ref, kc_ref, kp_ref, vc_ref, vp_ref, sinks_ref, o_ref, do_ref,
             dq_ref, dkc_ref, dkp_ref, dvc_ref, dvp_ref, ds_ref):
        n, hg = pl.program_id(0), pl.program_id(1)
        lane = lax.broadcasted_iota(jnp.int32, (8, LANES), 1)
        row0 = lax.broadcasted_iota(jnp.int32, (8, LANES), 0) == 0
        dsink = jnp.zeros((8, LANES), F32)
        dqs = []
        for j in range(hp):
            h = hg * hp + j
            cols = slice(j * qw, (j + 1) * qw)
            kb = jnp.concatenate([kp_ref[j], kc_ref[j]], axis=0)
            vb = jnp.concatenate([vp_ref[j], vc_ref[j]], axis=0)
            qs = _heads(q_ref[:, cols], qpk)
            dos = _heads(do_ref[:, cols], qpk)
            p, psink, rowg = _attn_probs(qs, kb, n, h, qpk, sinks_ref)
            pb = p.astype(BF16)
            delta = jnp.sum(dos.astype(F32) * _heads(o_ref[:, cols], qpk).astype(F32), axis=-1, keepdims=True)
            dv = lax.dot_general(pb, dos, (((0,), (0,)), ((), ())), preferred_element_type=F32)
            dp = lax.dot_general(dos, vb, (((1,), (1,)), ((), ())), preferred_element_type=F32)
            dsc = (p * (dp - delta)).astype(BF16)
            dqs.append(_unheads(jnp.dot(dsc, kb, preferred_element_type=F32) * scale, qpk))
            dk = lax.dot_general(dsc, qs, (((0,), (0,)), ((), ())), preferred_element_type=F32) * scale
            dkp_ref[j] = dk[:BLOCK]
            dkc_ref[j] = dk[BLOCK:]
            dvp_ref[j] = dv[:BLOCK]
            dvc_ref[j] = dv[BLOCK:]
            sink_term = psink * delta
            for g in range(qpk):
                val = -jnp.sum(jnp.where(rowg == g, sink_term, 0.0))
                dsink = jnp.where(row0 & (lane == h * qpk + g), val, dsink)
        dq_ref[...] = jnp.concatenate(dqs, axis=1).astype(dq_ref.dtype)
        first = (n == 0) & (hg == 0)

        @pl.when(first)
        def _():
            ds_ref[...] = dsink

        @pl.when(jnp.logical_not(first))
        def _():
            ds_ref[...] += dsink

    qblk = pl.BlockSpec((BLOCK, hp * qw), lambda n, h: (n, h))
    kvblk = pl.BlockSpec((hp, BLOCK, HEAD_DIM), lambda n, h: (h, n, 0))
    return pl.pallas_call(
        body, name="attn_bwd", grid=(t // BLOCK, nkv // hp),
        in_specs=_attn_specs(qw, hp) + [qblk, qblk],
        out_specs=[qblk, kvblk, kvblk, kvblk, kvblk, pl.BlockSpec((8, LANES), lambda n, h: (0, 0))],
        out_shape=[_sds((t, aw), BF16)] + [_sds((nkv, t, HEAD_DIM), F32)] * 4 + [_sds((8, LANES), F32)],
        compiler_params=_cparams("arbitrary", "arbitrary"),
    )(q_r, k_r, k_r, v_h, v_h, sinks, o, do)


HALO = 16


def _shift_down(v, k, halo):
    rows = lax.broadcasted_iota(jnp.int32, v.shape, 0)
    out = pltpu.roll(v, k, 0)
    for r in range(k):
        out = jnp.where(rows == r, halo[HALO - k + r:HALO - k + r + 1, :], out)
    return out


def _shift_up(v, k, halo):
    tm = v.shape[0]
    rows = lax.broadcasted_iota(jnp.int32, v.shape, 0)
    out = pltpu.roll(v, tm - k, 0)
    for r in range(k):
        out = jnp.where(rows == tm - k + r, halo[r:r + 1, :], out)
    return out


def _conv_fwd(proj, conv_w, zoff, cw, cb):
    t = proj.shape[0]
    tm = _pick(t, (512, 256, 128))
    zb, nb = zoff // cb, cw // cb
    hb = tm // HALO

    def body(z_ref, gb_ref, gc_ref, zp_ref, gcp_ref, w_ref, o_ref):
        i = pl.program_id(0)
        cz = gc_ref[...].astype(F32) * z_ref[...].astype(F32)
        czp = gcp_ref[...].astype(F32) * zp_ref[...].astype(F32) * (i > 0).astype(F32)
        w = w_ref[...]
        y = w[0:1] * _shift_down(cz, 2, czp) + w[1:2] * _shift_down(cz, 1, czp) + w[2:3] * cz
        o_ref[...] = (gb_ref[...].astype(F32) * y).astype(o_ref.dtype)

    def col(k):
        return pl.BlockSpec((tm, cb), lambda i, j: (i, zb + k * nb + j))

    def halo(k):
        return pl.BlockSpec((HALO, cb), lambda i, j: (jnp.maximum(i * hb - 1, 0), zb + k * nb + j))

    return pl.pallas_call(
        body, name="conv_fwd", grid=(t // tm, nb),
        in_specs=[col(0), col(1), col(2), halo(0), halo(2), pl.BlockSpec((3, cb), lambda i, j: (0, j))],
        out_specs=pl.BlockSpec((tm, cb), lambda i, j: (i, j)),
        out_shape=_sds((t, cw), BF16),
        compiler_params=_cparams("parallel", "parallel"),
    )(proj, proj, proj, proj, proj, conv_w)


def _conv_bwd(proj, conv_w, dco, zoff, cw, cb):
    t = proj.shape[0]
    tm = _pick(t, (512, 256, 128))
    zb, nb = zoff // cb, cw // cb
    hb = tm // HALO
    nt = t // tm

    def body(z_ref, gb_ref, gc_ref, zp_ref, gcp_ref, gbn_ref, w_ref, d_ref, dn_ref, dz_ref, dgb_ref, dgc_ref, dw_ref):
        i = pl.program_id(1)
        z, gb, gc = z_ref[...].astype(F32), gb_ref[...].astype(F32), gc_ref[...].astype(F32)
        d = d_ref[...].astype(F32)
        cz = gc * z
        czp = gcp_ref[...].astype(F32) * zp_ref[...].astype(F32) * (i > 0).astype(F32)
        w = w_ref[...]
        cz1 = _shift_down(cz, 1, czp)
        cz2 = _shift_down(cz, 2, czp)
        y = w[0:1] * cz2 + w[1:2] * cz1 + w[2:3] * cz
        dgb_ref[...] = (d * y).astype(dgb_ref.dtype)
        dy = d * gb
        dyn = dn_ref[...].astype(F32) * gbn_ref[...].astype(F32) * (i < nt - 1).astype(F32)
        dcz = w[2:3] * dy + w[1:2] * _shift_up(dy, 1, dyn) + w[0:1] * _shift_up(dy, 2, dyn)
        dgc_ref[...] = (dcz * z).astype(dgc_ref.dtype)
        dz_ref[...] = (dcz * gc).astype(dz_ref.dtype)
        rows = lax.broadcasted_iota(jnp.int32, (8, cb), 0)
        dw = jnp.zeros((8, cb), F32)
        for r, tap in enumerate((cz2, cz1, cz)):
            dw = jnp.where(rows == r, jnp.sum(dy * tap, axis=0, keepdims=True), dw)

        @pl.when(i == 0)
        def _():
            dw_ref[...] = dw

        @pl.when(i > 0)
        def _():
            dw_ref[...] += dw

    def col(k):
        return pl.BlockSpec((tm, cb), lambda j, i: (i, zb + k * nb + j))

    def halo_prev(k):
        return pl.BlockSpec((HALO, cb), lambda j, i: (jnp.maximum(i * hb - 1, 0), zb + k * nb + j))

    own = pl.BlockSpec((tm, cb), lambda j, i: (i, j))
    nxt = lambda i: jnp.minimum((i + 1) * hb, t // HALO - 1)
    return pl.pallas_call(
        body, name="conv_bwd", grid=(nb, nt),
        in_specs=[col(0), col(1), col(2), halo_prev(0), halo_prev(2),
                  pl.BlockSpec((HALO, cb), lambda j, i: (nxt(i), zb + nb + j)),
                  pl.BlockSpec((3, cb), lambda j, i: (0, j)), own,
                  pl.BlockSpec((HALO, cb), lambda j, i: (nxt(i), j))],
        out_specs=[own, own, own, pl.BlockSpec((8, cb), lambda j, i: (0, j))],
        out_shape=[_sds((t, cw), BF16)] * 3 + [_sds((8, cw), F32)],
        compiler_params=_cparams("parallel", "arbitrary"),
    )(proj, proj, proj, proj, proj, proj, conv_w, dco, dco)


def _yconv_merge(conv_o, w_cp, ya, proj, goff, cb, dep, name):
    t, k = conv_o.shape
    d = w_cp.shape[1]
    tm = _pick(t, (1024, 512, 256, 128))
    gb_, nb = goff // cb, d // cb
    deps = [] if dep is None else [dep]

    def body(a_ref, w_ref, ya_ref, ga_ref, gc_ref, *rest):
        yc_ref, o_ref = rest[len(deps):]
        yc = jnp.dot(a_ref[...], w_ref[...], preferred_element_type=F32)
        f = lambda r: r[...].astype(F32)
        yc_ref[...] = yc.astype(yc_ref.dtype)
        o_ref[...] = (_sigmoid(f(ga_ref)) * f(ya_ref) + _sigmoid(f(gc_ref)) * yc).astype(o_ref.dtype)

    own = pl.BlockSpec((tm, cb), lambda i, j: (i, j))
    return pl.pallas_call(
        body, name=name, grid=(t // tm, nb),
        in_specs=[pl.BlockSpec((tm, k), lambda i, j: (i, 0)), pl.BlockSpec((k, cb), lambda i, j: (0, j)), own,
                  pl.BlockSpec((tm, cb), lambda i, j: (i, gb_ + j)),
                  pl.BlockSpec((tm, cb), lambda i, j: (i, gb_ + nb + j))] + [ANY] * len(deps),
        out_specs=[own, own], out_shape=[_sds((t, d), BF16)] * 2,
        compiler_params=_cparams("parallel", "parallel"),
    )(conv_o, w_cp, ya, proj, proj, *deps)


def _dmerged_split(dh, w_mo, proj, ya, yc, goff, cb, dep, name):
    t, k = dh.shape
    d = w_mo.shape[0]
    tm = _pick(t, (1024, 512, 256, 128))
    gb_, nb = goff // cb, d // cb
    deps = [] if dep is None else [dep]

    def body(a_ref, w_ref, ga_ref, gc_ref, ya_ref, yc_ref, *rest):
        dya_ref, dyc_ref, dga_ref, dgc_ref = rest[len(deps):]
        dmv = lax.dot_general(a_ref[...], w_ref[...], (((1,), (1,)), ((), ())), preferred_element_type=F32)
        sa = _sigmoid(ga_ref[...].astype(F32))
        sc = _sigmoid(gc_ref[...].astype(F32))
        dya_ref[...] = (dmv * sa).astype(BF16)
        dyc_ref[...] = (dmv * sc).astype(BF16)
        dga_ref[...] = (dmv * ya_ref[...].astype(F32) * sa * (1.0 - sa)).astype(BF16)
        dgc_ref[...] = (dmv * yc_ref[...].astype(F32) * sc * (1.0 - sc)).astype(BF16)

    own = pl.BlockSpec((tm, cb), lambda i, j: (i, j))
    return pl.pallas_call(
        body, name=name, grid=(t // tm, nb),
        in_specs=[pl.BlockSpec((tm, k), lambda i, j: (i, 0)), pl.BlockSpec((cb, k), lambda i, j: (j, 0)),
                  pl.BlockSpec((tm, cb), lambda i, j: (i, gb_ + j)),
                  pl.BlockSpec((tm, cb), lambda i, j: (i, gb_ + nb + j)), own, own] + [ANY] * len(deps),
        out_specs=[own] * 4, out_shape=[_sds((t, d), BF16)] * 4,
        compiler_params=_cparams("parallel", "parallel"),
    )(dh, w_mo, proj, proj, ya, yc, *deps)


def _assemble_dproj(dq, dkc, dkp, dvc, dvp, cos, sin, dz, dgb, dgc, dga, dgg):
    t, aw = dq.shape
    nkv, cw, d = dkc.shape[0], dz.shape[1], dga.shape[1]
    kvw = nkv * HEAD_DIM
    nblk = t // BLOCK
    width = aw + 2 * kvw + 3 * cw + 2 * d

    def body(dq_ref, dkc_ref, dkp_ref, dvc_ref, dvp_ref, c_ref, s_ref, dz_ref, dgb_ref, dgc_ref, dga_ref, dgg_ref,
             o_ref):
        keep = (pl.program_id(0) < nblk - 1).astype(F32)
        c, s = c_ref[...], s_ref[...]
        dk = jnp.concatenate([dkc_ref[h] + dkp_ref[h] * keep for h in range(nkv)], axis=1)
        dv = jnp.concatenate([dvc_ref[h] + dvp_ref[h] * keep for h in range(nkv)], axis=1)
        o_ref[...] = jnp.concatenate(
            [_rope(dq_ref[...].astype(F32), c, -s).astype(BF16), _rope(dk, c, -s).astype(BF16), dv.astype(BF16),
             dz_ref[...], dgb_ref[...], dgc_ref[...], dga_ref[...], dgg_ref[...]], axis=1)

    def cur(w):
        return pl.BlockSpec((BLOCK, w), lambda n: (n, 0))

    head_cur = pl.BlockSpec((nkv, BLOCK, HEAD_DIM), lambda n: (0, n, 0))
    head_nxt = pl.BlockSpec((nkv, BLOCK, HEAD_DIM), lambda n: (0, jnp.minimum(n + 1, nblk - 1), 0))
    return pl.pallas_call(
        body, name="assemble_dproj", grid=(nblk,),
        in_specs=[cur(aw), head_cur, head_nxt, head_cur, head_nxt, cur(LANES), cur(LANES),
                  cur(cw), cur(cw), cur(cw), cur(d), cur(d)],
        out_specs=cur(width), out_shape=_sds((t, width), BF16),
        compiler_params=_cparams("parallel"),
    )(dq, dkc, dkp, dvc, dvp, cos, sin, dz, dgb, dgc, dga, dgg)


def _xattn_probs(qh, kh):
    s = lax.dot_general(qh, kh, (((1,), (1,)), ((), ())), preferred_element_type=F32) * (X_HEAD_DIM ** -0.5)
    e = jnp.exp(s - jnp.max(s, axis=-1, keepdims=True))
    return e * (1.0 / jnp.sum(e, axis=-1, keepdims=True))


def _xattn_fwd(xq, kv):
    t, xw = xq.shape
    mt = kv.shape[0]
    tm = _pick(t, (512, 256, 128))

    def body(q_ref, kv_ref, o_ref):
        outs = []
        for hd in range(xw // X_HEAD_DIM):
            hs = slice(hd * X_HEAD_DIM, (hd + 1) * X_HEAD_DIM)
            vs = slice(xw + hd * X_HEAD_DIM, xw + (hd + 1) * X_HEAD_DIM)
            p = _xattn_probs(q_ref[:, hs], kv_ref[:, hs])
            outs.append(jnp.dot(p.astype(BF16), kv_ref[:, vs], preferred_element_type=F32))
        o_ref[...] = jnp.concatenate(outs, axis=1).astype(o_ref.dtype)

    return pl.pallas_call(
        body, name="xattn_fwd", grid=(t // tm,),
        in_specs=[pl.BlockSpec((tm, xw), lambda i: (i, 0)), pl.BlockSpec((mt, 2 * xw), lambda i: (0, 0))],
        out_specs=pl.BlockSpec((tm, xw), lambda i: (i, 0)), out_shape=_sds((t, xw), BF16),
        compiler_params=_cparams("parallel"),
    )(xq, kv)


def _xattn_bwd(xq, kv, do):
    t, xw = xq.shape
    mt = kv.shape[0]
    tm = _pick(t, (512, 256, 128))
    scale = X_HEAD_DIM ** -0.5

    def body(q_ref, kv_ref, do_ref, dq_ref, dkv_ref):
        dqs, dks, dvs = [], [], []
        for hd in range(xw // X_HEAD_DIM):
            hs = slice(hd * X_HEAD_DIM, (hd + 1) * X_HEAD_DIM)
            vs = slice(xw + hd * X_HEAD_DIM, xw + (hd + 1) * X_HEAD_DIM)
            qh, kh, vh, doh = q_ref[:, hs], kv_ref[:, hs], kv_ref[:, vs], do_ref[:, hs]
            p = _xattn_probs(qh, kh)
            pb = p.astype(BF16)
            o = jnp.dot(pb, vh, preferred_element_type=F32)
            delta = jnp.sum(doh.astype(F32) * o, axis=-1, keepdims=True)
            dvs.append(lax.dot_general(pb, doh, (((0,), (0,)), ((), ())), preferred_element_type=F32))
            dp = lax.dot_general(doh, vh, (((1,), (1,)), ((), ())), preferred_element_type=F32)
            dsc = (p * (dp - delta)).astype(BF16)
            dqs.append(jnp.dot(dsc, kh, preferred_element_type=F32) * scale)
            dks.append(lax.dot_general(dsc, qh, (((0,), (0,)), ((), ())), preferred_element_type=F32) * scale)
        dq_ref[...] = jnp.concatenate(dqs, axis=1).astype(dq_ref.dtype)
        dkv = jnp.concatenate(dks + dvs, axis=1)

        @pl.when(pl.program_id(0) == 0)
        def _():
            dkv_ref[...] = dkv

        @pl.when(pl.program_id(0) > 0)
        def _():
            dkv_ref[...] += dkv

    row = pl.BlockSpec((tm, xw), lambda i: (i, 0))
    whole = pl.BlockSpec((mt, 2 * xw), lambda i: (0, 0))
    return pl.pallas_call(
        body, name="xattn_bwd", grid=(t // tm,),
        in_specs=[row, whole, row], out_specs=[row, whole],
        out_shape=[_sds((t, xw), BF16), _sds((mt, 2 * xw), F32)],
        compiler_params=_cparams("arbitrary"),
    )(xq, kv, do)


def _ffn_in_swiglu(u, w, dep, name):
    t, d = u.shape
    f = w.shape[1] // 2
    tm, tn, _ = _tiles("nn", t, f)
    nf = f // tn
    deps = [] if dep is None else [dep]

    def body(u_ref, wa_ref, wb_ref, *rest):
        h_ref, o_ref = rest[len(deps):]
        a = jnp.dot(u_ref[...], wa_ref[...], preferred_element_type=F32)
        b = jnp.dot(u_ref[...], wb_ref[...], preferred_element_type=F32)
        h_ref[0] = a.astype(h_ref.dtype)
        h_ref[1] = b.astype(h_ref.dtype)
        o_ref[...] = (a * _sigmoid(a) * b).astype(o_ref.dtype)

    return pl.pallas_call(
        body, name=name, grid=(t // tm, nf),
        in_specs=[pl.BlockSpec((tm, d), lambda i, j: (i, 0)), pl.BlockSpec((d, tn), lambda i, j: (0, j)),
                  pl.BlockSpec((d, tn), lambda i, j: (0, nf + j))] + [ANY] * len(deps),
        out_specs=[pl.BlockSpec((2, tm, tn), lambda i, j: (0, i, j)), pl.BlockSpec((tm, tn), lambda i, j: (i, j))],
        out_shape=[_sds((2, t, f), BF16), _sds((t, f), BF16)],
        compiler_params=_cparams("parallel", "parallel"),
    )(u, w, w, *deps)


def _dact_swiglu(dh, w_out, hid, dep, name):
    t, d = dh.shape
    f = w_out.shape[0]
    tm, tn, _ = _tiles("nt", t, f)
    deps = [] if dep is None else [dep]

    def body(dh_ref, w_ref, h_ref, *rest):
        o_ref = rest[len(deps)]
        g = lax.dot_general(dh_ref[...], w_ref[...], (((1,), (1,)), ((), ())), preferred_element_type=F32)
        a = h_ref[0].astype(F32)
        b = h_ref[1].astype(F32)
        sg = _sigmoid(a)
        o_ref[0] = (g * b * sg * (1.0 + a * (1.0 - sg))).astype(o_ref.dtype)
        o_ref[1] = (g * a * sg).astype(o_ref.dtype)

    pair = pl.BlockSpec((2, tm, tn), lambda i, j: (0, i, j))
    return pl.pallas_call(
        body, name=name, grid=(t // tm, f // tn),
        in_specs=[pl.BlockSpec((tm, d), lambda i, j: (i, 0)), pl.BlockSpec((tn, d), lambda i, j: (j, 0)), pair]
        + [ANY] * len(deps),
        out_specs=pair, out_shape=_sds((2, t, f), BF16),
        compiler_params=_cparams("parallel", "parallel"),
    )(dh, w_out, hid, *deps)


def _adamw(w, g, m, v, name):
    r, c = w.shape
    tr = _pick(r, (256, 128, 64, 32, 16, 8)) if r * c > 65536 else r

    def body(w_ref, g_ref, m_ref, v_ref, d_ref, nm_ref, nv_ref, go_ref):
        gv = g_ref[...]
        go_ref[...] = gv
        m2 = ADAM_B1 * m_ref[...] + (1.0 - ADAM_B1) * gv
        v2 = ADAM_B2 * v_ref[...] + (1.0 - ADAM_B2) * (gv * gv)
        m_hat = m2 / (1.0 - ADAM_B1 ** ADAM_STEP)
        v_hat = v2 / (1.0 - ADAM_B2 ** ADAM_STEP)
        d_ref[...] = -ADAM_LR * (m_hat / (jnp.sqrt(v_hat) + ADAM_EPS) + ADAM_WD * w_ref[...])
        nm_ref[...] = m2
        nv_ref[...] = v2

    blk = pl.BlockSpec((tr, c), lambda i: (i, 0))
    return pl.pallas_call(
        body, name=name, grid=(r // tr,),
        in_specs=[blk] * 4, out_specs=[blk] * 4, out_shape=[_sds((r, c), F32)] * 4,
        compiler_params=_cparams("parallel"),
    )(w, g, m, v)


class _Weights:
    def __init__(self, full):
        self.full = full
        self.grads = {}

    def get(self, name):
        return self.full[name]

    def mark(self, tag, value):
        return value

    def grad(self, name, g):
        self.grads[name] = g

    def dep(self):
        return None


def _local_step(x, mem, tgt, g_mix, sinks, g_xattn, g_mem, g_ffn, g_final, dims, wts):
    t, d = x.shape
    aw, cw, kvw = dims
    cb = 2 * kvw
    zoff = aw + 2 * kvw
    goff = zoff + 3 * cw
    cos, sin = _rope_tables(t)
    mark, get = wts.mark, wts.get

    def mm(a, b, **kw):
        return _matmul(a, b, dep=wts.dep(), **kw)

    u1 = mark("u1", _rms_fwd(x, g_mix, "rms_mix"))
    qkv = mm(u1, get("w_in"), mode="nn", out_dtype=F32, name="mm_qkv", cols=(0, zoff))
    proj = mark("proj", mm(u1, get("w_in"), mode="nn", out_dtype=BF16, name="mm_gates", cols=(zoff, goff - zoff + 2 * d)))
    zoff, goff = 0, goff - zoff
    q_r, k_r, v_h = _rope_qkv(qkv, cos, sin, aw, kvw)
    attn_o = mark("attn_o", _attn_fwd(q_r, k_r, v_h, sinks))
    conv_o = mark("conv_o", _conv_fwd(proj, get("conv_w"), zoff, cw, cb))
    ya = mark("ya", mm(attn_o, get("w_attn_proj"), mode="nn", out_dtype=BF16, name="mm_yattn"))
    yc, merged = _yconv_merge(conv_o, get("w_conv_proj"), ya, proj, goff, cb, wts.dep(), "mm_yconv")
    mark("merged", merged)
    norm_outs = [("row", F32), ("row", BF16)]
    bwd_outs = [("row", F32), ("row", BF16), ("vec",)]
    h1, u2 = _rows_matmul(merged, get("w_mix_out"), mode="nn", kslice=(0, 1), res=x, rows_in=[], vec=g_xattn,
                          epilogue=_ep_norm, outs=norm_outs, name="mm_mix", dep=wts.dep())
    mark("h1", h1)
    mem_n = _rms_fwd(mem, g_mem, "rms_mem")
    xq = mark("xq", mm(u2, get("w_xq"), mode="nn", out_dtype=BF16, name="mm_xq"))
    kv = mm(mem_n, get("w_xkv"), mode="nn", out_dtype=BF16, name="mm_xkv")
    xo = mark("xo", _xattn_fwd(xq, kv))
    h2, u3 = _rows_matmul(xo, get("w_xo"), mode="nn", kslice=(0, 1), res=h1, rows_in=[], vec=g_ffn,
                          epilogue=_ep_norm, outs=norm_outs, name="mm_xo", dep=wts.dep())
    mark("h2", h2)
    mark("u3", u3)
    hid, act = _ffn_in_swiglu(u3, get("w_ffn_in"), wts.dep(), "mm_ffn_in")
    mark("hid", hid)
    dh3, dh3b, dg_final, loss = _rows_matmul(
        act, get("w_ffn_out"), mode="nn", kslice=(0, 1), res=h2, rows_in=[tgt], vec=g_final, epilogue=_ep_loss,
        outs=bwd_outs + [("lane",)], name="mm_ffn_out", dep=wts.dep())
    mark("dh3", dh3b)

    wts.grad("w_ffn_out", mm(act, dh3b, mode="tn", out_dtype=BF16, name="mm_dw_ffn_out"))
    dhid = mark("dhid", _dact_swiglu(dh3b, get("w_ffn_out"), hid, wts.dep(), "mm_dact"))
    wts.grad("w_ffn_in", mm(u3, dhid, mode="tn", out_dtype=BF16, name="mm_dw_ffn_in"))
    du3 = mark("du3", _matmul_slice(dhid, get("w_ffn_in"), mode="nt", out_dtype=BF16, name="mm_du3_k0", res=None,
                                    dep=wts.dep(), kslice=(0, 2), cols=(0, d)))
    dh2, dh2b, dg_ffn = _rows_matmul(dhid, get("w_ffn_in"), mode="nt", kslice=(1, 2), res=du3, rows_in=[h2, dh3],
                                     vec=g_ffn, epilogue=_ep_rms_bwd, outs=bwd_outs, name="mm_du3_k1", dep=wts.dep())
    mark("dh2", dh2b)
    wts.grad("w_xo", mm(xo, dh2b, mode="tn", out_dtype=BF16, name="mm_dw_xo"))
    dxo = mm(dh2b, get("w_xo"), mode="nt", out_dtype=BF16, name="mm_dxo")
    dxq, dkv = _xattn_bwd(xq, kv, dxo)
    dkvb = dkv.astype(BF16)
    wts.grad("w_xq", mm(u2, dxq, mode="tn", out_dtype=BF16, name="mm_dw_xq"))
    wts.grad("w_xkv", mm(mem_n, dkvb, mode="tn", out_dtype=BF16, name="mm_dw_xkv"))
    dmem_n = mm(dkvb, get("w_xkv"), mode="nt", out_dtype=F32, name="mm_dmem")
    _, _, dg_mem = _rms_bwd(mem, g_mem, dmem_n, None, "rms_bwd_mem")
    dh1, dh1b, dg_xattn = _rows_matmul(dxq, get("w_xq"), mode="nt", kslice=(0, 1), res=None, rows_in=[h1, dh2],
                                       vec=g_xattn, epilogue=_ep_rms_bwd, outs=bwd_outs, name="mm_du2", dep=wts.dep())
    mark("dh1", dh1b)
    wts.grad("w_mix_out", mm(merged, dh1b, mode="tn", out_dtype=BF16, name="mm_dw_mix"))
    dya, dyc, dga, dgg = _dmerged_split(dh1b, get("w_mix_out"), proj, ya, yc, goff, cb, wts.dep(), "mm_dmerged")
    mark("dya", dya)
    wts.grad("w_attn_proj", mm(attn_o, dya, mode="tn", out_dtype=BF16, name="mm_dw_attn_proj"))
    dattn_o = mm(dya, get("w_attn_proj"), mode="nt", out_dtype=BF16, name="mm_dattn")
    wts.grad("w_conv_proj", mm(conv_o, dyc, mode="tn", out_dtype=BF16, name="mm_dw_conv_proj"))
    dconv_o = mark("dconv_o", mm(dyc, get("w_conv_proj"), mode="nt", out_dtype=BF16, name="mm_dconv"))
    dz, dgb, dgc, dconv_w = _conv_bwd(proj, get("conv_w"), dconv_o, zoff, cw, cb)
    mark("dz", dz)
    dq, dkc, dkp, dvc, dvp, dsinks = _attn_bwd(q_r, k_r, v_h, sinks, attn_o, dattn_o)
    mark("dq", dq)
    dproj = mark("dproj", _assemble_dproj(dq, dkc, dkp, dvc, dvp, cos, sin, dz, dgb, dgc, dga, dgg))
    wts.grad("w_in", mm(u1, dproj, mode="tn", out_dtype=BF16, name="mm_dw_in"))
    du1 = mark("du1", _matmul_slice(dproj, get("w_in"), mode="nt", out_dtype=BF16, name="mm_du1_k0", res=None,
                                    dep=wts.dep(), kslice=(0, 2), cols=(0, d)))
    grad_x, _, dg_mix = _rows_matmul(dproj, get("w_in"), mode="nt", kslice=(1, 2), res=du1, rows_in=[x, dh1],
                                     vec=g_mix, epilogue=_ep_rms_bwd, outs=bwd_outs, name="mm_du1_k1", dep=wts.dep())
    mark("grad_x", grad_x)

    small = dict(g_mix=dg_mix, g_xattn=dg_xattn, g_mem=dg_mem, g_ffn=dg_ffn, g_final=dg_final,
                 conv_w=dconv_w[:3], attn_sinks=dsinks[0:1, :sinks.shape[1]], loss=loss[0:1, 0:1])
    return grad_x, small


BIG = (("w_in", 1), ("w_attn_proj", 1), ("w_conv_proj", 1), ("w_mix_out", 0), ("w_xq", 0), ("w_xkv", 0),
       ("w_xo", 1), ("w_ffn_in", 1), ("w_ffn_out", 0))


def _place():
    x, y, c = lax.axis_index("x"), lax.axis_index("y"), lax.axis_index("c")
    chips = [(1 - x, y), (x, 1 - y), (1 - x, 1 - y)]
    return x, y, c, chips


def _window(ref, ax, shard_shape, s, h):
    sr, sc = shard_shape
    hr = sr // 2
    if ax == 1:
        return ref.at[pl.ds(pl.multiple_of(h * hr, 16), hr), pl.ds(pl.multiple_of(s * sc, LANES), sc)]
    return ref.at[pl.ds(pl.multiple_of(s * sr + h * hr, 16), hr), :]


def _half(ref, h):
    hr = ref.shape[0] // 2
    return ref.at[pl.ds(pl.multiple_of(h * hr, 16), hr), :]


def _remote(src, dst, send_sem, recv_sem, dev):
    return pltpu.make_async_remote_copy(src_ref=src, dst_ref=dst, send_sem=send_sem, recv_sem=recv_sem,
                                        device_id=dev, device_id_type=MESH)


def _cast_to_full(shard, ax, me, name, dtype=BF16):
    sr, sc = shard.shape
    tr = _pick(sr, (256, 352, 128, 64, 32, 16))
    nr = sr // tr
    full = (sr * N_CHIPS, sc) if ax == 0 else (sr, sc * N_CHIPS)

    def body(me_ref, s_ref, o_ref):
        o_ref[...] = s_ref[...].astype(o_ref.dtype)

    if ax == 1:
        out_spec = pl.BlockSpec((tr, sc), lambda r, me_ref: (r, me_ref[0]))
    else:
        out_spec = pl.BlockSpec((tr, sc), lambda r, me_ref: (me_ref[0] * nr + r, 0))
    return pl.pallas_call(
        body, name=name,
        grid_spec=pltpu.PrefetchScalarGridSpec(
            num_scalar_prefetch=1, grid=(nr,), in_specs=[pl.BlockSpec((tr, sc), lambda r, me_ref: (r, 0))],
            out_specs=out_spec),
        out_shape=_sds(full, dtype),
        compiler_params=_cparams("parallel"),
    )(me, shard)


HBM = pl.BlockSpec(memory_space=pltpu.HBM)
SEM = pl.BlockSpec(memory_space=pltpu.SEMAPHORE)
EFFECT = pltpu.SideEffectType.DATAFLOW_SIDE_EFFECTING


def _in_hbm(a):
    return pltpu.with_memory_space_constraint(a, pltpu.HBM)


def _gather_window(ref, ax, shard_shape, s, h):
    if h is not None:
        return _window(ref, ax, shard_shape, s, h)
    sr, sc = shard_shape
    if ax == 1:
        return ref.at[:, pl.ds(pl.multiple_of(s * sc, LANES), sc)]
    return ref.at[pl.ds(pl.multiple_of(s * sr, 8), sr), :]


def _ag_start(fulls, axes, shard_shapes, whole):
    n = len(fulls)

    def body(*refs):
        src = refs[:n]
        send_sems, recv_sems = refs[n], refs[n + 1]
        token = refs[2 * n + 2]
        x, y, c, chips = _place()
        me = 2 * x + y
        for i in range(n):
            h = None if whole[i] else c
            for j, chip in enumerate(chips):
                blk = _gather_window(src[i], axes[i], shard_shapes[i], me, h)
                _remote(blk, blk, send_sems.at[3 * i + j], recv_sems.at[3 * i + j], (*chip, c)).start()
        token[...] = jnp.zeros_like(token)

    res = pl.pallas_call(
        body, name="ag_start_" + str(n),
        out_shape=(pltpu.SemaphoreType.DMA((3 * n,)), pltpu.SemaphoreType.DMA((3 * n,)),
                   *[pltpu.HBM(f.shape, f.dtype) for f in fulls], _sds((8, LANES), F32)),
        in_specs=[HBM] * n, out_specs=(SEM, SEM, *[HBM] * n, pl.BlockSpec(memory_space=pltpu.VMEM)),
        input_output_aliases={i: 2 + i for i in range(n)},
        compiler_params=pltpu.CompilerParams(has_side_effects=EFFECT),
    )(*[_in_hbm(f) for f in fulls])
    return res[0], res[1], list(res[2:2 + n]), res[2 + n]


def _ag_mid(bufs, slots, axes, shard_shapes, whole, send_sems, recv_sems, after, name):
    ng = len(bufs)

    def body(*refs):
        src = refs[:ng]
        s_in, r_in = refs[ng], refs[ng + 1]
        fsend, frecv = refs[ng + 3], refs[ng + 4]
        x, y, c, chips = _place()
        me = 2 * x + y
        sib = (x, y, 1 - c)
        for k, i in enumerate(slots):
            h = None if whole[k] else c
            for j, chip in enumerate(chips):
                cj = 2 * chip[0] + chip[1]
                mine = _gather_window(src[k], axes[k], shard_shapes[k], me, h)
                theirs = _gather_window(src[k], axes[k], shard_shapes[k], cj, h)
                _remote(theirs, theirs, s_in.at[3 * i + j], r_in.at[3 * i + j], (*chip, c)).wait_recv()
                _remote(mine, mine, s_in.at[3 * i + j], r_in.at[3 * i + j], (*chip, c)).wait_send()
                if not whole[k]:
                    _remote(theirs, theirs, fsend.at[3 * k + j], frecv.at[3 * k + j], sib).start()
        token = refs[2 * ng + 5]
        token[...] = jnp.zeros_like(token)

    res = pl.pallas_call(
        body, name=name,
        out_shape=(pltpu.SemaphoreType.DMA((3 * ng,)), pltpu.SemaphoreType.DMA((3 * ng,)),
                   *[pltpu.HBM(b.shape, b.dtype) for b in bufs], _sds((8, LANES), F32)),
        in_specs=[HBM] * ng + [SEM, SEM, ANY],
        out_specs=(SEM, SEM, *[HBM] * ng, pl.BlockSpec(memory_space=pltpu.VMEM)),
        input_output_aliases={k: 2 + k for k in range(ng)},
        compiler_params=pltpu.CompilerParams(has_side_effects=EFFECT),
    )(*bufs, send_sems, recv_sems, after)
    return res[0], res[1], list(res[2:2 + ng]), res[2 + ng]


def _ag_wait(bufs, axes, shard_shapes, whole, fsend, frecv, after, name):
    ng = len(bufs)

    def body(*refs):
        src = refs[:ng]
        s_in, r_in = refs[ng], refs[ng + 1]
        x, y, c, chips = _place()
        sib = (x, y, 1 - c)
        for k in range(ng):
            if whole[k]:
                continue
            for j, chip in enumerate(chips):
                cj = 2 * chip[0] + chip[1]
                sent = _gather_window(src[k], axes[k], shard_shapes[k], cj, c)
                landed = _gather_window(src[k], axes[k], shard_shapes[k], cj, 1 - c)
                _remote(landed, landed, s_in.at[3 * k + j], r_in.at[3 * k + j], sib).wait_recv()
                _remote(sent, sent, s_in.at[3 * k + j], r_in.at[3 * k + j], sib).wait_send()

    res = pl.pallas_call(
        body, name=name,
        out_shape=tuple(pltpu.HBM(b.shape, b.dtype) for b in bufs),
        in_specs=[HBM] * ng + [SEM, SEM, ANY], out_specs=tuple([HBM] * ng),
        input_output_aliases={k: k for k in range(ng)},
        compiler_params=pltpu.CompilerParams(has_side_effects=EFFECT),
    )(*bufs, fsend, frecv, after)
    return list(res)


class _Schedule:
    GROUPS = ((("w_in", "conv_w"), "u1", "u1"),
              (("w_attn_proj", "w_conv_proj", "w_mix_out", "w_xq", "w_xkv", "w_xo"), "proj", "conv_o"),
              (("w_ffn_in",), "h1", "h2"),
              (("w_ffn_out",), "u3", "hid"))
    STARTS = ((0,), (1, 2, 3))
    REDUCE = ((("w_ffn_out",), "dhid", "grad:w_ffn_in", "du3"),
              (("w_ffn_in",), "grad:w_ffn_in", "dya", "grad_x"),
              (("w_xo", "w_xq", "w_xkv", "w_mix_out", "w_attn_proj", "w_conv_proj"), "dconv_o", "dq", "grad_x"),
              (("w_in",), "grad:w_in", "end", "end2"))

    def __init__(self, seed, axes, shard_shapes, place, on_ready):
        self.ax, self.shape, self.place, self.on_ready = axes, shard_shapes, place, on_ready
        self.stage, self.buf, self.slot, self.passes = {}, {}, {}, {}
        self.ready = set()
        self.grads = {}
        token = None
        for groups in self.STARTS:
            order = [nm for g in groups for nm in self.GROUPS[g][0]]
            send, recv, bufs, token = _ag_start([seed(nm, token) for nm in order], *self._meta(order))
            self.buf.update(zip(order, bufs))
            self.slot.update({nm: (send, recv, k) for k, nm in enumerate(order)})
        self.token = self.latest = token
        self.mark("start", token)

    def _meta(self, names):
        return ([self.ax[nm] for nm in names], [self.shape[nm] for nm in names], [nm == "conv_w" for nm in names])

    def mark(self, tag, value):
        for g, (names, mid, wait) in enumerate(self.GROUPS):
            if tag == mid:
                send, recv, _ = self.slot[names[0]]
                fs, fr, bufs, self.latest = _ag_mid([self.buf[nm] for nm in names], [self.slot[nm][2] for nm in names],
                                                    *self._meta(names), send, recv, value, "ag_mid_%d" % g)
                self.buf.update(zip(names, bufs))
                self.passes[g] = (fs, fr)
            if tag == wait:
                fs, fr = self.passes[g]
                bufs = _ag_wait([self.buf[nm] for nm in names], *self._meta(names), fs, fr, value, "ag_wait_%d" % g)
                self.buf.update(zip(names, bufs))
                self.ready.update(names)
        for g, (names, send, total, finish) in enumerate(self.REDUCE):
            st = self.stage.get(g)
            if st is None:
                continue
            ng = len(names)
            if tag == send and st["at"] == "pair":
                arrs = _exchange_wait("rs_pair_wait_%d" % g, st["arrs"], *st["sems"], st["plan"], value)
                parts = [_pair_add(arrs[k], arrs[ng + k], self.ax[nm], self.shape[nm], self.place, "pair_add_" + nm)
                         for k, nm in enumerate(names)]
                plan, nsem = _plan_chip(ng)
                ss, rs, arrs, self.latest = _exchange_start(
                    "rs_chip_start_%d" % g, parts + [lax.empty(p.shape, p.dtype) for p in parts], nsem, plan)
                self.stage[g] = dict(at="chip", arrs=arrs, sems=(ss, rs), plan=plan)
            elif tag == total and st["at"] == "chip":
                arrs = _exchange_wait("rs_chip_wait_%d" % g, st["arrs"], *st["sems"], st["plan"], value)
                halves = [_chip_add(arrs[k], arrs[ng + k], self.place, "chip_add_" + nm) for k, nm in enumerate(names)]
                plan, nsem = _plan_gather(ng)
                ss, rs, arrs, self.latest = _exchange_start("rs_gather_start_%d" % g, halves, nsem, plan)
                self.stage[g] = dict(at="gather", arrs=arrs, sems=(ss, rs), plan=plan)
            st = self.stage[g]
            if tag == finish and st["at"] == "gather":
                arrs = _exchange_wait("rs_gather_wait_%d" % g, st["arrs"], *st["sems"], st["plan"], value)
                self.stage[g] = dict(at="done")
                for nm, shard in zip(names, arrs):
                    self.on_ready(nm, shard)
        return value

    def get(self, name):
        assert name in self.ready, name
        return self.buf[name]

    def grad(self, name, g):
        self.grads[name] = g
        for gi, (names, _, _, _) in enumerate(self.REDUCE):
            if name == names[-1]:
                gs = [self.grads[nm] for nm in names]
                plan, nsem = _plan_pair(len(names), [self.ax[nm] for nm in names], [self.shape[nm] for nm in names])
                ss, rs, arrs, self.latest = _exchange_start(
                    "rs_pair_start_%d" % gi, gs + [lax.empty(a.shape, a.dtype) for a in gs], nsem, plan)
                self.stage[gi] = dict(at="pair", arrs=arrs, sems=(ss, rs), plan=plan)
                g = self.latest
        self.mark("grad:" + name, g)

    def dep(self):
        return self.latest


def _exchange_start(name, arrays, nsem, plan):
    n = len(arrays)

    def body(*refs):
        send_sems, recv_sems, token = refs[n], refs[n + 1], refs[2 * n + 2]
        sends, _ = plan(refs[:n])
        for k, (src, dst, dev) in enumerate(sends):
            _remote(src, dst, send_sems.at[k], recv_sems.at[k], dev).start()
        token[...] = jnp.zeros_like(token)

    res = pl.pallas_call(
        body, name=name,
        out_shape=(pltpu.SemaphoreType.DMA((nsem,)), pltpu.SemaphoreType.DMA((nsem,)),
                   *[pltpu.HBM(a.shape, a.dtype) for a in arrays], _sds((8, LANES), F32)),
        in_specs=[HBM] * n, out_specs=(SEM, SEM, *[HBM] * n, pl.BlockSpec(memory_space=pltpu.VMEM)),
        input_output_aliases={i: 2 + i for i in range(n)},
        compiler_params=pltpu.CompilerParams(has_side_effects=EFFECT),
    )(*[_in_hbm(a) for a in arrays])
    return res[0], res[1], list(res[2:2 + n]), res[2 + n]


def _exchange_wait(name, arrays, send_sems, recv_sems, plan, after):
    n = len(arrays)

    def body(*refs):
        s_in, r_in = refs[n], refs[n + 1]
        sends, recvs = plan(refs[:n])
        for k, land in enumerate(recvs):
            _remote(land, land, s_in.at[k], r_in.at[k], sends[k][2]).wait_recv()
        for k, (src, _, dev) in enumerate(sends):
            _remote(src, src, s_in.at[k], r_in.at[k], dev).wait_send()

    res = pl.pallas_call(
        body, name=name,
        out_shape=tuple(pltpu.HBM(a.shape, a.dtype) for a in arrays),
        in_specs=[HBM] * n + [SEM, SEM, ANY], out_specs=tuple([HBM] * n),
        input_output_aliases={i: i for i in range(n)},
        compiler_params=pltpu.CompilerParams(has_side_effects=EFFECT),
    )(*arrays, send_sems, recv_sems, after)
    return list(res)


def _plan_pair(n, axes, shard_shapes):
    def plan(refs):
        g, ra = refs[:n], refs[n:]
        x, y, c, _ = _place()
        sib = (x, y, 1 - c)

        def pieces(ref, i, h):
            if axes[i] == 1:
                return [_half(ref, h)]
            return [_window(ref, 0, shard_shapes[i], s, h) for s in range(N_CHIPS)]

        sends, recvs = [], []
        for i in range(n):
            sends += [(src, dst, sib) for src, dst in zip(pieces(g[i], i, 1 - c), pieces(ra[i], i, 1 - c))]
            recvs += pieces(ra[i], i, c)
        return sends, recvs

    return plan, sum(1 if ax == 1 else N_CHIPS for ax in axes)


def _plan_chip(n):
    def plan(refs):
        p, rc = refs[:n], refs[n:]
        x, y, c, chips = _place()
        me = 2 * x + y
        sends, recvs = [], []
        for i in range(n):
            for chip in chips:
                cj = 2 * chip[0] + chip[1]
                sends.append((p[i].at[cj], rc[i].at[me], (*chip, c)))
                recvs.append(rc[i].at[cj])
        return sends, recvs

    return plan, 3 * n


def _plan_gather(n):
    def plan(refs):
        x, y, c, _ = _place()
        sib = (x, y, 1 - c)
        return ([(_half(r, c), _half(r, c), sib) for r in refs], [_half(r, 1 - c) for r in refs])

    return plan, n


def _pair_add(g, ra, ax, shard_shape, place, name):
    sr, sc = shard_shape
    hr = sr // 2
    wc = sc
    tr = _pick(hr, (256, 352, 128, 64, 32, 16))
    nr = hr // tr

    def body(p_ref, a_ref, b_ref, o_ref):
        o_ref[...] = (a_ref[...].astype(F32) + b_ref[...].astype(F32)).astype(o_ref.dtype)

    if ax == 1:
        src = pl.BlockSpec((tr, wc), lambda s, r, p_ref: (p_ref[0] * nr + r, s))
    else:
        src = pl.BlockSpec((tr, wc), lambda s, r, p_ref: (s * 2 * nr + p_ref[0] * nr + r, 0))
    return pl.pallas_call(
        body, name=name,
        grid_spec=pltpu.PrefetchScalarGridSpec(
            num_scalar_prefetch=1, grid=(N_CHIPS, nr), in_specs=[src, src],
            out_specs=pl.BlockSpec((None, tr, wc), lambda s, r, p_ref: (s, r, 0))),
        out_shape=_sds((N_CHIPS, hr, wc), BF16),
        compiler_params=_cparams("parallel", "parallel"),
    )(place, g, ra)


def _chip_add(part, rc, place, name):
    _, hr, wc = rc.shape
    tr = _pick(hr, (256, 352, 128, 64, 32, 16))
    nr = hr // tr

    def body(p_ref, own_ref, r1_ref, r2_ref, r3_ref, o_ref):
        acc = own_ref[...].astype(F32)
        for r_ref in (r1_ref, r2_ref, r3_ref):
            acc = acc + r_ref[...].astype(F32)
        o_ref[...] = acc

    def slot(k):
        return pl.BlockSpec((None, tr, wc), lambda r, p_ref: ((p_ref[1] + k) % N_CHIPS, r, 0))

    return pl.pallas_call(
        body, name=name,
        grid_spec=pltpu.PrefetchScalarGridSpec(
            num_scalar_prefetch=1, grid=(nr,), in_specs=[slot(0), slot(1), slot(2), slot(3)],
            out_specs=pl.BlockSpec((tr, wc), lambda r, p_ref: (p_ref[0] * nr + r, 0))),
        out_shape=_sds((2 * hr, wc), F32),
        compiler_params=_cparams("parallel"),
    )(place, part, rc, rc, rc)


N_DEV = 8


def _all_reduce_small(buf):
    r, cdim = buf.shape

    def body(x_ref, o_ref, land, send_sems, recv_sems):
        x, y, c, _ = _place()
        me = 4 * x + 2 * y + c
        land[me] = x_ref[...]
        sends = []
        for k in range(1, N_DEV):
            kx, ky, kc = (k >> 2) & 1, (k >> 1) & 1, k & 1
            peer = (1 - x if kx else x, 1 - y if ky else y, 1 - c if kc else c)
            cp = _remote(x_ref, land.at[me], send_sems.at[k - 1], recv_sems.at[k - 1], peer)
            cp.start()
            sends.append(cp)
        for k in range(1, N_DEV):
            kx, ky, kc = (k >> 2) & 1, (k >> 1) & 1, k & 1
            peer = (1 - x if kx else x, 1 - y if ky else y, 1 - c if kc else c)
            pidx = 4 * peer[0] + 2 * peer[1] + peer[2]
            _remote(land.at[pidx], land.at[pidx], send_sems.at[k - 1], recv_sems.at[k - 1], peer).wait_recv()
        for cp in sends:
            cp.wait_send()
        acc = land[0]
        for dev in range(1, N_DEV):
            acc = acc + land[dev]
        o_ref[...] = acc

    vm = pl.BlockSpec(memory_space=pltpu.VMEM)
    return pl.pallas_call(
        body, name="all_reduce_small", in_specs=[vm], out_specs=vm, out_shape=_sds((r, cdim), F32),
        scratch_shapes=[pltpu.VMEM((N_DEV, r, cdim), F32), pltpu.SemaphoreType.DMA((N_DEV - 1,)),
                        pltpu.SemaphoreType.DMA((N_DEV - 1,))],
    )(buf)


SMALL_ROWS = 16


def kernel(x, mem, g_mix, w_in, conv_w, attn_sinks, w_attn_proj, w_conv_proj, w_mix_out, g_xattn, g_mem, w_xq, w_xkv, w_xo, g_ffn, w_ffn_in, w_ffn_out, g_final, loss_target, m_g_mix, m_w_in, m_conv_w, m_attn_sinks, m_w_attn_proj, m_w_conv_proj, m_w_mix_out, m_g_xattn, m_g_mem, m_w_xq, m_w_xkv, m_w_xo, m_g_ffn, m_w_ffn_in, m_w_ffn_out, m_g_final, v_g_mix, v_w_in, v_conv_w, v_attn_sinks, v_w_attn_proj, v_w_conv_proj, v_w_mix_out, v_g_xattn, v_g_mem, v_w_xq, v_w_xkv, v_w_xo, v_g_ffn, v_w_ffn_in, v_w_ffn_out, v_g_final):
    w = dict(g_mix=g_mix, w_in=w_in[0], conv_w=conv_w[0], attn_sinks=attn_sinks, w_attn_proj=w_attn_proj[0],
             w_conv_proj=w_conv_proj[0], w_mix_out=w_mix_out[0], g_xattn=g_xattn, g_mem=g_mem, w_xq=w_xq[0],
             w_xkv=w_xkv[0], w_xo=w_xo[0], g_ffn=g_ffn, w_ffn_in=w_ffn_in[0], w_ffn_out=w_ffn_out[0],
             g_final=g_final[None])
    m = dict(g_mix=m_g_mix, w_in=m_w_in[0], conv_w=m_conv_w[0], attn_sinks=m_attn_sinks,
             w_attn_proj=m_w_attn_proj[0], w_conv_proj=m_w_conv_proj[0], w_mix_out=m_w_mix_out[0],
             g_xattn=m_g_xattn, g_mem=m_g_mem, w_xq=m_w_xq[0], w_xkv=m_w_xkv[0], w_xo=m_w_xo[0], g_ffn=m_g_ffn,
             w_ffn_in=m_w_ffn_in[0], w_ffn_out=m_w_ffn_out[0], g_final=m_g_final[None])
    v = dict(g_mix=v_g_mix, w_in=v_w_in[0], conv_w=v_conv_w[0], attn_sinks=v_attn_sinks,
             w_attn_proj=v_w_attn_proj[0], w_conv_proj=v_w_conv_proj[0], w_mix_out=v_w_mix_out[0],
             g_xattn=v_g_xattn, g_mem=v_g_mem, w_xq=v_w_xq[0], w_xkv=v_w_xkv[0], w_xo=v_w_xo[0], g_ffn=v_g_ffn,
             w_ffn_in=v_w_ffn_in[0], w_ffn_out=v_w_ffn_out[0], g_final=v_g_final[None])
    names = [nm for nm, _ in BIG]
    axes = [ax for _, ax in BIG]
    d = x.shape[2]
    cw = w["conv_w"].shape[1] * N_CHIPS
    chip = (2 * lax.axis_index("x") + lax.axis_index("y")).astype(jnp.int32)
    place = jnp.stack([lax.axis_index("c").astype(jnp.int32), chip])
    shard_shapes = [w[nm].shape for nm in names]

    def seed(nm, token):
        me1 = chip.reshape(1)
        if token is not None:
            me1 = me1 + token[0, 0].astype(jnp.int32)
        if nm == "conv_w":
            return _cast_to_full(w[nm], 1, me1, "place_conv_w", F32)
        return _cast_to_full(w[nm], dict(BIG)[nm], me1, "cast_" + nm)

    upd = {}

    def on_ready(nm, shard):
        upd[nm] = _adamw(w[nm], shard, m[nm], v[nm], "adamw_" + nm)
        grads[nm] = upd[nm][3]

    grads = {}
    wts = _Schedule(seed, dict(zip(names + ["conv_w"], axes + [1])),
                    dict(zip(names + ["conv_w"], shard_shapes + [w["conv_w"].shape])), place, on_ready)
    aw, cw = w["w_attn_proj"].shape[0], w["w_conv_proj"].shape[0]
    kvw = (w["w_in"].shape[1] * N_CHIPS - aw - 3 * cw - 2 * d) // 2
    grad_x, small = _local_step(
        x[0], mem[0], loss_target[0], w["g_mix"] + wts.token[0:1, 0:1], w["attn_sinks"], w["g_xattn"], w["g_mem"],
        w["g_ffn"], w["g_final"], (aw, cw, kvw), wts)

    pw = max(d, cw)

    def row(a):
        return jnp.pad(a, ((0, 0), (0, pw - a.shape[1])))

    gains = ("g_mix", "g_xattn", "g_mem", "g_ffn", "g_final")
    packed = jnp.concatenate(
        [row(small[nm]) for nm in gains] + [row(small["conv_w"]),
         row(jnp.concatenate([small["attn_sinks"], small["loss"]], axis=1)),
         jnp.zeros((SMALL_ROWS - 9, pw), F32)], axis=0)
    total = _all_reduce_small(packed)
    wts.mark("end", total)
    nsink = attn_sinks.shape[1]
    grads.update({nm: total[k:k + 1, :d] for k, nm in enumerate(gains)})
    grads.update(conv_w=lax.dynamic_slice(total, (5, chip * (cw // N_CHIPS)), (3, cw // N_CHIPS)),
                 attn_sinks=total[8:9, :nsink])
    loss = total[8, nsink]
    for nm in gains + ("conv_w", "attn_sinks"):
        upd[nm] = _adamw(w[nm], grads[nm], m[nm], v[nm], "adamw_" + nm)
    wts.mark("end2", upd["g_final"][0])

    order = ["g_mix", "w_in", "conv_w", "attn_sinks", "w_attn_proj", "w_conv_proj", "w_mix_out", "g_xattn", "g_mem",
             "w_xq", "w_xkv", "w_xo", "g_ffn", "w_ffn_in", "w_ffn_out", "g_final"]

    stacked = set(names) | {"conv_w"}

    def shaped(nm, a):
        if nm == "g_final":
            return a[0]
        return a[None] if nm in stacked else a

    outs = [loss, grad_x[None]]
    outs += [shaped(nm, grads[nm]) for nm in order]
    for k in range(3):
        outs += [shaped(nm, upd[nm][k]) for nm in order]
    return tuple(outs)
```

```python
import functools

import jax
import jax.numpy as jnp
from jax import lax
from jax.experimental import pallas as pl
from jax.experimental.pallas import tpu as pltpu

F32 = jnp.float32
BF16 = jnp.bfloat16

VMEM_LIMIT_BYTES = 56 * 1024 * 1024
LANES = 128
HEAD_DIM = 64
BLOCK = 128
X_HEAD_DIM = 128
ROPE_THETA = 10000.0
EPS = 1e-6
NEG = -1e30
ADAM_LR, ADAM_B1, ADAM_B2, ADAM_EPS, ADAM_WD, ADAM_STEP = 0.001, 0.9, 0.999, 1e-08, 0.01, 10
N_CHIPS = 4
MESH = pl.DeviceIdType.MESH
ANY = pl.BlockSpec(memory_space=pl.ANY)


def _pick(dim, prefs):
    for p in prefs:
        if dim % p == 0:
            return p
    return dim


def _cparams(*sem):
    return pltpu.CompilerParams(dimension_semantics=sem, vmem_limit_bytes=VMEM_LIMIT_BYTES)


def _sds(shape, dtype):
    return jax.ShapeDtypeStruct(shape, dtype)


def _pin(dep):
    return ([], []) if dep is None else ([dep], [ANY])


def _drop_pin(body, n_in, dep):
    if dep is None:
        return body
    return lambda *refs: body(*refs[:n_in], *refs[n_in + 1:])


def _sigmoid(v):
    return 0.5 * jnp.tanh(0.5 * v) + 0.5


MATMUL_VMEM_BUDGET = 46 * 1024 * 1024


def _tiles(mode, m, n):
    if mode == "tn" and m % 1024 != 0:
        return _pick(m, (512, 256, 128)), _pick(n, (1024, 512, 256, 128)), True
    return _pick(m, (1024, 512, 256, 128)), _pick(n, (512, 256, 128)), False


def _k_parts(m, n, k):
    tm, tn = _pick(m, (1024, 512, 256, 128)), _pick(n, (512, 256, 128))
    for parts in range(1, k // LANES + 1):
        if k % (parts * LANES) == 0 and 4 * (tm + tn) * (k // parts) + 16 * tm * tn <= MATMUL_VMEM_BUDGET:
            return parts
    return k // LANES


def _matmul(a, b, *, mode, out_dtype, name, res=None, dep=None, cols=None):
    if mode == "nn":
        (m, k), (k2, n) = a.shape, b.shape
    elif mode == "nt":
        (m, k), (n, k2) = (a.shape[-2], a.shape[-1] * (a.shape[0] if a.ndim == 3 else 1)), b.shape
    else:
        (k, m), (k2, n) = a.shape, (b.shape[-2], b.shape[-1] * (b.shape[0] if b.ndim == 3 else 1))
    assert k == k2, (a.shape, b.shape, mode)
    assert cols is None or mode != "nt"
    cols = cols or (0, n)
    parts = 1 if mode == "tn" else (a.shape[0] if a.ndim == 3 else _k_parts(m, cols[1], k))
    for p in range(parts):
        last = p == parts - 1
        res = _matmul_slice(a, b, mode=mode, out_dtype=out_dtype if last else F32, res=res, dep=dep, cols=cols,
                            kslice=(p, parts), name=name + ("_k%d" % p if parts > 1 else ""))
    return res


def _matmul_slice(a, b, *, mode, out_dtype, name, res, dep, kslice, cols):
    part, parts = kslice
    (m, k) = a.shape[-2:] if mode != "tn" else a.shape[::-1]
    col0, n = cols
    tk = k if a.ndim == 3 else k // parts
    tm, tn, swap = _tiles(mode, m, b.shape[2] if (mode == "tn" and b.ndim == 3) else n)
    while col0 % tn:
        tn //= 2
    joff = col0 // tn
    dims = {"nn": (((1,), (0,)), ((), ())), "nt": (((1,), (1,)), ((), ())), "tn": (((0,), (0,)), ((), ()))}[mode]
    has_res = res is not None
    has_dep = dep is not None

    def body(*refs):
        a_ref, b_ref = refs[0], refs[1]
        o_ref = refs[2 + has_res + has_dep]
        val = lax.dot_general(a_ref[...], b_ref[...], dims, preferred_element_type=F32)
        if has_res:
            val = val + refs[2][...]
        o_ref[...] = val.astype(o_ref.dtype)

    def spec(shape, f):
        if swap:
            return pl.BlockSpec(shape, lambda j, i: f(i, j))
        return pl.BlockSpec(shape, f)

    if mode == "tn":
        a_spec = spec((tk, tm), lambda i, j: (part, i))
    elif a.ndim == 3:
        a_spec = spec((None, tm, tk), lambda i, j: (part, i, 0))
    else:
        a_spec = spec((tm, tk), lambda i, j: (i, part))
    if mode == "nt":
        b_spec = spec((tn, tk), lambda i, j: (j, part))
    elif b.ndim == 3:
        per = b.shape[2] // tn
        b_spec = spec((None, tk, tn), lambda i, j: (j // per, part, j % per))
    else:
        b_spec = spec((tk, tn), lambda i, j: (part, joff + j))
    o_spec = spec((tm, tn), lambda i, j: (i, j))
    return pl.pallas_call(
        body,
        name=name,
        grid=(n // tn, m // tm) if swap else (m // tm, n // tn),
        in_specs=[a_spec, b_spec] + ([o_spec] if has_res else []) + ([ANY] if has_dep else []),
        out_specs=o_spec,
        out_shape=_sds((m, n), out_dtype),
        compiler_params=_cparams("parallel", "parallel"),
    )(*([a, b] + ([res] if has_res else []) + ([dep] if has_dep else [])))


def _rms_fwd(x, g, name):
    t, d = x.shape
    tm = _pick(t, (512, 256, 128))

    def body(x_ref, g_ref, o_ref):
        xf = x_ref[...]
        r = lax.rsqrt(jnp.mean(xf * xf, axis=-1, keepdims=True) + EPS)
        o_ref[...] = (xf * r * g_ref[...]).astype(o_ref.dtype)

    row = pl.BlockSpec((tm, d), lambda i: (i, 0))
    return pl.pallas_call(
        body, name=name, grid=(t // tm,),
        in_specs=[row, pl.BlockSpec((1, d), lambda i: (0, 0))],
        out_specs=row, out_shape=_sds((t, d), BF16),
        compiler_params=_cparams("parallel"),
    )(x, g)


def _rms_bwd_math(xf, g, du):
    r = lax.rsqrt(jnp.mean(xf * xf, axis=-1, keepdims=True) + EPS)
    xh = xf * r
    gdy = g * du
    dx = r * (gdy - xh * jnp.mean(gdy * xh, axis=-1, keepdims=True))
    dg = jnp.sum(du * xh, axis=0, keepdims=True)
    return dx, dg


def _rms_bwd(x, g, du, dh, name):
    t, d = x.shape
    tm = _pick(t, (256, 128))
    has_dh = dh is not None

    def body(*refs):
        x_ref, g_ref, du_ref = refs[0], refs[1], refs[2]
        o_ref, ob_ref, dg_ref = refs[3 + has_dh:]
        dx, dg = _rms_bwd_math(x_ref[...], g_ref[...], du_ref[...].astype(F32))
        if has_dh:
            dx = dx + refs[3][...]
        o_ref[...] = dx
        ob_ref[...] = dx.astype(BF16)

        @pl.when(pl.program_id(0) == 0)
        def _():
            dg_ref[...] = dg

        @pl.when(pl.program_id(0) > 0)
        def _():
            dg_ref[...] += dg

    row = pl.BlockSpec((tm, d), lambda i: (i, 0))
    vec = pl.BlockSpec((1, d), lambda i: (0, 0))
    return pl.pallas_call(
        body, name=name, grid=(t // tm,),
        in_specs=[row, vec, row] + ([row] if has_dh else []),
        out_specs=[row, row, vec],
        out_shape=[_sds((t, d), F32), _sds((t, d), BF16), _sds((1, d), F32)],
        compiler_params=_cparams("arbitrary"),
    )(*([x, g, du] + ([dh] if has_dh else [])))


ROW_TILE = 256


def _rows_matmul(a, b, *, mode, kslice, res, rows_in, vec, epilogue, outs, name, dep):
    part, parts = kslice
    t = a.shape[-2]
    n = b.shape[1] if mode == "nn" else b.shape[0]
    kc = a.shape[-1] if a.ndim == 3 else a.shape[1] // parts
    tm = _pick(t, (ROW_TILE, 128))
    dims = (((1,), (0,)), ((), ())) if mode == "nn" else (((1,), (1,)), ((), ()))
    has_res = res is not None
    deps = [] if dep is None else [dep]
    n_in = 2 + has_res + len(rows_in) + 1 + len(deps)

    def body(*refs):
        val = lax.dot_general(refs[0][...], refs[1][...], dims, preferred_element_type=F32)
        if has_res:
            val = val + refs[2][...]
        row_refs = refs[2 + has_res:2 + has_res + len(rows_in)]
        epilogue(val, row_refs, refs[2 + has_res + len(rows_in)], refs[n_in:], pl.program_id(0) == 0)

    if a.ndim == 3:
        a_spec = pl.BlockSpec((None, tm, kc), lambda i: (part, i, 0))
    else:
        a_spec = pl.BlockSpec((tm, kc), lambda i: (i, part))
    if mode == "nn":
        b_spec = pl.BlockSpec((kc, n), lambda i: (part, 0), pipeline_mode=pl.Buffered(1))
    else:
        b_spec = pl.BlockSpec((n, kc), lambda i: (0, part), pipeline_mode=pl.Buffered(1))
    row = pl.BlockSpec((tm, n), lambda i: (i, 0))
    kinds = {"row": row, "vec": pl.BlockSpec((1, n), lambda i: (0, 0)), "lane": pl.BlockSpec((1, LANES), lambda i: (0, 0))}
    shapes = {"row": (t, n), "vec": (1, n), "lane": (1, LANES)}
    return pl.pallas_call(
        body, name=name, grid=(t // tm,),
        in_specs=[a_spec, b_spec] + [row] * (has_res + len(rows_in)) + [kinds["vec"]] + [ANY] * len(deps),
        out_specs=[kinds[o[0]] for o in outs],
        out_shape=[_sds(shapes[o[0]], o[1] if len(o) > 1 else F32) for o in outs],
        compiler_params=_cparams("arbitrary"),
    )(*([a, b] + ([res] if has_res else []) + list(rows_in) + [vec] + deps))


def _accumulate(ref, value, first):
    @pl.when(first)
    def _():
        ref[...] = value

    @pl.when(jnp.logical_not(first))
    def _():
        ref[...] += value


def _ep_norm(val, rows, g_ref, outs, first):
    h_ref, u_ref = outs
    h_ref[...] = val
    r = lax.rsqrt(jnp.mean(val * val, axis=-1, keepdims=True) + EPS)
    u_ref[...] = (val * r * g_ref[...]).astype(u_ref.dtype)


def _ep_loss(val, rows, g_ref, outs, first):
    o_ref, ob_ref, dg_ref, l_ref = outs
    gv = g_ref[...]
    r = lax.rsqrt(jnp.mean(val * val, axis=-1, keepdims=True) + EPS)
    err = val * r * gv - rows[0][...]
    part = 0.5 * jnp.sum(jnp.mean(err * err, axis=-1, keepdims=True), axis=0, keepdims=True)
    dx, dg = _rms_bwd_math(val, gv, err * (1.0 / val.shape[1]))
    o_ref[...] = dx
    ob_ref[...] = dx.astype(BF16)
    _accumulate(dg_ref, dg, first)
    _accumulate(l_ref, jnp.broadcast_to(part, (1, LANES)), first)


def _ep_rms_bwd(val, rows, g_ref, outs, first):
    o_ref, ob_ref, dg_ref = outs
    dx, dg = _rms_bwd_math(rows[0][...], g_ref[...], val)
    dx = dx + rows[1][...]
    o_ref[...] = dx
    ob_ref[...] = dx.astype(BF16)
    _accumulate(dg_ref, dg, first)


def _rope_tables(t):
    half = HEAD_DIM // 2
    inv_freq = ROPE_THETA ** (-jnp.arange(half, dtype=F32) / half)
    ang = jnp.arange(t, dtype=F32)[:, None] * inv_freq[None, :]
    cos = jnp.cos(ang)
    sin = jnp.sin(ang)
    reps = LANES // HEAD_DIM
    cos_t = jnp.tile(jnp.concatenate([cos, cos], axis=1), (1, reps))
    sin_t = jnp.tile(jnp.concatenate([-sin, sin], axis=1), (1, reps))
    return cos_t, sin_t


def _rope(v, cos, sin):
    w = v.shape[1]
    c = jnp.tile(cos, (1, w // LANES))
    s = jnp.tile(sin, (1, w // LANES))
    lane = lax.broadcasted_iota(jnp.int32, v.shape, 1)
    first = (lane % HEAD_DIM) < (HEAD_DIM // 2)
    partner = jnp.where(first, pltpu.roll(v, w - HEAD_DIM // 2, 1), pltpu.roll(v, HEAD_DIM // 2, 1))
    return v * c + partner * s


def _heads(v, count):
    return jnp.concatenate([v[:, i * HEAD_DIM:(i + 1) * HEAD_DIM] for i in range(count)], axis=0)


def _unheads(v, count):
    r = v.shape[0] // count
    return jnp.concatenate([v[g * r:(g + 1) * r] for g in range(count)], axis=1)


def _rope_qkv(proj, cos, sin, aw, kvw):
    t = proj.shape[0]
    nkv = kvw // HEAD_DIM
    koff = aw // kvw
    tm = _pick(t, (256, 128))

    def body(q_ref, k_ref, v_ref, c_ref, s_ref, qo_ref, ko_ref, vo_ref):
        c, s = c_ref[...], s_ref[...]
        qo_ref[...] = _rope(q_ref[...].astype(F32), c, s).astype(BF16)
        k = _rope(k_ref[...].astype(F32), c, s).astype(BF16)
        v = v_ref[...].astype(BF16)
        for h in range(nkv):
            ko_ref[h] = k[:, h * HEAD_DIM:(h + 1) * HEAD_DIM]
            vo_ref[h] = v[:, h * HEAD_DIM:(h + 1) * HEAD_DIM]

    def row(w, j):
        return pl.BlockSpec((tm, w), lambda i: (i, j))

    hm = pl.BlockSpec((nkv, tm, HEAD_DIM), lambda i: (0, i, 0))
    return pl.pallas_call(
        body, name="rope_qkv", grid=(t // tm,),
        in_specs=[row(aw, 0), row(kvw, koff), row(kvw, koff + 1), row(LANES, 0), row(LANES, 0)],
        out_specs=[row(aw, 0), hm, hm],
        out_shape=[_sds((t, aw), BF16), _sds((nkv, t, HEAD_DIM), BF16), _sds((nkv, t, HEAD_DIM), BF16)],
        compiler_params=_cparams("parallel"),
    )(proj, proj, proj, cos, sin)


def _attn_probs(qs, kb, n, h, qpk, sinks_ref):
    s = lax.dot_general(qs, kb, (((1,), (1,)), ((), ())), preferred_element_type=F32) * (HEAD_DIM ** -0.5)
    qi = lax.broadcasted_iota(jnp.int32, (BLOCK, 2 * BLOCK), 0)
    kc = lax.broadcasted_iota(jnp.int32, (BLOCK, 2 * BLOCK), 1)
    valid = (kc > qi) & (kc <= qi + BLOCK) & ((kc >= BLOCK) | (n > 0))
    bias = jnp.tile(jnp.where(valid, 0.0, NEG).astype(F32), (qpk, 1))
    s = s + bias
    rowg = lax.broadcasted_iota(jnp.int32, (qpk * BLOCK, 1), 0) // BLOCK
    sink = jnp.zeros((qpk * BLOCK, 1), F32)
    for g in range(qpk):
        sink = jnp.where(rowg == g, sinks_ref[0, h * qpk + g], sink)
    m = jnp.maximum(jnp.max(s, axis=-1, keepdims=True), sink)
    e = jnp.exp(s - m)
    es = jnp.exp(sink - m)
    inv = 1.0 / (jnp.sum(e, axis=-1, keepdims=True) + es)
    return e * inv, es * inv, rowg


HEADS_PER_STEP = 4


def _attn_specs(qw, hp):
    def head(f):
        return pl.BlockSpec((hp, BLOCK, HEAD_DIM), lambda n, h: (h, f(n), 0))

    cur = lambda n: n
    prev = lambda n: jnp.maximum(n - 1, 0)
    return [pl.BlockSpec((BLOCK, hp * qw), lambda n, h: (n, h)), head(cur), head(prev), head(cur), head(prev),
            pl.BlockSpec(memory_space=pltpu.SMEM)]


def _attn_fwd(q_r, k_r, v_h, sinks):
    t, aw = q_r.shape
    nkv = k_r.shape[0]
    qpk = aw // (nkv * HEAD_DIM)
    qw = qpk * HEAD_DIM
    hp = HEADS_PER_STEP if nkv % HEADS_PER_STEP == 0 else 1

    def body(q_ref, kc_ref, kp_ref, vc_ref, vp_ref, sinks_ref, o_ref):
        n, hg = pl.program_id(0), pl.program_id(1)
        outs = []
        for j in range(hp):
            kb = jnp.concatenate([kp_ref[j], kc_ref[j]], axis=0)
            vb = jnp.concatenate([vp_ref[j], vc_ref[j]], axis=0)
            qs = _heads(q_ref[:, j * qw:(j + 1) * qw], qpk)
            p, _, _ = _attn_probs(qs, kb, n, hg * hp + j, qpk, sinks_ref)
            outs.append(_unheads(jnp.dot(p.astype(BF16), vb, preferred_element_type=F32), qpk))
        o_ref[...] = jnp.concatenate(outs, axis=1).astype(o_ref.dtype)

    return pl.pallas_call(
        body, name="attn_fwd", grid=(t // BLOCK, nkv // hp),
        in_specs=_attn_specs(qw, hp),
        out_specs=pl.BlockSpec((BLOCK, hp * qw), lambda n, h: (n, h)),
        out_shape=_sds((t, aw), BF16),
        compiler_params=_cparams("parallel", "parallel"),
    )(q_r, k_r, k_r, v_h, v_h, sinks)


def _attn_bwd(q_r, k_r, v_h, sinks, o, do, dep=None):
    t, aw = q_r.shape
    nkv = k_r.shape[0]
    qpk = aw // (nkv * HEAD_DIM)
    qw = qpk * HEAD_DIM
    hp = HEADS_PER_STEP if nkv % HEADS_PER_STEP == 0 else 1
    scale = HEAD_DIM ** -0.5

    def body(q_ref, kc_ref, kp_ref, vc_ref, vp_ref, sinks_ref, o_ref, do_ref,
             dq_ref, dkc_ref, dkp_ref, dvc_ref, dvp_ref, ds_ref):
        n, hg = pl.program_id(0), pl.program_id(1)
        lane = lax.broadcasted_iota(jnp.int32, (8, LANES), 1)
        row0 = lax.broadcasted_iota(jnp.int32, (8, LANES), 0) == 0
        dsink = jnp.zeros((8, LANES), F32)
        dqs = []
        for j in range(hp):
            h = hg * hp + j
            cols = slice(j * qw, (j + 1) * qw)
            kb = jnp.concatenate([kp_ref[j], kc_ref[j]], axis=0)
            vb = jnp.concatenate([vp_ref[j], vc_ref[j]], axis=0)
            qs = _heads(q_ref[:, cols], qpk)
            dos = _heads(do_ref[:, cols], qpk)
            p, psink, rowg = _attn_probs(qs, kb, n, h, qpk, sinks_ref)
            pb = p.astype(BF16)
            delta = jnp.sum(dos.astype(F32) * _heads(o_ref[:, cols], qpk).astype(F32), axis=-1, keepdims=True)
            dv = lax.dot_general(pb, dos, (((0,), (0,)), ((), ())), preferred_element_type=F32)
            dp = lax.dot_general(dos, vb, (((1,), (1,)), ((), ())), preferred_element_type=F32)
            dsc = (p * (dp - delta)).astype(BF16)
            dqs.append(_unheads(jnp.dot(dsc, kb, preferred_element_type=F32) * scale, qpk))
            dk = lax.dot_general(dsc, qs, (((0,), (0,)), ((), ())), preferred_element_type=F32) * scale
            dkp_ref[j] = dk[:BLOCK]
            dkc_ref[j] = dk[BLOCK:]
            dvp_ref[j] = dv[:BLOCK]
            dvc_ref[j] = dv[BLOCK:]
            sink_term = psink * delta
            for g in range(qpk):
                val = -jnp.sum(jnp.where(rowg == g, sink_term, 0.0))
                dsink = jnp.where(row0 & (lane == h * qpk + g), val, dsink)
        dq_ref[...] = jnp.concatenate(dqs, axis=1).astype(dq_ref.dtype)
        first = (n == 0) & (hg == 0)

        @pl.when(first)
        def _():
            ds_ref[...] = dsink

        @pl.when(jnp.logical_not(first))
        def _():
            ds_ref[...] += dsink

    qblk = pl.BlockSpec((BLOCK, hp * qw), lambda n, h: (n, h))
    kvblk = pl.BlockSpec((hp, BLOCK, HEAD_DIM), lambda n, h: (h, n, 0))
    pins, pin_specs = _pin(dep)
    return pl.pallas_call(
        _drop_pin(body, 8, dep), name="attn_bwd", grid=(t // BLOCK, nkv // hp),
        in_specs=_attn_specs(qw, hp) + [qblk, qblk] + pin_specs,
        out_specs=[qblk, kvblk, kvblk, kvblk, kvblk, pl.BlockSpec((8, LANES), lambda n, h: (0, 0))],
        out_shape=[_sds((t, aw), BF16)] + [_sds((nkv, t, HEAD_DIM), F32)] * 4 + [_sds((8, LANES), F32)],
        compiler_params=_cparams("arbitrary", "arbitrary"),
    )(q_r, k_r, k_r, v_h, v_h, sinks, o, do, *pins)


HALO = 16


def _shift_down(v, k, halo):
    rows = lax.broadcasted_iota(jnp.int32, v.shape, 0)
    out = pltpu.roll(v, k, 0)
    for r in range(k):
        out = jnp.where(rows == r, halo[HALO - k + r:HALO - k + r + 1, :], out)
    return out


def _shift_up(v, k, halo):
    tm = v.shape[0]
    rows = lax.broadcasted_iota(jnp.int32, v.shape, 0)
    out = pltpu.roll(v, tm - k, 0)
    for r in range(k):
        out = jnp.where(rows == tm - k + r, halo[r:r + 1, :], out)
    return out


def _conv_fwd(proj, conv_w, zoff, cw, cb):
    t = proj.shape[0]
    tm = _pick(t, (512, 256, 128))
    zb, nb = zoff // cb, cw // cb
    hb = tm // HALO

    def body(z_ref, gb_ref, gc_ref, zp_ref, gcp_ref, w_ref, o_ref):
        i = pl.program_id(0)
        cz = gc_ref[...].astype(F32) * z_ref[...].astype(F32)
        czp = gcp_ref[...].astype(F32) * zp_ref[...].astype(F32) * (i > 0).astype(F32)
        w = w_ref[...]
        y = w[0:1] * _shift_down(cz, 2, czp) + w[1:2] * _shift_down(cz, 1, czp) + w[2:3] * cz
        o_ref[...] = (gb_ref[...].astype(F32) * y).astype(o_ref.dtype)

    def col(k):
        return pl.BlockSpec((tm, cb), lambda i, j: (i, zb + k * nb + j))

    def halo(k):
        return pl.BlockSpec((HALO, cb), lambda i, j: (jnp.maximum(i * hb - 1, 0), zb + k * nb + j))

    return pl.pallas_call(
        body, name="conv_fwd", grid=(t // tm, nb),
        in_specs=[col(0), col(1), col(2), halo(0), halo(2), pl.BlockSpec((3, cb), lambda i, j: (0, j))],
        out_specs=pl.BlockSpec((tm, cb), lambda i, j: (i, j)),
        out_shape=_sds((t, cw), BF16),
        compiler_params=_cparams("parallel", "parallel"),
    )(proj, proj, proj, proj, proj, conv_w)


def _conv_bwd(proj, conv_w, dco, zoff, cw, cb, dep=None):
    t = proj.shape[0]
    tm = _pick(t, (512, 256, 128))
    zb, nb = zoff // cb, cw // cb
    hb = tm // HALO
    nt = t // tm

    def body(z_ref, gb_ref, gc_ref, zp_ref, gcp_ref, gbn_ref, w_ref, d_ref, dn_ref, dz_ref, dgb_ref, dgc_ref, dw_ref):
        i = pl.program_id(1)
        z, gb, gc = z_ref[...].astype(F32), gb_ref[...].astype(F32), gc_ref[...].astype(F32)
        d = d_ref[...].astype(F32)
        cz = gc * z
        czp = gcp_ref[...].astype(F32) * zp_ref[...].astype(F32) * (i > 0).astype(F32)
        w = w_ref[...]
        cz1 = _shift_down(cz, 1, czp)
        cz2 = _shift_down(cz, 2, czp)
        y = w[0:1] * cz2 + w[1:2] * cz1 + w[2:3] * cz
        dgb_ref[...] = (d * y).astype(dgb_ref.dtype)
        dy = d * gb
        dyn = dn_ref[...].astype(F32) * gbn_ref[...].astype(F32) * (i < nt - 1).astype(F32)
        dcz = w[2:3] * dy + w[1:2] * _shift_up(dy, 1, dyn) + w[0:1] * _shift_up(dy, 2, dyn)
        dgc_ref[...] = (dcz * z).astype(dgc_ref.dtype)
        dz_ref[...] = (dcz * gc).astype(dz_ref.dtype)
        rows = lax.broadcasted_iota(jnp.int32, (8, cb), 0)
        dw = jnp.zeros((8, cb), F32)
        for r, tap in enumerate((cz2, cz1, cz)):
            dw = jnp.where(rows == r, jnp.sum(dy * tap, axis=0, keepdims=True), dw)

        @pl.when(i == 0)
        def _():
            dw_ref[...] = dw

        @pl.when(i > 0)
        def _():
            dw_ref[...] += dw

    def col(k):
        return pl.BlockSpec((tm, cb), lambda j, i: (i, zb + k * nb + j))

    def halo_prev(k):
        return pl.BlockSpec((HALO, cb), lambda j, i: (jnp.maximum(i * hb - 1, 0), zb + k * nb + j))

    own = pl.BlockSpec((tm, cb), lambda j, i: (i, j))
    nxt = lambda i: jnp.minimum((i + 1) * hb, t // HALO - 1)
    pins, pin_specs = _pin(dep)
    return pl.pallas_call(
        _drop_pin(body, 9, dep), name="conv_bwd", grid=(nb, nt),
        in_specs=[col(0), col(1), col(2), halo_prev(0), halo_prev(2),
                  pl.BlockSpec((HALO, cb), lambda j, i: (nxt(i), zb + nb + j)),
                  pl.BlockSpec((3, cb), lambda j, i: (0, j)), own,
                  pl.BlockSpec((HALO, cb), lambda j, i: (nxt(i), j))] + pin_specs,
        out_specs=[own, own, own, pl.BlockSpec((8, cb), lambda j, i: (0, j))],
        out_shape=[_sds((t, cw), BF16)] * 3 + [_sds((8, cw), F32)],
        compiler_params=_cparams("parallel", "arbitrary"),
    )(proj, proj, proj, proj, proj, proj, conv_w, dco, dco, *pins)


def _yconv_merge(conv_o, w_cp, ya, proj, goff, cb, dep, name):
    t, k = conv_o.shape
    d = w_cp.shape[1]
    tm = _pick(t, (1024, 512, 256, 128))
    gb_, nb = goff // cb, d // cb
    deps = [] if dep is None else [dep]

    def body(a_ref, w_ref, ya_ref, ga_ref, gc_ref, *rest):
        yc_ref, o_ref = rest[len(deps):]
        yc = jnp.dot(a_ref[...], w_ref[...], preferred_element_type=F32)
        f = lambda r: r[...].astype(F32)
        yc_ref[...] = yc.astype(yc_ref.dtype)
        o_ref[...] = (_sigmoid(f(ga_ref)) * f(ya_ref) + _sigmoid(f(gc_ref)) * yc).astype(o_ref.dtype)

    own = pl.BlockSpec((tm, cb), lambda i, j: (i, j))
    return pl.pallas_call(
        body, name=name, grid=(t // tm, nb),
        in_specs=[pl.BlockSpec((tm, k), lambda i, j: (i, 0)), pl.BlockSpec((k, cb), lambda i, j: (0, j)), own,
                  pl.BlockSpec((tm, cb), lambda i, j: (i, gb_ + j)),
                  pl.BlockSpec((tm, cb), lambda i, j: (i, gb_ + nb + j))] + [ANY] * len(deps),
        out_specs=[own, own], out_shape=[_sds((t, d), BF16)] * 2,
        compiler_params=_cparams("parallel", "parallel"),
    )(conv_o, w_cp, ya, proj, proj, *deps)


def _dmerged_split(dh, w_mo, proj, ya, yc, goff, cb, dep, name):
    t, k = dh.shape
    d = w_mo.shape[0]
    tm = _pick(t, (1024, 512, 256, 128))
    gb_, nb = goff // cb, d // cb
    deps = [] if dep is None else [dep]

    def body(a_ref, w_ref, ga_ref, gc_ref, ya_ref, yc_ref, *rest):
        dya_ref, dyc_ref, dga_ref, dgc_ref = rest[len(deps):]
        dmv = lax.dot_general(a_ref[...], w_ref[...], (((1,), (1,)), ((), ())), preferred_element_type=F32)
        sa = _sigmoid(ga_ref[...].astype(F32))
        sc = _sigmoid(gc_ref[...].astype(F32))
        dya_ref[...] = (dmv * sa).astype(BF16)
        dyc_ref[...] = (dmv * sc).astype(BF16)
        dga_ref[...] = (dmv * ya_ref[...].astype(F32) * sa * (1.0 - sa)).astype(BF16)
        dgc_ref[...] = (dmv * yc_ref[...].astype(F32) * sc * (1.0 - sc)).astype(BF16)

    own = pl.BlockSpec((tm, cb), lambda i, j: (i, j))
    return pl.pallas_call(
        body, name=name, grid=(t // tm, nb),
        in_specs=[pl.BlockSpec((tm, k), lambda i, j: (i, 0)), pl.BlockSpec((cb, k), lambda i, j: (j, 0)),
                  pl.BlockSpec((tm, cb), lambda i, j: (i, gb_ + j)),
                  pl.BlockSpec((tm, cb), lambda i, j: (i, gb_ + nb + j)), own, own] + [ANY] * len(deps),
        out_specs=[own] * 4, out_shape=[_sds((t, d), BF16)] * 4,
        compiler_params=_cparams("parallel", "parallel"),
    )(dh, w_mo, proj, proj, ya, yc, *deps)


def _assemble_dproj(dq, dkc, dkp, dvc, dvp, cos, sin, dz, dgb, dgc, dga, dgg, dep=None):
    t, aw = dq.shape
    nkv, cw, d = dkc.shape[0], dz.shape[1], dga.shape[1]
    kvw = nkv * HEAD_DIM
    nblk = t // BLOCK
    width = aw + 2 * kvw + 3 * cw + 2 * d

    def body(dq_ref, dkc_ref, dkp_ref, dvc_ref, dvp_ref, c_ref, s_ref, dz_ref, dgb_ref, dgc_ref, dga_ref, dgg_ref,
             o_ref):
        keep = (pl.program_id(0) < nblk - 1).astype(F32)
        c, s = c_ref[...], s_ref[...]
        dk = jnp.concatenate([dkc_ref[h] + dkp_ref[h] * keep for h in range(nkv)], axis=1)
        dv = jnp.concatenate([dvc_ref[h] + dvp_ref[h] * keep for h in range(nkv)], axis=1)
        o_ref[...] = jnp.concatenate(
            [_rope(dq_ref[...].astype(F32), c, -s).astype(BF16), _rope(dk, c, -s).astype(BF16), dv.astype(BF16),
             dz_ref[...], dgb_ref[...], dgc_ref[...], dga_ref[...], dgg_ref[...]], axis=1)

    def cur(w):
        return pl.BlockSpec((BLOCK, w), lambda n: (n, 0))

    head_cur = pl.BlockSpec((nkv, BLOCK, HEAD_DIM), lambda n: (0, n, 0))
    head_nxt = pl.BlockSpec((nkv, BLOCK, HEAD_DIM), lambda n: (0, jnp.minimum(n + 1, nblk - 1), 0))
    pins, pin_specs = _pin(dep)
    return pl.pallas_call(
        _drop_pin(body, 12, dep), name="assemble_dproj", grid=(nblk,),
        in_specs=[cur(aw), head_cur, head_nxt, head_cur, head_nxt, cur(LANES), cur(LANES),
                  cur(cw), cur(cw), cur(cw), cur(d), cur(d)] + pin_specs,
        out_specs=cur(width), out_shape=_sds((t, width), BF16),
        compiler_params=_cparams("parallel"),
    )(dq, dkc, dkp, dvc, dvp, cos, sin, dz, dgb, dgc, dga, dgg, *pins)


def _xattn_probs(qh, kh):
    s = lax.dot_general(qh, kh, (((1,), (1,)), ((), ())), preferred_element_type=F32) * (X_HEAD_DIM ** -0.5)
    e = jnp.exp(s - jnp.max(s, axis=-1, keepdims=True))
    return e * (1.0 / jnp.sum(e, axis=-1, keepdims=True))


def _xattn_fwd(xq, kv):
    t, xw = xq.shape
    mt = kv.shape[0]
    tm = _pick(t, (512, 256, 128))

    def body(q_ref, kv_ref, o_ref):
        outs = []
        for hd in range(xw // X_HEAD_DIM):
            hs = slice(hd * X_HEAD_DIM, (hd + 1) * X_HEAD_DIM)
            vs = slice(xw + hd * X_HEAD_DIM, xw + (hd + 1) * X_HEAD_DIM)
            p = _xattn_probs(q_ref[:, hs], kv_ref[:, hs])
            outs.append(jnp.dot(p.astype(BF16), kv_ref[:, vs], preferred_element_type=F32))
        o_ref[...] = jnp.concatenate(outs, axis=1).astype(o_ref.dtype)

    return pl.pallas_call(
        body, name="xattn_fwd", grid=(t // tm,),
        in_specs=[pl.BlockSpec((tm, xw), lambda i: (i, 0)), pl.BlockSpec((mt, 2 * xw), lambda i: (0, 0))],
        out_specs=pl.BlockSpec((tm, xw), lambda i: (i, 0)), out_shape=_sds((t, xw), BF16),
        compiler_params=_cparams("parallel"),
    )(xq, kv)


def _xattn_bwd(xq, kv, do):
    t, xw = xq.shape
    mt = kv.shape[0]
    tm = _pick(t, (512, 256, 128))
    scale = X_HEAD_DIM ** -0.5

    def body(q_ref, kv_ref, do_ref, dq_ref, dkv_ref):
        dqs, dks, dvs = [], [], []
        for hd in range(xw // X_HEAD_DIM):
            hs = slice(hd * X_HEAD_DIM, (hd + 1) * X_HEAD_DIM)
            vs = slice(xw + hd * X_HEAD_DIM, xw + (hd + 1) * X_HEAD_DIM)
            qh, kh, vh, doh = q_ref[:, hs], kv_ref[:, hs], kv_ref[:, vs], do_ref[:, hs]
            p = _xattn_probs(qh, kh)
            pb = p.astype(BF16)
            o = jnp.dot(pb, vh, preferred_element_type=F32)
            delta = jnp.sum(doh.astype(F32) * o, axis=-1, keepdims=True)
            dvs.append(lax.dot_general(pb, doh, (((0,), (0,)), ((), ())), preferred_element_type=F32))
            dp = lax.dot_general(doh, vh, (((1,), (1,)), ((), ())), preferred_element_type=F32)
            dsc = (p * (dp - delta)).astype(BF16)
            dqs.append(jnp.dot(dsc, kh, preferred_element_type=F32) * scale)
            dks.append(lax.dot_general(dsc, qh, (((0,), (0,)), ((), ())), preferred_element_type=F32) * scale)
        dq_ref[...] = jnp.concatenate(dqs, axis=1).astype(dq_ref.dtype)
        dkv = jnp.concatenate(dks + dvs, axis=1)

        @pl.when(pl.program_id(0) == 0)
        def _():
            dkv_ref[...] = dkv

        @pl.when(pl.program_id(0) > 0)
        def _():
            dkv_ref[...] += dkv

    row = pl.BlockSpec((tm, xw), lambda i: (i, 0))
    whole = pl.BlockSpec((mt, 2 * xw), lambda i: (0, 0))
    return pl.pallas_call(
        body, name="xattn_bwd", grid=(t // tm,),
        in_specs=[row, whole, row], out_specs=[row, whole],
        out_shape=[_sds((t, xw), BF16), _sds((mt, 2 * xw), F32)],
        compiler_params=_cparams("arbitrary"),
    )(xq, kv, do)


def _ffn_in_swiglu(u, w, dep, name):
    t, d = u.shape
    f = w.shape[1] // 2
    tm, tn, _ = _tiles("nn", t, f)
    nf = f // tn
    deps = [] if dep is None else [dep]

    def body(u_ref, wa_ref, wb_ref, *rest):
        h_ref, o_ref = rest[len(deps):]
        a = jnp.dot(u_ref[...], wa_ref[...], preferred_element_type=F32)
        b = jnp.dot(u_ref[...], wb_ref[...], preferred_element_type=F32)
        h_ref[0] = a.astype(h_ref.dtype)
        h_ref[1] = b.astype(h_ref.dtype)
        o_ref[...] = (a * _sigmoid(a) * b).astype(o_ref.dtype)

    return pl.pallas_call(
        body, name=name, grid=(t // tm, nf),
        in_specs=[pl.BlockSpec((tm, d), lambda i, j: (i, 0)), pl.BlockSpec((d, tn), lambda i, j: (0, j)),
                  pl.BlockSpec((d, tn), lambda i, j: (0, nf + j))] + [ANY] * len(deps),
        out_specs=[pl.BlockSpec((2, tm, tn), lambda i, j: (0, i, j)), pl.BlockSpec((tm, tn), lambda i, j: (i, j))],
        out_shape=[_sds((2, t, f), BF16), _sds((t, f), BF16)],
        compiler_params=_cparams("parallel", "parallel"),
    )(u, w, w, *deps)


def _dact_swiglu(dh, w_out, hid, dep, name):
    t, d = dh.shape
    f = w_out.shape[0]
    tm, tn, _ = _tiles("nt", t, f)
    deps = [] if dep is None else [dep]

    def body(dh_ref, w_ref, h_ref, *rest):
        o_ref = rest[len(deps)]
        g = lax.dot_general(dh_ref[...], w_ref[...], (((1,), (1,)), ((), ())), preferred_element_type=F32)
        a = h_ref[0].astype(F32)
        b = h_ref[1].astype(F32)
        sg = _sigmoid(a)
        o_ref[0] = (g * b * sg * (1.0 + a * (1.0 - sg))).astype(o_ref.dtype)
        o_ref[1] = (g * a * sg).astype(o_ref.dtype)

    pair = pl.BlockSpec((2, tm, tn), lambda i, j: (0, i, j))
    return pl.pallas_call(
        body, name=name, grid=(t // tm, f // tn),
        in_specs=[pl.BlockSpec((tm, d), lambda i, j: (i, 0)), pl.BlockSpec((tn, d), lambda i, j: (j, 0)), pair]
        + [ANY] * len(deps),
        out_specs=pair, out_shape=_sds((2, t, f), BF16),
        compiler_params=_cparams("parallel", "parallel"),
    )(dh, w_out, hid, *deps)


def _adamw(w, g, m, v, name):
    r, c = w.shape
    tr = _pick(r, (256, 128, 64, 32, 16, 8)) if r * c > 65536 else r

    def body(w_ref, g_ref, m_ref, v_ref, d_ref, nm_ref, nv_ref, go_ref):
        gv = g_ref[...]
        go_ref[...] = gv
        m2 = ADAM_B1 * m_ref[...] + (1.0 - ADAM_B1) * gv
        v2 = ADAM_B2 * v_ref[...] + (1.0 - ADAM_B2) * (gv * gv)
        m_hat = m2 / (1.0 - ADAM_B1 ** ADAM_STEP)
        v_hat = v2 / (1.0 - ADAM_B2 ** ADAM_STEP)
        d_ref[...] = -ADAM_LR * (m_hat / (jnp.sqrt(v_hat) + ADAM_EPS) + ADAM_WD * w_ref[...])
        nm_ref[...] = m2
        nv_ref[...] = v2

    blk = pl.BlockSpec((tr, c), lambda i: (i, 0))
    return pl.pallas_call(
        body, name=name, grid=(r // tr,),
        in_specs=[blk] * 4, out_specs=[blk] * 4, out_shape=[_sds((r, c), F32)] * 4,
        compiler_params=_cparams("parallel"),
    )(w, g, m, v)


class _Weights:
    def __init__(self, full):
        self.full = full
        self.grads = {}

    def get(self, name):
        return self.full[name]

    def mark(self, tag, value):
        return value

    def grad(self, name, g):
        self.grads[name] = g

    def dep(self):
        return None


def _local_step(x, mem, tgt, g_mix, sinks, g_xattn, g_mem, g_ffn, g_final, dims, wts):
    t, d = x.shape
    aw, cw, kvw = dims
    cb = 2 * kvw
    zoff = aw + 2 * kvw
    goff = zoff + 3 * cw
    cos, sin = _rope_tables(t)
    mark, get = wts.mark, wts.get

    def mm(a, b, **kw):
        return _matmul(a, b, dep=wts.dep(), **kw)

    u1 = mark("u1", _rms_fwd(x, g_mix, "rms_mix"))
    qkv = mm(u1, get("w_in"), mode="nn", out_dtype=F32, name="mm_qkv", cols=(0, zoff))
    proj = mark("proj", mm(u1, get("w_in"), mode="nn", out_dtype=BF16, name="mm_gates", cols=(zoff, goff - zoff + 2 * d)))
    zoff, goff = 0, goff - zoff
    q_r, k_r, v_h = _rope_qkv(qkv, cos, sin, aw, kvw)
    attn_o = mark("attn_o", _attn_fwd(q_r, k_r, v_h, sinks))
    conv_o = mark("conv_o", _conv_fwd(proj, get("conv_w"), zoff, cw, cb))
    ya = mark("ya", mm(attn_o, get("w_attn_proj"), mode="nn", out_dtype=BF16, name="mm_yattn"))
    yc, merged = _yconv_merge(conv_o, get("w_conv_proj"), ya, proj, goff, cb, wts.dep(), "mm_yconv")
    mark("merged", merged)
    norm_outs = [("row", F32), ("row", BF16)]
    bwd_outs = [("row", F32), ("row", BF16), ("vec",)]
    h1, u2 = _rows_matmul(merged, get("w_mix_out"), mode="nn", kslice=(0, 1), res=x, rows_in=[], vec=g_xattn,
                          epilogue=_ep_norm, outs=norm_outs, name="mm_mix", dep=wts.dep())
    mark("h1", h1)
    mem_n = _rms_fwd(mem, g_mem, "rms_mem")
    xq = mark("xq", mm(u2, get("w_xq"), mode="nn", out_dtype=BF16, name="mm_xq"))
    kv = mm(mem_n, get("w_xkv"), mode="nn", out_dtype=BF16, name="mm_xkv")
    xo = mark("xo", _xattn_fwd(xq, kv))
    h2, u3 = _rows_matmul(xo, get("w_xo"), mode="nn", kslice=(0, 1), res=h1, rows_in=[], vec=g_ffn,
                          epilogue=_ep_norm, outs=norm_outs, name="mm_xo", dep=wts.dep())
    mark("h2", h2)
    mark("u3", u3)
    hid, act = _ffn_in_swiglu(u3, get("w_ffn_in"), wts.dep(), "mm_ffn_in")
    mark("hid", hid)
    dh3, dh3b, dg_final, loss = _rows_matmul(
        act, get("w_ffn_out"), mode="nn", kslice=(0, 1), res=h2, rows_in=[tgt], vec=g_final, epilogue=_ep_loss,
        outs=bwd_outs + [("lane",)], name="mm_ffn_out", dep=wts.dep())
    mark("dh3", dh3b)

    wts.grad("w_ffn_out", mm(act, dh3b, mode="tn", out_dtype=BF16, name="mm_dw_ffn_out"))
    dhid = mark("dhid", _dact_swiglu(dh3b, get("w_ffn_out"), hid, wts.dep(), "mm_dact"))
    wts.grad("w_ffn_in", mm(u3, dhid, mode="tn", out_dtype=BF16, name="mm_dw_ffn_in"))
    du3 = mark("du3", _matmul_slice(dhid, get("w_ffn_in"), mode="nt", out_dtype=F32, name="mm_du3_k0", res=None,
                                    dep=wts.dep(), kslice=(0, 2), cols=(0, d)))
    dh2, dh2b, dg_ffn = _rows_matmul(dhid, get("w_ffn_in"), mode="nt", kslice=(1, 2), res=du3, rows_in=[h2, dh3],
                                     vec=g_ffn, epilogue=_ep_rms_bwd, outs=bwd_outs, name="mm_du3_k1", dep=wts.dep())
    mark("dh2", dh2b)
    wts.grad("w_xo", mm(xo, dh2b, mode="tn", out_dtype=BF16, name="mm_dw_xo"))
    dxo = mm(dh2b, get("w_xo"), mode="nt", out_dtype=BF16, name="mm_dxo")
    dxq, dkv = _xattn_bwd(xq, kv, dxo)
    dkvb = dkv.astype(BF16)
    wts.grad("w_xq", mm(u2, dxq, mode="tn", out_dtype=BF16, name="mm_dw_xq"))
    wts.grad("w_xkv", mm(mem_n, dkvb, mode="tn", out_dtype=BF16, name="mm_dw_xkv"))
    dmem_n = mm(dkvb, get("w_xkv"), mode="nt", out_dtype=F32, name="mm_dmem")
    _, _, dg_mem = _rms_bwd(mem, g_mem, dmem_n, None, "rms_bwd_mem")
    dh1, dh1b, dg_xattn = _rows_matmul(dxq, get("w_xq"), mode="nt", kslice=(0, 1), res=None, rows_in=[h1, dh2],
                                       vec=g_xattn, epilogue=_ep_rms_bwd, outs=bwd_outs, name="mm_du2", dep=wts.dep())
    mark("dh1", dh1b)
    wts.grad("w_mix_out", mm(merged, dh1b, mode="tn", out_dtype=BF16, name="mm_dw_mix"))
    dya, dyc, dga, dgg = _dmerged_split(dh1b, get("w_mix_out"), proj, ya, yc, goff, cb, wts.dep(), "mm_dmerged")
    mark("dya", dya)
    wts.grad("w_attn_proj", mm(attn_o, dya, mode="tn", out_dtype=BF16, name="mm_dw_attn_proj"))
    dattn_o = mm(dya, get("w_attn_proj"), mode="nt", out_dtype=BF16, name="mm_dattn")
    wts.grad("w_conv_proj", mm(conv_o, dyc, mode="tn", out_dtype=BF16, name="mm_dw_conv_proj"))
    dconv_o = mark("dconv_o", mm(dyc, get("w_conv_proj"), mode="nt", out_dtype=BF16, name="mm_dconv"))
    dz, dgb, dgc, dconv_w = _conv_bwd(proj, get("conv_w"), dconv_o, zoff, cw, cb, wts.dep())
    mark("dz", dz)
    dq, dkc, dkp, dvc, dvp, dsinks = _attn_bwd(q_r, k_r, v_h, sinks, attn_o, dattn_o, wts.dep())
    mark("dq", dq)
    dproj = mark("dproj", _assemble_dproj(dq, dkc, dkp, dvc, dvp, cos, sin, dz, dgb, dgc, dga, dgg, wts.dep()))
    wts.grad("w_in", mm(u1, dproj, mode="tn", out_dtype=BF16, name="mm_dw_in"))
    du1 = mark("du1", _matmul_slice(dproj, get("w_in"), mode="nt", out_dtype=F32, name="mm_du1_k0", res=None,
                                    dep=wts.dep(), kslice=(0, 2), cols=(0, d)))
    grad_x, _, dg_mix = _rows_matmul(dproj, get("w_in"), mode="nt", kslice=(1, 2), res=du1, rows_in=[x, dh1],
                                     vec=g_mix, epilogue=_ep_rms_bwd, outs=bwd_outs, name="mm_du1_k1", dep=wts.dep())
    mark("grad_x", grad_x)

    small = dict(g_mix=dg_mix, g_xattn=dg_xattn, g_mem=dg_mem, g_ffn=dg_ffn, g_final=dg_final,
                 conv_w=dconv_w[:3], attn_sinks=dsinks[0:1, :sinks.shape[1]], loss=loss[0:1, 0:1])
    return grad_x, small


BIG = (("w_in", 1), ("w_attn_proj", 1), ("w_conv_proj", 1), ("w_mix_out", 0), ("w_xq", 0), ("w_xkv", 0),
       ("w_xo", 1), ("w_ffn_in", 1), ("w_ffn_out", 0))


def _place():
    x, y, c = lax.axis_index("x"), lax.axis_index("y"), lax.axis_index("c")
    chips = [(1 - x, y), (x, 1 - y), (1 - x, 1 - y)]
    return x, y, c, chips


def _window(ref, ax, shard_shape, s, h):
    sr, sc = shard_shape
    hr = sr // 2
    if ax == 1:
        return ref.at[pl.ds(pl.multiple_of(h * hr, 16), hr), pl.ds(pl.multiple_of(s * sc, LANES), sc)]
    return ref.at[pl.ds(pl.multiple_of(s * sr + h * hr, 16), hr), :]


def _half(ref, h):
    hr = ref.shape[0] // 2
    return ref.at[pl.ds(pl.multiple_of(h * hr, 16), hr), :]


def _remote(src, dst, send_sem, recv_sem, dev):
    return pltpu.make_async_remote_copy(src_ref=src, dst_ref=dst, send_sem=send_sem, recv_sem=recv_sem,
                                        device_id=dev, device_id_type=MESH)


def _cast_to_full(shard, ax, me, name, dtype=BF16):
    sr, sc = shard.shape
    tr = _pick(sr, (256, 352, 128, 64, 32, 16))
    nr = sr // tr
    full = (sr * N_CHIPS, sc) if ax == 0 else (sr, sc * N_CHIPS)

    def body(me_ref, s_ref, o_ref):
        o_ref[...] = s_ref[...].astype(o_ref.dtype)

    if ax == 1:
        out_spec = pl.BlockSpec((tr, sc), lambda r, me_ref: (r, me_ref[0]))
    else:
        out_spec = pl.BlockSpec((tr, sc), lambda r, me_ref: (me_ref[0] * nr + r, 0))
    return pl.pallas_call(
        body, name=name,
        grid_spec=pltpu.PrefetchScalarGridSpec(
            num_scalar_prefetch=1, grid=(nr,), in_specs=[pl.BlockSpec((tr, sc), lambda r, me_ref: (r, 0))],
            out_specs=out_spec),
        out_shape=_sds(full, dtype),
        compiler_params=_cparams("parallel"),
    )(me, shard)


HBM = pl.BlockSpec(memory_space=pltpu.HBM)
SEM = pl.BlockSpec(memory_space=pltpu.SEMAPHORE)
EFFECT = pltpu.SideEffectType.DATAFLOW_SIDE_EFFECTING


def _in_hbm(a):
    return pltpu.with_memory_space_constraint(a, pltpu.HBM)


def _gather_window(ref, ax, shard_shape, s, h):
    if h is not None:
        return _window(ref, ax, shard_shape, s, h)
    sr, sc = shard_shape
    if ax == 1:
        return ref.at[:, pl.ds(pl.multiple_of(s * sc, LANES), sc)]
    return ref.at[pl.ds(pl.multiple_of(s * sr, 8), sr), :]


def _ag_start(fulls, axes, shard_shapes, whole):
    n = len(fulls)

    def body(*refs):
        src = refs[:n]
        send_sems, recv_sems = refs[n], refs[n + 1]
        token = refs[2 * n + 2]
        x, y, c, chips = _place()
        me = 2 * x + y
        for i in range(n):
            h = None if whole[i] else c
            for j, chip in enumerate(chips):
                blk = _gather_window(src[i], axes[i], shard_shapes[i], me, h)
                _remote(blk, blk, send_sems.at[3 * i + j], recv_sems.at[3 * i + j], (*chip, c)).start()
        token[...] = jnp.zeros_like(token)

    res = pl.pallas_call(
        body, name="ag_start_" + str(n),
        out_shape=(pltpu.SemaphoreType.DMA((3 * n,)), pltpu.SemaphoreType.DMA((3 * n,)),
                   *[pltpu.HBM(f.shape, f.dtype) for f in fulls], _sds((8, LANES), F32)),
        in_specs=[HBM] * n, out_specs=(SEM, SEM, *[HBM] * n, pl.BlockSpec(memory_space=pltpu.VMEM)),
        input_output_aliases={i: 2 + i for i in range(n)},
        compiler_params=pltpu.CompilerParams(has_side_effects=EFFECT),
    )(*[_in_hbm(f) for f in fulls])
    return res[0], res[1], list(res[2:2 + n]), res[2 + n]


def _ag_mid(bufs, slots, axes, shard_shapes, whole, send_sems, recv_sems, after, name):
    ng = len(bufs)

    def body(*refs):
        src = refs[:ng]
        s_in, r_in = refs[ng], refs[ng + 1]
        fsend, frecv = refs[ng + 3], refs[ng + 4]
        x, y, c, chips = _place()
        me = 2 * x + y
        sib = (x, y, 1 - c)
        for k, i in enumerate(slots):
            h = None if whole[k] else c
            for j, chip in enumerate(chips):
                cj = 2 * chip[0] + chip[1]
                mine = _gather_window(src[k], axes[k], shard_shapes[k], me, h)
                theirs = _gather_window(src[k], axes[k], shard_shapes[k], cj, h)
                _remote(theirs, theirs, s_in.at[3 * i + j], r_in.at[3 * i + j], (*chip, c)).wait_recv()
                _remote(mine, mine, s_in.at[3 * i + j], r_in.at[3 * i + j], (*chip, c)).wait_send()
                if not whole[k]:
                    _remote(theirs, theirs, fsend.at[3 * k + j], frecv.at[3 * k + j], sib).start()
        token = refs[2 * ng + 5]
        token[...] = jnp.zeros_like(token)

    res = pl.pallas_call(
        body, name=name,
        out_shape=(pltpu.SemaphoreType.DMA((3 * ng,)), pltpu.SemaphoreType.DMA((3 * ng,)),
                   *[pltpu.HBM(b.shape, b.dtype) for b in bufs], _sds((8, LANES), F32)),
        in_specs=[HBM] * ng + [SEM, SEM, ANY],
        out_specs=(SEM, SEM, *[HBM] * ng, pl.BlockSpec(memory_space=pltpu.VMEM)),
        input_output_aliases={k: 2 + k for k in range(ng)},
        compiler_params=pltpu.CompilerParams(has_side_effects=EFFECT),
    )(*bufs, send_sems, recv_sems, after)
    return res[0], res[1], list(res[2:2 + ng]), res[2 + ng]


def _ag_wait(bufs, axes, shard_shapes, whole, fsend, frecv, after, name):
    ng = len(bufs)

    def body(*refs):
        src = refs[:ng]
        s_in, r_in = refs[ng], refs[ng + 1]
        x, y, c, chips = _place()
        sib = (x, y, 1 - c)
        for k in range(ng):
            if whole[k]:
                continue
            for j, chip in enumerate(chips):
                cj = 2 * chip[0] + chip[1]
                sent = _gather_window(src[k], axes[k], shard_shapes[k], cj, c)
                landed = _gather_window(src[k], axes[k], shard_shapes[k], cj, 1 - c)
                _remote(landed, landed, s_in.at[3 * k + j], r_in.at[3 * k + j], sib).wait_recv()
                _remote(sent, sent, s_in.at[3 * k + j], r_in.at[3 * k + j], sib).wait_send()

    res = pl.pallas_call(
        body, name=name,
        out_shape=tuple(pltpu.HBM(b.shape, b.dtype) for b in bufs),
        in_specs=[HBM] * ng + [SEM, SEM, ANY], out_specs=tuple([HBM] * ng),
        input_output_aliases={k: k for k in range(ng)},
        compiler_params=pltpu.CompilerParams(has_side_effects=EFFECT),
    )(*bufs, fsend, frecv, after)
    return list(res)


class _Schedule:
    GROUPS = ((("w_in", "conv_w"), "u1", "u1"),
              (("w_attn_proj", "w_conv_proj", "w_mix_out", "w_xq", "w_xkv", "w_xo"), "proj", "conv_o"),
              (("w_ffn_in",), "h1", "h2"),
              (("w_ffn_out",), "u3", "hid"))
    STARTS = ((0,), (1, 2, 3))
    REDUCE = ((("w_ffn_out",), "dhid", "grad:w_ffn_in", "du3"),
              (("w_ffn_in",), "grad:w_ffn_in", "dya", "grad_x"),
              (("w_xo", "w_xq", "w_xkv", "w_mix_out", "w_attn_proj", "w_conv_proj"), "dconv_o", "dq", "grad:w_in"),
              (("w_in",), "grad:w_in", "end", "end2"))

    def __init__(self, seed, axes, shard_shapes, place, on_ready):
        self.ax, self.shape, self.place, self.on_ready = axes, shard_shapes, place, on_ready
        self.stage, self.buf, self.slot, self.passes = {}, {}, {}, {}
        self.ready = set()
        self.grads = {}
        token = None
        for groups in self.STARTS:
            order = [nm for g in groups for nm in self.GROUPS[g][0]]
            send, recv, bufs, token = _ag_start([seed(nm, token) for nm in order], *self._meta(order))
            self.buf.update(zip(order, bufs))
            self.slot.update({nm: (send, recv, k) for k, nm in enumerate(order)})
        self.token = self.latest = token
        self.mark("start", token)

    def _meta(self, names):
        return ([self.ax[nm] for nm in names], [self.shape[nm] for nm in names], [nm == "conv_w" for nm in names])

    def mark(self, tag, value):
        marked = value
        for g, (names, mid, wait) in enumerate(self.GROUPS):
            if tag == mid:
                send, recv, _ = self.slot[names[0]]
                fs, fr, bufs, self.latest = _ag_mid([self.buf[nm] for nm in names], [self.slot[nm][2] for nm in names],
                                                    *self._meta(names), send, recv, value, "ag_mid_%d" % g)
                self.buf.update(zip(names, bufs))
                self.passes[g] = (fs, fr)
            if tag == wait:
                fs, fr = self.passes[g]
                bufs = _ag_wait([self.buf[nm] for nm in names], *self._meta(names), fs, fr, value, "ag_wait_%d" % g)
                self.buf.update(zip(names, bufs))
                self.ready.update(names)
        for g, (names, send, total, finish) in enumerate(self.REDUCE):
            st = self.stage.get(g)
            if st is None:
                continue
            ng = len(names)
            if tag == send and st["at"] == "pair":
                arrs = _exchange_wait("rs_pair_wait_%d" % g, st["arrs"], *st["sems"], st["plan"], value)
                parts = [_pair_add(arrs[k], arrs[ng + k], self.ax[nm], self.shape[nm], self.place, "pair_add_" + nm)
                         for k, nm in enumerate(names)]
                plan, nsem = _plan_chip(ng)
                ss, rs, arrs, self.latest = _exchange_start(
                    "rs_chip_start_%d" % g, parts + [lax.empty(p.shape, p.dtype) for p in parts], nsem, plan)
                self.stage[g] = dict(at="chip", arrs=arrs, sems=(ss, rs), plan=plan)
            elif tag == total and st["at"] == "chip":
                arrs = _exchange_wait("rs_chip_wait_%d" % g, st["arrs"], *st["sems"], st["plan"], value)
                halves = [_chip_add(arrs[k], arrs[ng + k], self.place, "chip_add_" + nm) for k, nm in enumerate(names)]
                plan, nsem = _plan_gather(ng)
                ss, rs, arrs, self.latest = _exchange_start("rs_gather_start_%d" % g, halves, nsem, plan)
                self.stage[g] = dict(at="gather", arrs=arrs, sems=(ss, rs), plan=plan)
            st = self.stage[g]
            if tag == finish and st["at"] == "gather":
                arrs = _exchange_wait("rs_gather_wait_%d" % g, st["arrs"], *st["sems"], st["plan"], value)
                self.stage[g] = dict(at="done")
                for nm, shard in zip(names, arrs):
                    value = self.on_ready(nm, shard)
        return marked

    def get(self, name):
        assert name in self.ready, name
        return self.buf[name]

    def grad(self, name, g):
        self.grads[name] = g
        for gi, (names, _, _, _) in enumerate(self.REDUCE):
            if name == names[-1]:
                gs = [self.grads[nm] for nm in names]
                plan, nsem = _plan_pair(len(names), [self.ax[nm] for nm in names], [self.shape[nm] for nm in names])
                ss, rs, arrs, self.latest = _exchange_start(
                    "rs_pair_start_%d" % gi, gs + [lax.empty(a.shape, a.dtype) for a in gs], nsem, plan)
                self.stage[gi] = dict(at="pair", arrs=arrs, sems=(ss, rs), plan=plan)
                g = self.latest
        self.mark("grad:" + name, g)

    def dep(self):
        return self.latest


def _exchange_start(name, arrays, nsem, plan):
    n = len(arrays)

    def body(*refs):
        send_sems, recv_sems, token = refs[n], refs[n + 1], refs[2 * n + 2]
        sends, _ = plan(refs[:n])
        for k, (src, dst, dev) in enumerate(sends):
            _remote(src, dst, send_sems.at[k], recv_sems.at[k], dev).start()
        token[...] = jnp.zeros_like(token)

    res = pl.pallas_call(
        body, name=name,
        out_shape=(pltpu.SemaphoreType.DMA((nsem,)), pltpu.SemaphoreType.DMA((nsem,)),
                   *[pltpu.HBM(a.shape, a.dtype) for a in arrays], _sds((8, LANES), F32)),
        in_specs=[HBM] * n, out_specs=(SEM, SEM, *[HBM] * n, pl.BlockSpec(memory_space=pltpu.VMEM)),
        input_output_aliases={i: 2 + i for i in range(n)},
        compiler_params=pltpu.CompilerParams(has_side_effects=EFFECT),
    )(*[_in_hbm(a) for a in arrays])
    return res[0], res[1], list(res[2:2 + n]), res[2 + n]


def _exchange_wait(name, arrays, send_sems, recv_sems, plan, after):
    n = len(arrays)

    def body(*refs):
        s_in, r_in = refs[n], refs[n + 1]
        sends, recvs = plan(refs[:n])
        for k, land in enumerate(recvs):
            _remote(land, land, s_in.at[k], r_in.at[k], sends[k][2]).wait_recv()
        for k, (src, _, dev) in enumerate(sends):
            _remote(src, src, s_in.at[k], r_in.at[k], dev).wait_send()

    res = pl.pallas_call(
        body, name=name,
        out_shape=tuple(pltpu.HBM(a.shape, a.dtype) for a in arrays),
        in_specs=[HBM] * n + [SEM, SEM, ANY], out_specs=tuple([HBM] * n),
        input_output_aliases={i: i for i in range(n)},
        compiler_params=pltpu.CompilerParams(has_side_effects=EFFECT),
    )(*arrays, send_sems, recv_sems, after)
    return list(res)


def _plan_pair(n, axes, shard_shapes):
    def plan(refs):
        g, ra = refs[:n], refs[n:]
        x, y, c, _ = _place()
        sib = (x, y, 1 - c)

        def pieces(ref, i, h):
            if axes[i] == 1:
                return [_half(ref, h)]
            return [_window(ref, 0, shard_shapes[i], s, h) for s in range(N_CHIPS)]

        sends, recvs = [], []
        for i in range(n):
            sends += [(src, dst, sib) for src, dst in zip(pieces(g[i], i, 1 - c), pieces(ra[i], i, 1 - c))]
            recvs += pieces(ra[i], i, c)
        return sends, recvs

    return plan, sum(1 if ax == 1 else N_CHIPS for ax in axes)


def _plan_chip(n):
    def plan(refs):
        p, rc = refs[:n], refs[n:]
        x, y, c, chips = _place()
        me = 2 * x + y
        sends, recvs = [], []
        for i in range(n):
            for chip in chips:
                cj = 2 * chip[0] + chip[1]
                sends.append((p[i].at[cj], rc[i].at[me], (*chip, c)))
                recvs.append(rc[i].at[cj])
        return sends, recvs

    return plan, 3 * n


def _plan_gather(n):
    def plan(refs):
        x, y, c, _ = _place()
        sib = (x, y, 1 - c)
        return ([(_half(r, c), _half(r, c), sib) for r in refs], [_half(r, 1 - c) for r in refs])

    return plan, n


def _pair_add(g, ra, ax, shard_shape, place, name):
    sr, sc = shard_shape
    hr = sr // 2
    wc = sc
    tr = _pick(hr, (256, 352, 128, 64, 32, 16))
    nr = hr // tr

    def body(p_ref, a_ref, b_ref, o_ref):
        o_ref[...] = (a_ref[...].astype(F32) + b_ref[...].astype(F32)).astype(o_ref.dtype)

    if ax == 1:
        src = pl.BlockSpec((tr, wc), lambda s, r, p_ref: (p_ref[0] * nr + r, s))
    else:
        src = pl.BlockSpec((tr, wc), lambda s, r, p_ref: (s * 2 * nr + p_ref[0] * nr + r, 0))
    return pl.pallas_call(
        body, name=name,
        grid_spec=pltpu.PrefetchScalarGridSpec(
            num_scalar_prefetch=1, grid=(N_CHIPS, nr), in_specs=[src, src],
            out_specs=pl.BlockSpec((None, tr, wc), lambda s, r, p_ref: (s, r, 0))),
        out_shape=_sds((N_CHIPS, hr, wc), BF16),
        compiler_params=_cparams("parallel", "parallel"),
    )(place, g, ra)


def _chip_add(part, rc, place, name):
    _, hr, wc = rc.shape
    tr = _pick(hr, (256, 352, 128, 64, 32, 16))
    nr = hr // tr

    def body(p_ref, own_ref, r1_ref, r2_ref, r3_ref, o_ref):
        acc = own_ref[...].astype(F32)
        for r_ref in (r1_ref, r2_ref, r3_ref):
            acc = acc + r_ref[...].astype(F32)
        o_ref[...] = acc

    def slot(k):
        return pl.BlockSpec((None, tr, wc), lambda r, p_ref: ((p_ref[1] + k) % N_CHIPS, r, 0))

    return pl.pallas_call(
        body, name=name,
        grid_spec=pltpu.PrefetchScalarGridSpec(
            num_scalar_prefetch=1, grid=(nr,), in_specs=[slot(0), slot(1), slot(2), slot(3)],
            out_specs=pl.BlockSpec((tr, wc), lambda r, p_ref: (p_ref[0] * nr + r, 0))),
        out_shape=_sds((2 * hr, wc), F32),
        compiler_params=_cparams("parallel"),
    )(place, part, rc, rc, rc)


N_DEV = 8


def _all_reduce_small(buf):
    r, cdim = buf.shape

    def body(x_ref, o_ref, land, send_sems, recv_sems):
        x, y, c, _ = _place()
        me = 4 * x + 2 * y + c
        land[me] = x_ref[...]
        sends = []
        for k in range(1, N_DEV):
            kx, ky, kc = (k >> 2) & 1, (k >> 1) & 1, k & 1
            peer = (1 - x if kx else x, 1 - y if ky else y, 1 - c if kc else c)
            cp = _remote(x_ref, land.at[me], send_sems.at[k - 1], recv_sems.at[k - 1], peer)
            cp.start()
            sends.append(cp)
        for k in range(1, N_DEV):
            kx, ky, kc = (k >> 2) & 1, (k >> 1) & 1, k & 1
            peer = (1 - x if kx else x, 1 - y if ky else y, 1 - c if kc else c)
            pidx = 4 * peer[0] + 2 * peer[1] + peer[2]
            _remote(land.at[pidx], land.at[pidx], send_sems.at[k - 1], recv_sems.at[k - 1], peer).wait_recv()
        for cp in sends:
            cp.wait_send()
        acc = land[0]
        for dev in range(1, N_DEV):
            acc = acc + land[dev]
        o_ref[...] = acc

    vm = pl.BlockSpec(memory_space=pltpu.VMEM)
    return pl.pallas_call(
        body, name="all_reduce_small", in_specs=[vm], out_specs=vm, out_shape=_sds((r, cdim), F32),
        scratch_shapes=[pltpu.VMEM((N_DEV, r, cdim), F32), pltpu.SemaphoreType.DMA((N_DEV - 1,)),
                        pltpu.SemaphoreType.DMA((N_DEV - 1,))],
    )(buf)


SMALL_ROWS = 16


def kernel(x, mem, g_mix, w_in, conv_w, attn_sinks, w_attn_proj, w_conv_proj, w_mix_out, g_xattn, g_mem, w_xq, w_xkv, w_xo, g_ffn, w_ffn_in, w_ffn_out, g_final, loss_target, m_g_mix, m_w_in, m_conv_w, m_attn_sinks, m_w_attn_proj, m_w_conv_proj, m_w_mix_out, m_g_xattn, m_g_mem, m_w_xq, m_w_xkv, m_w_xo, m_g_ffn, m_w_ffn_in, m_w_ffn_out, m_g_final, v_g_mix, v_w_in, v_conv_w, v_attn_sinks, v_w_attn_proj, v_w_conv_proj, v_w_mix_out, v_g_xattn, v_g_mem, v_w_xq, v_w_xkv, v_w_xo, v_g_ffn, v_w_ffn_in, v_w_ffn_out, v_g_final):
    w = dict(g_mix=g_mix, w_in=w_in[0], conv_w=conv_w[0], attn_sinks=attn_sinks, w_attn_proj=w_attn_proj[0],
             w_conv_proj=w_conv_proj[0], w_mix_out=w_mix_out[0], g_xattn=g_xattn, g_mem=g_mem, w_xq=w_xq[0],
             w_xkv=w_xkv[0], w_xo=w_xo[0], g_ffn=g_ffn, w_ffn_in=w_ffn_in[0], w_ffn_out=w_ffn_out[0],
             g_final=g_final[None])
    m = dict(g_mix=m_g_mix, w_in=m_w_in[0], conv_w=m_conv_w[0], attn_sinks=m_attn_sinks,
             w_attn_proj=m_w_attn_proj[0], w_conv_proj=m_w_conv_proj[0], w_mix_out=m_w_mix_out[0],
             g_xattn=m_g_xattn, g_mem=m_g_mem, w_xq=m_w_xq[0], w_xkv=m_w_xkv[0], w_xo=m_w_xo[0], g_ffn=m_g_ffn,
             w_ffn_in=m_w_ffn_in[0], w_ffn_out=m_w_ffn_out[0], g_final=m_g_final[None])
    v = dict(g_mix=v_g_mix, w_in=v_w_in[0], conv_w=v_conv_w[0], attn_sinks=v_attn_sinks,
             w_attn_proj=v_w_attn_proj[0], w_conv_proj=v_w_conv_proj[0], w_mix_out=v_w_mix_out[0],
             g_xattn=v_g_xattn, g_mem=v_g_mem, w_xq=v_w_xq[0], w_xkv=v_w_xkv[0], w_xo=v_w_xo[0], g_ffn=v_g_ffn,
             w_ffn_in=v_w_ffn_in[0], w_ffn_out=v_w_ffn_out[0], g_final=v_g_final[None])
    names = [nm for nm, _ in BIG]
    axes = [ax for _, ax in BIG]
    d = x.shape[2]
    cw = w["conv_w"].shape[1] * N_CHIPS
    chip = (2 * lax.axis_index("x") + lax.axis_index("y")).astype(jnp.int32)
    place = jnp.stack([lax.axis_index("c").astype(jnp.int32), chip])
    shard_shapes = [w[nm].shape for nm in names]

    def seed(nm, token):
        me1 = chip.reshape(1)
        if token is not None:
            me1 = me1 + token[0, 0].astype(jnp.int32)
        if nm == "conv_w":
            return _cast_to_full(w[nm], 1, me1, "place_conv_w", F32)
        return _cast_to_full(w[nm], dict(BIG)[nm], me1, "cast_" + nm)

    upd = {}

    def on_ready(nm, shard):
        upd[nm] = _adamw(w[nm], shard, m[nm], v[nm], "adamw_" + nm)
        grads[nm] = upd[nm][3]
        return upd[nm][0]

    grads = {}
    wts = _Schedule(seed, dict(zip(names + ["conv_w"], axes + [1])),
                    dict(zip(names + ["conv_w"], shard_shapes + [w["conv_w"].shape])), place, on_ready)
    aw, cw = w["w_attn_proj"].shape[0], w["w_conv_proj"].shape[0]
    kvw = (w["w_in"].shape[1] * N_CHIPS - aw - 3 * cw - 2 * d) // 2
    grad_x, small = _local_step(
        x[0], mem[0], loss_target[0], w["g_mix"] + wts.token[0:1, 0:1], w["attn_sinks"], w["g_xattn"], w["g_mem"],
        w["g_ffn"], w["g_final"], (aw, cw, kvw), wts)

    pw = max(d, cw)

    def row(a):
        return jnp.pad(a, ((0, 0), (0, pw - a.shape[1])))

    gains = ("g_mix", "g_xattn", "g_mem", "g_ffn", "g_final")
    packed = jnp.concatenate(
        [row(small[nm]) for nm in gains] + [row(small["conv_w"]),
         row(jnp.concatenate([small["attn_sinks"], small["loss"]], axis=1)),
         jnp.zeros((SMALL_ROWS - 9, pw), F32)], axis=0)
    total = _all_reduce_small(packed)
    wts.mark("end", total)
    nsink = attn_sinks.shape[1]
    grads.update({nm: total[k:k + 1, :d] for k, nm in enumerate(gains)})
    grads.update(conv_w=lax.dynamic_slice(total, (5, chip * (cw // N_CHIPS)), (3, cw // N_CHIPS)),
                 attn_sinks=total[8:9, :nsink])
    loss = total[8, nsink]
    for nm in gains + ("conv_w", "attn_sinks"):
        upd[nm] = _adamw(w[nm], grads[nm], m[nm], v[nm], "adamw_" + nm)
    wts.mark("end2", upd["g_final"][0])

    order = ["g_mix", "w_in", "conv_w", "attn_sinks", "w_attn_proj", "w_conv_proj", "w_mix_out", "g_xattn", "g_mem",
             "w_xq", "w_xkv", "w_xo", "g_ffn", "w_ffn_in", "w_ffn_out", "g_final"]

    stacked = set(names) | {"conv_w"}

    def shaped(nm, a):
        if nm == "g_final":
            return a[0]
        return a[None] if nm in stacked else a

    outs = [loss, grad_x[None]]
    outs += [shaped(nm, grads[nm]) for nm in order]
    for k in range(3):
        outs += [shaped(nm, upd[nm][k]) for nm in order]
    return tuple(outs)
```

```python
import functools

import jax
import jax.numpy as jnp
from jax import lax
from jax.experimental import pallas as pl
from jax.experimental.pallas import tpu as pltpu

F32 = jnp.float32
BF16 = jnp.bfloat16

VMEM_LIMIT_BYTES = 56 * 1024 * 1024
LANES = 128
HEAD_DIM = 64
BLOCK = 128
X_HEAD_DIM = 128
ROPE_THETA = 10000.0
EPS = 1e-6
NEG = -1e30
ADAM_LR, ADAM_B1, ADAM_B2, ADAM_EPS, ADAM_WD, ADAM_STEP = 0.001, 0.9, 0.999, 1e-08, 0.01, 10
N_CHIPS = 4
MESH = pl.DeviceIdType.MESH
ANY = pl.BlockSpec(memory_space=pl.ANY)


def _pick(dim, prefs):
    for p in prefs:
        if dim % p == 0:
            return p
    return dim


def _cparams(*sem):
    return pltpu.CompilerParams(dimension_semantics=sem, vmem_limit_bytes=VMEM_LIMIT_BYTES)


def _sds(shape, dtype):
    return jax.ShapeDtypeStruct(shape, dtype)


def _pin(dep):
    return ([], []) if dep is None else ([dep], [ANY])


def _drop_pin(body, n_in, dep):
    if dep is None:
        return body
    return lambda *refs: body(*refs[:n_in], *refs[n_in + 1:])


def _sigmoid(v):
    return 0.5 * jnp.tanh(0.5 * v) + 0.5


MATMUL_VMEM_BUDGET = 46 * 1024 * 1024


def _tiles(mode, m, n):
    if mode == "tn" and m % 1024 != 0:
        return _pick(m, (512, 256, 128)), _pick(n, (1024, 512, 256, 128)), True
    return _pick(m, (1024, 512, 256, 128)), _pick(n, (512, 256, 128)), False


def _k_parts(m, n, k):
    tm, tn = _pick(m, (1024, 512, 256, 128)), _pick(n, (512, 256, 128))
    for parts in range(1, k // LANES + 1):
        if k % (parts * LANES) == 0 and 4 * (tm + tn) * (k // parts) + 16 * tm * tn <= MATMUL_VMEM_BUDGET:
            return parts
    return k // LANES


def _matmul(a, b, *, mode, out_dtype, name, res=None, dep=None, cols=None):
    if mode == "nn":
        (m, k), (k2, n) = a.shape, b.shape
    elif mode == "nt":
        (m, k), (n, k2) = (a.shape[-2], a.shape[-1] * (a.shape[0] if a.ndim == 3 else 1)), b.shape
    else:
        (k, m), (k2, n) = a.shape, (b.shape[-2], b.shape[-1] * (b.shape[0] if b.ndim == 3 else 1))
    assert k == k2, (a.shape, b.shape, mode)
    assert cols is None or mode != "nt"
    cols = cols or (0, n)
    parts = 1 if mode == "tn" else (a.shape[0] if a.ndim == 3 else _k_parts(m, cols[1], k))
    for p in range(parts):
        last = p == parts - 1
        res = _matmul_slice(a, b, mode=mode, out_dtype=out_dtype if last else F32, res=res, dep=dep, cols=cols,
                            kslice=(p, parts), name=name + ("_k%d" % p if parts > 1 else ""))
    return res


def _matmul_slice(a, b, *, mode, out_dtype, name, res, dep, kslice, cols):
    part, parts = kslice
    (m, k) = a.shape[-2:] if mode != "tn" else a.shape[::-1]
    col0, n = cols
    tk = k if a.ndim == 3 else k // parts
    tm, tn, swap = _tiles(mode, m, b.shape[2] if (mode == "tn" and b.ndim == 3) else n)
    while col0 % tn:
        tn //= 2
    joff = col0 // tn
    dims = {"nn": (((1,), (0,)), ((), ())), "nt": (((1,), (1,)), ((), ())), "tn": (((0,), (0,)), ((), ()))}[mode]
    has_res = res is not None
    has_dep = dep is not None

    def body(*refs):
        a_ref, b_ref = refs[0], refs[1]
        o_ref = refs[2 + has_res + has_dep]
        val = lax.dot_general(a_ref[...], b_ref[...], dims, preferred_element_type=F32)
        if has_res:
            val = val + refs[2][...]
        o_ref[...] = val.astype(o_ref.dtype)

    def spec(shape, f):
        if swap:
            return pl.BlockSpec(shape, lambda j, i: f(i, j))
        return pl.BlockSpec(shape, f)

    if mode == "tn":
        a_spec = spec((tk, tm), lambda i, j: (part, i))
    elif a.ndim == 3:
        a_spec = spec((None, tm, tk), lambda i, j: (part, i, 0))
    else:
        a_spec = spec((tm, tk), lambda i, j: (i, part))
    if mode == "nt":
        b_spec = spec((tn, tk), lambda i, j: (j, part))
    elif b.ndim == 3:
        per = b.shape[2] // tn
        b_spec = spec((None, tk, tn), lambda i, j: (j // per, part, j % per))
    else:
        b_spec = spec((tk, tn), lambda i, j: (part, joff + j))
    o_spec = spec((tm, tn), lambda i, j: (i, j))
    return pl.pallas_call(
        body,
        name=name,
        grid=(n // tn, m // tm) if swap else (m // tm, n // tn),
        in_specs=[a_spec, b_spec] + ([o_spec] if has_res else []) + ([ANY] if has_dep else []),
        out_specs=o_spec,
        out_shape=_sds((m, n), out_dtype),
        compiler_params=_cparams("parallel", "parallel"),
    )(*([a, b] + ([res] if has_res else []) + ([dep] if has_dep else [])))


def _rms_fwd(x, g, name):
    t, d = x.shape
    tm = _pick(t, (512, 256, 128))

    def body(x_ref, g_ref, o_ref):
        xf = x_ref[...]
        r = lax.rsqrt(jnp.mean(xf * xf, axis=-1, keepdims=True) + EPS)
        o_ref[...] = (xf * r * g_ref[...]).astype(o_ref.dtype)

    row = pl.BlockSpec((tm, d), lambda i: (i, 0))
    return pl.pallas_call(
        body, name=name, grid=(t // tm,),
        in_specs=[row, pl.BlockSpec((1, d), lambda i: (0, 0))],
        out_specs=row, out_shape=_sds((t, d), BF16),
        compiler_params=_cparams("parallel"),
    )(x, g)


def _rms_bwd_math(xf, g, du):
    r = lax.rsqrt(jnp.mean(xf * xf, axis=-1, keepdims=True) + EPS)
    xh = xf * r
    gdy = g * du
    dx = r * (gdy - xh * jnp.mean(gdy * xh, axis=-1, keepdims=True))
    dg = jnp.sum(du * xh, axis=0, keepdims=True)
    return dx, dg


def _rms_bwd(x, g, du, dh, name):
    t, d = x.shape
    tm = _pick(t, (256, 128))
    has_dh = dh is not None

    def body(*refs):
        x_ref, g_ref, du_ref = refs[0], refs[1], refs[2]
        o_ref, ob_ref, dg_ref = refs[3 + has_dh:]
        dx, dg = _rms_bwd_math(x_ref[...], g_ref[...], du_ref[...].astype(F32))
        if has_dh:
            dx = dx + refs[3][...]
        o_ref[...] = dx
        ob_ref[...] = dx.astype(BF16)

        @pl.when(pl.program_id(0) == 0)
        def _():
            dg_ref[...] = dg

        @pl.when(pl.program_id(0) > 0)
        def _():
            dg_ref[...] += dg

    row = pl.BlockSpec((tm, d), lambda i: (i, 0))
    vec = pl.BlockSpec((1, d), lambda i: (0, 0))
    return pl.pallas_call(
        body, name=name, grid=(t // tm,),
        in_specs=[row, vec, row] + ([row] if has_dh else []),
        out_specs=[row, row, vec],
        out_shape=[_sds((t, d), F32), _sds((t, d), BF16), _sds((1, d), F32)],
        compiler_params=_cparams("arbitrary"),
    )(*([x, g, du] + ([dh] if has_dh else [])))


ROW_TILE = 256


def _rows_matmul(a, b, *, mode, kslice, res, rows_in, vec, epilogue, outs, name, dep):
    part, parts = kslice
    t = a.shape[-2]
    n = b.shape[1] if mode == "nn" else b.shape[0]
    kc = a.shape[-1] if a.ndim == 3 else a.shape[1] // parts
    tm = _pick(t, (ROW_TILE, 128))
    dims = (((1,), (0,)), ((), ())) if mode == "nn" else (((1,), (1,)), ((), ()))
    has_res = res is not None
    deps = [] if dep is None else [dep]
    n_in = 2 + has_res + len(rows_in) + 1 + len(deps)

    def body(*refs):
        val = lax.dot_general(refs[0][...], refs[1][...], dims, preferred_element_type=F32)
        if has_res:
            val = val + refs[2][...]
        row_refs = refs[2 + has_res:2 + has_res + len(rows_in)]
        epilogue(val, row_refs, refs[2 + has_res + len(rows_in)], refs[n_in:], pl.program_id(0) == 0)

    if a.ndim == 3:
        a_spec = pl.BlockSpec((None, tm, kc), lambda i: (part, i, 0))
    else:
        a_spec = pl.BlockSpec((tm, kc), lambda i: (i, part))
    if mode == "nn":
        b_spec = pl.BlockSpec((kc, n), lambda i: (part, 0), pipeline_mode=pl.Buffered(1))
    else:
        b_spec = pl.BlockSpec((n, kc), lambda i: (0, part), pipeline_mode=pl.Buffered(1))
    row = pl.BlockSpec((tm, n), lambda i: (i, 0))
    kinds = {"row": row, "vec": pl.BlockSpec((1, n), lambda i: (0, 0)), "lane": pl.BlockSpec((1, LANES), lambda i: (0, 0))}
    shapes = {"row": (t, n), "vec": (1, n), "lane": (1, LANES)}
    return pl.pallas_call(
        body, name=name, grid=(t // tm,),
        in_specs=[a_spec, b_spec] + [row] * (has_res + len(rows_in)) + [kinds["vec"]] + [ANY] * len(deps),
        out_specs=[kinds[o[0]] for o in outs],
        out_shape=[_sds(shapes[o[0]], o[1] if len(o) > 1 else F32) for o in outs],
        compiler_params=_cparams("arbitrary"),
    )(*([a, b] + ([res] if has_res else []) + list(rows_in) + [vec] + deps))


def _accumulate(ref, value, first):
    @pl.when(first)
    def _():
        ref[...] = value

    @pl.when(jnp.logical_not(first))
    def _():
        ref[...] += value


def _ep_norm(val, rows, g_ref, outs, first):
    h_ref, u_ref = outs
    h_ref[...] = val
    r = lax.rsqrt(jnp.mean(val * val, axis=-1, keepdims=True) + EPS)
    u_ref[...] = (val * r * g_ref[...]).astype(u_ref.dtype)


def _ep_loss(val, rows, g_ref, outs, first):
    o_ref, ob_ref, dg_ref, l_ref = outs
    gv = g_ref[...]
    r = lax.rsqrt(jnp.mean(val * val, axis=-1, keepdims=True) + EPS)
    err = val * r * gv - rows[0][...]
    part = 0.5 * jnp.sum(jnp.mean(err * err, axis=-1, keepdims=True), axis=0, keepdims=True)
    dx, dg = _rms_bwd_math(val, gv, err * (1.0 / val.shape[1]))
    o_ref[...] = dx
    ob_ref[...] = dx.astype(BF16)
    _accumulate(dg_ref, dg, first)
    _accumulate(l_ref, jnp.broadcast_to(part, (1, LANES)), first)


def _ep_rms_bwd(val, rows, g_ref, outs, first):
    o_ref, ob_ref, dg_ref = outs
    dx, dg = _rms_bwd_math(rows[0][...], g_ref[...], val)
    dx = dx + rows[1][...]
    o_ref[...] = dx
    ob_ref[...] = dx.astype(BF16)
    _accumulate(dg_ref, dg, first)


def _rope_tables(t):
    half = HEAD_DIM // 2
    inv_freq = ROPE_THETA ** (-jnp.arange(half, dtype=F32) / half)
    ang = jnp.arange(t, dtype=F32)[:, None] * inv_freq[None, :]
    cos = jnp.cos(ang)
    sin = jnp.sin(ang)
    reps = LANES // HEAD_DIM
    cos_t = jnp.tile(jnp.concatenate([cos, cos], axis=1), (1, reps))
    sin_t = jnp.tile(jnp.concatenate([-sin, sin], axis=1), (1, reps))
    return cos_t, sin_t


def _rope(v, cos, sin):
    w = v.shape[1]
    c = jnp.tile(cos, (1, w // LANES))
    s = jnp.tile(sin, (1, w // LANES))
    lane = lax.broadcasted_iota(jnp.int32, v.shape, 1)
    first = (lane % HEAD_DIM) < (HEAD_DIM // 2)
    partner = jnp.where(first, pltpu.roll(v, w - HEAD_DIM // 2, 1), pltpu.roll(v, HEAD_DIM // 2, 1))
    return v * c + partner * s


def _heads(v, count):
    return jnp.concatenate([v[:, i * HEAD_DIM:(i + 1) * HEAD_DIM] for i in range(count)], axis=0)


def _unheads(v, count):
    r = v.shape[0] // count
    return jnp.concatenate([v[g * r:(g + 1) * r] for g in range(count)], axis=1)


def _rope_qkv(proj, cos, sin, aw, kvw):
    t = proj.shape[0]
    nkv = kvw // HEAD_DIM
    koff = aw // kvw
    tm = _pick(t, (256, 128))

    def body(q_ref, k_ref, v_ref, c_ref, s_ref, qo_ref, ko_ref, vo_ref):
        c, s = c_ref[...], s_ref[...]
        qo_ref[...] = _rope(q_ref[...].astype(F32), c, s).astype(BF16)
        k = _rope(k_ref[...].astype(F32), c, s).astype(BF16)
        v = v_ref[...].astype(BF16)
        for h in range(nkv):
            ko_ref[h] = k[:, h * HEAD_DIM:(h + 1) * HEAD_DIM]
            vo_ref[h] = v[:, h * HEAD_DIM:(h + 1) * HEAD_DIM]

    def row(w, j):
        return pl.BlockSpec((tm, w), lambda i: (i, j))

    hm = pl.BlockSpec((nkv, tm, HEAD_DIM), lambda i: (0, i, 0))
    return pl.pallas_call(
        body, name="rope_qkv", grid=(t // tm,),
        in_specs=[row(aw, 0), row(kvw, koff), row(kvw, koff + 1), row(LANES, 0), row(LANES, 0)],
        out_specs=[row(aw, 0), hm, hm],
        out_shape=[_sds((t, aw), BF16), _sds((nkv, t, HEAD_DIM), BF16), _sds((nkv, t, HEAD_DIM), BF16)],
        compiler_params=_cparams("parallel"),
    )(proj, proj, proj, cos, sin)


def _attn_probs(qs, kb, n, h, qpk, sinks_ref):
    s = lax.dot_general(qs, kb, (((1,), (1,)), ((), ())), preferred_element_type=F32) * (HEAD_DIM ** -0.5)
    qi = lax.broadcasted_iota(jnp.int32, (BLOCK, 2 * BLOCK), 0)
    kc = lax.broadcasted_iota(jnp.int32, (BLOCK, 2 * BLOCK), 1)
    valid = (kc > qi) & (kc <= qi + BLOCK) & ((kc >= BLOCK) | (n > 0))
    bias = jnp.tile(jnp.where(valid, 0.0, NEG).astype(F32), (qpk, 1))
    s = s + bias
    rowg = lax.broadcasted_iota(jnp.int32, (qpk * BLOCK, 1), 0) // BLOCK
    sink = jnp.zeros((qpk * BLOCK, 1), F32)
    for g in range(qpk):
        sink = jnp.where(rowg == g, sinks_ref[0, h * qpk + g], sink)
    m = jnp.maximum(jnp.max(s, axis=-1, keepdims=True), sink)
    e = jnp.exp(s - m)
    es = jnp.exp(sink - m)
    inv = 1.0 / (jnp.sum(e, axis=-1, keepdims=True) + es)
    return e * inv, es * inv, rowg


HEADS_PER_STEP = 4


def _attn_specs(qw, hp):
    def head(f):
        return pl.BlockSpec((hp, BLOCK, HEAD_DIM), lambda n, h: (h, f(n), 0))

    cur = lambda n: n
    prev = lambda n: jnp.maximum(n - 1, 0)
    return [pl.BlockSpec((BLOCK, hp * qw), lambda n, h: (n, h)), head(cur), head(prev), head(cur), head(prev),
            pl.BlockSpec(memory_space=pltpu.SMEM)]


def _attn_fwd(q_r, k_r, v_h, sinks):
    t, aw = q_r.shape
    nkv = k_r.shape[0]
    qpk = aw // (nkv * HEAD_DIM)
    qw = qpk * HEAD_DIM
    hp = HEADS_PER_STEP if nkv % HEADS_PER_STEP == 0 else 1

    def body(q_ref, kc_ref, kp_ref, vc_ref, vp_ref, sinks_ref, o_ref):
        n, hg = pl.program_id(0), pl.program_id(1)
        outs = []
        for j in range(hp):
            kb = jnp.concatenate([kp_ref[j], kc_ref[j]], axis=0)
            vb = jnp.concatenate([vp_ref[j], vc_ref[j]], axis=0)
            qs = _heads(q_ref[:, j * qw:(j + 1) * qw], qpk)
            p, _, _ = _attn_probs(qs, kb, n, hg * hp + j, qpk, sinks_ref)
            outs.append(_unheads(jnp.dot(p.astype(BF16), vb, preferred_element_type=F32), qpk))
        o_ref[...] = jnp.concatenate(outs, axis=1).astype(o_ref.dtype)

    return pl.pallas_call(
        body, name="attn_fwd", grid=(t // BLOCK, nkv // hp),
        in_specs=_attn_specs(qw, hp),
        out_specs=pl.BlockSpec((BLOCK, hp * qw), lambda n, h: (n, h)),
        out_shape=_sds((t, aw), BF16),
        compiler_params=_cparams("parallel", "parallel"),
    )(q_r, k_r, k_r, v_h, v_h, sinks)


def _attn_bwd(q_r, k_r, v_h, sinks, o, do, dep=None):
    t, aw = q_r.shape
    nkv = k_r.shape[0]
    qpk = aw // (nkv * HEAD_DIM)
    qw = qpk * HEAD_DIM
    hp = HEADS_PER_STEP if nkv % HEADS_PER_STEP == 0 else 1
    scale = HEAD_DIM ** -0.5

    def body(q_ref, kc_ref, kp_ref, vc_ref, vp_ref, sinks_ref, o_ref, do_ref,
             dq_ref, dkc_ref, dkp_ref, dvc_ref, dvp_ref, ds_ref):
        n, hg = pl.program_id(0), pl.program_id(1)
        lane = lax.broadcasted_iota(jnp.int32, (8, LANES), 1)
        row0 = lax.broadcasted_iota(jnp.int32, (8, LANES), 0) == 0
        dsink = jnp.zeros((8, LANES), F32)
        dqs = []
        for j in range(hp):
            h = hg * hp + j
            cols = slice(j * qw, (j + 1) * qw)
            kb = jnp.concatenate([kp_ref[j], kc_ref[j]], axis=0)
            vb = jnp.concatenate([vp_ref[j], vc_ref[j]], axis=0)
            qs = _heads(q_ref[:, cols], qpk)
            dos = _heads(do_ref[:, cols], qpk)
            p, psink, rowg = _attn_probs(qs, kb, n, h, qpk, sinks_ref)
            pb = p.astype(BF16)
            delta = jnp.sum(dos.astype(F32) * _heads(o_ref[:, cols], qpk).astype(F32), axis=-1, keepdims=True)
            dv = lax.dot_general(pb, dos, (((0,), (0,)), ((), ())), preferred_element_type=F32)
            dp = lax.dot_general(dos, vb, (((1,), (1,)), ((), ())), preferred_element_type=F32)
            dsc = (p * (dp - delta)).astype(BF16)
            dqs.append(_unheads(jnp.dot(dsc, kb, preferred_element_type=F32) * scale, qpk))
            dk = lax.dot_general(dsc, qs, (((0,), (0,)), ((), ())), preferred_element_type=F32) * scale
            dkp_ref[j] = dk[:BLOCK]
            dkc_ref[j] = dk[BLOCK:]
            dvp_ref[j] = dv[:BLOCK]
            dvc_ref[j] = dv[BLOCK:]
            sink_term = psink * delta
            for g in range(qpk):
                val = -jnp.sum(jnp.where(rowg == g, sink_term, 0.0))
                dsink = jnp.where(row0 & (lane == h * qpk + g), val, dsink)
        dq_ref[...] = jnp.concatenate(dqs, axis=1).astype(dq_ref.dtype)
        first = (n == 0) & (hg == 0)

        @pl.when(first)
        def _():
            ds_ref[...] = dsink

        @pl.when(jnp.logical_not(first))
        def _():
            ds_ref[...] += dsink

    qblk = pl.BlockSpec((BLOCK, hp * qw), lambda n, h: (n, h))
    kvblk = pl.BlockSpec((hp, BLOCK, HEAD_DIM), lambda n, h: (h, n, 0))
    pins, pin_specs = _pin(dep)
    return pl.pallas_call(
        _drop_pin(body, 8, dep), name="attn_bwd", grid=(t // BLOCK, nkv // hp),
        in_specs=_attn_specs(qw, hp) + [qblk, qblk] + pin_specs,
        out_specs=[qblk, kvblk, kvblk, kvblk, kvblk, pl.BlockSpec((8, LANES), lambda n, h: (0, 0))],
        out_shape=[_sds((t, aw), BF16)] + [_sds((nkv, t, HEAD_DIM), F32)] * 4 + [_sds((8, LANES), F32)],
        compiler_params=_cparams("arbitrary", "arbitrary"),
    )(q_r, k_r, k_r, v_h, v_h, sinks, o, do, *pins)


HALO = 16


def _shift_down(v, k, halo):
    rows = lax.broadcasted_iota(jnp.int32, v.shape, 0)
    out = pltpu.roll(v, k, 0)
    for r in range(k):
        out = jnp.where(rows == r, halo[HALO - k + r:HALO - k + r + 1, :], out)
    return out


def _shift_up(v, k, halo):
    tm = v.shape[0]
    rows = lax.broadcasted_iota(jnp.int32, v.shape, 0)
    out = pltpu.roll(v, tm - k, 0)
    for r in range(k):
        out = jnp.where(rows == tm - k + r, halo[r:r + 1, :], out)
    return out


def _conv_fwd(proj, conv_w, zoff, cw, cb):
    t = proj.shape[0]
    tm = _pick(t, (512, 256, 128))
    zb, nb = zoff // cb, cw // cb
    hb = tm // HALO

    def body(z_ref, gb_ref, gc_ref, zp_ref, gcp_ref, w_ref, o_ref):
        i = pl.program_id(0)
        cz = gc_ref[...].astype(F32) * z_ref[...].astype(F32)
        czp = gcp_ref[...].astype(F32) * zp_ref[...].astype(F32) * (i > 0).astype(F32)
        w = w_ref[...]
        y = w[0:1] * _shift_down(cz, 2, czp) + w[1:2] * _shift_down(cz, 1, czp) + w[2:3] * cz
        o_ref[...] = (gb_ref[...].astype(F32) * y).astype(o_ref.dtype)

    def col(k):
        return pl.BlockSpec((tm, cb), lambda i, j: (i, zb + k * nb + j))

    def halo(k):
        return pl.BlockSpec((HALO, cb), lambda i, j: (jnp.maximum(i * hb - 1, 0), zb + k * nb + j))

    return pl.pallas_call(
        body, name="conv_fwd", grid=(t // tm, nb),
        in_specs=[col(0), col(1), col(2), halo(0), halo(2), pl.BlockSpec((3, cb), lambda i, j: (0, j))],
        out_specs=pl.BlockSpec((tm, cb), lambda i, j: (i, j)),
        out_shape=_sds((t, cw), BF16),
        compiler_params=_cparams("parallel", "parallel"),
    )(proj, proj, proj, proj, proj, conv_w)


def _conv_bwd(proj, conv_w, dco, zoff, cw, cb, dep=None):
    t = proj.shape[0]
    tm = _pick(t, (512, 256, 128))
    zb, nb = zoff // cb, cw // cb
    hb = tm // HALO
    nt = t // tm

    def body(z_ref, gb_ref, gc_ref, zp_ref, gcp_ref, gbn_ref, w_ref, d_ref, dn_ref, dz_ref, dgb_ref, dgc_ref, dw_ref):
        i = pl.program_id(1)
        z, gb, gc = z_ref[...].astype(F32), gb_ref[...].astype(F32), gc_ref[...].astype(F32)
        d = d_ref[...].astype(F32)
        cz = gc * z
        czp = gcp_ref[...].astype(F32) * zp_ref[...].astype(F32) * (i > 0).astype(F32)
        w = w_ref[...]
        cz1 = _shift_down(cz, 1, czp)
        cz2 = _shift_down(cz, 2, czp)
        y = w[0:1] * cz2 + w[1:2] * cz1 + w[2:3] * cz
        dgb_ref[...] = (d * y).astype(dgb_ref.dtype)
        dy = d * gb
        dyn = dn_ref[...].astype(F32) * gbn_ref[...].astype(F32) * (i < nt - 1).astype(F32)
        dcz = w[2:3] * dy + w[1:2] * _shift_up(dy, 1, dyn) + w[0:1] * _shift_up(dy, 2, dyn)
        dgc_ref[...] = (dcz * z).astype(dgc_ref.dtype)
        dz_ref[...] = (dcz * gc).astype(dz_ref.dtype)
        rows = lax.broadcasted_iota(jnp.int32, (8, cb), 0)
        dw = jnp.zeros((8, cb), F32)
        for r, tap in enumerate((cz2, cz1, cz)):
            dw = jnp.where(rows == r, jnp.sum(dy * tap, axis=0, keepdims=True), dw)

        @pl.when(i == 0)
        def _():
            dw_ref[...] = dw

        @pl.when(i > 0)
        def _():
            dw_ref[...] += dw

    def col(k):
        return pl.BlockSpec((tm, cb), lambda j, i: (i, zb + k * nb + j))

    def halo_prev(k):
        return pl.BlockSpec((HALO, cb), lambda j, i: (jnp.maximum(i * hb - 1, 0), zb + k * nb + j))

    own = pl.BlockSpec((tm, cb), lambda j, i: (i, j))
    nxt = lambda i: jnp.minimum((i + 1) * hb, t // HALO - 1)
    pins, pin_specs = _pin(dep)
    return pl.pallas_call(
        _drop_pin(body, 9, dep), name="conv_bwd", grid=(nb, nt),
        in_specs=[col(0), col(1), col(2), halo_prev(0), halo_prev(2),
                  pl.BlockSpec((HALO, cb), lambda j, i: (nxt(i), zb + nb + j)),
                  pl.BlockSpec((3, cb), lambda j, i: (0, j)), own,
                  pl.BlockSpec((HALO, cb), lambda j, i: (nxt(i), j))] + pin_specs,
        out_specs=[own, own, own, pl.BlockSpec((8, cb), lambda j, i: (0, j))],
        out_shape=[_sds((t, cw), BF16)] * 3 + [_sds((8, cw), F32)],
        compiler_params=_cparams("parallel", "arbitrary"),
    )(proj, proj, proj, proj, proj, proj, conv_w, dco, dco, *pins)


def _yconv_merge(conv_o, w_cp, ya, proj, goff, cb, dep, name):
    t, k = conv_o.shape
    d = w_cp.shape[1]
    tm = _pick(t, (1024, 512, 256, 128))
    gb_, nb = goff // cb, d // cb
    deps = [] if dep is None else [dep]

    def body(a_ref, w_ref, ya_ref, ga_ref, gc_ref, *rest):
        yc_ref, o_ref = rest[len(deps):]
        yc = jnp.dot(a_ref[...], w_ref[...], preferred_element_type=F32)
        f = lambda r: r[...].astype(F32)
        yc_ref[...] = yc.astype(yc_ref.dtype)
        o_ref[...] = (_sigmoid(f(ga_ref)) * f(ya_ref) + _sigmoid(f(gc_ref)) * yc).astype(o_ref.dtype)

    own = pl.BlockSpec((tm, cb), lambda i, j: (i, j))
    return pl.pallas_call(
        body, name=name, grid=(t // tm, nb),
        in_specs=[pl.BlockSpec((tm, k), lambda i, j: (i, 0)), pl.BlockSpec((k, cb), lambda i, j: (0, j)), own,
                  pl.BlockSpec((tm, cb), lambda i, j: (i, gb_ + j)),
                  pl.BlockSpec((tm, cb), lambda i, j: (i, gb_ + nb + j))] + [ANY] * len(deps),
        out_specs=[own, own], out_shape=[_sds((t, d), BF16)] * 2,
        compiler_params=_cparams("parallel", "parallel"),
    )(conv_o, w_cp, ya, proj, proj, *deps)


def _dmerged_split(dh, w_mo, proj, ya, yc, goff, cb, dep, name):
    t, k = dh.shape
    d = w_mo.shape[0]
    tm = _pick(t, (1024, 512, 256, 128))
    gb_, nb = goff // cb, d // cb
    deps = [] if dep is None else [dep]

    def body(a_ref, w_ref, ga_ref, gc_ref, ya_ref, yc_ref, *rest):
        dya_ref, dyc_ref, dga_ref, dgc_ref = rest[len(deps):]
        dmv = lax.dot_general(a_ref[...], w_ref[...], (((1,), (1,)), ((), ())), preferred_element_type=F32)
        sa = _sigmoid(ga_ref[...].astype(F32))
        sc = _sigmoid(gc_ref[...].astype(F32))
        dya_ref[...] = (dmv * sa).astype(BF16)
        dyc_ref[...] = (dmv * sc).astype(BF16)
        dga_ref[...] = (dmv * ya_ref[...].astype(F32) * sa * (1.0 - sa)).astype(BF16)
        dgc_ref[...] = (dmv * yc_ref[...].astype(F32) * sc * (1.0 - sc)).astype(BF16)

    own = pl.BlockSpec((tm, cb), lambda i, j: (i, j))
    return pl.pallas_call(
        body, name=name, grid=(t // tm, nb),
        in_specs=[pl.BlockSpec((tm, k), lambda i, j: (i, 0)), pl.BlockSpec((cb, k), lambda i, j: (j, 0)),
                  pl.BlockSpec((tm, cb), lambda i, j: (i, gb_ + j)),
                  pl.BlockSpec((tm, cb), lambda i, j: (i, gb_ + nb + j)), own, own] + [ANY] * len(deps),
        out_specs=[own] * 4, out_shape=[_sds((t, d), BF16)] * 4,
        compiler_params=_cparams("parallel", "parallel"),
    )(dh, w_mo, proj, proj, ya, yc, *deps)


def _assemble_dproj(dq, dkc, dkp, dvc, dvp, cos, sin, dz, dgb, dgc, dga, dgg, dep=None):
    t, aw = dq.shape
    nkv, cw, d = dkc.shape[0], dz.shape[1], dga.shape[1]
    kvw = nkv * HEAD_DIM
    nblk = t // BLOCK
    width = aw + 2 * kvw + 3 * cw + 2 * d

    def body(dq_ref, dkc_ref, dkp_ref, dvc_ref, dvp_ref, c_ref, s_ref, dz_ref, dgb_ref, dgc_ref, dga_ref, dgg_ref,
             o_ref):
        keep = (pl.program_id(0) < nblk - 1).astype(F32)
        c, s = c_ref[...], s_ref[...]
        dk = jnp.concatenate([dkc_ref[h] + dkp_ref[h] * keep for h in range(nkv)], axis=1)
        dv = jnp.concatenate([dvc_ref[h] + dvp_ref[h] * keep for h in range(nkv)], axis=1)
        o_ref[...] = jnp.concatenate(
            [_rope(dq_ref[...].astype(F32), c, -s).astype(BF16), _rope(dk, c, -s).astype(BF16), dv.astype(BF16),
             dz_ref[...], dgb_ref[...], dgc_ref[...], dga_ref[...], dgg_ref[...]], axis=1)

    def cur(w):
        return pl.BlockSpec((BLOCK, w), lambda n: (n, 0))

    head_cur = pl.BlockSpec((nkv, BLOCK, HEAD_DIM), lambda n: (0, n, 0))
    head_nxt = pl.BlockSpec((nkv, BLOCK, HEAD_DIM), lambda n: (0, jnp.minimum(n + 1, nblk - 1), 0))
    pins, pin_specs = _pin(dep)
    return pl.pallas_call(
        _drop_pin(body, 12, dep), name="assemble_dproj", grid=(nblk,),
        in_specs=[cur(aw), head_cur, head_nxt, head_cur, head_nxt, cur(LANES), cur(LANES),
                  cur(cw), cur(cw), cur(cw), cur(d), cur(d)] + pin_specs,
        out_specs=cur(width), out_shape=_sds((t, width), BF16),
        compiler_params=_cparams("parallel"),
    )(dq, dkc, dkp, dvc, dvp, cos, sin, dz, dgb, dgc, dga, dgg, *pins)


def _xattn_probs(qh, kh):
    s = lax.dot_general(qh, kh, (((1,), (1,)), ((), ())), preferred_element_type=F32) * (X_HEAD_DIM ** -0.5)
    e = jnp.exp(s - jnp.max(s, axis=-1, keepdims=True))
    return e * (1.0 / jnp.sum(e, axis=-1, keepdims=True))


def _xattn_fwd(xq, kv):
    t, xw = xq.shape
    mt = kv.shape[0]
    tm = _pick(t, (512, 256, 128))

    def body(q_ref, kv_ref, o_ref):
        outs = []
        for hd in range(xw // X_HEAD_DIM):
            hs = slice(hd * X_HEAD_DIM, (hd + 1) * X_HEAD_DIM)
            vs = slice(xw + hd * X_HEAD_DIM, xw + (hd + 1) * X_HEAD_DIM)
            p = _xattn_probs(q_ref[:, hs], kv_ref[:, hs])
            outs.append(jnp.dot(p.astype(BF16), kv_ref[:, vs], preferred_element_type=F32))
        o_ref[...] = jnp.concatenate(outs, axis=1).astype(o_ref.dtype)

    return pl.pallas_call(
        body, name="xattn_fwd", grid=(t // tm,),
        in_specs=[pl.BlockSpec((tm, xw), lambda i: (i, 0)), pl.BlockSpec((mt, 2 * xw), lambda i: (0, 0))],
        out_specs=pl.BlockSpec((tm, xw), lambda i: (i, 0)), out_shape=_sds((t, xw), BF16),
        compiler_params=_cparams("parallel"),
    )(xq, kv)


def _xattn_bwd(xq, kv, do):
    t, xw = xq.shape
    mt = kv.shape[0]
    tm = _pick(t, (512, 256, 128))
    scale = X_HEAD_DIM ** -0.5

    def body(q_ref, kv_ref, do_ref, dq_ref, dkv_ref):
        dqs, dks, dvs = [], [], []
        for hd in range(xw // X_HEAD_DIM):
            hs = slice(hd * X_HEAD_DIM, (hd + 1) * X_HEAD_DIM)
            vs = slice(xw + hd * X_HEAD_DIM, xw + (hd + 1) * X_HEAD_DIM)
            qh, kh, vh, doh = q_ref[:, hs], kv_ref[:, hs], kv_ref[:, vs], do_ref[:, hs]
            p = _xattn_probs(qh, kh)
            pb = p.astype(BF16)
            o = jnp.dot(pb, vh, preferred_element_type=F32)
            delta = jnp.sum(doh.astype(F32) * o, axis=-1, keepdims=True)
            dvs.append(lax.dot_general(pb, doh, (((0,), (0,)), ((), ())), preferred_element_type=F32))
            dp = lax.dot_general(doh, vh, (((1,), (1,)), ((), ())), preferred_element_type=F32)
            dsc = (p * (dp - delta)).astype(BF16)
            dqs.append(jnp.dot(dsc, kh, preferred_element_type=F32) * scale)
            dks.append(lax.dot_general(dsc, qh, (((0,), (0,)), ((), ())), preferred_element_type=F32) * scale)
        dq_ref[...] = jnp.concatenate(dqs, axis=1).astype(dq_ref.dtype)
        dkv = jnp.concatenate(dks + dvs, axis=1)

        @pl.when(pl.program_id(0) == 0)
        def _():
            dkv_ref[...] = dkv

        @pl.when(pl.program_id(0) > 0)
        def _():
            dkv_ref[...] += dkv

    row = pl.BlockSpec((tm, xw), lambda i: (i, 0))
    whole = pl.BlockSpec((mt, 2 * xw), lambda i: (0, 0))
    return pl.pallas_call(
        body, name="xattn_bwd", grid=(t // tm,),
        in_specs=[row, whole, row], out_specs=[row, whole],
        out_shape=[_sds((t, xw), BF16), _sds((mt, 2 * xw), F32)],
        compiler_params=_cparams("arbitrary"),
    )(xq, kv, do)


def _ffn_in_swiglu(u, w, dep, name):
    t, d = u.shape
    f = w.shape[1] // 2
    tm, tn, _ = _tiles("nn", t, f)
    nf = f // tn
    deps = [] if dep is None else [dep]

    def body(u_ref, wa_ref, wb_ref, *rest):
        h_ref, o_ref = rest[len(deps):]
        a = jnp.dot(u_ref[...], wa_ref[...], preferred_element_type=F32)
        b = jnp.dot(u_ref[...], wb_ref[...], preferred_element_type=F32)
        h_ref[0] = a.astype(h_ref.dtype)
        h_ref[1] = b.astype(h_ref.dtype)
        o_ref[...] = (a * _sigmoid(a) * b).astype(o_ref.dtype)

    return pl.pallas_call(
        body, name=name, grid=(t // tm, nf),
        in_specs=[pl.BlockSpec((tm, d), lambda i, j: (i, 0)), pl.BlockSpec((d, tn), lambda i, j: (0, j)),
                  pl.BlockSpec((d, tn), lambda i, j: (0, nf + j))] + [ANY] * len(deps),
        out_specs=[pl.BlockSpec((2, tm, tn), lambda i, j: (0, i, j)), pl.BlockSpec((tm, tn), lambda i, j: (i, j))],
        out_shape=[_sds((2, t, f), BF16), _sds((t, f), BF16)],
        compiler_params=_cparams("parallel", "parallel"),
    )(u, w, w, *deps)


def _dact_swiglu(dh, w_out, hid, dep, name):
    t, d = dh.shape
    f = w_out.shape[0]
    tm, tn, _ = _tiles("nt", t, f)
    deps = [] if dep is None else [dep]

    def body(dh_ref, w_ref, h_ref, *rest):
        o_ref = rest[len(deps)]
        g = lax.dot_general(dh_ref[...], w_ref[...], (((1,), (1,)), ((), ())), preferred_element_type=F32)
        a = h_ref[0].astype(F32)
        b = h_ref[1].astype(F32)
        sg = _sigmoid(a)
        o_ref[0] = (g * b * sg * (1.0 + a * (1.0 - sg))).astype(o_ref.dtype)
        o_ref[1] = (g * a * sg).astype(o_ref.dtype)

    pair = pl.BlockSpec((2, tm, tn), lambda i, j: (0, i, j))
    return pl.pallas_call(
        body, name=name, grid=(t // tm, f // tn),
        in_specs=[pl.BlockSpec((tm, d), lambda i, j: (i, 0)), pl.BlockSpec((tn, d), lambda i, j: (j, 0)), pair]
        + [ANY] * len(deps),
        out_specs=pair, out_shape=_sds((2, t, f), BF16),
        compiler_params=_cparams("parallel", "parallel"),
    )(dh, w_out, hid, *deps)


def _adamw(w, g, m, v, name):
    r, c = w.shape
    tr = _pick(r, (256, 128, 64, 32, 16, 8)) if r * c > 65536 else r

    def body(w_ref, g_ref, m_ref, v_ref, d_ref, nm_ref, nv_ref, go_ref):
        gv = g_ref[...]
        go_ref[...] = gv
        m2 = ADAM_B1 * m_ref[...] + (1.0 - ADAM_B1) * gv
        v2 = ADAM_B2 * v_ref[...] + (1.0 - ADAM_B2) * (gv * gv)
        m_hat = m2 / (1.0 - ADAM_B1 ** ADAM_STEP)
        v_hat = v2 / (1.0 - ADAM_B2 ** ADAM_STEP)
        d_ref[...] = -ADAM_LR * (m_hat / (jnp.sqrt(v_hat) + ADAM_EPS) + ADAM_WD * w_ref[...])
        nm_ref[...] = m2
        nv_ref[...] = v2

    blk = pl.BlockSpec((tr, c), lambda i: (i, 0))
    return pl.pallas_call(
        body, name=name, grid=(r // tr,),
        in_specs=[blk] * 4, out_specs=[blk] * 4, out_shape=[_sds((r, c), F32)] * 4,
        compiler_params=_cparams("parallel"),
    )(w, g, m, v)


class _Weights:
    def __init__(self, full):
        self.full = full
        self.grads = {}

    def get(self, name):
        return self.full[name]

    def mark(self, tag, value):
        return value

    def grad(self, name, g):
        self.grads[name] = g

    def dep(self):
        return None


def _local_step(x, mem, tgt, g_mix, sinks, g_xattn, g_mem, g_ffn, g_final, dims, wts):
    t, d = x.shape
    aw, cw, kvw = dims
    cb = 2 * kvw
    zoff = aw + 2 * kvw
    goff = zoff + 3 * cw
    cos, sin = _rope_tables(t)
    mark, get = wts.mark, wts.get

    def mm(a, b, **kw):
        return _matmul(a, b, dep=wts.dep(), **kw)

    u1 = mark("u1", _rms_fwd(x, g_mix, "rms_mix"))
    qkv = mm(u1, get("w_in"), mode="nn", out_dtype=F32, name="mm_qkv", cols=(0, zoff))
    proj = mark("proj", mm(u1, get("w_in"), mode="nn", out_dtype=BF16, name="mm_gates", cols=(zoff, goff - zoff + 2 * d)))
    zoff, goff = 0, goff - zoff
    q_r, k_r, v_h = _rope_qkv(qkv, cos, sin, aw, kvw)
    attn_o = mark("attn_o", _attn_fwd(q_r, k_r, v_h, sinks))
    conv_o = mark("conv_o", _conv_fwd(proj, get("conv_w"), zoff, cw, cb))
    ya = mark("ya", mm(attn_o, get("w_attn_proj"), mode="nn", out_dtype=BF16, name="mm_yattn"))
    yc, merged = _yconv_merge(conv_o, get("w_conv_proj"), ya, proj, goff, cb, wts.dep(), "mm_yconv")
    mark("merged", merged)
    norm_outs = [("row", F32), ("row", BF16)]
    bwd_outs = [("row", F32), ("row", BF16), ("vec",)]
    h1, u2 = _rows_matmul(merged, get("w_mix_out"), mode="nn", kslice=(0, 1), res=x, rows_in=[], vec=g_xattn,
                          epilogue=_ep_norm, outs=norm_outs, name="mm_mix", dep=wts.dep())
    mark("h1", h1)
    mem_n = _rms_fwd(mem, g_mem, "rms_mem")
    xq = mark("xq", mm(u2, get("w_xq"), mode="nn", out_dtype=BF16, name="mm_xq"))
    kv = mm(mem_n, get("w_xkv"), mode="nn", out_dtype=BF16, name="mm_xkv")
    xo = mark("xo", _xattn_fwd(xq, kv))
    h2, u3 = _rows_matmul(xo, get("w_xo"), mode="nn", kslice=(0, 1), res=h1, rows_in=[], vec=g_ffn,
                          epilogue=_ep_norm, outs=norm_outs, name="mm_xo", dep=wts.dep())
    mark("h2", h2)
    mark("u3", u3)
    hid, act = _ffn_in_swiglu(u3, get("w_ffn_in"), wts.dep(), "mm_ffn_in")
    mark("hid", hid)
    dh3, dh3b, dg_final, loss = _rows_matmul(
        act, get("w_ffn_out"), mode="nn", kslice=(0, 1), res=h2, rows_in=[tgt], vec=g_final, epilogue=_ep_loss,
        outs=bwd_outs + [("lane",)], name="mm_ffn_out", dep=wts.dep())
    mark("dh3", dh3b)

    wts.grad("w_ffn_out", mm(act, dh3b, mode="tn", out_dtype=BF16, name="mm_dw_ffn_out"))
    dhid = mark("dhid", _dact_swiglu(dh3b, get("w_ffn_out"), hid, wts.dep(), "mm_dact"))
    wts.grad("w_ffn_in", mm(u3, dhid, mode="tn", out_dtype=BF16, name="mm_dw_ffn_in"))
    du3 = mark("du3", _matmul_slice(dhid, get("w_ffn_in"), mode="nt", out_dtype=F32, name="mm_du3_k0", res=None,
                                    dep=wts.dep(), kslice=(0, 2), cols=(0, d)))
    dh2, dh2b, dg_ffn = _rows_matmul(dhid, get("w_ffn_in"), mode="nt", kslice=(1, 2), res=du3, rows_in=[h2, dh3],
                                     vec=g_ffn, epilogue=_ep_rms_bwd, outs=bwd_outs, name="mm_du3_k1", dep=wts.dep())
    mark("dh2", dh2b)
    wts.grad("w_xo", mm(xo, dh2b, mode="tn", out_dtype=BF16, name="mm_dw_xo"))
    dxo = mm(dh2b, get("w_xo"), mode="nt", out_dtype=BF16, name="mm_dxo")
    dxq, dkv = _xattn_bwd(xq, kv, dxo)
    dkvb = dkv.astype(BF16)
    wts.grad("w_xq", mm(u2, dxq, mode="tn", out_dtype=BF16, name="mm_dw_xq"))
    wts.grad("w_xkv", mm(mem_n, dkvb, mode="tn", out_dtype=BF16, name="mm_dw_xkv"))
    dmem_n = mm(dkvb, get("w_xkv"), mode="nt", out_dtype=F32, name="mm_dmem")
    _, _, dg_mem = _rms_bwd(mem, g_mem, dmem_n, None, "rms_bwd_mem")
    dh1, dh1b, dg_xattn = _rows_matmul(dxq, get("w_xq"), mode="nt", kslice=(0, 1), res=None, rows_in=[h1, dh2],
                                       vec=g_xattn, epilogue=_ep_rms_bwd, outs=bwd_outs, name="mm_du2", dep=wts.dep())
    mark("dh1", dh1b)
    wts.grad("w_mix_out", mm(merged, dh1b, mode="tn", out_dtype=BF16, name="mm_dw_mix"))
    dya, dyc, dga, dgg = _dmerged_split(dh1b, get("w_mix_out"), proj, ya, yc, goff, cb, wts.dep(), "mm_dmerged")
    mark("dya", dya)
    wts.grad("w_attn_proj", mm(attn_o, dya, mode="tn", out_dtype=BF16, name="mm_dw_attn_proj"))
    dattn_o = mm(dya, get("w_attn_proj"), mode="nt", out_dtype=BF16, name="mm_dattn")
    wts.grad("w_conv_proj", mm(conv_o, dyc, mode="tn", out_dtype=BF16, name="mm_dw_conv_proj"))
    dconv_o = mark("dconv_o", mm(dyc, get("w_conv_proj"), mode="nt", out_dtype=BF16, name="mm_dconv"))
    dz, dgb, dgc, dconv_w = _conv_bwd(proj, get("conv_w"), dconv_o, zoff, cw, cb, wts.dep())
    mark("dz", dz)
    dq, dkc, dkp, dvc, dvp, dsinks = _attn_bwd(q_r, k_r, v_h, sinks, attn_o, dattn_o, wts.dep())
    mark("dq", dq)
    dproj = mark("dproj", _assemble_dproj(dq, dkc, dkp, dvc, dvp, cos, sin, dz, dgb, dgc, dga, dgg, wts.dep()))
    wts.grad("w_in", mm(u1, dproj, mode="tn", out_dtype=BF16, name="mm_dw_in"))
    du1 = mark("du1", _matmul_slice(dproj, get("w_in"), mode="nt", out_dtype=F32, name="mm_du1_k0", res=None,
                                    dep=wts.dep(), kslice=(0, 2), cols=(0, d)))
    grad_x, _, dg_mix = _rows_matmul(dproj, get("w_in"), mode="nt", kslice=(1, 2), res=du1, rows_in=[x, dh1],
                                     vec=g_mix, epilogue=_ep_rms_bwd, outs=bwd_outs, name="mm_du1_k1", dep=wts.dep())
    mark("grad_x", grad_x)

    small = dict(g_mix=dg_mix, g_xattn=dg_xattn, g_mem=dg_mem, g_ffn=dg_ffn, g_final=dg_final,
                 conv_w=dconv_w[:3], attn_sinks=dsinks[0:1, :sinks.shape[1]], loss=loss[0:1, 0:1])
    return grad_x, small


BIG = (("w_in", 1), ("w_attn_proj", 1), ("w_conv_proj", 1), ("w_mix_out", 0), ("w_xq", 0), ("w_xkv", 0),
       ("w_xo", 1), ("w_ffn_in", 1), ("w_ffn_out", 0))


def _place():
    x, y, c = lax.axis_index("x"), lax.axis_index("y"), lax.axis_index("c")
    chips = [(1 - x, y), (x, 1 - y), (1 - x, 1 - y)]
    return x, y, c, chips


def _window(ref, ax, shard_shape, s, h):
    sr, sc = shard_shape
    hr = sr // 2
    if ax == 1:
        return ref.at[pl.ds(pl.multiple_of(h * hr, 16), hr), pl.ds(pl.multiple_of(s * sc, LANES), sc)]
    return ref.at[pl.ds(pl.multiple_of(s * sr + h * hr, 16), hr), :]


def _half(ref, h):
    hr = ref.shape[0] // 2
    return ref.at[pl.ds(pl.multiple_of(h * hr, 16), hr), :]


def _remote(src, dst, send_sem, recv_sem, dev):
    return pltpu.make_async_remote_copy(src_ref=src, dst_ref=dst, send_sem=send_sem, recv_sem=recv_sem,
                                        device_id=dev, device_id_type=MESH)


def _cast_to_full(shard, ax, me, name, dtype=BF16):
    sr, sc = shard.shape
    tr = _pick(sr, (256, 352, 128, 64, 32, 16))
    nr = sr // tr
    full = (sr * N_CHIPS, sc) if ax == 0 else (sr, sc * N_CHIPS)

    def body(me_ref, s_ref, o_ref):
        o_ref[...] = s_ref[...].astype(o_ref.dtype)

    if ax == 1:
        out_spec = pl.BlockSpec((tr, sc), lambda r, me_ref: (r, me_ref[0]))
    else:
        out_spec = pl.BlockSpec((tr, sc), lambda r, me_ref: (me_ref[0] * nr + r, 0))
    return pl.pallas_call(
        body, name=name,
        grid_spec=pltpu.PrefetchScalarGridSpec(
            num_scalar_prefetch=1, grid=(nr,), in_specs=[pl.BlockSpec((tr, sc), lambda r, me_ref: (r, 0))],
            out_specs=out_spec),
        out_shape=_sds(full, dtype),
        compiler_params=_cparams("parallel"),
    )(me, shard)


HBM = pl.BlockSpec(memory_space=pltpu.HBM)
SEM = pl.BlockSpec(memory_space=pltpu.SEMAPHORE)
EFFECT = pltpu.SideEffectType.DATAFLOW_SIDE_EFFECTING


def _in_hbm(a):
    return pltpu.with_memory_space_constraint(a, pltpu.HBM)


def _gather_window(ref, ax, shard_shape, s, h):
    if h is not None:
        return _window(ref, ax, shard_shape, s, h)
    sr, sc = shard_shape
    if ax == 1:
        return ref.at[:, pl.ds(pl.multiple_of(s * sc, LANES), sc)]
    return ref.at[pl.ds(pl.multiple_of(s * sr, 8), sr), :]


def _ag_start(fulls, axes, shard_shapes, whole):
    n = len(fulls)

    def body(*refs):
        src = refs[:n]
        send_sems, recv_sems = refs[n], refs[n + 1]
        token = refs[2 * n + 2]
        x, y, c, chips = _place()
        me = 2 * x + y
        for i in range(n):
            h = None if whole[i] else c
            for j, chip in enumerate(chips):
                blk = _gather_window(src[i], axes[i], shard_shapes[i], me, h)
                _remote(blk, blk, send_sems.at[3 * i + j], recv_sems.at[3 * i + j], (*chip, c)).start()
        token[...] = jnp.zeros_like(token)

    res = pl.pallas_call(
        body, name="ag_start_" + str(n),
        out_shape=(pltpu.SemaphoreType.DMA((3 * n,)), pltpu.SemaphoreType.DMA((3 * n,)),
                   *[pltpu.HBM(f.shape, f.dtype) for f in fulls], _sds((8, LANES), F32)),
        in_specs=[HBM] * n, out_specs=(SEM, SEM, *[HBM] * n, pl.BlockSpec(memory_space=pltpu.VMEM)),
        input_output_aliases={i: 2 + i for i in range(n)},
        compiler_params=pltpu.CompilerParams(has_side_effects=EFFECT),
    )(*[_in_hbm(f) for f in fulls])
    return res[0], res[1], list(res[2:2 + n]), res[2 + n]


def _ag_mid(bufs, slots, axes, shard_shapes, whole, send_sems, recv_sems, after, name):
    ng = len(bufs)

    def body(*refs):
        src = refs[:ng]
        s_in, r_in = refs[ng], refs[ng + 1]
        fsend, frecv = refs[ng + 3], refs[ng + 4]
        x, y, c, chips = _place()
        me = 2 * x + y
        sib = (x, y, 1 - c)
        for k, i in enumerate(slots):
            h = None if whole[k] else c
            for j, chip in enumerate(chips):
                cj = 2 * chip[0] + chip[1]
                mine = _gather_window(src[k], axes[k], shard_shapes[k], me, h)
                theirs = _gather_window(src[k], axes[k], shard_shapes[k], cj, h)
                _remote(theirs, theirs, s_in.at[3 * i + j], r_in.at[3 * i + j], (*chip, c)).wait_recv()
                _remote(mine, mine, s_in.at[3 * i + j], r_in.at[3 * i + j], (*chip, c)).wait_send()
                if not whole[k]:
                    _remote(theirs, theirs, fsend.at[3 * k + j], frecv.at[3 * k + j], sib).start()
        token = refs[2 * ng + 5]
        token[...] = jnp.zeros_like(token)

    res = pl.pallas_call(
        body, name=name,
        out_shape=(pltpu.SemaphoreType.DMA((3 * ng,)), pltpu.SemaphoreType.DMA((3 * ng,)),
                   *[pltpu.HBM(b.shape, b.dtype) for b in bufs], _sds((8, LANES), F32)),
        in_specs=[HBM] * ng + [SEM, SEM, ANY],
        out_specs=(SEM, SEM, *[HBM] * ng, pl.BlockSpec(memory_space=pltpu.VMEM)),
        input_output_aliases={k: 2 + k for k in range(ng)},
        compiler_params=pltpu.CompilerParams(has_side_effects=EFFECT),
    )(*bufs, send_sems, recv_sems, after)
    return res[0], res[1], list(res[2:2 + ng]), res[2 + ng]


def _ag_wait(bufs, axes, shard_shapes, whole, fsend, frecv, after, name):
    ng = len(bufs)

    def body(*refs):
        src = refs[:ng]
        s_in, r_in = refs[ng], refs[ng + 1]
        x, y, c, chips = _place()
        sib = (x, y, 1 - c)
        for k in range(ng):
            if whole[k]:
                continue
            for j, chip in enumerate(chips):
                cj = 2 * chip[0] + chip[1]
                sent = _gather_window(src[k], axes[k], shard_shapes[k], cj, c)
                landed = _gather_window(src[k], axes[k], shard_shapes[k], cj, 1 - c)
                _remote(landed, landed, s_in.at[3 * k + j], r_in.at[3 * k + j], sib).wait_recv()
                _remote(sent, sent, s_in.at[3 * k + j], r_in.at[3 * k + j], sib).wait_send()

    res = pl.pallas_call(
        body, name=name,
        out_shape=tuple(pltpu.HBM(b.shape, b.dtype) for b in bufs),
        in_specs=[HBM] * ng + [SEM, SEM, ANY], out_specs=tuple([HBM] * ng),
        input_output_aliases={k: k for k in range(ng)},
        compiler_params=pltpu.CompilerParams(has_side_effects=EFFECT),
    )(*bufs, fsend, frecv, after)
    return list(res)


class _Schedule:
    GROUPS = ((("w_in", "conv_w"), "u1", "u1"),
              (("w_attn_proj", "w_conv_proj", "w_mix_out", "w_xq", "w_xkv", "w_xo"), "proj", "conv_o"),
              (("w_ffn_in",), "h1", "h2"),
              (("w_ffn_out",), "u3", "hid"))
    STARTS = ((0,), (1, 2, 3))
    REDUCE = ((("w_ffn_out",), "dhid", "grad:w_ffn_in", "grad:w_ffn_in"),
              (("w_ffn_in",), "grad:w_ffn_in", "dya", "grad_x"),
              (("w_xo", "w_xq", "w_xkv", "w_mix_out", "w_attn_proj", "w_conv_proj"), "dconv_o", "dq", "grad:w_in"),
              (("w_in",), "grad:w_in", "end", "end2"))

    def __init__(self, seed, axes, shard_shapes, place, on_ready):
        self.ax, self.shape, self.place, self.on_ready = axes, shard_shapes, place, on_ready
        self.stage, self.buf, self.slot, self.passes = {}, {}, {}, {}
        self.ready = set()
        self.grads = {}
        token = None
        for groups in self.STARTS:
            order = [nm for g in groups for nm in self.GROUPS[g][0]]
            send, recv, bufs, token = _ag_start([seed(nm, token) for nm in order], *self._meta(order))
            self.buf.update(zip(order, bufs))
            self.slot.update({nm: (send, recv, k) for k, nm in enumerate(order)})
        self.token = self.latest = token
        self.mark("start", token)

    def _meta(self, names):
        return ([self.ax[nm] for nm in names], [self.shape[nm] for nm in names], [nm == "conv_w" for nm in names])

    def mark(self, tag, value):
        marked = value
        for g, (names, mid, wait) in enumerate(self.GROUPS):
            if tag == mid:
                send, recv, _ = self.slot[names[0]]
                fs, fr, bufs, self.latest = _ag_mid([self.buf[nm] for nm in names], [self.slot[nm][2] for nm in names],
                                                    *self._meta(names), send, recv, value, "ag_mid_%d" % g)
                self.buf.update(zip(names, bufs))
                self.passes[g] = (fs, fr)
            if tag == wait:
                fs, fr = self.passes[g]
                bufs = _ag_wait([self.buf[nm] for nm in names], *self._meta(names), fs, fr, value, "ag_wait_%d" % g)
                self.buf.update(zip(names, bufs))
                self.ready.update(names)
        for g, (names, send, total, finish) in enumerate(self.REDUCE):
            st = self.stage.get(g)
            if st is None:
                continue
            ng = len(names)
            if tag == send and st["at"] == "pair":
                arrs = _exchange_wait("rs_pair_wait_%d" % g, st["arrs"], *st["sems"], st["plan"], value)
                parts = [_pair_add(arrs[k], arrs[ng + k], self.ax[nm], self.shape[nm], self.place, "pair_add_" + nm)
                         for k, nm in enumerate(names)]
                plan, nsem = _plan_chip(ng)
                ss, rs, arrs, self.latest = _exchange_start(
                    "rs_chip_start_%d" % g, parts + [lax.empty(p.shape, p.dtype) for p in parts], nsem, plan)
                self.stage[g] = dict(at="chip", arrs=arrs, sems=(ss, rs), plan=plan)
            elif tag == total and st["at"] == "chip":
                arrs = _exchange_wait("rs_chip_wait_%d" % g, st["arrs"], *st["sems"], st["plan"], value)
                halves = [_chip_add(arrs[k], arrs[ng + k], self.place, "chip_add_" + nm) for k, nm in enumerate(names)]
                plan, nsem = _plan_gather(ng)
                ss, rs, arrs, self.latest = _exchange_start("rs_gather_start_%d" % g, halves, nsem, plan)
                self.stage[g] = dict(at="gather", arrs=arrs, sems=(ss, rs), plan=plan)
            st = self.stage[g]
            if tag == finish and st["at"] == "gather":
                arrs = _exchange_wait("rs_gather_wait_%d" % g, st["arrs"], *st["sems"], st["plan"], value)
                self.stage[g] = dict(at="done")
                for nm, shard in zip(names, arrs):
                    value = self.on_ready(nm, shard)
        return marked

    def get(self, name):
        assert name in self.ready, name
        return self.buf[name]

    def grad(self, name, g):
        self.grads[name] = g
        for gi, (names, _, _, _) in enumerate(self.REDUCE):
            if name == names[-1]:
                gs = [self.grads[nm] for nm in names]
                plan, nsem = _plan_pair(len(names), [self.ax[nm] for nm in names], [self.shape[nm] for nm in names])
                ss, rs, arrs, self.latest = _exchange_start(
                    "rs_pair_start_%d" % gi, gs + [lax.empty(a.shape, a.dtype) for a in gs], nsem, plan)
                self.stage[gi] = dict(at="pair", arrs=arrs, sems=(ss, rs), plan=plan)
                g = self.latest
        self.mark("grad:" + name, g)

    def dep(self):
        return self.latest


def _exchange_start(name, arrays, nsem, plan):
    n = len(arrays)

    def body(*refs):
        send_sems, recv_sems, token = refs[n], refs[n + 1], refs[2 * n + 2]
        sends, _ = plan(refs[:n])
        for k, (src, dst, dev) in enumerate(sends):
            _remote(src, dst, send_sems.at[k], recv_sems.at[k], dev).start()
        token[...] = jnp.zeros_like(token)

    res = pl.pallas_call(
        body, name=name,
        out_shape=(pltpu.SemaphoreType.DMA((nsem,)), pltpu.SemaphoreType.DMA((nsem,)),
                   *[pltpu.HBM(a.shape, a.dtype) for a in arrays], _sds((8, LANES), F32)),
        in_specs=[HBM] * n, out_specs=(SEM, SEM, *[HBM] * n, pl.BlockSpec(memory_space=pltpu.VMEM)),
        input_output_aliases={i: 2 + i for i in range(n)},
        compiler_params=pltpu.CompilerParams(has_side_effects=EFFECT),
    )(*[_in_hbm(a) for a in arrays])
    return res[0], res[1], list(res[2:2 + n]), res[2 + n]


def _exchange_wait(name, arrays, send_sems, recv_sems, plan, after):
    n = len(arrays)

    def body(*refs):
        s_in, r_in = refs[n], refs[n + 1]
        sends, recvs = plan(refs[:n])
        for k, land in enumerate(recvs):
            _remote(land, land, s_in.at[k], r_in.at[k], sends[k][2]).wait_recv()
        for k, (src, _, dev) in enumerate(sends):
            _remote(src, src, s_in.at[k], r_in.at[k], dev).wait_send()

    res = pl.pallas_call(
        body, name=name,
        out_shape=tuple(pltpu.HBM(a.shape, a.dtype) for a in arrays),
        in_specs=[HBM] * n + [SEM, SEM, ANY], out_specs=tuple([HBM] * n),
        input_output_aliases={i: i for i in range(n)},
        compiler_params=pltpu.CompilerParams(has_side_effects=EFFECT),
    )(*arrays, send_sems, recv_sems, after)
    return list(res)


def _plan_pair(n, axes, shard_shapes):
    def plan(refs):
        g, ra = refs[:n], refs[n:]
        x, y, c, _ = _place()
        sib = (x, y, 1 - c)

        def pieces(ref, i, h):
            if axes[i] == 1:
                return [_half(ref, h)]
            return [_window(ref, 0, shard_shapes[i], s, h) for s in range(N_CHIPS)]

        sends, recvs = [], []
        for i in range(n):
            sends += [(src, dst, sib) for src, dst in zip(pieces(g[i], i, 1 - c), pieces(ra[i], i, 1 - c))]
            recvs += pieces(ra[i], i, c)
        return sends, recvs

    return plan, sum(1 if ax == 1 else N_CHIPS for ax in axes)


def _plan_chip(n):
    def plan(refs):
        p, rc = refs[:n], refs[n:]
        x, y, c, chips = _place()
        me = 2 * x + y
        sends, recvs = [], []
        for i in range(n):
            for chip in chips:
                cj = 2 * chip[0] + chip[1]
                sends.append((p[i].at[cj], rc[i].at[me], (*chip, c)))
                recvs.append(rc[i].at[cj])
        return sends, recvs

    return plan, 3 * n


def _plan_gather(n):
    def plan(refs):
        x, y, c, _ = _place()
        sib = (x, y, 1 - c)
        return ([(_half(r, c), _half(r, c), sib) for r in refs], [_half(r, 1 - c) for r in refs])

    return plan, n


def _pair_add(g, ra, ax, shard_shape, place, name):
    sr, sc = shard_shape
    hr = sr // 2
    wc = sc
    tr = _pick(hr, (256, 352, 128, 64, 32, 16))
    nr = hr // tr

    def body(p_ref, a_ref, b_ref, o_ref):
        o_ref[...] = (a_ref[...].astype(F32) + b_ref[...].astype(F32)).astype(o_ref.dtype)

    if ax == 1:
        src = pl.BlockSpec((tr, wc), lambda s, r, p_ref: (p_ref[0] * nr + r, s))
    else:
        src = pl.BlockSpec((tr, wc), lambda s, r, p_ref: (s * 2 * nr + p_ref[0] * nr + r, 0))
    return pl.pallas_call(
        body, name=name,
        grid_spec=pltpu.PrefetchScalarGridSpec(
            num_scalar_prefetch=1, grid=(N_CHIPS, nr), in_specs=[src, src],
            out_specs=pl.BlockSpec((None, tr, wc), lambda s, r, p_ref: (s, r, 0))),
        out_shape=_sds((N_CHIPS, hr, wc), BF16),
        compiler_params=_cparams("parallel", "parallel"),
    )(place, g, ra)


def _chip_add(part, rc, place, name):
    _, hr, wc = rc.shape
    tr = _pick(hr, (256, 352, 128, 64, 32, 16))
    nr = hr // tr

    def body(p_ref, own_ref, r1_ref, r2_ref, r3_ref, o_ref):
        acc = own_ref[...].astype(F32)
        for r_ref in (r1_ref, r2_ref, r3_ref):
            acc = acc + r_ref[...].astype(F32)
        o_ref[...] = acc

    def slot(k):
        return pl.BlockSpec((None, tr, wc), lambda r, p_ref: ((p_ref[1] + k) % N_CHIPS, r, 0))

    return pl.pallas_call(
        body, name=name,
        grid_spec=pltpu.PrefetchScalarGridSpec(
            num_scalar_prefetch=1, grid=(nr,), in_specs=[slot(0), slot(1), slot(2), slot(3)],
            out_specs=pl.BlockSpec((tr, wc), lambda r, p_ref: (p_ref[0] * nr + r, 0))),
        out_shape=_sds((2 * hr, wc), F32),
        compiler_params=_cparams("parallel"),
    )(place, part, rc, rc, rc)


N_DEV = 8


def _all_reduce_small(buf):
    r, cdim = buf.shape

    def body(x_ref, o_ref, land, send_sems, recv_sems):
        x, y, c, _ = _place()
        me = 4 * x + 2 * y + c
        land[me] = x_ref[...]
        sends = []
        for k in range(1, N_DEV):
            kx, ky, kc = (k >> 2) & 1, (k >> 1) & 1, k & 1
            peer = (1 - x if kx else x, 1 - y if ky else y, 1 - c if kc else c)
            cp = _remote(x_ref, land.at[me], send_sems.at[k - 1], recv_sems.at[k - 1], peer)
            cp.start()
            sends.append(cp)
        for k in range(1, N_DEV):
            kx, ky, kc = (k >> 2) & 1, (k >> 1) & 1, k & 1
            peer = (1 - x if kx else x, 1 - y if ky else y, 1 - c if kc else c)
            pidx = 4 * peer[0] + 2 * peer[1] + peer[2]
            _remote(land.at[pidx], land.at[pidx], send_sems.at[k - 1], recv_sems.at[k - 1], peer).wait_recv()
        for cp in sends:
            cp.wait_send()
        acc = land[0]
        for dev in range(1, N_DEV):
            acc = acc + land[dev]
        o_ref[...] = acc

    vm = pl.BlockSpec(memory_space=pltpu.VMEM)
    return pl.pallas_call(
        body, name="all_reduce_small", in_specs=[vm], out_specs=vm, out_shape=_sds((r, cdim), F32),
        scratch_shapes=[pltpu.VMEM((N_DEV, r, cdim), F32), pltpu.SemaphoreType.DMA((N_DEV - 1,)),
                        pltpu.SemaphoreType.DMA((N_DEV - 1,))],
    )(buf)


SMALL_ROWS = 16


def kernel(x, mem, g_mix, w_in, conv_w, attn_sinks, w_attn_proj, w_conv_proj, w_mix_out, g_xattn, g_mem, w_xq, w_xkv, w_xo, g_ffn, w_ffn_in, w_ffn_out, g_final, loss_target, m_g_mix, m_w_in, m_conv_w, m_attn_sinks, m_w_attn_proj, m_w_conv_proj, m_w_mix_out, m_g_xattn, m_g_mem, m_w_xq, m_w_xkv, m_w_xo, m_g_ffn, m_w_ffn_in, m_w_ffn_out, m_g_final, v_g_mix, v_w_in, v_conv_w, v_attn_sinks, v_w_attn_proj, v_w_conv_proj, v_w_mix_out, v_g_xattn, v_g_mem, v_w_xq, v_w_xkv, v_w_xo, v_g_ffn, v_w_ffn_in, v_w_ffn_out, v_g_final):
    w = dict(g_mix=g_mix, w_in=w_in[0], conv_w=conv_w[0], attn_sinks=attn_sinks, w_attn_proj=w_attn_proj[0],
             w_conv_proj=w_conv_proj[0], w_mix_out=w_mix_out[0], g_xattn=g_xattn, g_mem=g_mem, w_xq=w_xq[0],
             w_xkv=w_xkv[0], w_xo=w_xo[0], g_ffn=g_ffn, w_ffn_in=w_ffn_in[0], w_ffn_out=w_ffn_out[0],
             g_final=g_final[None])
    m = dict(g_mix=m_g_mix, w_in=m_w_in[0], conv_w=m_conv_w[0], attn_sinks=m_attn_sinks,
             w_attn_proj=m_w_attn_proj[0], w_conv_proj=m_w_conv_proj[0], w_mix_out=m_w_mix_out[0],
             g_xattn=m_g_xattn, g_mem=m_g_mem, w_xq=m_w_xq[0], w_xkv=m_w_xkv[0], w_xo=m_w_xo[0], g_ffn=m_g_ffn,
             w_ffn_in=m_w_ffn_in[0], w_ffn_out=m_w_ffn_out[0], g_final=m_g_final[None])
    v = dict(g_mix=v_g_mix, w_in=v_w_in[0], conv_w=v_conv_w[0], attn_sinks=v_attn_sinks,
             w_attn_proj=v_w_attn_proj[0], w_conv_proj=v_w_conv_proj[0], w_mix_out=v_w_mix_out[0],
             g_xattn=v_g_xattn, g_mem=v_g_mem, w_xq=v_w_xq[0], w_xkv=v_w_xkv[0], w_xo=v_w_xo[0], g_ffn=v_g_ffn,
             w_ffn_in=v_w_ffn_in[0], w_ffn_out=v_w_ffn_out[0], g_final=v_g_final[None])
    names = [nm for nm, _ in BIG]
    axes = [ax for _, ax in BIG]
    d = x.shape[2]
    cw = w["conv_w"].shape[1] * N_CHIPS
    chip = (2 * lax.axis_index("x") + lax.axis_index("y")).astype(jnp.int32)
    place = jnp.stack([lax.axis_index("c").astype(jnp.int32), chip])
    shard_shapes = [w[nm].shape for nm in names]

    def seed(nm, token):
        me1 = chip.reshape(1)
        if token is not None:
            me1 = me1 + token[0, 0].astype(jnp.int32)
        if nm == "conv_w":
            return _cast_to_full(w[nm], 1, me1, "place_conv_w", F32)
        return _cast_to_full(w[nm], dict(BIG)[nm], me1, "cast_" + nm)

    upd = {}

    def on_ready(nm, shard):
        upd[nm] = _adamw(w[nm], shard, m[nm], v[nm], "adamw_" + nm)
        grads[nm] = upd[nm][3]
        return upd[nm][0]

    grads = {}
    wts = _Schedule(seed, dict(zip(names + ["conv_w"], axes + [1])),
                    dict(zip(names + ["conv_w"], shard_shapes + [w["conv_w"].shape])), place, on_ready)
    aw, cw = w["w_attn_proj"].shape[0], w["w_conv_proj"].shape[0]
    kvw = (w["w_in"].shape[1] * N_CHIPS - aw - 3 * cw - 2 * d) // 2
    grad_x, small = _local_step(
        x[0], mem[0], loss_target[0], w["g_mix"] + wts.token[0:1, 0:1], w["attn_sinks"], w["g_xattn"], w["g_mem"],
        w["g_ffn"], w["g_final"], (aw, cw, kvw), wts)

    pw = max(d, cw)

    def row(a):
        return jnp.pad(a, ((0, 0), (0, pw - a.shape[1])))

    gains = ("g_mix", "g_xattn", "g_mem", "g_ffn", "g_final")
    packed = jnp.concatenate(
        [row(small[nm]) for nm in gains] + [row(small["conv_w"]),
         row(jnp.concatenate([small["attn_sinks"], small["loss"]], axis=1)),
         jnp.zeros((SMALL_ROWS - 9, pw), F32)], axis=0)
    total = _all_reduce_small(packed)
    wts.mark("end", total)
    nsink = attn_sinks.shape[1]
    grads.update({nm: total[k:k + 1, :d] for k, nm in enumerate(gains)})
    grads.update(conv_w=lax.dynamic_slice(total, (5, chip * (cw // N_CHIPS)), (3, cw // N_CHIPS)),
                 attn_sinks=total[8:9, :nsink])
    loss = total[8, nsink]
    for nm in gains + ("conv_w", "attn_sinks"):
        upd[nm] = _adamw(w[nm], grads[nm], m[nm], v[nm], "adamw_" + nm)
    wts.mark("end2", upd["g_final"][0])

    order = ["g_mix", "w_in", "conv_w", "attn_sinks", "w_attn_proj", "w_conv_proj", "w_mix_out", "g_xattn", "g_mem",
             "w_xq", "w_xkv", "w_xo", "g_ffn", "w_ffn_in", "w_ffn_out", "g_final"]

    stacked = set(names) | {"conv_w"}

    def shaped(nm, a):
        if nm == "g_final":
            return a[0]
        return a[None] if nm in stacked else a

    outs = [loss, grad_x[None]]
    outs += [shaped(nm, grads[nm]) for nm in order]
    for k in range(3):
        outs += [shaped(nm, upd[nm][k]) for nm in order]
    return tuple(outs)
```

```python
import functools

import jax
import jax.numpy as jnp
from jax import lax
from jax.experimental import pallas as pl
from jax.experimental.pallas import tpu as pltpu

F32 = jnp.float32
BF16 = jnp.bfloat16

VMEM_LIMIT_BYTES = 56 * 1024 * 1024
LANES = 128
HEAD_DIM = 64
BLOCK = 128
X_HEAD_DIM = 128
ROPE_THETA = 10000.0
EPS = 1e-6
NEG = -1e30
ADAM_LR, ADAM_B1, ADAM_B2, ADAM_EPS, ADAM_WD, ADAM_STEP = 0.001, 0.9, 0.999, 1e-08, 0.01, 10
N_CHIPS = 4
MESH = pl.DeviceIdType.MESH
ANY = pl.BlockSpec(memory_space=pl.ANY)


def _pick(dim, prefs):
    for p in prefs:
        if dim % p == 0:
            return p
    return dim


def _cparams(*sem):
    return pltpu.CompilerParams(dimension_semantics=sem, vmem_limit_bytes=VMEM_LIMIT_BYTES)


def _sds(shape, dtype):
    return jax.ShapeDtypeStruct(shape, dtype)


def _pin(dep):
    return ([], []) if dep is None else ([dep], [ANY])


def _drop_pin(body, n_in, dep):
    if dep is None:
        return body
    return lambda *refs: body(*refs[:n_in], *refs[n_in + 1:])


def _sigmoid(v):
    return 0.5 * jnp.tanh(0.5 * v) + 0.5


MATMUL_VMEM_BUDGET = 46 * 1024 * 1024


def _tiles(mode, m, n):
    if mode == "tn" and m % 1024 != 0:
        return _pick(m, (512, 256, 128)), _pick(n, (1024, 512, 256, 128)), True
    return _pick(m, (1024, 512, 256, 128)), _pick(n, (512, 256, 128)), False


def _k_parts(m, n, k):
    tm, tn = _pick(m, (1024, 512, 256, 128)), _pick(n, (512, 256, 128))
    for parts in range(1, k // LANES + 1):
        if k % (parts * LANES) == 0 and 4 * (tm + tn) * (k // parts) + 16 * tm * tn <= MATMUL_VMEM_BUDGET:
            return parts
    return k // LANES


def _matmul(a, b, *, mode, out_dtype, name, res=None, dep=None, cols=None):
    if mode == "nn":
        (m, k), (k2, n) = a.shape, b.shape
    elif mode == "nt":
        (m, k), (n, k2) = (a.shape[-2], a.shape[-1] * (a.shape[0] if a.ndim == 3 else 1)), b.shape
    else:
        (k, m), (k2, n) = a.shape, (b.shape[-2], b.shape[-1] * (b.shape[0] if b.ndim == 3 else 1))
    assert k == k2, (a.shape, b.shape, mode)
    assert cols is None or mode != "nt"
    cols = cols or (0, n)
    parts = 1 if mode == "tn" else (a.shape[0] if a.ndim == 3 else _k_parts(m, cols[1], k))
    for p in range(parts):
        last = p == parts - 1
        res = _matmul_slice(a, b, mode=mode, out_dtype=out_dtype if last else F32, res=res, dep=dep, cols=cols,
                            kslice=(p, parts), name=name + ("_k%d" % p if parts > 1 else ""))
    return res


def _matmul_slice(a, b, *, mode, out_dtype, name, res, dep, kslice, cols):
    part, parts = kslice
    (m, k) = a.shape[-2:] if mode != "tn" else a.shape[::-1]
    col0, n = cols
    tk = k if a.ndim == 3 else k // parts
    tm, tn, swap = _tiles(mode, m, b.shape[2] if (mode == "tn" and b.ndim == 3) else n)
    while col0 % tn:
        tn //= 2
    joff = col0 // tn
    dims = {"nn": (((1,), (0,)), ((), ())), "nt": (((1,), (1,)), ((), ())), "tn": (((0,), (0,)), ((), ()))}[mode]
    has_res = res is not None
    has_dep = dep is not None

    def body(*refs):
        a_ref, b_ref = refs[0], refs[1]
        o_ref = refs[2 + has_res + has_dep]
        val = lax.dot_general(a_ref[...], b_ref[...], dims, preferred_element_type=F32)
        if has_res:
            val = val + refs[2][...]
        o_ref[...] = val.astype(o_ref.dtype)

    def spec(shape, f):
        if swap:
            return pl.BlockSpec(shape, lambda j, i: f(i, j))
        return pl.BlockSpec(shape, f)

    if mode == "tn":
        a_spec = spec((tk, tm), lambda i, j: (part, i))
    elif a.ndim == 3:
        a_spec = spec((None, tm, tk), lambda i, j: (part, i, 0))
    else:
        a_spec = spec((tm, tk), lambda i, j: (i, part))
    if mode == "nt":
        b_spec = spec((tn, tk), lambda i, j: (j, part))
    elif b.ndim == 3:
        per = b.shape[2] // tn
        b_spec = spec((None, tk, tn), lambda i, j: (j // per, part, j % per))
    else:
        b_spec = spec((tk, tn), lambda i, j: (part, joff + j))
    o_spec = spec((tm, tn), lambda i, j: (i, j))
    return pl.pallas_call(
        body,
        name=name,
        grid=(n // tn, m // tm) if swap else (m // tm, n // tn),
        in_specs=[a_spec, b_spec] + ([o_spec] if has_res else []) + ([ANY] if has_dep else []),
        out_specs=o_spec,
        out_shape=_sds((m, n), out_dtype),
        compiler_params=_cparams("parallel", "parallel"),
    )(*([a, b] + ([res] if has_res else []) + ([dep] if has_dep else [])))


def _rms_fwd(x, g, name):
    t, d = x.shape
    tm = _pick(t, (512, 256, 128))

    def body(x_ref, g_ref, o_ref):
        xf = x_ref[...]
        r = lax.rsqrt(jnp.mean(xf * xf, axis=-1, keepdims=True) + EPS)
        o_ref[...] = (xf * r * g_ref[...]).astype(o_ref.dtype)

    row = pl.BlockSpec((tm, d), lambda i: (i, 0))
    return pl.pallas_call(
        body, name=name, grid=(t // tm,),
        in_specs=[row, pl.BlockSpec((1, d), lambda i: (0, 0))],
        out_specs=row, out_shape=_sds((t, d), BF16),
        compiler_params=_cparams("parallel"),
    )(x, g)


def _rms_bwd_math(xf, g, du):
    r = lax.rsqrt(jnp.mean(xf * xf, axis=-1, keepdims=True) + EPS)
    xh = xf * r
    gdy = g * du
    dx = r * (gdy - xh * jnp.mean(gdy * xh, axis=-1, keepdims=True))
    dg = jnp.sum(du * xh, axis=0, keepdims=True)
    return dx, dg


def _rms_bwd(x, g, du, dh, name):
    t, d = x.shape
    tm = _pick(t, (256, 128))
    has_dh = dh is not None

    def body(*refs):
        x_ref, g_ref, du_ref = refs[0], refs[1], refs[2]
        o_ref, ob_ref, dg_ref = refs[3 + has_dh:]
        dx, dg = _rms_bwd_math(x_ref[...], g_ref[...], du_ref[...].astype(F32))
        if has_dh:
            dx = dx + refs[3][...]
        o_ref[...] = dx
        ob_ref[...] = dx.astype(BF16)

        @pl.when(pl.program_id(0) == 0)
        def _():
            dg_ref[...] = dg

        @pl.when(pl.program_id(0) > 0)
        def _():
            dg_ref[...] += dg

    row = pl.BlockSpec((tm, d), lambda i: (i, 0))
    vec = pl.BlockSpec((1, d), lambda i: (0, 0))
    return pl.pallas_call(
        body, name=name, grid=(t // tm,),
        in_specs=[row, vec, row] + ([row] if has_dh else []),
        out_specs=[row, row, vec],
        out_shape=[_sds((t, d), F32), _sds((t, d), BF16), _sds((1, d), F32)],
        compiler_params=_cparams("arbitrary"),
    )(*([x, g, du] + ([dh] if has_dh else [])))


ROW_TILE = 256


def _rows_matmul(a, b, *, mode, kslice, res, rows_in, vec, epilogue, outs, name, dep):
    part, parts = kslice
    t = a.shape[-2]
    n = b.shape[1] if mode == "nn" else b.shape[0]
    kc = a.shape[-1] if a.ndim == 3 else a.shape[1] // parts
    tm = _pick(t, (ROW_TILE, 128))
    dims = (((1,), (0,)), ((), ())) if mode == "nn" else (((1,), (1,)), ((), ()))
    has_res = res is not None
    deps = [] if dep is None else [dep]
    n_in = 2 + has_res + len(rows_in) + 1 + len(deps)

    def body(*refs):
        val = lax.dot_general(refs[0][...], refs[1][...], dims, preferred_element_type=F32)
        if has_res:
            val = val + refs[2][...]
        row_refs = refs[2 + has_res:2 + has_res + len(rows_in)]
        epilogue(val, row_refs, refs[2 + has_res + len(rows_in)], refs[n_in:], pl.program_id(0) == 0)

    if a.ndim == 3:
        a_spec = pl.BlockSpec((None, tm, kc), lambda i: (part, i, 0))
    else:
        a_spec = pl.BlockSpec((tm, kc), lambda i: (i, part))
    if mode == "nn":
        b_spec = pl.BlockSpec((kc, n), lambda i: (part, 0), pipeline_mode=pl.Buffered(1))
    else:
        b_spec = pl.BlockSpec((n, kc), lambda i: (0, part), pipeline_mode=pl.Buffered(1))
    row = pl.BlockSpec((tm, n), lambda i: (i, 0))
    kinds = {"row": row, "vec": pl.BlockSpec((1, n), lambda i: (0, 0)), "lane": pl.BlockSpec((1, LANES), lambda i: (0, 0))}
    shapes = {"row": (t, n), "vec": (1, n), "lane": (1, LANES)}
    return pl.pallas_call(
        body, name=name, grid=(t // tm,),
        in_specs=[a_spec, b_spec] + [row] * (has_res + len(rows_in)) + [kinds["vec"]] + [ANY] * len(deps),
        out_specs=[kinds[o[0]] for o in outs],
        out_shape=[_sds(shapes[o[0]], o[1] if len(o) > 1 else F32) for o in outs],
        compiler_params=_cparams("arbitrary"),
    )(*([a, b] + ([res] if has_res else []) + list(rows_in) + [vec] + deps))


def _accumulate(ref, value, first):
    @pl.when(first)
    def _():
        ref[...] = value

    @pl.when(jnp.logical_not(first))
    def _():
        ref[...] += value


def _ep_norm(val, rows, g_ref, outs, first):
    h_ref, u_ref = outs
    h_ref[...] = val
    r = lax.rsqrt(jnp.mean(val * val, axis=-1, keepdims=True) + EPS)
    u_ref[...] = (val * r * g_ref[...]).astype(u_ref.dtype)


def _ep_loss(val, rows, g_ref, outs, first):
    o_ref, ob_ref, dg_ref, l_ref = outs
    gv = g_ref[...]
    r = lax.rsqrt(jnp.mean(val * val, axis=-1, keepdims=True) + EPS)
    err = val * r * gv - rows[0][...]
    part = 0.5 * jnp.sum(jnp.mean(err * err, axis=-1, keepdims=True), axis=0, keepdims=True)
    dx, dg = _rms_bwd_math(val, gv, err * (1.0 / val.shape[1]))
    o_ref[...] = dx
    ob_ref[...] = dx.astype(BF16)
    _accumulate(dg_ref, dg, first)
    _accumulate(l_ref, jnp.broadcast_to(part, (1, LANES)), first)


def _ep_rms_bwd(val, rows, g_ref, outs, first):
    o_ref, ob_ref, dg_ref = outs
    dx, dg = _rms_bwd_math(rows[0][...], g_ref[...], val)
    dx = dx + rows[1][...]
    o_ref[...] = dx
    ob_ref[...] = dx.astype(BF16)
    _accumulate(dg_ref, dg, first)


def _rope_tables(t):
    half = HEAD_DIM // 2
    inv_freq = ROPE_THETA ** (-jnp.arange(half, dtype=F32) / half)
    ang = jnp.arange(t, dtype=F32)[:, None] * inv_freq[None, :]
    cos = jnp.cos(ang)
    sin = jnp.sin(ang)
    reps = LANES // HEAD_DIM
    cos_t = jnp.tile(jnp.concatenate([cos, cos], axis=1), (1, reps))
    sin_t = jnp.tile(jnp.concatenate([-sin, sin], axis=1), (1, reps))
    return cos_t, sin_t


def _rope(v, cos, sin):
    w = v.shape[1]
    c = jnp.tile(cos, (1, w // LANES))
    s = jnp.tile(sin, (1, w // LANES))
    lane = lax.broadcasted_iota(jnp.int32, v.shape, 1)
    first = (lane % HEAD_DIM) < (HEAD_DIM // 2)
    partner = jnp.where(first, pltpu.roll(v, w - HEAD_DIM // 2, 1), pltpu.roll(v, HEAD_DIM // 2, 1))
    return v * c + partner * s


def _heads(v, count):
    return jnp.concatenate([v[:, i * HEAD_DIM:(i + 1) * HEAD_DIM] for i in range(count)], axis=0)


def _unheads(v, count):
    r = v.shape[0] // count
    return jnp.concatenate([v[g * r:(g + 1) * r] for g in range(count)], axis=1)


def _rope_qkv(proj, cos, sin, aw, kvw):
    t = proj.shape[0]
    nkv = kvw // HEAD_DIM
    koff = aw // kvw
    tm = _pick(t, (256, 128))

    def body(q_ref, k_ref, v_ref, c_ref, s_ref, qo_ref, ko_ref, vo_ref):
        c, s = c_ref[...], s_ref[...]
        qo_ref[...] = _rope(q_ref[...].astype(F32), c, s).astype(BF16)
        k = _rope(k_ref[...].astype(F32), c, s).astype(BF16)
        v = v_ref[...].astype(BF16)
        for h in range(nkv):
            ko_ref[h] = k[:, h * HEAD_DIM:(h + 1) * HEAD_DIM]
            vo_ref[h] = v[:, h * HEAD_DIM:(h + 1) * HEAD_DIM]

    def row(w, j):
        return pl.BlockSpec((tm, w), lambda i: (i, j))

    hm = pl.BlockSpec((nkv, tm, HEAD_DIM), lambda i: (0, i, 0))
    return pl.pallas_call(
        body, name="rope_qkv", grid=(t // tm,),
        in_specs=[row(aw, 0), row(kvw, koff), row(kvw, koff + 1), row(LANES, 0), row(LANES, 0)],
        out_specs=[row(aw, 0), hm, hm],
        out_shape=[_sds((t, aw), BF16), _sds((nkv, t, HEAD_DIM), BF16), _sds((nkv, t, HEAD_DIM), BF16)],
        compiler_params=_cparams("parallel"),
    )(proj, proj, proj, cos, sin)


def _attn_probs(qs, kb, n, h, qpk, sinks_ref):
    s = lax.dot_general(qs, kb, (((1,), (1,)), ((), ())), preferred_element_type=F32) * (HEAD_DIM ** -0.5)
    qi = lax.broadcasted_iota(jnp.int32, (BLOCK, 2 * BLOCK), 0)
    kc = lax.broadcasted_iota(jnp.int32, (BLOCK, 2 * BLOCK), 1)
    valid = (kc > qi) & (kc <= qi + BLOCK) & ((kc >= BLOCK) | (n > 0))
    bias = jnp.tile(jnp.where(valid, 0.0, NEG).astype(F32), (qpk, 1))
    s = s + bias
    rowg = lax.broadcasted_iota(jnp.int32, (qpk * BLOCK, 1), 0) // BLOCK
    sink = jnp.zeros((qpk * BLOCK, 1), F32)
    for g in range(qpk):
        sink = jnp.where(rowg == g, sinks_ref[0, h * qpk + g], sink)
    m = jnp.maximum(jnp.max(s, axis=-1, keepdims=True), sink)
    e = jnp.exp(s - m)
    es = jnp.exp(sink - m)
    inv = 1.0 / (jnp.sum(e, axis=-1, keepdims=True) + es)
    return e * inv, es * inv, rowg


HEADS_PER_STEP = 4


def _attn_specs(qw, hp):
    def head(f):
        return pl.BlockSpec((hp, BLOCK, HEAD_DIM), lambda n, h: (h, f(n), 0))

    cur = lambda n: n
    prev = lambda n: jnp.maximum(n - 1, 0)
    return [pl.BlockSpec((BLOCK, hp * qw), lambda n, h: (n, h)), head(cur), head(prev), head(cur), head(prev),
            pl.BlockSpec(memory_space=pltpu.SMEM)]


def _attn_fwd(q_r, k_r, v_h, sinks):
    t, aw = q_r.shape
    nkv = k_r.shape[0]
    qpk = aw // (nkv * HEAD_DIM)
    qw = qpk * HEAD_DIM
    hp = HEADS_PER_STEP if nkv % HEADS_PER_STEP == 0 else 1

    def body(q_ref, kc_ref, kp_ref, vc_ref, vp_ref, sinks_ref, o_ref):
        n, hg = pl.program_id(0), pl.program_id(1)
        outs = []
        for j in range(hp):
            kb = jnp.concatenate([kp_ref[j], kc_ref[j]], axis=0)
            vb = jnp.concatenate([vp_ref[j], vc_ref[j]], axis=0)
            qs = _heads(q_ref[:, j * qw:(j + 1) * qw], qpk)
            p, _, _ = _attn_probs(qs, kb, n, hg * hp + j, qpk, sinks_ref)
            outs.append(_unheads(jnp.dot(p.astype(BF16), vb, preferred_element_type=F32), qpk))
        o_ref[...] = jnp.concatenate(outs, axis=1).astype(o_ref.dtype)

    return pl.pallas_call(
        body, name="attn_fwd", grid=(t // BLOCK, nkv // hp),
        in_specs=_attn_specs(qw, hp),
        out_specs=pl.BlockSpec((BLOCK, hp * qw), lambda n, h: (n, h)),
        out_shape=_sds((t, aw), BF16),
        compiler_params=_cparams("parallel", "parallel"),
    )(q_r, k_r, k_r, v_h, v_h, sinks)


def _attn_bwd(q_r, k_r, v_h, sinks, o, do, dep=None):
    t, aw = q_r.shape
    nkv = k_r.shape[0]
    qpk = aw // (nkv * HEAD_DIM)
    qw = qpk * HEAD_DIM
    hp = HEADS_PER_STEP if nkv % HEADS_PER_STEP == 0 else 1
    scale = HEAD_DIM ** -0.5

    def body(q_ref, kc_ref, kp_ref, vc_ref, vp_ref, sinks_ref, o_ref, do_ref,
             dq_ref, dkc_ref, dkp_ref, dvc_ref, dvp_ref, ds_ref):
        n, hg = pl.program_id(0), pl.program_id(1)
        lane = lax.broadcasted_iota(jnp.int32, (8, LANES), 1)
        row0 = lax.broadcasted_iota(jnp.int32, (8, LANES), 0) == 0
        dsink = jnp.zeros((8, LANES), F32)
        dqs = []
        for j in range(hp):
            h = hg * hp + j
            cols = slice(j * qw, (j + 1) * qw)
            kb = jnp.concatenate([kp_ref[j], kc_ref[j]], axis=0)
            vb = jnp.concatenate([vp_ref[j], vc_ref[j]], axis=0)
            qs = _heads(q_ref[:, cols], qpk)
            dos = _heads(do_ref[:, cols], qpk)
            p, psink, rowg = _attn_probs(qs, kb, n, h, qpk, sinks_ref)
            pb = p.astype(BF16)
            delta = jnp.sum(dos.astype(F32) * _heads(o_ref[:, cols], qpk).astype(F32), axis=-1, keepdims=True)
            dv = lax.dot_general(pb, dos, (((0,), (0,)), ((), ())), preferred_element_type=F32)
            dp = lax.dot_general(dos, vb, (((1,), (1,)), ((), ())), preferred_element_type=F32)
            dsc = (p * (dp - delta)).astype(BF16)
            dqs.append(_unheads(jnp.dot(dsc, kb, preferred_element_type=F32) * scale, qpk))
            dk = lax.dot_general(dsc, qs, (((0,), (0,)), ((), ())), preferred_element_type=F32) * scale
            dkp_ref[j] = dk[:BLOCK]
            dkc_ref[j] = dk[BLOCK:]
            dvp_ref[j] = dv[:BLOCK]
            dvc_ref[j] = dv[BLOCK:]
            sink_term = psink * delta
            for g in range(qpk):
                val = -jnp.sum(jnp.where(rowg == g, sink_term, 0.0))
                dsink = jnp.where(row0 & (lane == h * qpk + g), val, dsink)
        dq_ref[...] = jnp.concatenate(dqs, axis=1).astype(dq_ref.dtype)
        first = (n == 0) & (hg == 0)

        @pl.when(first)
        def _():
            ds_ref[...] = dsink

        @pl.when(jnp.logical_not(first))
        def _():
            ds_ref[...] += dsink

    qblk = pl.BlockSpec((BLOCK, hp * qw), lambda n, h: (n, h))
    kvblk = pl.BlockSpec((hp, BLOCK, HEAD_DIM), lambda n, h: (h, n, 0))
    pins, pin_specs = _pin(dep)
    return pl.pallas_call(
        _drop_pin(body, 8, dep), name="attn_bwd", grid=(t // BLOCK, nkv // hp),
        in_specs=_attn_specs(qw, hp) + [qblk, qblk] + pin_specs,
        out_specs=[qblk, kvblk, kvblk, kvblk, kvblk, pl.BlockSpec((8, LANES), lambda n, h: (0, 0))],
        out_shape=[_sds((t, aw), BF16)] + [_sds((nkv, t, HEAD_DIM), F32)] * 4 + [_sds((8, LANES), F32)],
        compiler_params=_cparams("arbitrary", "arbitrary"),
    )(q_r, k_r, k_r, v_h, v_h, sinks, o, do, *pins)


HALO = 16


def _shift_down(v, k, halo):
    rows = lax.broadcasted_iota(jnp.int32, v.shape, 0)
    out = pltpu.roll(v, k, 0)
    for r in range(k):
        out = jnp.where(rows == r, halo[HALO - k + r:HALO - k + r + 1, :], out)
    return out


def _shift_up(v, k, halo):
    tm = v.shape[0]
    rows = lax.broadcasted_iota(jnp.int32, v.shape, 0)
    out = pltpu.roll(v, tm - k, 0)
    for r in range(k):
        out = jnp.where(rows == tm - k + r, halo[r:r + 1, :], out)
    return out


def _conv_fwd(proj, conv_w, zoff, cw, cb):
    t = proj.shape[0]
    tm = _pick(t, (512, 256, 128))
    zb, nb = zoff // cb, cw // cb
    hb = tm // HALO

    def body(z_ref, gb_ref, gc_ref, zp_ref, gcp_ref, w_ref, o_ref):
        i = pl.program_id(0)
        cz = gc_ref[...].astype(F32) * z_ref[...].astype(F32)
        czp = gcp_ref[...].astype(F32) * zp_ref[...].astype(F32) * (i > 0).astype(F32)
        w = w_ref[...]
        y = w[0:1] * _shift_down(cz, 2, czp) + w[1:2] * _shift_down(cz, 1, czp) + w[2:3] * cz
        o_ref[...] = (gb_ref[...].astype(F32) * y).astype(o_ref.dtype)

    def col(k):
        return pl.BlockSpec((tm, cb), lambda i, j: (i, zb + k * nb + j))

    def halo(k):
        return pl.BlockSpec((HALO, cb), lambda i, j: (jnp.maximum(i * hb - 1, 0), zb + k * nb + j))

    return pl.pallas_call(
        body, name="conv_fwd", grid=(t // tm, nb),
        in_specs=[col(0), col(1), col(2), halo(0), halo(2), pl.BlockSpec((3, cb), lambda i, j: (0, j))],
        out_specs=pl.BlockSpec((tm, cb), lambda i, j: (i, j)),
        out_shape=_sds((t, cw), BF16),
        compiler_params=_cparams("parallel", "parallel"),
    )(proj, proj, proj, proj, proj, conv_w)


def _conv_bwd(proj, conv_w, dco, zoff, cw, cb, dep=None):
    t = proj.shape[0]
    tm = _pick(t, (512, 256, 128))
    zb, nb = zoff // cb, cw // cb
    hb = tm // HALO
    nt = t // tm

    def body(z_ref, gb_ref, gc_ref, zp_ref, gcp_ref, gbn_ref, w_ref, d_ref, dn_ref, dz_ref, dgb_ref, dgc_ref, dw_ref):
        i = pl.program_id(1)
        z, gb, gc = z_ref[...].astype(F32), gb_ref[...].astype(F32), gc_ref[...].astype(F32)
        d = d_ref[...].astype(F32)
        cz = gc * z
        czp = gcp_ref[...].astype(F32) * zp_ref[...].astype(F32) * (i > 0).astype(F32)
        w = w_ref[...]
        cz1 = _shift_down(cz, 1, czp)
        cz2 = _shift_down(cz, 2, czp)
        y = w[0:1] * cz2 + w[1:2] * cz1 + w[2:3] * cz
        dgb_ref[...] = (d * y).astype(dgb_ref.dtype)
        dy = d * gb
        dyn = dn_ref[...].astype(F32) * gbn_ref[...].astype(F32) * (i < nt - 1).astype(F32)
        dcz = w[2:3] * dy + w[1:2] * _shift_up(dy, 1, dyn) + w[0:1] * _shift_up(dy, 2, dyn)
        dgc_ref[...] = (dcz * z).astype(dgc_ref.dtype)
        dz_ref[...] = (dcz * gc).astype(dz_ref.dtype)
        rows = lax.broadcasted_iota(jnp.int32, (8, cb), 0)
        dw = jnp.zeros((8, cb), F32)
        for r, tap in enumerate((cz2, cz1, cz)):
            dw = jnp.where(rows == r, jnp.sum(dy * tap, axis=0, keepdims=True), dw)

        @pl.when(i == 0)
        def _():
            dw_ref[...] = dw

        @pl.when(i > 0)
        def _():
            dw_ref[...] += dw

    def col(k):
        return pl.BlockSpec((tm, cb), lambda j, i: (i, zb + k * nb + j))

    def halo_prev(k):
        return pl.BlockSpec((HALO, cb), lambda j, i: (jnp.maximum(i * hb - 1, 0), zb + k * nb + j))

    own = pl.BlockSpec((tm, cb), lambda j, i: (i, j))
    nxt = lambda i: jnp.minimum((i + 1) * hb, t // HALO - 1)
    pins, pin_specs = _pin(dep)
    return pl.pallas_call(
        _drop_pin(body, 9, dep), name="conv_bwd", grid=(nb, nt),
        in_specs=[col(0), col(1), col(2), halo_prev(0), halo_prev(2),
                  pl.BlockSpec((HALO, cb), lambda j, i: (nxt(i), zb + nb + j)),
                  pl.BlockSpec((3, cb), lambda j, i: (0, j)), own,
                  pl.BlockSpec((HALO, cb), lambda j, i: (nxt(i), j))] + pin_specs,
        out_specs=[own, own, own, pl.BlockSpec((8, cb), lambda j, i: (0, j))],
        out_shape=[_sds((t, cw), BF16)] * 3 + [_sds((8, cw), F32)],
        compiler_params=_cparams("parallel", "arbitrary"),
    )(proj, proj, proj, proj, proj, proj, conv_w, dco, dco, *pins)


def _yconv_merge(conv_o, w_cp, ya, proj, goff, cb, dep, name):
    t, k = conv_o.shape
    d = w_cp.shape[1]
    tm = _pick(t, (1024, 512, 256, 128))
    gb_, nb = goff // cb, d // cb
    deps = [] if dep is None else [dep]

    def body(a_ref, w_ref, ya_ref, ga_ref, gc_ref, *rest):
        yc_ref, o_ref = rest[len(deps):]
        yc = jnp.dot(a_ref[...], w_ref[...], preferred_element_type=F32)
        f = lambda r: r[...].astype(F32)
        yc_ref[...] = yc.astype(yc_ref.dtype)
        o_ref[...] = (_sigmoid(f(ga_ref)) * f(ya_ref) + _sigmoid(f(gc_ref)) * yc).astype(o_ref.dtype)

    own = pl.BlockSpec((tm, cb), lambda i, j: (i, j))
    return pl.pallas_call(
        body, name=name, grid=(t // tm, nb),
        in_specs=[pl.BlockSpec((tm, k), lambda i, j: (i, 0)), pl.BlockSpec((k, cb), lambda i, j: (0, j)), own,
                  pl.BlockSpec((tm, cb), lambda i, j: (i, gb_ + j)),
                  pl.BlockSpec((tm, cb), lambda i, j: (i, gb_ + nb + j))] + [ANY] * len(deps),
        out_specs=[own, own], out_shape=[_sds((t, d), BF16)] * 2,
        compiler_params=_cparams("parallel", "parallel"),
    )(conv_o, w_cp, ya, proj, proj, *deps)


def _dmerged_split(dh, w_mo, proj, ya, yc, goff, cb, dep, name):
    t, k = dh.shape
    d = w_mo.shape[0]
    tm = _pick(t, (1024, 512, 256, 128))
    gb_, nb = goff // cb, d // cb
    deps = [] if dep is None else [dep]

    def body(a_ref, w_ref, ga_ref, gc_ref, ya_ref, yc_ref, *rest):
        dya_ref, dyc_ref, dga_ref, dgc_ref = rest[len(deps):]
        dmv = lax.dot_general(a_ref[...], w_ref[...], (((1,), (1,)), ((), ())), preferred_element_type=F32)
        sa = _sigmoid(ga_ref[...].astype(F32))
        sc = _sigmoid(gc_ref[...].astype(F32))
        dya_ref[...] = (dmv * sa).astype(BF16)
        dyc_ref[...] = (dmv * sc).astype(BF16)
        dga_ref[...] = (dmv * ya_ref[...].astype(F32) * sa * (1.0 - sa)).astype(BF16)
        dgc_ref[...] = (dmv * yc_ref[...].astype(F32) * sc * (1.0 - sc)).astype(BF16)

    own = pl.BlockSpec((tm, cb), lambda i, j: (i, j))
    return pl.pallas_call(
        body, name=name, grid=(t // tm, nb),
        in_specs=[pl.BlockSpec((tm, k), lambda i, j: (i, 0)), pl.BlockSpec((cb, k), lambda i, j: (j, 0)),
                  pl.BlockSpec((tm, cb), lambda i, j: (i, gb_ + j)),
                  pl.BlockSpec((tm, cb), lambda i, j: (i, gb_ + nb + j)), own, own] + [ANY] * len(deps),
        out_specs=[own] * 4, out_shape=[_sds((t, d), BF16)] * 4,
        compiler_params=_cparams("parallel", "parallel"),
    )(dh, w_mo, proj, proj, ya, yc, *deps)


def _assemble_dproj(dq, dkc, dkp, dvc, dvp, cos, sin, dz, dgb, dgc, dga, dgg, dep=None):
    t, aw = dq.shape
    nkv, cw, d = dkc.shape[0], dz.shape[1], dga.shape[1]
    kvw = nkv * HEAD_DIM
    nblk = t // BLOCK
    width = aw + 2 * kvw + 3 * cw + 2 * d

    def body(dq_ref, dkc_ref, dkp_ref, dvc_ref, dvp_ref, c_ref, s_ref, dz_ref, dgb_ref, dgc_ref, dga_ref, dgg_ref,
             o_ref):
        keep = (pl.program_id(0) < nblk - 1).astype(F32)
        c, s = c_ref[...], s_ref[...]
        dk = jnp.concatenate([dkc_ref[h] + dkp_ref[h] * keep for h in range(nkv)], axis=1)
        dv = jnp.concatenate([dvc_ref[h] + dvp_ref[h] * keep for h in range(nkv)], axis=1)
        o_ref[...] = jnp.concatenate(
            [_rope(dq_ref[...].astype(F32), c, -s).astype(BF16), _rope(dk, c, -s).astype(BF16), dv.astype(BF16),
             dz_ref[...], dgb_ref[...], dgc_ref[...], dga_ref[...], dgg_ref[...]], axis=1)

    def cur(w):
        return pl.BlockSpec((BLOCK, w), lambda n: (n, 0))

    head_cur = pl.BlockSpec((nkv, BLOCK, HEAD_DIM), lambda n: (0, n, 0))
    head_nxt = pl.BlockSpec((nkv, BLOCK, HEAD_DIM), lambda n: (0, jnp.minimum(n + 1, nblk - 1), 0))
    pins, pin_specs = _pin(dep)
    return pl.pallas_call(
        _drop_pin(body, 12, dep), name="assemble_dproj", grid=(nblk,),
        in_specs=[cur(aw), head_cur, head_nxt, head_cur, head_nxt, cur(LANES), cur(LANES),
                  cur(cw), cur(cw), cur(cw), cur(d), cur(d)] + pin_specs,
        out_specs=cur(width), out_shape=_sds((t, width), BF16),
        compiler_params=_cparams("parallel"),
    )(dq, dkc, dkp, dvc, dvp, cos, sin, dz, dgb, dgc, dga, dgg, *pins)


def _xattn_probs(qh, kh):
    s = lax.dot_general(qh, kh, (((1,), (1,)), ((), ())), preferred_element_type=F32) * (X_HEAD_DIM ** -0.5)
    e = jnp.exp(s - jnp.max(s, axis=-1, keepdims=True))
    return e * (1.0 / jnp.sum(e, axis=-1, keepdims=True))


def _xattn_fwd(xq, kv):
    t, xw = xq.shape
    mt = kv.shape[0]
    tm = _pick(t, (512, 256, 128))

    def body(q_ref, kv_ref, o_ref):
        outs = []
        for hd in range(xw // X_HEAD_DIM):
            hs = slice(hd * X_HEAD_DIM, (hd + 1) * X_HEAD_DIM)
            vs = slice(xw + hd * X_HEAD_DIM, xw + (hd + 1) * X_HEAD_DIM)
            p = _xattn_probs(q_ref[:, hs], kv_ref[:, hs])
            outs.append(jnp.dot(p.astype(BF16), kv_ref[:, vs], preferred_element_type=F32))
        o_ref[...] = jnp.concatenate(outs, axis=1).astype(o_ref.dtype)

    return pl.pallas_call(
        body, name="xattn_fwd", grid=(t // tm,),
        in_specs=[pl.BlockSpec((tm, xw), lambda i: (i, 0)), pl.BlockSpec((mt, 2 * xw), lambda i: (0, 0))],
        out_specs=pl.BlockSpec((tm, xw), lambda i: (i, 0)), out_shape=_sds((t, xw), BF16),
        compiler_params=_cparams("parallel"),
    )(xq, kv)


def _xattn_bwd(xq, kv, do):
    t, xw = xq.shape
    mt = kv.shape[0]
    tm = _pick(t, (512, 256, 128))
    scale = X_HEAD_DIM ** -0.5

    def body(q_ref, kv_ref, do_ref, dq_ref, dkv_ref):
        dqs, dks, dvs = [], [], []
        for hd in range(xw // X_HEAD_DIM):
            hs = slice(hd * X_HEAD_DIM, (hd + 1) * X_HEAD_DIM)
            vs = slice(xw + hd * X_HEAD_DIM, xw + (hd + 1) * X_HEAD_DIM)
            qh, kh, vh, doh = q_ref[:, hs], kv_ref[:, hs], kv_ref[:, vs], do_ref[:, hs]
            p = _xattn_probs(qh, kh)
            pb = p.astype(BF16)
            o = jnp.dot(pb, vh, preferred_element_type=F32)
            delta = jnp.sum(doh.astype(F32) * o, axis=-1, keepdims=True)
            dvs.append(lax.dot_general(pb, doh, (((0,), (0,)), ((), ())), preferred_element_type=F32))
            dp = lax.dot_general(doh, vh, (((1,), (1,)), ((), ())), preferred_element_type=F32)
            dsc = (p * (dp - delta)).astype(BF16)
            dqs.append(jnp.dot(dsc, kh, preferred_element_type=F32) * scale)
            dks.append(lax.dot_general(dsc, qh, (((0,), (0,)), ((), ())), preferred_element_type=F32) * scale)
        dq_ref[...] = jnp.concatenate(dqs, axis=1).astype(dq_ref.dtype)
        dkv = jnp.concatenate(dks + dvs, axis=1)

        @pl.when(pl.program_id(0) == 0)
        def _():
            dkv_ref[...] = dkv

        @pl.when(pl.program_id(0) > 0)
        def _():
            dkv_ref[...] += dkv

    row = pl.BlockSpec((tm, xw), lambda i: (i, 0))
    whole = pl.BlockSpec((mt, 2 * xw), lambda i: (0, 0))
    return pl.pallas_call(
        body, name="xattn_bwd", grid=(t // tm,),
        in_specs=[row, whole, row], out_specs=[row, whole],
        out_shape=[_sds((t, xw), BF16), _sds((mt, 2 * xw), F32)],
        compiler_params=_cparams("arbitrary"),
    )(xq, kv, do)


def _ffn_in_swiglu(u, w, dep, name):
    t, d = u.shape
    f = w.shape[1] // 2
    tm, tn, _ = _tiles("nn", t, f)
    nf = f // tn
    deps = [] if dep is None else [dep]

    def body(u_ref, wa_ref, wb_ref, *rest):
        h_ref, o_ref = rest[len(deps):]
        a = jnp.dot(u_ref[...], wa_ref[...], preferred_element_type=F32)
        b = jnp.dot(u_ref[...], wb_ref[...], preferred_element_type=F32)
        h_ref[0] = a.astype(h_ref.dtype)
        h_ref[1] = b.astype(h_ref.dtype)
        o_ref[...] = (a * _sigmoid(a) * b).astype(o_ref.dtype)

    return pl.pallas_call(
        body, name=name, grid=(t // tm, nf),
        in_specs=[pl.BlockSpec((tm, d), lambda i, j: (i, 0)), pl.BlockSpec((d, tn), lambda i, j: (0, j)),
                  pl.BlockSpec((d, tn), lambda i, j: (0, nf + j))] + [ANY] * len(deps),
        out_specs=[pl.BlockSpec((2, tm, tn), lambda i, j: (0, i, j)), pl.BlockSpec((tm, tn), lambda i, j: (i, j))],
        out_shape=[_sds((2, t, f), BF16), _sds((t, f), BF16)],
        compiler_params=_cparams("parallel", "parallel"),
    )(u, w, w, *deps)


def _dact_swiglu(dh, w_out, hid, dep, name):
    t, d = dh.shape
    f = w_out.shape[0]
    tm, tn, _ = _tiles("nt", t, f)
    deps = [] if dep is None else [dep]

    def body(dh_ref, w_ref, h_ref, *rest):
        o_ref = rest[len(deps)]
        g = lax.dot_general(dh_ref[...], w_ref[...], (((1,), (1,)), ((), ())), preferred_element_type=F32)
        a = h_ref[0].astype(F32)
        b = h_ref[1].astype(F32)
        sg = _sigmoid(a)
        o_ref[0] = (g * b * sg * (1.0 + a * (1.0 - sg))).astype(o_ref.dtype)
        o_ref[1] = (g * a * sg).astype(o_ref.dtype)

    pair = pl.BlockSpec((2, tm, tn), lambda i, j: (0, i, j))
    return pl.pallas_call(
        body, name=name, grid=(t // tm, f // tn),
        in_specs=[pl.BlockSpec((tm, d), lambda i, j: (i, 0)), pl.BlockSpec((tn, d), lambda i, j: (j, 0)), pair]
        + [ANY] * len(deps),
        out_specs=pair, out_shape=_sds((2, t, f), BF16),
        compiler_params=_cparams("parallel", "parallel"),
    )(dh, w_out, hid, *deps)


def _adamw(w, g, m, v, name):
    r, c = w.shape
    tr = _pick(r, (256, 128, 64, 32, 16, 8)) if r * c > 65536 else r

    def body(w_ref, g_ref, m_ref, v_ref, d_ref, nm_ref, nv_ref, go_ref):
        gv = g_ref[...]
        go_ref[...] = gv
        m2 = ADAM_B1 * m_ref[...] + (1.0 - ADAM_B1) * gv
        v2 = ADAM_B2 * v_ref[...] + (1.0 - ADAM_B2) * (gv * gv)
        m_hat = m2 / (1.0 - ADAM_B1 ** ADAM_STEP)
        v_hat = v2 / (1.0 - ADAM_B2 ** ADAM_STEP)
        d_ref[...] = -ADAM_LR * (m_hat / (jnp.sqrt(v_hat) + ADAM_EPS) + ADAM_WD * w_ref[...])
        nm_ref[...] = m2
        nv_ref[...] = v2

    blk = pl.BlockSpec((tr, c), lambda i: (i, 0))
    return pl.pallas_call(
        body, name=name, grid=(r // tr,),
        in_specs=[blk] * 4, out_specs=[blk] * 4, out_shape=[_sds((r, c), F32)] * 4,
        compiler_params=_cparams("parallel"),
    )(w, g, m, v)


class _Weights:
    def __init__(self, full):
        self.full = full
        self.grads = {}

    def get(self, name):
        return self.full[name]

    def mark(self, tag, value):
        return value

    def grad(self, name, g):
        self.grads[name] = g

    def dep(self):
        return None


def _local_step(x, mem, tgt, g_mix, sinks, g_xattn, g_mem, g_ffn, g_final, dims, wts):
    t, d = x.shape
    aw, cw, kvw = dims
    cb = 2 * kvw
    zoff = aw + 2 * kvw
    goff = zoff + 3 * cw
    cos, sin = _rope_tables(t)
    mark, get = wts.mark, wts.get

    def mm(a, b, **kw):
        return _matmul(a, b, dep=wts.dep(), **kw)

    u1 = mark("u1", _rms_fwd(x, g_mix, "rms_mix"))
    qkv = mm(u1, get("w_in"), mode="nn", out_dtype=F32, name="mm_qkv", cols=(0, zoff))
    proj = mark("proj", mm(u1, get("w_in"), mode="nn", out_dtype=BF16, name="mm_gates", cols=(zoff, goff - zoff + 2 * d)))
    zoff, goff = 0, goff - zoff
    q_r, k_r, v_h = _rope_qkv(qkv, cos, sin, aw, kvw)
    attn_o = mark("attn_o", _attn_fwd(q_r, k_r, v_h, sinks))
    conv_o = mark("conv_o", _conv_fwd(proj, get("conv_w"), zoff, cw, cb))
    ya = mark("ya", mm(attn_o, get("w_attn_proj"), mode="nn", out_dtype=BF16, name="mm_yattn"))
    yc, merged = _yconv_merge(conv_o, get("w_conv_proj"), ya, proj, goff, cb, wts.dep(), "mm_yconv")
    mark("merged", merged)
    norm_outs = [("row", F32), ("row", BF16)]
    bwd_outs = [("row", F32), ("row", BF16), ("vec",)]
    h1, u2 = _rows_matmul(merged, get("w_mix_out"), mode="nn", kslice=(0, 1), res=x, rows_in=[], vec=g_xattn,
                          epilogue=_ep_norm, outs=norm_outs, name="mm_mix", dep=wts.dep())
    mark("h1", h1)
    mem_n = _rms_fwd(mem, g_mem, "rms_mem")
    xq = mark("xq", mm(u2, get("w_xq"), mode="nn", out_dtype=BF16, name="mm_xq"))
    kv = mm(mem_n, get("w_xkv"), mode="nn", out_dtype=BF16, name="mm_xkv")
    xo = mark("xo", _xattn_fwd(xq, kv))
    h2, u3 = _rows_matmul(xo, get("w_xo"), mode="nn", kslice=(0, 1), res=h1, rows_in=[], vec=g_ffn,
                          epilogue=_ep_norm, outs=norm_outs, name="mm_xo", dep=wts.dep())
    mark("h2", h2)
    mark("u3", u3)
    hid, act = _ffn_in_swiglu(u3, get("w_ffn_in"), wts.dep(), "mm_ffn_in")
    mark("hid", hid)
    dh3, dh3b, dg_final, loss = _rows_matmul(
        act, get("w_ffn_out"), mode="nn", kslice=(0, 1), res=h2, rows_in=[tgt], vec=g_final, epilogue=_ep_loss,
        outs=bwd_outs + [("lane",)], name="mm_ffn_out", dep=wts.dep())
    mark("dh3", dh3b)

    wts.grad("w_ffn_out", mm(act, dh3b, mode="tn", out_dtype=BF16, name="mm_dw_ffn_out"))
    dhid = mark("dhid", _dact_swiglu(dh3b, get("w_ffn_out"), hid, wts.dep(), "mm_dact"))
    wts.grad("w_ffn_in", mm(u3, dhid, mode="tn", out_dtype=BF16, name="mm_dw_ffn_in"))
    du3 = mark("du3", _matmul_slice(dhid, get("w_ffn_in"), mode="nt", out_dtype=F32, name="mm_du3_k0", res=None,
                                    dep=wts.dep(), kslice=(0, 2), cols=(0, d)))
    dh2, dh2b, dg_ffn = _rows_matmul(dhid, get("w_ffn_in"), mode="nt", kslice=(1, 2), res=du3, rows_in=[h2, dh3],
                                     vec=g_ffn, epilogue=_ep_rms_bwd, outs=bwd_outs, name="mm_du3_k1", dep=wts.dep())
    mark("dh2", dh2b)
    wts.grad("w_xo", mm(xo, dh2b, mode="tn", out_dtype=BF16, name="mm_dw_xo"))
    dxo = mm(dh2b, get("w_xo"), mode="nt", out_dtype=BF16, name="mm_dxo")
    dxq, dkv = _xattn_bwd(xq, kv, dxo)
    dkvb = dkv.astype(BF16)
    wts.grad("w_xq", mm(u2, dxq, mode="tn", out_dtype=BF16, name="mm_dw_xq"))
    wts.grad("w_xkv", mm(mem_n, dkvb, mode="tn", out_dtype=BF16, name="mm_dw_xkv"))
    dmem_n = mm(dkvb, get("w_xkv"), mode="nt", out_dtype=F32, name="mm_dmem")
    _, _, dg_mem = _rms_bwd(mem, g_mem, dmem_n, None, "rms_bwd_mem")
    dh1, dh1b, dg_xattn = _rows_matmul(dxq, get("w_xq"), mode="nt", kslice=(0, 1), res=None, rows_in=[h1, dh2],
                                       vec=g_xattn, epilogue=_ep_rms_bwd, outs=bwd_outs, name="mm_du2", dep=wts.dep())
    mark("dh1", dh1b)
    wts.grad("w_mix_out", mm(merged, dh1b, mode="tn", out_dtype=BF16, name="mm_dw_mix"))
    dya, dyc, dga, dgg = _dmerged_split(dh1b, get("w_mix_out"), proj, ya, yc, goff, cb, wts.dep(), "mm_dmerged")
    mark("dya", dya)
    wts.grad("w_attn_proj", mm(attn_o, dya, mode="tn", out_dtype=BF16, name="mm_dw_attn_proj"))
    dattn_o = mm(dya, get("w_attn_proj"), mode="nt", out_dtype=BF16, name="mm_dattn")
    wts.grad("w_conv_proj", mm(conv_o, dyc, mode="tn", out_dtype=BF16, name="mm_dw_conv_proj"))
    dconv_o = mark("dconv_o", mm(dyc, get("w_conv_proj"), mode="nt", out_dtype=BF16, name="mm_dconv"))
    dz, dgb, dgc, dconv_w = _conv_bwd(proj, get("conv_w"), dconv_o, zoff, cw, cb, wts.dep())
    mark("dz", dz)
    dq, dkc, dkp, dvc, dvp, dsinks = _attn_bwd(q_r, k_r, v_h, sinks, attn_o, dattn_o, wts.dep())
    mark("dq", dq)
    dproj = mark("dproj", _assemble_dproj(dq, dkc, dkp, dvc, dvp, cos, sin, dz, dgb, dgc, dga, dgg, wts.dep()))
    wts.grad("w_in", mm(u1, dproj, mode="tn", out_dtype=BF16, name="mm_dw_in"))
    du1 = mark("du1", _matmul_slice(dproj, get("w_in"), mode="nt", out_dtype=F32, name="mm_du1_k0", res=None,
                                    dep=wts.dep(), kslice=(0, 2), cols=(0, d)))
    grad_x, _, dg_mix = _rows_matmul(dproj, get("w_in"), mode="nt", kslice=(1, 2), res=du1, rows_in=[x, dh1],
                                     vec=g_mix, epilogue=_ep_rms_bwd, outs=bwd_outs, name="mm_du1_k1", dep=wts.dep())
    mark("grad_x", grad_x)

    small = dict(g_mix=dg_mix, g_xattn=dg_xattn, g_mem=dg_mem, g_ffn=dg_ffn, g_final=dg_final,
                 conv_w=dconv_w[:3], attn_sinks=dsinks[0:1, :sinks.shape[1]], loss=loss[0:1, 0:1])
    return grad_x, small


BIG = (("w_in", 1), ("w_attn_proj", 1), ("w_conv_proj", 1), ("w_mix_out", 0), ("w_xq", 0), ("w_xkv", 0),
       ("w_xo", 1), ("w_ffn_in", 1), ("w_ffn_out", 0))


def _place():
    x, y, c = lax.axis_index("x"), lax.axis_index("y"), lax.axis_index("c")
    chips = [(1 - x, y), (x, 1 - y), (1 - x, 1 - y)]
    return x, y, c, chips


def _window(ref, ax, shard_shape, s, h):
    sr, sc = shard_shape
    hr = sr // 2
    if ax == 1:
        return ref.at[pl.ds(pl.multiple_of(h * hr, 16), hr), pl.ds(pl.multiple_of(s * sc, LANES), sc)]
    return ref.at[pl.ds(pl.multiple_of(s * sr + h * hr, 16), hr), :]


def _half(ref, h):
    hr = ref.shape[0] // 2
    return ref.at[pl.ds(pl.multiple_of(h * hr, 16), hr), :]


def _remote(src, dst, send_sem, recv_sem, dev):
    return pltpu.make_async_remote_copy(src_ref=src, dst_ref=dst, send_sem=send_sem, recv_sem=recv_sem,
                                        device_id=dev, device_id_type=MESH)


def _cast_to_full(shard, ax, me, name, dtype=BF16):
    sr, sc = shard.shape
    tr = _pick(sr, (256, 352, 128, 64, 32, 16))
    nr = sr // tr
    full = (sr * N_CHIPS, sc) if ax == 0 else (sr, sc * N_CHIPS)

    def body(me_ref, s_ref, o_ref):
        o_ref[...] = s_ref[...].astype(o_ref.dtype)

    if ax == 1:
        out_spec = pl.BlockSpec((tr, sc), lambda r, me_ref: (r, me_ref[0]))
    else:
        out_spec = pl.BlockSpec((tr, sc), lambda r, me_ref: (me_ref[0] * nr + r, 0))
    return pl.pallas_call(
        body, name=name,
        grid_spec=pltpu.PrefetchScalarGridSpec(
            num_scalar_prefetch=1, grid=(nr,), in_specs=[pl.BlockSpec((tr, sc), lambda r, me_ref: (r, 0))],
            out_specs=out_spec),
        out_shape=_sds(full, dtype),
        compiler_params=_cparams("parallel"),
    )(me, shard)


HBM = pl.BlockSpec(memory_space=pltpu.HBM)
SEM = pl.BlockSpec(memory_space=pltpu.SEMAPHORE)
EFFECT = pltpu.SideEffectType.DATAFLOW_SIDE_EFFECTING


def _in_hbm(a):
    return pltpu.with_memory_space_constraint(a, pltpu.HBM)


def _gather_window(ref, ax, shard_shape, s, h):
    if h is not None:
        return _window(ref, ax, shard_shape, s, h)
    sr, sc = shard_shape
    if ax == 1:
        return ref.at[:, pl.ds(pl.multiple_of(s * sc, LANES), sc)]
    return ref.at[pl.ds(pl.multiple_of(s * sr, 8), sr), :]


def _ag_start(fulls, axes, shard_shapes, whole):
    n = len(fulls)

    def body(*refs):
        src = refs[:n]
        send_sems, recv_sems = refs[n], refs[n + 1]
        token = refs[2 * n + 2]
        x, y, c, chips = _place()
        me = 2 * x + y
        for i in range(n):
            h = None if whole[i] else c
            for j, chip in enumerate(chips):
                blk = _gather_window(src[i], axes[i], shard_shapes[i], me, h)
                _remote(blk, blk, send_sems.at[3 * i + j], recv_sems.at[3 * i + j], (*chip, c)).start()
        token[...] = jnp.zeros_like(token)

    res = pl.pallas_call(
        body, name="ag_start_" + str(n),
        out_shape=(pltpu.SemaphoreType.DMA((3 * n,)), pltpu.SemaphoreType.DMA((3 * n,)),
                   *[pltpu.HBM(f.shape, f.dtype) for f in fulls], _sds((8, LANES), F32)),
        in_specs=[HBM] * n, out_specs=(SEM, SEM, *[HBM] * n, pl.BlockSpec(memory_space=pltpu.VMEM)),
        input_output_aliases={i: 2 + i for i in range(n)},
        compiler_params=pltpu.CompilerParams(has_side_effects=EFFECT),
    )(*[_in_hbm(f) for f in fulls])
    return res[0], res[1], list(res[2:2 + n]), res[2 + n]


def _ag_mid(bufs, slots, axes, shard_shapes, whole, send_sems, recv_sems, after, name):
    ng = len(bufs)

    def body(*refs):
        src = refs[:ng]
        s_in, r_in = refs[ng], refs[ng + 1]
        fsend, frecv = refs[ng + 3], refs[ng + 4]
        x, y, c, chips = _place()
        me = 2 * x + y
        sib = (x, y, 1 - c)
        for k, i in enumerate(slots):
            h = None if whole[k] else c
            for j, chip in enumerate(chips):
                cj = 2 * chip[0] + chip[1]
                mine = _gather_window(src[k], axes[k], shard_shapes[k], me, h)
                theirs = _gather_window(src[k], axes[k], shard_shapes[k], cj, h)
                _remote(theirs, theirs, s_in.at[3 * i + j], r_in.at[3 * i + j], (*chip, c)).wait_recv()
                _remote(mine, mine, s_in.at[3 * i + j], r_in.at[3 * i + j], (*chip, c)).wait_send()
                if not whole[k]:
                    _remote(theirs, theirs, fsend.at[3 * k + j], frecv.at[3 * k + j], sib).start()
        token = refs[2 * ng + 5]
        token[...] = jnp.zeros_like(token)

    res = pl.pallas_call(
        body, name=name,
        out_shape=(pltpu.SemaphoreType.DMA((3 * ng,)), pltpu.SemaphoreType.DMA((3 * ng,)),
                   *[pltpu.HBM(b.shape, b.dtype) for b in bufs], _sds((8, LANES), F32)),
        in_specs=[HBM] * ng + [SEM, SEM, ANY],
        out_specs=(SEM, SEM, *[HBM] * ng, pl.BlockSpec(memory_space=pltpu.VMEM)),
        input_output_aliases={k: 2 + k for k in range(ng)},
        compiler_params=pltpu.CompilerParams(has_side_effects=EFFECT),
    )(*bufs, send_sems, recv_sems, after)
    return res[0], res[1], list(res[2:2 + ng]), res[2 + ng]


def _ag_wait(bufs, axes, shard_shapes, whole, fsend, frecv, after, name):
    ng = len(bufs)

    def body(*refs):
        src = refs[:ng]
        s_in, r_in = refs[ng], refs[ng + 1]
        x, y, c, chips = _place()
        sib = (x, y, 1 - c)
        for k in range(ng):
            if whole[k]:
                continue
            for j, chip in enumerate(chips):
                cj = 2 * chip[0] + chip[1]
                sent = _gather_window(src[k], axes[k], shard_shapes[k], cj, c)
                landed = _gather_window(src[k], axes[k], shard_shapes[k], cj, 1 - c)
                _remote(landed, landed, s_in.at[3 * k + j], r_in.at[3 * k + j], sib).wait_recv()
                _remote(sent, sent, s_in.at[3 * k + j], r_in.at[3 * k + j], sib).wait_send()

    res = pl.pallas_call(
        body, name=name,
        out_shape=tuple(pltpu.HBM(b.shape, b.dtype) for b in bufs),
        in_specs=[HBM] * ng + [SEM, SEM, ANY], out_specs=tuple([HBM] * ng),
        input_output_aliases={k: k for k in range(ng)},
        compiler_params=pltpu.CompilerParams(has_side_effects=EFFECT),
    )(*bufs, fsend, frecv, after)
    return list(res)


class _Schedule:
    GROUPS = ((("w_in", "conv_w"), "u1", "u1"),
              (("w_attn_proj", "w_conv_proj", "w_mix_out", "w_xq", "w_xkv", "w_xo"), "proj", "conv_o"),
              (("w_ffn_in",), "h1", "h2"),
              (("w_ffn_out",), "u3", "hid"))
    STARTS = ((0,), (1, 2, 3))
    REDUCE = ((("w_ffn_out",), "dhid", "grad:w_ffn_in", "du3"),
              (("w_ffn_in",), "grad:w_ffn_in", "dya", "grad_x"),
              (("w_xo", "w_xq", "w_xkv", "w_mix_out", "w_attn_proj", "w_conv_proj"), "dconv_o", "dq", "grad:w_in"),
              (("w_in",), "grad:w_in", "end", "end2"))

    def __init__(self, seed, axes, shard_shapes, place, on_ready):
        self.ax, self.shape, self.place, self.on_ready = axes, shard_shapes, place, on_ready
        self.stage, self.buf, self.slot, self.passes = {}, {}, {}, {}
        self.ready = set()
        self.grads = {}
        token = None
        for groups in self.STARTS:
            order = [nm for g in groups for nm in self.GROUPS[g][0]]
            send, recv, bufs, token = _ag_start([seed(nm, token) for nm in order], *self._meta(order))
            self.buf.update(zip(order, bufs))
            self.slot.update({nm: (send, recv, k) for k, nm in enumerate(order)})
        self.token = self.latest = token
        self.mark("start", token)

    def _meta(self, names):
        return ([self.ax[nm] for nm in names], [self.shape[nm] for nm in names], [nm == "conv_w" for nm in names])

    def mark(self, tag, value):
        marked = value
        for g, (names, mid, wait) in enumerate(self.GROUPS):
            if tag == mid:
                send, recv, _ = self.slot[names[0]]
                fs, fr, bufs, self.latest = _ag_mid([self.buf[nm] for nm in names], [self.slot[nm][2] for nm in names],
                                                    *self._meta(names), send, recv, value, "ag_mid_%d" % g)
                self.buf.update(zip(names, bufs))
                self.passes[g] = (fs, fr)
            if tag == wait:
                fs, fr = self.passes[g]
                bufs = _ag_wait([self.buf[nm] for nm in names], *self._meta(names), fs, fr, value, "ag_wait_%d" % g)
                self.buf.update(zip(names, bufs))
                self.ready.update(names)
        for g, (names, send, total, finish) in enumerate(self.REDUCE):
            st = self.stage.get(g)
            if st is None:
                continue
            ng = len(names)
            if tag == send and st["at"] == "pair":
                arrs = _exchange_wait("rs_pair_wait_%d" % g, st["arrs"], *st["sems"], st["plan"], value)
                parts = [_pair_add(arrs[k], arrs[ng + k], self.ax[nm], self.shape[nm], self.place, "pair_add_" + nm)
                         for k, nm in enumerate(names)]
                plan, nsem = _plan_chip(ng)
                ss, rs, arrs, self.latest = _exchange_start(
                    "rs_chip_start_%d" % g, parts + [lax.empty(p.shape, p.dtype) for p in parts], nsem, plan)
                self.stage[g] = dict(at="chip", arrs=arrs, sems=(ss, rs), plan=plan)
            elif tag == total and st["at"] == "chip":
                arrs = _exchange_wait("rs_chip_wait_%d" % g, st["arrs"], *st["sems"], st["plan"], value)
                halves = [_chip_add(arrs[k], arrs[ng + k], self.place, "chip_add_" + nm) for k, nm in enumerate(names)]
                plan, nsem = _plan_gather(ng)
                ss, rs, arrs, self.latest = _exchange_start("rs_gather_start_%d" % g, halves, nsem, plan)
                self.stage[g] = dict(at="gather", arrs=arrs, sems=(ss, rs), plan=plan)
            st = self.stage[g]
            if tag == finish and st["at"] == "gather":
                arrs = _exchange_wait("rs_gather_wait_%d" % g, st["arrs"], *st["sems"], st["plan"], value)
                self.stage[g] = dict(at="done")
                for nm, shard in zip(names, arrs):
                    value = self.on_ready(nm, shard)
        return marked

    def get(self, name):
        assert name in self.ready, name
        return self.buf[name]

    def grad(self, name, g):
        self.grads[name] = g
        for gi, (names, _, _, _) in enumerate(self.REDUCE):
            if name == names[-1]:
                gs = [self.grads[nm] for nm in names]
                plan, nsem = _plan_pair(len(names), [self.ax[nm] for nm in names], [self.shape[nm] for nm in names])
                ss, rs, arrs, self.latest = _exchange_start(
                    "rs_pair_start_%d" % gi, gs + [lax.empty(a.shape, a.dtype) for a in gs], nsem, plan)
                self.stage[gi] = dict(at="pair", arrs=arrs, sems=(ss, rs), plan=plan)
                g = self.latest
        self.mark("grad:" + name, g)

    def dep(self):
        return self.latest


def _exchange_start(name, arrays, nsem, plan):
    n = len(arrays)

    def body(*refs):
        send_sems, recv_sems, token = refs[n], refs[n + 1], refs[2 * n + 2]
        sends, _ = plan(refs[:n])
        for k, (src, dst, dev) in enumerate(sends):
            _remote(src, dst, send_sems.at[k], recv_sems.at[k], dev).start()
        token[...] = jnp.zeros_like(token)

    res = pl.pallas_call(
        body, name=name,
        out_shape=(pltpu.SemaphoreType.DMA((nsem,)), pltpu.SemaphoreType.DMA((nsem,)),
                   *[pltpu.HBM(a.shape, a.dtype) for a in arrays], _sds((8, LANES), F32)),
        in_specs=[HBM] * n, out_specs=(SEM, SEM, *[HBM] * n, pl.BlockSpec(memory_space=pltpu.VMEM)),
        input_output_aliases={i: 2 + i for i in range(n)},
        compiler_params=pltpu.CompilerParams(has_side_effects=EFFECT),
    )(*[_in_hbm(a) for a in arrays])
    return res[0], res[1], list(res[2:2 + n]), res[2 + n]


def _exchange_wait(name, arrays, send_sems, recv_sems, plan, after):
    n = len(arrays)

    def body(*refs):
        s_in, r_in = refs[n], refs[n + 1]
        sends, recvs = plan(refs[:n])
        for k, land in enumerate(recvs):
            _remote(land, land, s_in.at[k], r_in.at[k], sends[k][2]).wait_recv()
        for k, (src, _, dev) in enumerate(sends):
            _remote(src, src, s_in.at[k], r_in.at[k], dev).wait_send()

    res = pl.pallas_call(
        body, name=name,
        out_shape=tuple(pltpu.HBM(a.shape, a.dtype) for a in arrays),
        in_specs=[HBM] * n + [SEM, SEM, ANY], out_specs=tuple([HBM] * n),
        input_output_aliases={i: i for i in range(n)},
        compiler_params=pltpu.CompilerParams(has_side_effects=EFFECT),
    )(*arrays, send_sems, recv_sems, after)
    return list(res)


def _plan_pair(n, axes, shard_shapes):
    def plan(refs):
        g, ra = refs[:n], refs[n:]
        x, y, c, _ = _place()
        sib = (x, y, 1 - c)

        def pieces(ref, i, h):
            if axes[i] == 1:
                return [_half(ref, h)]
            return [_window(ref, 0, shard_shapes[i], s, h) for s in range(N_CHIPS)]

        sends, recvs = [], []
        for i in range(n):
            sends += [(src, dst, sib) for src, dst in zip(pieces(g[i], i, 1 - c), pieces(ra[i], i, 1 - c))]
            recvs += pieces(ra[i], i, c)
        return sends, recvs

    return plan, sum(1 if ax == 1 else N_CHIPS for ax in axes)


def _plan_chip(n):
    def plan(refs):
        p, rc = refs[:n], refs[n:]
        x, y, c, chips = _place()
        me = 2 * x + y
        sends, recvs = [], []
        for i in range(n):
            for chip in chips:
                cj = 2 * chip[0] + chip[1]
                sends.append((p[i].at[cj], rc[i].at[me], (*chip, c)))
                recvs.append(rc[i].at[cj])
        return sends, recvs

    return plan, 3 * n


def _plan_gather(n):
    def plan(refs):
        x, y, c, _ = _place()
        sib = (x, y, 1 - c)
        return ([(_half(r, c), _half(r, c), sib) for r in refs], [_half(r, 1 - c) for r in refs])

    return plan, n


def _pair_add(g, ra, ax, shard_shape, place, name):
    sr, sc = shard_shape
    hr = sr // 2
    wc = sc
    tr = _pick(hr, (256, 352, 128, 64, 32, 16))
    nr = hr // tr

    def body(p_ref, a_ref, b_ref, o_ref):
        o_ref[...] = (a_ref[...].astype(F32) + b_ref[...].astype(F32)).astype(o_ref.dtype)

    if ax == 1:
        src = pl.BlockSpec((tr, wc), lambda s, r, p_ref: (p_ref[0] * nr + r, s))
    else:
        src = pl.BlockSpec((tr, wc), lambda s, r, p_ref: (s * 2 * nr + p_ref[0] * nr + r, 0))
    return pl.pallas_call(
        body, name=name,
        grid_spec=pltpu.PrefetchScalarGridSpec(
            num_scalar_prefetch=1, grid=(N_CHIPS, nr), in_specs=[src, src],
            out_specs=pl.BlockSpec((None, tr, wc), lambda s, r, p_ref: (s, r, 0))),
        out_shape=_sds((N_CHIPS, hr, wc), BF16),
        compiler_params=_cparams("parallel", "parallel"),
    )(place, g, ra)


def _chip_add(part, rc, place, name):
    _, hr, wc = rc.shape
    tr = _pick(hr, (256, 352, 128, 64, 32, 16))
    nr = hr // tr

    def body(p_ref, own_ref, r1_ref, r2_ref, r3_ref, o_ref):
        acc = own_ref[...].astype(F32)
        for r_ref in (r1_ref, r2_ref, r3_ref):
            acc = acc + r_ref[...].astype(F32)
        o_ref[...] = acc

    def slot(k):
        return pl.BlockSpec((None, tr, wc), lambda r, p_ref: ((p_ref[1] + k) % N_CHIPS, r, 0))

    return pl.pallas_call(
        body, name=name,
        grid_spec=pltpu.PrefetchScalarGridSpec(
            num_scalar_prefetch=1, grid=(nr,), in_specs=[slot(0), slot(1), slot(2), slot(3)],
            out_specs=pl.BlockSpec((tr, wc), lambda r, p_ref: (p_ref[0] * nr + r, 0))),
        out_shape=_sds((2 * hr, wc), F32),
        compiler_params=_cparams("parallel"),
    )(place, part, rc, rc, rc)


N_DEV = 8


def _all_reduce_small(buf):
    r, cdim = buf.shape

    def body(x_ref, o_ref, land, send_sems, recv_sems):
        x, y, c, _ = _place()
        me = 4 * x + 2 * y + c
        land[me] = x_ref[...]
        sends = []
        for k in range(1, N_DEV):
            kx, ky, kc = (k >> 2) & 1, (k >> 1) & 1, k & 1
            peer = (1 - x if kx else x, 1 - y if ky else y, 1 - c if kc else c)
            cp = _remote(x_ref, land.at[me], send_sems.at[k - 1], recv_sems.at[k - 1], peer)
            cp.start()
            sends.append(cp)
        for k in range(1, N_DEV):
            kx, ky, kc = (k >> 2) & 1, (k >> 1) & 1, k & 1
            peer = (1 - x if kx else x, 1 - y if ky else y, 1 - c if kc else c)
            pidx = 4 * peer[0] + 2 * peer[1] + peer[2]
            _remote(land.at[pidx], land.at[pidx], send_sems.at[k - 1], recv_sems.at[k - 1], peer).wait_recv()
        for cp in sends:
            cp.wait_send()
        acc = land[0]
        for dev in range(1, N_DEV):
            acc = acc + land[dev]
        o_ref[...] = acc

    vm = pl.BlockSpec(memory_space=pltpu.VMEM)
    return pl.pallas_call(
        body, name="all_reduce_small", in_specs=[vm], out_specs=vm, out_shape=_sds((r, cdim), F32),
        scratch_shapes=[pltpu.VMEM((N_DEV, r, cdim), F32), pltpu.SemaphoreType.DMA((N_DEV - 1,)),
                        pltpu.SemaphoreType.DMA((N_DEV - 1,))],
    )(buf)


def _seed_slot(buf, me):
    r, cdim = buf.shape

    def body(me_ref, x_ref, o_ref):
        o_ref[...] = x_ref[...]

    return pl.pallas_call(
        body, name="small_seed",
        grid_spec=pltpu.PrefetchScalarGridSpec(
            num_scalar_prefetch=1, grid=(1,), in_specs=[pl.BlockSpec((r, cdim), lambda i, me_ref: (0, 0))],
            out_specs=pl.BlockSpec((None, r, cdim), lambda i, me_ref: (me_ref[0], 0, 0))),
        out_shape=_sds((N_DEV, r, cdim), buf.dtype),
        compiler_params=_cparams("arbitrary"),
    )(me, buf)


def _plan_all(refs):
    src, land = refs
    x, y, c, _ = _place()
    me = 4 * x + 2 * y + c
    sends, recvs = [], []
    for k in range(1, N_DEV):
        peer = (1 - x if (k >> 2) & 1 else x, 1 - y if (k >> 1) & 1 else y, 1 - c if k & 1 else c)
        sends.append((src, land.at[me], peer))
        recvs.append(land.at[4 * peer[0] + 2 * peer[1] + peer[2]])
    return sends, recvs


def _sum_slots(land):
    n, r, cdim = land.shape

    def body(l_ref, o_ref):
        acc = l_ref[0]
        for dev in range(1, n):
            acc = acc + l_ref[dev]
        o_ref[...] = acc

    vm = pl.BlockSpec(memory_space=pltpu.VMEM)
    return pl.pallas_call(body, name="small_sum", in_specs=[vm], out_specs=vm, out_shape=_sds((r, cdim), F32))(land)


SMALL_ROWS = 16


def kernel(x, mem, g_mix, w_in, conv_w, attn_sinks, w_attn_proj, w_conv_proj, w_mix_out, g_xattn, g_mem, w_xq, w_xkv, w_xo, g_ffn, w_ffn_in, w_ffn_out, g_final, loss_target, m_g_mix, m_w_in, m_conv_w, m_attn_sinks, m_w_attn_proj, m_w_conv_proj, m_w_mix_out, m_g_xattn, m_g_mem, m_w_xq, m_w_xkv, m_w_xo, m_g_ffn, m_w_ffn_in, m_w_ffn_out, m_g_final, v_g_mix, v_w_in, v_conv_w, v_attn_sinks, v_w_attn_proj, v_w_conv_proj, v_w_mix_out, v_g_xattn, v_g_mem, v_w_xq, v_w_xkv, v_w_xo, v_g_ffn, v_w_ffn_in, v_w_ffn_out, v_g_final):
    w = dict(g_mix=g_mix, w_in=w_in[0], conv_w=conv_w[0], attn_sinks=attn_sinks, w_attn_proj=w_attn_proj[0],
             w_conv_proj=w_conv_proj[0], w_mix_out=w_mix_out[0], g_xattn=g_xattn, g_mem=g_mem, w_xq=w_xq[0],
             w_xkv=w_xkv[0], w_xo=w_xo[0], g_ffn=g_ffn, w_ffn_in=w_ffn_in[0], w_ffn_out=w_ffn_out[0],
             g_final=g_final[None])
    m = dict(g_mix=m_g_mix, w_in=m_w_in[0], conv_w=m_conv_w[0], attn_sinks=m_attn_sinks,
             w_attn_proj=m_w_attn_proj[0], w_conv_proj=m_w_conv_proj[0], w_mix_out=m_w_mix_out[0],
             g_xattn=m_g_xattn, g_mem=m_g_mem, w_xq=m_w_xq[0], w_xkv=m_w_xkv[0], w_xo=m_w_xo[0], g_ffn=m_g_ffn,
             w_ffn_in=m_w_ffn_in[0], w_ffn_out=m_w_ffn_out[0], g_final=m_g_final[None])
    v = dict(g_mix=v_g_mix, w_in=v_w_in[0], conv_w=v_conv_w[0], attn_sinks=v_attn_sinks,
             w_attn_proj=v_w_attn_proj[0], w_conv_proj=v_w_conv_proj[0], w_mix_out=v_w_mix_out[0],
             g_xattn=v_g_xattn, g_mem=v_g_mem, w_xq=v_w_xq[0], w_xkv=v_w_xkv[0], w_xo=v_w_xo[0], g_ffn=v_g_ffn,
             w_ffn_in=v_w_ffn_in[0], w_ffn_out=v_w_ffn_out[0], g_final=v_g_final[None])
    names = [nm for nm, _ in BIG]
    axes = [ax for _, ax in BIG]
    d = x.shape[2]
    cw = w["conv_w"].shape[1] * N_CHIPS
    chip = (2 * lax.axis_index("x") + lax.axis_index("y")).astype(jnp.int32)
    place = jnp.stack([lax.axis_index("c").astype(jnp.int32), chip])
    shard_shapes = [w[nm].shape for nm in names]

    def seed(nm, token):
        me1 = chip.reshape(1)
        if token is not None:
            me1 = me1 + token[0, 0].astype(jnp.int32)
        if nm == "conv_w":
            return _cast_to_full(w[nm], 1, me1, "place_conv_w", F32)
        return _cast_to_full(w[nm], dict(BIG)[nm], me1, "cast_" + nm)

    upd = {}

    def on_ready(nm, shard):
        upd[nm] = _adamw(w[nm], shard, m[nm], v[nm], "adamw_" + nm)
        grads[nm] = upd[nm][3]
        return upd[nm][0]

    grads = {}
    wts = _Schedule(seed, dict(zip(names + ["conv_w"], axes + [1])),
                    dict(zip(names + ["conv_w"], shard_shapes + [w["conv_w"].shape])), place, on_ready)
    aw, cw = w["w_attn_proj"].shape[0], w["w_conv_proj"].shape[0]
    kvw = (w["w_in"].shape[1] * N_CHIPS - aw - 3 * cw - 2 * d) // 2
    grad_x, small = _local_step(
        x[0], mem[0], loss_target[0], w["g_mix"] + wts.token[0:1, 0:1], w["attn_sinks"], w["g_xattn"], w["g_mem"],
        w["g_ffn"], w["g_final"], (aw, cw, kvw), wts)

    pw = max(d, cw)

    def row(a):
        return jnp.pad(a, ((0, 0), (0, pw - a.shape[1])))

    gains = ("g_mix", "g_xattn", "g_mem", "g_ffn", "g_final")
    packed = jnp.concatenate(
        [row(small[nm]) for nm in gains] + [row(small["conv_w"]),
         row(jnp.concatenate([small["attn_sinks"], small["loss"]], axis=1)),
         jnp.zeros((SMALL_ROWS - 9, pw), F32)], axis=0)
    slots = _seed_slot(packed, (2 * place[1] + place[0]).reshape(1))
    ss, rs, arrs, tok = _exchange_start("small_start", [packed, slots], N_DEV - 1, _plan_all)
    wts.mark("end", tok)
    wts.mark("end2", tok)
    total = _sum_slots(_exchange_wait("small_wait", arrs, ss, rs, _plan_all, upd["w_in"][0])[1])
    nsink = attn_sinks.shape[1]
    grads.update({nm: total[k:k + 1, :d] for k, nm in enumerate(gains)})
    grads.update(conv_w=lax.dynamic_slice(total, (5, chip * (cw // N_CHIPS)), (3, cw // N_CHIPS)),
                 attn_sinks=total[8:9, :nsink])
    loss = total[8, nsink]
    for nm in gains + ("conv_w", "attn_sinks"):
        upd[nm] = _adamw(w[nm], grads[nm], m[nm], v[nm], "adamw_" + nm)

    order = ["g_mix", "w_in", "conv_w", "attn_sinks", "w_attn_proj", "w_conv_proj", "w_mix_out", "g_xattn", "g_mem",
             "w_xq", "w_xkv", "w_xo", "g_ffn", "w_ffn_in", "w_ffn_out", "g_final"]

    stacked = set(names) | {"conv_w"}

    def shaped(nm, a):
        if nm == "g_final":
            return a[0]
        return a[None] if nm in stacked else a

    outs = [loss, grad_x[None]]
    outs += [shaped(nm, grads[nm]) for nm in order]
    for k in range(3):
        outs += [shaped(nm, upd[nm][k]) for nm in order]
    return tuple(outs)
```
